```python
import math
import jax
import jax.numpy as jnp
from jax import lax
import numpy as np

D_MODEL = 2048
BATCH = 8
SEQ = 2048
DEPTH = 1

MEM_LEN = 256
RMS_EPS = 1e-6
DN_HEAD_DIM = 128
DN_WIDTH = D_MODEL // 2
DN_HEADS = DN_WIDTH // DN_HEAD_DIM
DN_CONV = 4
DN_CHUNK = 64
RW_HEAD_DIM = 64
RW_WIDTH = D_MODEL - DN_WIDTH
RW_HEADS = RW_WIDTH // RW_HEAD_DIM
RW_DECAY_LORA = 64
RW_AAA_LORA = 64
RW_GATE_LORA = 128
RW_GN_EPS = 64e-5
DN_COLS = 4 * DN_WIDTH + 2 * DN_HEADS
RW_COLS = 3 * RW_WIDTH + RW_DECAY_LORA + RW_AAA_LORA + RW_GATE_LORA
IN_COLS = DN_COLS + RW_COLS
XA_HEADS = 4
XA_HEAD_DIM = 128
XA_WIDTH = XA_HEADS * XA_HEAD_DIM
FFN_HIDDEN = 4 * D_MODEL

kernel_name = 'hybrid_gdn_rwkv7_memxattn_layer'


def rmsnorm(x, w):
    xf = x.astype(jnp.float32)
    y = xf * lax.rsqrt(jnp.mean(xf * xf, axis=-1, keepdims=True) + RMS_EPS)
    return (y * w.astype(jnp.float32)).astype(x.dtype)


def l2norm(x):
    return x * lax.rsqrt(jnp.sum(x * x, axis=-1, keepdims=True) + 1e-6)


def causal_depthwise_conv(x, w):
    k = w.shape[0]
    return lax.conv_general_dilated(
        x, w[:, None, :].astype(x.dtype), window_strides=(1,), padding=[(k - 1, 0)],
        dimension_numbers=('NWC', 'WIO', 'NWC'), feature_group_count=x.shape[-1])


def token_shift(p):
    return jnp.pad(p, ((0, 0), (1, 0), (0, 0)))[:, :-1]


def chunked_gated_delta_rule(q, k, v, g, beta):
    b, s, h, dk = q.shape
    dv = v.shape[-1]
    c = DN_CHUNK
    n = s // c

    def chunks(t):
        return t.reshape(b, n, c, h, -1).transpose(0, 3, 1, 2, 4)

    q = chunks(q) * (dk ** -0.5)
    k = chunks(k)
    v = chunks(v)
    g = g.reshape(b, n, c, h).transpose(0, 3, 1, 2)
    beta = beta.reshape(b, n, c, h).transpose(0, 3, 1, 2)
    gc = jnp.cumsum(g, axis=-1)
    idx = jnp.arange(c)
    causal = idx[:, None] >= idx[None, :]
    strict = idx[:, None] > idx[None, :]
    decay = jnp.exp(jnp.where(causal, gc[..., :, None] - gc[..., None, :], -jnp.inf))
    kb = k * beta[..., None]
    a_strict = jnp.where(strict, jnp.einsum('bhncd,bhnmd->bhncm', kb, k) * decay, 0.0)
    lower = a_strict + jnp.eye(c, dtype=q.dtype)
    rhs = jnp.concatenate([v * beta[..., None], kb * jnp.exp(gc)[..., None]], axis=-1)
    sol = lax.linalg.triangular_solve(lower, rhs, left_side=True, lower=True)
    u, w = sol[..., :dv], sol[..., dv:]
    attn = jnp.einsum('bhncd,bhnmd->bhncm', q, k) * decay
    qg = q * jnp.exp(gc)[..., None]
    g_last = gc[..., -1]
    kd = k * jnp.exp(g_last[..., None] - gc)[..., None]
    xs = tuple(jnp.moveaxis(t, 2, 0) for t in (u, w, qg, attn, kd, g_last))

    def step(state, xs_n):
        u_n, w_n, qg_n, attn_n, kd_n, gl_n = xs_n
        v_new = u_n - jnp.einsum('bhcd,bhde->bhce', w_n, state)
        o_n = jnp.einsum('bhcd,bhde->bhce', qg_n, state) + jnp.einsum('bhcm,bhme->bhce', attn_n, v_new)
        state = state * jnp.exp(gl_n)[..., None, None] + jnp.einsum('bhcd,bhce->bhde', kd_n, v_new)
        return state, o_n

    s0 = jnp.zeros((b, h, dk, dv), q.dtype)
    _, o = lax.scan(step, s0, xs)
    return o.transpose(1, 0, 3, 2, 4).reshape(b, s, h, dv)


def wkv7_scan(r, w, k, v, kk, a):
    b, s, h, nd = r.shape
    xs = tuple(jnp.swapaxes(t, 0, 1) for t in (r, w, k, v, kk, a))

    def step(state, xs_t):
        r_t, w_t, k_t, v_t, kk_t, a_t = xs_t
        sa = jnp.einsum('bhvk,bhk->bhv', state, -kk_t)
        state = (state * w_t[:, :, None, :] + sa[..., None] * (kk_t * a_t)[:, :, None, :]
                 + v_t[..., None] * k_t[:, :, None, :])
        return state, jnp.einsum('bhvk,bhk->bhv', state, r_t)

    s0 = jnp.zeros((b, h, nd, nd), r.dtype)
    _, y = lax.scan(step, s0, xs)
    return jnp.swapaxes(y, 0, 1)


def deltanet_group(p, conv_w, a_log, dt_bias, norm_w):
    p = p.astype(jnp.float32)
    b, s, _ = p.shape
    qkv, z, ga, gb = jnp.split(p, [3 * DN_WIDTH, 4 * DN_WIDTH, 4 * DN_WIDTH + DN_HEADS], axis=-1)
    qkv = jax.nn.silu(causal_depthwise_conv(qkv, conv_w))
    q, k, v = (t.reshape(b, s, DN_HEADS, DN_HEAD_DIM) for t in jnp.split(qkv, 3, axis=-1))
    q = l2norm(q)
    k = l2norm(k)
    beta = jax.nn.sigmoid(gb)
    g = -jnp.exp(a_log) * jax.nn.softplus(ga + dt_bias)
    o = chunked_gated_delta_rule(q, k, v, g, beta)
    o = rmsnorm(o, norm_w) * jax.nn.silu(z.reshape(b, s, DN_HEADS, DN_HEAD_DIM))
    return o.reshape(b, s, DN_WIDTH)


def rwkv7_group(p, mu, w0, w2, a0, a2, g2, k_k, k_a, r_k, ln_w, ln_b):
    p = p.astype(jnp.float32)
    b, s, _ = p.shape
    p = p + (token_shift(p) - p) * mu
    pr, pk, pv, pw, pa, pg = jnp.split(
        p, [RW_WIDTH, 2 * RW_WIDTH, 3 * RW_WIDTH, 3 * RW_WIDTH + RW_DECAY_LORA,
            3 * RW_WIDTH + RW_DECAY_LORA + RW_AAA_LORA], axis=-1)

    def heads(t):
        return t.reshape(b, s, RW_HEADS, RW_HEAD_DIM)

    log_w = -jax.nn.softplus(-(w0 + jnp.tanh(pw) @ w2)) - 0.5
    decay = jnp.exp(-jnp.exp(log_w))
    a = jax.nn.sigmoid(a0 + pa @ a2)
    gate = jax.nn.sigmoid(pg) @ g2
    kk = heads(pk * k_k)
    kk = kk / jnp.maximum(jnp.sqrt(jnp.sum(kk * kk, axis=-1, keepdims=True)), 1e-12)
    k = pk * (1.0 + (a - 1.0) * k_a)
    r_h, k_h, v_h = heads(pr), heads(k), heads(pv)
    y = wkv7_scan(r_h, heads(decay), k_h, v_h, kk, heads(a))
    mean = jnp.mean(y, axis=-1, keepdims=True)
    var = jnp.mean(jnp.square(y - mean), axis=-1, keepdims=True)
    y = ((y - mean) * lax.rsqrt(var + RW_GN_EPS)).reshape(b, s, RW_WIDTH) * ln_w + ln_b
    bonus = jnp.sum(r_h * k_h * r_k, axis=-1, keepdims=True) * v_h
    return (y + bonus.reshape(b, s, RW_WIDTH)) * gate


def memory_cross_attention(hn, mn, wq, wk, wv, wo):
    b, s, _ = hn.shape
    m = mn.shape[1]
    q = (hn @ wq).reshape(b, s, XA_HEADS, XA_HEAD_DIM)
    k = (mn @ wk).reshape(b, m, XA_HEADS, XA_HEAD_DIM)
    v = (mn @ wv).reshape(b, m, XA_HEADS, XA_HEAD_DIM)
    scores = jnp.einsum('bshd,bmhd->bhsm', q, k).astype(jnp.float32) * (XA_HEAD_DIM ** -0.5)
    probs = jax.nn.softmax(scores, axis=-1).astype(v.dtype)
    o = jnp.einsum('bhsm,bmhd->bshd', probs, v).reshape(b, s, XA_WIDTH)
    return o @ wo


def squared_relu_mlp(u, w1, w2):
    return jnp.square(jax.nn.relu(u @ w1)) @ w2


def _fwd_setup_inputs(seed: int = 0) -> dict:
    key = jax.random.key(seed)
    ks = jax.random.split(key, 32)
    L = DEPTH
    D = D_MODEL

    def normal(i, shape, scale):
        return jax.random.normal(ks[i], shape, jnp.float32) * scale

    def uniform(i, shape, lo, hi):
        return jax.random.uniform(ks[i], shape, jnp.float32, lo, hi)

    dt = jnp.exp(uniform(6, (L, DN_HEADS), math.log(1e-3), math.log(1e-1)))
    return {
        'x': normal(0, (BATCH, SEQ, D), 1.0),
        'mem': normal(1, (BATCH, MEM_LEN, D), 1.0),
        'mix_norm_w': 1.0 + normal(2, (L, D), 0.02),
        'w_in': normal(3, (L, D, IN_COLS), D ** -0.5),
        'dn_conv_w': normal(4, (L, DN_CONV, 3 * DN_WIDTH), DN_CONV ** -0.5),
        'dn_a_log': jnp.log(uniform(5, (L, DN_HEADS), 1.0, 16.0)),
        'dn_dt_bias': dt + jnp.log(-jnp.expm1(-dt)),
        'dn_norm_w': 1.0 + normal(7, (L, DN_HEAD_DIM), 0.02),
        'rw_mu': uniform(8, (L, RW_COLS), 0.0, 1.0),
        'rw_w0': uniform(9, (L, RW_WIDTH), -6.0, 1.0),
        'rw_w2': normal(10, (L, RW_DECAY_LORA, RW_WIDTH), 0.1 * RW_DECAY_LORA ** -0.5),
        'rw_a0': normal(11, (L, RW_WIDTH), 0.1),
        'rw_a2': normal(12, (L, RW_AAA_LORA, RW_WIDTH), 0.1 * RW_AAA_LORA ** -0.5),
        'rw_g2': normal(13, (L, RW_GATE_LORA, RW_WIDTH), RW_GATE_LORA ** -0.5),
        'rw_k_k': 0.85 + normal(14, (L, RW_WIDTH), 0.02),
        'rw_k_a': 1.0 + normal(15, (L, RW_WIDTH), 0.02),
        'rw_r_k': normal(16, (L, RW_HEADS, RW_HEAD_DIM), 0.1),
        'rw_ln_w': 1.0 + normal(17, (L, RW_WIDTH), 0.02),
        'rw_ln_b': normal(18, (L, RW_WIDTH), 0.02),
        'w_out': normal(19, (L, D, D), D ** -0.5),
        'xa_norm_w': 1.0 + normal(20, (L, D), 0.02),
        'mem_norm_w': 1.0 + normal(21, (L, D), 0.02),
        'xa_wq': normal(22, (L, D, XA_WIDTH), D ** -0.5),
        'xa_wk': normal(23, (L, D, XA_WIDTH), D ** -0.5),
        'xa_wv': normal(24, (L, D, XA_WIDTH), D ** -0.5),
        'xa_wo': normal(25, (L, XA_WIDTH, D), XA_WIDTH ** -0.5),
        'ffn_norm_w': 1.0 + normal(26, (L, D), 0.02),
        'ffn_w1': normal(27, (L, D, FFN_HIDDEN), D ** -0.5),
        'ffn_w2': normal(28, (L, FFN_HIDDEN, D), FFN_HIDDEN ** -0.5),
        'final_norm_w': 1.0 + normal(29, (D,), 0.02),
    }


def _fwd_reference(x, mem, mix_norm_w, w_in, dn_conv_w, dn_a_log, dn_dt_bias, dn_norm_w, rw_mu, rw_w0,
              rw_w2, rw_a0, rw_a2, rw_g2, rw_k_k, rw_k_a, rw_r_k, rw_ln_w, rw_ln_b, w_out,
              xa_norm_w, mem_norm_w, xa_wq, xa_wk, xa_wv, xa_wo, ffn_norm_w, ffn_w1, ffn_w2,
              final_norm_w):
    h = x
    for l in range(DEPTH):
        u = rmsnorm(h, mix_norm_w[l])
        p = u @ w_in[l]
        o_dn = deltanet_group(p[..., :DN_COLS], dn_conv_w[l], dn_a_log[l], dn_dt_bias[l], dn_norm_w[l])
        o_rw = rwkv7_group(p[..., DN_COLS:], rw_mu[l], rw_w0[l], rw_w2[l], rw_a0[l], rw_a2[l], rw_g2[l],
                           rw_k_k[l], rw_k_a[l], rw_r_k[l], rw_ln_w[l], rw_ln_b[l])
        h = h + jnp.concatenate([o_dn, o_rw], axis=-1).astype(h.dtype) @ w_out[l]
        h = h + memory_cross_attention(rmsnorm(h, xa_norm_w[l]), rmsnorm(mem, mem_norm_w[l]),
                                       xa_wq[l], xa_wk[l], xa_wv[l], xa_wo[l])
        h = h + squared_relu_mlp(rmsnorm(h, ffn_norm_w[l]), ffn_w1[l], ffn_w2[l])
    return rmsnorm(h, final_norm_w)


import jax as _jax
import jax.numpy as _jnp

TWIN_FORMAT = 'train_step'
FWD_PARAMS = ['x', 'mem', 'mix_norm_w', 'w_in', 'dn_conv_w', 'dn_a_log', 'dn_dt_bias', 'dn_norm_w', 'rw_mu', 'rw_w0', 'rw_w2', 'rw_a0', 'rw_a2', 'rw_g2', 'rw_k_k', 'rw_k_a', 'rw_r_k', 'rw_ln_w', 'rw_ln_b', 'w_out', 'xa_norm_w', 'mem_norm_w', 'xa_wq', 'xa_wk', 'xa_wv', 'xa_wo', 'ffn_norm_w', 'ffn_w1', 'ffn_w2', 'final_norm_w']
TWIN_WEIGHTS = ['mix_norm_w', 'w_in', 'dn_conv_w', 'dn_a_log', 'dn_dt_bias', 'dn_norm_w', 'rw_mu', 'rw_w0', 'rw_w2', 'rw_a0', 'rw_a2', 'rw_g2', 'rw_k_k', 'rw_k_a', 'rw_r_k', 'rw_ln_w', 'rw_ln_b', 'w_out', 'xa_norm_w', 'mem_norm_w', 'xa_wq', 'xa_wk', 'xa_wv', 'xa_wo', 'ffn_norm_w', 'ffn_w1', 'ffn_w2', 'final_norm_w']
TWIN_DIFF_INPUT = 'x'
TWIN_INPUTS = ['x', 'mem', 'mix_norm_w', 'w_in', 'dn_conv_w', 'dn_a_log', 'dn_dt_bias', 'dn_norm_w', 'rw_mu', 'rw_w0', 'rw_w2', 'rw_a0', 'rw_a2', 'rw_g2', 'rw_k_k', 'rw_k_a', 'rw_r_k', 'rw_ln_w', 'rw_ln_b', 'w_out', 'xa_norm_w', 'mem_norm_w', 'xa_wq', 'xa_wk', 'xa_wv', 'xa_wo', 'ffn_norm_w', 'ffn_w1', 'ffn_w2', 'final_norm_w', 'loss_target', 'm_mix_norm_w', 'm_w_in', 'm_dn_conv_w', 'm_dn_a_log', 'm_dn_dt_bias', 'm_dn_norm_w', 'm_rw_mu', 'm_rw_w0', 'm_rw_w2', 'm_rw_a0', 'm_rw_a2', 'm_rw_g2', 'm_rw_k_k', 'm_rw_k_a', 'm_rw_r_k', 'm_rw_ln_w', 'm_rw_ln_b', 'm_w_out', 'm_xa_norm_w', 'm_mem_norm_w', 'm_xa_wq', 'm_xa_wk', 'm_xa_wv', 'm_xa_wo', 'm_ffn_norm_w', 'm_ffn_w1', 'm_ffn_w2', 'm_final_norm_w', 'v_mix_norm_w', 'v_w_in', 'v_dn_conv_w', 'v_dn_a_log', 'v_dn_dt_bias', 'v_dn_norm_w', 'v_rw_mu', 'v_rw_w0', 'v_rw_w2', 'v_rw_a0', 'v_rw_a2', 'v_rw_g2', 'v_rw_k_k', 'v_rw_k_a', 'v_rw_r_k', 'v_rw_ln_w', 'v_rw_ln_b', 'v_w_out', 'v_xa_norm_w', 'v_mem_norm_w', 'v_xa_wq', 'v_xa_wk', 'v_xa_wv', 'v_xa_wo', 'v_ffn_norm_w', 'v_ffn_w1', 'v_ffn_w2', 'v_final_norm_w']
TWIN_OUTPUTS = ['loss', 'grad_x', 'grad_mix_norm_w', 'grad_w_in', 'grad_dn_conv_w', 'grad_dn_a_log', 'grad_dn_dt_bias', 'grad_dn_norm_w', 'grad_rw_mu', 'grad_rw_w0', 'grad_rw_w2', 'grad_rw_a0', 'grad_rw_a2', 'grad_rw_g2', 'grad_rw_k_k', 'grad_rw_k_a', 'grad_rw_r_k', 'grad_rw_ln_w', 'grad_rw_ln_b', 'grad_w_out', 'grad_xa_norm_w', 'grad_mem_norm_w', 'grad_xa_wq', 'grad_xa_wk', 'grad_xa_wv', 'grad_xa_wo', 'grad_ffn_norm_w', 'grad_ffn_w1', 'grad_ffn_w2', 'grad_final_norm_w', 'delta_mix_norm_w', 'delta_w_in', 'delta_dn_conv_w', 'delta_dn_a_log', 'delta_dn_dt_bias', 'delta_dn_norm_w', 'delta_rw_mu', 'delta_rw_w0', 'delta_rw_w2', 'delta_rw_a0', 'delta_rw_a2', 'delta_rw_g2', 'delta_rw_k_k', 'delta_rw_k_a', 'delta_rw_r_k', 'delta_rw_ln_w', 'delta_rw_ln_b', 'delta_w_out', 'delta_xa_norm_w', 'delta_mem_norm_w', 'delta_xa_wq', 'delta_xa_wk', 'delta_xa_wv', 'delta_xa_wo', 'delta_ffn_norm_w', 'delta_ffn_w1', 'delta_ffn_w2', 'delta_final_norm_w', 'new_m_mix_norm_w', 'new_m_w_in', 'new_m_dn_conv_w', 'new_m_dn_a_log', 'new_m_dn_dt_bias', 'new_m_dn_norm_w', 'new_m_rw_mu', 'new_m_rw_w0', 'new_m_rw_w2', 'new_m_rw_a0', 'new_m_rw_a2', 'new_m_rw_g2', 'new_m_rw_k_k', 'new_m_rw_k_a', 'new_m_rw_r_k', 'new_m_rw_ln_w', 'new_m_rw_ln_b', 'new_m_w_out', 'new_m_xa_norm_w', 'new_m_mem_norm_w', 'new_m_xa_wq', 'new_m_xa_wk', 'new_m_xa_wv', 'new_m_xa_wo', 'new_m_ffn_norm_w', 'new_m_ffn_w1', 'new_m_ffn_w2', 'new_m_final_norm_w', 'new_v_mix_norm_w', 'new_v_w_in', 'new_v_dn_conv_w', 'new_v_dn_a_log', 'new_v_dn_dt_bias', 'new_v_dn_norm_w', 'new_v_rw_mu', 'new_v_rw_w0', 'new_v_rw_w2', 'new_v_rw_a0', 'new_v_rw_a2', 'new_v_rw_g2', 'new_v_rw_k_k', 'new_v_rw_k_a', 'new_v_rw_r_k', 'new_v_rw_ln_w', 'new_v_rw_ln_b', 'new_v_w_out', 'new_v_xa_norm_w', 'new_v_mem_norm_w', 'new_v_xa_wq', 'new_v_xa_wk', 'new_v_xa_wv', 'new_v_xa_wo', 'new_v_ffn_norm_w', 'new_v_ffn_w1', 'new_v_ffn_w2', 'new_v_final_norm_w']
TWIN_LEAF_KINDS = {'loss': 'loss', 'grad_x': 'grad_x', 'grad_mix_norm_w': 'grad_w', 'grad_w_in': 'grad_w', 'grad_dn_conv_w': 'grad_w', 'grad_dn_a_log': 'grad_w', 'grad_dn_dt_bias': 'grad_w', 'grad_dn_norm_w': 'grad_w', 'grad_rw_mu': 'grad_w', 'grad_rw_w0': 'grad_w', 'grad_rw_w2': 'grad_w', 'grad_rw_a0': 'grad_w', 'grad_rw_a2': 'grad_w', 'grad_rw_g2': 'grad_w', 'grad_rw_k_k': 'grad_w', 'grad_rw_k_a': 'grad_w', 'grad_rw_r_k': 'grad_w', 'grad_rw_ln_w': 'grad_w', 'grad_rw_ln_b': 'grad_w', 'grad_w_out': 'grad_w', 'grad_xa_norm_w': 'grad_w', 'grad_mem_norm_w': 'grad_w', 'grad_xa_wq': 'grad_w', 'grad_xa_wk': 'grad_w', 'grad_xa_wv': 'grad_w', 'grad_xa_wo': 'grad_w', 'grad_ffn_norm_w': 'grad_w', 'grad_ffn_w1': 'grad_w', 'grad_ffn_w2': 'grad_w', 'grad_final_norm_w': 'grad_w', 'delta_mix_norm_w': 'delta_w', 'delta_w_in': 'delta_w', 'delta_dn_conv_w': 'delta_w', 'delta_dn_a_log': 'delta_w', 'delta_dn_dt_bias': 'delta_w', 'delta_dn_norm_w': 'delta_w', 'delta_rw_mu': 'delta_w', 'delta_rw_w0': 'delta_w', 'delta_rw_w2': 'delta_w', 'delta_rw_a0': 'delta_w', 'delta_rw_a2': 'delta_w', 'delta_rw_g2': 'delta_w', 'delta_rw_k_k': 'delta_w', 'delta_rw_k_a': 'delta_w', 'delta_rw_r_k': 'delta_w', 'delta_rw_ln_w': 'delta_w', 'delta_rw_ln_b': 'delta_w', 'delta_w_out': 'delta_w', 'delta_xa_norm_w': 'delta_w', 'delta_mem_norm_w': 'delta_w', 'delta_xa_wq': 'delta_w', 'delta_xa_wk': 'delta_w', 'delta_xa_wv': 'delta_w', 'delta_xa_wo': 'delta_w', 'delta_ffn_norm_w': 'delta_w', 'delta_ffn_w1': 'delta_w', 'delta_ffn_w2': 'delta_w', 'delta_final_norm_w': 'delta_w', 'new_m_mix_norm_w': 'new_m', 'new_m_w_in': 'new_m', 'new_m_dn_conv_w': 'new_m', 'new_m_dn_a_log': 'new_m', 'new_m_dn_dt_bias': 'new_m', 'new_m_dn_norm_w': 'new_m', 'new_m_rw_mu': 'new_m', 'new_m_rw_w0': 'new_m', 'new_m_rw_w2': 'new_m', 'new_m_rw_a0': 'new_m', 'new_m_rw_a2': 'new_m', 'new_m_rw_g2': 'new_m', 'new_m_rw_k_k': 'new_m', 'new_m_rw_k_a': 'new_m', 'new_m_rw_r_k': 'new_m', 'new_m_rw_ln_w': 'new_m', 'new_m_rw_ln_b': 'new_m', 'new_m_w_out': 'new_m', 'new_m_xa_norm_w': 'new_m', 'new_m_mem_norm_w': 'new_m', 'new_m_xa_wq': 'new_m', 'new_m_xa_wk': 'new_m', 'new_m_xa_wv': 'new_m', 'new_m_xa_wo': 'new_m', 'new_m_ffn_norm_w': 'new_m', 'new_m_ffn_w1': 'new_m', 'new_m_ffn_w2': 'new_m', 'new_m_final_norm_w': 'new_m', 'new_v_mix_norm_w': 'new_v', 'new_v_w_in': 'new_v', 'new_v_dn_conv_w': 'new_v', 'new_v_dn_a_log': 'new_v', 'new_v_dn_dt_bias': 'new_v', 'new_v_dn_norm_w': 'new_v', 'new_v_rw_mu': 'new_v', 'new_v_rw_w0': 'new_v', 'new_v_rw_w2': 'new_v', 'new_v_rw_a0': 'new_v', 'new_v_rw_a2': 'new_v', 'new_v_rw_g2': 'new_v', 'new_v_rw_k_k': 'new_v', 'new_v_rw_k_a': 'new_v', 'new_v_rw_r_k': 'new_v', 'new_v_rw_ln_w': 'new_v', 'new_v_rw_ln_b': 'new_v', 'new_v_w_out': 'new_v', 'new_v_xa_norm_w': 'new_v', 'new_v_mem_norm_w': 'new_v', 'new_v_xa_wq': 'new_v', 'new_v_xa_wk': 'new_v', 'new_v_xa_wv': 'new_v', 'new_v_xa_wo': 'new_v', 'new_v_ffn_norm_w': 'new_v', 'new_v_ffn_w1': 'new_v', 'new_v_ffn_w2': 'new_v', 'new_v_final_norm_w': 'new_v'}


def _forward(args):
    return _fwd_reference(*[args[k] for k in FWD_PARAMS])


def _output_shape():
    out = _jax.eval_shape(lambda: _forward(_fwd_setup_inputs(0)))
    return out.shape, out.dtype

N_MICROBATCH = 1
ADAM_LR = 0.001
ADAM_B1 = 0.9
ADAM_B2 = 0.999
ADAM_EPS = 1e-08
ADAM_WD = 0.01
ADAM_STEP = 10
PER_EXAMPLE_BATCH_AXIS = {'x': 0, 'mem': 0, 'loss_target': 0}
SHARED_INPUTS = []
_WEIGHT_DTYPES = {'mix_norm_w': _jnp.float32, 'w_in': _jnp.float32, 'dn_conv_w': _jnp.float32, 'dn_a_log': _jnp.float32, 'dn_dt_bias': _jnp.float32, 'dn_norm_w': _jnp.float32, 'rw_mu': _jnp.float32, 'rw_w0': _jnp.float32, 'rw_w2': _jnp.float32, 'rw_a0': _jnp.float32, 'rw_a2': _jnp.float32, 'rw_g2': _jnp.float32, 'rw_k_k': _jnp.float32, 'rw_k_a': _jnp.float32, 'rw_r_k': _jnp.float32, 'rw_ln_w': _jnp.float32, 'rw_ln_b': _jnp.float32, 'w_out': _jnp.float32, 'xa_norm_w': _jnp.float32, 'mem_norm_w': _jnp.float32, 'xa_wq': _jnp.float32, 'xa_wk': _jnp.float32, 'xa_wv': _jnp.float32, 'xa_wo': _jnp.float32, 'ffn_norm_w': _jnp.float32, 'ffn_w1': _jnp.float32, 'ffn_w2': _jnp.float32, 'final_norm_w': _jnp.float32}
MOMENT_SCALE = {'mix_norm_w': 6.045856e-02, 'w_in': 3.136742e-02, 'dn_conv_w': 2.646547e-02, 'dn_a_log': 1.590767e-01, 'dn_dt_bias': 1.535547e-01, 'dn_norm_w': 1.164484e-01, 'rw_mu': 5.657414e-02, 'rw_w0': 1.235529e-02, 'rw_w2': 1.611125e-03, 'rw_a0': 1.490082e-02, 'rw_a2': 1.394198e-02, 'rw_g2': 3.299132e-02, 'rw_k_k': 3.492401e-02, 'rw_k_a': 3.520979e-02, 'rw_r_k': 6.795281e-02, 'rw_ln_w': 3.199886e-02, 'rw_ln_b': 3.904637e-02, 'w_out': 3.326748e-02, 'xa_norm_w': 6.418485e-03, 'mem_norm_w': 9.180187e-03, 'xa_wq': 1.218868e-02, 'xa_wk': 1.228459e-02, 'xa_wv': 1.275149e-02, 'xa_wo': 6.354833e-03, 'ffn_norm_w': 5.345856e-02, 'ffn_w1': 2.640445e-02, 'ffn_w2': 4.964275e-02, 'final_norm_w': 8.057782e+00}


def _to_microbatches(a, axis):
    t = _jnp.moveaxis(a, axis, 0)
    t = t.reshape((N_MICROBATCH, t.shape[0] // N_MICROBATCH) + t.shape[1:])
    return _jnp.moveaxis(t, 1, axis + 1)


def setup_inputs(seed: int = 0) -> dict:
    inp = _fwd_setup_inputs(seed)
    key = _jax.random.fold_in(_jax.random.key(seed), 7919)
    shape, _ = _output_shape()
    out = dict(inp)
    out["loss_target"] = _jax.random.normal(_jax.random.fold_in(key, 0), shape, _jnp.float32)
    for i, name in enumerate(TWIN_WEIGHTS):
        w = inp[name].astype(_jnp.float32)
        if MOMENT_SCALE is None:
            s = _jnp.sqrt(_jnp.mean(_jnp.square(w)) + 1e-30)
        else:
            s = MOMENT_SCALE[name]
        km, kv = _jax.random.split(_jax.random.fold_in(key, i + 1))
        out[name] = w
        out["m_" + name] = s * _jax.random.normal(km, w.shape, _jnp.float32)
        out["v_" + name] = (s * s) * _jax.random.uniform(kv, w.shape, _jnp.float32, 0.5, 1.5)
    if N_MICROBATCH > 1:
        for name, axis in PER_EXAMPLE_BATCH_AXIS.items():
            out[name] = _to_microbatches(out[name], axis)
    return {'x': out['x'], 'mem': out['mem'], 'mix_norm_w': out['mix_norm_w'], 'w_in': out['w_in'], 'dn_conv_w': out['dn_conv_w'], 'dn_a_log': out['dn_a_log'], 'dn_dt_bias': out['dn_dt_bias'], 'dn_norm_w': out['dn_norm_w'], 'rw_mu': out['rw_mu'], 'rw_w0': out['rw_w0'], 'rw_w2': out['rw_w2'], 'rw_a0': out['rw_a0'], 'rw_a2': out['rw_a2'], 'rw_g2': out['rw_g2'], 'rw_k_k': out['rw_k_k'], 'rw_k_a': out['rw_k_a'], 'rw_r_k': out['rw_r_k'], 'rw_ln_w': out['rw_ln_w'], 'rw_ln_b': out['rw_ln_b'], 'w_out': out['w_out'], 'xa_norm_w': out['xa_norm_w'], 'mem_norm_w': out['mem_norm_w'], 'xa_wq': out['xa_wq'], 'xa_wk': out['xa_wk'], 'xa_wv': out['xa_wv'], 'xa_wo': out['xa_wo'], 'ffn_norm_w': out['ffn_norm_w'], 'ffn_w1': out['ffn_w1'], 'ffn_w2': out['ffn_w2'], 'final_norm_w': out['final_norm_w'], 'loss_target': out['loss_target'], 'm_mix_norm_w': out['m_mix_norm_w'], 'm_w_in': out['m_w_in'], 'm_dn_conv_w': out['m_dn_conv_w'], 'm_dn_a_log': out['m_dn_a_log'], 'm_dn_dt_bias': out['m_dn_dt_bias'], 'm_dn_norm_w': out['m_dn_norm_w'], 'm_rw_mu': out['m_rw_mu'], 'm_rw_w0': out['m_rw_w0'], 'm_rw_w2': out['m_rw_w2'], 'm_rw_a0': out['m_rw_a0'], 'm_rw_a2': out['m_rw_a2'], 'm_rw_g2': out['m_rw_g2'], 'm_rw_k_k': out['m_rw_k_k'], 'm_rw_k_a': out['m_rw_k_a'], 'm_rw_r_k': out['m_rw_r_k'], 'm_rw_ln_w': out['m_rw_ln_w'], 'm_rw_ln_b': out['m_rw_ln_b'], 'm_w_out': out['m_w_out'], 'm_xa_norm_w': out['m_xa_norm_w'], 'm_mem_norm_w': out['m_mem_norm_w'], 'm_xa_wq': out['m_xa_wq'], 'm_xa_wk': out['m_xa_wk'], 'm_xa_wv': out['m_xa_wv'], 'm_xa_wo': out['m_xa_wo'], 'm_ffn_norm_w': out['m_ffn_norm_w'], 'm_ffn_w1': out['m_ffn_w1'], 'm_ffn_w2': out['m_ffn_w2'], 'm_final_norm_w': out['m_final_norm_w'], 'v_mix_norm_w': out['v_mix_norm_w'], 'v_w_in': out['v_w_in'], 'v_dn_conv_w': out['v_dn_conv_w'], 'v_dn_a_log': out['v_dn_a_log'], 'v_dn_dt_bias': out['v_dn_dt_bias'], 'v_dn_norm_w': out['v_dn_norm_w'], 'v_rw_mu': out['v_rw_mu'], 'v_rw_w0': out['v_rw_w0'], 'v_rw_w2': out['v_rw_w2'], 'v_rw_a0': out['v_rw_a0'], 'v_rw_a2': out['v_rw_a2'], 'v_rw_g2': out['v_rw_g2'], 'v_rw_k_k': out['v_rw_k_k'], 'v_rw_k_a': out['v_rw_k_a'], 'v_rw_r_k': out['v_rw_r_k'], 'v_rw_ln_w': out['v_rw_ln_w'], 'v_rw_ln_b': out['v_rw_ln_b'], 'v_w_out': out['v_w_out'], 'v_xa_norm_w': out['v_xa_norm_w'], 'v_mem_norm_w': out['v_mem_norm_w'], 'v_xa_wq': out['v_xa_wq'], 'v_xa_wk': out['v_xa_wk'], 'v_xa_wv': out['v_xa_wv'], 'v_xa_wo': out['v_xa_wo'], 'v_ffn_norm_w': out['v_ffn_norm_w'], 'v_ffn_w1': out['v_ffn_w1'], 'v_ffn_w2': out['v_ffn_w2'], 'v_final_norm_w': out['v_final_norm_w']}


def _loss(weights, diff, rest, loss_target):
    with _jax.named_scope("forward"):
        args = {**rest, TWIN_DIFF_INPUT: diff, **{k: w.astype(_WEIGHT_DTYPES[k]) for k, w in weights.items()}}
        y = _forward(args)
    with _jax.named_scope("loss_head"):
        err = _jnp.square(y.astype(_jnp.float32) - loss_target)
        return 0.5 * _jnp.sum(_jnp.mean(err, axis=-1)) if err.ndim else 0.5 * err


def _adamw(w, g, m, v):
    m = ADAM_B1 * m + (1.0 - ADAM_B1) * g
    v = ADAM_B2 * v + (1.0 - ADAM_B2) * _jnp.square(g)
    m_hat = m / (1.0 - ADAM_B1 ** ADAM_STEP)
    v_hat = v / (1.0 - ADAM_B2 ** ADAM_STEP)
    delta = -ADAM_LR * (m_hat / (_jnp.sqrt(v_hat) + ADAM_EPS) + ADAM_WD * w)
    return delta, m, v


def reference(x, mem, mix_norm_w, w_in, dn_conv_w, dn_a_log, dn_dt_bias, dn_norm_w, rw_mu, rw_w0, rw_w2, rw_a0, rw_a2, rw_g2, rw_k_k, rw_k_a, rw_r_k, rw_ln_w, rw_ln_b, w_out, xa_norm_w, mem_norm_w, xa_wq, xa_wk, xa_wv, xa_wo, ffn_norm_w, ffn_w1, ffn_w2, final_norm_w, loss_target, m_mix_norm_w, m_w_in, m_dn_conv_w, m_dn_a_log, m_dn_dt_bias, m_dn_norm_w, m_rw_mu, m_rw_w0, m_rw_w2, m_rw_a0, m_rw_a2, m_rw_g2, m_rw_k_k, m_rw_k_a, m_rw_r_k, m_rw_ln_w, m_rw_ln_b, m_w_out, m_xa_norm_w, m_mem_norm_w, m_xa_wq, m_xa_wk, m_xa_wv, m_xa_wo, m_ffn_norm_w, m_ffn_w1, m_ffn_w2, m_final_norm_w, v_mix_norm_w, v_w_in, v_dn_conv_w, v_dn_a_log, v_dn_dt_bias, v_dn_norm_w, v_rw_mu, v_rw_w0, v_rw_w2, v_rw_a0, v_rw_a2, v_rw_g2, v_rw_k_k, v_rw_k_a, v_rw_r_k, v_rw_ln_w, v_rw_ln_b, v_w_out, v_xa_norm_w, v_mem_norm_w, v_xa_wq, v_xa_wk, v_xa_wv, v_xa_wo, v_ffn_norm_w, v_ffn_w1, v_ffn_w2, v_final_norm_w):
    given = dict(x=x, mem=mem, mix_norm_w=mix_norm_w, w_in=w_in, dn_conv_w=dn_conv_w, dn_a_log=dn_a_log, dn_dt_bias=dn_dt_bias, dn_norm_w=dn_norm_w, rw_mu=rw_mu, rw_w0=rw_w0, rw_w2=rw_w2, rw_a0=rw_a0, rw_a2=rw_a2, rw_g2=rw_g2, rw_k_k=rw_k_k, rw_k_a=rw_k_a, rw_r_k=rw_r_k, rw_ln_w=rw_ln_w, rw_ln_b=rw_ln_b, w_out=w_out, xa_norm_w=xa_norm_w, mem_norm_w=mem_norm_w, xa_wq=xa_wq, xa_wk=xa_wk, xa_wv=xa_wv, xa_wo=xa_wo, ffn_norm_w=ffn_norm_w, ffn_w1=ffn_w1, ffn_w2=ffn_w2, final_norm_w=final_norm_w, loss_target=loss_target, m_mix_norm_w=m_mix_norm_w, m_w_in=m_w_in, m_dn_conv_w=m_dn_conv_w, m_dn_a_log=m_dn_a_log, m_dn_dt_bias=m_dn_dt_bias, m_dn_norm_w=m_dn_norm_w, m_rw_mu=m_rw_mu, m_rw_w0=m_rw_w0, m_rw_w2=m_rw_w2, m_rw_a0=m_rw_a0, m_rw_a2=m_rw_a2, m_rw_g2=m_rw_g2, m_rw_k_k=m_rw_k_k, m_rw_k_a=m_rw_k_a, m_rw_r_k=m_rw_r_k, m_rw_ln_w=m_rw_ln_w, m_rw_ln_b=m_rw_ln_b, m_w_out=m_w_out, m_xa_norm_w=m_xa_norm_w, m_mem_norm_w=m_mem_norm_w, m_xa_wq=m_xa_wq, m_xa_wk=m_xa_wk, m_xa_wv=m_xa_wv, m_xa_wo=m_xa_wo, m_ffn_norm_w=m_ffn_norm_w, m_ffn_w1=m_ffn_w1, m_ffn_w2=m_ffn_w2, m_final_norm_w=m_final_norm_w, v_mix_norm_w=v_mix_norm_w, v_w_in=v_w_in, v_dn_conv_w=v_dn_conv_w, v_dn_a_log=v_dn_a_log, v_dn_dt_bias=v_dn_dt_bias, v_dn_norm_w=v_dn_norm_w, v_rw_mu=v_rw_mu, v_rw_w0=v_rw_w0, v_rw_w2=v_rw_w2, v_rw_a0=v_rw_a0, v_rw_a2=v_rw_a2, v_rw_g2=v_rw_g2, v_rw_k_k=v_rw_k_k, v_rw_k_a=v_rw_k_a, v_rw_r_k=v_rw_r_k, v_rw_ln_w=v_rw_ln_w, v_rw_ln_b=v_rw_ln_b, v_w_out=v_w_out, v_xa_norm_w=v_xa_norm_w, v_mem_norm_w=v_mem_norm_w, v_xa_wq=v_xa_wq, v_xa_wk=v_xa_wk, v_xa_wv=v_xa_wv, v_xa_wo=v_xa_wo, v_ffn_norm_w=v_ffn_norm_w, v_ffn_w1=v_ffn_w1, v_ffn_w2=v_ffn_w2, v_final_norm_w=v_final_norm_w)
    weights = {n: given[n] for n in TWIN_WEIGHTS}
    shared = {n: given[n] for n in SHARED_INPUTS}
    per_example = {n: given[n] for n in ['x', 'mem']}
    grad_fn = _jax.value_and_grad(_loss, argnums=(0, 1))

    def one_microbatch(ex, loss_target):
        ex = dict(ex)
        diff = ex.pop(TWIN_DIFF_INPUT)
        return grad_fn(weights, diff, {**shared, **ex}, loss_target)

    if N_MICROBATCH == 1:
        loss, (grad_w, grad_x) = one_microbatch(per_example, given["loss_target"])
    else:
        def body(carry, xs):
            loss_sum, grad_sum = carry
            l_k, (gw_k, gx_k) = one_microbatch(xs[0], xs[1])
            with _jax.named_scope("update"):
                return (loss_sum + l_k, _jax.tree.map(_jnp.add, grad_sum, gw_k)), gx_k

        init = (_jnp.zeros((), _jnp.float32), _jax.tree.map(_jnp.zeros_like, weights))
        (loss, grad_w), grad_x = _jax.lax.scan(body, init, (per_example, given["loss_target"]))
    with _jax.named_scope("update"):
        delta_w, new_m, new_v = {}, {}, {}
        for n in TWIN_WEIGHTS:
            delta_w[n], new_m[n], new_v[n] = _adamw(weights[n], grad_w[n], given["m_" + n], given["v_" + n])
    return (loss, grad_x, *[grad_w[n] for n in TWIN_WEIGHTS], *[delta_w[n] for n in TWIN_WEIGHTS],
            *[new_m[n] for n in TWIN_WEIGHTS], *[new_v[n] for n in TWIN_WEIGHTS])
```

```python
import functools
import math

import jax
import jax.numpy as jnp
from jax import lax
from jax.experimental import pallas as pl
from jax.experimental.pallas import tpu as pltpu

F32 = jnp.float32
BF16 = jnp.bfloat16
SDS = jax.ShapeDtypeStruct

N_DEV = 8
D_MODEL = 2048
LANES = 128
CHUNK = 128
DN_HEADS = 8
DN_WIDTH = 1024
RW_WIDTH = 1024
RW_HEAD = 64
XA_HEADS = 4
XA_WIDTH = 512
FFN_HIDDEN = 8192
IN_COLS = 7440
DN_COLS = 4112
IN_PAD = 7680
RW_OFF = 4224
RMS_EPS = 1e-6
RW_GN_EPS = 64e-5
VMEM_LIMIT = 56 * 1024 * 1024

ADAM_LR = 0.001
ADAM_B1 = 0.9
ADAM_B2 = 0.999
ADAM_EPS = 1e-08
ADAM_WD = 0.01
ADAM_STEP = 10

_DIMS = {"nn": (((1,), (0,)), ((), ())), "nt": (((1,), (1,)), ((), ())), "tn": (((0,), (0,)), ((), ()))}


def _raw_dot(a, b, mode, hi):
    if hi:
        return lax.dot_general(a, b, _DIMS[mode], precision=lax.Precision.HIGHEST, preferred_element_type=F32)
    return lax.dot_general(a.astype(BF16), b.astype(BF16), _DIMS[mode], preferred_element_type=F32)


@functools.partial(jax.custom_vjp, nondiff_argnums=(2, 3))
def mm(a, b, mode="nn", hi=False):
    return _raw_dot(a, b, mode, hi)


def _mm_fwd(a, b, mode, hi):
    return _raw_dot(a, b, mode, hi), (a, b)


def _mm_bwd(mode, hi, res, g):
    a, b = res
    if mode == "nn":
        return _raw_dot(g, b, "nt", hi), _raw_dot(a, g, "tn", hi)
    if mode == "nt":
        return _raw_dot(g, b, "nn", hi), _raw_dot(g, a, "tn", hi)
    return _raw_dot(b, g, "nt", hi), _raw_dot(a, g, "nn", hi)


mm.defvjp(_mm_fwd, _mm_bwd)


def _shift_rows_raw(x, k):
    n = x.shape[0]
    rolled = pltpu.roll(x, k % n, axis=0)
    row = lax.broadcasted_iota(jnp.int32, x.shape, 0)
    keep = row >= k if k > 0 else row < n + k
    return jnp.where(keep, rolled, 0.0)


@functools.partial(jax.custom_vjp, nondiff_argnums=(1,))
def shift_rows(x, k):
    return _shift_rows_raw(x, k)


shift_rows.defvjp(lambda x, k: (_shift_rows_raw(x, k), None), lambda k, _, g: (_shift_rows_raw(g, -k),))


def _softplus(x):
    return jnp.maximum(x, 0.0) + jnp.log(1.0 + jnp.exp(-jnp.abs(x)))


def _sigmoid(x):
    return 1.0 / (1.0 + jnp.exp(-x))


def _silu(x):
    return x * _sigmoid(x)


def _tri_masks(n):
    ii = lax.broadcasted_iota(jnp.int32, (n, n), 0)
    jj = lax.broadcasted_iota(jnp.int32, (n, n), 1)
    return ii >= jj, ii > jj, ii == jj


def _neumann_inv(m):
    n = m.shape[0]
    _, _, eye = _tri_masks(n)
    p = jnp.where(eye, 1.0, 0.0) + m
    mk = m
    for _ in range(int(math.log2(n)) - 1):
        mk = mm(mk, mk, "nn", True)
        p = p + mm(p, mk, "nn", True)
    return p


def _gdn_chunk(s0, q, k, v, gb, bb):
    c = q.shape[0]
    causal, strict, _ = _tri_masks(c)
    gc = mm(jnp.where(causal, 1.0, 0.0), gb, "nn", True)
    diff = gc - gc.T
    decay = jnp.exp(jnp.where(causal, diff, -jnp.inf))
    kb = k * bb
    a = jnp.where(strict, mm(kb, k, "nt") * decay, 0.0)
    p = _neumann_inv(-a)
    u = mm(p, v * bb)
    w = mm(p, kb * jnp.exp(gc))
    attn = mm(q, k, "nt") * decay
    v_new = u - mm(w, s0)
    o = mm(q * jnp.exp(gc), s0) + mm(attn, v_new)
    g_last = jnp.sum(gb, axis=0, keepdims=True)
    s1 = s0 * jnp.exp(g_last) + mm(k * jnp.exp(g_last - gc), v_new, "tn")
    return o, s1


def _rw_chunk(s0, r, lw, k, v, al, be):
    c = r.shape[0]
    causal, strict, _ = _tri_masks(c)
    gc = mm(jnp.where(causal, 1.0, 0.0), lw, "nn", True)
    gp = gc - lw
    row = lax.broadcasted_iota(jnp.int32, lw.shape, 0)
    lane = lax.broadcasted_iota(jnp.int32, lw.shape, 1)
    g_mid = jnp.sum(jnp.where(row < c // 2, lw, 0.0), axis=0, keepdims=True)
    g_last = jnp.sum(lw, axis=0, keepdims=True)
    e_n = jnp.exp(g_mid - gc)
    rg = r * jnp.exp(gc - g_mid)
    bg = be * jnp.exp(gp - g_mid)
    an = al * e_n
    kn = k * e_n
    bt = mm(be * jnp.exp(gp), s0, "nt")
    rt = mm(r * jnp.exp(gc), s0, "nt")
    us, ys = [], []
    for h in range(2):
        mine = (lane >= RW_HEAD) if h else (lane < RW_HEAD)
        bgh = jnp.where(mine, bg, 0.0)
        rgh = jnp.where(mine, rg, 0.0)
        a_ab = jnp.where(strict, mm(bgh, an, "nt"), 0.0)
        a_kb = jnp.where(strict, mm(bgh, kn, "nt"), 0.0)
        a_ra = jnp.where(causal, mm(rgh, an, "nt"), 0.0)
        a_rk = jnp.where(causal, mm(rgh, kn, "nt"), 0.0)
        p = _neumann_inv(a_ab)
        u_h = mm(p, bt + mm(a_kb, v))
        us.append(u_h)
        ys.append(rt + mm(a_ra, u_h) + mm(a_rk, v))
    lo = lane < RW_HEAD
    u = jnp.where(lo, us[0], us[1])
    y = jnp.where(lo, ys[0], ys[1])
    tail = jnp.exp(g_last - gc)
    s1 = s0 * jnp.exp(g_last) + mm(u, al * tail, "tn") + mm(v, k * tail, "tn")
    vi = lax.broadcasted_iota(jnp.int32, s0.shape, 0)
    ki = lax.broadcasted_iota(jnp.int32, s0.shape, 1)
    s1 = jnp.where((vi < RW_HEAD) == (ki < RW_HEAD), s1, 0.0)
    return y, s1


def _scan_specs(arrs, n_chunks, reverse):
    def spec(off):
        if reverse:
            return pl.BlockSpec((CHUNK, LANES), lambda h, n: (n_chunks - 1 - n, off + h))
        return pl.BlockSpec((CHUNK, LANES), lambda h, n: (n, off + h))
    return [spec(off) for _, off in arrs]


def _scan_fwd(chunk_fn, name, arrs, heads):
    s = arrs[0][0].shape[0]
    n_chunks = s // CHUNK
    n_in = len(arrs)

    def body(*refs):
        y_ref, st_ref, s_scr = refs[n_in:]

        @pl.when(pl.program_id(1) == 0)
        def _():
            s_scr[...] = jnp.zeros_like(s_scr)

        s0 = s_scr[...]
        st_ref[...] = s0
        y, s1 = chunk_fn(s0, *[r[...] for r in refs[:n_in]])
        y_ref[...] = y
        s_scr[...] = s1

    return pl.pallas_call(
        body, grid=(heads, n_chunks), name=name,
        in_specs=_scan_specs(arrs, n_chunks, False),
        out_specs=[pl.BlockSpec((CHUNK, LANES), lambda h, n: (n, h)),
                   pl.BlockSpec((None, None, LANES, LANES), lambda h, n: (h, n, 0, 0))],
        out_shape=[SDS((s, heads * LANES), F32), SDS((heads, n_chunks, LANES, LANES), F32)],
        scratch_shapes=[pltpu.VMEM((LANES, LANES), F32)],
        compiler_params=pltpu.CompilerParams(dimension_semantics=("arbitrary", "arbitrary")),
    )(*[a for a, _ in arrs])


def _scan_bwd(chunk_fn, name, arrs, states, dy, heads):
    s = arrs[0][0].shape[0]
    n_chunks = s // CHUNK
    n_in = len(arrs)

    def body(*refs):
        st_ref, dy_ref = refs[n_in:n_in + 2]
        d_refs = refs[n_in + 2:2 * n_in + 2]
        ds_scr = refs[-1]

        @pl.when(pl.program_id(1) == 0)
        def _():
            ds_scr[...] = jnp.zeros_like(ds_scr)

        _, vjp = jax.vjp(chunk_fn, st_ref[...], *[r[...] for r in refs[:n_in]])
        grads = vjp((dy_ref[...], ds_scr[...]))
        ds_scr[...] = grads[0]
        for ref, g in zip(d_refs, grads[1:]):
            ref[...] = g

    rev = pl.BlockSpec((CHUNK, LANES), lambda h, n: (n_chunks - 1 - n, h))
    return pl.pallas_call(
        body, grid=(heads, n_chunks), name=name,
        in_specs=_scan_specs(arrs, n_chunks, True)
        + [pl.BlockSpec((None, None, LANES, LANES), lambda h, n: (h, n_chunks - 1 - n, 0, 0)), rev],
        out_specs=[rev] * n_in,
        out_shape=[SDS((s, heads * LANES), F32)] * n_in,
        scratch_shapes=[pltpu.VMEM((LANES, LANES), F32)],
        compiler_params=pltpu.CompilerParams(dimension_semantics=("arbitrary", "arbitrary")),
    )(*[a for a, _ in arrs], states, dy)


def _col_spec(tr, width, cb):
    return pl.BlockSpec((tr, width), lambda i: (i, cb))


def _whole(p):
    return pl.BlockSpec(p.shape, lambda i: (0,) * p.ndim)


def _row_fwd(fn, name, tiles, params, outs, tr):
    rows = tiles[0][0].shape[0]
    nt, npar = len(tiles), len(params)

    def body(*refs):
        vals = [r[...].astype(F32) for r in refs[:nt + npar]]
        for ref, o in zip(refs[nt + npar:], fn(*vals)):
            ref[...] = o.astype(ref.dtype)

    return pl.pallas_call(
        body, grid=(rows // tr,), name=name,
        in_specs=[_col_spec(tr, w, cb) for _, w, cb in tiles] + [_whole(p) for p in params],
        out_specs=[_col_spec(tr, w, 0) for w, _ in outs],
        out_shape=[SDS((rows, w), dt) for w, dt in outs],
        compiler_params=pltpu.CompilerParams(dimension_semantics=("arbitrary",), vmem_limit_bytes=VMEM_LIMIT),
    )(*[a for a, _, _ in tiles], *params)


def _row_bwd(fn, name, tiles, params, cts, tr, want_tiles=None):
    rows = tiles[0][0].shape[0]
    nt, npar = len(tiles), len(params)
    want = list(range(nt)) if want_tiles is None else list(want_tiles)
    flat_cts = [c for group in cts for c in group]
    n_ct = len(flat_cts)

    def body(*refs):
        vals = [r[...].astype(F32) for r in refs[:nt + npar]]
        ct_refs = refs[nt + npar:nt + npar + n_ct]
        out_refs = refs[nt + npar + n_ct:]
        ct_vals, at = [], 0
        for group in cts:
            total = ct_refs[at][...].astype(F32)
            for r in ct_refs[at + 1:at + len(group)]:
                total = total + r[...].astype(F32)
            ct_vals.append(total)
            at += len(group)
        _, vjp = jax.vjp(lambda *a: tuple(fn(*a)), *vals)
        grads = vjp(tuple(ct_vals))
        for ref, t in zip(out_refs[:len(want)], want):
            ref[...] = grads[t]
        first = pl.program_id(0) == 0
        for ref, g in zip(out_refs[len(want):], grads[nt:]):
            @pl.when(first)
            def _(ref=ref, g=g):
                ref[...] = g

            @pl.when(jnp.logical_not(first))
            def _(ref=ref, g=g):
                ref[...] += g

    res = pl.pallas_call(
        body, grid=(rows // tr,), name=name,
        in_specs=[_col_spec(tr, w, cb) for _, w, cb in tiles] + [_whole(p) for p in params]
        + [_col_spec(tr, w, cb) for _, w, cb in flat_cts],
        out_specs=[_col_spec(tr, tiles[t][1], 0) for t in want] + [_whole(p) for p in params],
        out_shape=[SDS((rows, tiles[t][1]), F32) for t in want] + [SDS(p.shape, F32) for p in params],
        compiler_params=pltpu.CompilerParams(dimension_semantics=("arbitrary",), vmem_limit_bytes=VMEM_LIMIT),
    )(*[a for a, _, _ in tiles], *params, *[a for a, _, _ in flat_cts])
    return res[:len(want)], res[len(want):]


def _col_fwd(fn, name, x, first_block, n_blocks, params):
    rows = x.shape[0]

    def body(*refs):
        refs[-1][...] = fn(*[r[...] for r in refs[:-1]])

    return pl.pallas_call(
        body, grid=(n_blocks,), name=name,
        in_specs=[pl.BlockSpec((rows, LANES), lambda j: (0, first_block + j))]
        + [pl.BlockSpec((p.shape[0], LANES), lambda j: (0, j)) for p in params],
        out_specs=pl.BlockSpec((rows, LANES), lambda j: (0, j)),
        out_shape=SDS((rows, n_blocks * LANES), F32),
        compiler_params=pltpu.CompilerParams(dimension_semantics=("arbitrary",), vmem_limit_bytes=VMEM_LIMIT),
    )(x, *params)


def _col_bwd(fn, name, x, first_block, n_blocks, params, dy):
    rows = x.shape[0]
    npar = len(params)

    def body(*refs):
        vals = [r[...] for r in refs[:1 + npar]]
        _, vjp = jax.vjp(fn, *vals)
        grads = vjp(refs[1 + npar][...])
        for ref, g in zip(refs[2 + npar:], grads):
            ref[...] = g

    pspecs = [pl.BlockSpec((p.shape[0], LANES), lambda j: (0, j)) for p in params]
    blk = pl.BlockSpec((rows, LANES), lambda j: (0, j))
    res = pl.pallas_call(
        body, grid=(n_blocks,), name=name,
        in_specs=[pl.BlockSpec((rows, LANES), lambda j: (0, first_block + j))] + pspecs + [blk],
        out_specs=[blk] + pspecs,
        out_shape=[SDS((rows, n_blocks * LANES), F32)] + [SDS(p.shape, F32) for p in params],
        compiler_params=pltpu.CompilerParams(dimension_semantics=("arbitrary",), vmem_limit_bytes=VMEM_LIMIT),
    )(x, *params, dy)
    return res[0], res[1:]


def _conv_fn(x, w):
    acc = x * w[3:4, :]
    for j in range(3):
        acc = acc + shift_rows(x, 3 - j) * w[j:j + 1, :]
    return _silu(acc)


def _lerp_fn(x, mu):
    return x + (shift_rows(x, 1) - x) * mu[0:1, :]


def _seg_sum(x, width):
    if width == LANES:
        return jnp.sum(x, axis=1, keepdims=True)
    lo = lax.broadcasted_iota(jnp.int32, x.shape, 1) < width
    s0 = jnp.sum(jnp.where(lo, x, 0.0), axis=1, keepdims=True)
    s1 = jnp.sum(jnp.where(lo, 0.0, x), axis=1, keepdims=True)
    return jnp.where(lo, s0, s1)


def _per_block(fn, *xs):
    n = xs[0].shape[1] // LANES
    return jnp.concatenate([fn(*[x[:, LANES * b:LANES * (b + 1)] for x in xs]) for b in range(n)], axis=1)


def _head_expand(col0):
    r = lax.broadcasted_iota(jnp.int32, (LANES, DN_WIDTH), 0)
    c = lax.shift_right_logical(lax.broadcasted_iota(jnp.int32, (LANES, DN_WIDTH), 1), 7)
    return jnp.where(r == c + col0, 1.0, 0.0)


def _dn_pre_fn(cq, ck, gates, a_log, dt_bias):
    l2 = lambda x: x * lax.rsqrt(_seg_sum(x * x, LANES) + 1e-6)
    qh = _per_block(l2, cq) * (LANES ** -0.5)
    kh = _per_block(l2, ck)
    g = -jnp.exp(a_log) * _softplus(gates + dt_bias)
    gb = mm(g, _head_expand(0), "nn", True)
    bb = mm(_sigmoid(gates), _head_expand(DN_HEADS), "nn", True)
    return qh, kh, gb, bb


def _dn_post_fn(o, z, nw):
    def one(ob, zb):
        return ob * lax.rsqrt(_seg_sum(ob * ob, LANES) * (1.0 / LANES) + RMS_EPS) * nw * _silu(zb)
    return (_per_block(one, o, z),)


def _rw_pre_fn(pr, pk, pv, pwa, pg, w0, a0, k_k, k_a, w2p, a2p, g2):
    log_w = -_softplus(-(w0 + mm(jnp.tanh(pwa), w2p))) - 0.5
    lw = -jnp.exp(log_w)
    a = _sigmoid(a0 + mm(pwa, a2p))
    gate = mm(_sigmoid(pg), g2)
    kk = pk * k_k
    kk = _per_block(lambda x: x / jnp.maximum(jnp.sqrt(_seg_sum(x * x, RW_HEAD)), 1e-12), kk)
    k = pk * (1.0 + (a - 1.0) * k_a)
    return pr, lw, k, pv, kk * a, -kk, gate


def _rw_post_fn(y, r, k, v, gate, ln_w, ln_b, r_k):
    def one(yb, rb, kb, vb, gb, wb, bb, rkb):
        d = yb - _seg_sum(yb, RW_HEAD) * (1.0 / RW_HEAD)
        var = _seg_sum(d * d, RW_HEAD) * (1.0 / RW_HEAD)
        yn = d * lax.rsqrt(var + RW_GN_EPS) * wb + bb
        return (yn + _seg_sum(rb * kb * rkb, RW_HEAD) * vb) * gb
    return (_per_block(one, y, r, k, v, gate, ln_w, ln_b, r_k),)


def _rms_fn(h, w):
    return (h * lax.rsqrt(jnp.mean(h * h, axis=1, keepdims=True) + RMS_EPS) * w,)


def _xattn_fn(q, k, v):
    outs = []
    for h in range(XA_HEADS):
        sl = slice(LANES * h, LANES * (h + 1))
        s = mm(q[:, sl], k[:, sl], "nt") * (LANES ** -0.5)
        e = jnp.exp(s - jnp.max(s, axis=1, keepdims=True))
        outs.append(mm(e / jnp.sum(e, axis=1, keepdims=True), v[:, sl]))
    return (jnp.concatenate(outs, axis=1),)


def _fit(tile, dim):
    best = [t for t in range(LANES, min(tile, dim) + 1, LANES) if dim % t == 0]
    assert best, (tile, dim)
    return best[-1]


def _matmul(name, a, b, mode, out_dtypes, epilogue=None, extras=(), tm=512, tn=512, tk=1024):
    if mode == "tn":
        (k_dim, m), n = a.shape, b.shape[1]
    else:
        (m, k_dim), n = a.shape, (b.shape[1] if mode == "nn" else b.shape[0])
    tm, tn, tk = _fit(tm, m), _fit(tn, n), _fit(tk, k_dim)
    nk = k_dim // tk
    a_spec = (pl.BlockSpec((tk, tm), lambda i, j, k: (k, i)) if mode == "tn"
              else pl.BlockSpec((tm, tk), lambda i, j, k: (i, k)))
    b_spec = (pl.BlockSpec((tn, tk), lambda i, j, k: (j, k)) if mode == "nt"
              else pl.BlockSpec((tk, tn), lambda i, j, k: (k, j)))
    o_spec = pl.BlockSpec((tm, tn), lambda i, j, k: (i, j))
    n_ex = len(extras)

    def body(a_ref, b_ref, *rest):
        acc = rest[-1]
        k = pl.program_id(2)

        @pl.when(k == 0)
        def _():
            acc[...] = jnp.zeros_like(acc)

        acc[...] += _raw_dot(a_ref[...], b_ref[...], mode, False)

        @pl.when(k == nk - 1)
        def _():
            ex = [r[...].astype(F32) for r in rest[:n_ex]]
            res = epilogue(acc[...], *ex) if epilogue else (acc[...],)
            for ref, o in zip(rest[n_ex:-1], res):
                ref[...] = o.astype(ref.dtype)

    res = pl.pallas_call(
        body, grid=(m // tm, n // tn, nk), name=name,
        in_specs=[a_spec, b_spec] + [o_spec] * n_ex,
        out_specs=[o_spec] * len(out_dtypes),
        out_shape=[SDS((m, n), dt) for dt in out_dtypes],
        scratch_shapes=[pltpu.VMEM((tm, tn), F32)],
        compiler_params=pltpu.CompilerParams(dimension_semantics=("parallel", "parallel", "arbitrary"),
                                             vmem_limit_bytes=VMEM_LIMIT),
    )(a, b, *extras)
    return res


def _loss_call(h, target, w, tr=256):
    rows, d = h.shape

    def fn(hv, wv, tv):
        y = _rms_fn(hv, wv)[0]
        return 0.5 * jnp.sum(jnp.mean(jnp.square(y - tv), axis=1, keepdims=True), axis=0, keepdims=True)

    def body(h_ref, t_ref, w_ref, loss_ref, dh_ref, dw_ref):
        tv = t_ref[...]
        val, vjp = jax.vjp(lambda hv, wv: fn(hv, wv, tv), h_ref[...], w_ref[...])
        dh, dw = vjp(jnp.ones((1, 1), F32))
        dh_ref[...] = dh
        first = pl.program_id(0) == 0

        @pl.when(first)
        def _():
            loss_ref[...] = jnp.broadcast_to(val, loss_ref.shape)
            dw_ref[...] = dw

        @pl.when(jnp.logical_not(first))
        def _():
            loss_ref[...] += jnp.broadcast_to(val, loss_ref.shape)
            dw_ref[...] += dw

    return pl.pallas_call(
        body, grid=(rows // tr,), name="loss_head",
        in_specs=[_col_spec(tr, d, 0), _col_spec(tr, d, 0), _whole(w)],
        out_specs=[pl.BlockSpec((8, LANES), lambda i: (0, 0)), _col_spec(tr, d, 0), _whole(w)],
        out_shape=[SDS((8, LANES), F32), SDS((rows, d), F32), SDS(w.shape, F32)],
        compiler_params=pltpu.CompilerParams(dimension_semantics=("arbitrary",), vmem_limit_bytes=VMEM_LIMIT),
    )(h, target, w)


def _adamw_vals(w, g, m, v):
    m = ADAM_B1 * m + (1.0 - ADAM_B1) * g
    v = ADAM_B2 * v + (1.0 - ADAM_B2) * jnp.square(g)
    m_hat = m / (1.0 - ADAM_B1 ** ADAM_STEP)
    v_hat = v / (1.0 - ADAM_B2 ** ADAM_STEP)
    delta = -ADAM_LR * (m_hat / (jnp.sqrt(v_hat) + ADAM_EPS) + ADAM_WD * w)
    return delta, m, v


def _sum_adamw(name, parts, w, m, v):
    r, c = w.shape
    tr = r
    for cand in (512, 256, 128, 64, 32, 16, 8):
        if r % cand == 0 and N_DEV * cand * c * 4 <= 6 * 1024 * 1024:
            tr = cand
            break

    def body(p_ref, w_ref, m_ref, v_ref, g_ref, d_ref, m2_ref, v2_ref):
        g = p_ref[0]
        for s in range(1, N_DEV):
            g = g + p_ref[s]
        g_ref[...] = g
        d_ref[...], m2_ref[...], v2_ref[...] = _adamw_vals(w_ref[...], g, m_ref[...], v_ref[...])

    blk = pl.BlockSpec((tr, c), lambda i: (i, 0))
    return pl.pallas_call(
        body, grid=(r // tr,), name=name,
        in_specs=[pl.BlockSpec((N_DEV, tr, c), lambda i: (0, i, 0)), blk, blk, blk],
        out_specs=[blk] * 4, out_shape=[SDS((r, c), F32)] * 4,
        compiler_params=pltpu.CompilerParams(dimension_semantics=("arbitrary",), vmem_limit_bytes=VMEM_LIMIT),
    )(parts, w, m, v)


def _peers():
    x, y, c = lax.axis_index("x"), lax.axis_index("y"), lax.axis_index("c")
    peers = []
    for k in range(1, N_DEV):
        px = 1 - x if k & 4 else x
        py = 1 - y if k & 2 else y
        pc = 1 - c if k & 1 else c
        peers.append(((px, py, pc), 4 * px + 2 * py + pc))
    return 4 * x + 2 * y + c, peers


def _slot(ref, idx, cols):
    if cols is None:
        return ref.at[idx]
    return ref.at[:, pl.ds(pl.multiple_of(idx * cols, LANES), cols)]


def _exchange(name, srcs, dsts, gather):
    n = len(srcs)

    def body(*refs):
        src_refs, out_refs = refs[:n], refs[n:2 * n]
        send_sems, recv_sems, local_sems = refs[2 * n:]
        me, peers = _peers()
        locals_, remotes = [], []
        for a in range(n):
            s_cols, d_cols = srcs[a][1], dsts[a][2]
            mine = src_refs[a] if gather else _slot(src_refs[a], me, s_cols)
            cp = pltpu.make_async_copy(mine, _slot(out_refs[a], me, d_cols), local_sems.at[a])
            cp.start()
            locals_.append(cp)
            for k, (pos, idx) in enumerate(peers):
                out_blk = src_refs[a] if gather else _slot(src_refs[a], idx, s_cols)
                rc = pltpu.make_async_remote_copy(
                    src_ref=out_blk, dst_ref=_slot(out_refs[a], me, d_cols),
                    send_sem=send_sems.at[a, k], recv_sem=recv_sems.at[a, k],
                    device_id=pos, device_id_type=pl.DeviceIdType.MESH)
                rc.start()
                remotes.append((rc, pltpu.make_async_remote_copy(
                    src_ref=out_blk, dst_ref=_slot(out_refs[a], idx, d_cols),
                    send_sem=send_sems.at[a, k], recv_sem=recv_sems.at[a, k],
                    device_id=pos, device_id_type=pl.DeviceIdType.MESH)))
        for rc, landing in remotes:
            landing.wait_recv()
        for rc, landing in remotes:
            rc.wait_send()
        for cp in locals_:
            cp.wait()

    any_spec = pl.BlockSpec(memory_space=pl.ANY)
    return pl.pallas_call(
        body, name=name,
        in_specs=[any_spec] * n, out_specs=[any_spec] * n,
        out_shape=[SDS(shape, dt) for shape, dt, _ in dsts],
        scratch_shapes=[pltpu.SemaphoreType.DMA((n, N_DEV - 1)), pltpu.SemaphoreType.DMA((n, N_DEV - 1)),
                        pltpu.SemaphoreType.DMA((n,))],
    )(*[a for a, _ in srcs])


def _rms_res_fn(h, w):
    return _rms_fn(h, w)[0], h


def _add_epilogue(acc, res):
    return (acc + res,)


def _local_step(x, mem, target, wt):
    d = D_MODEL
    g = {}
    u = _row_fwd(_rms_fn, "mix_norm", [(x, d, 0)], [wt["mix_norm_w"]], [(d, BF16)], 256)[0]
    p = _matmul("in_proj", u, wt["w_in"], "nn", [F32])[0]
    c = _col_fwd(_conv_fn, "dn_conv", p, 0, 24, [wt["dn_conv_w"]])
    dn_pre_tiles = [(c, DN_WIDTH, 0), (c, DN_WIDTH, 1), (p, LANES, 32)]
    dn_pre_params = [wt["dn_a_log"], wt["dn_dt_bias"]]
    qh, kh, gb, bb = _row_fwd(_dn_pre_fn, "dn_pre", dn_pre_tiles, dn_pre_params, [(DN_WIDTH, F32)] * 4, 128)
    dn_arrs = [(qh, 0), (kh, 0), (c, 16), (gb, 0), (bb, 0)]
    o, st_dn = _scan_fwd(_gdn_chunk, "gdn_scan", dn_arrs, DN_HEADS)
    dn_post_tiles = [(o, DN_WIDTH, 0), (p, DN_WIDTH, 3)]
    o_dn = _row_fwd(_dn_post_fn, "dn_post", dn_post_tiles, [wt["dn_norm_w"]], [(DN_WIDTH, BF16)], 256)[0]

    ps = _col_fwd(_lerp_fn, "rw_shift", p, RW_OFF // LANES, 26, [wt["rw_mu"]])
    rw_pre_tiles = [(ps, RW_WIDTH, 0), (ps, RW_WIDTH, 1), (ps, RW_WIDTH, 2), (ps, LANES, 24), (ps, LANES, 25)]
    rw_pre_params = [wt[n] for n in ("rw_w0", "rw_a0", "rw_k_k", "rw_k_a", "rw_w2", "rw_a2", "rw_g2")]
    r, lw, k, v, al, be, gate = _row_fwd(_rw_pre_fn, "rw_pre", rw_pre_tiles, rw_pre_params,
                                         [(RW_WIDTH, F32)] * 7, 128)
    rw_arrs = [(r, 0), (lw, 0), (k, 0), (v, 0), (al, 0), (be, 0)]
    y, st_rw = _scan_fwd(_rw_chunk, "rw_scan", rw_arrs, RW_WIDTH // LANES)
    rw_post_tiles = [(t, RW_WIDTH, 0) for t in (y, r, k, v, gate)]
    rw_post_params = [wt["rw_ln_w"], wt["rw_ln_b"], wt["rw_r_k"]]
    o_rw = _row_fwd(_rw_post_fn, "rw_post", rw_post_tiles, rw_post_params, [(RW_WIDTH, BF16)], 128)[0]
    o_cat = jnp.concatenate([o_dn, o_rw], axis=1)
    h1 = _matmul("out_proj", o_cat, wt["w_out"], "nn", [F32], _add_epilogue, (x,))[0]

    hn = _row_fwd(_rms_fn, "xa_norm", [(h1, d, 0)], [wt["xa_norm_w"]], [(d, BF16)], 256)[0]
    mn = _row_fwd(_rms_fn, "mem_norm", [(mem, d, 0)], [wt["mem_norm_w"]], [(d, BF16)], 256)[0]
    q = _matmul("xa_q", hn, wt["xa_wq"], "nn", [F32])[0]
    kx = _matmul("xa_k", mn, wt["xa_wk"], "nn", [F32])[0]
    vx = _matmul("xa_v", mn, wt["xa_wv"], "nn", [F32])[0]
    ao = _row_fwd(_xattn_fn, "xattn", [(q, XA_WIDTH, 0)], [kx, vx], [(XA_WIDTH, BF16)], 256)[0]
    h2 = _matmul("xa_o", ao, wt["xa_wo"], "nn", [F32], _add_epilogue, (h1,))[0]

    f = _row_fwd(_rms_fn, "ffn_norm", [(h2, d, 0)], [wt["ffn_norm_w"]], [(d, BF16)], 256)[0]
    a, hid = _matmul("ffn_up", f, wt["ffn_w1"], "nn", [F32, BF16],
                     lambda acc: (acc, jnp.square(jnp.maximum(acc, 0.0))))
    h3 = _matmul("ffn_down", hid, wt["ffn_w2"], "nn", [F32], _add_epilogue, (h2,))[0]
    loss8, dh3, g["final_norm_w"] = _loss_call(h3, target, wt["final_norm_w"])

    da = _matmul("ffn_down_dx", dh3, wt["ffn_w2"], "nt", [BF16],
                 lambda acc, av: (acc * 2.0 * jnp.maximum(av, 0.0),), (a,))[0]
    g["ffn_w2"] = _matmul("ffn_down_dw", hid, dh3, "tn", [F32])[0]
    g["ffn_w1"] = _matmul("ffn_up_dw", f, da, "tn", [F32])[0]
    df = _matmul("ffn_up_dx", da, wt["ffn_w1"], "nt", [F32])[0]
    (dh2,), (g["ffn_norm_w"],) = _row_bwd(_rms_res_fn, "ffn_norm_bwd", [(h2, d, 0)], [wt["ffn_norm_w"]],
                                          [[(df, d, 0)], [(dh3, d, 0)]], 256)

    dao = _matmul("xa_o_dx", dh2, wt["xa_wo"], "nt", [F32])[0]
    g["xa_wo"] = _matmul("xa_o_dw", ao, dh2, "tn", [F32])[0]
    (dq,), (dkx, dvx) = _row_bwd(_xattn_fn, "xattn_bwd", [(q, XA_WIDTH, 0)], [kx, vx], [[(dao, XA_WIDTH, 0)]], 256)
    dhn = _matmul("xa_q_dx", dq, wt["xa_wq"], "nt", [F32])[0]
    g["xa_wq"] = _matmul("xa_q_dw", hn, dq, "tn", [F32])[0]
    g["xa_wk"] = _matmul("xa_k_dw", mn, dkx, "tn", [F32])[0]
    g["xa_wv"] = _matmul("xa_v_dw", mn, dvx, "tn", [F32])[0]
    dmn = _matmul("xa_k_dx", dkx, wt["xa_wk"], "nt", [F32])[0]
    dmn = _matmul("xa_v_dx", dvx, wt["xa_wv"], "nt", [F32], _add_epilogue, (dmn,))[0]
    _, (g["mem_norm_w"],) = _row_bwd(_rms_fn, "mem_norm_bwd", [(mem, d, 0)], [wt["mem_norm_w"]],
                                     [[(dmn, d, 0)]], 256, want_tiles=())
    (dh1,), (g["xa_norm_w"],) = _row_bwd(_rms_res_fn, "xa_norm_bwd", [(h1, d, 0)], [wt["xa_norm_w"]],
                                         [[(dhn, d, 0)], [(dh2, d, 0)]], 256)

    do_cat = _matmul("out_proj_dx", dh1, wt["w_out"], "nt", [F32])[0]
    g["w_out"] = _matmul("out_proj_dw", o_cat, dh1, "tn", [F32])[0]

    (dy, dr1, dk1, dv1, dgate), (g["rw_ln_w"], g["rw_ln_b"], g["rw_r_k"]) = _row_bwd(
        _rw_post_fn, "rw_post_bwd", rw_post_tiles, rw_post_params, [[(do_cat, RW_WIDTH, 1)]], 128)
    dr2, dlw, dk2, dv2, dal, dbe = _scan_bwd(_rw_chunk, "rw_scan_bwd", rw_arrs, st_rw, dy, RW_WIDTH // LANES)
    one = lambda t: [(t, RW_WIDTH, 0)]
    two = lambda s, t: [(s, RW_WIDTH, 0), (t, RW_WIDTH, 0)]
    d_ps, rw_pre_grads = _row_bwd(
        _rw_pre_fn, "rw_pre_bwd", rw_pre_tiles, rw_pre_params,
        [two(dr1, dr2), one(dlw), two(dk1, dk2), two(dv1, dv2), one(dal), one(dbe), one(dgate)], 128)
    for n, val in zip(("rw_w0", "rw_a0", "rw_k_k", "rw_k_a", "rw_w2", "rw_a2", "rw_g2"), rw_pre_grads):
        g[n] = val
    dp_rw, (g["rw_mu"],) = _col_bwd(_lerp_fn, "rw_shift_bwd", p, RW_OFF // LANES, 26, [wt["rw_mu"]],
                                    jnp.concatenate(d_ps, axis=1))

    (do, dz), (g["dn_norm_w"],) = _row_bwd(_dn_post_fn, "dn_post_bwd", dn_post_tiles, [wt["dn_norm_w"]],
                                           [[(do_cat, DN_WIDTH, 0)]], 256)
    dqh, dkh, dv_dn, dgb, dbb = _scan_bwd(_gdn_chunk, "gdn_scan_bwd", dn_arrs, st_dn, do, DN_HEADS)
    one = lambda t: [(t, DN_WIDTH, 0)]
    (dcq, dck, dgates), (g["dn_a_log"], g["dn_dt_bias"]) = _row_bwd(
        _dn_pre_fn, "dn_pre_bwd", dn_pre_tiles, dn_pre_params, [one(dqh), one(dkh), one(dgb), one(dbb)], 128)
    dp_qkv, (g["dn_conv_w"],) = _col_bwd(_conv_fn, "dn_conv_bwd", p, 0, 24, [wt["dn_conv_w"]],
                                         jnp.concatenate([dcq, dck, dv_dn], axis=1))
    dp = jnp.concatenate([dp_qkv, dz, dgates, dp_rw, jnp.zeros((x.shape[0], LANES), F32)], axis=1)
    du = _matmul("in_proj_dx", dp, wt["w_in"], "nt", [F32])[0]
    g["w_in"] = _matmul("in_proj_dw", u, dp, "tn", [F32])[0]
    (dx,), (g["mix_norm_w"],) = _row_bwd(_rms_res_fn, "mix_norm_bwd", [(x, d, 0)], [wt["mix_norm_w"]],
                                         [[(du, d, 0)], [(dh1, d, 0)]], 256)
    return loss8, dx, g


WEIGHTS = ["mix_norm_w", "w_in", "dn_conv_w", "dn_a_log", "dn_dt_bias", "dn_norm_w", "rw_mu", "rw_w0", "rw_w2",
           "rw_a0", "rw_a2", "rw_g2", "rw_k_k", "rw_k_a", "rw_r_k", "rw_ln_w", "rw_ln_b", "w_out", "xa_norm_w",
           "mem_norm_w", "xa_wq", "xa_wk", "xa_wv", "xa_wo", "ffn_norm_w", "ffn_w1", "ffn_w2", "final_norm_w"]
SHARDED = {"w_in": True, "w_out": False, "xa_wq": False, "xa_wk": False, "xa_wv": False, "xa_wo": True,
           "ffn_w1": True, "ffn_w2": False, "dn_conv_w": True, "rw_w2": True, "rw_a2": True, "rw_g2": True}
BF16_PAYLOAD = ("w_in", "w_out", "xa_wq", "xa_wk", "xa_wv", "xa_wo", "ffn_w1", "ffn_w2")
REPLICATED = [n for n in WEIGHTS if n not in SHARDED]
RW_IN_COLS = IN_COLS - DN_COLS


def _layout_weights(fw):
    wt = dict(fw)
    w_in = fw["w_in"]
    rows = w_in.shape[0]
    wt["w_in"] = jnp.concatenate(
        [w_in[:, :DN_COLS], jnp.zeros((rows, RW_OFF - DN_COLS), w_in.dtype), w_in[:, DN_COLS:],
         jnp.zeros((rows, IN_PAD - RW_OFF - RW_IN_COLS), w_in.dtype)], axis=1)
    wt["dn_conv_w"] = jnp.pad(fw["dn_conv_w"], ((0, 4), (0, 0)))
    wt["dn_a_log"] = jnp.pad(fw["dn_a_log"], ((0, 0), (0, LANES - DN_HEADS)))
    wt["dn_dt_bias"] = jnp.pad(fw["dn_dt_bias"], ((0, 0), (0, LANES - DN_HEADS)))
    wt["rw_w2"] = jnp.pad(fw["rw_w2"], ((0, 64), (0, 0)))
    wt["rw_a2"] = jnp.pad(fw["rw_a2"], ((64, 0), (0, 0)))
    return wt


def _logical_grads(g):
    out = dict(g)
    out["w_in"] = jnp.concatenate([g["w_in"][:, :DN_COLS], g["w_in"][:, RW_OFF:RW_OFF + RW_IN_COLS]], axis=1)
    out["dn_conv_w"] = g["dn_conv_w"][:4]
    out["dn_a_log"] = g["dn_a_log"][:, :DN_HEADS]
    out["dn_dt_bias"] = g["dn_dt_bias"][:, :DN_HEADS]
    out["rw_w2"] = g["rw_w2"][:64]
    out["rw_a2"] = g["rw_a2"][64:]
    return out


def _pack(vals):
    parts = []
    for v in vals:
        flat = v.reshape(-1)
        parts.append(jnp.pad(flat, (0, -flat.shape[0] % LANES)))
    flat = jnp.concatenate(parts)
    flat = jnp.pad(flat, (0, -flat.shape[0] % (8 * LANES)))
    return flat.reshape(-1, LANES)


def _unpack(packed, shapes):
    flat = packed.reshape(-1)
    out, at = [], 0
    for shp in shapes:
        size = math.prod(shp)
        out.append(flat[at:at + size].reshape(shp))
        at += size + (-size % LANES)
    return out


def kernel(x, mem, mix_norm_w, w_in, dn_conv_w, dn_a_log, dn_dt_bias, dn_norm_w, rw_mu, rw_w0, rw_w2, rw_a0, rw_a2, rw_g2, rw_k_k, rw_k_a, rw_r_k, rw_ln_w, rw_ln_b, w_out, xa_norm_w, mem_norm_w, xa_wq, xa_wk, xa_wv, xa_wo, ffn_norm_w, ffn_w1, ffn_w2, final_norm_w, loss_target, m_mix_norm_w, m_w_in, m_dn_conv_w, m_dn_a_log, m_dn_dt_bias, m_dn_norm_w, m_rw_mu, m_rw_w0, m_rw_w2, m_rw_a0, m_rw_a2, m_rw_g2, m_rw_k_k, m_rw_k_a, m_rw_r_k, m_rw_ln_w, m_rw_ln_b, m_w_out, m_xa_norm_w, m_mem_norm_w, m_xa_wq, m_xa_wk, m_xa_wv, m_xa_wo, m_ffn_norm_w, m_ffn_w1, m_ffn_w2, m_final_norm_w, v_mix_norm_w, v_w_in, v_dn_conv_w, v_dn_a_log, v_dn_dt_bias, v_dn_norm_w, v_rw_mu, v_rw_w0, v_rw_w2, v_rw_a0, v_rw_a2, v_rw_g2, v_rw_k_k, v_rw_k_a, v_rw_r_k, v_rw_ln_w, v_rw_ln_b, v_w_out, v_xa_norm_w, v_mem_norm_w, v_xa_wq, v_xa_wk, v_xa_wv, v_xa_wo, v_ffn_norm_w, v_ffn_w1, v_ffn_w2, v_final_norm_w):
    given = dict(locals())
    w = {n: given[n] for n in WEIGHTS}
    m = {n: given["m_" + n] for n in WEIGHTS}
    v = {n: given["v_" + n] for n in WEIGHTS}

    srcs, dsts = [], []
    for n, by_cols in SHARDED.items():
        sh = w[n][0].astype(BF16) if n in BF16_PAYLOAD else w[n][0]
        r, c = sh.shape
        srcs.append((sh, None))
        if by_cols and c % LANES == 0:
            dsts.append(((r, N_DEV * c), sh.dtype, c))
        else:
            dsts.append(((N_DEV, r, c), sh.dtype, None))
    full = {}
    for n, arr in zip(SHARDED, _exchange("weight_all_gather", srcs, dsts, True)):
        if arr.ndim == 2:
            full[n] = arr
        elif SHARDED[n]:
            full[n] = arr.transpose(1, 0, 2).reshape(arr.shape[1], -1)
        else:
            full[n] = arr.reshape(-1, arr.shape[2])
    for n in REPLICATED:
        full[n] = w[n].reshape(1, -1)

    loss8, dx, g = _local_step(x[0], mem[0], loss_target[0], _layout_weights(full))
    g = _logical_grads(g)
    loss = lax.psum(loss8[0, 0], ("x", "y", "c"))

    srcs, dsts = [], []
    for n, by_cols in SHARDED.items():
        r, c = w[n].shape[1:]
        if by_cols and c % LANES == 0:
            srcs.append((g[n], c))
        elif by_cols:
            srcs.append((g[n].reshape(r, N_DEV, c).transpose(1, 0, 2), None))
        else:
            srcs.append((g[n].reshape(N_DEV, r, c), None))
        dsts.append(((N_DEV, r, c), F32, None))
    grad, delta, new_m, new_v = {}, {}, {}, {}
    for n, parts in zip(SHARDED, _exchange("grad_all_to_all", srcs, dsts, False)):
        res = _sum_adamw("adamw_" + n, parts, w[n][0], m[n][0], v[n][0])
        grad[n], delta[n], new_m[n], new_v[n] = [t[None] for t in res]

    packed = _pack([g[n] for n in REPLICATED])
    parts = _exchange("small_all_gather", [(packed, None)], [((N_DEV,) + packed.shape, F32, None)], True)[0]
    res = _sum_adamw("adamw_small", parts, _pack([w[n] for n in REPLICATED]),
                     _pack([m[n] for n in REPLICATED]), _pack([v[n] for n in REPLICATED]))
    shapes = [w[n].shape for n in REPLICATED]
    for store, packed_out in zip((grad, delta, new_m, new_v), res):
        for n, val in zip(REPLICATED, _unpack(packed_out, shapes)):
            store[n] = val

    return (loss, dx[None], *[grad[n] for n in WEIGHTS], *[delta[n] for n in WEIGHTS],
            *[new_m[n] for n in WEIGHTS], *[new_v[n] for n in WEIGHTS])
```

```python
import functools
import math

import jax
import jax.numpy as jnp
from jax import lax
from jax.experimental import pallas as pl
from jax.experimental.pallas import tpu as pltpu

F32 = jnp.float32
BF16 = jnp.bfloat16
SDS = jax.ShapeDtypeStruct

N_DEV = 8
D_MODEL = 2048
LANES = 128
CHUNK = 128
DN_HEADS = 8
DN_WIDTH = 1024
RW_WIDTH = 1024
RW_HEAD = 64
XA_HEADS = 4
XA_WIDTH = 512
FFN_HIDDEN = 8192
IN_COLS = 7440
DN_COLS = 4112
IN_PAD = 7680
RW_OFF = 4224
RMS_EPS = 1e-6
RW_GN_EPS = 64e-5
VMEM_LIMIT = 56 * 1024 * 1024

ADAM_LR = 0.001
ADAM_B1 = 0.9
ADAM_B2 = 0.999
ADAM_EPS = 1e-08
ADAM_WD = 0.01
ADAM_STEP = 10

_DIMS = {"nn": (((1,), (0,)), ((), ())), "nt": (((1,), (1,)), ((), ())), "tn": (((0,), (0,)), ((), ()))}


def _raw_dot(a, b, mode, hi):
    if hi:
        return lax.dot_general(a, b, _DIMS[mode], precision=lax.Precision.HIGHEST, preferred_element_type=F32)
    return lax.dot_general(a.astype(BF16), b.astype(BF16), _DIMS[mode], preferred_element_type=F32)


@functools.partial(jax.custom_vjp, nondiff_argnums=(2, 3))
def mm(a, b, mode="nn", hi=False):
    return _raw_dot(a, b, mode, hi)


def _mm_fwd(a, b, mode, hi):
    return _raw_dot(a, b, mode, hi), (a, b)


def _mm_bwd(mode, hi, res, g):
    a, b = res
    if mode == "nn":
        return _raw_dot(g, b, "nt", hi), _raw_dot(a, g, "tn", hi)
    if mode == "nt":
        return _raw_dot(g, b, "nn", hi), _raw_dot(g, a, "tn", hi)
    return _raw_dot(b, g, "nt", hi), _raw_dot(a, g, "nn", hi)


mm.defvjp(_mm_fwd, _mm_bwd)


def _shift_rows_raw(x, k):
    n = x.shape[0]
    rolled = pltpu.roll(x, k % n, axis=0)
    row = lax.broadcasted_iota(jnp.int32, x.shape, 0)
    keep = row >= k if k > 0 else row < n + k
    return jnp.where(keep, rolled, 0.0)


@functools.partial(jax.custom_vjp, nondiff_argnums=(1,))
def shift_rows(x, k):
    return _shift_rows_raw(x, k)


shift_rows.defvjp(lambda x, k: (_shift_rows_raw(x, k), None), lambda k, _, g: (_shift_rows_raw(g, -k),))


def _softplus(x):
    return jnp.maximum(x, 0.0) + jnp.log(1.0 + jnp.exp(-jnp.abs(x)))


def _sigmoid(x):
    return 1.0 / (1.0 + jnp.exp(-x))


def _silu(x):
    return x * _sigmoid(x)


def _tri_masks(n):
    ii = lax.broadcasted_iota(jnp.int32, (n, n), 0)
    jj = lax.broadcasted_iota(jnp.int32, (n, n), 1)
    return ii >= jj, ii > jj, ii == jj


def _neumann_inv_raw(m):
    n = m.shape[0]
    _, _, eye = _tri_masks(n)
    eye = jnp.where(eye, 1.0, 0.0)
    p = eye + m
    mk = m
    for _ in range(int(math.log2(n)) - 1):
        mk = _raw_dot(mk, mk, "nn", False)
        p = p + _raw_dot(p, mk, "nn", False)
    resid = eye - p + _raw_dot(m, p, "nn", True)
    return p + _raw_dot(p, resid, "nn", False)


@jax.custom_vjp
def _neumann_inv(m):
    return _neumann_inv_raw(m)


def _neumann_inv_fwd(m):
    p = _neumann_inv_raw(m)
    return p, p


def _neumann_inv_bwd(p, g):
    return (_raw_dot(_raw_dot(p, g, "tn", False), p, "nt", False),)


_neumann_inv.defvjp(_neumann_inv_fwd, _neumann_inv_bwd)


def _gdn_chunk(s0, q, k, v, gb, bb):
    c = q.shape[0]
    causal, strict, _ = _tri_masks(c)
    gc = mm(jnp.where(causal, 1.0, 0.0), gb, "nn", True)
    diff = gc - gc.T
    decay = jnp.exp(jnp.where(causal, diff, -jnp.inf))
    kb = k * bb
    a = jnp.where(strict, mm(kb, k, "nt") * decay, 0.0)
    p = _neumann_inv(-a)
    u = mm(p, v * bb)
    w = mm(p, kb * jnp.exp(gc))
    attn = mm(q, k, "nt") * decay
    v_new = u - mm(w, s0)
    o = mm(q * jnp.exp(gc), s0) + mm(attn, v_new)
    g_last = jnp.sum(gb, axis=0, keepdims=True)
    s1 = s0 * jnp.exp(g_last) + mm(k * jnp.exp(g_last - gc), v_new, "tn")
    return o, s1


def _rw_chunk(s0, r, lw, k, v, al, be):
    c = r.shape[0]
    causal, strict, _ = _tri_masks(c)
    gc = mm(jnp.where(causal, 1.0, 0.0), lw, "nn", True)
    gp = gc - lw
    row = lax.broadcasted_iota(jnp.int32, lw.shape, 0)
    lane = lax.broadcasted_iota(jnp.int32, lw.shape, 1)
    g_mid = jnp.sum(jnp.where(row < c // 2, lw, 0.0), axis=0, keepdims=True)
    g_last = jnp.sum(lw, axis=0, keepdims=True)
    e_n = jnp.exp(g_mid - gc)
    rg = r * jnp.exp(gc - g_mid)
    bg = be * jnp.exp(gp - g_mid)
    an = al * e_n
    kn = k * e_n
    bt = mm(be * jnp.exp(gp), s0, "nt")
    rt = mm(r * jnp.exp(gc), s0, "nt")
    us, ys = [], []
    for h in range(2):
        mine = (lane >= RW_HEAD) if h else (lane < RW_HEAD)
        bgh = jnp.where(mine, bg, 0.0)
        rgh = jnp.where(mine, rg, 0.0)
        a_ab = jnp.where(strict, mm(bgh, an, "nt"), 0.0)
        a_kb = jnp.where(strict, mm(bgh, kn, "nt"), 0.0)
        a_ra = jnp.where(causal, mm(rgh, an, "nt"), 0.0)
        a_rk = jnp.where(causal, mm(rgh, kn, "nt"), 0.0)
        p = _neumann_inv(a_ab)
        u_h = mm(p, bt + mm(a_kb, v))
        us.append(u_h)
        ys.append(rt + mm(a_ra, u_h) + mm(a_rk, v))
    lo = lane < RW_HEAD
    u = jnp.where(lo, us[0], us[1])
    y = jnp.where(lo, ys[0], ys[1])
    tail = jnp.exp(g_last - gc)
    s1 = s0 * jnp.exp(g_last) + mm(u, al * tail, "tn") + mm(v, k * tail, "tn")
    vi = lax.broadcasted_iota(jnp.int32, s0.shape, 0)
    ki = lax.broadcasted_iota(jnp.int32, s0.shape, 1)
    s1 = jnp.where((vi < RW_HEAD) == (ki < RW_HEAD), s1, 0.0)
    return y, s1


def _scan_specs(arrs, n_chunks, reverse):
    def spec(off):
        if reverse:
            return pl.BlockSpec((CHUNK, LANES), lambda h, n: (n_chunks - 1 - n, off + h))
        return pl.BlockSpec((CHUNK, LANES), lambda h, n: (n, off + h))
    return [spec(off) for _, off in arrs]


def _scan_fwd(chunk_fn, name, arrs, heads):
    s = arrs[0][0].shape[0]
    n_chunks = s // CHUNK
    n_in = len(arrs)

    def body(*refs):
        y_ref, st_ref, s_scr = refs[n_in:]

        @pl.when(pl.program_id(1) == 0)
        def _():
            s_scr[...] = jnp.zeros_like(s_scr)

        s0 = s_scr[...]
        st_ref[...] = s0
        y, s1 = chunk_fn(s0, *[r[...] for r in refs[:n_in]])
        y_ref[...] = y
        s_scr[...] = s1

    return pl.pallas_call(
        body, grid=(heads, n_chunks), name=name,
        in_specs=_scan_specs(arrs, n_chunks, False),
        out_specs=[pl.BlockSpec((CHUNK, LANES), lambda h, n: (n, h)),
                   pl.BlockSpec((None, None, LANES, LANES), lambda h, n: (h, n, 0, 0))],
        out_shape=[SDS((s, heads * LANES), F32), SDS((heads, n_chunks, LANES, LANES), F32)],
        scratch_shapes=[pltpu.VMEM((LANES, LANES), F32)],
        compiler_params=pltpu.CompilerParams(dimension_semantics=("arbitrary", "arbitrary")),
    )(*[a for a, _ in arrs])


def _scan_bwd(chunk_fn, name, arrs, states, dy, heads):
    s = arrs[0][0].shape[0]
    n_chunks = s // CHUNK
    n_in = len(arrs)

    def body(*refs):
        st_ref, dy_ref = refs[n_in:n_in + 2]
        d_refs = refs[n_in + 2:2 * n_in + 2]
        ds_scr = refs[-1]

        @pl.when(pl.program_id(1) == 0)
        def _():
            ds_scr[...] = jnp.zeros_like(ds_scr)

        _, vjp = jax.vjp(chunk_fn, st_ref[...], *[r[...] for r in refs[:n_in]])
        grads = vjp((dy_ref[...], ds_scr[...]))
        ds_scr[...] = grads[0]
        for ref, g in zip(d_refs, grads[1:]):
            ref[...] = g

    rev = pl.BlockSpec((CHUNK, LANES), lambda h, n: (n_chunks - 1 - n, h))
    return pl.pallas_call(
        body, grid=(heads, n_chunks), name=name,
        in_specs=_scan_specs(arrs, n_chunks, True)
        + [pl.BlockSpec((None, None, LANES, LANES), lambda h, n: (h, n_chunks - 1 - n, 0, 0)), rev],
        out_specs=[rev] * n_in,
        out_shape=[SDS((s, heads * LANES), F32)] * n_in,
        scratch_shapes=[pltpu.VMEM((LANES, LANES), F32)],
        compiler_params=pltpu.CompilerParams(dimension_semantics=("arbitrary", "arbitrary")),
    )(*[a for a, _ in arrs], states, dy)


def _col_spec(tr, width, cb):
    return pl.BlockSpec((tr, width), lambda i: (i, cb))


def _whole(p):
    return pl.BlockSpec(p.shape, lambda i: (0,) * p.ndim)


def _row_fwd(fn, name, tiles, params, outs, tr):
    rows = tiles[0][0].shape[0]
    nt, npar = len(tiles), len(params)

    def body(*refs):
        vals = [r[...].astype(F32) for r in refs[:nt + npar]]
        for ref, o in zip(refs[nt + npar:], fn(*vals)):
            ref[...] = o.astype(ref.dtype)

    return pl.pallas_call(
        body, grid=(rows // tr,), name=name,
        in_specs=[_col_spec(tr, w, cb) for _, w, cb in tiles] + [_whole(p) for p in params],
        out_specs=[_col_spec(tr, w, 0) for w, _ in outs],
        out_shape=[SDS((rows, w), dt) for w, dt in outs],
        compiler_params=pltpu.CompilerParams(dimension_semantics=("arbitrary",), vmem_limit_bytes=VMEM_LIMIT),
    )(*[a for a, _, _ in tiles], *params)


def _row_bwd(fn, name, tiles, params, cts, tr, want_tiles=None):
    rows = tiles[0][0].shape[0]
    nt, npar = len(tiles), len(params)
    want = list(range(nt)) if want_tiles is None else list(want_tiles)
    flat_cts = [c for group in cts for c in group]
    n_ct = len(flat_cts)

    def body(*refs):
        vals = [r[...].astype(F32) for r in refs[:nt + npar]]
        ct_refs = refs[nt + npar:nt + npar + n_ct]
        out_refs = refs[nt + npar + n_ct:]
        ct_vals, at = [], 0
        for group in cts:
            total = ct_refs[at][...].astype(F32)
            for r in ct_refs[at + 1:at + len(group)]:
                total = total + r[...].astype(F32)
            ct_vals.append(total)
            at += len(group)
        _, vjp = jax.vjp(lambda *a: tuple(fn(*a)), *vals)
        grads = vjp(tuple(ct_vals))
        for ref, t in zip(out_refs[:len(want)], want):
            ref[...] = grads[t]
        first = pl.program_id(0) == 0
        for ref, g in zip(out_refs[len(want):], grads[nt:]):
            @pl.when(first)
            def _(ref=ref, g=g):
                ref[...] = g

            @pl.when(jnp.logical_not(first))
            def _(ref=ref, g=g):
                ref[...] += g

    res = pl.pallas_call(
        body, grid=(rows // tr,), name=name,
        in_specs=[_col_spec(tr, w, cb) for _, w, cb in tiles] + [_whole(p) for p in params]
        + [_col_spec(tr, w, cb) for _, w, cb in flat_cts],
        out_specs=[_col_spec(tr, tiles[t][1], 0) for t in want] + [_whole(p) for p in params],
        out_shape=[SDS((rows, tiles[t][1]), F32) for t in want] + [SDS(p.shape, F32) for p in params],
        compiler_params=pltpu.CompilerParams(dimension_semantics=("arbitrary",), vmem_limit_bytes=VMEM_LIMIT),
    )(*[a for a, _, _ in tiles], *params, *[a for a, _, _ in flat_cts])
    return res[:len(want)], res[len(want):]


def _col_fwd(fn, name, x, first_block, n_blocks, params):
    rows = x.shape[0]

    def body(*refs):
        refs[-1][...] = fn(*[r[...] for r in refs[:-1]])

    return pl.pallas_call(
        body, grid=(n_blocks,), name=name,
        in_specs=[pl.BlockSpec((rows, LANES), lambda j: (0, first_block + j))]
        + [pl.BlockSpec((p.shape[0], LANES), lambda j: (0, j)) for p in params],
        out_specs=pl.BlockSpec((rows, LANES), lambda j: (0, j)),
        out_shape=SDS((rows, n_blocks * LANES), F32),
        compiler_params=pltpu.CompilerParams(dimension_semantics=("arbitrary",), vmem_limit_bytes=VMEM_LIMIT),
    )(x, *params)


def _col_bwd(fn, name, x, first_block, n_blocks, params, dy):
    rows = x.shape[0]
    npar = len(params)

    def body(*refs):
        vals = [r[...] for r in refs[:1 + npar]]
        _, vjp = jax.vjp(fn, *vals)
        grads = vjp(refs[1 + npar][...])
        for ref, g in zip(refs[2 + npar:], grads):
            ref[...] = g

    pspecs = [pl.BlockSpec((p.shape[0], LANES), lambda j: (0, j)) for p in params]
    blk = pl.BlockSpec((rows, LANES), lambda j: (0, j))
    res = pl.pallas_call(
        body, grid=(n_blocks,), name=name,
        in_specs=[pl.BlockSpec((rows, LANES), lambda j: (0, first_block + j))] + pspecs + [blk],
        out_specs=[blk] + pspecs,
        out_shape=[SDS((rows, n_blocks * LANES), F32)] + [SDS(p.shape, F32) for p in params],
        compiler_params=pltpu.CompilerParams(dimension_semantics=("arbitrary",), vmem_limit_bytes=VMEM_LIMIT),
    )(x, *params, dy)
    return res[0], res[1:]


def _conv_fn(x, w):
    acc = x * w[3:4, :]
    for j in range(3):
        acc = acc + shift_rows(x, 3 - j) * w[j:j + 1, :]
    return _silu(acc)


def _lerp_fn(x, mu):
    return x + (shift_rows(x, 1) - x) * mu[0:1, :]


def _seg_sum(x, width):
    if width == LANES:
        return jnp.sum(x, axis=1, keepdims=True)
    lo = lax.broadcasted_iota(jnp.int32, x.shape, 1) < width
    s0 = jnp.sum(jnp.where(lo, x, 0.0), axis=1, keepdims=True)
    s1 = jnp.sum(jnp.where(lo, 0.0, x), axis=1, keepdims=True)
    return jnp.where(lo, s0, s1)


def _per_block(fn, *xs):
    n = xs[0].shape[1] // LANES
    return jnp.concatenate([fn(*[x[:, LANES * b:LANES * (b + 1)] for x in xs]) for b in range(n)], axis=1)


def _head_expand(col0):
    r = lax.broadcasted_iota(jnp.int32, (LANES, DN_WIDTH), 0)
    c = lax.shift_right_logical(lax.broadcasted_iota(jnp.int32, (LANES, DN_WIDTH), 1), 7)
    return jnp.where(r == c + col0, 1.0, 0.0)


def _dn_pre_fn(cq, ck, gates, a_log, dt_bias):
    l2 = lambda x: x * lax.rsqrt(_seg_sum(x * x, LANES) + 1e-6)
    qh = _per_block(l2, cq) * (LANES ** -0.5)
    kh = _per_block(l2, ck)
    g = -jnp.exp(a_log) * _softplus(gates + dt_bias)
    gb = mm(g, _head_expand(0), "nn", True)
    bb = mm(_sigmoid(gates), _head_expand(DN_HEADS), "nn", True)
    return qh, kh, gb, bb


def _dn_post_fn(o, z, nw):
    def one(ob, zb):
        return ob * lax.rsqrt(_seg_sum(ob * ob, LANES) * (1.0 / LANES) + RMS_EPS) * nw * _silu(zb)
    return (_per_block(one, o, z),)


def _rw_pre_fn(pr, pk, pv, pwa, pg, w0, a0, k_k, k_a, w2p, a2p, g2):
    log_w = -_softplus(-(w0 + mm(jnp.tanh(pwa), w2p))) - 0.5
    lw = -jnp.exp(log_w)
    a = _sigmoid(a0 + mm(pwa, a2p))
    gate = mm(_sigmoid(pg), g2)
    kk = pk * k_k
    kk = _per_block(lambda x: x / jnp.maximum(jnp.sqrt(_seg_sum(x * x, RW_HEAD)), 1e-12), kk)
    k = pk * (1.0 + (a - 1.0) * k_a)
    return pr, lw, k, pv, kk * a, -kk, gate


def _rw_post_fn(y, r, k, v, gate, ln_w, ln_b, r_k):
    def one(yb, rb, kb, vb, gb, wb, bb, rkb):
        d = yb - _seg_sum(yb, RW_HEAD) * (1.0 / RW_HEAD)
        var = _seg_sum(d * d, RW_HEAD) * (1.0 / RW_HEAD)
        yn = d * lax.rsqrt(var + RW_GN_EPS) * wb + bb
        return (yn + _seg_sum(rb * kb * rkb, RW_HEAD) * vb) * gb
    return (_per_block(one, y, r, k, v, gate, ln_w, ln_b, r_k),)


def _rms_fn(h, w):
    return (h * lax.rsqrt(jnp.mean(h * h, axis=1, keepdims=True) + RMS_EPS) * w,)


def _xattn_fn(q, k, v):
    outs = []
    for h in range(XA_HEADS):
        sl = slice(LANES * h, LANES * (h + 1))
        s = mm(q[:, sl], k[:, sl], "nt") * (LANES ** -0.5)
        e = jnp.exp(s - jnp.max(s, axis=1, keepdims=True))
        outs.append(mm(e / jnp.sum(e, axis=1, keepdims=True), v[:, sl]))
    return (jnp.concatenate(outs, axis=1),)


def _fit(tile, dim):
    best = [t for t in range(LANES, min(tile, dim) + 1, LANES) if dim % t == 0]
    assert best, (tile, dim)
    return best[-1]


def _matmul(name, a, b, mode, out_dtypes, epilogue=None, extras=(), tm=1024, tn=1024, tk=2048):
    if mode == "tn":
        (k_dim, m), n = a.shape, b.shape[1]
    else:
        (m, k_dim), n = a.shape, (b.shape[1] if mode == "nn" else b.shape[0])
    tm, tn, tk = _fit(tm, m), _fit(tn, n), _fit(tk, k_dim)
    nk = k_dim // tk
    a_spec = (pl.BlockSpec((tk, tm), lambda i, j, k: (k, i)) if mode == "tn"
              else pl.BlockSpec((tm, tk), lambda i, j, k: (i, k)))
    b_spec = (pl.BlockSpec((tn, tk), lambda i, j, k: (j, k)) if mode == "nt"
              else pl.BlockSpec((tk, tn), lambda i, j, k: (k, j)))
    o_spec = pl.BlockSpec((tm, tn), lambda i, j, k: (i, j))
    n_ex, n_out = len(extras), len(out_dtypes)

    def finish(total, rest):
        ex = [r[...].astype(F32) for r in rest[:n_ex]]
        res = epilogue(total, *ex) if epilogue else (total,)
        for ref, o in zip(rest[n_ex:n_ex + n_out], res):
            ref[...] = o.astype(ref.dtype)

    def body_single(a_ref, b_ref, *rest):
        finish(_raw_dot(a_ref[...], b_ref[...], mode, False), rest)

    def body_acc(a_ref, b_ref, *rest):
        acc = rest[-1]
        k = pl.program_id(2)

        @pl.when(k == 0)
        def _():
            acc[...] = jnp.zeros_like(acc)

        acc[...] += _raw_dot(a_ref[...], b_ref[...], mode, False)

        @pl.when(k == nk - 1)
        def _():
            finish(acc[...], rest)

    res = pl.pallas_call(
        body_single if nk == 1 else body_acc, grid=(m // tm, n // tn, nk), name=name,
        in_specs=[a_spec, b_spec] + [o_spec] * n_ex,
        out_specs=[o_spec] * n_out,
        out_shape=[SDS((m, n), dt) for dt in out_dtypes],
        scratch_shapes=[] if nk == 1 else [pltpu.VMEM((tm, tn), F32)],
        compiler_params=pltpu.CompilerParams(dimension_semantics=("parallel", "parallel", "arbitrary"),
                                             vmem_limit_bytes=VMEM_LIMIT),
    )(a, b, *extras)
    return res


def _loss_call(h, target, w, tr=256):
    rows, d = h.shape

    def fn(hv, wv, tv):
        y = _rms_fn(hv, wv)[0]
        return 0.5 * jnp.sum(jnp.mean(jnp.square(y - tv), axis=1, keepdims=True), axis=0, keepdims=True)

    def body(h_ref, t_ref, w_ref, loss_ref, dh_ref, dw_ref):
        tv = t_ref[...]
        val, vjp = jax.vjp(lambda hv, wv: fn(hv, wv, tv), h_ref[...], w_ref[...])
        dh, dw = vjp(jnp.ones((1, 1), F32))
        dh_ref[...] = dh
        first = pl.program_id(0) == 0

        @pl.when(first)
        def _():
            loss_ref[...] = jnp.broadcast_to(val, loss_ref.shape)
            dw_ref[...] = dw

        @pl.when(jnp.logical_not(first))
        def _():
            loss_ref[...] += jnp.broadcast_to(val, loss_ref.shape)
            dw_ref[...] += dw

    return pl.pallas_call(
        body, grid=(rows // tr,), name="loss_head",
        in_specs=[_col_spec(tr, d, 0), _col_spec(tr, d, 0), _whole(w)],
        out_specs=[pl.BlockSpec((8, LANES), lambda i: (0, 0)), _col_spec(tr, d, 0), _whole(w)],
        out_shape=[SDS((8, LANES), F32), SDS((rows, d), F32), SDS(w.shape, F32)],
        compiler_params=pltpu.CompilerParams(dimension_semantics=("arbitrary",), vmem_limit_bytes=VMEM_LIMIT),
    )(h, target, w)


def _adamw_vals(w, g, m, v):
    m = ADAM_B1 * m + (1.0 - ADAM_B1) * g
    v = ADAM_B2 * v + (1.0 - ADAM_B2) * jnp.square(g)
    m_hat = m / (1.0 - ADAM_B1 ** ADAM_STEP)
    v_hat = v / (1.0 - ADAM_B2 ** ADAM_STEP)
    delta = -ADAM_LR * (m_hat / (jnp.sqrt(v_hat) + ADAM_EPS) + ADAM_WD * w)
    return delta, m, v


def _sum_adamw(name, parts, w, m, v):
    r, c = w.shape
    tr = r
    for cand in (512, 256, 128, 64, 32, 16, 8):
        if r % cand == 0 and N_DEV * cand * c * 4 <= 6 * 1024 * 1024:
            tr = cand
            break

    def body(p_ref, w_ref, m_ref, v_ref, g_ref, d_ref, m2_ref, v2_ref):
        g = p_ref[0]
        for s in range(1, N_DEV):
            g = g + p_ref[s]
        g_ref[...] = g
        d_ref[...], m2_ref[...], v2_ref[...] = _adamw_vals(w_ref[...], g, m_ref[...], v_ref[...])

    blk = pl.BlockSpec((tr, c), lambda i: (i, 0))
    return pl.pallas_call(
        body, grid=(r // tr,), name=name,
        in_specs=[pl.BlockSpec((N_DEV, tr, c), lambda i: (0, i, 0)), blk, blk, blk],
        out_specs=[blk] * 4, out_shape=[SDS((r, c), F32)] * 4,
        compiler_params=pltpu.CompilerParams(dimension_semantics=("arbitrary",), vmem_limit_bytes=VMEM_LIMIT),
    )(parts, w, m, v)


def _peers():
    x, y, c = lax.axis_index("x"), lax.axis_index("y"), lax.axis_index("c")
    peers = []
    for k in range(1, N_DEV):
        px = 1 - x if k & 4 else x
        py = 1 - y if k & 2 else y
        pc = 1 - c if k & 1 else c
        peers.append(((px, py, pc), 4 * px + 2 * py + pc))
    return 4 * x + 2 * y + c, peers


def _slot(ref, idx, cols):
    if cols is None:
        return ref.at[idx]
    return ref.at[:, pl.ds(pl.multiple_of(idx * cols, LANES), cols)]


def _exchange(name, srcs, dsts, gather):
    n = len(srcs)

    def body(*refs):
        src_refs, out_refs = refs[:n], refs[n:2 * n]
        send_sems, recv_sems, local_sems = refs[2 * n:]
        me, peers = _peers()
        locals_, remotes = [], []
        for a in range(n):
            s_cols, d_cols = srcs[a][1], dsts[a][2]
            mine = src_refs[a] if gather else _slot(src_refs[a], me, s_cols)
            cp = pltpu.make_async_copy(mine, _slot(out_refs[a], me, d_cols), local_sems.at[a])
            cp.start()
            locals_.append(cp)
            for k, (pos, idx) in enumerate(peers):
                out_blk = src_refs[a] if gather else _slot(src_refs[a], idx, s_cols)
                rc = pltpu.make_async_remote_copy(
                    src_ref=out_blk, dst_ref=_slot(out_refs[a], me, d_cols),
                    send_sem=send_sems.at[a, k], recv_sem=recv_sems.at[a, k],
                    device_id=pos, device_id_type=pl.DeviceIdType.MESH)
                rc.start()
                remotes.append((rc, pltpu.make_async_remote_copy(
                    src_ref=out_blk, dst_ref=_slot(out_refs[a], idx, d_cols),
                    send_sem=send_sems.at[a, k], recv_sem=recv_sems.at[a, k],
                    device_id=pos, device_id_type=pl.DeviceIdType.MESH)))
        for rc, landing in remotes:
            landing.wait_recv()
        for rc, landing in remotes:
            rc.wait_send()
        for cp in locals_:
            cp.wait()

    any_spec = pl.BlockSpec(memory_space=pl.ANY)
    return pl.pallas_call(
        body, name=name,
        in_specs=[any_spec] * n, out_specs=[any_spec] * n,
        out_shape=[SDS(shape, dt) for shape, dt, _ in dsts],
        scratch_shapes=[pltpu.SemaphoreType.DMA((n, N_DEV - 1)), pltpu.SemaphoreType.DMA((n, N_DEV - 1)),
                        pltpu.SemaphoreType.DMA((n,))],
    )(*[a for a, _ in srcs])


def _rms_res_fn(h, w):
    return _rms_fn(h, w)[0], h


def _add_epilogue(acc, res):
    return (acc + res,)


def _local_step(x, mem, target, wt):
    d = D_MODEL
    g = {}
    u = _row_fwd(_rms_fn, "mix_norm", [(x, d, 0)], [wt["mix_norm_w"]], [(d, BF16)], 256)[0]
    p = _matmul("in_proj", u, wt["w_in"], "nn", [F32], tn=1536)[0]
    c = _col_fwd(_conv_fn, "dn_conv", p, 0, 24, [wt["dn_conv_w"]])
    dn_pre_tiles = [(c, DN_WIDTH, 0), (c, DN_WIDTH, 1), (p, LANES, 32)]
    dn_pre_params = [wt["dn_a_log"], wt["dn_dt_bias"]]
    qh, kh, gb, bb = _row_fwd(_dn_pre_fn, "dn_pre", dn_pre_tiles, dn_pre_params, [(DN_WIDTH, F32)] * 4, 128)
    dn_arrs = [(qh, 0), (kh, 0), (c, 16), (gb, 0), (bb, 0)]
    o, st_dn = _scan_fwd(_gdn_chunk, "gdn_scan", dn_arrs, DN_HEADS)
    dn_post_tiles = [(o, DN_WIDTH, 0), (p, DN_WIDTH, 3)]
    o_dn = _row_fwd(_dn_post_fn, "dn_post", dn_post_tiles, [wt["dn_norm_w"]], [(DN_WIDTH, BF16)], 256)[0]

    ps = _col_fwd(_lerp_fn, "rw_shift", p, RW_OFF // LANES, 26, [wt["rw_mu"]])
    rw_pre_tiles = [(ps, RW_WIDTH, 0), (ps, RW_WIDTH, 1), (ps, RW_WIDTH, 2), (ps, LANES, 24), (ps, LANES, 25)]
    rw_pre_params = [wt[n] for n in ("rw_w0", "rw_a0", "rw_k_k", "rw_k_a", "rw_w2", "rw_a2", "rw_g2")]
    r, lw, k, v, al, be, gate = _row_fwd(_rw_pre_fn, "rw_pre", rw_pre_tiles, rw_pre_params,
                                         [(RW_WIDTH, F32)] * 7, 128)
    rw_arrs = [(r, 0), (lw, 0), (k, 0), (v, 0), (al, 0), (be, 0)]
    y, st_rw = _scan_fwd(_rw_chunk, "rw_scan", rw_arrs, RW_WIDTH // LANES)
    rw_post_tiles = [(t, RW_WIDTH, 0) for t in (y, r, k, v, gate)]
    rw_post_params = [wt["rw_ln_w"], wt["rw_ln_b"], wt["rw_r_k"]]
    o_rw = _row_fwd(_rw_post_fn, "rw_post", rw_post_tiles, rw_post_params, [(RW_WIDTH, BF16)], 128)[0]
    o_cat = jnp.concatenate([o_dn, o_rw], axis=1)
    h1 = _matmul("out_proj", o_cat, wt["w_out"], "nn", [F32], _add_epilogue, (x,))[0]

    hn = _row_fwd(_rms_fn, "xa_norm", [(h1, d, 0)], [wt["xa_norm_w"]], [(d, BF16)], 256)[0]
    mn = _row_fwd(_rms_fn, "mem_norm", [(mem, d, 0)], [wt["mem_norm_w"]], [(d, BF16)], 256)[0]
    q = _matmul("xa_q", hn, wt["xa_wq"], "nn", [F32])[0]
    kx = _matmul("xa_k", mn, wt["xa_wk"], "nn", [F32])[0]
    vx = _matmul("xa_v", mn, wt["xa_wv"], "nn", [F32])[0]
    ao = _row_fwd(_xattn_fn, "xattn", [(q, XA_WIDTH, 0)], [kx, vx], [(XA_WIDTH, BF16)], 256)[0]
    h2 = _matmul("xa_o", ao, wt["xa_wo"], "nn", [F32], _add_epilogue, (h1,))[0]

    f = _row_fwd(_rms_fn, "ffn_norm", [(h2, d, 0)], [wt["ffn_norm_w"]], [(d, BF16)], 256)[0]
    a, hid = _matmul("ffn_up", f, wt["ffn_w1"], "nn", [F32, BF16],
                     lambda acc: (acc, jnp.square(jnp.maximum(acc, 0.0))))
    h3 = _matmul("ffn_down", hid, wt["ffn_w2"], "nn", [F32], _add_epilogue, (h2,))[0]
    loss8, dh3, g["final_norm_w"] = _loss_call(h3, target, wt["final_norm_w"])

    da = _matmul("ffn_down_dx", dh3, wt["ffn_w2"], "nt", [BF16],
                 lambda acc, av: (acc * 2.0 * jnp.maximum(av, 0.0),), (a,))[0]
    g["ffn_w2"] = _matmul("ffn_down_dw", hid, dh3, "tn", [F32])[0]
    g["ffn_w1"] = _matmul("ffn_up_dw", f, da, "tn", [F32])[0]
    df = _matmul("ffn_up_dx", da, wt["ffn_w1"], "nt", [F32])[0]
    (dh2,), (g["ffn_norm_w"],) = _row_bwd(_rms_res_fn, "ffn_norm_bwd", [(h2, d, 0)], [wt["ffn_norm_w"]],
                                          [[(df, d, 0)], [(dh3, d, 0)]], 256)

    dao = _matmul("xa_o_dx", dh2, wt["xa_wo"], "nt", [F32])[0]
    g["xa_wo"] = _matmul("xa_o_dw", ao, dh2, "tn", [F32])[0]
    (dq,), (dkx, dvx) = _row_bwd(_xattn_fn, "xattn_bwd", [(q, XA_WIDTH, 0)], [kx, vx], [[(dao, XA_WIDTH, 0)]], 256)
    dhn = _matmul("xa_q_dx", dq, wt["xa_wq"], "nt", [F32])[0]
    g["xa_wq"] = _matmul("xa_q_dw", hn, dq, "tn", [F32])[0]
    g["xa_wk"] = _matmul("xa_k_dw", mn, dkx, "tn", [F32])[0]
    g["xa_wv"] = _matmul("xa_v_dw", mn, dvx, "tn", [F32])[0]
    dmn = _matmul("xa_k_dx", dkx, wt["xa_wk"], "nt", [F32])[0]
    dmn = _matmul("xa_v_dx", dvx, wt["xa_wv"], "nt", [F32], _add_epilogue, (dmn,))[0]
    _, (g["mem_norm_w"],) = _row_bwd(_rms_fn, "mem_norm_bwd", [(mem, d, 0)], [wt["mem_norm_w"]],
                                     [[(dmn, d, 0)]], 256, want_tiles=())
    (dh1,), (g["xa_norm_w"],) = _row_bwd(_rms_res_fn, "xa_norm_bwd", [(h1, d, 0)], [wt["xa_norm_w"]],
                                         [[(dhn, d, 0)], [(dh2, d, 0)]], 256)

    do_cat = _matmul("out_proj_dx", dh1, wt["w_out"], "nt", [F32])[0]
    g["w_out"] = _matmul("out_proj_dw", o_cat, dh1, "tn", [F32])[0]

    (dy, dr1, dk1, dv1, dgate), (g["rw_ln_w"], g["rw_ln_b"], g["rw_r_k"]) = _row_bwd(
        _rw_post_fn, "rw_post_bwd", rw_post_tiles, rw_post_params, [[(do_cat, RW_WIDTH, 1)]], 128)
    dr2, dlw, dk2, dv2, dal, dbe = _scan_bwd(_rw_chunk, "rw_scan_bwd", rw_arrs, st_rw, dy, RW_WIDTH // LANES)
    one = lambda t: [(t, RW_WIDTH, 0)]
    two = lambda s, t: [(s, RW_WIDTH, 0), (t, RW_WIDTH, 0)]
    d_ps, rw_pre_grads = _row_bwd(
        _rw_pre_fn, "rw_pre_bwd", rw_pre_tiles, rw_pre_params,
        [two(dr1, dr2), one(dlw), two(dk1, dk2), two(dv1, dv2), one(dal), one(dbe), one(dgate)], 128)
    for n, val in zip(("rw_w0", "rw_a0", "rw_k_k", "rw_k_a", "rw_w2", "rw_a2", "rw_g2"), rw_pre_grads):
        g[n] = val
    dp_rw, (g["rw_mu"],) = _col_bwd(_lerp_fn, "rw_shift_bwd", p, RW_OFF // LANES, 26, [wt["rw_mu"]],
                                    jnp.concatenate(d_ps, axis=1))

    (do, dz), (g["dn_norm_w"],) = _row_bwd(_dn_post_fn, "dn_post_bwd", dn_post_tiles, [wt["dn_norm_w"]],
                                           [[(do_cat, DN_WIDTH, 0)]], 256)
    dqh, dkh, dv_dn, dgb, dbb = _scan_bwd(_gdn_chunk, "gdn_scan_bwd", dn_arrs, st_dn, do, DN_HEADS)
    one = lambda t: [(t, DN_WIDTH, 0)]
    (dcq, dck, dgates), (g["dn_a_log"], g["dn_dt_bias"]) = _row_bwd(
        _dn_pre_fn, "dn_pre_bwd", dn_pre_tiles, dn_pre_params, [one(dqh), one(dkh), one(dgb), one(dbb)], 128)
    dp_qkv, (g["dn_conv_w"],) = _col_bwd(_conv_fn, "dn_conv_bwd", p, 0, 24, [wt["dn_conv_w"]],
                                         jnp.concatenate([dcq, dck, dv_dn], axis=1))
    dp = jnp.concatenate([dp_qkv, dz, dgates, dp_rw, jnp.zeros((x.shape[0], LANES), F32)], axis=1).astype(BF16)
    du = _matmul("in_proj_dx", dp, wt["w_in"], "nt", [F32])[0]
    g["w_in"] = _matmul("in_proj_dw", u, dp, "tn", [F32], tn=1536)[0]
    (dx,), (g["mix_norm_w"],) = _row_bwd(_rms_res_fn, "mix_norm_bwd", [(x, d, 0)], [wt["mix_norm_w"]],
                                         [[(du, d, 0)], [(dh1, d, 0)]], 256)
    return loss8, dx, g


WEIGHTS = ["mix_norm_w", "w_in", "dn_conv_w", "dn_a_log", "dn_dt_bias", "dn_norm_w", "rw_mu", "rw_w0", "rw_w2",
           "rw_a0", "rw_a2", "rw_g2", "rw_k_k", "rw_k_a", "rw_r_k", "rw_ln_w", "rw_ln_b", "w_out", "xa_norm_w",
           "mem_norm_w", "xa_wq", "xa_wk", "xa_wv", "xa_wo", "ffn_norm_w", "ffn_w1", "ffn_w2", "final_norm_w"]
SHARDED = {"w_in": True, "w_out": False, "xa_wq": False, "xa_wk": False, "xa_wv": False, "xa_wo": True,
           "ffn_w1": True, "ffn_w2": False, "dn_conv_w": True, "rw_w2": True, "rw_a2": True, "rw_g2": True}
BF16_PAYLOAD = ("w_in", "w_out", "xa_wq", "xa_wk", "xa_wv", "xa_wo", "ffn_w1", "ffn_w2")
REPLICATED = [n for n in WEIGHTS if n not in SHARDED]
RW_IN_COLS = IN_COLS - DN_COLS


def _layout_weights(fw):
    wt = dict(fw)
    w_in = fw["w_in"]
    rows = w_in.shape[0]
    wt["w_in"] = jnp.concatenate(
        [w_in[:, :DN_COLS], jnp.zeros((rows, RW_OFF - DN_COLS), w_in.dtype), w_in[:, DN_COLS:],
         jnp.zeros((rows, IN_PAD - RW_OFF - RW_IN_COLS), w_in.dtype)], axis=1)
    wt["dn_conv_w"] = jnp.pad(fw["dn_conv_w"], ((0, 4), (0, 0)))
    wt["dn_a_log"] = jnp.pad(fw["dn_a_log"], ((0, 0), (0, LANES - DN_HEADS)))
    wt["dn_dt_bias"] = jnp.pad(fw["dn_dt_bias"], ((0, 0), (0, LANES - DN_HEADS)))
    wt["rw_w2"] = jnp.pad(fw["rw_w2"], ((0, 64), (0, 0)))
    wt["rw_a2"] = jnp.pad(fw["rw_a2"], ((64, 0), (0, 0)))
    return wt


def _logical_grads(g):
    out = dict(g)
    out["w_in"] = jnp.concatenate([g["w_in"][:, :DN_COLS], g["w_in"][:, RW_OFF:RW_OFF + RW_IN_COLS]], axis=1)
    out["dn_conv_w"] = g["dn_conv_w"][:4]
    out["dn_a_log"] = g["dn_a_log"][:, :DN_HEADS]
    out["dn_dt_bias"] = g["dn_dt_bias"][:, :DN_HEADS]
    out["rw_w2"] = g["rw_w2"][:64]
    out["rw_a2"] = g["rw_a2"][64:]
    return out


def _pack(vals):
    parts = []
    for v in vals:
        flat = v.reshape(-1)
        parts.append(jnp.pad(flat, (0, -flat.shape[0] % LANES)))
    flat = jnp.concatenate(parts)
    flat = jnp.pad(flat, (0, -flat.shape[0] % (8 * LANES)))
    return flat.reshape(-1, LANES)


def _unpack(packed, shapes):
    flat = packed.reshape(-1)
    out, at = [], 0
    for shp in shapes:
        size = math.prod(shp)
        out.append(flat[at:at + size].reshape(shp))
        at += size + (-size % LANES)
    return out


def kernel(x, mem, mix_norm_w, w_in, dn_conv_w, dn_a_log, dn_dt_bias, dn_norm_w, rw_mu, rw_w0, rw_w2, rw_a0, rw_a2, rw_g2, rw_k_k, rw_k_a, rw_r_k, rw_ln_w, rw_ln_b, w_out, xa_norm_w, mem_norm_w, xa_wq, xa_wk, xa_wv, xa_wo, ffn_norm_w, ffn_w1, ffn_w2, final_norm_w, loss_target, m_mix_norm_w, m_w_in, m_dn_conv_w, m_dn_a_log, m_dn_dt_bias, m_dn_norm_w, m_rw_mu, m_rw_w0, m_rw_w2, m_rw_a0, m_rw_a2, m_rw_g2, m_rw_k_k, m_rw_k_a, m_rw_r_k, m_rw_ln_w, m_rw_ln_b, m_w_out, m_xa_norm_w, m_mem_norm_w, m_xa_wq, m_xa_wk, m_xa_wv, m_xa_wo, m_ffn_norm_w, m_ffn_w1, m_ffn_w2, m_final_norm_w, v_mix_norm_w, v_w_in, v_dn_conv_w, v_dn_a_log, v_dn_dt_bias, v_dn_norm_w, v_rw_mu, v_rw_w0, v_rw_w2, v_rw_a0, v_rw_a2, v_rw_g2, v_rw_k_k, v_rw_k_a, v_rw_r_k, v_rw_ln_w, v_rw_ln_b, v_w_out, v_xa_norm_w, v_mem_norm_w, v_xa_wq, v_xa_wk, v_xa_wv, v_xa_wo, v_ffn_norm_w, v_ffn_w1, v_ffn_w2, v_final_norm_w):
    given = dict(locals())
    w = {n: given[n] for n in WEIGHTS}
    m = {n: given["m_" + n] for n in WEIGHTS}
    v = {n: given["v_" + n] for n in WEIGHTS}

    srcs, dsts = [], []
    for n, by_cols in SHARDED.items():
        sh = w[n][0].astype(BF16) if n in BF16_PAYLOAD else w[n][0]
        r, c = sh.shape
        srcs.append((sh, None))
        if by_cols and c % LANES == 0:
            dsts.append(((r, N_DEV * c), sh.dtype, c))
        else:
            dsts.append(((N_DEV, r, c), sh.dtype, None))
    full = {}
    for n, arr in zip(SHARDED, _exchange("weight_all_gather", srcs, dsts, True)):
        if arr.ndim == 2:
            full[n] = arr
        elif SHARDED[n]:
            full[n] = arr.transpose(1, 0, 2).reshape(arr.shape[1], -1)
        else:
            full[n] = arr.reshape(-1, arr.shape[2])
    for n in REPLICATED:
        full[n] = w[n].reshape(1, -1)

    loss8, dx, g = _local_step(x[0], mem[0], loss_target[0], _layout_weights(full))
    g = _logical_grads(g)
    loss = lax.psum(loss8[0, 0], ("x", "y", "c"))

    srcs, dsts = [], []
    for n, by_cols in SHARDED.items():
        r, c = w[n].shape[1:]
        if by_cols and c % LANES == 0:
            srcs.append((g[n], c))
        elif by_cols:
            srcs.append((g[n].reshape(r, N_DEV, c).transpose(1, 0, 2), None))
        else:
            srcs.append((g[n].reshape(N_DEV, r, c), None))
        dsts.append(((N_DEV, r, c), F32, None))
    grad, delta, new_m, new_v = {}, {}, {}, {}
    for n, parts in zip(SHARDED, _exchange("grad_all_to_all", srcs, dsts, False)):
        res = _sum_adamw("adamw_" + n, parts, w[n][0], m[n][0], v[n][0])
        grad[n], delta[n], new_m[n], new_v[n] = [t[None] for t in res]

    packed = _pack([g[n] for n in REPLICATED])
    parts = _exchange("small_all_gather", [(packed, None)], [((N_DEV,) + packed.shape, F32, None)], True)[0]
    res = _sum_adamw("adamw_small", parts, _pack([w[n] for n in REPLICATED]),
                     _pack([m[n] for n in REPLICATED]), _pack([v[n] for n in REPLICATED]))
    shapes = [w[n].shape for n in REPLICATED]
    for store, packed_out in zip((grad, delta, new_m, new_v), res):
        for n, val in zip(REPLICATED, _unpack(packed_out, shapes)):
            store[n] = val

    return (loss, dx[None], *[grad[n] for n in WEIGHTS], *[delta[n] for n in WEIGHTS],
            *[new_m[n] for n in WEIGHTS], *[new_v[n] for n in WEIGHTS])
```

```python
import functools
import math

import jax
import jax.numpy as jnp
from jax import lax
from jax.experimental import pallas as pl
from jax.experimental.pallas import tpu as pltpu

F32 = jnp.float32
BF16 = jnp.bfloat16
SDS = jax.ShapeDtypeStruct

N_DEV = 8
D_MODEL = 2048
LANES = 128
CHUNK = 128
DN_HEADS = 8
DN_WIDTH = 1024
RW_WIDTH = 1024
RW_HEAD = 64
XA_HEADS = 4
XA_WIDTH = 512
FFN_HIDDEN = 8192
IN_COLS = 7440
DN_COLS = 4112
IN_PAD = 7680
RW_OFF = 4224
RMS_EPS = 1e-6
RW_GN_EPS = 64e-5
VMEM_LIMIT = 56 * 1024 * 1024

ADAM_LR = 0.001
ADAM_B1 = 0.9
ADAM_B2 = 0.999
ADAM_EPS = 1e-08
ADAM_WD = 0.01
ADAM_STEP = 10

_DIMS = {"nn": (((1,), (0,)), ((), ())), "nt": (((1,), (1,)), ((), ())), "tn": (((0,), (0,)), ((), ()))}


def _raw_dot(a, b, mode, hi):
    if hi:
        return lax.dot_general(a, b, _DIMS[mode], precision=lax.Precision.HIGHEST, preferred_element_type=F32)
    return lax.dot_general(a.astype(BF16), b.astype(BF16), _DIMS[mode], preferred_element_type=F32)


@functools.partial(jax.custom_vjp, nondiff_argnums=(2, 3))
def mm(a, b, mode="nn", hi=False):
    return _raw_dot(a, b, mode, hi)


def _mm_fwd(a, b, mode, hi):
    return _raw_dot(a, b, mode, hi), (a, b)


def _mm_bwd(mode, hi, res, g):
    a, b = res
    if mode == "nn":
        return _raw_dot(g, b, "nt", hi), _raw_dot(a, g, "tn", hi)
    if mode == "nt":
        return _raw_dot(g, b, "nn", hi), _raw_dot(g, a, "tn", hi)
    return _raw_dot(b, g, "nt", hi), _raw_dot(a, g, "nn", hi)


mm.defvjp(_mm_fwd, _mm_bwd)


def _shift_rows_raw(x, k):
    n = x.shape[0]
    rolled = pltpu.roll(x, k % n, axis=0)
    row = lax.broadcasted_iota(jnp.int32, x.shape, 0)
    keep = row >= k if k > 0 else row < n + k
    return jnp.where(keep, rolled, 0.0)


@functools.partial(jax.custom_vjp, nondiff_argnums=(1,))
def shift_rows(x, k):
    return _shift_rows_raw(x, k)


shift_rows.defvjp(lambda x, k: (_shift_rows_raw(x, k), None), lambda k, _, g: (_shift_rows_raw(g, -k),))


def _softplus(x):
    return jnp.maximum(x, 0.0) + jnp.log(1.0 + jnp.exp(-jnp.abs(x)))


def _sigmoid(x):
    return 1.0 / (1.0 + jnp.exp(-x))


def _silu(x):
    return x * _sigmoid(x)


def _tri_masks(n):
    ii = lax.broadcasted_iota(jnp.int32, (n, n), 0)
    jj = lax.broadcasted_iota(jnp.int32, (n, n), 1)
    return ii >= jj, ii > jj, ii == jj


def _neumann_inv_raw(m):
    n = m.shape[0]
    _, _, eye = _tri_masks(n)
    eye = jnp.where(eye, 1.0, 0.0)
    p = eye + m
    mk = m
    for _ in range(int(math.log2(n)) - 1):
        mk = _raw_dot(mk, mk, "nn", False)
        p = p + _raw_dot(p, mk, "nn", False)
    resid = eye - p + _raw_dot(m, p, "nn", True)
    return p + _raw_dot(p, resid, "nn", False)


@jax.custom_vjp
def _neumann_inv(m):
    return _neumann_inv_raw(m)


def _neumann_inv_fwd(m):
    p = _neumann_inv_raw(m)
    return p, p


def _neumann_inv_bwd(p, g):
    return (_raw_dot(_raw_dot(p, g, "tn", False), p, "nt", False),)


_neumann_inv.defvjp(_neumann_inv_fwd, _neumann_inv_bwd)


def _gdn_chunk(s0, q, k, v, gb, bb):
    c = q.shape[0]
    causal, strict, _ = _tri_masks(c)
    gc = mm(jnp.where(causal, 1.0, 0.0), gb, "nn", True)
    diff = gc - gc.T
    decay = jnp.exp(jnp.where(causal, diff, -jnp.inf))
    kb = k * bb
    a = jnp.where(strict, mm(kb, k, "nt") * decay, 0.0)
    p = _neumann_inv(-a)
    u = mm(p, v * bb)
    w = mm(p, kb * jnp.exp(gc))
    attn = mm(q, k, "nt") * decay
    v_new = u - mm(w, s0)
    o = mm(q * jnp.exp(gc), s0) + mm(attn, v_new)
    g_last = jnp.sum(gb, axis=0, keepdims=True)
    s1 = s0 * jnp.exp(g_last) + mm(k * jnp.exp(g_last - gc), v_new, "tn")
    return o, s1


def _rw_chunk(s0, r, lw, k, v, al, be):
    c = r.shape[0]
    causal, strict, _ = _tri_masks(c)
    gc = mm(jnp.where(causal, 1.0, 0.0), lw, "nn", True)
    gp = gc - lw
    row = lax.broadcasted_iota(jnp.int32, lw.shape, 0)
    lane = lax.broadcasted_iota(jnp.int32, lw.shape, 1)
    g_mid = jnp.sum(jnp.where(row < c // 2, lw, 0.0), axis=0, keepdims=True)
    g_last = jnp.sum(lw, axis=0, keepdims=True)
    e_n = jnp.exp(g_mid - gc)
    rg = r * jnp.exp(gc - g_mid)
    bg = be * jnp.exp(gp - g_mid)
    an = al * e_n
    kn = k * e_n
    bt = mm(be * jnp.exp(gp), s0, "nt")
    rt = mm(r * jnp.exp(gc), s0, "nt")
    us, ys = [], []
    for h in range(2):
        mine = (lane >= RW_HEAD) if h else (lane < RW_HEAD)
        bgh = jnp.where(mine, bg, 0.0)
        rgh = jnp.where(mine, rg, 0.0)
        a_ab = jnp.where(strict, mm(bgh, an, "nt"), 0.0)
        a_kb = jnp.where(strict, mm(bgh, kn, "nt"), 0.0)
        a_ra = jnp.where(causal, mm(rgh, an, "nt"), 0.0)
        a_rk = jnp.where(causal, mm(rgh, kn, "nt"), 0.0)
        p = _neumann_inv(a_ab)
        u_h = mm(p, bt + mm(a_kb, v))
        us.append(u_h)
        ys.append(rt + mm(a_ra, u_h) + mm(a_rk, v))
    lo = lane < RW_HEAD
    u = jnp.where(lo, us[0], us[1])
    y = jnp.where(lo, ys[0], ys[1])
    tail = jnp.exp(g_last - gc)
    s1 = s0 * jnp.exp(g_last) + mm(u, al * tail, "tn") + mm(v, k * tail, "tn")
    vi = lax.broadcasted_iota(jnp.int32, s0.shape, 0)
    ki = lax.broadcasted_iota(jnp.int32, s0.shape, 1)
    s1 = jnp.where((vi < RW_HEAD) == (ki < RW_HEAD), s1, 0.0)
    return y, s1


def _scan_specs(arrs, n_chunks, reverse):
    def spec(off):
        if reverse:
            return pl.BlockSpec((CHUNK, LANES), lambda h, n: (n_chunks - 1 - n, off + h))
        return pl.BlockSpec((CHUNK, LANES), lambda h, n: (n, off + h))
    return [spec(off) for _, off in arrs]


def _hosted(xfer, heads, n_chunks):
    if xfer is None:
        return 0, [], [], [], [], lambda refs: None
    srcs, dsts, gather = xfer
    n = len(srcs)
    any_spec = pl.BlockSpec(memory_space=pl.ANY)

    def run(src_refs, out_refs, sems):
        start, wait = _exchange_ops([c for _, c in srcs], [c for _, _, c in dsts], gather, src_refs, out_refs, *sems)
        h, c = pl.program_id(0), pl.program_id(1)
        pl.when(jnp.logical_and(h == 0, c == 0))(start)
        pl.when(jnp.logical_and(h == heads - 1, c == n_chunks - 1))(wait)

    return (n, [any_spec] * n, [any_spec] * n, [SDS(shape, dt) for shape, dt, _ in dsts],
            _exchange_sems(n), run)


def _scan_fwd(chunk_fn, name, arrs, heads, xfer=None):
    s = arrs[0][0].shape[0]
    n_chunks = s // CHUNK
    n_in = len(arrs)
    n_x, x_in, x_out, x_shapes, x_sems, run_x = _hosted(xfer, heads, n_chunks)

    def body(*refs):
        y_ref, st_ref = refs[n_in + n_x:n_in + n_x + 2]
        s_scr = refs[n_in + 2 * n_x + 2]
        if n_x:
            run_x(refs[n_in:n_in + n_x], refs[n_in + n_x + 2:n_in + 2 * n_x + 2], refs[n_in + 2 * n_x + 3:])

        @pl.when(pl.program_id(1) == 0)
        def _():
            s_scr[...] = jnp.zeros_like(s_scr)

        s0 = s_scr[...]
        st_ref[...] = s0
        y, s1 = chunk_fn(s0, *[r[...] for r in refs[:n_in]])
        y_ref[...] = y
        s_scr[...] = s1

    res = pl.pallas_call(
        body, grid=(heads, n_chunks), name=name,
        in_specs=_scan_specs(arrs, n_chunks, False) + x_in,
        out_specs=[pl.BlockSpec((CHUNK, LANES), lambda h, n: (n, h)),
                   pl.BlockSpec((None, None, LANES, LANES), lambda h, n: (h, n, 0, 0))] + x_out,
        out_shape=[SDS((s, heads * LANES), F32), SDS((heads, n_chunks, LANES, LANES), F32)] + x_shapes,
        scratch_shapes=[pltpu.VMEM((LANES, LANES), F32)] + x_sems,
        compiler_params=pltpu.CompilerParams(dimension_semantics=("arbitrary", "arbitrary")),
    )(*[a for a, _ in arrs], *([a for a, _ in xfer[0]] if xfer else []))
    return res[0], res[1], res[2:]


def _scan_bwd(chunk_fn, name, arrs, states, dy, heads, xfer=None):
    s = arrs[0][0].shape[0]
    n_chunks = s // CHUNK
    n_in = len(arrs)
    n_x, x_in, x_out, x_shapes, x_sems, run_x = _hosted(xfer, heads, n_chunks)

    def body(*refs):
        st_ref, dy_ref = refs[n_in:n_in + 2]
        first_out = n_in + 2 + n_x
        d_refs = refs[first_out:first_out + n_in]
        ds_scr = refs[first_out + n_in + n_x]
        if n_x:
            run_x(refs[n_in + 2:first_out], refs[first_out + n_in:first_out + n_in + n_x],
                  refs[first_out + n_in + n_x + 1:])

        @pl.when(pl.program_id(1) == 0)
        def _():
            ds_scr[...] = jnp.zeros_like(ds_scr)

        _, vjp = jax.vjp(chunk_fn, st_ref[...], *[r[...] for r in refs[:n_in]])
        grads = vjp((dy_ref[...], ds_scr[...]))
        ds_scr[...] = grads[0]
        for ref, g in zip(d_refs, grads[1:]):
            ref[...] = g

    rev = pl.BlockSpec((CHUNK, LANES), lambda h, n: (n_chunks - 1 - n, h))
    res = pl.pallas_call(
        body, grid=(heads, n_chunks), name=name,
        in_specs=_scan_specs(arrs, n_chunks, True)
        + [pl.BlockSpec((None, None, LANES, LANES), lambda h, n: (h, n_chunks - 1 - n, 0, 0)), rev] + x_in,
        out_specs=[rev] * n_in + x_out,
        out_shape=[SDS((s, heads * LANES), F32)] * n_in + x_shapes,
        scratch_shapes=[pltpu.VMEM((LANES, LANES), F32)] + x_sems,
        compiler_params=pltpu.CompilerParams(dimension_semantics=("arbitrary", "arbitrary")),
    )(*[a for a, _ in arrs], states, dy, *([a for a, _ in xfer[0]] if xfer else []))
    return res[:n_in], res[n_in:]


def _col_spec(tr, width, cb):
    return pl.BlockSpec((tr, width), lambda i: (i, cb))


def _whole(p):
    return pl.BlockSpec(p.shape, lambda i: (0,) * p.ndim)


def _row_fwd(fn, name, tiles, params, outs, tr):
    rows = tiles[0][0].shape[0]
    nt, npar = len(tiles), len(params)

    def body(*refs):
        vals = [r[...].astype(F32) for r in refs[:nt + npar]]
        for ref, o in zip(refs[nt + npar:], fn(*vals)):
            ref[...] = o.astype(ref.dtype)

    return pl.pallas_call(
        body, grid=(rows // tr,), name=name,
        in_specs=[_col_spec(tr, w, cb) for _, w, cb in tiles] + [_whole(p) for p in params],
        out_specs=[_col_spec(tr, w, 0) for w, _ in outs],
        out_shape=[SDS((rows, w), dt) for w, dt in outs],
        compiler_params=pltpu.CompilerParams(dimension_semantics=("arbitrary",), vmem_limit_bytes=VMEM_LIMIT),
    )(*[a for a, _, _ in tiles], *params)


def _row_bwd(fn, name, tiles, params, cts, tr, want_tiles=None):
    rows = tiles[0][0].shape[0]
    nt, npar = len(tiles), len(params)
    want = list(range(nt)) if want_tiles is None else list(want_tiles)
    flat_cts = [c for group in cts for c in group]
    n_ct = len(flat_cts)

    def body(*refs):
        vals = [r[...].astype(F32) for r in refs[:nt + npar]]
        ct_refs = refs[nt + npar:nt + npar + n_ct]
        out_refs = refs[nt + npar + n_ct:]
        ct_vals, at = [], 0
        for group in cts:
            total = ct_refs[at][...].astype(F32)
            for r in ct_refs[at + 1:at + len(group)]:
                total = total + r[...].astype(F32)
            ct_vals.append(total)
            at += len(group)
        _, vjp = jax.vjp(lambda *a: tuple(fn(*a)), *vals)
        grads = vjp(tuple(ct_vals))
        for ref, t in zip(out_refs[:len(want)], want):
            ref[...] = grads[t]
        first = pl.program_id(0) == 0
        for ref, g in zip(out_refs[len(want):], grads[nt:]):
            @pl.when(first)
            def _(ref=ref, g=g):
                ref[...] = g

            @pl.when(jnp.logical_not(first))
            def _(ref=ref, g=g):
                ref[...] += g

    res = pl.pallas_call(
        body, grid=(rows // tr,), name=name,
        in_specs=[_col_spec(tr, w, cb) for _, w, cb in tiles] + [_whole(p) for p in params]
        + [_col_spec(tr, w, cb) for _, w, cb in flat_cts],
        out_specs=[_col_spec(tr, tiles[t][1], 0) for t in want] + [_whole(p) for p in params],
        out_shape=[SDS((rows, tiles[t][1]), F32) for t in want] + [SDS(p.shape, F32) for p in params],
        compiler_params=pltpu.CompilerParams(dimension_semantics=("arbitrary",), vmem_limit_bytes=VMEM_LIMIT),
    )(*[a for a, _, _ in tiles], *params, *[a for a, _, _ in flat_cts])
    return res[:len(want)], res[len(want):]


def _col_fwd(fn, name, x, first_block, n_blocks, params):
    rows = x.shape[0]

    def body(*refs):
        refs[-1][...] = fn(*[r[...] for r in refs[:-1]])

    return pl.pallas_call(
        body, grid=(n_blocks,), name=name,
        in_specs=[pl.BlockSpec((rows, LANES), lambda j: (0, first_block + j))]
        + [pl.BlockSpec((p.shape[0], LANES), lambda j: (0, j)) for p in params],
        out_specs=pl.BlockSpec((rows, LANES), lambda j: (0, j)),
        out_shape=SDS((rows, n_blocks * LANES), F32),
        compiler_params=pltpu.CompilerParams(dimension_semantics=("arbitrary",), vmem_limit_bytes=VMEM_LIMIT),
    )(x, *params)


def _col_bwd(fn, name, x, first_block, n_blocks, params, dy):
    rows = x.shape[0]
    npar = len(params)

    def body(*refs):
        vals = [r[...] for r in refs[:1 + npar]]
        _, vjp = jax.vjp(fn, *vals)
        grads = vjp(refs[1 + npar][...])
        for ref, g in zip(refs[2 + npar:], grads):
            ref[...] = g

    pspecs = [pl.BlockSpec((p.shape[0], LANES), lambda j: (0, j)) for p in params]
    blk = pl.BlockSpec((rows, LANES), lambda j: (0, j))
    res = pl.pallas_call(
        body, grid=(n_blocks,), name=name,
        in_specs=[pl.BlockSpec((rows, LANES), lambda j: (0, first_block + j))] + pspecs + [blk],
        out_specs=[blk] + pspecs,
        out_shape=[SDS((rows, n_blocks * LANES), F32)] + [SDS(p.shape, F32) for p in params],
        compiler_params=pltpu.CompilerParams(dimension_semantics=("arbitrary",), vmem_limit_bytes=VMEM_LIMIT),
    )(x, *params, dy)
    return res[0], res[1:]


def _conv_fn(x, w):
    acc = x * w[3:4, :]
    for j in range(3):
        acc = acc + shift_rows(x, 3 - j) * w[j:j + 1, :]
    return _silu(acc)


def _lerp_fn(x, mu):
    return x + (shift_rows(x, 1) - x) * mu[0:1, :]


def _seg_sum(x, width):
    if width == LANES:
        return jnp.sum(x, axis=1, keepdims=True)
    lo = lax.broadcasted_iota(jnp.int32, x.shape, 1) < width
    s0 = jnp.sum(jnp.where(lo, x, 0.0), axis=1, keepdims=True)
    s1 = jnp.sum(jnp.where(lo, 0.0, x), axis=1, keepdims=True)
    return jnp.where(lo, s0, s1)


def _per_block(fn, *xs):
    n = xs[0].shape[1] // LANES
    return jnp.concatenate([fn(*[x[:, LANES * b:LANES * (b + 1)] for x in xs]) for b in range(n)], axis=1)


def _head_expand(col0):
    r = lax.broadcasted_iota(jnp.int32, (LANES, DN_WIDTH), 0)
    c = lax.shift_right_logical(lax.broadcasted_iota(jnp.int32, (LANES, DN_WIDTH), 1), 7)
    return jnp.where(r == c + col0, 1.0, 0.0)


def _dn_pre_fn(cq, ck, gates, a_log, dt_bias):
    l2 = lambda x: x * lax.rsqrt(_seg_sum(x * x, LANES) + 1e-6)
    qh = _per_block(l2, cq) * (LANES ** -0.5)
    kh = _per_block(l2, ck)
    g = -jnp.exp(a_log) * _softplus(gates + dt_bias)
    gb = mm(g, _head_expand(0), "nn", True)
    bb = mm(_sigmoid(gates), _head_expand(DN_HEADS), "nn", True)
    return qh, kh, gb, bb


def _dn_post_fn(o, z, nw):
    def one(ob, zb):
        return ob * lax.rsqrt(_seg_sum(ob * ob, LANES) * (1.0 / LANES) + RMS_EPS) * nw * _silu(zb)
    return (_per_block(one, o, z),)


def _rw_pre_fn(pr, pk, pv, pwa, pg, w0, a0, k_k, k_a, w2p, a2p, g2):
    log_w = -_softplus(-(w0 + mm(jnp.tanh(pwa), w2p))) - 0.5
    lw = -jnp.exp(log_w)
    a = _sigmoid(a0 + mm(pwa, a2p))
    gate = mm(_sigmoid(pg), g2)
    kk = pk * k_k
    kk = _per_block(lambda x: x / jnp.maximum(jnp.sqrt(_seg_sum(x * x, RW_HEAD)), 1e-12), kk)
    k = pk * (1.0 + (a - 1.0) * k_a)
    return pr, lw, k, pv, kk * a, -kk, gate


def _rw_post_fn(y, r, k, v, gate, ln_w, ln_b, r_k):
    def one(yb, rb, kb, vb, gb, wb, bb, rkb):
        d = yb - _seg_sum(yb, RW_HEAD) * (1.0 / RW_HEAD)
        var = _seg_sum(d * d, RW_HEAD) * (1.0 / RW_HEAD)
        yn = d * lax.rsqrt(var + RW_GN_EPS) * wb + bb
        return (yn + _seg_sum(rb * kb * rkb, RW_HEAD) * vb) * gb
    return (_per_block(one, y, r, k, v, gate, ln_w, ln_b, r_k),)


def _rms_fn(h, w):
    return (h * lax.rsqrt(jnp.mean(h * h, axis=1, keepdims=True) + RMS_EPS) * w,)


def _xattn_fn(q, k, v):
    outs = []
    for h in range(XA_HEADS):
        sl = slice(LANES * h, LANES * (h + 1))
        s = mm(q[:, sl], k[:, sl], "nt") * (LANES ** -0.5)
        e = jnp.exp(s - jnp.max(s, axis=1, keepdims=True))
        outs.append(mm(e / jnp.sum(e, axis=1, keepdims=True), v[:, sl]))
    return (jnp.concatenate(outs, axis=1),)


def _fit(tile, dim):
    best = [t for t in range(LANES, min(tile, dim) + 1, LANES) if dim % t == 0]
    assert best, (tile, dim)
    return best[-1]


def _matmul(name, a, b, mode, out_dtypes, epilogue=None, extras=(), tm=1024, tn=1024, tk=2048):
    if mode == "tn":
        (k_dim, m), n = a.shape, b.shape[1]
    else:
        (m, k_dim), n = a.shape, (b.shape[1] if mode == "nn" else b.shape[0])
    tm, tn, tk = _fit(tm, m), _fit(tn, n), _fit(tk, k_dim)
    nk = k_dim // tk
    a_spec = (pl.BlockSpec((tk, tm), lambda i, j, k: (k, i)) if mode == "tn"
              else pl.BlockSpec((tm, tk), lambda i, j, k: (i, k)))
    b_spec = (pl.BlockSpec((tn, tk), lambda i, j, k: (j, k)) if mode == "nt"
              else pl.BlockSpec((tk, tn), lambda i, j, k: (k, j)))
    o_spec = pl.BlockSpec((tm, tn), lambda i, j, k: (i, j))
    n_ex, n_out = len(extras), len(out_dtypes)

    def finish(total, rest):
        ex = [r[...].astype(F32) for r in rest[:n_ex]]
        res = epilogue(total, *ex) if epilogue else (total,)
        for ref, o in zip(rest[n_ex:n_ex + n_out], res):
            ref[...] = o.astype(ref.dtype)

    def body_single(a_ref, b_ref, *rest):
        finish(_raw_dot(a_ref[...], b_ref[...], mode, False), rest)

    def body_acc(a_ref, b_ref, *rest):
        acc = rest[-1]
        k = pl.program_id(2)

        @pl.when(k == 0)
        def _():
            acc[...] = jnp.zeros_like(acc)

        acc[...] += _raw_dot(a_ref[...], b_ref[...], mode, False)

        @pl.when(k == nk - 1)
        def _():
            finish(acc[...], rest)

    res = pl.pallas_call(
        body_single if nk == 1 else body_acc, grid=(m // tm, n // tn, nk), name=name,
        in_specs=[a_spec, b_spec] + [o_spec] * n_ex,
        out_specs=[o_spec] * n_out,
        out_shape=[SDS((m, n), dt) for dt in out_dtypes],
        scratch_shapes=[] if nk == 1 else [pltpu.VMEM((tm, tn), F32)],
        compiler_params=pltpu.CompilerParams(dimension_semantics=("parallel", "parallel", "arbitrary"),
                                             vmem_limit_bytes=VMEM_LIMIT),
    )(a, b, *extras)
    return res


def _loss_call(h, target, w, tr=256):
    rows, d = h.shape

    def fn(hv, wv, tv):
        y = _rms_fn(hv, wv)[0]
        return 0.5 * jnp.sum(jnp.mean(jnp.square(y - tv), axis=1, keepdims=True), axis=0, keepdims=True)

    def body(h_ref, t_ref, w_ref, loss_ref, dh_ref, dw_ref):
        tv = t_ref[...]
        val, vjp = jax.vjp(lambda hv, wv: fn(hv, wv, tv), h_ref[...], w_ref[...])
        dh, dw = vjp(jnp.ones((1, 1), F32))
        dh_ref[...] = dh
        first = pl.program_id(0) == 0

        @pl.when(first)
        def _():
            loss_ref[...] = jnp.broadcast_to(val, loss_ref.shape)
            dw_ref[...] = dw

        @pl.when(jnp.logical_not(first))
        def _():
            loss_ref[...] += jnp.broadcast_to(val, loss_ref.shape)
            dw_ref[...] += dw

    return pl.pallas_call(
        body, grid=(rows // tr,), name="loss_head",
        in_specs=[_col_spec(tr, d, 0), _col_spec(tr, d, 0), _whole(w)],
        out_specs=[pl.BlockSpec((8, LANES), lambda i: (0, 0)), _col_spec(tr, d, 0), _whole(w)],
        out_shape=[SDS((8, LANES), F32), SDS((rows, d), F32), SDS(w.shape, F32)],
        compiler_params=pltpu.CompilerParams(dimension_semantics=("arbitrary",), vmem_limit_bytes=VMEM_LIMIT),
    )(h, target, w)


def _adamw_vals(w, g, m, v):
    m = ADAM_B1 * m + (1.0 - ADAM_B1) * g
    v = ADAM_B2 * v + (1.0 - ADAM_B2) * jnp.square(g)
    m_hat = m / (1.0 - ADAM_B1 ** ADAM_STEP)
    v_hat = v / (1.0 - ADAM_B2 ** ADAM_STEP)
    delta = -ADAM_LR * (m_hat / (jnp.sqrt(v_hat) + ADAM_EPS) + ADAM_WD * w)
    return delta, m, v


def _sum_adamw(name, parts, w, m, v):
    r, c = w.shape
    tr = r
    for cand in (512, 256, 128, 64, 32, 16, 8):
        if r % cand == 0 and N_DEV * cand * c * 4 <= 6 * 1024 * 1024:
            tr = cand
            break

    def body(p_ref, w_ref, m_ref, v_ref, g_ref, d_ref, m2_ref, v2_ref):
        g = p_ref[0].astype(F32)
        for s in range(1, N_DEV):
            g = g + p_ref[s].astype(F32)
        g_ref[...] = g
        d_ref[...], m2_ref[...], v2_ref[...] = _adamw_vals(w_ref[...], g, m_ref[...], v_ref[...])

    blk = pl.BlockSpec((tr, c), lambda i: (i, 0))
    return pl.pallas_call(
        body, grid=(r // tr,), name=name,
        in_specs=[pl.BlockSpec((N_DEV, tr, c), lambda i: (0, i, 0)), blk, blk, blk],
        out_specs=[blk] * 4, out_shape=[SDS((r, c), F32)] * 4,
        compiler_params=pltpu.CompilerParams(dimension_semantics=("arbitrary",), vmem_limit_bytes=VMEM_LIMIT),
    )(parts, w, m, v)


def _peers():
    x, y, c = lax.axis_index("x"), lax.axis_index("y"), lax.axis_index("c")
    peers = []
    for k in range(1, N_DEV):
        px = 1 - x if k & 4 else x
        py = 1 - y if k & 2 else y
        pc = 1 - c if k & 1 else c
        peers.append(((px, py, pc), 4 * px + 2 * py + pc))
    return 4 * x + 2 * y + c, peers


def _slot(ref, idx, cols):
    if cols is None:
        return ref.at[idx]
    return ref.at[:, pl.ds(pl.multiple_of(idx * cols, LANES), cols)]


def _exchange(name, srcs, dsts, gather):
    n = len(srcs)

    def body(*refs):
        start, wait = _exchange_ops([c for _, c in srcs], [c for _, _, c in dsts], gather,
                                    refs[:n], refs[n:2 * n], *refs[2 * n:])
        start()
        wait()

    any_spec = pl.BlockSpec(memory_space=pl.ANY)
    return pl.pallas_call(
        body, name=name,
        in_specs=[any_spec] * n, out_specs=[any_spec] * n,
        out_shape=[SDS(shape, dt) for shape, dt, _ in dsts],
        scratch_shapes=_exchange_sems(n),
    )(*[a for a, _ in srcs])


def _exchange_sems(n):
    return [pltpu.SemaphoreType.DMA((n, N_DEV - 1)), pltpu.SemaphoreType.DMA((n, N_DEV - 1)),
            pltpu.SemaphoreType.DMA((n,))]


def _exchange_ops(src_cols, dst_cols, gather, src_refs, out_refs, send_sems, recv_sems, local_sems):
    def copies(with_landings):
        me, peers = _peers()
        local, sends, landings = [], [], []
        for a, (s_cols, d_cols) in enumerate(zip(src_cols, dst_cols)):
            mine = src_refs[a] if gather else _slot(src_refs[a], me, s_cols)
            local.append(pltpu.make_async_copy(mine, _slot(out_refs[a], me, d_cols), local_sems.at[a]))
            for k, (pos, idx) in enumerate(peers):
                out_blk = src_refs[a] if gather else _slot(src_refs[a], idx, s_cols)
                both = dict(src_ref=out_blk, send_sem=send_sems.at[a, k], recv_sem=recv_sems.at[a, k],
                            device_id=pos, device_id_type=pl.DeviceIdType.MESH)
                sends.append(pltpu.make_async_remote_copy(dst_ref=_slot(out_refs[a], me, d_cols), **both))
                if with_landings:
                    landings.append(pltpu.make_async_remote_copy(dst_ref=_slot(out_refs[a], idx, d_cols), **both))
        return local, sends, landings

    def start():
        local, sends, _ = copies(False)
        for cp in local + sends:
            cp.start()

    def wait():
        local, sends, landings = copies(True)
        for cp in landings:
            cp.wait_recv()
        for cp in sends:
            cp.wait_send()
        for cp in local:
            cp.wait()

    return start, wait


def _rms_res_fn(h, w):
    return _rms_fn(h, w)[0], h


def _add_epilogue(acc, res):
    return (acc + res,)


def _gather_plan(shards):
    srcs, dsts = [], []
    for n, sh in shards.items():
        r, c = sh.shape
        srcs.append((sh, None))
        if SHARDED[n] and c % LANES == 0:
            dsts.append(((r, N_DEV * c), sh.dtype, c))
        else:
            dsts.append(((N_DEV, r, c), sh.dtype, None))
    return srcs, dsts, True


def _gather_finish(names, outs):
    full = {}
    for n, arr in zip(names, outs):
        if arr.ndim == 2:
            full[n] = arr
        elif SHARDED[n]:
            full[n] = arr.transpose(1, 0, 2).reshape(arr.shape[1], -1)
        else:
            full[n] = arr.reshape(-1, arr.shape[2])
    return full


def _scatter_plan(grads):
    srcs, dsts = [], []
    for n, gr in grads.items():
        rows, cols = gr.shape
        if not SHARDED[n]:
            r, c = rows // N_DEV, cols
            srcs.append((gr.reshape(N_DEV, r, c), None))
        else:
            r, c = rows, cols // N_DEV
            if c % LANES == 0:
                srcs.append((gr, c))
            else:
                srcs.append((gr.reshape(r, N_DEV, c).transpose(1, 0, 2), None))
        dsts.append(((N_DEV, r, c), gr.dtype, None))
    return srcs, dsts, False


def _local_step(x, mem, target, wt, late):
    d = D_MODEL
    g = {}
    wt = dict(wt)
    u = _row_fwd(_rms_fn, "mix_norm", [(x, d, 0)], [wt["mix_norm_w"]], [(d, BF16)], 256)[0]
    p = _matmul("in_proj", u, wt["w_in"], "nn", [F32], tn=1536)[0]
    c = _col_fwd(_conv_fn, "dn_conv", p, 0, 24, [wt["dn_conv_w"]])
    dn_pre_tiles = [(c, DN_WIDTH, 0), (c, DN_WIDTH, 1), (p, LANES, 32)]
    dn_pre_params = [wt["dn_a_log"], wt["dn_dt_bias"]]
    qh, kh, gb, bb = _row_fwd(_dn_pre_fn, "dn_pre", dn_pre_tiles, dn_pre_params, [(DN_WIDTH, F32)] * 4, 128)
    dn_arrs = [(qh, 0), (kh, 0), (c, 16), (gb, 0), (bb, 0)]
    with_gdn = ("ffn_w1",)
    o, st_dn, got = _scan_fwd(_gdn_chunk, "gdn_scan", dn_arrs, DN_HEADS,
                              _gather_plan({n: late[n] for n in with_gdn}))
    wt.update(_gather_finish(with_gdn, got))
    dn_post_tiles = [(o, DN_WIDTH, 0), (p, DN_WIDTH, 3)]
    o_dn = _row_fwd(_dn_post_fn, "dn_post", dn_post_tiles, [wt["dn_norm_w"]], [(DN_WIDTH, BF16)], 256)[0]

    ps = _col_fwd(_lerp_fn, "rw_shift", p, RW_OFF // LANES, 26, [wt["rw_mu"]])
    rw_pre_tiles = [(ps, RW_WIDTH, 0), (ps, RW_WIDTH, 1), (ps, RW_WIDTH, 2), (ps, LANES, 24), (ps, LANES, 25)]
    rw_pre_params = [wt[n] for n in ("rw_w0", "rw_a0", "rw_k_k", "rw_k_a", "rw_w2", "rw_a2", "rw_g2")]
    r, lw, k, v, al, be, gate = _row_fwd(_rw_pre_fn, "rw_pre", rw_pre_tiles, rw_pre_params,
                                         [(RW_WIDTH, F32)] * 7, 128)
    rw_arrs = [(r, 0), (lw, 0), (k, 0), (v, 0), (al, 0), (be, 0)]
    with_rw = ("ffn_w2", "w_out", "xa_wq", "xa_wk", "xa_wv", "xa_wo")
    y, st_rw, got = _scan_fwd(_rw_chunk, "rw_scan", rw_arrs, RW_WIDTH // LANES,
                              _gather_plan({n: late[n] for n in with_rw}))
    wt.update(_gather_finish(with_rw, got))
    rw_post_tiles = [(t, RW_WIDTH, 0) for t in (y, r, k, v, gate)]
    rw_post_params = [wt["rw_ln_w"], wt["rw_ln_b"], wt["rw_r_k"]]
    o_rw = _row_fwd(_rw_post_fn, "rw_post", rw_post_tiles, rw_post_params, [(RW_WIDTH, BF16)], 128)[0]
    o_cat = jnp.concatenate([o_dn, o_rw], axis=1)
    h1 = _matmul("out_proj", o_cat, wt["w_out"], "nn", [F32], _add_epilogue, (x,))[0]

    hn = _row_fwd(_rms_fn, "xa_norm", [(h1, d, 0)], [wt["xa_norm_w"]], [(d, BF16)], 256)[0]
    mn = _row_fwd(_rms_fn, "mem_norm", [(mem, d, 0)], [wt["mem_norm_w"]], [(d, BF16)], 256)[0]
    q = _matmul("xa_q", hn, wt["xa_wq"], "nn", [F32])[0]
    kx = _matmul("xa_k", mn, wt["xa_wk"], "nn", [F32])[0]
    vx = _matmul("xa_v", mn, wt["xa_wv"], "nn", [F32])[0]
    ao = _row_fwd(_xattn_fn, "xattn", [(q, XA_WIDTH, 0)], [kx, vx], [(XA_WIDTH, BF16)], 256)[0]
    h2 = _matmul("xa_o", ao, wt["xa_wo"], "nn", [F32], _add_epilogue, (h1,))[0]

    f = _row_fwd(_rms_fn, "ffn_norm", [(h2, d, 0)], [wt["ffn_norm_w"]], [(d, BF16)], 256)[0]
    a, hid = _matmul("ffn_up", f, wt["ffn_w1"], "nn", [F32, BF16],
                     lambda acc: (acc, jnp.square(jnp.maximum(acc, 0.0))))
    h3 = _matmul("ffn_down", hid, wt["ffn_w2"], "nn", [F32], _add_epilogue, (h2,))[0]
    loss8, dh3, g["final_norm_w"] = _loss_call(h3, target, wt["final_norm_w"])

    da = _matmul("ffn_down_dx", dh3, wt["ffn_w2"], "nt", [BF16],
                 lambda acc, av: (acc * 2.0 * jnp.maximum(av, 0.0),), (a,))[0]
    g["ffn_w2"] = _matmul("ffn_down_dw", hid, dh3, "tn", [BF16])[0]
    g["ffn_w1"] = _matmul("ffn_up_dw", f, da, "tn", [BF16])[0]
    df = _matmul("ffn_up_dx", da, wt["ffn_w1"], "nt", [F32])[0]
    (dh2,), (g["ffn_norm_w"],) = _row_bwd(_rms_res_fn, "ffn_norm_bwd", [(h2, d, 0)], [wt["ffn_norm_w"]],
                                          [[(df, d, 0)], [(dh3, d, 0)]], 256)

    dao = _matmul("xa_o_dx", dh2, wt["xa_wo"], "nt", [F32])[0]
    g["xa_wo"] = _matmul("xa_o_dw", ao, dh2, "tn", [BF16])[0]
    (dq,), (dkx, dvx) = _row_bwd(_xattn_fn, "xattn_bwd", [(q, XA_WIDTH, 0)], [kx, vx], [[(dao, XA_WIDTH, 0)]], 256)
    dhn = _matmul("xa_q_dx", dq, wt["xa_wq"], "nt", [F32])[0]
    g["xa_wq"] = _matmul("xa_q_dw", hn, dq, "tn", [BF16])[0]
    g["xa_wk"] = _matmul("xa_k_dw", mn, dkx, "tn", [BF16])[0]
    g["xa_wv"] = _matmul("xa_v_dw", mn, dvx, "tn", [BF16])[0]
    dmn = _matmul("xa_k_dx", dkx, wt["xa_wk"], "nt", [F32])[0]
    dmn = _matmul("xa_v_dx", dvx, wt["xa_wv"], "nt", [F32], _add_epilogue, (dmn,))[0]
    _, (g["mem_norm_w"],) = _row_bwd(_rms_fn, "mem_norm_bwd", [(mem, d, 0)], [wt["mem_norm_w"]],
                                     [[(dmn, d, 0)]], 256, want_tiles=())
    (dh1,), (g["xa_norm_w"],) = _row_bwd(_rms_res_fn, "xa_norm_bwd", [(h1, d, 0)], [wt["xa_norm_w"]],
                                         [[(dhn, d, 0)], [(dh2, d, 0)]], 256)

    do_cat = _matmul("out_proj_dx", dh1, wt["w_out"], "nt", [F32])[0]
    g["w_out"] = _matmul("out_proj_dw", o_cat, dh1, "tn", [BF16])[0]

    (dy, dr1, dk1, dv1, dgate), (g["rw_ln_w"], g["rw_ln_b"], g["rw_r_k"]) = _row_bwd(
        _rw_post_fn, "rw_post_bwd", rw_post_tiles, rw_post_params, [[(do_cat, RW_WIDTH, 1)]], 128)
    parts = {}
    with_rw_bwd = ("ffn_w1", "ffn_w2")
    (dr2, dlw, dk2, dv2, dal, dbe), got = _scan_bwd(_rw_chunk, "rw_scan_bwd", rw_arrs, st_rw, dy, RW_WIDTH // LANES,
                                                    _scatter_plan({n: g.pop(n) for n in with_rw_bwd}))
    parts.update(zip(with_rw_bwd, got))
    one = lambda t: [(t, RW_WIDTH, 0)]
    two = lambda s, t: [(s, RW_WIDTH, 0), (t, RW_WIDTH, 0)]
    d_ps, rw_pre_grads = _row_bwd(
        _rw_pre_fn, "rw_pre_bwd", rw_pre_tiles, rw_pre_params,
        [two(dr1, dr2), one(dlw), two(dk1, dk2), two(dv1, dv2), one(dal), one(dbe), one(dgate)], 128)
    for n, val in zip(("rw_w0", "rw_a0", "rw_k_k", "rw_k_a", "rw_w2", "rw_a2", "rw_g2"), rw_pre_grads):
        g[n] = val
    dp_rw, (g["rw_mu"],) = _col_bwd(_lerp_fn, "rw_shift_bwd", p, RW_OFF // LANES, 26, [wt["rw_mu"]],
                                    jnp.concatenate(d_ps, axis=1))

    (do, dz), (g["dn_norm_w"],) = _row_bwd(_dn_post_fn, "dn_post_bwd", dn_post_tiles, [wt["dn_norm_w"]],
                                           [[(do_cat, DN_WIDTH, 0)]], 256)
    with_gdn_bwd = ("w_out", "xa_wq", "xa_wk", "xa_wv", "xa_wo")
    (dqh, dkh, dv_dn, dgb, dbb), got = _scan_bwd(_gdn_chunk, "gdn_scan_bwd", dn_arrs, st_dn, do, DN_HEADS,
                                                 _scatter_plan({n: g.pop(n) for n in with_gdn_bwd}))
    parts.update(zip(with_gdn_bwd, got))
    one = lambda t: [(t, DN_WIDTH, 0)]
    (dcq, dck, dgates), (g["dn_a_log"], g["dn_dt_bias"]) = _row_bwd(
        _dn_pre_fn, "dn_pre_bwd", dn_pre_tiles, dn_pre_params, [one(dqh), one(dkh), one(dgb), one(dbb)], 128)
    dp_qkv, (g["dn_conv_w"],) = _col_bwd(_conv_fn, "dn_conv_bwd", p, 0, 24, [wt["dn_conv_w"]],
                                         jnp.concatenate([dcq, dck, dv_dn], axis=1))
    dp = jnp.concatenate([dp_qkv, dz, dgates, dp_rw, jnp.zeros((x.shape[0], LANES), F32)], axis=1).astype(BF16)
    du = _matmul("in_proj_dx", dp, wt["w_in"], "nt", [F32])[0]
    g["w_in"] = _matmul("in_proj_dw", u, dp, "tn", [BF16], tn=1536)[0]
    (dx,), (g["mix_norm_w"],) = _row_bwd(_rms_res_fn, "mix_norm_bwd", [(x, d, 0)], [wt["mix_norm_w"]],
                                         [[(du, d, 0)], [(dh1, d, 0)]], 256)
    return loss8, dx, g, parts


WEIGHTS = ["mix_norm_w", "w_in", "dn_conv_w", "dn_a_log", "dn_dt_bias", "dn_norm_w", "rw_mu", "rw_w0", "rw_w2",
           "rw_a0", "rw_a2", "rw_g2", "rw_k_k", "rw_k_a", "rw_r_k", "rw_ln_w", "rw_ln_b", "w_out", "xa_norm_w",
           "mem_norm_w", "xa_wq", "xa_wk", "xa_wv", "xa_wo", "ffn_norm_w", "ffn_w1", "ffn_w2", "final_norm_w"]
SHARDED = {"w_in": True, "w_out": False, "xa_wq": False, "xa_wk": False, "xa_wv": False, "xa_wo": True,
           "ffn_w1": True, "ffn_w2": False, "dn_conv_w": True, "rw_w2": True, "rw_a2": True, "rw_g2": True}
BF16_PAYLOAD = ("w_in", "w_out", "xa_wq", "xa_wk", "xa_wv", "xa_wo", "ffn_w1", "ffn_w2")
REPLICATED = [n for n in WEIGHTS if n not in SHARDED]
EARLY = ("w_in", "dn_conv_w", "rw_w2", "rw_a2", "rw_g2")
RW_IN_COLS = IN_COLS - DN_COLS


def _layout_weights(fw):
    wt = dict(fw)
    w_in = fw["w_in"]
    rows = w_in.shape[0]
    wt["w_in"] = jnp.concatenate(
        [w_in[:, :DN_COLS], jnp.zeros((rows, RW_OFF - DN_COLS), w_in.dtype), w_in[:, DN_COLS:],
         jnp.zeros((rows, IN_PAD - RW_OFF - RW_IN_COLS), w_in.dtype)], axis=1)
    wt["dn_conv_w"] = jnp.pad(fw["dn_conv_w"], ((0, 4), (0, 0)))
    wt["dn_a_log"] = jnp.pad(fw["dn_a_log"], ((0, 0), (0, LANES - DN_HEADS)))
    wt["dn_dt_bias"] = jnp.pad(fw["dn_dt_bias"], ((0, 0), (0, LANES - DN_HEADS)))
    wt["rw_w2"] = jnp.pad(fw["rw_w2"], ((0, 64), (0, 0)))
    wt["rw_a2"] = jnp.pad(fw["rw_a2"], ((64, 0), (0, 0)))
    return wt


def _logical_grads(g):
    out = dict(g)
    out["w_in"] = jnp.concatenate([g["w_in"][:, :DN_COLS], g["w_in"][:, RW_OFF:RW_OFF + RW_IN_COLS]], axis=1)
    out["dn_conv_w"] = g["dn_conv_w"][:4]
    out["dn_a_log"] = g["dn_a_log"][:, :DN_HEADS]
    out["dn_dt_bias"] = g["dn_dt_bias"][:, :DN_HEADS]
    out["rw_w2"] = g["rw_w2"][:64]
    out["rw_a2"] = g["rw_a2"][64:]
    return out


def _pack(vals):
    parts = []
    for v in vals:
        flat = v.reshape(-1)
        parts.append(jnp.pad(flat, (0, -flat.shape[0] % LANES)))
    flat = jnp.concatenate(parts)
    flat = jnp.pad(flat, (0, -flat.shape[0] % (8 * LANES)))
    return flat.reshape(-1, LANES)


def _unpack(packed, shapes):
    flat = packed.reshape(-1)
    out, at = [], 0
    for shp in shapes:
        size = math.prod(shp)
        out.append(flat[at:at + size].reshape(shp))
        at += size + (-size % LANES)
    return out


def kernel(x, mem, mix_norm_w, w_in, dn_conv_w, dn_a_log, dn_dt_bias, dn_norm_w, rw_mu, rw_w0, rw_w2, rw_a0, rw_a2, rw_g2, rw_k_k, rw_k_a, rw_r_k, rw_ln_w, rw_ln_b, w_out, xa_norm_w, mem_norm_w, xa_wq, xa_wk, xa_wv, xa_wo, ffn_norm_w, ffn_w1, ffn_w2, final_norm_w, loss_target, m_mix_norm_w, m_w_in, m_dn_conv_w, m_dn_a_log, m_dn_dt_bias, m_dn_norm_w, m_rw_mu, m_rw_w0, m_rw_w2, m_rw_a0, m_rw_a2, m_rw_g2, m_rw_k_k, m_rw_k_a, m_rw_r_k, m_rw_ln_w, m_rw_ln_b, m_w_out, m_xa_norm_w, m_mem_norm_w, m_xa_wq, m_xa_wk, m_xa_wv, m_xa_wo, m_ffn_norm_w, m_ffn_w1, m_ffn_w2, m_final_norm_w, v_mix_norm_w, v_w_in, v_dn_conv_w, v_dn_a_log, v_dn_dt_bias, v_dn_norm_w, v_rw_mu, v_rw_w0, v_rw_w2, v_rw_a0, v_rw_a2, v_rw_g2, v_rw_k_k, v_rw_k_a, v_rw_r_k, v_rw_ln_w, v_rw_ln_b, v_w_out, v_xa_norm_w, v_mem_norm_w, v_xa_wq, v_xa_wk, v_xa_wv, v_xa_wo, v_ffn_norm_w, v_ffn_w1, v_ffn_w2, v_final_norm_w):
    given = dict(locals())
    w = {n: given[n] for n in WEIGHTS}
    m = {n: given["m_" + n] for n in WEIGHTS}
    v = {n: given["v_" + n] for n in WEIGHTS}

    shards = {n: (w[n][0].astype(BF16) if n in BF16_PAYLOAD else w[n][0]) for n in SHARDED}
    srcs, dsts, _ = _gather_plan({n: shards[n] for n in EARLY})
    full = _gather_finish(EARLY, _exchange("early_all_gather", srcs, dsts, True))
    for n in REPLICATED:
        full[n] = w[n].reshape(1, -1)

    loss8, dx, g, parts = _local_step(x[0], mem[0], loss_target[0], _layout_weights(full),
                                      {n: shards[n] for n in SHARDED if n not in EARLY})
    g = _logical_grads(g)
    loss = lax.psum(loss8[0, 0], ("x", "y", "c"))

    srcs, dsts, _ = _scatter_plan({n: g[n] for n in EARLY})
    parts.update(zip(EARLY, _exchange("early_grad_all_to_all", srcs, dsts, False)))
    grad, delta, new_m, new_v = {}, {}, {}, {}
    for n in SHARDED:
        res = _sum_adamw("adamw_" + n, parts[n], w[n][0], m[n][0], v[n][0])
        grad[n], delta[n], new_m[n], new_v[n] = [t[None] for t in res]

    packed = _pack([g[n] for n in REPLICATED])
    parts = _exchange("small_all_gather", [(packed, None)], [((N_DEV,) + packed.shape, F32, None)], True)[0]
    res = _sum_adamw("adamw_small", parts, _pack([w[n] for n in REPLICATED]),
                     _pack([m[n] for n in REPLICATED]), _pack([v[n] for n in REPLICATED]))
    shapes = [w[n].shape for n in REPLICATED]
    for store, packed_out in zip((grad, delta, new_m, new_v), res):
        for n, val in zip(REPLICATED, _unpack(packed_out, shapes)):
            store[n] = val

    return (loss, dx[None], *[grad[n] for n in WEIGHTS], *[delta[n] for n in WEIGHTS],
            *[new_m[n] for n in WEIGHTS], *[new_v[n] for n in WEIGHTS])
```

```python
import functools
import math

import jax
import jax.numpy as jnp
from jax import lax
from jax.experimental import pallas as pl
from jax.experimental.pallas import tpu as pltpu

F32 = jnp.float32
BF16 = jnp.bfloat16
SDS = jax.ShapeDtypeStruct

N_DEV = 8
D_MODEL = 2048
LANES = 128
CHUNK = 128
DN_HEADS = 8
DN_WIDTH = 1024
RW_WIDTH = 1024
RW_HEAD = 64
XA_HEADS = 4
XA_WIDTH = 512
FFN_HIDDEN = 8192
IN_COLS = 7440
DN_COLS = 4112
IN_PAD = 7680
RW_OFF = 4224
RMS_EPS = 1e-6
RW_GN_EPS = 64e-5
VMEM_LIMIT = 56 * 1024 * 1024

ADAM_LR = 0.001
ADAM_B1 = 0.9
ADAM_B2 = 0.999
ADAM_EPS = 1e-08
ADAM_WD = 0.01
ADAM_STEP = 10

_DIMS = {"nn": (((1,), (0,)), ((), ())), "nt": (((1,), (1,)), ((), ())), "tn": (((0,), (0,)), ((), ()))}


def _raw_dot(a, b, mode, hi):
    if hi:
        return lax.dot_general(a, b, _DIMS[mode], precision=lax.Precision.HIGHEST, preferred_element_type=F32)
    return lax.dot_general(a.astype(BF16), b.astype(BF16), _DIMS[mode], preferred_element_type=F32)


@functools.partial(jax.custom_vjp, nondiff_argnums=(2, 3))
def mm(a, b, mode="nn", hi=False):
    return _raw_dot(a, b, mode, hi)


def _mm_fwd(a, b, mode, hi):
    return _raw_dot(a, b, mode, hi), (a, b)


def _mm_bwd(mode, hi, res, g):
    a, b = res
    if mode == "nn":
        return _raw_dot(g, b, "nt", hi), _raw_dot(a, g, "tn", hi)
    if mode == "nt":
        return _raw_dot(g, b, "nn", hi), _raw_dot(g, a, "tn", hi)
    return _raw_dot(b, g, "nt", hi), _raw_dot(a, g, "nn", hi)


mm.defvjp(_mm_fwd, _mm_bwd)


def _shift_rows_raw(x, k):
    n = x.shape[0]
    rolled = pltpu.roll(x, k % n, axis=0)
    row = lax.broadcasted_iota(jnp.int32, x.shape, 0)
    keep = row >= k if k > 0 else row < n + k
    return jnp.where(keep, rolled, 0.0)


@functools.partial(jax.custom_vjp, nondiff_argnums=(1,))
def shift_rows(x, k):
    return _shift_rows_raw(x, k)


shift_rows.defvjp(lambda x, k: (_shift_rows_raw(x, k), None), lambda k, _, g: (_shift_rows_raw(g, -k),))


def _softplus(x):
    return jnp.maximum(x, 0.0) + jnp.log(1.0 + jnp.exp(-jnp.abs(x)))


def _sigmoid(x):
    return 1.0 / (1.0 + jnp.exp(-x))


def _silu(x):
    return x * _sigmoid(x)


def _tri_masks(n):
    ii = lax.broadcasted_iota(jnp.int32, (n, n), 0)
    jj = lax.broadcasted_iota(jnp.int32, (n, n), 1)
    return ii >= jj, ii > jj, ii == jj


def _neumann_inv_raw(m):
    n = m.shape[0]
    _, _, eye = _tri_masks(n)
    eye = jnp.where(eye, 1.0, 0.0)
    p = eye + m
    mk = m
    for _ in range(int(math.log2(n)) - 1):
        mk = _raw_dot(mk, mk, "nn", False)
        p = p + _raw_dot(p, mk, "nn", False)
    resid = eye - p + _raw_dot(m, p, "nn", True)
    return p + _raw_dot(p, resid, "nn", False)


@jax.custom_vjp
def _neumann_inv(m):
    return _neumann_inv_raw(m)


def _neumann_inv_fwd(m):
    p = _neumann_inv_raw(m)
    return p, p


def _neumann_inv_bwd(p, g):
    return (_raw_dot(_raw_dot(p, g, "tn", False), p, "nt", False),)


_neumann_inv.defvjp(_neumann_inv_fwd, _neumann_inv_bwd)


def _gdn_chunk(s0, q, k, v, gb, bb):
    c = q.shape[0]
    causal, strict, _ = _tri_masks(c)
    gc = mm(jnp.where(causal, 1.0, 0.0), gb, "nn", True)
    diff = gc - gc.T
    decay = jnp.exp(jnp.where(causal, diff, -jnp.inf))
    kb = k * bb
    a = jnp.where(strict, mm(kb, k, "nt") * decay, 0.0)
    p = _neumann_inv(-a)
    u = mm(p, v * bb)
    w = mm(p, kb * jnp.exp(gc))
    attn = mm(q, k, "nt") * decay
    v_new = u - mm(w, s0)
    o = mm(q * jnp.exp(gc), s0) + mm(attn, v_new)
    g_last = jnp.sum(gb, axis=0, keepdims=True)
    s1 = s0 * jnp.exp(g_last) + mm(k * jnp.exp(g_last - gc), v_new, "tn")
    return o, s1


def _rw_chunk(s0, r, lw, k, v, al, be):
    c = r.shape[0]
    causal, strict, _ = _tri_masks(c)
    gc = mm(jnp.where(causal, 1.0, 0.0), lw, "nn", True)
    gp = gc - lw
    row = lax.broadcasted_iota(jnp.int32, lw.shape, 0)
    lane = lax.broadcasted_iota(jnp.int32, lw.shape, 1)
    g_mid = jnp.sum(jnp.where(row < c // 2, lw, 0.0), axis=0, keepdims=True)
    g_last = jnp.sum(lw, axis=0, keepdims=True)
    e_n = jnp.exp(g_mid - gc)
    rg = r * jnp.exp(gc - g_mid)
    bg = be * jnp.exp(gp - g_mid)
    an = al * e_n
    kn = k * e_n
    bt = mm(be * jnp.exp(gp), s0, "nt")
    rt = mm(r * jnp.exp(gc), s0, "nt")
    us, ys = [], []
    for h in range(2):
        mine = (lane >= RW_HEAD) if h else (lane < RW_HEAD)
        bgh = jnp.where(mine, bg, 0.0)
        rgh = jnp.where(mine, rg, 0.0)
        a_ab = jnp.where(strict, mm(bgh, an, "nt"), 0.0)
        a_kb = jnp.where(strict, mm(bgh, kn, "nt"), 0.0)
        a_ra = jnp.where(causal, mm(rgh, an, "nt"), 0.0)
        a_rk = jnp.where(causal, mm(rgh, kn, "nt"), 0.0)
        p = _neumann_inv(a_ab)
        u_h = mm(p, bt + mm(a_kb, v))
        us.append(u_h)
        ys.append(rt + mm(a_ra, u_h) + mm(a_rk, v))
    lo = lane < RW_HEAD
    u = jnp.where(lo, us[0], us[1])
    y = jnp.where(lo, ys[0], ys[1])
    tail = jnp.exp(g_last - gc)
    s1 = s0 * jnp.exp(g_last) + mm(u, al * tail, "tn") + mm(v, k * tail, "tn")
    vi = lax.broadcasted_iota(jnp.int32, s0.shape, 0)
    ki = lax.broadcasted_iota(jnp.int32, s0.shape, 1)
    s1 = jnp.where((vi < RW_HEAD) == (ki < RW_HEAD), s1, 0.0)
    return y, s1


SCAN_HB = 4


def _scan_specs(arrs, n_chunks, reverse):
    def spec(off):
        assert off % SCAN_HB == 0
        if reverse:
            return pl.BlockSpec((CHUNK, SCAN_HB * LANES), lambda h, n: (n_chunks - 1 - n, off // SCAN_HB + h))
        return pl.BlockSpec((CHUNK, SCAN_HB * LANES), lambda h, n: (n, off // SCAN_HB + h))
    return [spec(off) for _, off in arrs]


def _split_heads(x):
    return jnp.stack([x[:, LANES * j:LANES * (j + 1)] for j in range(SCAN_HB)], axis=0)


def _merge_heads(x):
    return jnp.concatenate([x[j] for j in range(SCAN_HB)], axis=1)


def _hosted(xfer, heads, n_chunks):
    if xfer is None:
        return 0, [], [], [], [], lambda refs: None
    srcs, dsts, gather = xfer
    n = len(srcs)
    any_spec = pl.BlockSpec(memory_space=pl.ANY)

    def run(src_refs, out_refs, sems):
        start, wait = _exchange_ops([c for _, c in srcs], [c for _, _, c in dsts], gather, src_refs, out_refs, *sems)
        h, c = pl.program_id(0), pl.program_id(1)
        pl.when(jnp.logical_and(h == 0, c == 0))(start)
        pl.when(jnp.logical_and(h == heads - 1, c == n_chunks - 1))(wait)

    return (n, [any_spec] * n, [any_spec] * n, [SDS(shape, dt) for shape, dt, _ in dsts],
            _exchange_sems(n), run)


def _scan_fwd(chunk_fn, name, arrs, heads, xfer=None):
    s = arrs[0][0].shape[0]
    n_chunks = s // CHUNK
    n_in = len(arrs)
    groups = heads // SCAN_HB
    n_x, x_in, x_out, x_shapes, x_sems, run_x = _hosted(xfer, groups, n_chunks)

    def body(*refs):
        y_ref, st_ref = refs[n_in + n_x:n_in + n_x + 2]
        s_scr = refs[n_in + 2 * n_x + 2]
        if n_x:
            run_x(refs[n_in:n_in + n_x], refs[n_in + n_x + 2:n_in + 2 * n_x + 2], refs[n_in + 2 * n_x + 3:])

        @pl.when(pl.program_id(1) == 0)
        def _():
            s_scr[...] = jnp.zeros_like(s_scr)

        s0 = s_scr[...]
        st_ref[...] = s0
        y, s1 = jax.vmap(chunk_fn)(s0, *[_split_heads(r[...]) for r in refs[:n_in]])
        y_ref[...] = _merge_heads(y)
        s_scr[...] = s1

    wide = SCAN_HB * LANES
    res = pl.pallas_call(
        body, grid=(groups, n_chunks), name=name,
        in_specs=_scan_specs(arrs, n_chunks, False) + x_in,
        out_specs=[pl.BlockSpec((CHUNK, wide), lambda h, n: (n, h)),
                   pl.BlockSpec((SCAN_HB, None, LANES, LANES), lambda h, n: (h, n, 0, 0))] + x_out,
        out_shape=[SDS((s, heads * LANES), F32), SDS((heads, n_chunks, LANES, LANES), F32)] + x_shapes,
        scratch_shapes=[pltpu.VMEM((SCAN_HB, LANES, LANES), F32)] + x_sems,
        compiler_params=pltpu.CompilerParams(dimension_semantics=("arbitrary", "arbitrary")),
    )(*[a for a, _ in arrs], *([a for a, _ in xfer[0]] if xfer else []))
    return res[0], res[1], res[2:]


def _scan_bwd(chunk_fn, name, arrs, states, dy, heads, xfer=None):
    s = arrs[0][0].shape[0]
    n_chunks = s // CHUNK
    n_in = len(arrs)
    groups = heads // SCAN_HB
    n_x, x_in, x_out, x_shapes, x_sems, run_x = _hosted(xfer, groups, n_chunks)

    def body(*refs):
        st_ref, dy_ref = refs[n_in:n_in + 2]
        first_out = n_in + 2 + n_x
        d_refs = refs[first_out:first_out + n_in]
        ds_scr = refs[first_out + n_in + n_x]
        if n_x:
            run_x(refs[n_in + 2:first_out], refs[first_out + n_in:first_out + n_in + n_x],
                  refs[first_out + n_in + n_x + 1:])

        @pl.when(pl.program_id(1) == 0)
        def _():
            ds_scr[...] = jnp.zeros_like(ds_scr)

        _, vjp = jax.vjp(jax.vmap(chunk_fn), st_ref[...], *[_split_heads(r[...]) for r in refs[:n_in]])
        grads = vjp((_split_heads(dy_ref[...]), ds_scr[...]))
        ds_scr[...] = grads[0]
        for ref, g in zip(d_refs, grads[1:]):
            ref[...] = _merge_heads(g)

    wide = SCAN_HB * LANES
    rev = pl.BlockSpec((CHUNK, wide), lambda h, n: (n_chunks - 1 - n, h))
    res = pl.pallas_call(
        body, grid=(groups, n_chunks), name=name,
        in_specs=_scan_specs(arrs, n_chunks, True)
        + [pl.BlockSpec((SCAN_HB, None, LANES, LANES), lambda h, n: (h, n_chunks - 1 - n, 0, 0)), rev] + x_in,
        out_specs=[rev] * n_in + x_out,
        out_shape=[SDS((s, heads * LANES), F32)] * n_in + x_shapes,
        scratch_shapes=[pltpu.VMEM((SCAN_HB, LANES, LANES), F32)] + x_sems,
        compiler_params=pltpu.CompilerParams(dimension_semantics=("arbitrary", "arbitrary")),
    )(*[a for a, _ in arrs], states, dy, *([a for a, _ in xfer[0]] if xfer else []))
    return res[:n_in], res[n_in:]


def _col_spec(tr, width, cb):
    return pl.BlockSpec((tr, width), lambda i: (i, cb))


def _whole(p):
    return pl.BlockSpec(p.shape, lambda i: (0,) * p.ndim)


def _row_fwd(fn, name, tiles, params, outs, tr):
    rows = tiles[0][0].shape[0]
    nt, npar = len(tiles), len(params)

    def body(*refs):
        vals = [r[...].astype(F32) for r in refs[:nt + npar]]
        for ref, o in zip(refs[nt + npar:], fn(*vals)):
            ref[...] = o.astype(ref.dtype)

    return pl.pallas_call(
        body, grid=(rows // tr,), name=name,
        in_specs=[_col_spec(tr, w, cb) for _, w, cb in tiles] + [_whole(p) for p in params],
        out_specs=[_col_spec(tr, w, 0) for w, _ in outs],
        out_shape=[SDS((rows, w), dt) for w, dt in outs],
        compiler_params=pltpu.CompilerParams(dimension_semantics=("arbitrary",), vmem_limit_bytes=VMEM_LIMIT),
    )(*[a for a, _, _ in tiles], *params)


def _row_bwd(fn, name, tiles, params, cts, tr, want_tiles=None):
    rows = tiles[0][0].shape[0]
    nt, npar = len(tiles), len(params)
    want = list(range(nt)) if want_tiles is None else list(want_tiles)
    flat_cts = [c for group in cts for c in group]
    n_ct = len(flat_cts)

    def body(*refs):
        vals = [r[...].astype(F32) for r in refs[:nt + npar]]
        ct_refs = refs[nt + npar:nt + npar + n_ct]
        out_refs = refs[nt + npar + n_ct:]
        ct_vals, at = [], 0
        for group in cts:
            total = ct_refs[at][...].astype(F32)
            for r in ct_refs[at + 1:at + len(group)]:
                total = total + r[...].astype(F32)
            ct_vals.append(total)
            at += len(group)
        _, vjp = jax.vjp(lambda *a: tuple(fn(*a)), *vals)
        grads = vjp(tuple(ct_vals))
        for ref, t in zip(out_refs[:len(want)], want):
            ref[...] = grads[t]
        first = pl.program_id(0) == 0
        for ref, g in zip(out_refs[len(want):], grads[nt:]):
            @pl.when(first)
            def _(ref=ref, g=g):
                ref[...] = g

            @pl.when(jnp.logical_not(first))
            def _(ref=ref, g=g):
                ref[...] += g

    res = pl.pallas_call(
        body, grid=(rows // tr,), name=name,
        in_specs=[_col_spec(tr, w, cb) for _, w, cb in tiles] + [_whole(p) for p in params]
        + [_col_spec(tr, w, cb) for _, w, cb in flat_cts],
        out_specs=[_col_spec(tr, tiles[t][1], 0) for t in want] + [_whole(p) for p in params],
        out_shape=[SDS((rows, tiles[t][1]), F32) for t in want] + [SDS(p.shape, F32) for p in params],
        compiler_params=pltpu.CompilerParams(dimension_semantics=("arbitrary",), vmem_limit_bytes=VMEM_LIMIT),
    )(*[a for a, _, _ in tiles], *params, *[a for a, _, _ in flat_cts])
    return res[:len(want)], res[len(want):]


def _col_fwd(fn, name, x, first_block, n_blocks, params):
    rows = x.shape[0]

    def body(*refs):
        refs[-1][...] = fn(*[r[...] for r in refs[:-1]])

    return pl.pallas_call(
        body, grid=(n_blocks,), name=name,
        in_specs=[pl.BlockSpec((rows, LANES), lambda j: (0, first_block + j))]
        + [pl.BlockSpec((p.shape[0], LANES), lambda j: (0, j)) for p in params],
        out_specs=pl.BlockSpec((rows, LANES), lambda j: (0, j)),
        out_shape=SDS((rows, n_blocks * LANES), F32),
        compiler_params=pltpu.CompilerParams(dimension_semantics=("arbitrary",), vmem_limit_bytes=VMEM_LIMIT),
    )(x, *params)


def _col_bwd(fn, name, x, first_block, n_blocks, params, dy):
    rows = x.shape[0]
    npar = len(params)

    def body(*refs):
        vals = [r[...] for r in refs[:1 + npar]]
        _, vjp = jax.vjp(fn, *vals)
        grads = vjp(refs[1 + npar][...])
        for ref, g in zip(refs[2 + npar:], grads):
            ref[...] = g

    pspecs = [pl.BlockSpec((p.shape[0], LANES), lambda j: (0, j)) for p in params]
    blk = pl.BlockSpec((rows, LANES), lambda j: (0, j))
    res = pl.pallas_call(
        body, grid=(n_blocks,), name=name,
        in_specs=[pl.BlockSpec((rows, LANES), lambda j: (0, first_block + j))] + pspecs + [blk],
        out_specs=[blk] + pspecs,
        out_shape=[SDS((rows, n_blocks * LANES), F32)] + [SDS(p.shape, F32) for p in params],
        compiler_params=pltpu.CompilerParams(dimension_semantics=("arbitrary",), vmem_limit_bytes=VMEM_LIMIT),
    )(x, *params, dy)
    return res[0], res[1:]


def _conv_fn(x, w):
    acc = x * w[3:4, :]
    for j in range(3):
        acc = acc + shift_rows(x, 3 - j) * w[j:j + 1, :]
    return _silu(acc)


def _lerp_fn(x, mu):
    return x + (shift_rows(x, 1) - x) * mu[0:1, :]


def _seg_sum(x, width):
    if width == LANES:
        return jnp.sum(x, axis=1, keepdims=True)
    lo = lax.broadcasted_iota(jnp.int32, x.shape, 1) < width
    s0 = jnp.sum(jnp.where(lo, x, 0.0), axis=1, keepdims=True)
    s1 = jnp.sum(jnp.where(lo, 0.0, x), axis=1, keepdims=True)
    return jnp.where(lo, s0, s1)


def _per_block(fn, *xs):
    n = xs[0].shape[1] // LANES
    return jnp.concatenate([fn(*[x[:, LANES * b:LANES * (b + 1)] for x in xs]) for b in range(n)], axis=1)


def _head_expand(col0):
    r = lax.broadcasted_iota(jnp.int32, (LANES, DN_WIDTH), 0)
    c = lax.shift_right_logical(lax.broadcasted_iota(jnp.int32, (LANES, DN_WIDTH), 1), 7)
    return jnp.where(r == c + col0, 1.0, 0.0)


def _dn_pre_fn(cq, ck, gates, a_log, dt_bias):
    l2 = lambda x: x * lax.rsqrt(_seg_sum(x * x, LANES) + 1e-6)
    qh = _per_block(l2, cq) * (LANES ** -0.5)
    kh = _per_block(l2, ck)
    g = -jnp.exp(a_log) * _softplus(gates + dt_bias)
    gb = mm(g, _head_expand(0), "nn", True)
    bb = mm(_sigmoid(gates), _head_expand(DN_HEADS), "nn", True)
    return qh, kh, gb, bb


def _dn_post_fn(o, z, nw):
    def one(ob, zb):
        return ob * lax.rsqrt(_seg_sum(ob * ob, LANES) * (1.0 / LANES) + RMS_EPS) * nw * _silu(zb)
    return (_per_block(one, o, z),)


def _rw_pre_fn(pr, pk, pv, pwa, pg, w0, a0, k_k, k_a, w2p, a2p, g2):
    log_w = -_softplus(-(w0 + mm(jnp.tanh(pwa), w2p))) - 0.5
    lw = -jnp.exp(log_w)
    a = _sigmoid(a0 + mm(pwa, a2p))
    gate = mm(_sigmoid(pg), g2)
    kk = pk * k_k
    kk = _per_block(lambda x: x / jnp.maximum(jnp.sqrt(_seg_sum(x * x, RW_HEAD)), 1e-12), kk)
    k = pk * (1.0 + (a - 1.0) * k_a)
    return pr, lw, k, pv, kk * a, -kk, gate


def _rw_post_fn(y, r, k, v, gate, ln_w, ln_b, r_k):
    def one(yb, rb, kb, vb, gb, wb, bb, rkb):
        d = yb - _seg_sum(yb, RW_HEAD) * (1.0 / RW_HEAD)
        var = _seg_sum(d * d, RW_HEAD) * (1.0 / RW_HEAD)
        yn = d * lax.rsqrt(var + RW_GN_EPS) * wb + bb
        return (yn + _seg_sum(rb * kb * rkb, RW_HEAD) * vb) * gb
    return (_per_block(one, y, r, k, v, gate, ln_w, ln_b, r_k),)


def _rms_fn(h, w):
    return (h * lax.rsqrt(jnp.mean(h * h, axis=1, keepdims=True) + RMS_EPS) * w,)


def _xattn_fn(q, k, v):
    outs = []
    for h in range(XA_HEADS):
        sl = slice(LANES * h, LANES * (h + 1))
        s = mm(q[:, sl], k[:, sl], "nt") * (LANES ** -0.5)
        e = jnp.exp(s - jnp.max(s, axis=1, keepdims=True))
        outs.append(mm(e / jnp.sum(e, axis=1, keepdims=True), v[:, sl]))
    return (jnp.concatenate(outs, axis=1),)


def _fit(tile, dim):
    best = [t for t in range(LANES, min(tile, dim) + 1, LANES) if dim % t == 0]
    assert best, (tile, dim)
    return best[-1]


def _matmul(name, a, b, mode, out_dtypes, epilogue=None, extras=(), tm=1024, tn=1024, tk=2048):
    if mode == "tn":
        (k_dim, m), n = a.shape, b.shape[1]
    else:
        (m, k_dim), n = a.shape, (b.shape[1] if mode == "nn" else b.shape[0])
    tm, tn, tk = _fit(tm, m), _fit(tn, n), _fit(tk, k_dim)
    nk = k_dim // tk
    a_spec = (pl.BlockSpec((tk, tm), lambda i, j, k: (k, i)) if mode == "tn"
              else pl.BlockSpec((tm, tk), lambda i, j, k: (i, k)))
    b_spec = (pl.BlockSpec((tn, tk), lambda i, j, k: (j, k)) if mode == "nt"
              else pl.BlockSpec((tk, tn), lambda i, j, k: (k, j)))
    o_spec = pl.BlockSpec((tm, tn), lambda i, j, k: (i, j))
    n_ex, n_out = len(extras), len(out_dtypes)

    def finish(total, rest):
        ex = [r[...].astype(F32) for r in rest[:n_ex]]
        res = epilogue(total, *ex) if epilogue else (total,)
        for ref, o in zip(rest[n_ex:n_ex + n_out], res):
            ref[...] = o.astype(ref.dtype)

    def body_single(a_ref, b_ref, *rest):
        finish(_raw_dot(a_ref[...], b_ref[...], mode, False), rest)

    def body_acc(a_ref, b_ref, *rest):
        acc = rest[-1]
        k = pl.program_id(2)

        @pl.when(k == 0)
        def _():
            acc[...] = jnp.zeros_like(acc)

        acc[...] += _raw_dot(a_ref[...], b_ref[...], mode, False)

        @pl.when(k == nk - 1)
        def _():
            finish(acc[...], rest)

    res = pl.pallas_call(
        body_single if nk == 1 else body_acc, grid=(m // tm, n // tn, nk), name=name,
        in_specs=[a_spec, b_spec] + [o_spec] * n_ex,
        out_specs=[o_spec] * n_out,
        out_shape=[SDS((m, n), dt) for dt in out_dtypes],
        scratch_shapes=[] if nk == 1 else [pltpu.VMEM((tm, tn), F32)],
        compiler_params=pltpu.CompilerParams(dimension_semantics=("parallel", "parallel", "arbitrary"),
                                             vmem_limit_bytes=VMEM_LIMIT),
    )(a, b, *extras)
    return res


def _loss_call(h, target, w, tr=256):
    rows, d = h.shape

    def fn(hv, wv, tv):
        y = _rms_fn(hv, wv)[0]
        return 0.5 * jnp.sum(jnp.mean(jnp.square(y - tv), axis=1, keepdims=True), axis=0, keepdims=True)

    def body(h_ref, t_ref, w_ref, loss_ref, dh_ref, dw_ref):
        tv = t_ref[...]
        val, vjp = jax.vjp(lambda hv, wv: fn(hv, wv, tv), h_ref[...], w_ref[...])
        dh, dw = vjp(jnp.ones((1, 1), F32))
        dh_ref[...] = dh
        first = pl.program_id(0) == 0

        @pl.when(first)
        def _():
            loss_ref[...] = jnp.broadcast_to(val, loss_ref.shape)
            dw_ref[...] = dw

        @pl.when(jnp.logical_not(first))
        def _():
            loss_ref[...] += jnp.broadcast_to(val, loss_ref.shape)
            dw_ref[...] += dw

    return pl.pallas_call(
        body, grid=(rows // tr,), name="loss_head",
        in_specs=[_col_spec(tr, d, 0), _col_spec(tr, d, 0), _whole(w)],
        out_specs=[pl.BlockSpec((8, LANES), lambda i: (0, 0)), _col_spec(tr, d, 0), _whole(w)],
        out_shape=[SDS((8, LANES), F32), SDS((rows, d), F32), SDS(w.shape, F32)],
        compiler_params=pltpu.CompilerParams(dimension_semantics=("arbitrary",), vmem_limit_bytes=VMEM_LIMIT),
    )(h, target, w)


def _adamw_vals(w, g, m, v):
    m = ADAM_B1 * m + (1.0 - ADAM_B1) * g
    v = ADAM_B2 * v + (1.0 - ADAM_B2) * jnp.square(g)
    m_hat = m / (1.0 - ADAM_B1 ** ADAM_STEP)
    v_hat = v / (1.0 - ADAM_B2 ** ADAM_STEP)
    delta = -ADAM_LR * (m_hat / (jnp.sqrt(v_hat) + ADAM_EPS) + ADAM_WD * w)
    return delta, m, v


def _sum_adamw(name, parts, w, m, v):
    r, c = w.shape
    tr = r
    for cand in (512, 256, 128, 64, 32, 16, 8):
        if r % cand == 0 and N_DEV * cand * c * 4 <= 6 * 1024 * 1024:
            tr = cand
            break

    def body(p_ref, w_ref, m_ref, v_ref, g_ref, d_ref, m2_ref, v2_ref):
        g = p_ref[0].astype(F32)
        for s in range(1, N_DEV):
            g = g + p_ref[s].astype(F32)
        g_ref[...] = g
        d_ref[...], m2_ref[...], v2_ref[...] = _adamw_vals(w_ref[...], g, m_ref[...], v_ref[...])

    blk = pl.BlockSpec((tr, c), lambda i: (i, 0))
    return pl.pallas_call(
        body, grid=(r // tr,), name=name,
        in_specs=[pl.BlockSpec((N_DEV, tr, c), lambda i: (0, i, 0)), blk, blk, blk],
        out_specs=[blk] * 4, out_shape=[SDS((r, c), F32)] * 4,
        compiler_params=pltpu.CompilerParams(dimension_semantics=("arbitrary",), vmem_limit_bytes=VMEM_LIMIT),
    )(parts, w, m, v)


def _peers():
    x, y, c = lax.axis_index("x"), lax.axis_index("y"), lax.axis_index("c")
    peers = []
    for k in range(1, N_DEV):
        px = 1 - x if k & 4 else x
        py = 1 - y if k & 2 else y
        pc = 1 - c if k & 1 else c
        peers.append(((px, py, pc), 4 * px + 2 * py + pc))
    return 4 * x + 2 * y + c, peers


def _slot(ref, idx, cols):
    if cols is None:
        return ref.at[idx]
    return ref.at[:, pl.ds(pl.multiple_of(idx * cols, LANES), cols)]


def _exchange(name, srcs, dsts, gather):
    n = len(srcs)

    def body(*refs):
        start, wait = _exchange_ops([c for _, c in srcs], [c for _, _, c in dsts], gather,
                                    refs[:n], refs[n:2 * n], *refs[2 * n:])
        start()
        wait()

    any_spec = pl.BlockSpec(memory_space=pl.ANY)
    return pl.pallas_call(
        body, name=name,
        in_specs=[any_spec] * n, out_specs=[any_spec] * n,
        out_shape=[SDS(shape, dt) for shape, dt, _ in dsts],
        scratch_shapes=_exchange_sems(n),
    )(*[a for a, _ in srcs])


def _exchange_sems(n):
    return [pltpu.SemaphoreType.DMA((n, N_DEV - 1)), pltpu.SemaphoreType.DMA((n, N_DEV - 1)),
            pltpu.SemaphoreType.DMA((n,))]


def _exchange_ops(src_cols, dst_cols, gather, src_refs, out_refs, send_sems, recv_sems, local_sems):
    def copies(with_landings):
        me, peers = _peers()
        local, sends, landings = [], [], []
        for a, (s_cols, d_cols) in enumerate(zip(src_cols, dst_cols)):
            mine = src_refs[a] if gather else _slot(src_refs[a], me, s_cols)
            local.append(pltpu.make_async_copy(mine, _slot(out_refs[a], me, d_cols), local_sems.at[a]))
            for k, (pos, idx) in enumerate(peers):
                out_blk = src_refs[a] if gather else _slot(src_refs[a], idx, s_cols)
                both = dict(src_ref=out_blk, send_sem=send_sems.at[a, k], recv_sem=recv_sems.at[a, k],
                            device_id=pos, device_id_type=pl.DeviceIdType.MESH)
                sends.append(pltpu.make_async_remote_copy(dst_ref=_slot(out_refs[a], me, d_cols), **both))
                if with_landings:
                    landings.append(pltpu.make_async_remote_copy(dst_ref=_slot(out_refs[a], idx, d_cols), **both))
        return local, sends, landings

    def start():
        local, sends, _ = copies(False)
        for cp in local + sends:
            cp.start()

    def wait():
        local, sends, landings = copies(True)
        for cp in landings:
            cp.wait_recv()
        for cp in sends:
            cp.wait_send()
        for cp in local:
            cp.wait()

    return start, wait


def _rms_res_fn(h, w):
    return _rms_fn(h, w)[0], h


def _add_epilogue(acc, res):
    return (acc + res,)


def _gather_plan(shards):
    srcs, dsts = [], []
    for n, sh in shards.items():
        r, c = sh.shape
        srcs.append((sh, None))
        if SHARDED[n] and c % LANES == 0:
            dsts.append(((r, N_DEV * c), sh.dtype, c))
        else:
            dsts.append(((N_DEV, r, c), sh.dtype, None))
    return srcs, dsts, True


def _gather_finish(names, outs):
    full = {}
    for n, arr in zip(names, outs):
        if arr.ndim == 2:
            full[n] = arr
        elif SHARDED[n]:
            full[n] = arr.transpose(1, 0, 2).reshape(arr.shape[1], -1)
        else:
            full[n] = arr.reshape(-1, arr.shape[2])
    return full


def _scatter_plan(grads):
    srcs, dsts = [], []
    for n, gr in grads.items():
        rows, cols = gr.shape
        if not SHARDED[n]:
            r, c = rows // N_DEV, cols
            srcs.append((gr.reshape(N_DEV, r, c), None))
        else:
            r, c = rows, cols // N_DEV
            if c % LANES == 0:
                srcs.append((gr, c))
            else:
                srcs.append((gr.reshape(r, N_DEV, c).transpose(1, 0, 2), None))
        dsts.append(((N_DEV, r, c), gr.dtype, None))
    return srcs, dsts, False


def _local_step(x, mem, target, wt, late):
    d = D_MODEL
    g = {}
    wt = dict(wt)
    u = _row_fwd(_rms_fn, "mix_norm", [(x, d, 0)], [wt["mix_norm_w"]], [(d, BF16)], 256)[0]
    p = _matmul("in_proj", u, wt["w_in"], "nn", [F32], tn=1536)[0]
    c = _col_fwd(_conv_fn, "dn_conv", p, 0, 24, [wt["dn_conv_w"]])
    dn_pre_tiles = [(c, DN_WIDTH, 0), (c, DN_WIDTH, 1), (p, LANES, 32)]
    dn_pre_params = [wt["dn_a_log"], wt["dn_dt_bias"]]
    qh, kh, gb, bb = _row_fwd(_dn_pre_fn, "dn_pre", dn_pre_tiles, dn_pre_params, [(DN_WIDTH, F32)] * 4, 128)
    dn_arrs = [(qh, 0), (kh, 0), (c, 16), (gb, 0), (bb, 0)]
    with_gdn = ("ffn_w1",)
    o, st_dn, got = _scan_fwd(_gdn_chunk, "gdn_scan", dn_arrs, DN_HEADS,
                              _gather_plan({n: late[n] for n in with_gdn}))
    wt.update(_gather_finish(with_gdn, got))
    dn_post_tiles = [(o, DN_WIDTH, 0), (p, DN_WIDTH, 3)]
    o_dn = _row_fwd(_dn_post_fn, "dn_post", dn_post_tiles, [wt["dn_norm_w"]], [(DN_WIDTH, BF16)], 256)[0]

    ps = _col_fwd(_lerp_fn, "rw_shift", p, RW_OFF // LANES, 26, [wt["rw_mu"]])
    rw_pre_tiles = [(ps, RW_WIDTH, 0), (ps, RW_WIDTH, 1), (ps, RW_WIDTH, 2), (ps, LANES, 24), (ps, LANES, 25)]
    rw_pre_params = [wt[n] for n in ("rw_w0", "rw_a0", "rw_k_k", "rw_k_a", "rw_w2", "rw_a2", "rw_g2")]
    r, lw, k, v, al, be, gate = _row_fwd(_rw_pre_fn, "rw_pre", rw_pre_tiles, rw_pre_params,
                                         [(RW_WIDTH, F32)] * 7, 128)
    rw_arrs = [(r, 0), (lw, 0), (k, 0), (v, 0), (al, 0), (be, 0)]
    with_rw = ("ffn_w2", "w_out", "xa_wq", "xa_wk", "xa_wv", "xa_wo")
    y, st_rw, got = _scan_fwd(_rw_chunk, "rw_scan", rw_arrs, RW_WIDTH // LANES,
                              _gather_plan({n: late[n] for n in with_rw}))
    wt.update(_gather_finish(with_rw, got))
    rw_post_tiles = [(t, RW_WIDTH, 0) for t in (y, r, k, v, gate)]
    rw_post_params = [wt["rw_ln_w"], wt["rw_ln_b"], wt["rw_r_k"]]
    o_rw = _row_fwd(_rw_post_fn, "rw_post", rw_post_tiles, rw_post_params, [(RW_WIDTH, BF16)], 128)[0]
    o_cat = jnp.concatenate([o_dn, o_rw], axis=1)
    h1 = _matmul("out_proj", o_cat, wt["w_out"], "nn", [F32], _add_epilogue, (x,))[0]

    hn = _row_fwd(_rms_fn, "xa_norm", [(h1, d, 0)], [wt["xa_norm_w"]], [(d, BF16)], 256)[0]
    mn = _row_fwd(_rms_fn, "mem_norm", [(mem, d, 0)], [wt["mem_norm_w"]], [(d, BF16)], 256)[0]
    q = _matmul("xa_q", hn, wt["xa_wq"], "nn", [F32])[0]
    kx = _matmul("xa_k", mn, wt["xa_wk"], "nn", [F32])[0]
    vx = _matmul("xa_v", mn, wt["xa_wv"], "nn", [F32])[0]
    ao = _row_fwd(_xattn_fn, "xattn", [(q, XA_WIDTH, 0)], [kx, vx], [(XA_WIDTH, BF16)], 256)[0]
    h2 = _matmul("xa_o", ao, wt["xa_wo"], "nn", [F32], _add_epilogue, (h1,))[0]

    f = _row_fwd(_rms_fn, "ffn_norm", [(h2, d, 0)], [wt["ffn_norm_w"]], [(d, BF16)], 256)[0]
    a, hid = _matmul("ffn_up", f, wt["ffn_w1"], "nn", [F32, BF16],
                     lambda acc: (acc, jnp.square(jnp.maximum(acc, 0.0))))
    h3 = _matmul("ffn_down", hid, wt["ffn_w2"], "nn", [F32], _add_epilogue, (h2,))[0]
    loss8, dh3, g["final_norm_w"] = _loss_call(h3, target, wt["final_norm_w"])

    da = _matmul("ffn_down_dx", dh3, wt["ffn_w2"], "nt", [BF16],
                 lambda acc, av: (acc * 2.0 * jnp.maximum(av, 0.0),), (a,))[0]
    g["ffn_w2"] = _matmul("ffn_down_dw", hid, dh3, "tn", [BF16])[0]
    g["ffn_w1"] = _matmul("ffn_up_dw", f, da, "tn", [BF16])[0]
    df = _matmul("ffn_up_dx", da, wt["ffn_w1"], "nt", [F32])[0]
    (dh2,), (g["ffn_norm_w"],) = _row_bwd(_rms_res_fn, "ffn_norm_bwd", [(h2, d, 0)], [wt["ffn_norm_w"]],
                                          [[(df, d, 0)], [(dh3, d, 0)]], 256)

    dao = _matmul("xa_o_dx", dh2, wt["xa_wo"], "nt", [F32])[0]
    g["xa_wo"] = _matmul("xa_o_dw", ao, dh2, "tn", [BF16])[0]
    (dq,), (dkx, dvx) = _row_bwd(_xattn_fn, "xattn_bwd", [(q, XA_WIDTH, 0)], [kx, vx], [[(dao, XA_WIDTH, 0)]], 256)
    dhn = _matmul("xa_q_dx", dq, wt["xa_wq"], "nt", [F32])[0]
    g["xa_wq"] = _matmul("xa_q_dw", hn, dq, "tn", [BF16])[0]
    g["xa_wk"] = _matmul("xa_k_dw", mn, dkx, "tn", [BF16])[0]
    g["xa_wv"] = _matmul("xa_v_dw", mn, dvx, "tn", [BF16])[0]
    dmn = _matmul("xa_k_dx", dkx, wt["xa_wk"], "nt", [F32])[0]
    dmn = _matmul("xa_v_dx", dvx, wt["xa_wv"], "nt", [F32], _add_epilogue, (dmn,))[0]
    _, (g["mem_norm_w"],) = _row_bwd(_rms_fn, "mem_norm_bwd", [(mem, d, 0)], [wt["mem_norm_w"]],
                                     [[(dmn, d, 0)]], 256, want_tiles=())
    (dh1,), (g["xa_norm_w"],) = _row_bwd(_rms_res_fn, "xa_norm_bwd", [(h1, d, 0)], [wt["xa_norm_w"]],
                                         [[(dhn, d, 0)], [(dh2, d, 0)]], 256)

    do_cat = _matmul("out_proj_dx", dh1, wt["w_out"], "nt", [F32])[0]
    g["w_out"] = _matmul("out_proj_dw", o_cat, dh1, "tn", [BF16])[0]

    (dy, dr1, dk1, dv1, dgate), (g["rw_ln_w"], g["rw_ln_b"], g["rw_r_k"]) = _row_bwd(
        _rw_post_fn, "rw_post_bwd", rw_post_tiles, rw_post_params, [[(do_cat, RW_WIDTH, 1)]], 128)
    parts = {}
    with_rw_bwd = ("ffn_w1", "ffn_w2")
    (dr2, dlw, dk2, dv2, dal, dbe), got = _scan_bwd(_rw_chunk, "rw_scan_bwd", rw_arrs, st_rw, dy, RW_WIDTH // LANES,
                                                    _scatter_plan({n: g.pop(n) for n in with_rw_bwd}))
    parts.update(zip(with_rw_bwd, got))
    one = lambda t: [(t, RW_WIDTH, 0)]
    two = lambda s, t: [(s, RW_WIDTH, 0), (t, RW_WIDTH, 0)]
    d_ps, rw_pre_grads = _row_bwd(
        _rw_pre_fn, "rw_pre_bwd", rw_pre_tiles, rw_pre_params,
        [two(dr1, dr2), one(dlw), two(dk1, dk2), two(dv1, dv2), one(dal), one(dbe), one(dgate)], 128)
    for n, val in zip(("rw_w0", "rw_a0", "rw_k_k", "rw_k_a", "rw_w2", "rw_a2", "rw_g2"), rw_pre_grads):
        g[n] = val
    dp_rw, (g["rw_mu"],) = _col_bwd(_lerp_fn, "rw_shift_bwd", p, RW_OFF // LANES, 26, [wt["rw_mu"]],
                                    jnp.concatenate(d_ps, axis=1))

    (do, dz), (g["dn_norm_w"],) = _row_bwd(_dn_post_fn, "dn_post_bwd", dn_post_tiles, [wt["dn_norm_w"]],
                                           [[(do_cat, DN_WIDTH, 0)]], 256)
    with_gdn_bwd = ("w_out", "xa_wq", "xa_wk", "xa_wv", "xa_wo")
    (dqh, dkh, dv_dn, dgb, dbb), got = _scan_bwd(_gdn_chunk, "gdn_scan_bwd", dn_arrs, st_dn, do, DN_HEADS,
                                                 _scatter_plan({n: g.pop(n) for n in with_gdn_bwd}))
    parts.update(zip(with_gdn_bwd, got))
    one = lambda t: [(t, DN_WIDTH, 0)]
    (dcq, dck, dgates), (g["dn_a_log"], g["dn_dt_bias"]) = _row_bwd(
        _dn_pre_fn, "dn_pre_bwd", dn_pre_tiles, dn_pre_params, [one(dqh), one(dkh), one(dgb), one(dbb)], 128)
    dp_qkv, (g["dn_conv_w"],) = _col_bwd(_conv_fn, "dn_conv_bwd", p, 0, 24, [wt["dn_conv_w"]],
                                         jnp.concatenate([dcq, dck, dv_dn], axis=1))
    dp = jnp.concatenate([dp_qkv, dz, dgates, dp_rw, jnp.zeros((x.shape[0], LANES), F32)], axis=1).astype(BF16)
    du = _matmul("in_proj_dx", dp, wt["w_in"], "nt", [F32])[0]
    g["w_in"] = _matmul("in_proj_dw", u, dp, "tn", [BF16], tn=1536)[0]
    (dx,), (g["mix_norm_w"],) = _row_bwd(_rms_res_fn, "mix_norm_bwd", [(x, d, 0)], [wt["mix_norm_w"]],
                                         [[(du, d, 0)], [(dh1, d, 0)]], 256)
    return loss8, dx, g, parts


WEIGHTS = ["mix_norm_w", "w_in", "dn_conv_w", "dn_a_log", "dn_dt_bias", "dn_norm_w", "rw_mu", "rw_w0", "rw_w2",
           "rw_a0", "rw_a2", "rw_g2", "rw_k_k", "rw_k_a", "rw_r_k", "rw_ln_w", "rw_ln_b", "w_out", "xa_norm_w",
           "mem_norm_w", "xa_wq", "xa_wk", "xa_wv", "xa_wo", "ffn_norm_w", "ffn_w1", "ffn_w2", "final_norm_w"]
SHARDED = {"w_in": True, "w_out": False, "xa_wq": False, "xa_wk": False, "xa_wv": False, "xa_wo": True,
           "ffn_w1": True, "ffn_w2": False, "dn_conv_w": True, "rw_w2": True, "rw_a2": True, "rw_g2": True}
BF16_PAYLOAD = ("w_in", "w_out", "xa_wq", "xa_wk", "xa_wv", "xa_wo", "ffn_w1", "ffn_w2")
REPLICATED = [n for n in WEIGHTS if n not in SHARDED]
EARLY = ("w_in", "dn_conv_w", "rw_w2", "rw_a2", "rw_g2")
RW_IN_COLS = IN_COLS - DN_COLS


def _layout_weights(fw):
    wt = dict(fw)
    w_in = fw["w_in"]
    rows = w_in.shape[0]
    wt["w_in"] = jnp.concatenate(
        [w_in[:, :DN_COLS], jnp.zeros((rows, RW_OFF - DN_COLS), w_in.dtype), w_in[:, DN_COLS:],
         jnp.zeros((rows, IN_PAD - RW_OFF - RW_IN_COLS), w_in.dtype)], axis=1)
    wt["dn_conv_w"] = jnp.pad(fw["dn_conv_w"], ((0, 4), (0, 0)))
    wt["dn_a_log"] = jnp.pad(fw["dn_a_log"], ((0, 0), (0, LANES - DN_HEADS)))
    wt["dn_dt_bias"] = jnp.pad(fw["dn_dt_bias"], ((0, 0), (0, LANES - DN_HEADS)))
    wt["rw_w2"] = jnp.pad(fw["rw_w2"], ((0, 64), (0, 0)))
    wt["rw_a2"] = jnp.pad(fw["rw_a2"], ((64, 0), (0, 0)))
    return wt


def _logical_grads(g):
    out = dict(g)
    out["w_in"] = jnp.concatenate([g["w_in"][:, :DN_COLS], g["w_in"][:, RW_OFF:RW_OFF + RW_IN_COLS]], axis=1)
    out["dn_conv_w"] = g["dn_conv_w"][:4]
    out["dn_a_log"] = g["dn_a_log"][:, :DN_HEADS]
    out["dn_dt_bias"] = g["dn_dt_bias"][:, :DN_HEADS]
    out["rw_w2"] = g["rw_w2"][:64]
    out["rw_a2"] = g["rw_a2"][64:]
    return out


def _pack(vals):
    parts = []
    for v in vals:
        flat = v.reshape(-1)
        parts.append(jnp.pad(flat, (0, -flat.shape[0] % LANES)))
    flat = jnp.concatenate(parts)
    flat = jnp.pad(flat, (0, -flat.shape[0] % (8 * LANES)))
    return flat.reshape(-1, LANES)


def _unpack(packed, shapes):
    flat = packed.reshape(-1)
    out, at = [], 0
    for shp in shapes:
        size = math.prod(shp)
        out.append(flat[at:at + size].reshape(shp))
        at += size + (-size % LANES)
    return out


def kernel(x, mem, mix_norm_w, w_in, dn_conv_w, dn_a_log, dn_dt_bias, dn_norm_w, rw_mu, rw_w0, rw_w2, rw_a0, rw_a2, rw_g2, rw_k_k, rw_k_a, rw_r_k, rw_ln_w, rw_ln_b, w_out, xa_norm_w, mem_norm_w, xa_wq, xa_wk, xa_wv, xa_wo, ffn_norm_w, ffn_w1, ffn_w2, final_norm_w, loss_target, m_mix_norm_w, m_w_in, m_dn_conv_w, m_dn_a_log, m_dn_dt_bias, m_dn_norm_w, m_rw_mu, m_rw_w0, m_rw_w2, m_rw_a0, m_rw_a2, m_rw_g2, m_rw_k_k, m_rw_k_a, m_rw_r_k, m_rw_ln_w, m_rw_ln_b, m_w_out, m_xa_norm_w, m_mem_norm_w, m_xa_wq, m_xa_wk, m_xa_wv, m_xa_wo, m_ffn_norm_w, m_ffn_w1, m_ffn_w2, m_final_norm_w, v_mix_norm_w, v_w_in, v_dn_conv_w, v_dn_a_log, v_dn_dt_bias, v_dn_norm_w, v_rw_mu, v_rw_w0, v_rw_w2, v_rw_a0, v_rw_a2, v_rw_g2, v_rw_k_k, v_rw_k_a, v_rw_r_k, v_rw_ln_w, v_rw_ln_b, v_w_out, v_xa_norm_w, v_mem_norm_w, v_xa_wq, v_xa_wk, v_xa_wv, v_xa_wo, v_ffn_norm_w, v_ffn_w1, v_ffn_w2, v_final_norm_w):
    given = dict(locals())
    w = {n: given[n] for n in WEIGHTS}
    m = {n: given["m_" + n] for n in WEIGHTS}
    v = {n: given["v_" + n] for n in WEIGHTS}

    shards = {n: (w[n][0].astype(BF16) if n in BF16_PAYLOAD else w[n][0]) for n in SHARDED}
    srcs, dsts, _ = _gather_plan({n: shards[n] for n in EARLY})
    full = _gather_finish(EARLY, _exchange("early_all_gather", srcs, dsts, True))
    for n in REPLICATED:
        full[n] = w[n].reshape(1, -1)

    loss8, dx, g, parts = _local_step(x[0], mem[0], loss_target[0], _layout_weights(full),
                                      {n: shards[n] for n in SHARDED if n not in EARLY})
    g = _logical_grads(g)
    loss = lax.psum(loss8[0, 0], ("x", "y", "c"))

    srcs, dsts, _ = _scatter_plan({n: g[n] for n in EARLY})
    parts.update(zip(EARLY, _exchange("early_grad_all_to_all", srcs, dsts, False)))
    grad, delta, new_m, new_v = {}, {}, {}, {}
    for n in SHARDED:
        res = _sum_adamw("adamw_" + n, parts[n], w[n][0], m[n][0], v[n][0])
        grad[n], delta[n], new_m[n], new_v[n] = [t[None] for t in res]

    packed = _pack([g[n] for n in REPLICATED])
    parts = _exchange("small_all_gather", [(packed, None)], [((N_DEV,) + packed.shape, F32, None)], True)[0]
    res = _sum_adamw("adamw_small", parts, _pack([w[n] for n in REPLICATED]),
                     _pack([m[n] for n in REPLICATED]), _pack([v[n] for n in REPLICATED]))
    shapes = [w[n].shape for n in REPLICATED]
    for store, packed_out in zip((grad, delta, new_m, new_v), res):
        for n, val in zip(REPLICATED, _unpack(packed_out, shapes)):
            store[n] = val

    return (loss, dx[None], *[grad[n] for n in WEIGHTS], *[delta[n] for n in WEIGHTS],
            *[new_m[n] for n in WEIGHTS], *[new_v[n] for n in WEIGHTS])
```

```python
import functools
import math

import jax
import jax.numpy as jnp
from jax import lax
from jax.experimental import pallas as pl
from jax.experimental.pallas import tpu as pltpu

F32 = jnp.float32
BF16 = jnp.bfloat16
SDS = jax.ShapeDtypeStruct

N_DEV = 8
D_MODEL = 2048
LANES = 128
CHUNK = 128
DN_HEADS = 8
DN_WIDTH = 1024
RW_WIDTH = 1024
RW_HEAD = 64
XA_HEADS = 4
XA_WIDTH = 512
FFN_HIDDEN = 8192
IN_COLS = 7440
DN_COLS = 4112
IN_PAD = 7680
RW_OFF = 4224
RMS_EPS = 1e-6
RW_GN_EPS = 64e-5
VMEM_LIMIT = 56 * 1024 * 1024

ADAM_LR = 0.001
ADAM_B1 = 0.9
ADAM_B2 = 0.999
ADAM_EPS = 1e-08
ADAM_WD = 0.01
ADAM_STEP = 10

_DIMS = {"nn": (((1,), (0,)), ((), ())), "nt": (((1,), (1,)), ((), ())), "tn": (((0,), (0,)), ((), ()))}


def _raw_dot(a, b, mode, hi):
    if hi:
        return lax.dot_general(a, b, _DIMS[mode], precision=lax.Precision.HIGHEST, preferred_element_type=F32)
    return lax.dot_general(a.astype(BF16), b.astype(BF16), _DIMS[mode], preferred_element_type=F32)


@functools.partial(jax.custom_vjp, nondiff_argnums=(2, 3))
def mm(a, b, mode="nn", hi=False):
    return _raw_dot(a, b, mode, hi)


def _mm_fwd(a, b, mode, hi):
    return _raw_dot(a, b, mode, hi), (a, b)


def _mm_bwd(mode, hi, res, g):
    a, b = res
    if mode == "nn":
        return _raw_dot(g, b, "nt", hi), _raw_dot(a, g, "tn", hi)
    if mode == "nt":
        return _raw_dot(g, b, "nn", hi), _raw_dot(g, a, "tn", hi)
    return _raw_dot(b, g, "nt", hi), _raw_dot(a, g, "nn", hi)


mm.defvjp(_mm_fwd, _mm_bwd)


def _shift_rows_raw(x, k):
    n = x.shape[0]
    rolled = pltpu.roll(x, k % n, axis=0)
    row = lax.broadcasted_iota(jnp.int32, x.shape, 0)
    keep = row >= k if k > 0 else row < n + k
    return jnp.where(keep, rolled, 0.0)


@functools.partial(jax.custom_vjp, nondiff_argnums=(1,))
def shift_rows(x, k):
    return _shift_rows_raw(x, k)


shift_rows.defvjp(lambda x, k: (_shift_rows_raw(x, k), None), lambda k, _, g: (_shift_rows_raw(g, -k),))


def _softplus(x):
    return jnp.maximum(x, 0.0) + jnp.log(1.0 + jnp.exp(-jnp.abs(x)))


def _sigmoid(x):
    return 1.0 / (1.0 + jnp.exp(-x))


def _silu(x):
    return x * _sigmoid(x)


def _tri_masks(n):
    ii = lax.broadcasted_iota(jnp.int32, (n, n), 0)
    jj = lax.broadcasted_iota(jnp.int32, (n, n), 1)
    return ii >= jj, ii > jj, ii == jj


def _neumann_inv_raw(m):
    n = m.shape[0]
    _, _, eye = _tri_masks(n)
    eye = jnp.where(eye, 1.0, 0.0)
    p = eye + m
    mk = m
    for _ in range(int(math.log2(n)) - 1):
        mk = _raw_dot(mk, mk, "nn", False)
        p = p + _raw_dot(p, mk, "nn", False)
    resid = eye - p + _raw_dot(m, p, "nn", True)
    return p + _raw_dot(p, resid, "nn", False)


@jax.custom_vjp
def _neumann_inv(m):
    return _neumann_inv_raw(m)


def _neumann_inv_fwd(m):
    p = _neumann_inv_raw(m)
    return p, p


def _neumann_inv_bwd(p, g):
    return (_raw_dot(_raw_dot(p, g, "tn", False), p, "nt", False),)


_neumann_inv.defvjp(_neumann_inv_fwd, _neumann_inv_bwd)


def _gdn_chunk(s0, q, k, v, gb, bb):
    c = q.shape[0]
    causal, strict, _ = _tri_masks(c)
    gc = mm(jnp.where(causal, 1.0, 0.0), gb, "nn", True)
    diff = gc - gc.T
    decay = jnp.exp(jnp.where(causal, diff, -jnp.inf))
    kb = k * bb
    a = jnp.where(strict, mm(kb, k, "nt") * decay, 0.0)
    p = _neumann_inv(-a)
    u = mm(p, v * bb)
    w = mm(p, kb * jnp.exp(gc))
    attn = mm(q, k, "nt") * decay
    v_new = u - mm(w, s0)
    o = mm(q * jnp.exp(gc), s0) + mm(attn, v_new)
    g_last = jnp.sum(gb, axis=0, keepdims=True)
    s1 = s0 * jnp.exp(g_last) + mm(k * jnp.exp(g_last - gc), v_new, "tn")
    return o, s1


def _rw_chunk(s0, r, lw, k, v, al, be):
    c = r.shape[0]
    causal, strict, _ = _tri_masks(c)
    gc = mm(jnp.where(causal, 1.0, 0.0), lw, "nn", True)
    gp = gc - lw
    row = lax.broadcasted_iota(jnp.int32, lw.shape, 0)
    lane = lax.broadcasted_iota(jnp.int32, lw.shape, 1)
    g_mid = jnp.sum(jnp.where(row < c // 2, lw, 0.0), axis=0, keepdims=True)
    g_last = jnp.sum(lw, axis=0, keepdims=True)
    e_n = jnp.exp(g_mid - gc)
    rg = r * jnp.exp(gc - g_mid)
    bg = be * jnp.exp(gp - g_mid)
    an = al * e_n
    kn = k * e_n
    bt = mm(be * jnp.exp(gp), s0, "nt")
    rt = mm(r * jnp.exp(gc), s0, "nt")
    us, ys = [], []
    for h in range(2):
        mine = (lane >= RW_HEAD) if h else (lane < RW_HEAD)
        bgh = jnp.where(mine, bg, 0.0)
        rgh = jnp.where(mine, rg, 0.0)
        a_ab = jnp.where(strict, mm(bgh, an, "nt"), 0.0)
        a_kb = jnp.where(strict, mm(bgh, kn, "nt"), 0.0)
        a_ra = jnp.where(causal, mm(rgh, an, "nt"), 0.0)
        a_rk = jnp.where(causal, mm(rgh, kn, "nt"), 0.0)
        p = _neumann_inv(a_ab)
        u_h = mm(p, bt + mm(a_kb, v))
        us.append(u_h)
        ys.append(rt + mm(a_ra, u_h) + mm(a_rk, v))
    lo = lane < RW_HEAD
    u = jnp.where(lo, us[0], us[1])
    y = jnp.where(lo, ys[0], ys[1])
    tail = jnp.exp(g_last - gc)
    s1 = s0 * jnp.exp(g_last) + mm(u, al * tail, "tn") + mm(v, k * tail, "tn")
    vi = lax.broadcasted_iota(jnp.int32, s0.shape, 0)
    ki = lax.broadcasted_iota(jnp.int32, s0.shape, 1)
    s1 = jnp.where((vi < RW_HEAD) == (ki < RW_HEAD), s1, 0.0)
    return y, s1


SCAN_HB = 4


def _scan_specs(arrs, n_chunks, reverse):
    def spec(off):
        assert off % SCAN_HB == 0
        if reverse:
            return pl.BlockSpec((CHUNK, SCAN_HB * LANES), lambda h, n: (n_chunks - 1 - n, off // SCAN_HB + h))
        return pl.BlockSpec((CHUNK, SCAN_HB * LANES), lambda h, n: (n, off // SCAN_HB + h))
    return [spec(off) for _, off in arrs]


def _split_heads(x):
    return jnp.stack([x[:, LANES * j:LANES * (j + 1)] for j in range(SCAN_HB)], axis=0)


def _merge_heads(x):
    return jnp.concatenate([x[j] for j in range(SCAN_HB)], axis=1)


def _hosted(xfer, heads, n_chunks):
    if xfer is None:
        return 0, [], [], [], [], lambda refs: None
    srcs, dsts, gather = xfer
    n = len(srcs)
    any_spec = pl.BlockSpec(memory_space=pl.ANY)

    def run(src_refs, out_refs, sems):
        start, wait = _exchange_ops([c for _, c in srcs], [c for _, _, c in dsts], gather, src_refs, out_refs, *sems)
        h, c = pl.program_id(0), pl.program_id(1)
        pl.when(jnp.logical_and(h == 0, c == 0))(start)
        pl.when(jnp.logical_and(h == heads - 1, c == n_chunks - 1))(wait)

    return (n, [any_spec] * n, [any_spec] * n, [SDS(shape, dt) for shape, dt, _ in dsts],
            _exchange_sems(n), run)


def _scan_fwd(chunk_fn, name, arrs, heads, xfer=None):
    s = arrs[0][0].shape[0]
    n_chunks = s // CHUNK
    n_in = len(arrs)
    groups = heads // SCAN_HB
    n_x, x_in, x_out, x_shapes, x_sems, run_x = _hosted(xfer, groups, n_chunks)

    def body(*refs):
        y_ref, st_ref = refs[n_in + n_x:n_in + n_x + 2]
        s_scr = refs[n_in + 2 * n_x + 2]
        if n_x:
            run_x(refs[n_in:n_in + n_x], refs[n_in + n_x + 2:n_in + 2 * n_x + 2], refs[n_in + 2 * n_x + 3:])

        @pl.when(pl.program_id(1) == 0)
        def _():
            s_scr[...] = jnp.zeros_like(s_scr)

        s0 = s_scr[...]
        st_ref[...] = s0
        y, s1 = jax.vmap(chunk_fn)(s0, *[_split_heads(r[...]) for r in refs[:n_in]])
        y_ref[...] = _merge_heads(y)
        s_scr[...] = s1

    wide = SCAN_HB * LANES
    res = pl.pallas_call(
        body, grid=(groups, n_chunks), name=name,
        in_specs=_scan_specs(arrs, n_chunks, False) + x_in,
        out_specs=[pl.BlockSpec((CHUNK, wide), lambda h, n: (n, h)),
                   pl.BlockSpec((SCAN_HB, None, LANES, LANES), lambda h, n: (h, n, 0, 0))] + x_out,
        out_shape=[SDS((s, heads * LANES), F32), SDS((heads, n_chunks, LANES, LANES), F32)] + x_shapes,
        scratch_shapes=[pltpu.VMEM((SCAN_HB, LANES, LANES), F32)] + x_sems,
        compiler_params=pltpu.CompilerParams(dimension_semantics=("arbitrary", "arbitrary")),
    )(*[a for a, _ in arrs], *([a for a, _ in xfer[0]] if xfer else []))
    return res[0], res[1], res[2:]


def _scan_bwd(chunk_fn, name, arrs, states, dy, heads, xfer=None):
    s = arrs[0][0].shape[0]
    n_chunks = s // CHUNK
    n_in = len(arrs)
    groups = heads // SCAN_HB
    n_x, x_in, x_out, x_shapes, x_sems, run_x = _hosted(xfer, groups, n_chunks)

    def body(*refs):
        st_ref, dy_ref = refs[n_in:n_in + 2]
        first_out = n_in + 2 + n_x
        d_refs = refs[first_out:first_out + n_in]
        ds_scr = refs[first_out + n_in + n_x]
        if n_x:
            run_x(refs[n_in + 2:first_out], refs[first_out + n_in:first_out + n_in + n_x],
                  refs[first_out + n_in + n_x + 1:])

        @pl.when(pl.program_id(1) == 0)
        def _():
            ds_scr[...] = jnp.zeros_like(ds_scr)

        _, vjp = jax.vjp(jax.vmap(chunk_fn), st_ref[...], *[_split_heads(r[...]) for r in refs[:n_in]])
        grads = vjp((_split_heads(dy_ref[...]), ds_scr[...]))
        ds_scr[...] = grads[0]
        for ref, g in zip(d_refs, grads[1:]):
            ref[...] = _merge_heads(g)

    wide = SCAN_HB * LANES
    rev = pl.BlockSpec((CHUNK, wide), lambda h, n: (n_chunks - 1 - n, h))
    res = pl.pallas_call(
        body, grid=(groups, n_chunks), name=name,
        in_specs=_scan_specs(arrs, n_chunks, True)
        + [pl.BlockSpec((SCAN_HB, None, LANES, LANES), lambda h, n: (h, n_chunks - 1 - n, 0, 0)), rev] + x_in,
        out_specs=[rev] * n_in + x_out,
        out_shape=[SDS((s, heads * LANES), F32)] * n_in + x_shapes,
        scratch_shapes=[pltpu.VMEM((SCAN_HB, LANES, LANES), F32)] + x_sems,
        compiler_params=pltpu.CompilerParams(dimension_semantics=("arbitrary", "arbitrary")),
    )(*[a for a, _ in arrs], states, dy, *([a for a, _ in xfer[0]] if xfer else []))
    return res[:n_in], res[n_in:]


def _col_spec(tr, width, cb):
    return pl.BlockSpec((tr, width), lambda i: (i, cb))


def _whole(p):
    return pl.BlockSpec(p.shape, lambda i: (0,) * p.ndim)


def _row_fwd(fn, name, tiles, params, outs, tr):
    rows = tiles[0][0].shape[0]
    nt, npar = len(tiles), len(params)

    def body(*refs):
        vals = [r[...].astype(F32) for r in refs[:nt + npar]]
        for ref, o in zip(refs[nt + npar:], fn(*vals)):
            ref[...] = o.astype(ref.dtype)

    return pl.pallas_call(
        body, grid=(rows // tr,), name=name,
        in_specs=[_col_spec(tr, w, cb) for _, w, cb in tiles] + [_whole(p) for p in params],
        out_specs=[_col_spec(tr, w, 0) for w, _ in outs],
        out_shape=[SDS((rows, w), dt) for w, dt in outs],
        compiler_params=pltpu.CompilerParams(dimension_semantics=("arbitrary",), vmem_limit_bytes=VMEM_LIMIT),
    )(*[a for a, _, _ in tiles], *params)


def _row_bwd(fn, name, tiles, params, cts, tr, want_tiles=None):
    rows = tiles[0][0].shape[0]
    nt, npar = len(tiles), len(params)
    want = list(range(nt)) if want_tiles is None else list(want_tiles)
    flat_cts = [c for group in cts for c in group]
    n_ct = len(flat_cts)

    def body(*refs):
        vals = [r[...].astype(F32) for r in refs[:nt + npar]]
        ct_refs = refs[nt + npar:nt + npar + n_ct]
        out_refs = refs[nt + npar + n_ct:]
        ct_vals, at = [], 0
        for group in cts:
            total = ct_refs[at][...].astype(F32)
            for r in ct_refs[at + 1:at + len(group)]:
                total = total + r[...].astype(F32)
            ct_vals.append(total)
            at += len(group)
        _, vjp = jax.vjp(lambda *a: tuple(fn(*a)), *vals)
        grads = vjp(tuple(ct_vals))
        for ref, t in zip(out_refs[:len(want)], want):
            ref[...] = grads[t]
        first = pl.program_id(0) == 0
        for ref, g in zip(out_refs[len(want):], grads[nt:]):
            @pl.when(first)
            def _(ref=ref, g=g):
                ref[...] = g

            @pl.when(jnp.logical_not(first))
            def _(ref=ref, g=g):
                ref[...] += g

    res = pl.pallas_call(
        body, grid=(rows // tr,), name=name,
        in_specs=[_col_spec(tr, w, cb) for _, w, cb in tiles] + [_whole(p) for p in params]
        + [_col_spec(tr, w, cb) for _, w, cb in flat_cts],
        out_specs=[_col_spec(tr, tiles[t][1], 0) for t in want] + [_whole(p) for p in params],
        out_shape=[SDS((rows, tiles[t][1]), F32) for t in want] + [SDS(p.shape, F32) for p in params],
        compiler_params=pltpu.CompilerParams(dimension_semantics=("arbitrary",), vmem_limit_bytes=VMEM_LIMIT),
    )(*[a for a, _, _ in tiles], *params, *[a for a, _, _ in flat_cts])
    return res[:len(want)], res[len(want):]


def _col_fwd(fn, name, x, first_block, n_blocks, params):
    rows = x.shape[0]

    def body(*refs):
        refs[-1][...] = fn(*[r[...] for r in refs[:-1]])

    return pl.pallas_call(
        body, grid=(n_blocks,), name=name,
        in_specs=[pl.BlockSpec((rows, LANES), lambda j: (0, first_block + j))]
        + [pl.BlockSpec((p.shape[0], LANES), lambda j: (0, j)) for p in params],
        out_specs=pl.BlockSpec((rows, LANES), lambda j: (0, j)),
        out_shape=SDS((rows, n_blocks * LANES), F32),
        compiler_params=pltpu.CompilerParams(dimension_semantics=("arbitrary",), vmem_limit_bytes=VMEM_LIMIT),
    )(x, *params)


def _col_bwd(fn, name, x, first_block, n_blocks, params, dy):
    rows = x.shape[0]
    npar = len(params)

    def body(*refs):
        vals = [r[...] for r in refs[:1 + npar]]
        _, vjp = jax.vjp(fn, *vals)
        grads = vjp(refs[1 + npar][...])
        for ref, g in zip(refs[2 + npar:], grads):
            ref[...] = g

    pspecs = [pl.BlockSpec((p.shape[0], LANES), lambda j: (0, j)) for p in params]
    blk = pl.BlockSpec((rows, LANES), lambda j: (0, j))
    res = pl.pallas_call(
        body, grid=(n_blocks,), name=name,
        in_specs=[pl.BlockSpec((rows, LANES), lambda j: (0, first_block + j))] + pspecs + [blk],
        out_specs=[blk] + pspecs,
        out_shape=[SDS((rows, n_blocks * LANES), F32)] + [SDS(p.shape, F32) for p in params],
        compiler_params=pltpu.CompilerParams(dimension_semantics=("arbitrary",), vmem_limit_bytes=VMEM_LIMIT),
    )(x, *params, dy)
    return res[0], res[1:]


def _conv_fn(x, w):
    acc = x * w[3:4, :]
    for j in range(3):
        acc = acc + shift_rows(x, 3 - j) * w[j:j + 1, :]
    return _silu(acc)


def _lerp_fn(x, mu):
    return x + (shift_rows(x, 1) - x) * mu[0:1, :]


def _seg_sum(x, width):
    if width == LANES:
        return jnp.sum(x, axis=1, keepdims=True)
    lo = lax.broadcasted_iota(jnp.int32, x.shape, 1) < width
    s0 = jnp.sum(jnp.where(lo, x, 0.0), axis=1, keepdims=True)
    s1 = jnp.sum(jnp.where(lo, 0.0, x), axis=1, keepdims=True)
    return jnp.where(lo, s0, s1)


def _per_block(fn, *xs):
    n = xs[0].shape[1] // LANES
    return jnp.concatenate([fn(*[x[:, LANES * b:LANES * (b + 1)] for x in xs]) for b in range(n)], axis=1)


def _head_expand(col0):
    r = lax.broadcasted_iota(jnp.int32, (LANES, DN_WIDTH), 0)
    c = lax.shift_right_logical(lax.broadcasted_iota(jnp.int32, (LANES, DN_WIDTH), 1), 7)
    return jnp.where(r == c + col0, 1.0, 0.0)


def _dn_pre_fn(cq, ck, gates, a_log, dt_bias):
    l2 = lambda x: x * lax.rsqrt(_seg_sum(x * x, LANES) + 1e-6)
    qh = _per_block(l2, cq) * (LANES ** -0.5)
    kh = _per_block(l2, ck)
    g = -jnp.exp(a_log) * _softplus(gates + dt_bias)
    gb = mm(g, _head_expand(0), "nn", True)
    bb = mm(_sigmoid(gates), _head_expand(DN_HEADS), "nn", True)
    return qh, kh, gb, bb


def _dn_post_fn(o, z, nw):
    def one(ob, zb):
        return ob * lax.rsqrt(_seg_sum(ob * ob, LANES) * (1.0 / LANES) + RMS_EPS) * nw * _silu(zb)
    return (_per_block(one, o, z),)


def _rw_pre_fn(pr, pk, pv, pwa, pg, w0, a0, k_k, k_a, w2p, a2p, g2):
    log_w = -_softplus(-(w0 + mm(jnp.tanh(pwa), w2p))) - 0.5
    lw = -jnp.exp(log_w)
    a = _sigmoid(a0 + mm(pwa, a2p))
    gate = mm(_sigmoid(pg), g2)
    kk = pk * k_k
    kk = _per_block(lambda x: x / jnp.maximum(jnp.sqrt(_seg_sum(x * x, RW_HEAD)), 1e-12), kk)
    k = pk * (1.0 + (a - 1.0) * k_a)
    return pr, lw, k, pv, kk * a, -kk, gate


def _rw_post_fn(y, r, k, v, gate, ln_w, ln_b, r_k):
    def one(yb, rb, kb, vb, gb, wb, bb, rkb):
        d = yb - _seg_sum(yb, RW_HEAD) * (1.0 / RW_HEAD)
        var = _seg_sum(d * d, RW_HEAD) * (1.0 / RW_HEAD)
        yn = d * lax.rsqrt(var + RW_GN_EPS) * wb + bb
        return (yn + _seg_sum(rb * kb * rkb, RW_HEAD) * vb) * gb
    return (_per_block(one, y, r, k, v, gate, ln_w, ln_b, r_k),)


def _rms_fn(h, w):
    return (h * lax.rsqrt(jnp.mean(h * h, axis=1, keepdims=True) + RMS_EPS) * w,)


def _xattn_fn(q, k, v):
    outs = []
    for h in range(XA_HEADS):
        sl = slice(LANES * h, LANES * (h + 1))
        s = mm(q[:, sl], k[:, sl], "nt") * (LANES ** -0.5)
        e = jnp.exp(s - jnp.max(s, axis=1, keepdims=True))
        outs.append(mm(e / jnp.sum(e, axis=1, keepdims=True), v[:, sl]))
    return (jnp.concatenate(outs, axis=1),)


def _fit(tile, dim):
    best = [t for t in range(LANES, min(tile, dim) + 1, LANES) if dim % t == 0]
    assert best, (tile, dim)
    return best[-1]


def _matmul(name, a, b, mode, out_dtypes, epilogue=None, extras=(), tm=1024, tn=1024, tk=2048):
    if mode == "tn":
        (k_dim, m), n = a.shape, b.shape[1]
    else:
        (m, k_dim), n = a.shape, (b.shape[1] if mode == "nn" else b.shape[0])
    tm, tn, tk = _fit(tm, m), _fit(tn, n), _fit(tk, k_dim)
    nk = k_dim // tk
    a_spec = (pl.BlockSpec((tk, tm), lambda i, j, k: (k, i)) if mode == "tn"
              else pl.BlockSpec((tm, tk), lambda i, j, k: (i, k)))
    b_spec = (pl.BlockSpec((tn, tk), lambda i, j, k: (j, k)) if mode == "nt"
              else pl.BlockSpec((tk, tn), lambda i, j, k: (k, j)))
    o_spec = pl.BlockSpec((tm, tn), lambda i, j, k: (i, j))
    n_ex, n_out = len(extras), len(out_dtypes)

    def finish(total, rest):
        ex = [r[...].astype(F32) for r in rest[:n_ex]]
        res = epilogue(total, *ex) if epilogue else (total,)
        for ref, o in zip(rest[n_ex:n_ex + n_out], res):
            ref[...] = o.astype(ref.dtype)

    def body_single(a_ref, b_ref, *rest):
        finish(_raw_dot(a_ref[...], b_ref[...], mode, False), rest)

    def body_acc(a_ref, b_ref, *rest):
        acc = rest[-1]
        k = pl.program_id(2)

        @pl.when(k == 0)
        def _():
            acc[...] = jnp.zeros_like(acc)

        acc[...] += _raw_dot(a_ref[...], b_ref[...], mode, False)

        @pl.when(k == nk - 1)
        def _():
            finish(acc[...], rest)

    res = pl.pallas_call(
        body_single if nk == 1 else body_acc, grid=(m // tm, n // tn, nk), name=name,
        in_specs=[a_spec, b_spec] + [o_spec] * n_ex,
        out_specs=[o_spec] * n_out,
        out_shape=[SDS((m, n), dt) for dt in out_dtypes],
        scratch_shapes=[] if nk == 1 else [pltpu.VMEM((tm, tn), F32)],
        compiler_params=pltpu.CompilerParams(dimension_semantics=("parallel", "parallel", "arbitrary"),
                                             vmem_limit_bytes=VMEM_LIMIT),
    )(a, b, *extras)
    return res


def _loss_call(h, target, w, tr=256):
    rows, d = h.shape

    def fn(hv, wv, tv):
        y = _rms_fn(hv, wv)[0]
        return 0.5 * jnp.sum(jnp.mean(jnp.square(y - tv), axis=1, keepdims=True), axis=0, keepdims=True)

    def body(h_ref, t_ref, w_ref, loss_ref, dh_ref, dw_ref):
        tv = t_ref[...]
        val, vjp = jax.vjp(lambda hv, wv: fn(hv, wv, tv), h_ref[...], w_ref[...])
        dh, dw = vjp(jnp.ones((1, 1), F32))
        dh_ref[...] = dh
        first = pl.program_id(0) == 0

        @pl.when(first)
        def _():
            loss_ref[...] = jnp.broadcast_to(val, loss_ref.shape)
            dw_ref[...] = dw

        @pl.when(jnp.logical_not(first))
        def _():
            loss_ref[...] += jnp.broadcast_to(val, loss_ref.shape)
            dw_ref[...] += dw

    return pl.pallas_call(
        body, grid=(rows // tr,), name="loss_head",
        in_specs=[_col_spec(tr, d, 0), _col_spec(tr, d, 0), _whole(w)],
        out_specs=[pl.BlockSpec((8, LANES), lambda i: (0, 0)), _col_spec(tr, d, 0), _whole(w)],
        out_shape=[SDS((8, LANES), F32), SDS((rows, d), F32), SDS(w.shape, F32)],
        compiler_params=pltpu.CompilerParams(dimension_semantics=("arbitrary",), vmem_limit_bytes=VMEM_LIMIT),
    )(h, target, w)


def _adamw_vals(w, g, m, v):
    m = ADAM_B1 * m + (1.0 - ADAM_B1) * g
    v = ADAM_B2 * v + (1.0 - ADAM_B2) * jnp.square(g)
    m_hat = m / (1.0 - ADAM_B1 ** ADAM_STEP)
    v_hat = v / (1.0 - ADAM_B2 ** ADAM_STEP)
    delta = -ADAM_LR * (m_hat / (jnp.sqrt(v_hat) + ADAM_EPS) + ADAM_WD * w)
    return delta, m, v


def _sum_adamw(name, parts, w, m, v):
    r, c = w.shape
    tr = r
    for cand in (512, 256, 128, 64, 32, 16, 8):
        if r % cand == 0 and N_DEV * cand * c * 4 <= 6 * 1024 * 1024:
            tr = cand
            break

    def body(p_ref, w_ref, m_ref, v_ref, g_ref, d_ref, m2_ref, v2_ref):
        g = p_ref[0].astype(F32)
        for s in range(1, N_DEV):
            g = g + p_ref[s].astype(F32)
        g_ref[...] = g
        d_ref[...], m2_ref[...], v2_ref[...] = _adamw_vals(w_ref[...], g, m_ref[...], v_ref[...])

    blk = pl.BlockSpec((tr, c), lambda i: (i, 0))
    return pl.pallas_call(
        body, grid=(r // tr,), name=name,
        in_specs=[pl.BlockSpec((N_DEV, tr, c), lambda i: (0, i, 0)), blk, blk, blk],
        out_specs=[blk] * 4, out_shape=[SDS((r, c), F32)] * 4,
        compiler_params=pltpu.CompilerParams(dimension_semantics=("arbitrary",), vmem_limit_bytes=VMEM_LIMIT),
    )(parts, w, m, v)


def _peers():
    x, y, c = lax.axis_index("x"), lax.axis_index("y"), lax.axis_index("c")
    peers = []
    for k in range(1, N_DEV):
        px = 1 - x if k & 4 else x
        py = 1 - y if k & 2 else y
        pc = 1 - c if k & 1 else c
        peers.append(((px, py, pc), 4 * px + 2 * py + pc))
    return 4 * x + 2 * y + c, peers


def _slot(ref, idx, cols):
    if cols is None:
        return ref.at[idx]
    return ref.at[:, pl.ds(pl.multiple_of(idx * cols, LANES), cols)]


def _exchange(name, srcs, dsts, gather):
    n = len(srcs)

    def body(*refs):
        start, wait = _exchange_ops([c for _, c in srcs], [c for _, _, c in dsts], gather,
                                    refs[:n], refs[n:2 * n], *refs[2 * n:])
        start()
        wait()

    any_spec = pl.BlockSpec(memory_space=pl.ANY)
    return pl.pallas_call(
        body, name=name,
        in_specs=[any_spec] * n, out_specs=[any_spec] * n,
        out_shape=[SDS(shape, dt) for shape, dt, _ in dsts],
        scratch_shapes=_exchange_sems(n),
    )(*[a for a, _ in srcs])


_HBM = pl.BlockSpec(memory_space=pltpu.HBM)
_SEM = pl.BlockSpec(memory_space=pltpu.SEMAPHORE)
_EFFECT = pltpu.SideEffectType.DATAFLOW_SIDE_EFFECTING


def _split_copies(src_cols, dst_cols, gather, src_refs, land_refs, send_sems, recv_sems, landings):
    me, peers = _peers()
    out = []
    for a, (s_cols, d_cols) in enumerate(zip(src_cols, dst_cols)):
        for k, (pos, idx) in enumerate(peers):
            blk = src_refs[a] if gather else _slot(src_refs[a], idx, s_cols)
            out.append(pltpu.make_async_remote_copy(
                src_ref=blk, dst_ref=_slot(land_refs[a], idx if landings else me, d_cols),
                send_sem=send_sems.at[a * (N_DEV - 1) + k], recv_sem=recv_sems.at[a * (N_DEV - 1) + k],
                device_id=pos, device_id_type=pl.DeviceIdType.MESH))
    return out


def _exchange_start(name, srcs, dsts, gather):
    n = len(srcs)
    src_cols, dst_cols = [c for _, c in srcs], [c for _, _, c in dsts]

    def body(*refs):
        src_refs, land_refs = refs[:n], refs[n:2 * n]
        send_sems, recv_sems = refs[2 * n:2 * n + 2]
        token = refs[-1]
        for cp in _split_copies(src_cols, dst_cols, gather, src_refs, land_refs, send_sems, recv_sems, False):
            cp.start()
        token[...] = jnp.zeros_like(token)

    hbm = lambda a: pltpu.with_memory_space_constraint(a, pltpu.HBM)
    lands = [hbm(lax.empty(shape, dt)) for shape, dt, _ in dsts]
    res = pl.pallas_call(
        body, name=name,
        out_shape=(pltpu.SemaphoreType.DMA((n * (N_DEV - 1),)), pltpu.SemaphoreType.DMA((n * (N_DEV - 1),)),
                   *[pltpu.HBM(a.shape, a.dtype) for a, _ in srcs], *[pltpu.HBM(a.shape, a.dtype) for a in lands],
                   SDS((8, LANES), F32)),
        in_specs=[_HBM] * (2 * n),
        out_specs=(_SEM, _SEM, *[_HBM] * (2 * n), pl.BlockSpec(memory_space=pltpu.VMEM)),
        input_output_aliases={i: 2 + i for i in range(2 * n)},
        compiler_params=pltpu.CompilerParams(has_side_effects=_EFFECT),
    )(*[hbm(a) for a, _ in srcs], *lands)
    handle = (res[0], res[1], res[2:2 + n], res[2 + n:2 + 2 * n], src_cols, dst_cols, gather)
    return handle, res[-1]


def _exchange_wait(name, handle, after):
    send_sems, recv_sems, src_thru, land_thru, src_cols, dst_cols, gather = handle
    n = len(src_thru)

    def body(*refs):
        src_refs, land_refs = refs[:n], refs[n:2 * n]
        s_sems, r_sems = refs[2 * n:2 * n + 2]
        for cp in _split_copies(src_cols, dst_cols, gather, src_refs, land_refs, s_sems, r_sems, True):
            cp.wait_send()
            cp.wait_recv()

    res = pl.pallas_call(
        body, name=name,
        out_shape=tuple(pltpu.HBM(a.shape, a.dtype) for a in (*src_thru, *land_thru)),
        in_specs=[_HBM] * (2 * n) + [_SEM, _SEM, pl.BlockSpec(memory_space=pl.ANY)],
        out_specs=tuple([_HBM] * (2 * n)),
        input_output_aliases={i: i for i in range(2 * n)},
        compiler_params=pltpu.CompilerParams(has_side_effects=_EFFECT),
    )(*src_thru, *land_thru, send_sems, recv_sems, after)
    return res[:n], res[n:]


def _exchange_sems(n):
    return [pltpu.SemaphoreType.DMA((n, N_DEV - 1)), pltpu.SemaphoreType.DMA((n, N_DEV - 1)),
            pltpu.SemaphoreType.DMA((n,))]


def _exchange_ops(src_cols, dst_cols, gather, src_refs, out_refs, send_sems, recv_sems, local_sems):
    def copies(with_landings):
        me, peers = _peers()
        local, sends, landings = [], [], []
        for a, (s_cols, d_cols) in enumerate(zip(src_cols, dst_cols)):
            mine = src_refs[a] if gather else _slot(src_refs[a], me, s_cols)
            local.append(pltpu.make_async_copy(mine, _slot(out_refs[a], me, d_cols), local_sems.at[a]))
            for k, (pos, idx) in enumerate(peers):
                out_blk = src_refs[a] if gather else _slot(src_refs[a], idx, s_cols)
                both = dict(src_ref=out_blk, send_sem=send_sems.at[a, k], recv_sem=recv_sems.at[a, k],
                            device_id=pos, device_id_type=pl.DeviceIdType.MESH)
                sends.append(pltpu.make_async_remote_copy(dst_ref=_slot(out_refs[a], me, d_cols), **both))
                if with_landings:
                    landings.append(pltpu.make_async_remote_copy(dst_ref=_slot(out_refs[a], idx, d_cols), **both))
        return local, sends, landings

    def start():
        local, sends, _ = copies(False)
        for cp in local + sends:
            cp.start()

    def wait():
        local, sends, landings = copies(True)
        for cp in landings:
            cp.wait_recv()
        for cp in sends:
            cp.wait_send()
        for cp in local:
            cp.wait()

    return start, wait


def _rms_res_fn(h, w):
    return _rms_fn(h, w)[0], h


def _add_epilogue(acc, res):
    return (acc + res,)


def _gather_plan(shards):
    srcs, dsts = [], []
    for n, sh in shards.items():
        r, c = sh.shape
        srcs.append((sh, None))
        if SHARDED[n] and c % LANES == 0:
            dsts.append(((r, N_DEV * c), sh.dtype, c))
        else:
            dsts.append(((N_DEV, r, c), sh.dtype, None))
    return srcs, dsts, True


def _gather_finish(names, outs):
    full = {}
    for n, arr in zip(names, outs):
        if arr.ndim == 2:
            full[n] = arr
        elif SHARDED[n]:
            full[n] = arr.transpose(1, 0, 2).reshape(arr.shape[1], -1)
        else:
            full[n] = arr.reshape(-1, arr.shape[2])
    return full


def _my_index():
    return 4 * lax.axis_index("x") + 2 * lax.axis_index("y") + lax.axis_index("c")


def _gather_landed(names, waited):
    me = _my_index()
    outs = []
    for sh, buf in zip(*waited):
        if buf.ndim == 3:
            outs.append(lax.dynamic_update_slice(buf, sh[None], (me, 0, 0)))
        else:
            outs.append(lax.dynamic_update_slice(buf, sh, (0, me * sh.shape[1])))
    return _gather_finish(names, outs)


def _scatter_plan(grads):
    srcs, dsts = [], []
    for n, gr in grads.items():
        rows, cols = gr.shape
        if not SHARDED[n]:
            r, c = rows // N_DEV, cols
            srcs.append((gr.reshape(N_DEV, r, c), None))
        else:
            r, c = rows, cols // N_DEV
            if c % LANES == 0:
                srcs.append((gr, c))
            else:
                srcs.append((gr.reshape(r, N_DEV, c).transpose(1, 0, 2), None))
        dsts.append(((N_DEV, r, c), gr.dtype, None))
    return srcs, dsts, False


def _scatter_landed(handle, waited):
    me = _my_index()
    outs = []
    for src, cols, buf in zip(waited[0], handle[4], waited[1]):
        if cols is None:
            own = lax.dynamic_index_in_dim(src, me, 0, keepdims=True)
        else:
            own = lax.dynamic_slice(src, (0, me * cols), (src.shape[0], cols))[None]
        outs.append(lax.dynamic_update_slice(buf, own, (me, 0, 0)))
    return outs


def _local_step(x, mem, target, wt, late):
    d = D_MODEL
    g = {}
    wt = dict(wt)
    grp_a = ("w_out", "xa_wq", "xa_wk", "xa_wv", "xa_wo")
    grp_b = ("ffn_w1", "ffn_w2")
    handle_a, tok_a = _exchange_start("late_gather_a_start", *_gather_plan({n: late[n] for n in grp_a}))
    handle_b, tok_b = _exchange_start("late_gather_b_start", *_gather_plan({n: late[n] for n in grp_b}))
    mix_w = wt["mix_norm_w"] + (tok_a[0:1, 0:1] + tok_b[0:1, 0:1])
    u = _row_fwd(_rms_fn, "mix_norm", [(x, d, 0)], [mix_w], [(d, BF16)], 256)[0]
    p = _matmul("in_proj", u, wt["w_in"], "nn", [F32], tn=1536)[0]
    c = _col_fwd(_conv_fn, "dn_conv", p, 0, 24, [wt["dn_conv_w"]])
    dn_pre_tiles = [(c, DN_WIDTH, 0), (c, DN_WIDTH, 1), (p, LANES, 32)]
    dn_pre_params = [wt["dn_a_log"], wt["dn_dt_bias"]]
    qh, kh, gb, bb = _row_fwd(_dn_pre_fn, "dn_pre", dn_pre_tiles, dn_pre_params, [(DN_WIDTH, F32)] * 4, 128)
    dn_arrs = [(qh, 0), (kh, 0), (c, 16), (gb, 0), (bb, 0)]
    o, st_dn, _ = _scan_fwd(_gdn_chunk, "gdn_scan", dn_arrs, DN_HEADS)
    dn_post_tiles = [(o, DN_WIDTH, 0), (p, DN_WIDTH, 3)]
    o_dn = _row_fwd(_dn_post_fn, "dn_post", dn_post_tiles, [wt["dn_norm_w"]], [(DN_WIDTH, BF16)], 256)[0]

    ps = _col_fwd(_lerp_fn, "rw_shift", p, RW_OFF // LANES, 26, [wt["rw_mu"]])
    rw_pre_tiles = [(ps, RW_WIDTH, 0), (ps, RW_WIDTH, 1), (ps, RW_WIDTH, 2), (ps, LANES, 24), (ps, LANES, 25)]
    rw_pre_params = [wt[n] for n in ("rw_w0", "rw_a0", "rw_k_k", "rw_k_a", "rw_w2", "rw_a2", "rw_g2")]
    r, lw, k, v, al, be, gate = _row_fwd(_rw_pre_fn, "rw_pre", rw_pre_tiles, rw_pre_params,
                                         [(RW_WIDTH, F32)] * 7, 128)
    rw_arrs = [(r, 0), (lw, 0), (k, 0), (v, 0), (al, 0), (be, 0)]
    y, st_rw, _ = _scan_fwd(_rw_chunk, "rw_scan", rw_arrs, RW_WIDTH // LANES)
    rw_post_tiles = [(t, RW_WIDTH, 0) for t in (y, r, k, v, gate)]
    rw_post_params = [wt["rw_ln_w"], wt["rw_ln_b"], wt["rw_r_k"]]
    o_rw = _row_fwd(_rw_post_fn, "rw_post", rw_post_tiles, rw_post_params, [(RW_WIDTH, BF16)], 128)[0]
    o_cat = jnp.concatenate([o_dn, o_rw], axis=1)
    wt.update(_gather_landed(grp_a, _exchange_wait("late_gather_a_wait", handle_a, o_cat)))
    h1 = _matmul("out_proj", o_cat, wt["w_out"], "nn", [F32], _add_epilogue, (x,))[0]

    hn = _row_fwd(_rms_fn, "xa_norm", [(h1, d, 0)], [wt["xa_norm_w"]], [(d, BF16)], 256)[0]
    mn = _row_fwd(_rms_fn, "mem_norm", [(mem, d, 0)], [wt["mem_norm_w"]], [(d, BF16)], 256)[0]
    q = _matmul("xa_q", hn, wt["xa_wq"], "nn", [F32])[0]
    kx = _matmul("xa_k", mn, wt["xa_wk"], "nn", [F32])[0]
    vx = _matmul("xa_v", mn, wt["xa_wv"], "nn", [F32])[0]
    ao = _row_fwd(_xattn_fn, "xattn", [(q, XA_WIDTH, 0)], [kx, vx], [(XA_WIDTH, BF16)], 256)[0]
    h2 = _matmul("xa_o", ao, wt["xa_wo"], "nn", [F32], _add_epilogue, (h1,))[0]

    f = _row_fwd(_rms_fn, "ffn_norm", [(h2, d, 0)], [wt["ffn_norm_w"]], [(d, BF16)], 256)[0]
    wt.update(_gather_landed(grp_b, _exchange_wait("late_gather_b_wait", handle_b, f)))
    a, hid = _matmul("ffn_up", f, wt["ffn_w1"], "nn", [F32, BF16],
                     lambda acc: (acc, jnp.square(jnp.maximum(acc, 0.0))))
    h3 = _matmul("ffn_down", hid, wt["ffn_w2"], "nn", [F32], _add_epilogue, (h2,))[0]
    loss8, dh3, g["final_norm_w"] = _loss_call(h3, target, wt["final_norm_w"])

    da = _matmul("ffn_down_dx", dh3, wt["ffn_w2"], "nt", [BF16],
                 lambda acc, av: (acc * 2.0 * jnp.maximum(av, 0.0),), (a,))[0]
    g["ffn_w2"] = _matmul("ffn_down_dw", hid, dh3, "tn", [BF16])[0]
    g["ffn_w1"] = _matmul("ffn_up_dw", f, da, "tn", [BF16])[0]
    df = _matmul("ffn_up_dx", da, wt["ffn_w1"], "nt", [F32])[0]
    pending = {}
    plan = _scatter_plan({n: g.pop(n) for n in grp_b})
    pending[grp_b], tok = _exchange_start("late_grad_b_start", *plan)
    (dh2,), (g["ffn_norm_w"],) = _row_bwd(_rms_res_fn, "ffn_norm_bwd", [(h2, d, 0)],
                                          [wt["ffn_norm_w"] + tok[0:1, 0:1]],
                                          [[(df, d, 0)], [(dh3, d, 0)]], 256)

    dao = _matmul("xa_o_dx", dh2, wt["xa_wo"], "nt", [F32])[0]
    g["xa_wo"] = _matmul("xa_o_dw", ao, dh2, "tn", [BF16])[0]
    (dq,), (dkx, dvx) = _row_bwd(_xattn_fn, "xattn_bwd", [(q, XA_WIDTH, 0)], [kx, vx], [[(dao, XA_WIDTH, 0)]], 256)
    dhn = _matmul("xa_q_dx", dq, wt["xa_wq"], "nt", [F32])[0]
    g["xa_wq"] = _matmul("xa_q_dw", hn, dq, "tn", [BF16])[0]
    g["xa_wk"] = _matmul("xa_k_dw", mn, dkx, "tn", [BF16])[0]
    g["xa_wv"] = _matmul("xa_v_dw", mn, dvx, "tn", [BF16])[0]
    dmn = _matmul("xa_k_dx", dkx, wt["xa_wk"], "nt", [F32])[0]
    dmn = _matmul("xa_v_dx", dvx, wt["xa_wv"], "nt", [F32], _add_epilogue, (dmn,))[0]
    _, (g["mem_norm_w"],) = _row_bwd(_rms_fn, "mem_norm_bwd", [(mem, d, 0)], [wt["mem_norm_w"]],
                                     [[(dmn, d, 0)]], 256, want_tiles=())
    (dh1,), (g["xa_norm_w"],) = _row_bwd(_rms_res_fn, "xa_norm_bwd", [(h1, d, 0)], [wt["xa_norm_w"]],
                                         [[(dhn, d, 0)], [(dh2, d, 0)]], 256)

    do_cat = _matmul("out_proj_dx", dh1, wt["w_out"], "nt", [F32])[0]
    g["w_out"] = _matmul("out_proj_dw", o_cat, dh1, "tn", [BF16])[0]

    plan = _scatter_plan({n: g.pop(n) for n in grp_a})
    pending[grp_a], tok = _exchange_start("late_grad_a_start", *plan)
    (dy, dr1, dk1, dv1, dgate), (g["rw_ln_w"], g["rw_ln_b"], g["rw_r_k"]) = _row_bwd(
        _rw_post_fn, "rw_post_bwd", rw_post_tiles, [rw_post_params[0] + tok[0:1, 0:1]] + rw_post_params[1:],
        [[(do_cat, RW_WIDTH, 1)]], 128)
    (dr2, dlw, dk2, dv2, dal, dbe), _ = _scan_bwd(_rw_chunk, "rw_scan_bwd", rw_arrs, st_rw, dy, RW_WIDTH // LANES)
    one = lambda t: [(t, RW_WIDTH, 0)]
    two = lambda s, t: [(s, RW_WIDTH, 0), (t, RW_WIDTH, 0)]
    d_ps, rw_pre_grads = _row_bwd(
        _rw_pre_fn, "rw_pre_bwd", rw_pre_tiles, rw_pre_params,
        [two(dr1, dr2), one(dlw), two(dk1, dk2), two(dv1, dv2), one(dal), one(dbe), one(dgate)], 128)
    for n, val in zip(("rw_w0", "rw_a0", "rw_k_k", "rw_k_a", "rw_w2", "rw_a2", "rw_g2"), rw_pre_grads):
        g[n] = val
    dp_rw, (g["rw_mu"],) = _col_bwd(_lerp_fn, "rw_shift_bwd", p, RW_OFF // LANES, 26, [wt["rw_mu"]],
                                    jnp.concatenate(d_ps, axis=1))

    (do, dz), (g["dn_norm_w"],) = _row_bwd(_dn_post_fn, "dn_post_bwd", dn_post_tiles, [wt["dn_norm_w"]],
                                           [[(do_cat, DN_WIDTH, 0)]], 256)
    (dqh, dkh, dv_dn, dgb, dbb), _ = _scan_bwd(_gdn_chunk, "gdn_scan_bwd", dn_arrs, st_dn, do, DN_HEADS)
    one = lambda t: [(t, DN_WIDTH, 0)]
    (dcq, dck, dgates), (g["dn_a_log"], g["dn_dt_bias"]) = _row_bwd(
        _dn_pre_fn, "dn_pre_bwd", dn_pre_tiles, dn_pre_params, [one(dqh), one(dkh), one(dgb), one(dbb)], 128)
    dp_qkv, (g["dn_conv_w"],) = _col_bwd(_conv_fn, "dn_conv_bwd", p, 0, 24, [wt["dn_conv_w"]],
                                         jnp.concatenate([dcq, dck, dv_dn], axis=1))
    dp = jnp.concatenate([dp_qkv, dz, dgates, dp_rw, jnp.zeros((x.shape[0], LANES), F32)], axis=1).astype(BF16)
    du = _matmul("in_proj_dx", dp, wt["w_in"], "nt", [F32])[0]
    g["w_in"] = _matmul("in_proj_dw", u, dp, "tn", [BF16], tn=1536)[0]
    (dx,), (g["mix_norm_w"],) = _row_bwd(_rms_res_fn, "mix_norm_bwd", [(x, d, 0)], [wt["mix_norm_w"]],
                                         [[(du, d, 0)], [(dh1, d, 0)]], 256)
    return loss8, dx, g, pending


WEIGHTS = ["mix_norm_w", "w_in", "dn_conv_w", "dn_a_log", "dn_dt_bias", "dn_norm_w", "rw_mu", "rw_w0", "rw_w2",
           "rw_a0", "rw_a2", "rw_g2", "rw_k_k", "rw_k_a", "rw_r_k", "rw_ln_w", "rw_ln_b", "w_out", "xa_norm_w",
           "mem_norm_w", "xa_wq", "xa_wk", "xa_wv", "xa_wo", "ffn_norm_w", "ffn_w1", "ffn_w2", "final_norm_w"]
SHARDED = {"w_in": True, "w_out": False, "xa_wq": False, "xa_wk": False, "xa_wv": False, "xa_wo": True,
           "ffn_w1": True, "ffn_w2": False, "dn_conv_w": True, "rw_w2": True, "rw_a2": True, "rw_g2": True}
BF16_PAYLOAD = ("w_in", "w_out", "xa_wq", "xa_wk", "xa_wv", "xa_wo", "ffn_w1", "ffn_w2")
REPLICATED = [n for n in WEIGHTS if n not in SHARDED]
EARLY = ("w_in", "dn_conv_w", "rw_w2", "rw_a2", "rw_g2")
RW_IN_COLS = IN_COLS - DN_COLS


def _layout_weights(fw):
    wt = dict(fw)
    w_in = fw["w_in"]
    rows = w_in.shape[0]
    wt["w_in"] = jnp.concatenate(
        [w_in[:, :DN_COLS], jnp.zeros((rows, RW_OFF - DN_COLS), w_in.dtype), w_in[:, DN_COLS:],
         jnp.zeros((rows, IN_PAD - RW_OFF - RW_IN_COLS), w_in.dtype)], axis=1)
    wt["dn_conv_w"] = jnp.pad(fw["dn_conv_w"], ((0, 4), (0, 0)))
    wt["dn_a_log"] = jnp.pad(fw["dn_a_log"], ((0, 0), (0, LANES - DN_HEADS)))
    wt["dn_dt_bias"] = jnp.pad(fw["dn_dt_bias"], ((0, 0), (0, LANES - DN_HEADS)))
    wt["rw_w2"] = jnp.pad(fw["rw_w2"], ((0, 64), (0, 0)))
    wt["rw_a2"] = jnp.pad(fw["rw_a2"], ((64, 0), (0, 0)))
    return wt


def _logical_grads(g):
    out = dict(g)
    out["w_in"] = jnp.concatenate([g["w_in"][:, :DN_COLS], g["w_in"][:, RW_OFF:RW_OFF + RW_IN_COLS]], axis=1)
    out["dn_conv_w"] = g["dn_conv_w"][:4]
    out["dn_a_log"] = g["dn_a_log"][:, :DN_HEADS]
    out["dn_dt_bias"] = g["dn_dt_bias"][:, :DN_HEADS]
    out["rw_w2"] = g["rw_w2"][:64]
    out["rw_a2"] = g["rw_a2"][64:]
    return out


def _pack(vals):
    parts = []
    for v in vals:
        flat = v.reshape(-1)
        parts.append(jnp.pad(flat, (0, -flat.shape[0] % LANES)))
    flat = jnp.concatenate(parts)
    flat = jnp.pad(flat, (0, -flat.shape[0] % (8 * LANES)))
    return flat.reshape(-1, LANES)


def _unpack(packed, shapes):
    flat = packed.reshape(-1)
    out, at = [], 0
    for shp in shapes:
        size = math.prod(shp)
        out.append(flat[at:at + size].reshape(shp))
        at += size + (-size % LANES)
    return out


def kernel(x, mem, mix_norm_w, w_in, dn_conv_w, dn_a_log, dn_dt_bias, dn_norm_w, rw_mu, rw_w0, rw_w2, rw_a0, rw_a2, rw_g2, rw_k_k, rw_k_a, rw_r_k, rw_ln_w, rw_ln_b, w_out, xa_norm_w, mem_norm_w, xa_wq, xa_wk, xa_wv, xa_wo, ffn_norm_w, ffn_w1, ffn_w2, final_norm_w, loss_target, m_mix_norm_w, m_w_in, m_dn_conv_w, m_dn_a_log, m_dn_dt_bias, m_dn_norm_w, m_rw_mu, m_rw_w0, m_rw_w2, m_rw_a0, m_rw_a2, m_rw_g2, m_rw_k_k, m_rw_k_a, m_rw_r_k, m_rw_ln_w, m_rw_ln_b, m_w_out, m_xa_norm_w, m_mem_norm_w, m_xa_wq, m_xa_wk, m_xa_wv, m_xa_wo, m_ffn_norm_w, m_ffn_w1, m_ffn_w2, m_final_norm_w, v_mix_norm_w, v_w_in, v_dn_conv_w, v_dn_a_log, v_dn_dt_bias, v_dn_norm_w, v_rw_mu, v_rw_w0, v_rw_w2, v_rw_a0, v_rw_a2, v_rw_g2, v_rw_k_k, v_rw_k_a, v_rw_r_k, v_rw_ln_w, v_rw_ln_b, v_w_out, v_xa_norm_w, v_mem_norm_w, v_xa_wq, v_xa_wk, v_xa_wv, v_xa_wo, v_ffn_norm_w, v_ffn_w1, v_ffn_w2, v_final_norm_w):
    given = dict(locals())
    w = {n: given[n] for n in WEIGHTS}
    m = {n: given["m_" + n] for n in WEIGHTS}
    v = {n: given["v_" + n] for n in WEIGHTS}

    shards = {n: (w[n][0].astype(BF16) if n in BF16_PAYLOAD else w[n][0]) for n in SHARDED}
    srcs, dsts, _ = _gather_plan({n: shards[n] for n in EARLY})
    full = _gather_finish(EARLY, _exchange("early_all_gather", srcs, dsts, True))
    for n in REPLICATED:
        full[n] = w[n].reshape(1, -1)

    loss8, dx, g, pending = _local_step(x[0], mem[0], loss_target[0], _layout_weights(full),
                                        {n: shards[n] for n in SHARDED if n not in EARLY})
    g = _logical_grads(g)
    loss = lax.psum(loss8[0, 0], ("x", "y", "c"))

    pending[EARLY], after = _exchange_start("early_grad_start", *_scatter_plan({n: g[n] for n in EARLY}))
    grad, delta, new_m, new_v = {}, {}, {}, {}
    for names in sorted(pending, key=lambda names: names == EARLY):
        handle = pending[names]
        waited = _exchange_wait("grad_wait_" + names[0], handle, after)
        for n, parts in zip(names, _scatter_landed(handle, waited)):
            res = _sum_adamw("adamw_" + n, parts, w[n][0], m[n][0], v[n][0])
            grad[n], delta[n], new_m[n], new_v[n] = [t[None] for t in res]
            after = res[1]

    packed = _pack([g[n] for n in REPLICATED])
    parts = _exchange("small_all_gather", [(packed, None)], [((N_DEV,) + packed.shape, F32, None)], True)[0]
    res = _sum_adamw("adamw_small", parts, _pack([w[n] for n in REPLICATED]),
                     _pack([m[n] for n in REPLICATED]), _pack([v[n] for n in REPLICATED]))
    shapes = [w[n].shape for n in REPLICATED]
    for store, packed_out in zip((grad, delta, new_m, new_v), res):
        for n, val in zip(REPLICATED, _unpack(packed_out, shapes)):
            store[n] = val

    return (loss, dx[None], *[grad[n] for n in WEIGHTS], *[delta[n] for n in WEIGHTS],
            *[new_m[n] for n in WEIGHTS], *[new_v[n] for n in WEIGHTS])
```

```python
import functools
import math

import jax
import jax.numpy as jnp
from jax import lax
from jax.experimental import pallas as pl
from jax.experimental.pallas import tpu as pltpu

F32 = jnp.float32
BF16 = jnp.bfloat16
SDS = jax.ShapeDtypeStruct

N_DEV = 8
D_MODEL = 2048
LANES = 128
CHUNK = 128
DN_HEADS = 8
DN_WIDTH = 1024
RW_WIDTH = 1024
RW_HEAD = 64
XA_HEADS = 4
XA_WIDTH = 512
FFN_HIDDEN = 8192
IN_COLS = 7440
DN_COLS = 4112
IN_PAD = 7680
RW_OFF = 4224
RMS_EPS = 1e-6
RW_GN_EPS = 64e-5
VMEM_LIMIT = 56 * 1024 * 1024

ADAM_LR = 0.001
ADAM_B1 = 0.9
ADAM_B2 = 0.999
ADAM_EPS = 1e-08
ADAM_WD = 0.01
ADAM_STEP = 10

_DIMS = {"nn": (((1,), (0,)), ((), ())), "nt": (((1,), (1,)), ((), ())), "tn": (((0,), (0,)), ((), ()))}


def _raw_dot(a, b, mode, hi):
    if hi:
        return lax.dot_general(a, b, _DIMS[mode], precision=lax.Precision.HIGHEST, preferred_element_type=F32)
    return lax.dot_general(a.astype(BF16), b.astype(BF16), _DIMS[mode], preferred_element_type=F32)


@functools.partial(jax.custom_vjp, nondiff_argnums=(2, 3))
def mm(a, b, mode="nn", hi=False):
    return _raw_dot(a, b, mode, hi)


def _mm_fwd(a, b, mode, hi):
    return _raw_dot(a, b, mode, hi), (a, b)


def _mm_bwd(mode, hi, res, g):
    a, b = res
    if mode == "nn":
        return _raw_dot(g, b, "nt", hi), _raw_dot(a, g, "tn", hi)
    if mode == "nt":
        return _raw_dot(g, b, "nn", hi), _raw_dot(g, a, "tn", hi)
    return _raw_dot(b, g, "nt", hi), _raw_dot(a, g, "nn", hi)


mm.defvjp(_mm_fwd, _mm_bwd)


def _shift_rows_raw(x, k):
    n = x.shape[0]
    rolled = pltpu.roll(x, k % n, axis=0)
    row = lax.broadcasted_iota(jnp.int32, x.shape, 0)
    keep = row >= k if k > 0 else row < n + k
    return jnp.where(keep, rolled, 0.0)


@functools.partial(jax.custom_vjp, nondiff_argnums=(1,))
def shift_rows(x, k):
    return _shift_rows_raw(x, k)


shift_rows.defvjp(lambda x, k: (_shift_rows_raw(x, k), None), lambda k, _, g: (_shift_rows_raw(g, -k),))


def _softplus(x):
    return jnp.maximum(x, 0.0) + jnp.log(1.0 + jnp.exp(-jnp.abs(x)))


def _sigmoid(x):
    return 1.0 / (1.0 + jnp.exp(-x))


def _silu(x):
    return x * _sigmoid(x)


def _tri_masks(n):
    ii = lax.broadcasted_iota(jnp.int32, (n, n), 0)
    jj = lax.broadcasted_iota(jnp.int32, (n, n), 1)
    return ii >= jj, ii > jj, ii == jj


def _neumann_inv_raw(m):
    n = m.shape[0]
    _, _, eye = _tri_masks(n)
    eye = jnp.where(eye, 1.0, 0.0)
    p = eye + m
    mk = m
    for _ in range(int(math.log2(n)) - 1):
        mk = _raw_dot(mk, mk, "nn", False)
        p = p + _raw_dot(p, mk, "nn", False)
    resid = eye - p + _raw_dot(m, p, "nn", True)
    return p + _raw_dot(p, resid, "nn", False)


@jax.custom_vjp
def _neumann_inv(m):
    return _neumann_inv_raw(m)


def _neumann_inv_fwd(m):
    p = _neumann_inv_raw(m)
    return p, p


def _neumann_inv_bwd(p, g):
    return (_raw_dot(_raw_dot(p, g, "tn", False), p, "nt", False),)


_neumann_inv.defvjp(_neumann_inv_fwd, _neumann_inv_bwd)


def _gdn_group(s0, q, k, v, gb, bb):
    c = q.shape[1]
    causal, _, _ = _tri_masks(c)
    lower = jnp.where(causal, 1.0, 0.0)
    gcs = [mm(lower, gb[j], "nn", True) for j in range(gb.shape[0])]
    gc = jnp.stack(gcs)
    diff = jnp.stack([t - t.T for t in gcs])
    return jax.vmap(_gdn_chunk)(s0, q, k, v, gb, bb, gc, diff)


def _rw_group(*args):
    return jax.vmap(_rw_chunk)(*args)


def _gdn_chunk(s0, q, k, v, gb, bb, gc, diff):
    c = q.shape[0]
    causal, strict, _ = _tri_masks(c)
    decay = jnp.exp(jnp.where(causal, diff, -jnp.inf))
    kb = k * bb
    a = jnp.where(strict, mm(kb, k, "nt") * decay, 0.0)
    p = _neumann_inv(-a)
    u = mm(p, v * bb)
    w = mm(p, kb * jnp.exp(gc))
    attn = mm(q, k, "nt") * decay
    v_new = u - mm(w, s0)
    o = mm(q * jnp.exp(gc), s0) + mm(attn, v_new)
    g_last = jnp.sum(gb, axis=0, keepdims=True)
    s1 = s0 * jnp.exp(g_last) + mm(k * jnp.exp(g_last - gc), v_new, "tn")
    return o, s1


def _rw_chunk(s0, r, lw, k, v, al, be):
    c = r.shape[0]
    causal, strict, _ = _tri_masks(c)
    gc = mm(jnp.where(causal, 1.0, 0.0), lw, "nn", True)
    gp = gc - lw
    row = lax.broadcasted_iota(jnp.int32, lw.shape, 0)
    lane = lax.broadcasted_iota(jnp.int32, lw.shape, 1)
    g_mid = jnp.sum(jnp.where(row < c // 2, lw, 0.0), axis=0, keepdims=True)
    g_last = jnp.sum(lw, axis=0, keepdims=True)
    e_n = jnp.exp(g_mid - gc)
    rg = r * jnp.exp(gc - g_mid)
    bg = be * jnp.exp(gp - g_mid)
    an = al * e_n
    kn = k * e_n
    bt = mm(be * jnp.exp(gp), s0, "nt")
    rt = mm(r * jnp.exp(gc), s0, "nt")
    us, ys = [], []
    for h in range(2):
        mine = (lane >= RW_HEAD) if h else (lane < RW_HEAD)
        bgh = jnp.where(mine, bg, 0.0)
        rgh = jnp.where(mine, rg, 0.0)
        a_ab = jnp.where(strict, mm(bgh, an, "nt"), 0.0)
        a_kb = jnp.where(strict, mm(bgh, kn, "nt"), 0.0)
        a_ra = jnp.where(causal, mm(rgh, an, "nt"), 0.0)
        a_rk = jnp.where(causal, mm(rgh, kn, "nt"), 0.0)
        p = _neumann_inv(a_ab)
        u_h = mm(p, bt + mm(a_kb, v))
        us.append(u_h)
        ys.append(rt + mm(a_ra, u_h) + mm(a_rk, v))
    lo = lane < RW_HEAD
    u = jnp.where(lo, us[0], us[1])
    y = jnp.where(lo, ys[0], ys[1])
    tail = jnp.exp(g_last - gc)
    s1 = s0 * jnp.exp(g_last) + mm(u, al * tail, "tn") + mm(v, k * tail, "tn")
    vi = lax.broadcasted_iota(jnp.int32, s0.shape, 0)
    ki = lax.broadcasted_iota(jnp.int32, s0.shape, 1)
    s1 = jnp.where((vi < RW_HEAD) == (ki < RW_HEAD), s1, 0.0)
    return y, s1


SCAN_HB = 8


def _scan_specs(arrs, n_chunks, reverse):
    def spec(off):
        assert off % SCAN_HB == 0
        if reverse:
            return pl.BlockSpec((CHUNK, SCAN_HB * LANES), lambda h, n: (n_chunks - 1 - n, off // SCAN_HB + h))
        return pl.BlockSpec((CHUNK, SCAN_HB * LANES), lambda h, n: (n, off // SCAN_HB + h))
    return [spec(off) for _, off in arrs]


def _split_heads(x):
    return jnp.stack([x[:, LANES * j:LANES * (j + 1)] for j in range(SCAN_HB)], axis=0)


def _merge_heads(x):
    return jnp.concatenate([x[j] for j in range(SCAN_HB)], axis=1)


def _scan_fwd(group_fn, name, arrs, heads):
    s = arrs[0][0].shape[0]
    n_chunks = s // CHUNK
    n_in = len(arrs)

    def body(*refs):
        y_ref, st_ref, s_scr = refs[n_in:]

        @pl.when(pl.program_id(1) == 0)
        def _():
            s_scr[...] = jnp.zeros_like(s_scr)

        s0 = s_scr[...]
        st_ref[...] = s0
        y, s1 = group_fn(s0, *[_split_heads(r[...]) for r in refs[:n_in]])
        y_ref[...] = _merge_heads(y)
        s_scr[...] = s1

    return pl.pallas_call(
        body, grid=(heads // SCAN_HB, n_chunks), name=name,
        in_specs=_scan_specs(arrs, n_chunks, False),
        out_specs=[pl.BlockSpec((CHUNK, SCAN_HB * LANES), lambda h, n: (n, h)),
                   pl.BlockSpec((SCAN_HB, None, LANES, LANES), lambda h, n: (h, n, 0, 0))],
        out_shape=[SDS((s, heads * LANES), F32), SDS((heads, n_chunks, LANES, LANES), F32)],
        scratch_shapes=[pltpu.VMEM((SCAN_HB, LANES, LANES), F32)],
        compiler_params=pltpu.CompilerParams(dimension_semantics=("arbitrary", "arbitrary")),
    )(*[a for a, _ in arrs])


def _scan_bwd(group_fn, name, arrs, states, dy, heads):
    s = arrs[0][0].shape[0]
    n_chunks = s // CHUNK
    n_in = len(arrs)

    def body(*refs):
        st_ref, dy_ref = refs[n_in:n_in + 2]
        d_refs = refs[n_in + 2:2 * n_in + 2]
        ds_scr = refs[-1]

        @pl.when(pl.program_id(1) == 0)
        def _():
            ds_scr[...] = jnp.zeros_like(ds_scr)

        _, vjp = jax.vjp(group_fn, st_ref[...], *[_split_heads(r[...]) for r in refs[:n_in]])
        grads = vjp((_split_heads(dy_ref[...]), ds_scr[...]))
        ds_scr[...] = grads[0]
        for ref, g in zip(d_refs, grads[1:]):
            ref[...] = _merge_heads(g)

    rev = pl.BlockSpec((CHUNK, SCAN_HB * LANES), lambda h, n: (n_chunks - 1 - n, h))
    return pl.pallas_call(
        body, grid=(heads // SCAN_HB, n_chunks), name=name,
        in_specs=_scan_specs(arrs, n_chunks, True)
        + [pl.BlockSpec((SCAN_HB, None, LANES, LANES), lambda h, n: (h, n_chunks - 1 - n, 0, 0)), rev],
        out_specs=[rev] * n_in,
        out_shape=[SDS((s, heads * LANES), F32)] * n_in,
        scratch_shapes=[pltpu.VMEM((SCAN_HB, LANES, LANES), F32)],
        compiler_params=pltpu.CompilerParams(dimension_semantics=("arbitrary", "arbitrary")),
    )(*[a for a, _ in arrs], states, dy)


def _col_spec(tr, width, cb):
    return pl.BlockSpec((tr, width), lambda i: (i, cb))


def _whole(p):
    return pl.BlockSpec(p.shape, lambda i: (0,) * p.ndim)


def _row_fwd(fn, name, tiles, params, outs, tr):
    rows = tiles[0][0].shape[0]
    nt, npar = len(tiles), len(params)

    def body(*refs):
        vals = [r[...].astype(F32) for r in refs[:nt + npar]]
        for ref, o in zip(refs[nt + npar:], fn(*vals)):
            ref[...] = o.astype(ref.dtype)

    return pl.pallas_call(
        body, grid=(rows // tr,), name=name,
        in_specs=[_col_spec(tr, w, cb) for _, w, cb in tiles] + [_whole(p) for p in params],
        out_specs=[_col_spec(tr, w, 0) for w, _ in outs],
        out_shape=[SDS((rows, w), dt) for w, dt in outs],
        compiler_params=pltpu.CompilerParams(dimension_semantics=("arbitrary",), vmem_limit_bytes=VMEM_LIMIT),
    )(*[a for a, _, _ in tiles], *params)


def _row_bwd(fn, name, tiles, params, cts, tr, want_tiles=None):
    rows = tiles[0][0].shape[0]
    nt, npar = len(tiles), len(params)
    want = list(range(nt)) if want_tiles is None else list(want_tiles)
    flat_cts = [c for group in cts for c in group]
    n_ct = len(flat_cts)

    def body(*refs):
        vals = [r[...].astype(F32) for r in refs[:nt + npar]]
        ct_refs = refs[nt + npar:nt + npar + n_ct]
        out_refs = refs[nt + npar + n_ct:]
        ct_vals, at = [], 0
        for group in cts:
            total = ct_refs[at][...].astype(F32)
            for r in ct_refs[at + 1:at + len(group)]:
                total = total + r[...].astype(F32)
            ct_vals.append(total)
            at += len(group)
        _, vjp = jax.vjp(lambda *a: tuple(fn(*a)), *vals)
        grads = vjp(tuple(ct_vals))
        for ref, t in zip(out_refs[:len(want)], want):
            ref[...] = grads[t]
        first = pl.program_id(0) == 0
        for ref, g in zip(out_refs[len(want):], grads[nt:]):
            @pl.when(first)
            def _(ref=ref, g=g):
                ref[...] = g

            @pl.when(jnp.logical_not(first))
            def _(ref=ref, g=g):
                ref[...] += g

    res = pl.pallas_call(
        body, grid=(rows // tr,), name=name,
        in_specs=[_col_spec(tr, w, cb) for _, w, cb in tiles] + [_whole(p) for p in params]
        + [_col_spec(tr, w, cb) for _, w, cb in flat_cts],
        out_specs=[_col_spec(tr, tiles[t][1], 0) for t in want] + [_whole(p) for p in params],
        out_shape=[SDS((rows, tiles[t][1]), F32) for t in want] + [SDS(p.shape, F32) for p in params],
        compiler_params=pltpu.CompilerParams(dimension_semantics=("arbitrary",), vmem_limit_bytes=VMEM_LIMIT),
    )(*[a for a, _, _ in tiles], *params, *[a for a, _, _ in flat_cts])
    return res[:len(want)], res[len(want):]


def _col_fwd(fn, name, x, first_block, n_blocks, params):
    rows = x.shape[0]

    def body(*refs):
        refs[-1][...] = fn(*[r[...] for r in refs[:-1]])

    return pl.pallas_call(
        body, grid=(n_blocks,), name=name,
        in_specs=[pl.BlockSpec((rows, LANES), lambda j: (0, first_block + j))]
        + [pl.BlockSpec((p.shape[0], LANES), lambda j: (0, j)) for p in params],
        out_specs=pl.BlockSpec((rows, LANES), lambda j: (0, j)),
        out_shape=SDS((rows, n_blocks * LANES), F32),
        compiler_params=pltpu.CompilerParams(dimension_semantics=("arbitrary",), vmem_limit_bytes=VMEM_LIMIT),
    )(x, *params)


def _col_bwd(fn, name, x, first_block, n_blocks, params, dy):
    rows = x.shape[0]
    npar = len(params)

    def body(*refs):
        vals = [r[...] for r in refs[:1 + npar]]
        _, vjp = jax.vjp(fn, *vals)
        grads = vjp(refs[1 + npar][...])
        for ref, g in zip(refs[2 + npar:], grads):
            ref[...] = g

    pspecs = [pl.BlockSpec((p.shape[0], LANES), lambda j: (0, j)) for p in params]
    blk = pl.BlockSpec((rows, LANES), lambda j: (0, j))
    res = pl.pallas_call(
        body, grid=(n_blocks,), name=name,
        in_specs=[pl.BlockSpec((rows, LANES), lambda j: (0, first_block + j))] + pspecs + [blk],
        out_specs=[blk] + pspecs,
        out_shape=[SDS((rows, n_blocks * LANES), F32)] + [SDS(p.shape, F32) for p in params],
        compiler_params=pltpu.CompilerParams(dimension_semantics=("arbitrary",), vmem_limit_bytes=VMEM_LIMIT),
    )(x, *params, dy)
    return res[0], res[1:]


def _conv_fn(x, w):
    acc = x * w[3:4, :]
    for j in range(3):
        acc = acc + shift_rows(x, 3 - j) * w[j:j + 1, :]
    return _silu(acc)


def _lerp_fn(x, mu):
    return x + (shift_rows(x, 1) - x) * mu[0:1, :]


def _seg_sum(x, width):
    if width == LANES:
        return jnp.sum(x, axis=1, keepdims=True)
    lo = lax.broadcasted_iota(jnp.int32, x.shape, 1) < width
    s0 = jnp.sum(jnp.where(lo, x, 0.0), axis=1, keepdims=True)
    s1 = jnp.sum(jnp.where(lo, 0.0, x), axis=1, keepdims=True)
    return jnp.where(lo, s0, s1)


def _per_block(fn, *xs):
    n = xs[0].shape[1] // LANES
    return jnp.concatenate([fn(*[x[:, LANES * b:LANES * (b + 1)] for x in xs]) for b in range(n)], axis=1)


def _head_expand(col0):
    r = lax.broadcasted_iota(jnp.int32, (LANES, DN_WIDTH), 0)
    c = lax.shift_right_logical(lax.broadcasted_iota(jnp.int32, (LANES, DN_WIDTH), 1), 7)
    return jnp.where(r == c + col0, 1.0, 0.0)


def _dn_pre_fn(cq, ck, gates, a_log, dt_bias):
    l2 = lambda x: x * lax.rsqrt(_seg_sum(x * x, LANES) + 1e-6)
    qh = _per_block(l2, cq) * (LANES ** -0.5)
    kh = _per_block(l2, ck)
    g = -jnp.exp(a_log) * _softplus(gates + dt_bias)
    gb = mm(g, _head_expand(0), "nn", True)
    bb = mm(_sigmoid(gates), _head_expand(DN_HEADS), "nn", True)
    return qh, kh, gb, bb


def _dn_post_fn(o, z, nw):
    def one(ob, zb):
        return ob * lax.rsqrt(_seg_sum(ob * ob, LANES) * (1.0 / LANES) + RMS_EPS) * nw * _silu(zb)
    return (_per_block(one, o, z),)


def _rw_pre_fn(pr, pk, pv, pwa, pg, w0, a0, k_k, k_a, w2p, a2p, g2):
    log_w = -_softplus(-(w0 + mm(jnp.tanh(pwa), w2p))) - 0.5
    lw = -jnp.exp(log_w)
    a = _sigmoid(a0 + mm(pwa, a2p))
    gate = mm(_sigmoid(pg), g2)
    kk = pk * k_k
    kk = _per_block(lambda x: x / jnp.maximum(jnp.sqrt(_seg_sum(x * x, RW_HEAD)), 1e-12), kk)
    k = pk * (1.0 + (a - 1.0) * k_a)
    return pr, lw, k, pv, kk * a, -kk, gate


def _rw_post_fn(y, r, k, v, gate, ln_w, ln_b, r_k):
    def one(yb, rb, kb, vb, gb, wb, bb, rkb):
        d = yb - _seg_sum(yb, RW_HEAD) * (1.0 / RW_HEAD)
        var = _seg_sum(d * d, RW_HEAD) * (1.0 / RW_HEAD)
        yn = d * lax.rsqrt(var + RW_GN_EPS) * wb + bb
        return (yn + _seg_sum(rb * kb * rkb, RW_HEAD) * vb) * gb
    return (_per_block(one, y, r, k, v, gate, ln_w, ln_b, r_k),)


def _rms_fn(h, w):
    return (h * lax.rsqrt(jnp.mean(h * h, axis=1, keepdims=True) + RMS_EPS) * w,)


def _xattn_fn(q, k, v):
    outs = []
    for h in range(XA_HEADS):
        sl = slice(LANES * h, LANES * (h + 1))
        s = mm(q[:, sl], k[:, sl], "nt") * (LANES ** -0.5)
        e = jnp.exp(s - jnp.max(s, axis=1, keepdims=True))
        outs.append(mm(e / jnp.sum(e, axis=1, keepdims=True), v[:, sl]))
    return (jnp.concatenate(outs, axis=1),)


def _fit(tile, dim):
    best = [t for t in range(LANES, min(tile, dim) + 1, LANES) if dim % t == 0]
    assert best, (tile, dim)
    return best[-1]


def _matmul(name, a, b, mode, out_dtypes, epilogue=None, extras=(), tm=1024, tn=1024, tk=2048, after=None):
    if mode == "tn":
        (k_dim, m), n = a.shape, b.shape[1]
    else:
        (m, k_dim), n = a.shape, (b.shape[1] if mode == "nn" else b.shape[0])
    tm, tn, tk = _fit(tm, m), _fit(tn, n), _fit(tk, k_dim)
    nk = k_dim // tk
    a_spec = (pl.BlockSpec((tk, tm), lambda i, j, k: (k, i)) if mode == "tn"
              else pl.BlockSpec((tm, tk), lambda i, j, k: (i, k)))
    b_spec = (pl.BlockSpec((tn, tk), lambda i, j, k: (j, k)) if mode == "nt"
              else pl.BlockSpec((tk, tn), lambda i, j, k: (k, j)))
    o_spec = pl.BlockSpec((tm, tn), lambda i, j, k: (i, j))
    n_ex, n_out = len(extras), len(out_dtypes)
    ties = [] if after is None else [after]

    def finish(total, rest):
        ex = [r[...].astype(F32) for r in rest[:n_ex]]
        res = epilogue(total, *ex) if epilogue else (total,)
        for ref, o in zip(rest[n_ex + len(ties):n_ex + len(ties) + n_out], res):
            ref[...] = o.astype(ref.dtype)

    def body_single(a_ref, b_ref, *rest):
        finish(_raw_dot(a_ref[...], b_ref[...], mode, False), rest)

    def body_acc(a_ref, b_ref, *rest):
        acc = rest[-1]
        k = pl.program_id(2)

        @pl.when(k == 0)
        def _():
            acc[...] = jnp.zeros_like(acc)

        acc[...] += _raw_dot(a_ref[...], b_ref[...], mode, False)

        @pl.when(k == nk - 1)
        def _():
            finish(acc[...], rest)

    res = pl.pallas_call(
        body_single if nk == 1 else body_acc, grid=(m // tm, n // tn, nk), name=name,
        in_specs=[a_spec, b_spec] + [o_spec] * n_ex + [pl.BlockSpec((8, LANES), lambda i, j, k: (0, 0))] * len(ties),
        out_specs=[o_spec] * n_out,
        out_shape=[SDS((m, n), dt) for dt in out_dtypes],
        scratch_shapes=[] if nk == 1 else [pltpu.VMEM((tm, tn), F32)],
        compiler_params=pltpu.CompilerParams(dimension_semantics=("parallel", "parallel", "arbitrary"),
                                             vmem_limit_bytes=VMEM_LIMIT),
    )(a, b, *extras, *ties)
    return res


def _loss_call(h, target, w, tr=256):
    rows, d = h.shape

    def fn(hv, wv, tv):
        y = _rms_fn(hv, wv)[0]
        return 0.5 * jnp.sum(jnp.mean(jnp.square(y - tv), axis=1, keepdims=True), axis=0, keepdims=True)

    def body(h_ref, t_ref, w_ref, loss_ref, dh_ref, dw_ref):
        tv = t_ref[...]
        val, vjp = jax.vjp(lambda hv, wv: fn(hv, wv, tv), h_ref[...], w_ref[...])
        dh, dw = vjp(jnp.ones((1, 1), F32))
        dh_ref[...] = dh
        first = pl.program_id(0) == 0

        @pl.when(first)
        def _():
            loss_ref[...] = jnp.broadcast_to(val, loss_ref.shape)
            dw_ref[...] = dw

        @pl.when(jnp.logical_not(first))
        def _():
            loss_ref[...] += jnp.broadcast_to(val, loss_ref.shape)
            dw_ref[...] += dw

    return pl.pallas_call(
        body, grid=(rows // tr,), name="loss_head",
        in_specs=[_col_spec(tr, d, 0), _col_spec(tr, d, 0), _whole(w)],
        out_specs=[pl.BlockSpec((8, LANES), lambda i: (0, 0)), _col_spec(tr, d, 0), _whole(w)],
        out_shape=[SDS((8, LANES), F32), SDS((rows, d), F32), SDS(w.shape, F32)],
        compiler_params=pltpu.CompilerParams(dimension_semantics=("arbitrary",), vmem_limit_bytes=VMEM_LIMIT),
    )(h, target, w)


def _adamw_vals(w, g, m, v):
    m = ADAM_B1 * m + (1.0 - ADAM_B1) * g
    v = ADAM_B2 * v + (1.0 - ADAM_B2) * jnp.square(g)
    m_hat = m / (1.0 - ADAM_B1 ** ADAM_STEP)
    v_hat = v / (1.0 - ADAM_B2 ** ADAM_STEP)
    delta = -ADAM_LR * (m_hat / (jnp.sqrt(v_hat) + ADAM_EPS) + ADAM_WD * w)
    return delta, m, v


def _sum_adamw(name, parts, w, m, v):
    r, c = w.shape
    tr = r
    for cand in (512, 256, 128, 64, 32, 16, 8):
        if r % cand == 0 and N_DEV * cand * c * 4 <= 6 * 1024 * 1024:
            tr = cand
            break

    def body(p_ref, w_ref, m_ref, v_ref, g_ref, d_ref, m2_ref, v2_ref):
        g = p_ref[0].astype(F32)
        for s in range(1, N_DEV):
            g = g + p_ref[s].astype(F32)
        g_ref[...] = g
        d_ref[...], m2_ref[...], v2_ref[...] = _adamw_vals(w_ref[...], g, m_ref[...], v_ref[...])

    blk = pl.BlockSpec((tr, c), lambda i: (i, 0))
    return pl.pallas_call(
        body, grid=(r // tr,), name=name,
        in_specs=[pl.BlockSpec((N_DEV, tr, c), lambda i: (0, i, 0)), blk, blk, blk],
        out_specs=[blk] * 4, out_shape=[SDS((r, c), F32)] * 4,
        compiler_params=pltpu.CompilerParams(dimension_semantics=("arbitrary",), vmem_limit_bytes=VMEM_LIMIT),
    )(parts, w, m, v)


def _peers():
    x, y, c = lax.axis_index("x"), lax.axis_index("y"), lax.axis_index("c")
    peers = []
    for k in range(1, N_DEV):
        px = 1 - x if k & 4 else x
        py = 1 - y if k & 2 else y
        pc = 1 - c if k & 1 else c
        peers.append(((px, py, pc), 4 * px + 2 * py + pc))
    return 4 * x + 2 * y + c, peers


def _slot(ref, idx, cols):
    if cols is None:
        return ref.at[idx]
    return ref.at[:, pl.ds(pl.multiple_of(idx * cols, LANES), cols)]


def _exchange(name, srcs, dsts, gather):
    n = len(srcs)

    def body(*refs):
        start, wait = _exchange_ops([c for _, c in srcs], [c for _, _, c in dsts], gather,
                                    refs[:n], refs[n:2 * n], *refs[2 * n:])
        start()
        wait()

    any_spec = pl.BlockSpec(memory_space=pl.ANY)
    return pl.pallas_call(
        body, name=name,
        in_specs=[any_spec] * n, out_specs=[any_spec] * n,
        out_shape=[SDS(shape, dt) for shape, dt, _ in dsts],
        scratch_shapes=_exchange_sems(n),
    )(*[a for a, _ in srcs])


_HBM = pl.BlockSpec(memory_space=pltpu.HBM)
_SEM = pl.BlockSpec(memory_space=pltpu.SEMAPHORE)
_EFFECT = pltpu.SideEffectType.DATAFLOW_SIDE_EFFECTING


def _split_copies(src_cols, dst_cols, gather, src_refs, land_refs, send_sems, recv_sems, landings):
    me, peers = _peers()
    out = []
    for a, (s_cols, d_cols) in enumerate(zip(src_cols, dst_cols)):
        for k, (pos, idx) in enumerate(peers):
            blk = src_refs[a] if gather else _slot(src_refs[a], idx, s_cols)
            out.append(pltpu.make_async_remote_copy(
                src_ref=blk, dst_ref=_slot(land_refs[a], idx if landings else me, d_cols),
                send_sem=send_sems.at[a * (N_DEV - 1) + k], recv_sem=recv_sems.at[a * (N_DEV - 1) + k],
                device_id=pos, device_id_type=pl.DeviceIdType.MESH))
    return out


def _exchange_start(name, srcs, dsts, gather):
    n = len(srcs)
    src_cols, dst_cols = [c for _, c in srcs], [c for _, _, c in dsts]

    def body(*refs):
        src_refs, land_refs = refs[:n], refs[n:2 * n]
        send_sems, recv_sems = refs[2 * n:2 * n + 2]
        token = refs[-1]
        for cp in _split_copies(src_cols, dst_cols, gather, src_refs, land_refs, send_sems, recv_sems, False):
            cp.start()
        token[...] = jnp.zeros_like(token)

    hbm = lambda a: pltpu.with_memory_space_constraint(a, pltpu.HBM)
    lands = [hbm(lax.empty(shape, dt)) for shape, dt, _ in dsts]
    res = pl.pallas_call(
        body, name=name,
        out_shape=(pltpu.SemaphoreType.DMA((n * (N_DEV - 1),)), pltpu.SemaphoreType.DMA((n * (N_DEV - 1),)),
                   *[pltpu.HBM(a.shape, a.dtype) for a, _ in srcs], *[pltpu.HBM(a.shape, a.dtype) for a in lands],
                   SDS((8, LANES), F32)),
        in_specs=[_HBM] * (2 * n),
        out_specs=(_SEM, _SEM, *[_HBM] * (2 * n), pl.BlockSpec(memory_space=pltpu.VMEM)),
        input_output_aliases={i: 2 + i for i in range(2 * n)},
        compiler_params=pltpu.CompilerParams(has_side_effects=_EFFECT),
    )(*[hbm(a) for a, _ in srcs], *lands)
    handle = (res[0], res[1], res[2:2 + n], res[2 + n:2 + 2 * n], src_cols, dst_cols, gather)
    return handle, res[-1]


def _exchange_wait(name, handle, after):
    send_sems, recv_sems, src_thru, land_thru, src_cols, dst_cols, gather = handle
    n = len(src_thru)

    def body(*refs):
        src_refs, land_refs = refs[:n], refs[n:2 * n]
        s_sems, r_sems = refs[2 * n:2 * n + 2]
        for cp in _split_copies(src_cols, dst_cols, gather, src_refs, land_refs, s_sems, r_sems, True):
            cp.wait_send()
            cp.wait_recv()

    res = pl.pallas_call(
        body, name=name,
        out_shape=tuple(pltpu.HBM(a.shape, a.dtype) for a in (*src_thru, *land_thru)),
        in_specs=[_HBM] * (2 * n) + [_SEM, _SEM, pl.BlockSpec(memory_space=pl.ANY)],
        out_specs=tuple([_HBM] * (2 * n)),
        input_output_aliases={i: i for i in range(2 * n)},
        compiler_params=pltpu.CompilerParams(has_side_effects=_EFFECT),
    )(*src_thru, *land_thru, send_sems, recv_sems, after)
    return res[:n], res[n:]


def _exchange_sems(n):
    return [pltpu.SemaphoreType.DMA((n, N_DEV - 1)), pltpu.SemaphoreType.DMA((n, N_DEV - 1)),
            pltpu.SemaphoreType.DMA((n,))]


def _exchange_ops(src_cols, dst_cols, gather, src_refs, out_refs, send_sems, recv_sems, local_sems):
    def copies(with_landings):
        me, peers = _peers()
        local, sends, landings = [], [], []
        for a, (s_cols, d_cols) in enumerate(zip(src_cols, dst_cols)):
            mine = src_refs[a] if gather else _slot(src_refs[a], me, s_cols)
            local.append(pltpu.make_async_copy(mine, _slot(out_refs[a], me, d_cols), local_sems.at[a]))
            for k, (pos, idx) in enumerate(peers):
                out_blk = src_refs[a] if gather else _slot(src_refs[a], idx, s_cols)
                both = dict(src_ref=out_blk, send_sem=send_sems.at[a, k], recv_sem=recv_sems.at[a, k],
                            device_id=pos, device_id_type=pl.DeviceIdType.MESH)
                sends.append(pltpu.make_async_remote_copy(dst_ref=_slot(out_refs[a], me, d_cols), **both))
                if with_landings:
                    landings.append(pltpu.make_async_remote_copy(dst_ref=_slot(out_refs[a], idx, d_cols), **both))
        return local, sends, landings

    def start():
        local, sends, _ = copies(False)
        for cp in local + sends:
            cp.start()

    def wait():
        local, sends, landings = copies(True)
        for cp in landings:
            cp.wait_recv()
        for cp in sends:
            cp.wait_send()
        for cp in local:
            cp.wait()

    return start, wait


def _rms_res_fn(h, w):
    return _rms_fn(h, w)[0], h


def _add_epilogue(acc, res):
    return (acc + res,)


def _gather_plan(shards):
    srcs, dsts = [], []
    for n, sh in shards.items():
        r, c = sh.shape
        srcs.append((sh, None))
        if SHARDED[n] and c % LANES == 0:
            dsts.append(((r, N_DEV * c), sh.dtype, c))
        else:
            dsts.append(((N_DEV, r, c), sh.dtype, None))
    return srcs, dsts, True


def _gather_finish(names, outs):
    full = {}
    for n, arr in zip(names, outs):
        if arr.ndim == 2:
            full[n] = arr
        elif SHARDED[n]:
            full[n] = arr.transpose(1, 0, 2).reshape(arr.shape[1], -1)
        else:
            full[n] = arr.reshape(-1, arr.shape[2])
    return full


def _my_index():
    return 4 * lax.axis_index("x") + 2 * lax.axis_index("y") + lax.axis_index("c")


def _gather_landed(names, waited):
    me = _my_index()
    outs = []
    for sh, buf in zip(*waited):
        if buf.ndim == 3:
            outs.append(lax.dynamic_update_slice(buf, sh[None], (me, 0, 0)))
        else:
            outs.append(lax.dynamic_update_slice(buf, sh, (0, me * sh.shape[1])))
    return _gather_finish(names, outs)


def _scatter_plan(grads):
    srcs, dsts = [], []
    for n, gr in grads.items():
        rows, cols = gr.shape
        if not SHARDED[n]:
            r, c = rows // N_DEV, cols
            srcs.append((gr.reshape(N_DEV, r, c), None))
        else:
            r, c = rows, cols // N_DEV
            if c % LANES == 0:
                srcs.append((gr, c))
            else:
                srcs.append((gr.reshape(r, N_DEV, c).transpose(1, 0, 2), None))
        dsts.append(((N_DEV, r, c), gr.dtype, None))
    return srcs, dsts, False


def _scatter_landed(handle, waited):
    me = _my_index()
    outs = []
    for src, cols, buf in zip(waited[0], handle[4], waited[1]):
        if cols is None:
            own = lax.dynamic_index_in_dim(src, me, 0, keepdims=True)
        else:
            own = lax.dynamic_slice(src, (0, me * cols), (src.shape[0], cols))[None]
        outs.append(lax.dynamic_update_slice(buf, own, (me, 0, 0)))
    return outs


def _local_step(x, mem, target, wt, late):
    d = D_MODEL
    g = {}
    wt = dict(wt)
    grp_a = ("w_out", "xa_wq", "xa_wk", "xa_wv", "xa_wo")
    grp_b = ("ffn_w1", "ffn_w2")
    handle_a, tok_a = _exchange_start("late_gather_a_start", *_gather_plan({n: late[n] for n in grp_a}))
    handle_b, tok_b = _exchange_start("late_gather_b_start", *_gather_plan({n: late[n] for n in grp_b}))
    mix_w = wt["mix_norm_w"] + (tok_a[0:1, 0:1] + tok_b[0:1, 0:1])
    u = _row_fwd(_rms_fn, "mix_norm", [(x, d, 0)], [mix_w], [(d, BF16)], 256)[0]
    p = _matmul("in_proj", u, wt["w_in"], "nn", [F32], tn=1536)[0]
    c = _col_fwd(_conv_fn, "dn_conv", p, 0, 24, [wt["dn_conv_w"]])
    dn_pre_tiles = [(c, DN_WIDTH, 0), (c, DN_WIDTH, 1), (p, LANES, 32)]
    dn_pre_params = [wt["dn_a_log"], wt["dn_dt_bias"]]
    qh, kh, gb, bb = _row_fwd(_dn_pre_fn, "dn_pre", dn_pre_tiles, dn_pre_params, [(DN_WIDTH, F32)] * 4, 128)
    dn_arrs = [(qh, 0), (kh, 0), (c, 16), (gb, 0), (bb, 0)]
    o, st_dn = _scan_fwd(_gdn_group, "gdn_scan", dn_arrs, DN_HEADS)
    dn_post_tiles = [(o, DN_WIDTH, 0), (p, DN_WIDTH, 3)]
    o_dn = _row_fwd(_dn_post_fn, "dn_post", dn_post_tiles, [wt["dn_norm_w"]], [(DN_WIDTH, BF16)], 256)[0]

    ps = _col_fwd(_lerp_fn, "rw_shift", p, RW_OFF // LANES, 26, [wt["rw_mu"]])
    rw_pre_tiles = [(ps, RW_WIDTH, 0), (ps, RW_WIDTH, 1), (ps, RW_WIDTH, 2), (ps, LANES, 24), (ps, LANES, 25)]
    rw_pre_params = [wt[n] for n in ("rw_w0", "rw_a0", "rw_k_k", "rw_k_a", "rw_w2", "rw_a2", "rw_g2")]
    r, lw, k, v, al, be, gate = _row_fwd(_rw_pre_fn, "rw_pre", rw_pre_tiles, rw_pre_params,
                                         [(RW_WIDTH, F32)] * 7, 128)
    rw_arrs = [(r, 0), (lw, 0), (k, 0), (v, 0), (al, 0), (be, 0)]
    y, st_rw = _scan_fwd(_rw_group, "rw_scan", rw_arrs, RW_WIDTH // LANES)
    rw_post_tiles = [(t, RW_WIDTH, 0) for t in (y, r, k, v, gate)]
    rw_post_params = [wt["rw_ln_w"], wt["rw_ln_b"], wt["rw_r_k"]]
    o_rw = _row_fwd(_rw_post_fn, "rw_post", rw_post_tiles, rw_post_params, [(RW_WIDTH, BF16)], 128)[0]
    o_cat = jnp.concatenate([o_dn, o_rw], axis=1)
    wt.update(_gather_landed(grp_a, _exchange_wait("late_gather_a_wait", handle_a, o_cat)))
    h1 = _matmul("out_proj", o_cat, wt["w_out"], "nn", [F32], _add_epilogue, (x,))[0]

    hn = _row_fwd(_rms_fn, "xa_norm", [(h1, d, 0)], [wt["xa_norm_w"]], [(d, BF16)], 256)[0]
    mn = _row_fwd(_rms_fn, "mem_norm", [(mem, d, 0)], [wt["mem_norm_w"]], [(d, BF16)], 256)[0]
    q = _matmul("xa_q", hn, wt["xa_wq"], "nn", [F32])[0]
    kx = _matmul("xa_k", mn, wt["xa_wk"], "nn", [F32])[0]
    vx = _matmul("xa_v", mn, wt["xa_wv"], "nn", [F32])[0]
    ao = _row_fwd(_xattn_fn, "xattn", [(q, XA_WIDTH, 0)], [kx, vx], [(XA_WIDTH, BF16)], 256)[0]
    h2 = _matmul("xa_o", ao, wt["xa_wo"], "nn", [F32], _add_epilogue, (h1,))[0]

    f = _row_fwd(_rms_fn, "ffn_norm", [(h2, d, 0)], [wt["ffn_norm_w"]], [(d, BF16)], 256)[0]
    wt.update(_gather_landed(grp_b, _exchange_wait("late_gather_b_wait", handle_b, f)))
    a, hid = _matmul("ffn_up", f, wt["ffn_w1"], "nn", [F32, BF16],
                     lambda acc: (acc, jnp.square(jnp.maximum(acc, 0.0))))
    h3 = _matmul("ffn_down", hid, wt["ffn_w2"], "nn", [F32], _add_epilogue, (h2,))[0]
    loss8, dh3, g["final_norm_w"] = _loss_call(h3, target, wt["final_norm_w"])

    da = _matmul("ffn_down_dx", dh3, wt["ffn_w2"], "nt", [BF16],
                 lambda acc, av: (acc * 2.0 * jnp.maximum(av, 0.0),), (a,))[0]
    g["ffn_w2"] = _matmul("ffn_down_dw", hid, dh3, "tn", [BF16])[0]
    g["ffn_w1"] = _matmul("ffn_up_dw", f, da, "tn", [BF16])[0]
    df = _matmul("ffn_up_dx", da, wt["ffn_w1"], "nt", [F32])[0]
    pending = {}
    plan = _scatter_plan({n: g.pop(n) for n in grp_b})
    pending[grp_b], tok = _exchange_start("late_grad_b_start", *plan)
    (dh2,), (g["ffn_norm_w"],) = _row_bwd(_rms_res_fn, "ffn_norm_bwd", [(h2, d, 0)],
                                          [wt["ffn_norm_w"] + tok[0:1, 0:1]],
                                          [[(df, d, 0)], [(dh3, d, 0)]], 256)

    dao = _matmul("xa_o_dx", dh2, wt["xa_wo"], "nt", [F32])[0]
    g["xa_wo"] = _matmul("xa_o_dw", ao, dh2, "tn", [BF16])[0]
    (dq,), (dkx, dvx) = _row_bwd(_xattn_fn, "xattn_bwd", [(q, XA_WIDTH, 0)], [kx, vx], [[(dao, XA_WIDTH, 0)]], 256)
    dhn = _matmul("xa_q_dx", dq, wt["xa_wq"], "nt", [F32])[0]
    g["xa_wq"] = _matmul("xa_q_dw", hn, dq, "tn", [BF16])[0]
    g["xa_wk"] = _matmul("xa_k_dw", mn, dkx, "tn", [BF16])[0]
    g["xa_wv"] = _matmul("xa_v_dw", mn, dvx, "tn", [BF16])[0]
    dmn = _matmul("xa_k_dx", dkx, wt["xa_wk"], "nt", [F32])[0]
    dmn = _matmul("xa_v_dx", dvx, wt["xa_wv"], "nt", [F32], _add_epilogue, (dmn,))[0]
    _, (g["mem_norm_w"],) = _row_bwd(_rms_fn, "mem_norm_bwd", [(mem, d, 0)], [wt["mem_norm_w"]],
                                     [[(dmn, d, 0)]], 256, want_tiles=())
    (dh1,), (g["xa_norm_w"],) = _row_bwd(_rms_res_fn, "xa_norm_bwd", [(h1, d, 0)], [wt["xa_norm_w"]],
                                         [[(dhn, d, 0)], [(dh2, d, 0)]], 256)

    do_cat = _matmul("out_proj_dx", dh1, wt["w_out"], "nt", [F32])[0]
    g["w_out"] = _matmul("out_proj_dw", o_cat, dh1, "tn", [BF16])[0]

    plan = _scatter_plan({n: g.pop(n) for n in grp_a})
    pending[grp_a], tok = _exchange_start("late_grad_a_start", *plan)
    (dy, dr1, dk1, dv1, dgate), (g["rw_ln_w"], g["rw_ln_b"], g["rw_r_k"]) = _row_bwd(
        _rw_post_fn, "rw_post_bwd", rw_post_tiles, [rw_post_params[0] + tok[0:1, 0:1]] + rw_post_params[1:],
        [[(do_cat, RW_WIDTH, 1)]], 128)
    dr2, dlw, dk2, dv2, dal, dbe = _scan_bwd(_rw_group, "rw_scan_bwd", rw_arrs, st_rw, dy, RW_WIDTH // LANES)
    one = lambda t: [(t, RW_WIDTH, 0)]
    two = lambda s, t: [(s, RW_WIDTH, 0), (t, RW_WIDTH, 0)]
    d_ps, rw_pre_grads = _row_bwd(
        _rw_pre_fn, "rw_pre_bwd", rw_pre_tiles, rw_pre_params,
        [two(dr1, dr2), one(dlw), two(dk1, dk2), two(dv1, dv2), one(dal), one(dbe), one(dgate)], 128)
    for n, val in zip(("rw_w0", "rw_a0", "rw_k_k", "rw_k_a", "rw_w2", "rw_a2", "rw_g2"), rw_pre_grads):
        g[n] = val
    dp_rw, (g["rw_mu"],) = _col_bwd(_lerp_fn, "rw_shift_bwd", p, RW_OFF // LANES, 26, [wt["rw_mu"]],
                                    jnp.concatenate(d_ps, axis=1))

    (do, dz), (g["dn_norm_w"],) = _row_bwd(_dn_post_fn, "dn_post_bwd", dn_post_tiles, [wt["dn_norm_w"]],
                                           [[(do_cat, DN_WIDTH, 0)]], 256)
    dqh, dkh, dv_dn, dgb, dbb = _scan_bwd(_gdn_group, "gdn_scan_bwd", dn_arrs, st_dn, do, DN_HEADS)
    one = lambda t: [(t, DN_WIDTH, 0)]
    (dcq, dck, dgates), (g["dn_a_log"], g["dn_dt_bias"]) = _row_bwd(
        _dn_pre_fn, "dn_pre_bwd", dn_pre_tiles, dn_pre_params, [one(dqh), one(dkh), one(dgb), one(dbb)], 128)
    dp_qkv, (g["dn_conv_w"],) = _col_bwd(_conv_fn, "dn_conv_bwd", p, 0, 24, [wt["dn_conv_w"]],
                                         jnp.concatenate([dcq, dck, dv_dn], axis=1))
    dp = jnp.concatenate([dp_qkv, dz, dgates, dp_rw, jnp.zeros((x.shape[0], LANES), F32)], axis=1).astype(BF16)
    g["w_in"] = _matmul("in_proj_dw", u, dp, "tn", [BF16], tn=1536)[0]
    early = _logical_grads(g)
    pending[EARLY], tok = _exchange_start("early_grad_start", *_scatter_plan({n: early.pop(n) for n in EARLY}))
    du = _matmul("in_proj_dx", dp, wt["w_in"], "nt", [F32], after=tok)[0]
    (dx,), (early["mix_norm_w"],) = _row_bwd(_rms_res_fn, "mix_norm_bwd", [(x, d, 0)], [wt["mix_norm_w"]],
                                             [[(du, d, 0)], [(dh1, d, 0)]], 256)
    return loss8, dx, early, pending, tok


WEIGHTS = ["mix_norm_w", "w_in", "dn_conv_w", "dn_a_log", "dn_dt_bias", "dn_norm_w", "rw_mu", "rw_w0", "rw_w2",
           "rw_a0", "rw_a2", "rw_g2", "rw_k_k", "rw_k_a", "rw_r_k", "rw_ln_w", "rw_ln_b", "w_out", "xa_norm_w",
           "mem_norm_w", "xa_wq", "xa_wk", "xa_wv", "xa_wo", "ffn_norm_w", "ffn_w1", "ffn_w2", "final_norm_w"]
SHARDED = {"w_in": True, "w_out": False, "xa_wq": False, "xa_wk": False, "xa_wv": False, "xa_wo": True,
           "ffn_w1": True, "ffn_w2": False, "dn_conv_w": True, "rw_w2": True, "rw_a2": True, "rw_g2": True}
BF16_PAYLOAD = ("w_in", "w_out", "xa_wq", "xa_wk", "xa_wv", "xa_wo", "ffn_w1", "ffn_w2")
REPLICATED = [n for n in WEIGHTS if n not in SHARDED]
EARLY = ("w_in", "dn_conv_w", "rw_w2", "rw_a2", "rw_g2")
RW_IN_COLS = IN_COLS - DN_COLS


def _layout_weights(fw):
    wt = dict(fw)
    w_in = fw["w_in"]
    rows = w_in.shape[0]
    wt["w_in"] = jnp.concatenate(
        [w_in[:, :DN_COLS], jnp.zeros((rows, RW_OFF - DN_COLS), w_in.dtype), w_in[:, DN_COLS:],
         jnp.zeros((rows, IN_PAD - RW_OFF - RW_IN_COLS), w_in.dtype)], axis=1)
    wt["dn_conv_w"] = jnp.pad(fw["dn_conv_w"], ((0, 4), (0, 0)))
    wt["dn_a_log"] = jnp.pad(fw["dn_a_log"], ((0, 0), (0, LANES - DN_HEADS)))
    wt["dn_dt_bias"] = jnp.pad(fw["dn_dt_bias"], ((0, 0), (0, LANES - DN_HEADS)))
    wt["rw_w2"] = jnp.pad(fw["rw_w2"], ((0, 64), (0, 0)))
    wt["rw_a2"] = jnp.pad(fw["rw_a2"], ((64, 0), (0, 0)))
    return wt


def _logical_grads(g):
    out = dict(g)
    out["w_in"] = jnp.concatenate([g["w_in"][:, :DN_COLS], g["w_in"][:, RW_OFF:RW_OFF + RW_IN_COLS]], axis=1)
    out["dn_conv_w"] = g["dn_conv_w"][:4]
    out["dn_a_log"] = g["dn_a_log"][:, :DN_HEADS]
    out["dn_dt_bias"] = g["dn_dt_bias"][:, :DN_HEADS]
    out["rw_w2"] = g["rw_w2"][:64]
    out["rw_a2"] = g["rw_a2"][64:]
    return out


def _pack(vals):
    parts = []
    for v in vals:
        flat = v.reshape(-1)
        parts.append(jnp.pad(flat, (0, -flat.shape[0] % LANES)))
    flat = jnp.concatenate(parts)
    flat = jnp.pad(flat, (0, -flat.shape[0] % (8 * LANES)))
    return flat.reshape(-1, LANES)


def _unpack(packed, shapes):
    flat = packed.reshape(-1)
    out, at = [], 0
    for shp in shapes:
        size = math.prod(shp)
        out.append(flat[at:at + size].reshape(shp))
        at += size + (-size % LANES)
    return out


def kernel(x, mem, mix_norm_w, w_in, dn_conv_w, dn_a_log, dn_dt_bias, dn_norm_w, rw_mu, rw_w0, rw_w2, rw_a0, rw_a2, rw_g2, rw_k_k, rw_k_a, rw_r_k, rw_ln_w, rw_ln_b, w_out, xa_norm_w, mem_norm_w, xa_wq, xa_wk, xa_wv, xa_wo, ffn_norm_w, ffn_w1, ffn_w2, final_norm_w, loss_target, m_mix_norm_w, m_w_in, m_dn_conv_w, m_dn_a_log, m_dn_dt_bias, m_dn_norm_w, m_rw_mu, m_rw_w0, m_rw_w2, m_rw_a0, m_rw_a2, m_rw_g2, m_rw_k_k, m_rw_k_a, m_rw_r_k, m_rw_ln_w, m_rw_ln_b, m_w_out, m_xa_norm_w, m_mem_norm_w, m_xa_wq, m_xa_wk, m_xa_wv, m_xa_wo, m_ffn_norm_w, m_ffn_w1, m_ffn_w2, m_final_norm_w, v_mix_norm_w, v_w_in, v_dn_conv_w, v_dn_a_log, v_dn_dt_bias, v_dn_norm_w, v_rw_mu, v_rw_w0, v_rw_w2, v_rw_a0, v_rw_a2, v_rw_g2, v_rw_k_k, v_rw_k_a, v_rw_r_k, v_rw_ln_w, v_rw_ln_b, v_w_out, v_xa_norm_w, v_mem_norm_w, v_xa_wq, v_xa_wk, v_xa_wv, v_xa_wo, v_ffn_norm_w, v_ffn_w1, v_ffn_w2, v_final_norm_w):
    given = dict(locals())
    w = {n: given[n] for n in WEIGHTS}
    m = {n: given["m_" + n] for n in WEIGHTS}
    v = {n: given["v_" + n] for n in WEIGHTS}

    shards = {n: (w[n][0].astype(BF16) if n in BF16_PAYLOAD else w[n][0]) for n in SHARDED}
    srcs, dsts, _ = _gather_plan({n: shards[n] for n in EARLY})
    full = _gather_finish(EARLY, _exchange("early_all_gather", srcs, dsts, True))
    for n in REPLICATED:
        full[n] = w[n].reshape(1, -1)

    loss8, dx, g, pending, after = _local_step(x[0], mem[0], loss_target[0], _layout_weights(full),
                                               {n: shards[n] for n in SHARDED if n not in EARLY})
    loss = lax.psum(loss8[0, 0], ("x", "y", "c"))

    grad, delta, new_m, new_v = {}, {}, {}, {}
    for names in sorted(pending, key=lambda names: names == EARLY):
        handle = pending[names]
        waited = _exchange_wait("grad_wait_" + names[0], handle, after)
        for n, parts in zip(names, _scatter_landed(handle, waited)):
            res = _sum_adamw("adamw_" + n, parts, w[n][0], m[n][0], v[n][0])
            grad[n], delta[n], new_m[n], new_v[n] = [t[None] for t in res]
            after = res[1]

    packed = _pack([g[n] for n in REPLICATED])
    parts = _exchange("small_all_gather", [(packed, None)], [((N_DEV,) + packed.shape, F32, None)], True)[0]
    res = _sum_adamw("adamw_small", parts, _pack([w[n] for n in REPLICATED]),
                     _pack([m[n] for n in REPLICATED]), _pack([v[n] for n in REPLICATED]))
    shapes = [w[n].shape for n in REPLICATED]
    for store, packed_out in zip((grad, delta, new_m, new_v), res):
        for n, val in zip(REPLICATED, _unpack(packed_out, shapes)):
            store[n] = val

    return (loss, dx[None], *[grad[n] for n in WEIGHTS], *[delta[n] for n in WEIGHTS],
            *[new_m[n] for n in WEIGHTS], *[new_v[n] for n in WEIGHTS])
```

```python
import functools
import math

import jax
import jax.numpy as jnp
from jax import lax
from jax.experimental import pallas as pl
from jax.experimental.pallas import tpu as pltpu

F32 = jnp.float32
BF16 = jnp.bfloat16
SDS = jax.ShapeDtypeStruct

N_DEV = 8
D_MODEL = 2048
LANES = 128
CHUNK = 128
DN_HEADS = 8
DN_WIDTH = 1024
RW_WIDTH = 1024
RW_HEAD = 64
XA_HEADS = 4
XA_WIDTH = 512
FFN_HIDDEN = 8192
IN_COLS = 7440
DN_COLS = 4112
IN_PAD = 7680
RW_OFF = 4224
RMS_EPS = 1e-6
RW_GN_EPS = 64e-5
VMEM_LIMIT = 56 * 1024 * 1024

ADAM_LR = 0.001
ADAM_B1 = 0.9
ADAM_B2 = 0.999
ADAM_EPS = 1e-08
ADAM_WD = 0.01
ADAM_STEP = 10

_DIMS = {"nn": (((1,), (0,)), ((), ())), "nt": (((1,), (1,)), ((), ())), "tn": (((0,), (0,)), ((), ()))}


def _raw_dot(a, b, mode, hi):
    if hi:
        return lax.dot_general(a, b, _DIMS[mode], precision=lax.Precision.HIGHEST, preferred_element_type=F32)
    return lax.dot_general(a.astype(BF16), b.astype(BF16), _DIMS[mode], preferred_element_type=F32)


@functools.partial(jax.custom_vjp, nondiff_argnums=(2, 3))
def mm(a, b, mode="nn", hi=False):
    return _raw_dot(a, b, mode, hi)


def _mm_fwd(a, b, mode, hi):
    return _raw_dot(a, b, mode, hi), (a, b)


def _mm_bwd(mode, hi, res, g):
    a, b = res
    if mode == "nn":
        return _raw_dot(g, b, "nt", hi), _raw_dot(a, g, "tn", hi)
    if mode == "nt":
        return _raw_dot(g, b, "nn", hi), _raw_dot(g, a, "tn", hi)
    return _raw_dot(b, g, "nt", hi), _raw_dot(a, g, "nn", hi)


mm.defvjp(_mm_fwd, _mm_bwd)


def _shift_rows_raw(x, k):
    n = x.shape[0]
    rolled = pltpu.roll(x, k % n, axis=0)
    row = lax.broadcasted_iota(jnp.int32, x.shape, 0)
    keep = row >= k if k > 0 else row < n + k
    return jnp.where(keep, rolled, 0.0)


@functools.partial(jax.custom_vjp, nondiff_argnums=(1,))
def shift_rows(x, k):
    return _shift_rows_raw(x, k)


shift_rows.defvjp(lambda x, k: (_shift_rows_raw(x, k), None), lambda k, _, g: (_shift_rows_raw(g, -k),))


def _softplus(x):
    return jnp.maximum(x, 0.0) + jnp.log(1.0 + jnp.exp(-jnp.abs(x)))


def _sigmoid(x):
    return 1.0 / (1.0 + jnp.exp(-x))


def _silu(x):
    return x * _sigmoid(x)


def _tri_masks(n):
    ii = lax.broadcasted_iota(jnp.int32, (n, n), 0)
    jj = lax.broadcasted_iota(jnp.int32, (n, n), 1)
    return ii >= jj, ii > jj, ii == jj


def _neumann_inv_raw(m):
    n = m.shape[0]
    _, _, eye = _tri_masks(n)
    eye = jnp.where(eye, 1.0, 0.0)
    p = eye + m
    mk = m
    for _ in range(int(math.log2(n)) - 1):
        mk = _raw_dot(mk, mk, "nn", False)
        p = p + _raw_dot(p, mk, "nn", False)
    resid = eye - p + _raw_dot(m, p, "nn", True)
    return p + _raw_dot(p, resid, "nn", False)


@jax.custom_vjp
def _neumann_inv(m):
    return _neumann_inv_raw(m)


def _neumann_inv_fwd(m):
    p = _neumann_inv_raw(m)
    return p, p


def _neumann_inv_bwd(p, g):
    return (_raw_dot(_raw_dot(p, g, "tn", False), p, "nt", False),)


_neumann_inv.defvjp(_neumann_inv_fwd, _neumann_inv_bwd)


def _gdn_group(s0, q, k, v, gb, bb):
    c = q.shape[1]
    causal, _, _ = _tri_masks(c)
    lower = jnp.where(causal, 1.0, 0.0)
    gcs = [mm(lower, gb[j], "nn", True) for j in range(gb.shape[0])]
    gc = jnp.stack(gcs)
    diff = jnp.stack([t - t.T for t in gcs])
    return jax.vmap(_gdn_chunk)(s0, q, k, v, gb, bb, gc, diff)


def _rw_group(*args):
    return jax.vmap(_rw_chunk)(*args)


def _gdn_chunk(s0, q, k, v, gb, bb, gc, diff):
    c = q.shape[0]
    causal, strict, _ = _tri_masks(c)
    decay = jnp.exp(jnp.where(causal, diff, -jnp.inf))
    kb = k * bb
    a = jnp.where(strict, mm(kb, k, "nt") * decay, 0.0)
    p = _neumann_inv(-a)
    u = mm(p, v * bb)
    w = mm(p, kb * jnp.exp(gc))
    attn = mm(q, k, "nt") * decay
    v_new = u - mm(w, s0)
    o = mm(q * jnp.exp(gc), s0) + mm(attn, v_new)
    g_last = jnp.sum(gb, axis=0, keepdims=True)
    s1 = s0 * jnp.exp(g_last) + mm(k * jnp.exp(g_last - gc), v_new, "tn")
    return o, s1


def _rw_chunk(s0, r, lw, k, v, al, be):
    c = r.shape[0]
    causal, strict, _ = _tri_masks(c)
    gc = mm(jnp.where(causal, 1.0, 0.0), lw, "nn", True)
    gp = gc - lw
    row = lax.broadcasted_iota(jnp.int32, lw.shape, 0)
    lane = lax.broadcasted_iota(jnp.int32, lw.shape, 1)
    g_mid = jnp.sum(jnp.where(row < c // 2, lw, 0.0), axis=0, keepdims=True)
    g_last = jnp.sum(lw, axis=0, keepdims=True)
    e_n = jnp.exp(g_mid - gc)
    rg = r * jnp.exp(gc - g_mid)
    bg = be * jnp.exp(gp - g_mid)
    an = al * e_n
    kn = k * e_n
    bt = mm(be * jnp.exp(gp), s0, "nt")
    rt = mm(r * jnp.exp(gc), s0, "nt")
    us, ys = [], []
    for h in range(2):
        mine = (lane >= RW_HEAD) if h else (lane < RW_HEAD)
        bgh = jnp.where(mine, bg, 0.0)
        rgh = jnp.where(mine, rg, 0.0)
        a_ab = jnp.where(strict, mm(bgh, an, "nt"), 0.0)
        a_kb = jnp.where(strict, mm(bgh, kn, "nt"), 0.0)
        a_ra = jnp.where(causal, mm(rgh, an, "nt"), 0.0)
        a_rk = jnp.where(causal, mm(rgh, kn, "nt"), 0.0)
        p = _neumann_inv(a_ab)
        u_h = mm(p, bt + mm(a_kb, v))
        us.append(u_h)
        ys.append(rt + mm(a_ra, u_h) + mm(a_rk, v))
    lo = lane < RW_HEAD
    u = jnp.where(lo, us[0], us[1])
    y = jnp.where(lo, ys[0], ys[1])
    tail = jnp.exp(g_last - gc)
    s1 = s0 * jnp.exp(g_last) + mm(u, al * tail, "tn") + mm(v, k * tail, "tn")
    vi = lax.broadcasted_iota(jnp.int32, s0.shape, 0)
    ki = lax.broadcasted_iota(jnp.int32, s0.shape, 1)
    s1 = jnp.where((vi < RW_HEAD) == (ki < RW_HEAD), s1, 0.0)
    return y, s1


SCAN_HB = 8


def _scan_specs(arrs, n_chunks, reverse):
    def spec(off):
        assert off % SCAN_HB == 0
        if reverse:
            return pl.BlockSpec((CHUNK, SCAN_HB * LANES), lambda h, n: (n_chunks - 1 - n, off // SCAN_HB + h))
        return pl.BlockSpec((CHUNK, SCAN_HB * LANES), lambda h, n: (n, off // SCAN_HB + h))
    return [spec(off) for _, off in arrs]


def _split_heads(x):
    return jnp.stack([x[:, LANES * j:LANES * (j + 1)] for j in range(SCAN_HB)], axis=0)


def _merge_heads(x):
    return jnp.concatenate([x[j] for j in range(SCAN_HB)], axis=1)


def _scan_fwd(group_fn, name, arrs, heads):
    s = arrs[0][0].shape[0]
    n_chunks = s // CHUNK
    n_in = len(arrs)

    def body(*refs):
        y_ref, st_ref, s_scr = refs[n_in:]

        @pl.when(pl.program_id(1) == 0)
        def _():
            s_scr[...] = jnp.zeros_like(s_scr)

        s0 = s_scr[...]
        st_ref[...] = s0
        y, s1 = group_fn(s0, *[_split_heads(r[...]) for r in refs[:n_in]])
        y_ref[...] = _merge_heads(y)
        s_scr[...] = s1

    return pl.pallas_call(
        body, grid=(heads // SCAN_HB, n_chunks), name=name,
        in_specs=_scan_specs(arrs, n_chunks, False),
        out_specs=[pl.BlockSpec((CHUNK, SCAN_HB * LANES), lambda h, n: (n, h)),
                   pl.BlockSpec((SCAN_HB, None, LANES, LANES), lambda h, n: (h, n, 0, 0))],
        out_shape=[SDS((s, heads * LANES), F32), SDS((heads, n_chunks, LANES, LANES), F32)],
        scratch_shapes=[pltpu.VMEM((SCAN_HB, LANES, LANES), F32)],
        compiler_params=pltpu.CompilerParams(dimension_semantics=("arbitrary", "arbitrary")),
    )(*[a for a, _ in arrs])


def _scan_bwd(group_fn, name, arrs, states, dy, heads):
    s = arrs[0][0].shape[0]
    n_chunks = s // CHUNK
    n_in = len(arrs)

    def body(*refs):
        st_ref, dy_ref = refs[n_in:n_in + 2]
        d_refs = refs[n_in + 2:2 * n_in + 2]
        ds_scr = refs[-1]

        @pl.when(pl.program_id(1) == 0)
        def _():
            ds_scr[...] = jnp.zeros_like(ds_scr)

        _, vjp = jax.vjp(group_fn, st_ref[...], *[_split_heads(r[...]) for r in refs[:n_in]])
        grads = vjp((_split_heads(dy_ref[...]), ds_scr[...]))
        ds_scr[...] = grads[0]
        for ref, g in zip(d_refs, grads[1:]):
            ref[...] = _merge_heads(g)

    rev = pl.BlockSpec((CHUNK, SCAN_HB * LANES), lambda h, n: (n_chunks - 1 - n, h))
    return pl.pallas_call(
        body, grid=(heads // SCAN_HB, n_chunks), name=name,
        in_specs=_scan_specs(arrs, n_chunks, True)
        + [pl.BlockSpec((SCAN_HB, None, LANES, LANES), lambda h, n: (h, n_chunks - 1 - n, 0, 0)), rev],
        out_specs=[rev] * n_in,
        out_shape=[SDS((s, heads * LANES), F32)] * n_in,
        scratch_shapes=[pltpu.VMEM((SCAN_HB, LANES, LANES), F32)],
        compiler_params=pltpu.CompilerParams(dimension_semantics=("arbitrary", "arbitrary")),
    )(*[a for a, _ in arrs], states, dy)


def _col_spec(tr, width, cb):
    return pl.BlockSpec((tr, width), lambda i: (i, cb))


def _whole(p):
    return pl.BlockSpec(p.shape, lambda i: (0,) * p.ndim)


def _row_fwd(fn, name, tiles, params, outs, tr):
    rows = tiles[0][0].shape[0]
    nt, npar = len(tiles), len(params)

    def body(*refs):
        vals = [r[...].astype(F32) for r in refs[:nt + npar]]
        for ref, o in zip(refs[nt + npar:], fn(*vals)):
            ref[...] = o.astype(ref.dtype)

    return pl.pallas_call(
        body, grid=(rows // tr,), name=name,
        in_specs=[_col_spec(tr, w, cb) for _, w, cb in tiles] + [_whole(p) for p in params],
        out_specs=[_col_spec(tr, w, 0) for w, _ in outs],
        out_shape=[SDS((rows, w), dt) for w, dt in outs],
        compiler_params=pltpu.CompilerParams(dimension_semantics=("arbitrary",), vmem_limit_bytes=VMEM_LIMIT),
    )(*[a for a, _, _ in tiles], *params)


def _row_bwd(fn, name, tiles, params, cts, tr, want_tiles=None):
    rows = tiles[0][0].shape[0]
    nt, npar = len(tiles), len(params)
    want = list(range(nt)) if want_tiles is None else list(want_tiles)
    flat_cts = [c for group in cts for c in group]
    n_ct = len(flat_cts)

    def body(*refs):
        vals = [r[...].astype(F32) for r in refs[:nt + npar]]
        ct_refs = refs[nt + npar:nt + npar + n_ct]
        out_refs = refs[nt + npar + n_ct:]
        ct_vals, at = [], 0
        for group in cts:
            total = ct_refs[at][...].astype(F32)
            for r in ct_refs[at + 1:at + len(group)]:
                total = total + r[...].astype(F32)
            ct_vals.append(total)
            at += len(group)
        _, vjp = jax.vjp(lambda *a: tuple(fn(*a)), *vals)
        grads = vjp(tuple(ct_vals))
        for ref, t in zip(out_refs[:len(want)], want):
            ref[...] = grads[t]
        first = pl.program_id(0) == 0
        for ref, g in zip(out_refs[len(want):], grads[nt:]):
            @pl.when(first)
            def _(ref=ref, g=g):
                ref[...] = g

            @pl.when(jnp.logical_not(first))
            def _(ref=ref, g=g):
                ref[...] += g

    res = pl.pallas_call(
        body, grid=(rows // tr,), name=name,
        in_specs=[_col_spec(tr, w, cb) for _, w, cb in tiles] + [_whole(p) for p in params]
        + [_col_spec(tr, w, cb) for _, w, cb in flat_cts],
        out_specs=[_col_spec(tr, tiles[t][1], 0) for t in want] + [_whole(p) for p in params],
        out_shape=[SDS((rows, tiles[t][1]), F32) for t in want] + [SDS(p.shape, F32) for p in params],
        compiler_params=pltpu.CompilerParams(dimension_semantics=("arbitrary",), vmem_limit_bytes=VMEM_LIMIT),
    )(*[a for a, _, _ in tiles], *params, *[a for a, _, _ in flat_cts])
    return res[:len(want)], res[len(want):]


def _col_fwd(fn, name, x, first_block, n_blocks, params):
    rows = x.shape[0]

    def body(*refs):
        refs[-1][...] = fn(*[r[...] for r in refs[:-1]])

    return pl.pallas_call(
        body, grid=(n_blocks,), name=name,
        in_specs=[pl.BlockSpec((rows, LANES), lambda j: (0, first_block + j))]
        + [pl.BlockSpec((p.shape[0], LANES), lambda j: (0, j)) for p in params],
        out_specs=pl.BlockSpec((rows, LANES), lambda j: (0, j)),
        out_shape=SDS((rows, n_blocks * LANES), F32),
        compiler_params=pltpu.CompilerParams(dimension_semantics=("arbitrary",), vmem_limit_bytes=VMEM_LIMIT),
    )(x, *params)


def _col_bwd(fn, name, x, first_block, n_blocks, params, dy):
    rows = x.shape[0]
    npar = len(params)

    def body(*refs):
        vals = [r[...] for r in refs[:1 + npar]]
        _, vjp = jax.vjp(fn, *vals)
        grads = vjp(refs[1 + npar][...])
        for ref, g in zip(refs[2 + npar:], grads):
            ref[...] = g

    pspecs = [pl.BlockSpec((p.shape[0], LANES), lambda j: (0, j)) for p in params]
    blk = pl.BlockSpec((rows, LANES), lambda j: (0, j))
    res = pl.pallas_call(
        body, grid=(n_blocks,), name=name,
        in_specs=[pl.BlockSpec((rows, LANES), lambda j: (0, first_block + j))] + pspecs + [blk],
        out_specs=[blk] + pspecs,
        out_shape=[SDS((rows, n_blocks * LANES), F32)] + [SDS(p.shape, F32) for p in params],
        compiler_params=pltpu.CompilerParams(dimension_semantics=("arbitrary",), vmem_limit_bytes=VMEM_LIMIT),
    )(x, *params, dy)
    return res[0], res[1:]


def _conv_fn(x, w):
    acc = x * w[3:4, :]
    for j in range(3):
        acc = acc + shift_rows(x, 3 - j) * w[j:j + 1, :]
    return _silu(acc)


def _lerp_fn(x, mu):
    return x + (shift_rows(x, 1) - x) * mu[0:1, :]


def _seg_sum(x, width):
    if width == LANES:
        return jnp.sum(x, axis=1, keepdims=True)
    lo = lax.broadcasted_iota(jnp.int32, x.shape, 1) < width
    s0 = jnp.sum(jnp.where(lo, x, 0.0), axis=1, keepdims=True)
    s1 = jnp.sum(jnp.where(lo, 0.0, x), axis=1, keepdims=True)
    return jnp.where(lo, s0, s1)


def _per_block(fn, *xs):
    n = xs[0].shape[1] // LANES
    return jnp.concatenate([fn(*[x[:, LANES * b:LANES * (b + 1)] for x in xs]) for b in range(n)], axis=1)


def _head_expand(col0):
    r = lax.broadcasted_iota(jnp.int32, (LANES, DN_WIDTH), 0)
    c = lax.shift_right_logical(lax.broadcasted_iota(jnp.int32, (LANES, DN_WIDTH), 1), 7)
    return jnp.where(r == c + col0, 1.0, 0.0)


def _dn_pre_fn(cq, ck, gates, a_log, dt_bias):
    l2 = lambda x: x * lax.rsqrt(_seg_sum(x * x, LANES) + 1e-6)
    qh = _per_block(l2, cq) * (LANES ** -0.5)
    kh = _per_block(l2, ck)
    g = -jnp.exp(a_log) * _softplus(gates + dt_bias)
    gb = mm(g, _head_expand(0), "nn", True)
    bb = mm(_sigmoid(gates), _head_expand(DN_HEADS), "nn", True)
    return qh, kh, gb, bb


def _dn_post_fn(o, z, nw):
    def one(ob, zb):
        return ob * lax.rsqrt(_seg_sum(ob * ob, LANES) * (1.0 / LANES) + RMS_EPS) * nw * _silu(zb)
    return (_per_block(one, o, z),)


def _rw_pre_fn(pr, pk, pv, pwa, pg, w0, a0, k_k, k_a, w2p, a2p, g2):
    log_w = -_softplus(-(w0 + mm(jnp.tanh(pwa), w2p))) - 0.5
    lw = -jnp.exp(log_w)
    a = _sigmoid(a0 + mm(pwa, a2p))
    gate = mm(_sigmoid(pg), g2)
    kk = pk * k_k
    kk = _per_block(lambda x: x / jnp.maximum(jnp.sqrt(_seg_sum(x * x, RW_HEAD)), 1e-12), kk)
    k = pk * (1.0 + (a - 1.0) * k_a)
    return pr, lw, k, pv, kk * a, -kk, gate


def _rw_post_fn(y, r, k, v, gate, ln_w, ln_b, r_k):
    def one(yb, rb, kb, vb, gb, wb, bb, rkb):
        d = yb - _seg_sum(yb, RW_HEAD) * (1.0 / RW_HEAD)
        var = _seg_sum(d * d, RW_HEAD) * (1.0 / RW_HEAD)
        yn = d * lax.rsqrt(var + RW_GN_EPS) * wb + bb
        return (yn + _seg_sum(rb * kb * rkb, RW_HEAD) * vb) * gb
    return (_per_block(one, y, r, k, v, gate, ln_w, ln_b, r_k),)


def _rms_fn(h, w):
    return (h * lax.rsqrt(jnp.mean(h * h, axis=1, keepdims=True) + RMS_EPS) * w,)


def _xattn_fn(q, k, v):
    outs = []
    for h in range(XA_HEADS):
        sl = slice(LANES * h, LANES * (h + 1))
        s = mm(q[:, sl], k[:, sl], "nt") * (LANES ** -0.5)
        e = jnp.exp(s - jnp.max(s, axis=1, keepdims=True))
        outs.append(mm(e / jnp.sum(e, axis=1, keepdims=True), v[:, sl]))
    return (jnp.concatenate(outs, axis=1),)


def _fit(tile, dim):
    best = [t for t in range(LANES, min(tile, dim) + 1, LANES) if dim % t == 0]
    assert best, (tile, dim)
    return best[-1]


def _matmul(name, a, b, mode, out_dtypes, epilogue=None, extras=(), tm=1024, tn=1024, tk=2048, after=None):
    if mode == "tn":
        (k_dim, m), n = a.shape, b.shape[1]
    else:
        (m, k_dim), n = a.shape, (b.shape[1] if mode == "nn" else b.shape[0])
    tm, tn, tk = _fit(tm, m), _fit(tn, n), _fit(tk, k_dim)
    nk = k_dim // tk
    a_spec = (pl.BlockSpec((tk, tm), lambda i, j, k: (k, i)) if mode == "tn"
              else pl.BlockSpec((tm, tk), lambda i, j, k: (i, k)))
    b_spec = (pl.BlockSpec((tn, tk), lambda i, j, k: (j, k)) if mode == "nt"
              else pl.BlockSpec((tk, tn), lambda i, j, k: (k, j)))
    o_spec = pl.BlockSpec((tm, tn), lambda i, j, k: (i, j))
    n_ex, n_out = len(extras), len(out_dtypes)
    ties = [] if after is None else [after]

    def finish(total, rest):
        ex = [r[...].astype(F32) for r in rest[:n_ex]]
        res = epilogue(total, *ex) if epilogue else (total,)
        for ref, o in zip(rest[n_ex + len(ties):n_ex + len(ties) + n_out], res):
            ref[...] = o.astype(ref.dtype)

    def body_single(a_ref, b_ref, *rest):
        finish(_raw_dot(a_ref[...], b_ref[...], mode, False), rest)

    def body_acc(a_ref, b_ref, *rest):
        acc = rest[-1]
        k = pl.program_id(2)

        @pl.when(k == 0)
        def _():
            acc[...] = jnp.zeros_like(acc)

        acc[...] += _raw_dot(a_ref[...], b_ref[...], mode, False)

        @pl.when(k == nk - 1)
        def _():
            finish(acc[...], rest)

    res = pl.pallas_call(
        body_single if nk == 1 else body_acc, grid=(m // tm, n // tn, nk), name=name,
        in_specs=[a_spec, b_spec] + [o_spec] * n_ex + [pl.BlockSpec((8, LANES), lambda i, j, k: (0, 0))] * len(ties),
        out_specs=[o_spec] * n_out,
        out_shape=[SDS((m, n), dt) for dt in out_dtypes],
        scratch_shapes=[] if nk == 1 else [pltpu.VMEM((tm, tn), F32)],
        compiler_params=pltpu.CompilerParams(dimension_semantics=("parallel", "parallel", "arbitrary"),
                                             vmem_limit_bytes=VMEM_LIMIT),
    )(a, b, *extras, *ties)
    return res


def _loss_call(h, target, w, tr=256):
    rows, d = h.shape

    def fn(hv, wv, tv):
        y = _rms_fn(hv, wv)[0]
        return 0.5 * jnp.sum(jnp.mean(jnp.square(y - tv), axis=1, keepdims=True), axis=0, keepdims=True)

    def body(h_ref, t_ref, w_ref, loss_ref, dh_ref, dw_ref):
        tv = t_ref[...]
        val, vjp = jax.vjp(lambda hv, wv: fn(hv, wv, tv), h_ref[...], w_ref[...])
        dh, dw = vjp(jnp.ones((1, 1), F32))
        dh_ref[...] = dh
        first = pl.program_id(0) == 0

        @pl.when(first)
        def _():
            loss_ref[...] = jnp.broadcast_to(val, loss_ref.shape)
            dw_ref[...] = dw

        @pl.when(jnp.logical_not(first))
        def _():
            loss_ref[...] += jnp.broadcast_to(val, loss_ref.shape)
            dw_ref[...] += dw

    return pl.pallas_call(
        body, grid=(rows // tr,), name="loss_head",
        in_specs=[_col_spec(tr, d, 0), _col_spec(tr, d, 0), _whole(w)],
        out_specs=[pl.BlockSpec((8, LANES), lambda i: (0, 0)), _col_spec(tr, d, 0), _whole(w)],
        out_shape=[SDS((8, LANES), F32), SDS((rows, d), F32), SDS(w.shape, F32)],
        compiler_params=pltpu.CompilerParams(dimension_semantics=("arbitrary",), vmem_limit_bytes=VMEM_LIMIT),
    )(h, target, w)


def _adamw_vals(w, g, m, v):
    m = ADAM_B1 * m + (1.0 - ADAM_B1) * g
    v = ADAM_B2 * v + (1.0 - ADAM_B2) * jnp.square(g)
    m_hat = m / (1.0 - ADAM_B1 ** ADAM_STEP)
    v_hat = v / (1.0 - ADAM_B2 ** ADAM_STEP)
    delta = -ADAM_LR * (m_hat / (jnp.sqrt(v_hat) + ADAM_EPS) + ADAM_WD * w)
    return delta, m, v


def _sum_adamw(name, parts, w, m, v):
    r, c = w.shape
    tr = r
    for cand in (512, 256, 128, 64, 32, 16, 8):
        if r % cand == 0 and N_DEV * cand * c * 4 <= 6 * 1024 * 1024:
            tr = cand
            break

    def body(p_ref, w_ref, m_ref, v_ref, g_ref, d_ref, m2_ref, v2_ref):
        g = p_ref[0].astype(F32)
        for s in range(1, N_DEV):
            g = g + p_ref[s].astype(F32)
        g_ref[...] = g
        d_ref[...], m2_ref[...], v2_ref[...] = _adamw_vals(w_ref[...], g, m_ref[...], v_ref[...])

    blk = pl.BlockSpec((tr, c), lambda i: (i, 0))
    return pl.pallas_call(
        body, grid=(r // tr,), name=name,
        in_specs=[pl.BlockSpec((N_DEV, tr, c), lambda i: (0, i, 0)), blk, blk, blk],
        out_specs=[blk] * 4, out_shape=[SDS((r, c), F32)] * 4,
        compiler_params=pltpu.CompilerParams(dimension_semantics=("arbitrary",), vmem_limit_bytes=VMEM_LIMIT),
    )(parts, w, m, v)


def _peers():
    x, y, c = lax.axis_index("x"), lax.axis_index("y"), lax.axis_index("c")
    peers = []
    for k in range(1, N_DEV):
        px = 1 - x if k & 4 else x
        py = 1 - y if k & 2 else y
        pc = 1 - c if k & 1 else c
        peers.append(((px, py, pc), 4 * px + 2 * py + pc))
    return 4 * x + 2 * y + c, peers


def _slot(ref, idx, cols):
    if cols is None:
        return ref.at[idx]
    return ref.at[:, pl.ds(pl.multiple_of(idx * cols, LANES), cols)]


def _exchange(name, srcs, dsts, gather):
    n = len(srcs)

    def body(*refs):
        start, wait = _exchange_ops([c for _, c in srcs], [c for _, _, c in dsts], gather,
                                    refs[:n], refs[n:2 * n], *refs[2 * n:])
        start()
        wait()

    any_spec = pl.BlockSpec(memory_space=pl.ANY)
    return pl.pallas_call(
        body, name=name,
        in_specs=[any_spec] * n, out_specs=[any_spec] * n,
        out_shape=[SDS(shape, dt) for shape, dt, _ in dsts],
        scratch_shapes=_exchange_sems(n),
    )(*[a for a, _ in srcs])


def _gather_two_level(name, srcs, dsts):
    n = len(srcs)
    dst_cols = [c for _, _, c in dsts]

    def body(*refs):
        src_refs, out_refs = refs[:n], refs[n:2 * n]
        send_sems, recv_sems, local_sems = refs[2 * n:]
        x, y, c = lax.axis_index("x"), lax.axis_index("y"), lax.axis_index("c")
        index = lambda px, py, pc: 4 * px + 2 * py + pc
        me, sibling = index(x, y, c), (x, y, 1 - c)
        chips = [(x, 1 - y), (1 - x, y), (1 - x, 1 - y)]

        def copy(a, k, src, block, to):
            return pltpu.make_async_remote_copy(
                src_ref=src, dst_ref=_slot(out_refs[a], block, dst_cols[a]),
                send_sem=send_sems.at[a, k], recv_sem=recv_sems.at[a, k],
                device_id=to, device_id_type=pl.DeviceIdType.MESH)

        local, first, passed = [], [], []
        for a in range(n):
            cp = pltpu.make_async_copy(src_refs[a], _slot(out_refs[a], me, dst_cols[a]), local_sems.at[a])
            cp.start()
            local.append(cp)
            first.append(copy(a, 0, src_refs[a], me, sibling))
            first += [copy(a, 1 + j, src_refs[a], me, (*chip, c)) for j, chip in enumerate(chips)]
        for cp in first:
            cp.start()
        for a in range(n):
            for j, chip in enumerate(chips):
                block = index(*chip, c)
                arrived = _slot(out_refs[a], block, dst_cols[a])
                copy(a, 1 + j, arrived, block, (*chip, c)).wait_recv()
                passed.append(copy(a, 4 + j, arrived, block, sibling))
                passed[-1].start()
        for a in range(n):
            copy(a, 0, src_refs[a], index(x, y, 1 - c), sibling).wait_recv()
            for j, chip in enumerate(chips):
                block = index(*chip, 1 - c)
                copy(a, 4 + j, src_refs[a], block, sibling).wait_recv()
        for cp in first + passed:
            cp.wait_send()
        for cp in local:
            cp.wait()

    any_spec = pl.BlockSpec(memory_space=pl.ANY)
    return pl.pallas_call(
        body, name=name,
        in_specs=[any_spec] * n, out_specs=[any_spec] * n,
        out_shape=[SDS(shape, dt) for shape, dt, _ in dsts],
        scratch_shapes=_exchange_sems(n),
    )(*[a for a, _ in srcs])


_HBM = pl.BlockSpec(memory_space=pltpu.HBM)
_SEM = pl.BlockSpec(memory_space=pltpu.SEMAPHORE)
_EFFECT = pltpu.SideEffectType.DATAFLOW_SIDE_EFFECTING


def _split_copies(src_cols, dst_cols, gather, src_refs, land_refs, send_sems, recv_sems, landings):
    me, peers = _peers()
    out = []
    for a, (s_cols, d_cols) in enumerate(zip(src_cols, dst_cols)):
        for k, (pos, idx) in enumerate(peers):
            blk = src_refs[a] if gather else _slot(src_refs[a], idx, s_cols)
            out.append(pltpu.make_async_remote_copy(
                src_ref=blk, dst_ref=_slot(land_refs[a], idx if landings else me, d_cols),
                send_sem=send_sems.at[a * (N_DEV - 1) + k], recv_sem=recv_sems.at[a * (N_DEV - 1) + k],
                device_id=pos, device_id_type=pl.DeviceIdType.MESH))
    return out


def _exchange_start(name, srcs, dsts, gather):
    n = len(srcs)
    src_cols, dst_cols = [c for _, c in srcs], [c for _, _, c in dsts]

    def body(*refs):
        src_refs, land_refs = refs[:n], refs[n:2 * n]
        send_sems, recv_sems = refs[2 * n:2 * n + 2]
        token = refs[-1]
        for cp in _split_copies(src_cols, dst_cols, gather, src_refs, land_refs, send_sems, recv_sems, False):
            cp.start()
        token[...] = jnp.zeros_like(token)

    hbm = lambda a: pltpu.with_memory_space_constraint(a, pltpu.HBM)
    lands = [hbm(lax.empty(shape, dt)) for shape, dt, _ in dsts]
    res = pl.pallas_call(
        body, name=name,
        out_shape=(pltpu.SemaphoreType.DMA((n * (N_DEV - 1),)), pltpu.SemaphoreType.DMA((n * (N_DEV - 1),)),
                   *[pltpu.HBM(a.shape, a.dtype) for a, _ in srcs], *[pltpu.HBM(a.shape, a.dtype) for a in lands],
                   SDS((8, LANES), F32)),
        in_specs=[_HBM] * (2 * n),
        out_specs=(_SEM, _SEM, *[_HBM] * (2 * n), pl.BlockSpec(memory_space=pltpu.VMEM)),
        input_output_aliases={i: 2 + i for i in range(2 * n)},
        compiler_params=pltpu.CompilerParams(has_side_effects=_EFFECT),
    )(*[hbm(a) for a, _ in srcs], *lands)
    handle = (res[0], res[1], res[2:2 + n], res[2 + n:2 + 2 * n], src_cols, dst_cols, gather)
    return handle, res[-1]


def _exchange_wait(name, handle, after):
    send_sems, recv_sems, src_thru, land_thru, src_cols, dst_cols, gather = handle
    n = len(src_thru)

    def body(*refs):
        src_refs, land_refs = refs[:n], refs[n:2 * n]
        s_sems, r_sems = refs[2 * n:2 * n + 2]
        for cp in _split_copies(src_cols, dst_cols, gather, src_refs, land_refs, s_sems, r_sems, True):
            cp.wait_send()
            cp.wait_recv()

    res = pl.pallas_call(
        body, name=name,
        out_shape=tuple(pltpu.HBM(a.shape, a.dtype) for a in (*src_thru, *land_thru)),
        in_specs=[_HBM] * (2 * n) + [_SEM, _SEM, pl.BlockSpec(memory_space=pl.ANY)],
        out_specs=tuple([_HBM] * (2 * n)),
        input_output_aliases={i: i for i in range(2 * n)},
        compiler_params=pltpu.CompilerParams(has_side_effects=_EFFECT),
    )(*src_thru, *land_thru, send_sems, recv_sems, after)
    return res[:n], res[n:]


def _exchange_sems(n):
    return [pltpu.SemaphoreType.DMA((n, N_DEV - 1)), pltpu.SemaphoreType.DMA((n, N_DEV - 1)),
            pltpu.SemaphoreType.DMA((n,))]


def _exchange_ops(src_cols, dst_cols, gather, src_refs, out_refs, send_sems, recv_sems, local_sems):
    def copies(with_landings):
        me, peers = _peers()
        local, sends, landings = [], [], []
        for a, (s_cols, d_cols) in enumerate(zip(src_cols, dst_cols)):
            mine = src_refs[a] if gather else _slot(src_refs[a], me, s_cols)
            local.append(pltpu.make_async_copy(mine, _slot(out_refs[a], me, d_cols), local_sems.at[a]))
            for k, (pos, idx) in enumerate(peers):
                out_blk = src_refs[a] if gather else _slot(src_refs[a], idx, s_cols)
                both = dict(src_ref=out_blk, send_sem=send_sems.at[a, k], recv_sem=recv_sems.at[a, k],
                            device_id=pos, device_id_type=pl.DeviceIdType.MESH)
                sends.append(pltpu.make_async_remote_copy(dst_ref=_slot(out_refs[a], me, d_cols), **both))
                if with_landings:
                    landings.append(pltpu.make_async_remote_copy(dst_ref=_slot(out_refs[a], idx, d_cols), **both))
        return local, sends, landings

    def start():
        local, sends, _ = copies(False)
        for cp in local + sends:
            cp.start()

    def wait():
        local, sends, landings = copies(True)
        for cp in landings:
            cp.wait_recv()
        for cp in sends:
            cp.wait_send()
        for cp in local:
            cp.wait()

    return start, wait


def _rms_res_fn(h, w):
    return _rms_fn(h, w)[0], h


def _add_epilogue(acc, res):
    return (acc + res,)


def _gather_plan(shards):
    srcs, dsts = [], []
    for n, sh in shards.items():
        r, c = sh.shape
        srcs.append((sh, None))
        if SHARDED[n] and c % LANES == 0:
            dsts.append(((r, N_DEV * c), sh.dtype, c))
        else:
            dsts.append(((N_DEV, r, c), sh.dtype, None))
    return srcs, dsts, True


def _gather_finish(names, outs):
    full = {}
    for n, arr in zip(names, outs):
        if arr.ndim == 2:
            full[n] = arr
        elif SHARDED[n]:
            full[n] = arr.transpose(1, 0, 2).reshape(arr.shape[1], -1)
        else:
            full[n] = arr.reshape(-1, arr.shape[2])
    return full


def _my_index():
    return 4 * lax.axis_index("x") + 2 * lax.axis_index("y") + lax.axis_index("c")


def _gather_landed(names, waited):
    me = _my_index()
    outs = []
    for sh, buf in zip(*waited):
        if buf.ndim == 3:
            outs.append(lax.dynamic_update_slice(buf, sh[None], (me, 0, 0)))
        else:
            outs.append(lax.dynamic_update_slice(buf, sh, (0, me * sh.shape[1])))
    return _gather_finish(names, outs)


def _scatter_plan(grads):
    srcs, dsts = [], []
    for n, gr in grads.items():
        rows, cols = gr.shape
        if not SHARDED[n]:
            r, c = rows // N_DEV, cols
            srcs.append((gr.reshape(N_DEV, r, c), None))
        else:
            r, c = rows, cols // N_DEV
            if c % LANES == 0:
                srcs.append((gr, c))
            else:
                srcs.append((gr.reshape(r, N_DEV, c).transpose(1, 0, 2), None))
        dsts.append(((N_DEV, r, c), gr.dtype, None))
    return srcs, dsts, False


def _scatter_landed(handle, waited):
    me = _my_index()
    outs = []
    for src, cols, buf in zip(waited[0], handle[4], waited[1]):
        if cols is None:
            own = lax.dynamic_index_in_dim(src, me, 0, keepdims=True)
        else:
            own = lax.dynamic_slice(src, (0, me * cols), (src.shape[0], cols))[None]
        outs.append(lax.dynamic_update_slice(buf, own, (me, 0, 0)))
    return outs


def _local_step(x, mem, target, wt, late):
    d = D_MODEL
    g = {}
    wt = dict(wt)
    grp_a = ("w_out", "xa_wq", "xa_wk", "xa_wv", "xa_wo")
    grp_b = ("ffn_w1", "ffn_w2")
    handle_a, tok_a = _exchange_start("late_gather_a_start", *_gather_plan({n: late[n] for n in grp_a}))
    handle_b, tok_b = _exchange_start("late_gather_b_start", *_gather_plan({n: late[n] for n in grp_b}))
    mix_w = wt["mix_norm_w"] + (tok_a[0:1, 0:1] + tok_b[0:1, 0:1])
    u = _row_fwd(_rms_fn, "mix_norm", [(x, d, 0)], [mix_w], [(d, BF16)], 256)[0]
    p = _matmul("in_proj", u, wt["w_in"], "nn", [F32], tn=1536)[0]
    c = _col_fwd(_conv_fn, "dn_conv", p, 0, 24, [wt["dn_conv_w"]])
    dn_pre_tiles = [(c, DN_WIDTH, 0), (c, DN_WIDTH, 1), (p, LANES, 32)]
    dn_pre_params = [wt["dn_a_log"], wt["dn_dt_bias"]]
    qh, kh, gb, bb = _row_fwd(_dn_pre_fn, "dn_pre", dn_pre_tiles, dn_pre_params, [(DN_WIDTH, F32)] * 4, 128)
    dn_arrs = [(qh, 0), (kh, 0), (c, 16), (gb, 0), (bb, 0)]
    o, st_dn = _scan_fwd(_gdn_group, "gdn_scan", dn_arrs, DN_HEADS)
    dn_post_tiles = [(o, DN_WIDTH, 0), (p, DN_WIDTH, 3)]
    o_dn = _row_fwd(_dn_post_fn, "dn_post", dn_post_tiles, [wt["dn_norm_w"]], [(DN_WIDTH, BF16)], 256)[0]

    ps = _col_fwd(_lerp_fn, "rw_shift", p, RW_OFF // LANES, 26, [wt["rw_mu"]])
    rw_pre_tiles = [(ps, RW_WIDTH, 0), (ps, RW_WIDTH, 1), (ps, RW_WIDTH, 2), (ps, LANES, 24), (ps, LANES, 25)]
    rw_pre_params = [wt[n] for n in ("rw_w0", "rw_a0", "rw_k_k", "rw_k_a", "rw_w2", "rw_a2", "rw_g2")]
    r, lw, k, v, al, be, gate = _row_fwd(_rw_pre_fn, "rw_pre", rw_pre_tiles, rw_pre_params,
                                         [(RW_WIDTH, F32)] * 7, 128)
    rw_arrs = [(r, 0), (lw, 0), (k, 0), (v, 0), (al, 0), (be, 0)]
    y, st_rw = _scan_fwd(_rw_group, "rw_scan", rw_arrs, RW_WIDTH // LANES)
    rw_post_tiles = [(t, RW_WIDTH, 0) for t in (y, r, k, v, gate)]
    rw_post_params = [wt["rw_ln_w"], wt["rw_ln_b"], wt["rw_r_k"]]
    o_rw = _row_fwd(_rw_post_fn, "rw_post", rw_post_tiles, rw_post_params, [(RW_WIDTH, BF16)], 128)[0]
    o_cat = jnp.concatenate([o_dn, o_rw], axis=1)
    wt.update(_gather_landed(grp_a, _exchange_wait("late_gather_a_wait", handle_a, o_cat)))
    h1 = _matmul("out_proj", o_cat, wt["w_out"], "nn", [F32], _add_epilogue, (x,))[0]

    hn = _row_fwd(_rms_fn, "xa_norm", [(h1, d, 0)], [wt["xa_norm_w"]], [(d, BF16)], 256)[0]
    mn = _row_fwd(_rms_fn, "mem_norm", [(mem, d, 0)], [wt["mem_norm_w"]], [(d, BF16)], 256)[0]
    q = _matmul("xa_q", hn, wt["xa_wq"], "nn", [F32])[0]
    kx = _matmul("xa_k", mn, wt["xa_wk"], "nn", [F32])[0]
    vx = _matmul("xa_v", mn, wt["xa_wv"], "nn", [F32])[0]
    ao = _row_fwd(_xattn_fn, "xattn", [(q, XA_WIDTH, 0)], [kx, vx], [(XA_WIDTH, BF16)], 256)[0]
    h2 = _matmul("xa_o", ao, wt["xa_wo"], "nn", [F32], _add_epilogue, (h1,))[0]

    f = _row_fwd(_rms_fn, "ffn_norm", [(h2, d, 0)], [wt["ffn_norm_w"]], [(d, BF16)], 256)[0]
    wt.update(_gather_landed(grp_b, _exchange_wait("late_gather_b_wait", handle_b, f)))
    a, hid = _matmul("ffn_up", f, wt["ffn_w1"], "nn", [F32, BF16],
                     lambda acc: (acc, jnp.square(jnp.maximum(acc, 0.0))))
    h3 = _matmul("ffn_down", hid, wt["ffn_w2"], "nn", [F32], _add_epilogue, (h2,))[0]
    loss8, dh3, g["final_norm_w"] = _loss_call(h3, target, wt["final_norm_w"])

    da = _matmul("ffn_down_dx", dh3, wt["ffn_w2"], "nt", [BF16],
                 lambda acc, av: (acc * 2.0 * jnp.maximum(av, 0.0),), (a,))[0]
    g["ffn_w2"] = _matmul("ffn_down_dw", hid, dh3, "tn", [BF16])[0]
    g["ffn_w1"] = _matmul("ffn_up_dw", f, da, "tn", [BF16])[0]
    df = _matmul("ffn_up_dx", da, wt["ffn_w1"], "nt", [F32])[0]
    pending = {}
    plan = _scatter_plan({n: g.pop(n) for n in grp_b})
    pending[grp_b], tok = _exchange_start("late_grad_b_start", *plan)
    (dh2,), (g["ffn_norm_w"],) = _row_bwd(_rms_res_fn, "ffn_norm_bwd", [(h2, d, 0)],
                                          [wt["ffn_norm_w"] + tok[0:1, 0:1]],
                                          [[(df, d, 0)], [(dh3, d, 0)]], 256)

    dao = _matmul("xa_o_dx", dh2, wt["xa_wo"], "nt", [F32])[0]
    g["xa_wo"] = _matmul("xa_o_dw", ao, dh2, "tn", [BF16])[0]
    (dq,), (dkx, dvx) = _row_bwd(_xattn_fn, "xattn_bwd", [(q, XA_WIDTH, 0)], [kx, vx], [[(dao, XA_WIDTH, 0)]], 256)
    dhn = _matmul("xa_q_dx", dq, wt["xa_wq"], "nt", [F32])[0]
    g["xa_wq"] = _matmul("xa_q_dw", hn, dq, "tn", [BF16])[0]
    g["xa_wk"] = _matmul("xa_k_dw", mn, dkx, "tn", [BF16])[0]
    g["xa_wv"] = _matmul("xa_v_dw", mn, dvx, "tn", [BF16])[0]
    dmn = _matmul("xa_k_dx", dkx, wt["xa_wk"], "nt", [F32])[0]
    dmn = _matmul("xa_v_dx", dvx, wt["xa_wv"], "nt", [F32], _add_epilogue, (dmn,))[0]
    _, (g["mem_norm_w"],) = _row_bwd(_rms_fn, "mem_norm_bwd", [(mem, d, 0)], [wt["mem_norm_w"]],
                                     [[(dmn, d, 0)]], 256, want_tiles=())
    (dh1,), (g["xa_norm_w"],) = _row_bwd(_rms_res_fn, "xa_norm_bwd", [(h1, d, 0)], [wt["xa_norm_w"]],
                                         [[(dhn, d, 0)], [(dh2, d, 0)]], 256)

    do_cat = _matmul("out_proj_dx", dh1, wt["w_out"], "nt", [F32])[0]
    g["w_out"] = _matmul("out_proj_dw", o_cat, dh1, "tn", [BF16])[0]

    plan = _scatter_plan({n: g.pop(n) for n in grp_a})
    pending[grp_a], tok = _exchange_start("late_grad_a_start", *plan)
    (dy, dr1, dk1, dv1, dgate), (g["rw_ln_w"], g["rw_ln_b"], g["rw_r_k"]) = _row_bwd(
        _rw_post_fn, "rw_post_bwd", rw_post_tiles, [rw_post_params[0] + tok[0:1, 0:1]] + rw_post_params[1:],
        [[(do_cat, RW_WIDTH, 1)]], 128)
    dr2, dlw, dk2, dv2, dal, dbe = _scan_bwd(_rw_group, "rw_scan_bwd", rw_arrs, st_rw, dy, RW_WIDTH // LANES)
    one = lambda t: [(t, RW_WIDTH, 0)]
    two = lambda s, t: [(s, RW_WIDTH, 0), (t, RW_WIDTH, 0)]
    d_ps, rw_pre_grads = _row_bwd(
        _rw_pre_fn, "rw_pre_bwd", rw_pre_tiles, rw_pre_params,
        [two(dr1, dr2), one(dlw), two(dk1, dk2), two(dv1, dv2), one(dal), one(dbe), one(dgate)], 128)
    for n, val in zip(("rw_w0", "rw_a0", "rw_k_k", "rw_k_a", "rw_w2", "rw_a2", "rw_g2"), rw_pre_grads):
        g[n] = val
    dp_rw, (g["rw_mu"],) = _col_bwd(_lerp_fn, "rw_shift_bwd", p, RW_OFF // LANES, 26, [wt["rw_mu"]],
                                    jnp.concatenate(d_ps, axis=1))

    (do, dz), (g["dn_norm_w"],) = _row_bwd(_dn_post_fn, "dn_post_bwd", dn_post_tiles, [wt["dn_norm_w"]],
                                           [[(do_cat, DN_WIDTH, 0)]], 256)
    dqh, dkh, dv_dn, dgb, dbb = _scan_bwd(_gdn_group, "gdn_scan_bwd", dn_arrs, st_dn, do, DN_HEADS)
    one = lambda t: [(t, DN_WIDTH, 0)]
    (dcq, dck, dgates), (g["dn_a_log"], g["dn_dt_bias"]) = _row_bwd(
        _dn_pre_fn, "dn_pre_bwd", dn_pre_tiles, dn_pre_params, [one(dqh), one(dkh), one(dgb), one(dbb)], 128)
    dp_qkv, (g["dn_conv_w"],) = _col_bwd(_conv_fn, "dn_conv_bwd", p, 0, 24, [wt["dn_conv_w"]],
                                         jnp.concatenate([dcq, dck, dv_dn], axis=1))
    dp = jnp.concatenate([dp_qkv, dz, dgates, dp_rw, jnp.zeros((x.shape[0], LANES), F32)], axis=1).astype(BF16)
    g["w_in"] = _matmul("in_proj_dw", u, dp, "tn", [BF16], tn=1536)[0]
    early = _logical_grads(g)
    pending[EARLY], tok = _exchange_start("early_grad_start", *_scatter_plan({n: early.pop(n) for n in EARLY}))
    du = _matmul("in_proj_dx", dp, wt["w_in"], "nt", [F32], after=tok)[0]
    (dx,), (early["mix_norm_w"],) = _row_bwd(_rms_res_fn, "mix_norm_bwd", [(x, d, 0)], [wt["mix_norm_w"]],
                                             [[(du, d, 0)], [(dh1, d, 0)]], 256)
    return loss8, dx, early, pending, tok


WEIGHTS = ["mix_norm_w", "w_in", "dn_conv_w", "dn_a_log", "dn_dt_bias", "dn_norm_w", "rw_mu", "rw_w0", "rw_w2",
           "rw_a0", "rw_a2", "rw_g2", "rw_k_k", "rw_k_a", "rw_r_k", "rw_ln_w", "rw_ln_b", "w_out", "xa_norm_w",
           "mem_norm_w", "xa_wq", "xa_wk", "xa_wv", "xa_wo", "ffn_norm_w", "ffn_w1", "ffn_w2", "final_norm_w"]
SHARDED = {"w_in": True, "w_out": False, "xa_wq": False, "xa_wk": False, "xa_wv": False, "xa_wo": True,
           "ffn_w1": True, "ffn_w2": False, "dn_conv_w": True, "rw_w2": True, "rw_a2": True, "rw_g2": True}
BF16_PAYLOAD = ("w_in", "w_out", "xa_wq", "xa_wk", "xa_wv", "xa_wo", "ffn_w1", "ffn_w2")
REPLICATED = [n for n in WEIGHTS if n not in SHARDED]
EARLY = ("w_in", "dn_conv_w", "rw_w2", "rw_a2", "rw_g2")
RW_IN_COLS = IN_COLS - DN_COLS


def _layout_weights(fw):
    wt = dict(fw)
    w_in = fw["w_in"]
    rows = w_in.shape[0]
    wt["w_in"] = jnp.concatenate(
        [w_in[:, :DN_COLS], jnp.zeros((rows, RW_OFF - DN_COLS), w_in.dtype), w_in[:, DN_COLS:],
         jnp.zeros((rows, IN_PAD - RW_OFF - RW_IN_COLS), w_in.dtype)], axis=1)
    wt["dn_conv_w"] = jnp.pad(fw["dn_conv_w"], ((0, 4), (0, 0)))
    wt["dn_a_log"] = jnp.pad(fw["dn_a_log"], ((0, 0), (0, LANES - DN_HEADS)))
    wt["dn_dt_bias"] = jnp.pad(fw["dn_dt_bias"], ((0, 0), (0, LANES - DN_HEADS)))
    wt["rw_w2"] = jnp.pad(fw["rw_w2"], ((0, 64), (0, 0)))
    wt["rw_a2"] = jnp.pad(fw["rw_a2"], ((64, 0), (0, 0)))
    return wt


def _logical_grads(g):
    out = dict(g)
    out["w_in"] = jnp.concatenate([g["w_in"][:, :DN_COLS], g["w_in"][:, RW_OFF:RW_OFF + RW_IN_COLS]], axis=1)
    out["dn_conv_w"] = g["dn_conv_w"][:4]
    out["dn_a_log"] = g["dn_a_log"][:, :DN_HEADS]
    out["dn_dt_bias"] = g["dn_dt_bias"][:, :DN_HEADS]
    out["rw_w2"] = g["rw_w2"][:64]
    out["rw_a2"] = g["rw_a2"][64:]
    return out


def _pack(vals):
    parts = []
    for v in vals:
        flat = v.reshape(-1)
        parts.append(jnp.pad(flat, (0, -flat.shape[0] % LANES)))
    flat = jnp.concatenate(parts)
    flat = jnp.pad(flat, (0, -flat.shape[0] % (8 * LANES)))
    return flat.reshape(-1, LANES)


def _unpack(packed, shapes):
    flat = packed.reshape(-1)
    out, at = [], 0
    for shp in shapes:
        size = math.prod(shp)
        out.append(flat[at:at + size].reshape(shp))
        at += size + (-size % LANES)
    return out


def kernel(x, mem, mix_norm_w, w_in, dn_conv_w, dn_a_log, dn_dt_bias, dn_norm_w, rw_mu, rw_w0, rw_w2, rw_a0, rw_a2, rw_g2, rw_k_k, rw_k_a, rw_r_k, rw_ln_w, rw_ln_b, w_out, xa_norm_w, mem_norm_w, xa_wq, xa_wk, xa_wv, xa_wo, ffn_norm_w, ffn_w1, ffn_w2, final_norm_w, loss_target, m_mix_norm_w, m_w_in, m_dn_conv_w, m_dn_a_log, m_dn_dt_bias, m_dn_norm_w, m_rw_mu, m_rw_w0, m_rw_w2, m_rw_a0, m_rw_a2, m_rw_g2, m_rw_k_k, m_rw_k_a, m_rw_r_k, m_rw_ln_w, m_rw_ln_b, m_w_out, m_xa_norm_w, m_mem_norm_w, m_xa_wq, m_xa_wk, m_xa_wv, m_xa_wo, m_ffn_norm_w, m_ffn_w1, m_ffn_w2, m_final_norm_w, v_mix_norm_w, v_w_in, v_dn_conv_w, v_dn_a_log, v_dn_dt_bias, v_dn_norm_w, v_rw_mu, v_rw_w0, v_rw_w2, v_rw_a0, v_rw_a2, v_rw_g2, v_rw_k_k, v_rw_k_a, v_rw_r_k, v_rw_ln_w, v_rw_ln_b, v_w_out, v_xa_norm_w, v_mem_norm_w, v_xa_wq, v_xa_wk, v_xa_wv, v_xa_wo, v_ffn_norm_w, v_ffn_w1, v_ffn_w2, v_final_norm_w):
    given = dict(locals())
    w = {n: given[n] for n in WEIGHTS}
    m = {n: given["m_" + n] for n in WEIGHTS}
    v = {n: given["v_" + n] for n in WEIGHTS}

    shards = {n: (w[n][0].astype(BF16) if n in BF16_PAYLOAD else w[n][0]) for n in SHARDED}
    srcs, dsts, _ = _gather_plan({n: shards[n] for n in EARLY})
    full = _gather_finish(EARLY, _gather_two_level("early_all_gather", srcs, dsts))
    for n in REPLICATED:
        full[n] = w[n].reshape(1, -1)

    loss8, dx, g, pending, after = _local_step(x[0], mem[0], loss_target[0], _layout_weights(full),
                                               {n: shards[n] for n in SHARDED if n not in EARLY})
    loss = lax.psum(loss8[0, 0], ("x", "y", "c"))

    packed = _pack([g[n] for n in REPLICATED])
    small, _ = _exchange_start("small_gather_start", [(packed, None)], [((N_DEV,) + packed.shape, F32, None)], True)
    grad, delta, new_m, new_v = {}, {}, {}, {}
    for names in sorted(pending, key=lambda names: names == EARLY):
        handle = pending[names]
        waited = _exchange_wait("grad_wait_" + names[0], handle, after)
        for n, parts in zip(names, _scatter_landed(handle, waited)):
            res = _sum_adamw("adamw_" + n, parts, w[n][0], m[n][0], v[n][0])
            grad[n], delta[n], new_m[n], new_v[n] = [t[None] for t in res]
            after = res[1]

    (packed,), (parts,) = _exchange_wait("small_gather_wait", small, after)
    parts = lax.dynamic_update_slice(parts, packed[None], (_my_index(), 0, 0))
    res = _sum_adamw("adamw_small", parts, _pack([w[n] for n in REPLICATED]),
                     _pack([m[n] for n in REPLICATED]), _pack([v[n] for n in REPLICATED]))
    shapes = [w[n].shape for n in REPLICATED]
    for store, packed_out in zip((grad, delta, new_m, new_v), res):
        for n, val in zip(REPLICATED, _unpack(packed_out, shapes)):
            store[n] = val

    return (loss, dx[None], *[grad[n] for n in WEIGHTS], *[delta[n] for n in WEIGHTS],
            *[new_m[n] for n in WEIGHTS], *[new_v[n] for n in WEIGHTS])
```

```python
import functools
import math

import jax
import jax.numpy as jnp
from jax import lax
from jax.experimental import pallas as pl
from jax.experimental.pallas import tpu as pltpu

F32 = jnp.float32
BF16 = jnp.bfloat16
SDS = jax.ShapeDtypeStruct

N_DEV = 8
D_MODEL = 2048
LANES = 128
CHUNK = 128
DN_HEADS = 8
DN_WIDTH = 1024
RW_WIDTH = 1024
RW_HEAD = 64
XA_HEADS = 4
XA_WIDTH = 512
FFN_HIDDEN = 8192
IN_COLS = 7440
DN_COLS = 4112
IN_PAD = 7680
RW_OFF = 4224
RMS_EPS = 1e-6
RW_GN_EPS = 64e-5
VMEM_LIMIT = 56 * 1024 * 1024

ADAM_LR = 0.001
ADAM_B1 = 0.9
ADAM_B2 = 0.999
ADAM_EPS = 1e-08
ADAM_WD = 0.01
ADAM_STEP = 10

_DIMS = {"nn": (((1,), (0,)), ((), ())), "nt": (((1,), (1,)), ((), ())), "tn": (((0,), (0,)), ((), ()))}


def _raw_dot(a, b, mode, hi):
    if hi:
        return lax.dot_general(a, b, _DIMS[mode], precision=lax.Precision.HIGHEST, preferred_element_type=F32)
    return lax.dot_general(a.astype(BF16), b.astype(BF16), _DIMS[mode], preferred_element_type=F32)


@functools.partial(jax.custom_vjp, nondiff_argnums=(2, 3))
def mm(a, b, mode="nn", hi=False):
    return _raw_dot(a, b, mode, hi)


def _mm_fwd(a, b, mode, hi):
    return _raw_dot(a, b, mode, hi), (a, b)


def _mm_bwd(mode, hi, res, g):
    a, b = res
    if mode == "nn":
        return _raw_dot(g, b, "nt", hi), _raw_dot(a, g, "tn", hi)
    if mode == "nt":
        return _raw_dot(g, b, "nn", hi), _raw_dot(g, a, "tn", hi)
    return _raw_dot(b, g, "nt", hi), _raw_dot(a, g, "nn", hi)


mm.defvjp(_mm_fwd, _mm_bwd)


def _shift_rows_raw(x, k):
    n = x.shape[0]
    rolled = pltpu.roll(x, k % n, axis=0)
    row = lax.broadcasted_iota(jnp.int32, x.shape, 0)
    keep = row >= k if k > 0 else row < n + k
    return jnp.where(keep, rolled, 0.0)


@functools.partial(jax.custom_vjp, nondiff_argnums=(1,))
def shift_rows(x, k):
    return _shift_rows_raw(x, k)


shift_rows.defvjp(lambda x, k: (_shift_rows_raw(x, k), None), lambda k, _, g: (_shift_rows_raw(g, -k),))


def _softplus(x):
    return jnp.maximum(x, 0.0) + jnp.log(1.0 + jnp.exp(-jnp.abs(x)))


def _sigmoid(x):
    return 1.0 / (1.0 + jnp.exp(-x))


def _silu(x):
    return x * _sigmoid(x)


def _tri_masks(n):
    ii = lax.broadcasted_iota(jnp.int32, (n, n), 0)
    jj = lax.broadcasted_iota(jnp.int32, (n, n), 1)
    return ii >= jj, ii > jj, ii == jj


def _neumann_inv_raw(m):
    n = m.shape[0]
    _, _, eye = _tri_masks(n)
    eye = jnp.where(eye, 1.0, 0.0)
    p = eye + m
    mk = m
    for _ in range(int(math.log2(n)) - 1):
        mk = _raw_dot(mk, mk, "nn", False)
        p = p + _raw_dot(p, mk, "nn", False)
    resid = eye - p + _raw_dot(m, p, "nn", True)
    return p + _raw_dot(p, resid, "nn", False)


@jax.custom_vjp
def _neumann_inv(m):
    return _neumann_inv_raw(m)


def _neumann_inv_fwd(m):
    p = _neumann_inv_raw(m)
    return p, p


def _neumann_inv_bwd(p, g):
    return (_raw_dot(_raw_dot(p, g, "tn", False), p, "nt", False),)


_neumann_inv.defvjp(_neumann_inv_fwd, _neumann_inv_bwd)


@jax.custom_vjp
def _saved_inv(m, p):
    return p


_saved_inv.defvjp(lambda m, p: (p, p), lambda p, g: (_neumann_inv_bwd(p, g)[0], jnp.zeros_like(p)))


def _inverse(m, saved):
    return _neumann_inv(m) if saved is None else _saved_inv(m, saved)


def _cumsum_rows(x):
    causal, _, _ = _tri_masks(x.shape[0])
    return mm(jnp.where(causal, 1.0, 0.0), x, "nn", True)


def _gdn_group(s0, q, k, v, gb, bb, gc, *saved):
    diff = jnp.stack([gc[j] - gc[j].T for j in range(gc.shape[0])])
    return jax.vmap(_gdn_chunk)(s0, q, k, v, gb, bb, gc, diff, *saved)


def _rw_group(*args):
    return jax.vmap(_rw_chunk)(*args)


def _gdn_chunk(s0, q, k, v, gb, bb, gc, diff, saved=None):
    c = q.shape[0]
    causal, strict, _ = _tri_masks(c)
    decay = jnp.exp(jnp.where(causal, diff, -jnp.inf))
    kb = k * bb
    a = jnp.where(strict, mm(kb, k, "nt") * decay, 0.0)
    p = _inverse(-a, saved)
    u = mm(p, v * bb)
    w = mm(p, kb * jnp.exp(gc))
    attn = mm(q, k, "nt") * decay
    v_new = u - mm(w, s0)
    o = mm(q * jnp.exp(gc), s0) + mm(attn, v_new)
    g_last = jnp.sum(gb, axis=0, keepdims=True)
    s1 = s0 * jnp.exp(g_last) + mm(k * jnp.exp(g_last - gc), v_new, "tn")
    return o, s1, p


def _rw_chunk(s0, r, lw, k, v, al, be, gc, saved0=None, saved1=None):
    c = r.shape[0]
    causal, strict, _ = _tri_masks(c)
    gp = gc - lw
    row = lax.broadcasted_iota(jnp.int32, lw.shape, 0)
    lane = lax.broadcasted_iota(jnp.int32, lw.shape, 1)
    g_mid = jnp.sum(jnp.where(row < c // 2, lw, 0.0), axis=0, keepdims=True)
    g_last = jnp.sum(lw, axis=0, keepdims=True)
    e_n = jnp.exp(g_mid - gc)
    rg = r * jnp.exp(gc - g_mid)
    bg = be * jnp.exp(gp - g_mid)
    an = al * e_n
    kn = k * e_n
    bt = mm(be * jnp.exp(gp), s0, "nt")
    rt = mm(r * jnp.exp(gc), s0, "nt")
    us, ys, ps = [], [], []
    for h, saved in enumerate((saved0, saved1)):
        mine = (lane >= RW_HEAD) if h else (lane < RW_HEAD)
        bgh = jnp.where(mine, bg, 0.0)
        rgh = jnp.where(mine, rg, 0.0)
        a_ab = jnp.where(strict, mm(bgh, an, "nt"), 0.0)
        a_kb = jnp.where(strict, mm(bgh, kn, "nt"), 0.0)
        a_ra = jnp.where(causal, mm(rgh, an, "nt"), 0.0)
        a_rk = jnp.where(causal, mm(rgh, kn, "nt"), 0.0)
        p = _inverse(a_ab, saved)
        ps.append(p)
        u_h = mm(p, bt + mm(a_kb, v))
        us.append(u_h)
        ys.append(rt + mm(a_ra, u_h) + mm(a_rk, v))
    lo = lane < RW_HEAD
    u = jnp.where(lo, us[0], us[1])
    y = jnp.where(lo, ys[0], ys[1])
    tail = jnp.exp(g_last - gc)
    s1 = s0 * jnp.exp(g_last) + mm(u, al * tail, "tn") + mm(v, k * tail, "tn")
    vi = lax.broadcasted_iota(jnp.int32, s0.shape, 0)
    ki = lax.broadcasted_iota(jnp.int32, s0.shape, 1)
    s1 = jnp.where((vi < RW_HEAD) == (ki < RW_HEAD), s1, 0.0)
    return y, s1, ps[0], ps[1]


SCAN_HB = 8


def _scan_specs(arrs, n_chunks, reverse):
    def spec(off):
        assert off % SCAN_HB == 0
        if reverse:
            return pl.BlockSpec((CHUNK, SCAN_HB * LANES), lambda h, n: (n_chunks - 1 - n, off // SCAN_HB + h))
        return pl.BlockSpec((CHUNK, SCAN_HB * LANES), lambda h, n: (n, off // SCAN_HB + h))
    return [spec(off) for _, off in arrs]


def _split_heads(x):
    return jnp.stack([x[:, LANES * j:LANES * (j + 1)] for j in range(SCAN_HB)], axis=0)


def _merge_heads(x):
    return jnp.concatenate([x[j] for j in range(SCAN_HB)], axis=1)


def _scan_fwd(group_fn, name, arrs, heads, n_kept):
    s = arrs[0][0].shape[0]
    n_chunks = s // CHUNK
    n_in = len(arrs)

    def body(*refs):
        y_ref, st_ref = refs[n_in:n_in + 2]
        kept_refs, s_scr = refs[n_in + 2:-1], refs[-1]

        @pl.when(pl.program_id(1) == 0)
        def _():
            s_scr[...] = jnp.zeros_like(s_scr)

        s0 = s_scr[...]
        st_ref[...] = s0
        y, s1, *kept = group_fn(s0, *[_split_heads(r[...]) for r in refs[:n_in]])
        y_ref[...] = _merge_heads(y)
        s_scr[...] = s1
        for ref, val in zip(kept_refs, kept):
            ref[...] = val

    per_chunk = pl.BlockSpec((SCAN_HB, None, LANES, LANES), lambda h, n: (h, n, 0, 0))
    res = pl.pallas_call(
        body, grid=(heads // SCAN_HB, n_chunks), name=name,
        in_specs=_scan_specs(arrs, n_chunks, False),
        out_specs=[pl.BlockSpec((CHUNK, SCAN_HB * LANES), lambda h, n: (n, h))] + [per_chunk] * (1 + n_kept),
        out_shape=[SDS((s, heads * LANES), F32)] + [SDS((heads, n_chunks, LANES, LANES), F32)] * (1 + n_kept),
        scratch_shapes=[pltpu.VMEM((SCAN_HB, LANES, LANES), F32)],
        compiler_params=pltpu.CompilerParams(dimension_semantics=("arbitrary", "arbitrary")),
    )(*[a for a, _ in arrs])
    return res[0], res[1:]


def _scan_bwd(group_fn, name, arrs, kept, dy, heads):
    s = arrs[0][0].shape[0]
    n_chunks = s // CHUNK
    n_in, n_kept = len(arrs), len(kept)

    def body(*refs):
        kept_vals = [r[...] for r in refs[n_in:n_in + n_kept]]
        dy_ref = refs[n_in + n_kept]
        d_refs = refs[n_in + n_kept + 1:2 * n_in + n_kept + 1]
        ds_scr = refs[-1]

        @pl.when(pl.program_id(1) == 0)
        def _():
            ds_scr[...] = jnp.zeros_like(ds_scr)

        def fn(s0, *ins):
            return group_fn(s0, *ins, *kept_vals[1:])[:2]

        _, vjp = jax.vjp(fn, kept_vals[0], *[_split_heads(r[...]) for r in refs[:n_in]])
        grads = vjp((_split_heads(dy_ref[...]), ds_scr[...]))
        ds_scr[...] = grads[0]
        for ref, g in zip(d_refs, grads[1:]):
            ref[...] = _merge_heads(g)

    rev = pl.BlockSpec((CHUNK, SCAN_HB * LANES), lambda h, n: (n_chunks - 1 - n, h))
    per_chunk = pl.BlockSpec((SCAN_HB, None, LANES, LANES), lambda h, n: (h, n_chunks - 1 - n, 0, 0))
    return pl.pallas_call(
        body, grid=(heads // SCAN_HB, n_chunks), name=name,
        in_specs=_scan_specs(arrs, n_chunks, True) + [per_chunk] * n_kept + [rev],
        out_specs=[rev] * n_in,
        out_shape=[SDS((s, heads * LANES), F32)] * n_in,
        scratch_shapes=[pltpu.VMEM((SCAN_HB, LANES, LANES), F32)],
        compiler_params=pltpu.CompilerParams(dimension_semantics=("arbitrary", "arbitrary")),
    )(*[a for a, _ in arrs], *kept, dy)


def _col_spec(tr, width, cb):
    return pl.BlockSpec((tr, width), lambda i: (i, cb))


def _whole(p):
    return pl.BlockSpec(p.shape, lambda i: (0,) * p.ndim)


def _row_fwd(fn, name, tiles, params, outs, tr):
    rows = tiles[0][0].shape[0]
    nt, npar = len(tiles), len(params)

    def body(*refs):
        vals = [r[...].astype(F32) for r in refs[:nt + npar]]
        for ref, o in zip(refs[nt + npar:], fn(*vals)):
            ref[...] = o.astype(ref.dtype)

    return pl.pallas_call(
        body, grid=(rows // tr,), name=name,
        in_specs=[_col_spec(tr, w, cb) for _, w, cb in tiles] + [_whole(p) for p in params],
        out_specs=[_col_spec(tr, w, 0) for w, _ in outs],
        out_shape=[SDS((rows, w), dt) for w, dt in outs],
        compiler_params=pltpu.CompilerParams(dimension_semantics=("arbitrary",), vmem_limit_bytes=VMEM_LIMIT),
    )(*[a for a, _, _ in tiles], *params)


def _row_bwd(fn, name, tiles, params, cts, tr, want_tiles=None):
    rows = tiles[0][0].shape[0]
    nt, npar = len(tiles), len(params)
    want = list(range(nt)) if want_tiles is None else list(want_tiles)
    flat_cts = [c for group in cts for c in group]
    n_ct = len(flat_cts)

    def body(*refs):
        vals = [r[...].astype(F32) for r in refs[:nt + npar]]
        ct_refs = refs[nt + npar:nt + npar + n_ct]
        out_refs = refs[nt + npar + n_ct:]
        ct_vals, at = [], 0
        for group in cts:
            total = ct_refs[at][...].astype(F32)
            for r in ct_refs[at + 1:at + len(group)]:
                total = total + r[...].astype(F32)
            ct_vals.append(total)
            at += len(group)
        _, vjp = jax.vjp(lambda *a: tuple(fn(*a)), *vals)
        grads = vjp(tuple(ct_vals))
        for ref, t in zip(out_refs[:len(want)], want):
            ref[...] = grads[t]
        first = pl.program_id(0) == 0
        for ref, g in zip(out_refs[len(want):], grads[nt:]):
            @pl.when(first)
            def _(ref=ref, g=g):
                ref[...] = g

            @pl.when(jnp.logical_not(first))
            def _(ref=ref, g=g):
                ref[...] += g

    res = pl.pallas_call(
        body, grid=(rows // tr,), name=name,
        in_specs=[_col_spec(tr, w, cb) for _, w, cb in tiles] + [_whole(p) for p in params]
        + [_col_spec(tr, w, cb) for _, w, cb in flat_cts],
        out_specs=[_col_spec(tr, tiles[t][1], 0) for t in want] + [_whole(p) for p in params],
        out_shape=[SDS((rows, tiles[t][1]), F32) for t in want] + [SDS(p.shape, F32) for p in params],
        compiler_params=pltpu.CompilerParams(dimension_semantics=("arbitrary",), vmem_limit_bytes=VMEM_LIMIT),
    )(*[a for a, _, _ in tiles], *params, *[a for a, _, _ in flat_cts])
    return res[:len(want)], res[len(want):]


def _col_fwd(fn, name, x, first_block, n_blocks, params):
    rows = x.shape[0]

    def body(*refs):
        refs[-1][...] = fn(*[r[...] for r in refs[:-1]])

    return pl.pallas_call(
        body, grid=(n_blocks,), name=name,
        in_specs=[pl.BlockSpec((rows, LANES), lambda j: (0, first_block + j))]
        + [pl.BlockSpec((p.shape[0], LANES), lambda j: (0, j)) for p in params],
        out_specs=pl.BlockSpec((rows, LANES), lambda j: (0, j)),
        out_shape=SDS((rows, n_blocks * LANES), F32),
        compiler_params=pltpu.CompilerParams(dimension_semantics=("arbitrary",), vmem_limit_bytes=VMEM_LIMIT),
    )(x, *params)


def _col_bwd(fn, name, x, first_block, n_blocks, params, dy):
    rows = x.shape[0]
    npar = len(params)

    def body(*refs):
        vals = [r[...] for r in refs[:1 + npar]]
        _, vjp = jax.vjp(fn, *vals)
        grads = vjp(refs[1 + npar][...])
        for ref, g in zip(refs[2 + npar:], grads):
            ref[...] = g

    pspecs = [pl.BlockSpec((p.shape[0], LANES), lambda j: (0, j)) for p in params]
    blk = pl.BlockSpec((rows, LANES), lambda j: (0, j))
    res = pl.pallas_call(
        body, grid=(n_blocks,), name=name,
        in_specs=[pl.BlockSpec((rows, LANES), lambda j: (0, first_block + j))] + pspecs + [blk],
        out_specs=[blk] + pspecs,
        out_shape=[SDS((rows, n_blocks * LANES), F32)] + [SDS(p.shape, F32) for p in params],
        compiler_params=pltpu.CompilerParams(dimension_semantics=("arbitrary",), vmem_limit_bytes=VMEM_LIMIT),
    )(x, *params, dy)
    return res[0], res[1:]


def _conv_fn(x, w):
    acc = x * w[3:4, :]
    for j in range(3):
        acc = acc + shift_rows(x, 3 - j) * w[j:j + 1, :]
    return _silu(acc)


def _lerp_fn(x, mu):
    return x + (shift_rows(x, 1) - x) * mu[0:1, :]


def _seg_sum(x, width):
    if width == LANES:
        return jnp.sum(x, axis=1, keepdims=True)
    lo = lax.broadcasted_iota(jnp.int32, x.shape, 1) < width
    s0 = jnp.sum(jnp.where(lo, x, 0.0), axis=1, keepdims=True)
    s1 = jnp.sum(jnp.where(lo, 0.0, x), axis=1, keepdims=True)
    return jnp.where(lo, s0, s1)


def _per_block(fn, *xs):
    n = xs[0].shape[1] // LANES
    return jnp.concatenate([fn(*[x[:, LANES * b:LANES * (b + 1)] for x in xs]) for b in range(n)], axis=1)


def _head_expand(col0):
    r = lax.broadcasted_iota(jnp.int32, (LANES, DN_WIDTH), 0)
    c = lax.shift_right_logical(lax.broadcasted_iota(jnp.int32, (LANES, DN_WIDTH), 1), 7)
    return jnp.where(r == c + col0, 1.0, 0.0)


def _dn_pre_fn(cq, ck, gates, a_log, dt_bias):
    l2 = lambda x: x * lax.rsqrt(_seg_sum(x * x, LANES) + 1e-6)
    qh = _per_block(l2, cq) * (LANES ** -0.5)
    kh = _per_block(l2, ck)
    g = -jnp.exp(a_log) * _softplus(gates + dt_bias)
    gb = mm(g, _head_expand(0), "nn", True)
    bb = mm(_sigmoid(gates), _head_expand(DN_HEADS), "nn", True)
    return qh, kh, gb, bb, _cumsum_rows(gb)


def _dn_post_fn(o, z, nw):
    def one(ob, zb):
        return ob * lax.rsqrt(_seg_sum(ob * ob, LANES) * (1.0 / LANES) + RMS_EPS) * nw * _silu(zb)
    return (_per_block(one, o, z),)


def _rw_pre_fn(pr, pk, pv, pwa, pg, w0, a0, k_k, k_a, w2p, a2p, g2):
    log_w = -_softplus(-(w0 + mm(jnp.tanh(pwa), w2p))) - 0.5
    lw = -jnp.exp(log_w)
    a = _sigmoid(a0 + mm(pwa, a2p))
    gate = mm(_sigmoid(pg), g2)
    kk = pk * k_k
    kk = _per_block(lambda x: x / jnp.maximum(jnp.sqrt(_seg_sum(x * x, RW_HEAD)), 1e-12), kk)
    k = pk * (1.0 + (a - 1.0) * k_a)
    return pr, lw, k, pv, kk * a, -kk, gate, _cumsum_rows(lw)


def _rw_post_fn(y, r, k, v, gate, ln_w, ln_b, r_k):
    def one(yb, rb, kb, vb, gb, wb, bb, rkb):
        d = yb - _seg_sum(yb, RW_HEAD) * (1.0 / RW_HEAD)
        var = _seg_sum(d * d, RW_HEAD) * (1.0 / RW_HEAD)
        yn = d * lax.rsqrt(var + RW_GN_EPS) * wb + bb
        return (yn + _seg_sum(rb * kb * rkb, RW_HEAD) * vb) * gb
    return (_per_block(one, y, r, k, v, gate, ln_w, ln_b, r_k),)


def _rms_fn(h, w):
    return (h * lax.rsqrt(jnp.mean(h * h, axis=1, keepdims=True) + RMS_EPS) * w,)


def _xattn_fn(q, k, v):
    outs = []
    for h in range(XA_HEADS):
        sl = slice(LANES * h, LANES * (h + 1))
        s = mm(q[:, sl], k[:, sl], "nt") * (LANES ** -0.5)
        e = jnp.exp(s - jnp.max(s, axis=1, keepdims=True))
        outs.append(mm(e / jnp.sum(e, axis=1, keepdims=True), v[:, sl]))
    return (jnp.concatenate(outs, axis=1),)


def _fit(tile, dim):
    best = [t for t in range(LANES, min(tile, dim) + 1, LANES) if dim % t == 0]
    assert best, (tile, dim)
    return best[-1]


def _matmul(name, a, b, mode, out_dtypes, epilogue=None, extras=(), tm=1024, tn=1024, tk=2048, after=None):
    if mode == "tn":
        (k_dim, m), n = a.shape, b.shape[1]
    else:
        (m, k_dim), n = a.shape, (b.shape[1] if mode == "nn" else b.shape[0])
    tm, tn, tk = _fit(tm, m), _fit(tn, n), _fit(tk, k_dim)
    nk = k_dim // tk
    a_spec = (pl.BlockSpec((tk, tm), lambda i, j, k: (k, i)) if mode == "tn"
              else pl.BlockSpec((tm, tk), lambda i, j, k: (i, k)))
    b_spec = (pl.BlockSpec((tn, tk), lambda i, j, k: (j, k)) if mode == "nt"
              else pl.BlockSpec((tk, tn), lambda i, j, k: (k, j)))
    o_spec = pl.BlockSpec((tm, tn), lambda i, j, k: (i, j))
    n_ex, n_out = len(extras), len(out_dtypes)
    ties = [] if after is None else [after]

    def finish(total, rest):
        ex = [r[...].astype(F32) for r in rest[:n_ex]]
        res = epilogue(total, *ex) if epilogue else (total,)
        for ref, o in zip(rest[n_ex + len(ties):n_ex + len(ties) + n_out], res):
            ref[...] = o.astype(ref.dtype)

    def body_single(a_ref, b_ref, *rest):
        finish(_raw_dot(a_ref[...], b_ref[...], mode, False), rest)

    def body_acc(a_ref, b_ref, *rest):
        acc = rest[-1]
        k = pl.program_id(2)

        @pl.when(k == 0)
        def _():
            acc[...] = jnp.zeros_like(acc)

        acc[...] += _raw_dot(a_ref[...], b_ref[...], mode, False)

        @pl.when(k == nk - 1)
        def _():
            finish(acc[...], rest)

    res = pl.pallas_call(
        body_single if nk == 1 else body_acc, grid=(m // tm, n // tn, nk), name=name,
        in_specs=[a_spec, b_spec] + [o_spec] * n_ex + [pl.BlockSpec((8, LANES), lambda i, j, k: (0, 0))] * len(ties),
        out_specs=[o_spec] * n_out,
        out_shape=[SDS((m, n), dt) for dt in out_dtypes],
        scratch_shapes=[] if nk == 1 else [pltpu.VMEM((tm, tn), F32)],
        compiler_params=pltpu.CompilerParams(dimension_semantics=("parallel", "parallel", "arbitrary"),
                                             vmem_limit_bytes=VMEM_LIMIT),
    )(a, b, *extras, *ties)
    return res


def _loss_call(h, target, w, tr=256):
    rows, d = h.shape

    def fn(hv, wv, tv):
        y = _rms_fn(hv, wv)[0]
        return 0.5 * jnp.sum(jnp.mean(jnp.square(y - tv), axis=1, keepdims=True), axis=0, keepdims=True)

    def body(h_ref, t_ref, w_ref, loss_ref, dh_ref, dw_ref):
        tv = t_ref[...]
        val, vjp = jax.vjp(lambda hv, wv: fn(hv, wv, tv), h_ref[...], w_ref[...])
        dh, dw = vjp(jnp.ones((1, 1), F32))
        dh_ref[...] = dh
        first = pl.program_id(0) == 0

        @pl.when(first)
        def _():
            loss_ref[...] = jnp.broadcast_to(val, loss_ref.shape)
            dw_ref[...] = dw

        @pl.when(jnp.logical_not(first))
        def _():
            loss_ref[...] += jnp.broadcast_to(val, loss_ref.shape)
            dw_ref[...] += dw

    return pl.pallas_call(
        body, grid=(rows // tr,), name="loss_head",
        in_specs=[_col_spec(tr, d, 0), _col_spec(tr, d, 0), _whole(w)],
        out_specs=[pl.BlockSpec((8, LANES), lambda i: (0, 0)), _col_spec(tr, d, 0), _whole(w)],
        out_shape=[SDS((8, LANES), F32), SDS((rows, d), F32), SDS(w.shape, F32)],
        compiler_params=pltpu.CompilerParams(dimension_semantics=("arbitrary",), vmem_limit_bytes=VMEM_LIMIT),
    )(h, target, w)


def _adamw_vals(w, g, m, v):
    m = ADAM_B1 * m + (1.0 - ADAM_B1) * g
    v = ADAM_B2 * v + (1.0 - ADAM_B2) * jnp.square(g)
    m_hat = m / (1.0 - ADAM_B1 ** ADAM_STEP)
    v_hat = v / (1.0 - ADAM_B2 ** ADAM_STEP)
    delta = -ADAM_LR * (m_hat / (jnp.sqrt(v_hat) + ADAM_EPS) + ADAM_WD * w)
    return delta, m, v


def _sum_adamw(name, parts, w, m, v):
    r, c = w.shape
    tr = r
    for cand in (512, 256, 128, 64, 32, 16, 8):
        if r % cand == 0 and N_DEV * cand * c * 4 <= 6 * 1024 * 1024:
            tr = cand
            break

    def body(p_ref, w_ref, m_ref, v_ref, g_ref, d_ref, m2_ref, v2_ref):
        g = p_ref[0].astype(F32)
        for s in range(1, N_DEV):
            g = g + p_ref[s].astype(F32)
        g_ref[...] = g
        d_ref[...], m2_ref[...], v2_ref[...] = _adamw_vals(w_ref[...], g, m_ref[...], v_ref[...])

    blk = pl.BlockSpec((tr, c), lambda i: (i, 0))
    return pl.pallas_call(
        body, grid=(r // tr,), name=name,
        in_specs=[pl.BlockSpec((N_DEV, tr, c), lambda i: (0, i, 0)), blk, blk, blk],
        out_specs=[blk] * 4, out_shape=[SDS((r, c), F32)] * 4,
        compiler_params=pltpu.CompilerParams(dimension_semantics=("arbitrary",), vmem_limit_bytes=VMEM_LIMIT),
    )(parts, w, m, v)


def _peers():
    x, y, c = lax.axis_index("x"), lax.axis_index("y"), lax.axis_index("c")
    peers = []
    for k in range(1, N_DEV):
        px = 1 - x if k & 4 else x
        py = 1 - y if k & 2 else y
        pc = 1 - c if k & 1 else c
        peers.append(((px, py, pc), 4 * px + 2 * py + pc))
    return 4 * x + 2 * y + c, peers


def _slot(ref, idx, cols):
    if cols is None:
        return ref.at[idx]
    return ref.at[:, pl.ds(pl.multiple_of(idx * cols, LANES), cols)]


def _exchange(name, srcs, dsts, gather):
    n = len(srcs)

    def body(*refs):
        start, wait = _exchange_ops([c for _, c in srcs], [c for _, _, c in dsts], gather,
                                    refs[:n], refs[n:2 * n], *refs[2 * n:])
        start()
        wait()

    any_spec = pl.BlockSpec(memory_space=pl.ANY)
    return pl.pallas_call(
        body, name=name,
        in_specs=[any_spec] * n, out_specs=[any_spec] * n,
        out_shape=[SDS(shape, dt) for shape, dt, _ in dsts],
        scratch_shapes=_exchange_sems(n),
    )(*[a for a, _ in srcs])


def _gather_two_level(name, srcs, dsts):
    n = len(srcs)
    dst_cols = [c for _, _, c in dsts]

    def body(*refs):
        src_refs, out_refs = refs[:n], refs[n:2 * n]
        send_sems, recv_sems, local_sems = refs[2 * n:]
        x, y, c = lax.axis_index("x"), lax.axis_index("y"), lax.axis_index("c")
        index = lambda px, py, pc: 4 * px + 2 * py + pc
        me, sibling = index(x, y, c), (x, y, 1 - c)
        chips = [(x, 1 - y), (1 - x, y), (1 - x, 1 - y)]

        def copy(a, k, src, block, to):
            return pltpu.make_async_remote_copy(
                src_ref=src, dst_ref=_slot(out_refs[a], block, dst_cols[a]),
                send_sem=send_sems.at[a, k], recv_sem=recv_sems.at[a, k],
                device_id=to, device_id_type=pl.DeviceIdType.MESH)

        local, first, passed = [], [], []
        for a in range(n):
            cp = pltpu.make_async_copy(src_refs[a], _slot(out_refs[a], me, dst_cols[a]), local_sems.at[a])
            cp.start()
            local.append(cp)
            first.append(copy(a, 0, src_refs[a], me, sibling))
            first += [copy(a, 1 + j, src_refs[a], me, (*chip, c)) for j, chip in enumerate(chips)]
        for cp in first:
            cp.start()
        for a in range(n):
            for j, chip in enumerate(chips):
                block = index(*chip, c)
                arrived = _slot(out_refs[a], block, dst_cols[a])
                copy(a, 1 + j, arrived, block, (*chip, c)).wait_recv()
                passed.append(copy(a, 4 + j, arrived, block, sibling))
                passed[-1].start()
        for a in range(n):
            copy(a, 0, src_refs[a], index(x, y, 1 - c), sibling).wait_recv()
            for j, chip in enumerate(chips):
                block = index(*chip, 1 - c)
                copy(a, 4 + j, src_refs[a], block, sibling).wait_recv()
        for cp in first + passed:
            cp.wait_send()
        for cp in local:
            cp.wait()

    any_spec = pl.BlockSpec(memory_space=pl.ANY)
    return pl.pallas_call(
        body, name=name,
        in_specs=[any_spec] * n, out_specs=[any_spec] * n,
        out_shape=[SDS(shape, dt) for shape, dt, _ in dsts],
        scratch_shapes=_exchange_sems(n),
    )(*[a for a, _ in srcs])


_HBM = pl.BlockSpec(memory_space=pltpu.HBM)
_SEM = pl.BlockSpec(memory_space=pltpu.SEMAPHORE)
_EFFECT = pltpu.SideEffectType.DATAFLOW_SIDE_EFFECTING


def _split_copies(src_cols, dst_cols, gather, src_refs, land_refs, send_sems, recv_sems, landings):
    me, peers = _peers()
    out = []
    for a, (s_cols, d_cols) in enumerate(zip(src_cols, dst_cols)):
        for k, (pos, idx) in enumerate(peers):
            blk = src_refs[a] if gather else _slot(src_refs[a], idx, s_cols)
            out.append(pltpu.make_async_remote_copy(
                src_ref=blk, dst_ref=_slot(land_refs[a], idx if landings else me, d_cols),
                send_sem=send_sems.at[a * (N_DEV - 1) + k], recv_sem=recv_sems.at[a * (N_DEV - 1) + k],
                device_id=pos, device_id_type=pl.DeviceIdType.MESH))
    return out


def _exchange_start(name, srcs, dsts, gather):
    n = len(srcs)
    src_cols, dst_cols = [c for _, c in srcs], [c for _, _, c in dsts]

    def body(*refs):
        src_refs, land_refs = refs[:n], refs[n:2 * n]
        send_sems, recv_sems = refs[2 * n:2 * n + 2]
        token = refs[-1]
        for cp in _split_copies(src_cols, dst_cols, gather, src_refs, land_refs, send_sems, recv_sems, False):
            cp.start()
        token[...] = jnp.zeros_like(token)

    hbm = lambda a: pltpu.with_memory_space_constraint(a, pltpu.HBM)
    lands = [hbm(lax.empty(shape, dt)) for shape, dt, _ in dsts]
    res = pl.pallas_call(
        body, name=name,
        out_shape=(pltpu.SemaphoreType.DMA((n * (N_DEV - 1),)), pltpu.SemaphoreType.DMA((n * (N_DEV - 1),)),
                   *[pltpu.HBM(a.shape, a.dtype) for a, _ in srcs], *[pltpu.HBM(a.shape, a.dtype) for a in lands],
                   SDS((8, LANES), F32)),
        in_specs=[_HBM] * (2 * n),
        out_specs=(_SEM, _SEM, *[_HBM] * (2 * n), pl.BlockSpec(memory_space=pltpu.VMEM)),
        input_output_aliases={i: 2 + i for i in range(2 * n)},
        compiler_params=pltpu.CompilerParams(has_side_effects=_EFFECT),
    )(*[hbm(a) for a, _ in srcs], *lands)
    handle = (res[0], res[1], res[2:2 + n], res[2 + n:2 + 2 * n], src_cols, dst_cols, gather)
    return handle, res[-1]


def _exchange_wait(name, handle, after):
    send_sems, recv_sems, src_thru, land_thru, src_cols, dst_cols, gather = handle
    n = len(src_thru)

    def body(*refs):
        src_refs, land_refs = refs[:n], refs[n:2 * n]
        s_sems, r_sems = refs[2 * n:2 * n + 2]
        for cp in _split_copies(src_cols, dst_cols, gather, src_refs, land_refs, s_sems, r_sems, True):
            cp.wait_send()
            cp.wait_recv()

    res = pl.pallas_call(
        body, name=name,
        out_shape=tuple(pltpu.HBM(a.shape, a.dtype) for a in (*src_thru, *land_thru)),
        in_specs=[_HBM] * (2 * n) + [_SEM, _SEM, pl.BlockSpec(memory_space=pl.ANY)],
        out_specs=tuple([_HBM] * (2 * n)),
        input_output_aliases={i: i for i in range(2 * n)},
        compiler_params=pltpu.CompilerParams(has_side_effects=_EFFECT),
    )(*src_thru, *land_thru, send_sems, recv_sems, after)
    return res[:n], res[n:]


def _exchange_sems(n):
    return [pltpu.SemaphoreType.DMA((n, N_DEV - 1)), pltpu.SemaphoreType.DMA((n, N_DEV - 1)),
            pltpu.SemaphoreType.DMA((n,))]


def _exchange_ops(src_cols, dst_cols, gather, src_refs, out_refs, send_sems, recv_sems, local_sems):
    def copies(with_landings):
        me, peers = _peers()
        local, sends, landings = [], [], []
        for a, (s_cols, d_cols) in enumerate(zip(src_cols, dst_cols)):
            mine = src_refs[a] if gather else _slot(src_refs[a], me, s_cols)
            local.append(pltpu.make_async_copy(mine, _slot(out_refs[a], me, d_cols), local_sems.at[a]))
            for k, (pos, idx) in enumerate(peers):
                out_blk = src_refs[a] if gather else _slot(src_refs[a], idx, s_cols)
                both = dict(src_ref=out_blk, send_sem=send_sems.at[a, k], recv_sem=recv_sems.at[a, k],
                            device_id=pos, device_id_type=pl.DeviceIdType.MESH)
                sends.append(pltpu.make_async_remote_copy(dst_ref=_slot(out_refs[a], me, d_cols), **both))
                if with_landings:
                    landings.append(pltpu.make_async_remote_copy(dst_ref=_slot(out_refs[a], idx, d_cols), **both))
        return local, sends, landings

    def start():
        local, sends, _ = copies(False)
        for cp in local + sends:
            cp.start()

    def wait():
        local, sends, landings = copies(True)
        for cp in landings:
            cp.wait_recv()
        for cp in sends:
            cp.wait_send()
        for cp in local:
            cp.wait()

    return start, wait


def _rms_res_fn(h, w):
    return _rms_fn(h, w)[0], h


def _add_epilogue(acc, res):
    return (acc + res,)


def _gather_plan(shards):
    srcs, dsts = [], []
    for n, sh in shards.items():
        r, c = sh.shape
        srcs.append((sh, None))
        if SHARDED[n] and c % LANES == 0:
            dsts.append(((r, N_DEV * c), sh.dtype, c))
        else:
            dsts.append(((N_DEV, r, c), sh.dtype, None))
    return srcs, dsts, True


def _gather_finish(names, outs):
    full = {}
    for n, arr in zip(names, outs):
        if arr.ndim == 2:
            full[n] = arr
        elif SHARDED[n]:
            full[n] = arr.transpose(1, 0, 2).reshape(arr.shape[1], -1)
        else:
            full[n] = arr.reshape(-1, arr.shape[2])
    return full


def _my_index():
    return 4 * lax.axis_index("x") + 2 * lax.axis_index("y") + lax.axis_index("c")


def _gather_landed(names, waited):
    me = _my_index()
    outs = []
    for sh, buf in zip(*waited):
        if buf.ndim == 3:
            outs.append(lax.dynamic_update_slice(buf, sh[None], (me, 0, 0)))
        else:
            outs.append(lax.dynamic_update_slice(buf, sh, (0, me * sh.shape[1])))
    return _gather_finish(names, outs)


def _scatter_plan(grads):
    srcs, dsts = [], []
    for n, gr in grads.items():
        rows, cols = gr.shape
        if not SHARDED[n]:
            r, c = rows // N_DEV, cols
            srcs.append((gr.reshape(N_DEV, r, c), None))
        else:
            r, c = rows, cols // N_DEV
            if c % LANES == 0:
                srcs.append((gr, c))
            else:
                srcs.append((gr.reshape(r, N_DEV, c).transpose(1, 0, 2), None))
        dsts.append(((N_DEV, r, c), gr.dtype, None))
    return srcs, dsts, False


def _scatter_landed(handle, waited):
    me = _my_index()
    outs = []
    for src, cols, buf in zip(waited[0], handle[4], waited[1]):
        if cols is None:
            own = lax.dynamic_index_in_dim(src, me, 0, keepdims=True)
        else:
            own = lax.dynamic_slice(src, (0, me * cols), (src.shape[0], cols))[None]
        outs.append(lax.dynamic_update_slice(buf, own, (me, 0, 0)))
    return outs


def _local_step(x, mem, target, wt, late):
    d = D_MODEL
    g = {}
    wt = dict(wt)
    grp_a = ("w_out", "xa_wq", "xa_wk", "xa_wv", "xa_wo")
    grp_b = ("ffn_w1", "ffn_w2")
    handle_a, tok_a = _exchange_start("late_gather_a_start", *_gather_plan({n: late[n] for n in grp_a}))
    handle_w1, tok_b = _exchange_start("late_gather_w1_start", *_gather_plan({"ffn_w1": late["ffn_w1"]}))
    handle_w2, tok_c = _exchange_start("late_gather_w2_start", *_gather_plan({"ffn_w2": late["ffn_w2"]}))
    mix_w = wt["mix_norm_w"] + (tok_a[0:1, 0:1] + tok_b[0:1, 0:1] + tok_c[0:1, 0:1])
    u = _row_fwd(_rms_fn, "mix_norm", [(x, d, 0)], [mix_w], [(d, BF16)], 256)[0]
    p = _matmul("in_proj", u, wt["w_in"], "nn", [F32], tn=1536)[0]
    c = _col_fwd(_conv_fn, "dn_conv", p, 0, 24, [wt["dn_conv_w"]])
    dn_pre_tiles = [(c, DN_WIDTH, 0), (c, DN_WIDTH, 1), (p, LANES, 32)]
    dn_pre_params = [wt["dn_a_log"], wt["dn_dt_bias"]]
    qh, kh, gb, bb, gcb = _row_fwd(_dn_pre_fn, "dn_pre", dn_pre_tiles, dn_pre_params, [(DN_WIDTH, F32)] * 5, CHUNK)
    dn_arrs = [(qh, 0), (kh, 0), (c, 16), (gb, 0), (bb, 0), (gcb, 0)]
    o, kept_dn = _scan_fwd(_gdn_group, "gdn_scan", dn_arrs, DN_HEADS, 1)
    dn_post_tiles = [(o, DN_WIDTH, 0), (p, DN_WIDTH, 3)]
    o_dn = _row_fwd(_dn_post_fn, "dn_post", dn_post_tiles, [wt["dn_norm_w"]], [(DN_WIDTH, BF16)], 256)[0]

    ps = _col_fwd(_lerp_fn, "rw_shift", p, RW_OFF // LANES, 26, [wt["rw_mu"]])
    rw_pre_tiles = [(ps, RW_WIDTH, 0), (ps, RW_WIDTH, 1), (ps, RW_WIDTH, 2), (ps, LANES, 24), (ps, LANES, 25)]
    rw_pre_params = [wt[n] for n in ("rw_w0", "rw_a0", "rw_k_k", "rw_k_a", "rw_w2", "rw_a2", "rw_g2")]
    r, lw, k, v, al, be, gate, gcw = _row_fwd(_rw_pre_fn, "rw_pre", rw_pre_tiles, rw_pre_params,
                                              [(RW_WIDTH, F32)] * 8, CHUNK)
    rw_arrs = [(r, 0), (lw, 0), (k, 0), (v, 0), (al, 0), (be, 0), (gcw, 0)]
    y, kept_rw = _scan_fwd(_rw_group, "rw_scan", rw_arrs, RW_WIDTH // LANES, 2)
    rw_post_tiles = [(t, RW_WIDTH, 0) for t in (y, r, k, v, gate)]
    rw_post_params = [wt["rw_ln_w"], wt["rw_ln_b"], wt["rw_r_k"]]
    o_rw = _row_fwd(_rw_post_fn, "rw_post", rw_post_tiles, rw_post_params, [(RW_WIDTH, BF16)], 128)[0]
    o_cat = jnp.concatenate([o_dn, o_rw], axis=1)
    wt.update(_gather_landed(grp_a, _exchange_wait("late_gather_a_wait", handle_a, o_cat)))
    h1 = _matmul("out_proj", o_cat, wt["w_out"], "nn", [F32], _add_epilogue, (x,))[0]

    hn = _row_fwd(_rms_fn, "xa_norm", [(h1, d, 0)], [wt["xa_norm_w"]], [(d, BF16)], 256)[0]
    mn = _row_fwd(_rms_fn, "mem_norm", [(mem, d, 0)], [wt["mem_norm_w"]], [(d, BF16)], 256)[0]
    q = _matmul("xa_q", hn, wt["xa_wq"], "nn", [F32])[0]
    kx = _matmul("xa_k", mn, wt["xa_wk"], "nn", [F32])[0]
    vx = _matmul("xa_v", mn, wt["xa_wv"], "nn", [F32])[0]
    ao = _row_fwd(_xattn_fn, "xattn", [(q, XA_WIDTH, 0)], [kx, vx], [(XA_WIDTH, BF16)], 256)[0]
    h2 = _matmul("xa_o", ao, wt["xa_wo"], "nn", [F32], _add_epilogue, (h1,))[0]

    f = _row_fwd(_rms_fn, "ffn_norm", [(h2, d, 0)], [wt["ffn_norm_w"]], [(d, BF16)], 256)[0]
    wt.update(_gather_landed(("ffn_w1",), _exchange_wait("late_gather_w1_wait", handle_w1, f)))
    a, hid = _matmul("ffn_up", f, wt["ffn_w1"], "nn", [F32, BF16],
                     lambda acc: (acc, jnp.square(jnp.maximum(acc, 0.0))))
    wt.update(_gather_landed(("ffn_w2",), _exchange_wait("late_gather_w2_wait", handle_w2, hid)))
    h3 = _matmul("ffn_down", hid, wt["ffn_w2"], "nn", [F32], _add_epilogue, (h2,))[0]
    loss8, dh3, g["final_norm_w"] = _loss_call(h3, target, wt["final_norm_w"])

    da = _matmul("ffn_down_dx", dh3, wt["ffn_w2"], "nt", [BF16],
                 lambda acc, av: (acc * 2.0 * jnp.maximum(av, 0.0),), (a,))[0]
    g["ffn_w2"] = _matmul("ffn_down_dw", hid, dh3, "tn", [BF16])[0]
    g["ffn_w1"] = _matmul("ffn_up_dw", f, da, "tn", [BF16])[0]
    df = _matmul("ffn_up_dx", da, wt["ffn_w1"], "nt", [F32])[0]
    pending = {}
    plan = _scatter_plan({n: g.pop(n) for n in grp_b})
    pending[grp_b], tok = _exchange_start("late_grad_b_start", *plan)
    (dh2,), (g["ffn_norm_w"],) = _row_bwd(_rms_res_fn, "ffn_norm_bwd", [(h2, d, 0)],
                                          [wt["ffn_norm_w"] + tok[0:1, 0:1]],
                                          [[(df, d, 0)], [(dh3, d, 0)]], 256)

    dao = _matmul("xa_o_dx", dh2, wt["xa_wo"], "nt", [F32])[0]
    g["xa_wo"] = _matmul("xa_o_dw", ao, dh2, "tn", [BF16])[0]
    (dq,), (dkx, dvx) = _row_bwd(_xattn_fn, "xattn_bwd", [(q, XA_WIDTH, 0)], [kx, vx], [[(dao, XA_WIDTH, 0)]], 256)
    dhn = _matmul("xa_q_dx", dq, wt["xa_wq"], "nt", [F32])[0]
    g["xa_wq"] = _matmul("xa_q_dw", hn, dq, "tn", [BF16])[0]
    g["xa_wk"] = _matmul("xa_k_dw", mn, dkx, "tn", [BF16])[0]
    g["xa_wv"] = _matmul("xa_v_dw", mn, dvx, "tn", [BF16])[0]
    dmn = _matmul("xa_k_dx", dkx, wt["xa_wk"], "nt", [F32])[0]
    dmn = _matmul("xa_v_dx", dvx, wt["xa_wv"], "nt", [F32], _add_epilogue, (dmn,))[0]
    _, (g["mem_norm_w"],) = _row_bwd(_rms_fn, "mem_norm_bwd", [(mem, d, 0)], [wt["mem_norm_w"]],
                                     [[(dmn, d, 0)]], 256, want_tiles=())
    (dh1,), (g["xa_norm_w"],) = _row_bwd(_rms_res_fn, "xa_norm_bwd", [(h1, d, 0)], [wt["xa_norm_w"]],
                                         [[(dhn, d, 0)], [(dh2, d, 0)]], 256)

    do_cat = _matmul("out_proj_dx", dh1, wt["w_out"], "nt", [F32])[0]
    g["w_out"] = _matmul("out_proj_dw", o_cat, dh1, "tn", [BF16])[0]

    plan = _scatter_plan({n: g.pop(n) for n in grp_a})
    pending[grp_a], tok = _exchange_start("late_grad_a_start", *plan)
    (dy, dr1, dk1, dv1, dgate), (g["rw_ln_w"], g["rw_ln_b"], g["rw_r_k"]) = _row_bwd(
        _rw_post_fn, "rw_post_bwd", rw_post_tiles, [rw_post_params[0] + tok[0:1, 0:1]] + rw_post_params[1:],
        [[(do_cat, RW_WIDTH, 1)]], 128)
    dr2, dlw, dk2, dv2, dal, dbe, dgcw = _scan_bwd(_rw_group, "rw_scan_bwd", rw_arrs, kept_rw, dy,
                                                   RW_WIDTH // LANES)
    one = lambda t: [(t, RW_WIDTH, 0)]
    two = lambda s, t: [(s, RW_WIDTH, 0), (t, RW_WIDTH, 0)]
    d_ps, rw_pre_grads = _row_bwd(
        _rw_pre_fn, "rw_pre_bwd", rw_pre_tiles, rw_pre_params,
        [two(dr1, dr2), one(dlw), two(dk1, dk2), two(dv1, dv2), one(dal), one(dbe), one(dgate), one(dgcw)],
        CHUNK)
    for n, val in zip(("rw_w0", "rw_a0", "rw_k_k", "rw_k_a", "rw_w2", "rw_a2", "rw_g2"), rw_pre_grads):
        g[n] = val
    dp_rw, (g["rw_mu"],) = _col_bwd(_lerp_fn, "rw_shift_bwd", p, RW_OFF // LANES, 26, [wt["rw_mu"]],
                                    jnp.concatenate(d_ps, axis=1))

    (do, dz), (g["dn_norm_w"],) = _row_bwd(_dn_post_fn, "dn_post_bwd", dn_post_tiles, [wt["dn_norm_w"]],
                                           [[(do_cat, DN_WIDTH, 0)]], 256)
    dqh, dkh, dv_dn, dgb, dbb, dgcb = _scan_bwd(_gdn_group, "gdn_scan_bwd", dn_arrs, kept_dn, do, DN_HEADS)
    one = lambda t: [(t, DN_WIDTH, 0)]
    (dcq, dck, dgates), (g["dn_a_log"], g["dn_dt_bias"]) = _row_bwd(
        _dn_pre_fn, "dn_pre_bwd", dn_pre_tiles, dn_pre_params,
        [one(dqh), one(dkh), one(dgb), one(dbb), one(dgcb)], CHUNK)
    dp_qkv, (g["dn_conv_w"],) = _col_bwd(_conv_fn, "dn_conv_bwd", p, 0, 24, [wt["dn_conv_w"]],
                                         jnp.concatenate([dcq, dck, dv_dn], axis=1))
    dp = jnp.concatenate([dp_qkv, dz, dgates, dp_rw, jnp.zeros((x.shape[0], LANES), F32)], axis=1).astype(BF16)
    g["w_in"] = _matmul("in_proj_dw", u, dp, "tn", [BF16], tn=1536)[0]
    early = _logical_grads(g)
    pending[EARLY], tok = _exchange_start("early_grad_start", *_scatter_plan({n: early.pop(n) for n in EARLY}))
    du = _matmul("in_proj_dx", dp, wt["w_in"], "nt", [F32], after=tok)[0]
    (dx,), (early["mix_norm_w"],) = _row_bwd(_rms_res_fn, "mix_norm_bwd", [(x, d, 0)], [wt["mix_norm_w"]],
                                             [[(du, d, 0)], [(dh1, d, 0)]], 256)
    return loss8, dx, early, pending, tok


WEIGHTS = ["mix_norm_w", "w_in", "dn_conv_w", "dn_a_log", "dn_dt_bias", "dn_norm_w", "rw_mu", "rw_w0", "rw_w2",
           "rw_a0", "rw_a2", "rw_g2", "rw_k_k", "rw_k_a", "rw_r_k", "rw_ln_w", "rw_ln_b", "w_out", "xa_norm_w",
           "mem_norm_w", "xa_wq", "xa_wk", "xa_wv", "xa_wo", "ffn_norm_w", "ffn_w1", "ffn_w2", "final_norm_w"]
SHARDED = {"w_in": True, "w_out": False, "xa_wq": False, "xa_wk": False, "xa_wv": False, "xa_wo": True,
           "ffn_w1": True, "ffn_w2": False, "dn_conv_w": True, "rw_w2": True, "rw_a2": True, "rw_g2": True}
BF16_PAYLOAD = ("w_in", "w_out", "xa_wq", "xa_wk", "xa_wv", "xa_wo", "ffn_w1", "ffn_w2")
REPLICATED = [n for n in WEIGHTS if n not in SHARDED]
EARLY = ("w_in", "dn_conv_w", "rw_w2", "rw_a2", "rw_g2")
RW_IN_COLS = IN_COLS - DN_COLS


def _layout_weights(fw):
    wt = dict(fw)
    w_in = fw["w_in"]
    rows = w_in.shape[0]
    wt["w_in"] = jnp.concatenate(
        [w_in[:, :DN_COLS], jnp.zeros((rows, RW_OFF - DN_COLS), w_in.dtype), w_in[:, DN_COLS:],
         jnp.zeros((rows, IN_PAD - RW_OFF - RW_IN_COLS), w_in.dtype)], axis=1)
    wt["dn_conv_w"] = jnp.pad(fw["dn_conv_w"], ((0, 4), (0, 0)))
    wt["dn_a_log"] = jnp.pad(fw["dn_a_log"], ((0, 0), (0, LANES - DN_HEADS)))
    wt["dn_dt_bias"] = jnp.pad(fw["dn_dt_bias"], ((0, 0), (0, LANES - DN_HEADS)))
    wt["rw_w2"] = jnp.pad(fw["rw_w2"], ((0, 64), (0, 0)))
    wt["rw_a2"] = jnp.pad(fw["rw_a2"], ((64, 0), (0, 0)))
    return wt


def _logical_grads(g):
    out = dict(g)
    out["w_in"] = jnp.concatenate([g["w_in"][:, :DN_COLS], g["w_in"][:, RW_OFF:RW_OFF + RW_IN_COLS]], axis=1)
    out["dn_conv_w"] = g["dn_conv_w"][:4]
    out["dn_a_log"] = g["dn_a_log"][:, :DN_HEADS]
    out["dn_dt_bias"] = g["dn_dt_bias"][:, :DN_HEADS]
    out["rw_w2"] = g["rw_w2"][:64]
    out["rw_a2"] = g["rw_a2"][64:]
    return out


def _pack(vals):
    parts = []
    for v in vals:
        flat = v.reshape(-1)
        parts.append(jnp.pad(flat, (0, -flat.shape[0] % LANES)))
    flat = jnp.concatenate(parts)
    flat = jnp.pad(flat, (0, -flat.shape[0] % (8 * LANES)))
    return flat.reshape(-1, LANES)


def _unpack(packed, shapes):
    flat = packed.reshape(-1)
    out, at = [], 0
    for shp in shapes:
        size = math.prod(shp)
        out.append(flat[at:at + size].reshape(shp))
        at += size + (-size % LANES)
    return out


def kernel(x, mem, mix_norm_w, w_in, dn_conv_w, dn_a_log, dn_dt_bias, dn_norm_w, rw_mu, rw_w0, rw_w2, rw_a0, rw_a2, rw_g2, rw_k_k, rw_k_a, rw_r_k, rw_ln_w, rw_ln_b, w_out, xa_norm_w, mem_norm_w, xa_wq, xa_wk, xa_wv, xa_wo, ffn_norm_w, ffn_w1, ffn_w2, final_norm_w, loss_target, m_mix_norm_w, m_w_in, m_dn_conv_w, m_dn_a_log, m_dn_dt_bias, m_dn_norm_w, m_rw_mu, m_rw_w0, m_rw_w2, m_rw_a0, m_rw_a2, m_rw_g2, m_rw_k_k, m_rw_k_a, m_rw_r_k, m_rw_ln_w, m_rw_ln_b, m_w_out, m_xa_norm_w, m_mem_norm_w, m_xa_wq, m_xa_wk, m_xa_wv, m_xa_wo, m_ffn_norm_w, m_ffn_w1, m_ffn_w2, m_final_norm_w, v_mix_norm_w, v_w_in, v_dn_conv_w, v_dn_a_log, v_dn_dt_bias, v_dn_norm_w, v_rw_mu, v_rw_w0, v_rw_w2, v_rw_a0, v_rw_a2, v_rw_g2, v_rw_k_k, v_rw_k_a, v_rw_r_k, v_rw_ln_w, v_rw_ln_b, v_w_out, v_xa_norm_w, v_mem_norm_w, v_xa_wq, v_xa_wk, v_xa_wv, v_xa_wo, v_ffn_norm_w, v_ffn_w1, v_ffn_w2, v_final_norm_w):
    given = dict(locals())
    w = {n: given[n] for n in WEIGHTS}
    m = {n: given["m_" + n] for n in WEIGHTS}
    v = {n: given["v_" + n] for n in WEIGHTS}

    shards = {n: (w[n][0].astype(BF16) if n in BF16_PAYLOAD else w[n][0]) for n in SHARDED}
    srcs, dsts, _ = _gather_plan({n: shards[n] for n in EARLY})
    full = _gather_finish(EARLY, _gather_two_level("early_all_gather", srcs, dsts))
    for n in REPLICATED:
        full[n] = w[n].reshape(1, -1)

    loss8, dx, g, pending, after = _local_step(x[0], mem[0], loss_target[0], _layout_weights(full),
                                               {n: shards[n] for n in SHARDED if n not in EARLY})
    loss = lax.psum(loss8[0, 0], ("x", "y", "c"))

    packed = _pack([g[n] for n in REPLICATED])
    small, _ = _exchange_start("small_gather_start", [(packed, None)], [((N_DEV,) + packed.shape, F32, None)], True)
    grad, delta, new_m, new_v = {}, {}, {}, {}
    for names in sorted(pending, key=lambda names: names == EARLY):
        handle = pending[names]
        waited = _exchange_wait("grad_wait_" + names[0], handle, after)
        for n, parts in zip(names, _scatter_landed(handle, waited)):
            res = _sum_adamw("adamw_" + n, parts, w[n][0], m[n][0], v[n][0])
            grad[n], delta[n], new_m[n], new_v[n] = [t[None] for t in res]
            after = res[1]

    (packed,), (parts,) = _exchange_wait("small_gather_wait", small, after)
    parts = lax.dynamic_update_slice(parts, packed[None], (_my_index(), 0, 0))
    res = _sum_adamw("adamw_small", parts, _pack([w[n] for n in REPLICATED]),
                     _pack([m[n] for n in REPLICATED]), _pack([v[n] for n in REPLICATED]))
    shapes = [w[n].shape for n in REPLICATED]
    for store, packed_out in zip((grad, delta, new_m, new_v), res):
        for n, val in zip(REPLICATED, _unpack(packed_out, shapes)):
            store[n] = val

    return (loss, dx[None], *[grad[n] for n in WEIGHTS], *[delta[n] for n in WEIGHTS],
            *[new_m[n] for n in WEIGHTS], *[new_v[n] for n in WEIGHTS])
```

```python
import functools
import math

import jax
import jax.numpy as jnp
from jax import lax
from jax.experimental import pallas as pl
from jax.experimental.pallas import tpu as pltpu

F32 = jnp.float32
BF16 = jnp.bfloat16
SDS = jax.ShapeDtypeStruct

N_DEV = 8
D_MODEL = 2048
LANES = 128
CHUNK = 128
DN_HEADS = 8
DN_WIDTH = 1024
RW_WIDTH = 1024
RW_HEAD = 64
XA_HEADS = 4
XA_WIDTH = 512
FFN_HIDDEN = 8192
IN_COLS = 7440
DN_COLS = 4112
IN_PAD = 7680
RW_OFF = 4224
RMS_EPS = 1e-6
RW_GN_EPS = 64e-5
VMEM_LIMIT = 56 * 1024 * 1024

ADAM_LR = 0.001
ADAM_B1 = 0.9
ADAM_B2 = 0.999
ADAM_EPS = 1e-08
ADAM_WD = 0.01
ADAM_STEP = 10

_DIMS = {"nn": (((1,), (0,)), ((), ())), "nt": (((1,), (1,)), ((), ())), "tn": (((0,), (0,)), ((), ()))}


def _raw_dot(a, b, mode, hi):
    if hi:
        return lax.dot_general(a, b, _DIMS[mode], precision=lax.Precision.HIGHEST, preferred_element_type=F32)
    return lax.dot_general(a.astype(BF16), b.astype(BF16), _DIMS[mode], preferred_element_type=F32)


@functools.partial(jax.custom_vjp, nondiff_argnums=(2, 3))
def mm(a, b, mode="nn", hi=False):
    return _raw_dot(a, b, mode, hi)


def _mm_fwd(a, b, mode, hi):
    return _raw_dot(a, b, mode, hi), (a, b)


def _mm_bwd(mode, hi, res, g):
    a, b = res
    if mode == "nn":
        return _raw_dot(g, b, "nt", hi), _raw_dot(a, g, "tn", hi)
    if mode == "nt":
        return _raw_dot(g, b, "nn", hi), _raw_dot(g, a, "tn", hi)
    return _raw_dot(b, g, "nt", hi), _raw_dot(a, g, "nn", hi)


mm.defvjp(_mm_fwd, _mm_bwd)


def _shift_rows_raw(x, k):
    n = x.shape[0]
    rolled = pltpu.roll(x, k % n, axis=0)
    row = lax.broadcasted_iota(jnp.int32, x.shape, 0)
    keep = row >= k if k > 0 else row < n + k
    return jnp.where(keep, rolled, 0.0)


@functools.partial(jax.custom_vjp, nondiff_argnums=(1,))
def shift_rows(x, k):
    return _shift_rows_raw(x, k)


shift_rows.defvjp(lambda x, k: (_shift_rows_raw(x, k), None), lambda k, _, g: (_shift_rows_raw(g, -k),))


def _softplus(x):
    return jnp.maximum(x, 0.0) + jnp.log(1.0 + jnp.exp(-jnp.abs(x)))


def _sigmoid(x):
    return 1.0 / (1.0 + jnp.exp(-x))


def _silu(x):
    return x * _sigmoid(x)


def _tri_masks(n):
    ii = lax.broadcasted_iota(jnp.int32, (n, n), 0)
    jj = lax.broadcasted_iota(jnp.int32, (n, n), 1)
    return ii >= jj, ii > jj, ii == jj


def _neumann_inv_raw(m):
    n = m.shape[0]
    _, _, eye = _tri_masks(n)
    eye = jnp.where(eye, 1.0, 0.0)
    p = eye + m
    mk = m
    for _ in range(int(math.log2(n)) - 1):
        mk = _raw_dot(mk, mk, "nn", False)
        p = p + _raw_dot(p, mk, "nn", False)
    resid = eye - p + _raw_dot(m, p, "nn", True)
    return p + _raw_dot(p, resid, "nn", False)


@jax.custom_vjp
def _neumann_inv(m):
    return _neumann_inv_raw(m)


def _neumann_inv_fwd(m):
    p = _neumann_inv_raw(m)
    return p, p


def _neumann_inv_bwd(p, g):
    return (_raw_dot(_raw_dot(p, g, "tn", False), p, "nt", False),)


_neumann_inv.defvjp(_neumann_inv_fwd, _neumann_inv_bwd)


@jax.custom_vjp
def _saved_inv(m, p):
    return p


_saved_inv.defvjp(lambda m, p: (p, p), lambda p, g: (_neumann_inv_bwd(p, g)[0], jnp.zeros_like(p)))


def _inverse(m, saved):
    return _neumann_inv(m) if saved is None else _saved_inv(m, saved)


def _cumsum_rows(x):
    causal, _, _ = _tri_masks(x.shape[0])
    return mm(jnp.where(causal, 1.0, 0.0), x, "nn", True)


def _gdn_group(s0, q, k, v, gb, bb, gc, *saved):
    diff = jnp.stack([gc[j] - gc[j].T for j in range(gc.shape[0])])
    return jax.vmap(_gdn_chunk)(s0, q, k, v, gb, bb, gc, diff, *saved)


def _rw_group(*args):
    return jax.vmap(_rw_chunk)(*args)


def _gdn_chunk(s0, q, k, v, gb, bb, gc, diff, saved=None):
    c = q.shape[0]
    causal, strict, _ = _tri_masks(c)
    decay = jnp.exp(jnp.where(causal, diff, -jnp.inf))
    kb = k * bb
    a = jnp.where(strict, mm(kb, k, "nt") * decay, 0.0)
    p = _inverse(-a, saved)
    u = mm(p, v * bb)
    w = mm(p, kb * jnp.exp(gc))
    attn = mm(q, k, "nt") * decay
    v_new = u - mm(w, s0)
    o = mm(q * jnp.exp(gc), s0) + mm(attn, v_new)
    g_last = jnp.sum(gb, axis=0, keepdims=True)
    s1 = s0 * jnp.exp(g_last) + mm(k * jnp.exp(g_last - gc), v_new, "tn")
    return o, s1, p


def _rw_chunk(s0, r, lw, k, v, al, be, gc, saved0=None, saved1=None):
    c = r.shape[0]
    causal, strict, _ = _tri_masks(c)
    gp = gc - lw
    row = lax.broadcasted_iota(jnp.int32, lw.shape, 0)
    lane = lax.broadcasted_iota(jnp.int32, lw.shape, 1)
    g_mid = jnp.sum(jnp.where(row < c // 2, lw, 0.0), axis=0, keepdims=True)
    g_last = jnp.sum(lw, axis=0, keepdims=True)
    e_n = jnp.exp(g_mid - gc)
    rg = r * jnp.exp(gc - g_mid)
    bg = be * jnp.exp(gp - g_mid)
    an = al * e_n
    kn = k * e_n
    bt = mm(be * jnp.exp(gp), s0, "nt")
    rt = mm(r * jnp.exp(gc), s0, "nt")
    us, ys, ps = [], [], []
    for h, saved in enumerate((saved0, saved1)):
        mine = (lane >= RW_HEAD) if h else (lane < RW_HEAD)
        bgh = jnp.where(mine, bg, 0.0)
        rgh = jnp.where(mine, rg, 0.0)
        a_ab = jnp.where(strict, mm(bgh, an, "nt"), 0.0)
        a_kb = jnp.where(strict, mm(bgh, kn, "nt"), 0.0)
        a_ra = jnp.where(causal, mm(rgh, an, "nt"), 0.0)
        a_rk = jnp.where(causal, mm(rgh, kn, "nt"), 0.0)
        p = _inverse(a_ab, saved)
        ps.append(p)
        u_h = mm(p, bt + mm(a_kb, v))
        us.append(u_h)
        ys.append(rt + mm(a_ra, u_h) + mm(a_rk, v))
    lo = lane < RW_HEAD
    u = jnp.where(lo, us[0], us[1])
    y = jnp.where(lo, ys[0], ys[1])
    tail = jnp.exp(g_last - gc)
    s1 = s0 * jnp.exp(g_last) + mm(u, al * tail, "tn") + mm(v, k * tail, "tn")
    vi = lax.broadcasted_iota(jnp.int32, s0.shape, 0)
    ki = lax.broadcasted_iota(jnp.int32, s0.shape, 1)
    s1 = jnp.where((vi < RW_HEAD) == (ki < RW_HEAD), s1, 0.0)
    return y, s1, ps[0], ps[1]


SCAN_HB = 8


def _scan_specs(arrs, n_chunks, reverse):
    def spec(off):
        assert off % SCAN_HB == 0
        if reverse:
            return pl.BlockSpec((CHUNK, SCAN_HB * LANES), lambda h, n: (n_chunks - 1 - n, off // SCAN_HB + h))
        return pl.BlockSpec((CHUNK, SCAN_HB * LANES), lambda h, n: (n, off // SCAN_HB + h))
    return [spec(off) for _, off in arrs]


def _split_heads(x):
    return jnp.stack([x[:, LANES * j:LANES * (j + 1)] for j in range(SCAN_HB)], axis=0)


def _merge_heads(x):
    return jnp.concatenate([x[j] for j in range(SCAN_HB)], axis=1)


def _scan_fwd(group_fn, name, arrs, heads, n_kept):
    s = arrs[0][0].shape[0]
    n_chunks = s // CHUNK
    n_in = len(arrs)

    def body(*refs):
        y_ref, st_ref = refs[n_in:n_in + 2]
        kept_refs, s_scr = refs[n_in + 2:-1], refs[-1]

        @pl.when(pl.program_id(1) == 0)
        def _():
            s_scr[...] = jnp.zeros_like(s_scr)

        s0 = s_scr[...]
        st_ref[...] = s0
        y, s1, *kept = group_fn(s0, *[_split_heads(r[...]) for r in refs[:n_in]])
        y_ref[...] = _merge_heads(y)
        s_scr[...] = s1
        for ref, val in zip(kept_refs, kept):
            ref[...] = val

    per_chunk = pl.BlockSpec((SCAN_HB, None, LANES, LANES), lambda h, n: (h, n, 0, 0))
    res = pl.pallas_call(
        body, grid=(heads // SCAN_HB, n_chunks), name=name,
        in_specs=_scan_specs(arrs, n_chunks, False),
        out_specs=[pl.BlockSpec((CHUNK, SCAN_HB * LANES), lambda h, n: (n, h))] + [per_chunk] * (1 + n_kept),
        out_shape=[SDS((s, heads * LANES), F32)] + [SDS((heads, n_chunks, LANES, LANES), F32)] * (1 + n_kept),
        scratch_shapes=[pltpu.VMEM((SCAN_HB, LANES, LANES), F32)],
        compiler_params=pltpu.CompilerParams(dimension_semantics=("arbitrary", "arbitrary")),
    )(*[a for a, _ in arrs])
    return res[0], res[1:]


def _scan_bwd(group_fn, name, arrs, kept, dy, heads):
    s = arrs[0][0].shape[0]
    n_chunks = s // CHUNK
    n_in, n_kept = len(arrs), len(kept)

    def body(*refs):
        kept_vals = [r[...] for r in refs[n_in:n_in + n_kept]]
        dy_ref = refs[n_in + n_kept]
        d_refs = refs[n_in + n_kept + 1:2 * n_in + n_kept + 1]
        ds_scr = refs[-1]

        @pl.when(pl.program_id(1) == 0)
        def _():
            ds_scr[...] = jnp.zeros_like(ds_scr)

        def fn(s0, *ins):
            return group_fn(s0, *ins, *kept_vals[1:])[:2]

        _, vjp = jax.vjp(fn, kept_vals[0], *[_split_heads(r[...]) for r in refs[:n_in]])
        grads = vjp((_split_heads(dy_ref[...]), ds_scr[...]))
        ds_scr[...] = grads[0]
        for ref, g in zip(d_refs, grads[1:]):
            ref[...] = _merge_heads(g)

    rev = pl.BlockSpec((CHUNK, SCAN_HB * LANES), lambda h, n: (n_chunks - 1 - n, h))
    per_chunk = pl.BlockSpec((SCAN_HB, None, LANES, LANES), lambda h, n: (h, n_chunks - 1 - n, 0, 0))
    return pl.pallas_call(
        body, grid=(heads // SCAN_HB, n_chunks), name=name,
        in_specs=_scan_specs(arrs, n_chunks, True) + [per_chunk] * n_kept + [rev],
        out_specs=[rev] * n_in,
        out_shape=[SDS((s, heads * LANES), F32)] * n_in,
        scratch_shapes=[pltpu.VMEM((SCAN_HB, LANES, LANES), F32)],
        compiler_params=pltpu.CompilerParams(dimension_semantics=("arbitrary", "arbitrary")),
    )(*[a for a, _ in arrs], *kept, dy)


def _col_spec(tr, width, cb):
    return pl.BlockSpec((tr, width), lambda i: (i, cb))


def _whole(p):
    return pl.BlockSpec(p.shape, lambda i: (0,) * p.ndim)


def _row_fwd(fn, name, tiles, params, outs, tr):
    rows = tiles[0][0].shape[0]
    nt, npar = len(tiles), len(params)

    def body(*refs):
        vals = [r[...].astype(F32) for r in refs[:nt + npar]]
        for ref, o in zip(refs[nt + npar:], fn(*vals)):
            ref[...] = o.astype(ref.dtype)

    return pl.pallas_call(
        body, grid=(rows // tr,), name=name,
        in_specs=[_col_spec(tr, w, cb) for _, w, cb in tiles] + [_whole(p) for p in params],
        out_specs=[_col_spec(tr, w, 0) for w, _ in outs],
        out_shape=[SDS((rows, w), dt) for w, dt in outs],
        compiler_params=pltpu.CompilerParams(dimension_semantics=("arbitrary",), vmem_limit_bytes=VMEM_LIMIT),
    )(*[a for a, _, _ in tiles], *params)


def _row_bwd(fn, name, tiles, params, cts, tr, want_tiles=None):
    rows = tiles[0][0].shape[0]
    nt, npar = len(tiles), len(params)
    want = list(range(nt)) if want_tiles is None else list(want_tiles)
    flat_cts = [c for group in cts for c in group]
    n_ct = len(flat_cts)

    def body(*refs):
        vals = [r[...].astype(F32) for r in refs[:nt + npar]]
        ct_refs = refs[nt + npar:nt + npar + n_ct]
        out_refs = refs[nt + npar + n_ct:]
        ct_vals, at = [], 0
        for group in cts:
            total = ct_refs[at][...].astype(F32)
            for r in ct_refs[at + 1:at + len(group)]:
                total = total + r[...].astype(F32)
            ct_vals.append(total)
            at += len(group)
        _, vjp = jax.vjp(lambda *a: tuple(fn(*a)), *vals)
        grads = vjp(tuple(ct_vals))
        for ref, t in zip(out_refs[:len(want)], want):
            ref[...] = grads[t]
        first = pl.program_id(0) == 0
        for ref, g in zip(out_refs[len(want):], grads[nt:]):
            @pl.when(first)
            def _(ref=ref, g=g):
                ref[...] = g

            @pl.when(jnp.logical_not(first))
            def _(ref=ref, g=g):
                ref[...] += g

    res = pl.pallas_call(
        body, grid=(rows // tr,), name=name,
        in_specs=[_col_spec(tr, w, cb) for _, w, cb in tiles] + [_whole(p) for p in params]
        + [_col_spec(tr, w, cb) for _, w, cb in flat_cts],
        out_specs=[_col_spec(tr, tiles[t][1], 0) for t in want] + [_whole(p) for p in params],
        out_shape=[SDS((rows, tiles[t][1]), F32) for t in want] + [SDS(p.shape, F32) for p in params],
        compiler_params=pltpu.CompilerParams(dimension_semantics=("arbitrary",), vmem_limit_bytes=VMEM_LIMIT),
    )(*[a for a, _, _ in tiles], *params, *[a for a, _, _ in flat_cts])
    return res[:len(want)], res[len(want):]


def _col_fwd(fn, name, x, first_block, n_blocks, params):
    rows = x.shape[0]

    def body(*refs):
        refs[-1][...] = fn(*[r[...] for r in refs[:-1]])

    return pl.pallas_call(
        body, grid=(n_blocks,), name=name,
        in_specs=[pl.BlockSpec((rows, LANES), lambda j: (0, first_block + j))]
        + [pl.BlockSpec((p.shape[0], LANES), lambda j: (0, j)) for p in params],
        out_specs=pl.BlockSpec((rows, LANES), lambda j: (0, j)),
        out_shape=SDS((rows, n_blocks * LANES), F32),
        compiler_params=pltpu.CompilerParams(dimension_semantics=("arbitrary",), vmem_limit_bytes=VMEM_LIMIT),
    )(x, *params)


def _col_bwd(fn, name, x, first_block, n_blocks, params, dy):
    rows = x.shape[0]
    npar = len(params)

    def body(*refs):
        vals = [r[...] for r in refs[:1 + npar]]
        _, vjp = jax.vjp(fn, *vals)
        grads = vjp(refs[1 + npar][...])
        for ref, g in zip(refs[2 + npar:], grads):
            ref[...] = g

    pspecs = [pl.BlockSpec((p.shape[0], LANES), lambda j: (0, j)) for p in params]
    blk = pl.BlockSpec((rows, LANES), lambda j: (0, j))
    res = pl.pallas_call(
        body, grid=(n_blocks,), name=name,
        in_specs=[pl.BlockSpec((rows, LANES), lambda j: (0, first_block + j))] + pspecs + [blk],
        out_specs=[blk] + pspecs,
        out_shape=[SDS((rows, n_blocks * LANES), F32)] + [SDS(p.shape, F32) for p in params],
        compiler_params=pltpu.CompilerParams(dimension_semantics=("arbitrary",), vmem_limit_bytes=VMEM_LIMIT),
    )(x, *params, dy)
    return res[0], res[1:]


def _conv_fn(x, w):
    acc = x * w[3:4, :]
    for j in range(3):
        acc = acc + shift_rows(x, 3 - j) * w[j:j + 1, :]
    return _silu(acc)


def _lerp_fn(x, mu):
    return x + (shift_rows(x, 1) - x) * mu[0:1, :]


def _seg_sum(x, width):
    if width == LANES:
        return jnp.sum(x, axis=1, keepdims=True)
    lo = lax.broadcasted_iota(jnp.int32, x.shape, 1) < width
    s0 = jnp.sum(jnp.where(lo, x, 0.0), axis=1, keepdims=True)
    s1 = jnp.sum(jnp.where(lo, 0.0, x), axis=1, keepdims=True)
    return jnp.where(lo, s0, s1)


def _per_block(fn, *xs):
    n = xs[0].shape[1] // LANES
    return jnp.concatenate([fn(*[x[:, LANES * b:LANES * (b + 1)] for x in xs]) for b in range(n)], axis=1)


def _head_expand(col0):
    r = lax.broadcasted_iota(jnp.int32, (LANES, DN_WIDTH), 0)
    c = lax.shift_right_logical(lax.broadcasted_iota(jnp.int32, (LANES, DN_WIDTH), 1), 7)
    return jnp.where(r == c + col0, 1.0, 0.0)


def _dn_pre_fn(cq, ck, gates, a_log, dt_bias):
    l2 = lambda x: x * lax.rsqrt(_seg_sum(x * x, LANES) + 1e-6)
    qh = _per_block(l2, cq) * (LANES ** -0.5)
    kh = _per_block(l2, ck)
    g = -jnp.exp(a_log) * _softplus(gates + dt_bias)
    gb = mm(g, _head_expand(0), "nn", True)
    bb = mm(_sigmoid(gates), _head_expand(DN_HEADS), "nn", True)
    return qh, kh, gb, bb, _cumsum_rows(gb)


def _dn_post_fn(o, z, nw):
    def one(ob, zb):
        return ob * lax.rsqrt(_seg_sum(ob * ob, LANES) * (1.0 / LANES) + RMS_EPS) * nw * _silu(zb)
    return (_per_block(one, o, z),)


def _rw_pre_fn(pr, pk, pv, pwa, pg, w0, a0, k_k, k_a, w2p, a2p, g2):
    log_w = -_softplus(-(w0 + mm(jnp.tanh(pwa), w2p))) - 0.5
    lw = -jnp.exp(log_w)
    a = _sigmoid(a0 + mm(pwa, a2p))
    gate = mm(_sigmoid(pg), g2)
    kk = pk * k_k
    kk = _per_block(lambda x: x / jnp.maximum(jnp.sqrt(_seg_sum(x * x, RW_HEAD)), 1e-12), kk)
    k = pk * (1.0 + (a - 1.0) * k_a)
    return pr, lw, k, pv, kk * a, -kk, gate, _cumsum_rows(lw)


def _rw_post_fn(y, r, k, v, gate, ln_w, ln_b, r_k):
    def one(yb, rb, kb, vb, gb, wb, bb, rkb):
        d = yb - _seg_sum(yb, RW_HEAD) * (1.0 / RW_HEAD)
        var = _seg_sum(d * d, RW_HEAD) * (1.0 / RW_HEAD)
        yn = d * lax.rsqrt(var + RW_GN_EPS) * wb + bb
        return (yn + _seg_sum(rb * kb * rkb, RW_HEAD) * vb) * gb
    return (_per_block(one, y, r, k, v, gate, ln_w, ln_b, r_k),)


def _rms_fn(h, w):
    return (h * lax.rsqrt(jnp.mean(h * h, axis=1, keepdims=True) + RMS_EPS) * w,)


def _xattn_fn(q, k, v):
    outs = []
    for h in range(XA_HEADS):
        sl = slice(LANES * h, LANES * (h + 1))
        s = mm(q[:, sl], k[:, sl], "nt") * (LANES ** -0.5)
        e = jnp.exp(s - jnp.max(s, axis=1, keepdims=True))
        outs.append(mm(e / jnp.sum(e, axis=1, keepdims=True), v[:, sl]))
    return (jnp.concatenate(outs, axis=1),)


def _fit(tile, dim):
    best = [t for t in range(LANES, min(tile, dim) + 1, LANES) if dim % t == 0]
    assert best, (tile, dim)
    return best[-1]


def _matmul(name, a, b, mode, out_dtypes, epilogue=None, extras=(), tm=1024, tn=1024, tk=2048, after=None):
    if mode == "tn":
        (k_dim, m), n = a.shape, b.shape[1]
    else:
        (m, k_dim), n = a.shape, (b.shape[1] if mode == "nn" else b.shape[0])
    tm, tn, tk = _fit(tm, m), _fit(tn, n), _fit(tk, k_dim)
    nk = k_dim // tk
    a_spec = (pl.BlockSpec((tk, tm), lambda i, j, k: (k, i)) if mode == "tn"
              else pl.BlockSpec((tm, tk), lambda i, j, k: (i, k)))
    b_spec = (pl.BlockSpec((tn, tk), lambda i, j, k: (j, k)) if mode == "nt"
              else pl.BlockSpec((tk, tn), lambda i, j, k: (k, j)))
    o_spec = pl.BlockSpec((tm, tn), lambda i, j, k: (i, j))
    n_ex, n_out = len(extras), len(out_dtypes)
    ties = [] if after is None else [after]

    def finish(total, rest):
        ex = [r[...].astype(F32) for r in rest[:n_ex]]
        res = epilogue(total, *ex) if epilogue else (total,)
        for ref, o in zip(rest[n_ex + len(ties):n_ex + len(ties) + n_out], res):
            ref[...] = o.astype(ref.dtype)

    def body_single(a_ref, b_ref, *rest):
        finish(_raw_dot(a_ref[...], b_ref[...], mode, False), rest)

    def body_acc(a_ref, b_ref, *rest):
        acc = rest[-1]
        k = pl.program_id(2)

        @pl.when(k == 0)
        def _():
            acc[...] = jnp.zeros_like(acc)

        acc[...] += _raw_dot(a_ref[...], b_ref[...], mode, False)

        @pl.when(k == nk - 1)
        def _():
            finish(acc[...], rest)

    res = pl.pallas_call(
        body_single if nk == 1 else body_acc, grid=(m // tm, n // tn, nk), name=name,
        in_specs=[a_spec, b_spec] + [o_spec] * n_ex + [pl.BlockSpec((8, LANES), lambda i, j, k: (0, 0))] * len(ties),
        out_specs=[o_spec] * n_out,
        out_shape=[SDS((m, n), dt) for dt in out_dtypes],
        scratch_shapes=[] if nk == 1 else [pltpu.VMEM((tm, tn), F32)],
        compiler_params=pltpu.CompilerParams(dimension_semantics=("parallel", "parallel", "arbitrary"),
                                             vmem_limit_bytes=VMEM_LIMIT),
    )(a, b, *extras, *ties)
    return res


def _loss_call(h, target, w, tr=256):
    rows, d = h.shape

    def fn(hv, wv, tv):
        y = _rms_fn(hv, wv)[0]
        return 0.5 * jnp.sum(jnp.mean(jnp.square(y - tv), axis=1, keepdims=True), axis=0, keepdims=True)

    def body(h_ref, t_ref, w_ref, loss_ref, dh_ref, dw_ref):
        tv = t_ref[...]
        val, vjp = jax.vjp(lambda hv, wv: fn(hv, wv, tv), h_ref[...], w_ref[...])
        dh, dw = vjp(jnp.ones((1, 1), F32))
        dh_ref[...] = dh
        first = pl.program_id(0) == 0

        @pl.when(first)
        def _():
            loss_ref[...] = jnp.broadcast_to(val, loss_ref.shape)
            dw_ref[...] = dw

        @pl.when(jnp.logical_not(first))
        def _():
            loss_ref[...] += jnp.broadcast_to(val, loss_ref.shape)
            dw_ref[...] += dw

    return pl.pallas_call(
        body, grid=(rows // tr,), name="loss_head",
        in_specs=[_col_spec(tr, d, 0), _col_spec(tr, d, 0), _whole(w)],
        out_specs=[pl.BlockSpec((8, LANES), lambda i: (0, 0)), _col_spec(tr, d, 0), _whole(w)],
        out_shape=[SDS((8, LANES), F32), SDS((rows, d), F32), SDS(w.shape, F32)],
        compiler_params=pltpu.CompilerParams(dimension_semantics=("arbitrary",), vmem_limit_bytes=VMEM_LIMIT),
    )(h, target, w)


def _adamw_vals(w, g, m, v):
    m = ADAM_B1 * m + (1.0 - ADAM_B1) * g
    v = ADAM_B2 * v + (1.0 - ADAM_B2) * jnp.square(g)
    m_hat = m / (1.0 - ADAM_B1 ** ADAM_STEP)
    v_hat = v / (1.0 - ADAM_B2 ** ADAM_STEP)
    delta = -ADAM_LR * (m_hat / (jnp.sqrt(v_hat) + ADAM_EPS) + ADAM_WD * w)
    return delta, m, v


def _sum_adamw(name, parts, w, m, v):
    r, c = w.shape
    tr = r
    for cand in (512, 256, 128, 64, 32, 16, 8):
        if r % cand == 0 and N_DEV * cand * c * 4 <= 6 * 1024 * 1024:
            tr = cand
            break

    def body(p_ref, w_ref, m_ref, v_ref, g_ref, d_ref, m2_ref, v2_ref):
        g = p_ref[0].astype(F32)
        for s in range(1, N_DEV):
            g = g + p_ref[s].astype(F32)
        g_ref[...] = g
        d_ref[...], m2_ref[...], v2_ref[...] = _adamw_vals(w_ref[...], g, m_ref[...], v_ref[...])

    blk = pl.BlockSpec((tr, c), lambda i: (i, 0))
    return pl.pallas_call(
        body, grid=(r // tr,), name=name,
        in_specs=[pl.BlockSpec((N_DEV, tr, c), lambda i: (0, i, 0)), blk, blk, blk],
        out_specs=[blk] * 4, out_shape=[SDS((r, c), F32)] * 4,
        compiler_params=pltpu.CompilerParams(dimension_semantics=("arbitrary",), vmem_limit_bytes=VMEM_LIMIT),
    )(parts, w, m, v)


def _peers():
    x, y, c = lax.axis_index("x"), lax.axis_index("y"), lax.axis_index("c")
    peers = []
    for k in range(1, N_DEV):
        px = 1 - x if k & 4 else x
        py = 1 - y if k & 2 else y
        pc = 1 - c if k & 1 else c
        peers.append(((px, py, pc), 4 * px + 2 * py + pc))
    return 4 * x + 2 * y + c, peers


def _slot(ref, idx, cols):
    if cols is None:
        return ref.at[idx]
    return ref.at[:, pl.ds(pl.multiple_of(idx * cols, LANES), cols)]


def _exchange(name, srcs, dsts, gather):
    n = len(srcs)

    def body(*refs):
        start, wait = _exchange_ops([c for _, c in srcs], [c for _, _, c in dsts], gather,
                                    refs[:n], refs[n:2 * n], *refs[2 * n:])
        start()
        wait()

    any_spec = pl.BlockSpec(memory_space=pl.ANY)
    return pl.pallas_call(
        body, name=name,
        in_specs=[any_spec] * n, out_specs=[any_spec] * n,
        out_shape=[SDS(shape, dt) for shape, dt, _ in dsts],
        scratch_shapes=_exchange_sems(n),
    )(*[a for a, _ in srcs])


def _gather_two_level(name, srcs, dsts):
    n = len(srcs)
    dst_cols = [c for _, _, c in dsts]

    def body(*refs):
        src_refs, out_refs = refs[:n], refs[n:2 * n]
        send_sems, recv_sems, local_sems = refs[2 * n:]
        x, y, c = lax.axis_index("x"), lax.axis_index("y"), lax.axis_index("c")
        index = lambda px, py, pc: 4 * px + 2 * py + pc
        me, sibling = index(x, y, c), (x, y, 1 - c)
        chips = [(x, 1 - y), (1 - x, y), (1 - x, 1 - y)]

        def copy(a, k, src, block, to):
            return pltpu.make_async_remote_copy(
                src_ref=src, dst_ref=_slot(out_refs[a], block, dst_cols[a]),
                send_sem=send_sems.at[a, k], recv_sem=recv_sems.at[a, k],
                device_id=to, device_id_type=pl.DeviceIdType.MESH)

        local, first, passed = [], [], []
        for a in range(n):
            cp = pltpu.make_async_copy(src_refs[a], _slot(out_refs[a], me, dst_cols[a]), local_sems.at[a])
            cp.start()
            local.append(cp)
            first.append(copy(a, 0, src_refs[a], me, sibling))
            first += [copy(a, 1 + j, src_refs[a], me, (*chip, c)) for j, chip in enumerate(chips)]
        for cp in first:
            cp.start()
        for a in range(n):
            for j, chip in enumerate(chips):
                block = index(*chip, c)
                arrived = _slot(out_refs[a], block, dst_cols[a])
                copy(a, 1 + j, arrived, block, (*chip, c)).wait_recv()
                passed.append(copy(a, 4 + j, arrived, block, sibling))
                passed[-1].start()
        for a in range(n):
            copy(a, 0, src_refs[a], index(x, y, 1 - c), sibling).wait_recv()
            for j, chip in enumerate(chips):
                block = index(*chip, 1 - c)
                copy(a, 4 + j, src_refs[a], block, sibling).wait_recv()
        for cp in first + passed:
            cp.wait_send()
        for cp in local:
            cp.wait()

    any_spec = pl.BlockSpec(memory_space=pl.ANY)
    return pl.pallas_call(
        body, name=name,
        in_specs=[any_spec] * n, out_specs=[any_spec] * n,
        out_shape=[SDS(shape, dt) for shape, dt, _ in dsts],
        scratch_shapes=_exchange_sems(n),
    )(*[a for a, _ in srcs])


_HBM = pl.BlockSpec(memory_space=pltpu.HBM)
_SEM = pl.BlockSpec(memory_space=pltpu.SEMAPHORE)
_EFFECT = pltpu.SideEffectType.DATAFLOW_SIDE_EFFECTING


def _split_copies(src_cols, dst_cols, gather, src_refs, land_refs, send_sems, recv_sems, landings):
    me, peers = _peers()
    n = len(src_cols)
    remote, local = [], []
    for a, (s_cols, d_cols) in enumerate(zip(src_cols, dst_cols)):
        mine = src_refs[a] if gather else _slot(src_refs[a], me, s_cols)
        local.append(pltpu.make_async_copy(mine, _slot(land_refs[a], me, d_cols),
                                           send_sems.at[n * (N_DEV - 1) + a]))
        for k, (pos, idx) in enumerate(peers):
            blk = src_refs[a] if gather else _slot(src_refs[a], idx, s_cols)
            remote.append(pltpu.make_async_remote_copy(
                src_ref=blk, dst_ref=_slot(land_refs[a], idx if landings else me, d_cols),
                send_sem=send_sems.at[a * (N_DEV - 1) + k], recv_sem=recv_sems.at[a * (N_DEV - 1) + k],
                device_id=pos, device_id_type=pl.DeviceIdType.MESH))
    return remote, local


def _exchange_start(name, srcs, dsts, gather):
    n = len(srcs)
    src_cols, dst_cols = [c for _, c in srcs], [c for _, _, c in dsts]

    def body(*refs):
        src_refs, land_refs = refs[:n], refs[n:2 * n]
        send_sems, recv_sems = refs[2 * n:2 * n + 2]
        token = refs[-1]
        remote, local = _split_copies(src_cols, dst_cols, gather, src_refs, land_refs, send_sems, recv_sems, False)
        for cp in remote + local:
            cp.start()
        token[...] = jnp.zeros_like(token)

    hbm = lambda a: pltpu.with_memory_space_constraint(a, pltpu.HBM)
    lands = [hbm(lax.empty(shape, dt)) for shape, dt, _ in dsts]
    res = pl.pallas_call(
        body, name=name,
        out_shape=(pltpu.SemaphoreType.DMA((n * N_DEV,)), pltpu.SemaphoreType.DMA((n * (N_DEV - 1),)),
                   *[pltpu.HBM(a.shape, a.dtype) for a, _ in srcs], *[pltpu.HBM(a.shape, a.dtype) for a in lands],
                   SDS((8, LANES), F32)),
        in_specs=[_HBM] * (2 * n),
        out_specs=(_SEM, _SEM, *[_HBM] * (2 * n), pl.BlockSpec(memory_space=pltpu.VMEM)),
        input_output_aliases={i: 2 + i for i in range(2 * n)},
        compiler_params=pltpu.CompilerParams(has_side_effects=_EFFECT),
    )(*[hbm(a) for a, _ in srcs], *lands)
    handle = (res[0], res[1], res[2:2 + n], res[2 + n:2 + 2 * n], src_cols, dst_cols, gather)
    return handle, res[-1]


def _exchange_wait(name, handle, after):
    send_sems, recv_sems, src_thru, land_thru, src_cols, dst_cols, gather = handle
    n = len(src_thru)

    def body(*refs):
        src_refs, land_refs = refs[:n], refs[n:2 * n]
        s_sems, r_sems = refs[2 * n:2 * n + 2]
        remote, local = _split_copies(src_cols, dst_cols, gather, src_refs, land_refs, s_sems, r_sems, True)
        for cp in remote:
            cp.wait_send()
            cp.wait_recv()
        for cp in local:
            cp.wait()

    res = pl.pallas_call(
        body, name=name,
        out_shape=tuple(pltpu.HBM(a.shape, a.dtype) for a in (*src_thru, *land_thru)),
        in_specs=[_HBM] * (2 * n) + [_SEM, _SEM, pl.BlockSpec(memory_space=pl.ANY)],
        out_specs=tuple([_HBM] * (2 * n)),
        input_output_aliases={i: i for i in range(2 * n)},
        compiler_params=pltpu.CompilerParams(has_side_effects=_EFFECT),
    )(*src_thru, *land_thru, send_sems, recv_sems, after)
    return res[n:]


def _exchange_sems(n):
    return [pltpu.SemaphoreType.DMA((n, N_DEV - 1)), pltpu.SemaphoreType.DMA((n, N_DEV - 1)),
            pltpu.SemaphoreType.DMA((n,))]


def _exchange_ops(src_cols, dst_cols, gather, src_refs, out_refs, send_sems, recv_sems, local_sems):
    def copies(with_landings):
        me, peers = _peers()
        local, sends, landings = [], [], []
        for a, (s_cols, d_cols) in enumerate(zip(src_cols, dst_cols)):
            mine = src_refs[a] if gather else _slot(src_refs[a], me, s_cols)
            local.append(pltpu.make_async_copy(mine, _slot(out_refs[a], me, d_cols), local_sems.at[a]))
            for k, (pos, idx) in enumerate(peers):
                out_blk = src_refs[a] if gather else _slot(src_refs[a], idx, s_cols)
                both = dict(src_ref=out_blk, send_sem=send_sems.at[a, k], recv_sem=recv_sems.at[a, k],
                            device_id=pos, device_id_type=pl.DeviceIdType.MESH)
                sends.append(pltpu.make_async_remote_copy(dst_ref=_slot(out_refs[a], me, d_cols), **both))
                if with_landings:
                    landings.append(pltpu.make_async_remote_copy(dst_ref=_slot(out_refs[a], idx, d_cols), **both))
        return local, sends, landings

    def start():
        local, sends, _ = copies(False)
        for cp in local + sends:
            cp.start()

    def wait():
        local, sends, landings = copies(True)
        for cp in landings:
            cp.wait_recv()
        for cp in sends:
            cp.wait_send()
        for cp in local:
            cp.wait()

    return start, wait


def _rms_res_fn(h, w):
    return _rms_fn(h, w)[0], h


def _add_epilogue(acc, res):
    return (acc + res,)


def _gather_plan(shards):
    srcs, dsts = [], []
    for n, sh in shards.items():
        r, c = sh.shape
        srcs.append((sh, None))
        if SHARDED[n] and c % LANES == 0:
            dsts.append(((r, N_DEV * c), sh.dtype, c))
        else:
            dsts.append(((N_DEV, r, c), sh.dtype, None))
    return srcs, dsts, True


def _w_in_segments():
    out = []
    for j in range(N_DEV):
        lo, hi = W_IN_SHARD * j, W_IN_SHARD * (j + 1)
        for a, b in ((lo, min(hi, DN_COLS)), (max(lo, DN_COLS), hi)):
            if a < b:
                out.append((j, a - lo, b - lo, a if a < DN_COLS else a + RW_OFF - DN_COLS))
    return out


def _w_in_to_padded(shards, tr=256):
    _, rows, _ = shards.shape

    def body(g_ref, o_ref):
        o_ref[...] = jnp.zeros_like(o_ref)
        for j, a, b, dst in _w_in_segments():
            o_ref[:, dst:dst + b - a] = g_ref[j, :, a:b]

    return pl.pallas_call(
        body, grid=(rows // tr,), name="w_in_to_padded",
        in_specs=[pl.BlockSpec((N_DEV, tr, W_IN_SHARD), lambda i: (0, i, 0))],
        out_specs=pl.BlockSpec((tr, IN_PAD), lambda i: (i, 0)),
        out_shape=SDS((rows, IN_PAD), shards.dtype),
        compiler_params=pltpu.CompilerParams(dimension_semantics=("arbitrary",), vmem_limit_bytes=VMEM_LIMIT),
    )(shards)


def _w_in_grad_to_shards(gw, tr=256):
    rows, _ = gw.shape

    def body(w_ref, o_ref):
        for j, a, b, dst in _w_in_segments():
            o_ref[j, :, a:b] = w_ref[:, dst:dst + b - a]

    return pl.pallas_call(
        body, grid=(rows // tr,), name="w_in_grad_to_shards",
        in_specs=[pl.BlockSpec((tr, IN_PAD), lambda i: (i, 0))],
        out_specs=pl.BlockSpec((N_DEV, tr, W_IN_SHARD), lambda i: (0, i, 0)),
        out_shape=SDS((N_DEV, rows, W_IN_SHARD), gw.dtype),
        compiler_params=pltpu.CompilerParams(dimension_semantics=("arbitrary",), vmem_limit_bytes=VMEM_LIMIT),
    )(gw)


def _gather_finish(names, outs):
    full = {}
    for n, arr in zip(names, outs):
        if n == "w_in":
            full[n] = _w_in_to_padded(arr)
        elif arr.ndim == 2:
            full[n] = arr
        elif SHARDED[n]:
            full[n] = arr.transpose(1, 0, 2).reshape(arr.shape[1], -1)
        else:
            full[n] = arr.reshape(-1, arr.shape[2])
    return full


def _scatter_plan(grads):
    srcs, dsts = [], []
    for n, gr in grads.items():
        if gr.ndim == 3:
            srcs.append((gr, None))
            dsts.append((gr.shape, gr.dtype, None))
            continue
        rows, cols = gr.shape
        if not SHARDED[n]:
            r, c = rows // N_DEV, cols
            srcs.append((gr.reshape(N_DEV, r, c), None))
        else:
            r, c = rows, cols // N_DEV
            if c % LANES == 0:
                srcs.append((gr, c))
            else:
                srcs.append((gr.reshape(r, N_DEV, c).transpose(1, 0, 2), None))
        dsts.append(((N_DEV, r, c), gr.dtype, None))
    return srcs, dsts, False


def _local_step(x, mem, target, wt, late):
    d = D_MODEL
    g = {}
    wt = dict(wt)
    grp_a = ("w_out", "xa_wq", "xa_wk", "xa_wv", "xa_wo")
    grp_b = ("ffn_w1", "ffn_w2")
    handle_a, tok_a = _exchange_start("late_gather_a_start", *_gather_plan({n: late[n] for n in grp_a}))
    handle_w1, tok_b = _exchange_start("late_gather_w1_start", *_gather_plan({"ffn_w1": late["ffn_w1"]}))
    handle_w2, tok_c = _exchange_start("late_gather_w2_start", *_gather_plan({"ffn_w2": late["ffn_w2"]}))
    mix_w = wt["mix_norm_w"] + (tok_a[0:1, 0:1] + tok_b[0:1, 0:1] + tok_c[0:1, 0:1])
    u = _row_fwd(_rms_fn, "mix_norm", [(x, d, 0)], [mix_w], [(d, BF16)], 256)[0]
    p = _matmul("in_proj", u, wt["w_in"], "nn", [F32], tn=1536)[0]
    c = _col_fwd(_conv_fn, "dn_conv", p, 0, 24, [wt["dn_conv_w"]])
    dn_pre_tiles = [(c, DN_WIDTH, 0), (c, DN_WIDTH, 1), (p, LANES, 32)]
    dn_pre_params = [wt["dn_a_log"], wt["dn_dt_bias"]]
    qh, kh, gb, bb, gcb = _row_fwd(_dn_pre_fn, "dn_pre", dn_pre_tiles, dn_pre_params, [(DN_WIDTH, F32)] * 5, CHUNK)
    dn_arrs = [(qh, 0), (kh, 0), (c, 16), (gb, 0), (bb, 0), (gcb, 0)]
    o, kept_dn = _scan_fwd(_gdn_group, "gdn_scan", dn_arrs, DN_HEADS, 1)
    dn_post_tiles = [(o, DN_WIDTH, 0), (p, DN_WIDTH, 3)]
    o_dn = _row_fwd(_dn_post_fn, "dn_post", dn_post_tiles, [wt["dn_norm_w"]], [(DN_WIDTH, BF16)], 256)[0]

    ps = _col_fwd(_lerp_fn, "rw_shift", p, RW_OFF // LANES, 26, [wt["rw_mu"]])
    rw_pre_tiles = [(ps, RW_WIDTH, 0), (ps, RW_WIDTH, 1), (ps, RW_WIDTH, 2), (ps, LANES, 24), (ps, LANES, 25)]
    rw_pre_params = [wt[n] for n in ("rw_w0", "rw_a0", "rw_k_k", "rw_k_a", "rw_w2", "rw_a2", "rw_g2")]
    r, lw, k, v, al, be, gate, gcw = _row_fwd(_rw_pre_fn, "rw_pre", rw_pre_tiles, rw_pre_params,
                                              [(RW_WIDTH, F32)] * 8, CHUNK)
    rw_arrs = [(r, 0), (lw, 0), (k, 0), (v, 0), (al, 0), (be, 0), (gcw, 0)]
    y, kept_rw = _scan_fwd(_rw_group, "rw_scan", rw_arrs, RW_WIDTH // LANES, 2)
    rw_post_tiles = [(t, RW_WIDTH, 0) for t in (y, r, k, v, gate)]
    rw_post_params = [wt["rw_ln_w"], wt["rw_ln_b"], wt["rw_r_k"]]
    o_rw = _row_fwd(_rw_post_fn, "rw_post", rw_post_tiles, rw_post_params, [(RW_WIDTH, BF16)], 128)[0]
    o_cat = jnp.concatenate([o_dn, o_rw], axis=1)
    wt.update(_gather_finish(grp_a, _exchange_wait("late_gather_a_wait", handle_a, o_cat)))
    h1 = _matmul("out_proj", o_cat, wt["w_out"], "nn", [F32], _add_epilogue, (x,))[0]

    hn = _row_fwd(_rms_fn, "xa_norm", [(h1, d, 0)], [wt["xa_norm_w"]], [(d, BF16)], 256)[0]
    mn = _row_fwd(_rms_fn, "mem_norm", [(mem, d, 0)], [wt["mem_norm_w"]], [(d, BF16)], 256)[0]
    q = _matmul("xa_q", hn, wt["xa_wq"], "nn", [F32])[0]
    kx = _matmul("xa_k", mn, wt["xa_wk"], "nn", [F32])[0]
    vx = _matmul("xa_v", mn, wt["xa_wv"], "nn", [F32])[0]
    ao = _row_fwd(_xattn_fn, "xattn", [(q, XA_WIDTH, 0)], [kx, vx], [(XA_WIDTH, BF16)], 256)[0]
    h2 = _matmul("xa_o", ao, wt["xa_wo"], "nn", [F32], _add_epilogue, (h1,))[0]

    f = _row_fwd(_rms_fn, "ffn_norm", [(h2, d, 0)], [wt["ffn_norm_w"]], [(d, BF16)], 256)[0]
    wt.update(_gather_finish(("ffn_w1",), _exchange_wait("late_gather_w1_wait", handle_w1, f)))
    a, hid = _matmul("ffn_up", f, wt["ffn_w1"], "nn", [F32, BF16],
                     lambda acc: (acc, jnp.square(jnp.maximum(acc, 0.0))))
    wt.update(_gather_finish(("ffn_w2",), _exchange_wait("late_gather_w2_wait", handle_w2, hid)))
    h3 = _matmul("ffn_down", hid, wt["ffn_w2"], "nn", [F32], _add_epilogue, (h2,))[0]
    loss8, dh3, g["final_norm_w"] = _loss_call(h3, target, wt["final_norm_w"])

    da = _matmul("ffn_down_dx", dh3, wt["ffn_w2"], "nt", [BF16],
                 lambda acc, av: (acc * 2.0 * jnp.maximum(av, 0.0),), (a,))[0]
    g["ffn_w2"] = _matmul("ffn_down_dw", hid, dh3, "tn", [BF16])[0]
    g["ffn_w1"] = _matmul("ffn_up_dw", f, da, "tn", [BF16])[0]
    df = _matmul("ffn_up_dx", da, wt["ffn_w1"], "nt", [F32])[0]
    pending = {}
    plan = _scatter_plan({n: g.pop(n) for n in grp_b})
    pending[grp_b], tok = _exchange_start("late_grad_b_start", *plan)
    (dh2,), (g["ffn_norm_w"],) = _row_bwd(_rms_res_fn, "ffn_norm_bwd", [(h2, d, 0)],
                                          [wt["ffn_norm_w"] + tok[0:1, 0:1]],
                                          [[(df, d, 0)], [(dh3, d, 0)]], 256)

    dao = _matmul("xa_o_dx", dh2, wt["xa_wo"], "nt", [F32])[0]
    g["xa_wo"] = _matmul("xa_o_dw", ao, dh2, "tn", [BF16])[0]
    (dq,), (dkx, dvx) = _row_bwd(_xattn_fn, "xattn_bwd", [(q, XA_WIDTH, 0)], [kx, vx], [[(dao, XA_WIDTH, 0)]], 256)
    dhn = _matmul("xa_q_dx", dq, wt["xa_wq"], "nt", [F32])[0]
    g["xa_wq"] = _matmul("xa_q_dw", hn, dq, "tn", [BF16])[0]
    g["xa_wk"] = _matmul("xa_k_dw", mn, dkx, "tn", [BF16])[0]
    g["xa_wv"] = _matmul("xa_v_dw", mn, dvx, "tn", [BF16])[0]
    dmn = _matmul("xa_k_dx", dkx, wt["xa_wk"], "nt", [F32])[0]
    dmn = _matmul("xa_v_dx", dvx, wt["xa_wv"], "nt", [F32], _add_epilogue, (dmn,))[0]
    _, (g["mem_norm_w"],) = _row_bwd(_rms_fn, "mem_norm_bwd", [(mem, d, 0)], [wt["mem_norm_w"]],
                                     [[(dmn, d, 0)]], 256, want_tiles=())
    (dh1,), (g["xa_norm_w"],) = _row_bwd(_rms_res_fn, "xa_norm_bwd", [(h1, d, 0)], [wt["xa_norm_w"]],
                                         [[(dhn, d, 0)], [(dh2, d, 0)]], 256)

    do_cat = _matmul("out_proj_dx", dh1, wt["w_out"], "nt", [F32])[0]
    g["w_out"] = _matmul("out_proj_dw", o_cat, dh1, "tn", [BF16])[0]

    plan = _scatter_plan({n: g.pop(n) for n in grp_a})
    pending[grp_a], tok = _exchange_start("late_grad_a_start", *plan)
    (dy, dr1, dk1, dv1, dgate), (g["rw_ln_w"], g["rw_ln_b"], g["rw_r_k"]) = _row_bwd(
        _rw_post_fn, "rw_post_bwd", rw_post_tiles, [rw_post_params[0] + tok[0:1, 0:1]] + rw_post_params[1:],
        [[(do_cat, RW_WIDTH, 1)]], 128)
    dr2, dlw, dk2, dv2, dal, dbe, dgcw = _scan_bwd(_rw_group, "rw_scan_bwd", rw_arrs, kept_rw, dy,
                                                   RW_WIDTH // LANES)
    one = lambda t: [(t, RW_WIDTH, 0)]
    two = lambda s, t: [(s, RW_WIDTH, 0), (t, RW_WIDTH, 0)]
    d_ps, rw_pre_grads = _row_bwd(
        _rw_pre_fn, "rw_pre_bwd", rw_pre_tiles, rw_pre_params,
        [two(dr1, dr2), one(dlw), two(dk1, dk2), two(dv1, dv2), one(dal), one(dbe), one(dgate), one(dgcw)],
        CHUNK)
    for n, val in zip(("rw_w0", "rw_a0", "rw_k_k", "rw_k_a", "rw_w2", "rw_a2", "rw_g2"), rw_pre_grads):
        g[n] = val
    dp_rw, (g["rw_mu"],) = _col_bwd(_lerp_fn, "rw_shift_bwd", p, RW_OFF // LANES, 26, [wt["rw_mu"]],
                                    jnp.concatenate(d_ps, axis=1))

    (do, dz), (g["dn_norm_w"],) = _row_bwd(_dn_post_fn, "dn_post_bwd", dn_post_tiles, [wt["dn_norm_w"]],
                                           [[(do_cat, DN_WIDTH, 0)]], 256)
    dqh, dkh, dv_dn, dgb, dbb, dgcb = _scan_bwd(_gdn_group, "gdn_scan_bwd", dn_arrs, kept_dn, do, DN_HEADS)
    one = lambda t: [(t, DN_WIDTH, 0)]
    (dcq, dck, dgates), (g["dn_a_log"], g["dn_dt_bias"]) = _row_bwd(
        _dn_pre_fn, "dn_pre_bwd", dn_pre_tiles, dn_pre_params,
        [one(dqh), one(dkh), one(dgb), one(dbb), one(dgcb)], CHUNK)
    dp_qkv, (g["dn_conv_w"],) = _col_bwd(_conv_fn, "dn_conv_bwd", p, 0, 24, [wt["dn_conv_w"]],
                                         jnp.concatenate([dcq, dck, dv_dn], axis=1))
    dp = jnp.concatenate([dp_qkv, dz, dgates, dp_rw, jnp.zeros((x.shape[0], LANES), F32)], axis=1).astype(BF16)
    g["w_in"] = _matmul("in_proj_dw", u, dp, "tn", [BF16], tn=1536)[0]
    early = _logical_grads(g)
    pending[EARLY], tok = _exchange_start("early_grad_start", *_scatter_plan({n: early.pop(n) for n in EARLY}))
    du = _matmul("in_proj_dx", dp, wt["w_in"], "nt", [F32], after=tok)[0]
    (dx,), (early["mix_norm_w"],) = _row_bwd(_rms_res_fn, "mix_norm_bwd", [(x, d, 0)], [wt["mix_norm_w"]],
                                             [[(du, d, 0)], [(dh1, d, 0)]], 256)
    return loss8, dx, early, pending, tok


WEIGHTS = ["mix_norm_w", "w_in", "dn_conv_w", "dn_a_log", "dn_dt_bias", "dn_norm_w", "rw_mu", "rw_w0", "rw_w2",
           "rw_a0", "rw_a2", "rw_g2", "rw_k_k", "rw_k_a", "rw_r_k", "rw_ln_w", "rw_ln_b", "w_out", "xa_norm_w",
           "mem_norm_w", "xa_wq", "xa_wk", "xa_wv", "xa_wo", "ffn_norm_w", "ffn_w1", "ffn_w2", "final_norm_w"]
SHARDED = {"w_in": True, "w_out": False, "xa_wq": False, "xa_wk": False, "xa_wv": False, "xa_wo": True,
           "ffn_w1": True, "ffn_w2": False, "dn_conv_w": True, "rw_w2": True, "rw_a2": True, "rw_g2": True}
BF16_PAYLOAD = ("w_in", "w_out", "xa_wq", "xa_wk", "xa_wv", "xa_wo", "ffn_w1", "ffn_w2")
REPLICATED = [n for n in WEIGHTS if n not in SHARDED]
EARLY = ("w_in", "dn_conv_w", "rw_w2", "rw_a2", "rw_g2")
RW_IN_COLS = IN_COLS - DN_COLS
W_IN_SHARD = IN_COLS // N_DEV


def _layout_weights(fw):
    wt = dict(fw)
    wt["dn_conv_w"] = jnp.pad(fw["dn_conv_w"], ((0, 4), (0, 0)))
    wt["dn_a_log"] = jnp.pad(fw["dn_a_log"], ((0, 0), (0, LANES - DN_HEADS)))
    wt["dn_dt_bias"] = jnp.pad(fw["dn_dt_bias"], ((0, 0), (0, LANES - DN_HEADS)))
    wt["rw_w2"] = jnp.pad(fw["rw_w2"], ((0, 64), (0, 0)))
    wt["rw_a2"] = jnp.pad(fw["rw_a2"], ((64, 0), (0, 0)))
    return wt


def _logical_grads(g):
    out = dict(g)
    out["w_in"] = _w_in_grad_to_shards(g["w_in"])
    out["dn_conv_w"] = g["dn_conv_w"][:4]
    out["dn_a_log"] = g["dn_a_log"][:, :DN_HEADS]
    out["dn_dt_bias"] = g["dn_dt_bias"][:, :DN_HEADS]
    out["rw_w2"] = g["rw_w2"][:64]
    out["rw_a2"] = g["rw_a2"][64:]
    return out


def _pack(vals):
    parts = []
    for v in vals:
        flat = v.reshape(-1)
        parts.append(jnp.pad(flat, (0, -flat.shape[0] % LANES)))
    flat = jnp.concatenate(parts)
    flat = jnp.pad(flat, (0, -flat.shape[0] % (8 * LANES)))
    return flat.reshape(-1, LANES)


def _unpack(packed, shapes):
    flat = packed.reshape(-1)
    out, at = [], 0
    for shp in shapes:
        size = math.prod(shp)
        out.append(flat[at:at + size].reshape(shp))
        at += size + (-size % LANES)
    return out


def kernel(x, mem, mix_norm_w, w_in, dn_conv_w, dn_a_log, dn_dt_bias, dn_norm_w, rw_mu, rw_w0, rw_w2, rw_a0, rw_a2, rw_g2, rw_k_k, rw_k_a, rw_r_k, rw_ln_w, rw_ln_b, w_out, xa_norm_w, mem_norm_w, xa_wq, xa_wk, xa_wv, xa_wo, ffn_norm_w, ffn_w1, ffn_w2, final_norm_w, loss_target, m_mix_norm_w, m_w_in, m_dn_conv_w, m_dn_a_log, m_dn_dt_bias, m_dn_norm_w, m_rw_mu, m_rw_w0, m_rw_w2, m_rw_a0, m_rw_a2, m_rw_g2, m_rw_k_k, m_rw_k_a, m_rw_r_k, m_rw_ln_w, m_rw_ln_b, m_w_out, m_xa_norm_w, m_mem_norm_w, m_xa_wq, m_xa_wk, m_xa_wv, m_xa_wo, m_ffn_norm_w, m_ffn_w1, m_ffn_w2, m_final_norm_w, v_mix_norm_w, v_w_in, v_dn_conv_w, v_dn_a_log, v_dn_dt_bias, v_dn_norm_w, v_rw_mu, v_rw_w0, v_rw_w2, v_rw_a0, v_rw_a2, v_rw_g2, v_rw_k_k, v_rw_k_a, v_rw_r_k, v_rw_ln_w, v_rw_ln_b, v_w_out, v_xa_norm_w, v_mem_norm_w, v_xa_wq, v_xa_wk, v_xa_wv, v_xa_wo, v_ffn_norm_w, v_ffn_w1, v_ffn_w2, v_final_norm_w):
    given = dict(locals())
    w = {n: given[n] for n in WEIGHTS}
    m = {n: given["m_" + n] for n in WEIGHTS}
    v = {n: given["v_" + n] for n in WEIGHTS}

    shards = {n: (w[n][0].astype(BF16) if n in BF16_PAYLOAD else w[n][0]) for n in SHARDED}
    srcs, dsts, _ = _gather_plan({n: shards[n] for n in EARLY})
    full = _gather_finish(EARLY, _gather_two_level("early_all_gather", srcs, dsts))
    for n in REPLICATED:
        full[n] = w[n].reshape(1, -1)

    loss8, dx, g, pending, after = _local_step(x[0], mem[0], loss_target[0], _layout_weights(full),
                                               {n: shards[n] for n in SHARDED if n not in EARLY})
    loss = lax.psum(loss8[0, 0], ("x", "y", "c"))

    packed = _pack([g[n] for n in REPLICATED])
    small, _ = _exchange_start("small_gather_start", [(packed, None)], [((N_DEV,) + packed.shape, F32, None)], True)
    grad, delta, new_m, new_v = {}, {}, {}, {}
    for names in sorted(pending, key=lambda names: names == EARLY):
        handle = pending[names]
        for n, parts in zip(names, _exchange_wait("grad_wait_" + names[0], handle, after)):
            res = _sum_adamw("adamw_" + n, parts, w[n][0], m[n][0], v[n][0])
            grad[n], delta[n], new_m[n], new_v[n] = [t[None] for t in res]
            after = res[1]

    (parts,) = _exchange_wait("small_gather_wait", small, after)
    res = _sum_adamw("adamw_small", parts, _pack([w[n] for n in REPLICATED]),
                     _pack([m[n] for n in REPLICATED]), _pack([v[n] for n in REPLICATED]))
    shapes = [w[n].shape for n in REPLICATED]
    for store, packed_out in zip((grad, delta, new_m, new_v), res):
        for n, val in zip(REPLICATED, _unpack(packed_out, shapes)):
            store[n] = val

    return (loss, dx[None], *[grad[n] for n in WEIGHTS], *[delta[n] for n in WEIGHTS],
            *[new_m[n] for n in WEIGHTS], *[new_v[n] for n in WEIGHTS])
```

```python
import functools
import math

import jax
import jax.numpy as jnp
from jax import lax
from jax.experimental import pallas as pl
from jax.experimental.pallas import tpu as pltpu

F32 = jnp.float32
BF16 = jnp.bfloat16
SDS = jax.ShapeDtypeStruct

N_DEV = 8
D_MODEL = 2048
LANES = 128
CHUNK = 128
DN_HEADS = 8
DN_WIDTH = 1024
RW_WIDTH = 1024
RW_HEAD = 64
XA_HEADS = 4
XA_WIDTH = 512
FFN_HIDDEN = 8192
IN_COLS = 7440
DN_COLS = 4112
IN_PAD = 7680
RW_OFF = 4224
RMS_EPS = 1e-6
RW_GN_EPS = 64e-5
VMEM_LIMIT = 56 * 1024 * 1024

ADAM_LR = 0.001
ADAM_B1 = 0.9
ADAM_B2 = 0.999
ADAM_EPS = 1e-08
ADAM_WD = 0.01
ADAM_STEP = 10

_DIMS = {"nn": (((1,), (0,)), ((), ())), "nt": (((1,), (1,)), ((), ())), "tn": (((0,), (0,)), ((), ()))}


def _raw_dot(a, b, mode, hi):
    if hi:
        return lax.dot_general(a, b, _DIMS[mode], precision=lax.Precision.HIGHEST, preferred_element_type=F32)
    return lax.dot_general(a.astype(BF16), b.astype(BF16), _DIMS[mode], preferred_element_type=F32)


@functools.partial(jax.custom_vjp, nondiff_argnums=(2, 3))
def mm(a, b, mode="nn", hi=False):
    return _raw_dot(a, b, mode, hi)


def _mm_fwd(a, b, mode, hi):
    return _raw_dot(a, b, mode, hi), (a, b)


def _mm_bwd(mode, hi, res, g):
    a, b = res
    if mode == "nn":
        return _raw_dot(g, b, "nt", hi), _raw_dot(a, g, "tn", hi)
    if mode == "nt":
        return _raw_dot(g, b, "nn", hi), _raw_dot(g, a, "tn", hi)
    return _raw_dot(b, g, "nt", hi), _raw_dot(a, g, "nn", hi)


mm.defvjp(_mm_fwd, _mm_bwd)


def _shift_rows_raw(x, k):
    n = x.shape[0]
    rolled = pltpu.roll(x, k % n, axis=0)
    row = lax.broadcasted_iota(jnp.int32, x.shape, 0)
    keep = row >= k if k > 0 else row < n + k
    return jnp.where(keep, rolled, 0.0)


@functools.partial(jax.custom_vjp, nondiff_argnums=(1,))
def shift_rows(x, k):
    return _shift_rows_raw(x, k)


shift_rows.defvjp(lambda x, k: (_shift_rows_raw(x, k), None), lambda k, _, g: (_shift_rows_raw(g, -k),))


def _softplus(x):
    return jnp.maximum(x, 0.0) + jnp.log(1.0 + jnp.exp(-jnp.abs(x)))


def _sigmoid(x):
    return 1.0 / (1.0 + jnp.exp(-x))


def _silu(x):
    return x * _sigmoid(x)


def _tri_masks(n):
    ii = lax.broadcasted_iota(jnp.int32, (n, n), 0)
    jj = lax.broadcasted_iota(jnp.int32, (n, n), 1)
    return ii >= jj, ii > jj, ii == jj


def _neumann_inv_raw(m):
    n = m.shape[0]
    _, _, eye = _tri_masks(n)
    eye = jnp.where(eye, 1.0, 0.0)
    p = eye + m
    mk = m
    for _ in range(int(math.log2(n)) - 1):
        mk = _raw_dot(mk, mk, "nn", False)
        p = p + _raw_dot(p, mk, "nn", False)
    resid = eye - p + _raw_dot(m, p, "nn", True)
    return p + _raw_dot(p, resid, "nn", False)


@jax.custom_vjp
def _neumann_inv(m):
    return _neumann_inv_raw(m)


def _neumann_inv_fwd(m):
    p = _neumann_inv_raw(m)
    return p, p


def _neumann_inv_bwd(p, g):
    return (_raw_dot(_raw_dot(p, g, "tn", False), p, "nt", False),)


_neumann_inv.defvjp(_neumann_inv_fwd, _neumann_inv_bwd)


@jax.custom_vjp
def _saved_inv(m, p):
    return p


_saved_inv.defvjp(lambda m, p: (p, p), lambda p, g: (_neumann_inv_bwd(p, g)[0], jnp.zeros_like(p)))


def _inverse(m, saved):
    return _neumann_inv(m) if saved is None else _saved_inv(m, saved)


def _cumsum_rows(x):
    causal, _, _ = _tri_masks(x.shape[0])
    return mm(jnp.where(causal, 1.0, 0.0), x, "nn", True)


def _gdn_group(s0, q, k, v, gb, bb, gc, *saved):
    diff = jnp.stack([gc[j] - gc[j].T for j in range(gc.shape[0])])
    return jax.vmap(_gdn_chunk)(s0, q, k, v, gb, bb, gc, diff, *saved)


def _rw_group(*args):
    return jax.vmap(_rw_chunk)(*args)


def _gdn_chunk(s0, q, k, v, gb, bb, gc, diff, saved=None):
    c = q.shape[0]
    causal, strict, _ = _tri_masks(c)
    decay = jnp.exp(jnp.where(causal, diff, -jnp.inf))
    kb = k * bb
    a = jnp.where(strict, mm(kb, k, "nt") * decay, 0.0)
    p = _inverse(-a, saved)
    u = mm(p, v * bb)
    w = mm(p, kb * jnp.exp(gc))
    attn = mm(q, k, "nt") * decay
    v_new = u - mm(w, s0)
    o = mm(q * jnp.exp(gc), s0) + mm(attn, v_new)
    g_last = jnp.sum(gb, axis=0, keepdims=True)
    s1 = s0 * jnp.exp(g_last) + mm(k * jnp.exp(g_last - gc), v_new, "tn")
    return o, s1, p


def _rw_chunk(s0, r, lw, k, v, al, be, gc, saved0=None, saved1=None):
    c = r.shape[0]
    causal, strict, _ = _tri_masks(c)
    gp = gc - lw
    row = lax.broadcasted_iota(jnp.int32, lw.shape, 0)
    lane = lax.broadcasted_iota(jnp.int32, lw.shape, 1)
    g_mid = jnp.sum(jnp.where(row < c // 2, lw, 0.0), axis=0, keepdims=True)
    g_last = jnp.sum(lw, axis=0, keepdims=True)
    e_n = jnp.exp(g_mid - gc)
    rg = r * jnp.exp(gc - g_mid)
    bg = be * jnp.exp(gp - g_mid)
    an = al * e_n
    kn = k * e_n
    bt = mm(be * jnp.exp(gp), s0, "nt")
    rt = mm(r * jnp.exp(gc), s0, "nt")
    us, ys, ps = [], [], []
    for h, saved in enumerate((saved0, saved1)):
        mine = (lane >= RW_HEAD) if h else (lane < RW_HEAD)
        bgh = jnp.where(mine, bg, 0.0)
        rgh = jnp.where(mine, rg, 0.0)
        a_ab = jnp.where(strict, mm(bgh, an, "nt"), 0.0)
        a_kb = jnp.where(strict, mm(bgh, kn, "nt"), 0.0)
        a_ra = jnp.where(causal, mm(rgh, an, "nt"), 0.0)
        a_rk = jnp.where(causal, mm(rgh, kn, "nt"), 0.0)
        p = _inverse(a_ab, saved)
        ps.append(p)
        u_h = mm(p, bt + mm(a_kb, v))
        us.append(u_h)
        ys.append(rt + mm(a_ra, u_h) + mm(a_rk, v))
    lo = lane < RW_HEAD
    u = jnp.where(lo, us[0], us[1])
    y = jnp.where(lo, ys[0], ys[1])
    tail = jnp.exp(g_last - gc)
    s1 = s0 * jnp.exp(g_last) + mm(u, al * tail, "tn") + mm(v, k * tail, "tn")
    vi = lax.broadcasted_iota(jnp.int32, s0.shape, 0)
    ki = lax.broadcasted_iota(jnp.int32, s0.shape, 1)
    s1 = jnp.where((vi < RW_HEAD) == (ki < RW_HEAD), s1, 0.0)
    return y, s1, ps[0], ps[1]


SCAN_HB = 8


def _scan_specs(arrs, n_chunks, reverse):
    def spec(off):
        assert off % SCAN_HB == 0
        if reverse:
            return pl.BlockSpec((CHUNK, SCAN_HB * LANES), lambda h, n: (n_chunks - 1 - n, off // SCAN_HB + h))
        return pl.BlockSpec((CHUNK, SCAN_HB * LANES), lambda h, n: (n, off // SCAN_HB + h))
    return [spec(off) for _, off in arrs]


def _split_heads(x):
    return jnp.stack([x[:, LANES * j:LANES * (j + 1)] for j in range(SCAN_HB)], axis=0)


def _merge_heads(x):
    return jnp.concatenate([x[j] for j in range(SCAN_HB)], axis=1)


def _scan_fwd(group_fn, name, arrs, heads, n_kept):
    s = arrs[0][0].shape[0]
    n_chunks = s // CHUNK
    n_in = len(arrs)

    def body(*refs):
        y_ref, st_ref = refs[n_in:n_in + 2]
        kept_refs, s_scr = refs[n_in + 2:-1], refs[-1]

        @pl.when(pl.program_id(1) == 0)
        def _():
            s_scr[...] = jnp.zeros_like(s_scr)

        s0 = s_scr[...]
        st_ref[...] = s0
        y, s1, *kept = group_fn(s0, *[_split_heads(r[...]) for r in refs[:n_in]])
        y_ref[...] = _merge_heads(y)
        s_scr[...] = s1
        for ref, val in zip(kept_refs, kept):
            ref[...] = val

    per_chunk = pl.BlockSpec((SCAN_HB, None, LANES, LANES), lambda h, n: (h, n, 0, 0))
    res = pl.pallas_call(
        body, grid=(heads // SCAN_HB, n_chunks), name=name,
        in_specs=_scan_specs(arrs, n_chunks, False),
        out_specs=[pl.BlockSpec((CHUNK, SCAN_HB * LANES), lambda h, n: (n, h))] + [per_chunk] * (1 + n_kept),
        out_shape=[SDS((s, heads * LANES), F32)] + [SDS((heads, n_chunks, LANES, LANES), F32)] * (1 + n_kept),
        scratch_shapes=[pltpu.VMEM((SCAN_HB, LANES, LANES), F32)],
        compiler_params=pltpu.CompilerParams(dimension_semantics=("arbitrary", "arbitrary")),
    )(*[a for a, _ in arrs])
    return res[0], res[1:]


def _scan_bwd(group_fn, name, arrs, kept, dy, heads):
    s = arrs[0][0].shape[0]
    n_chunks = s // CHUNK
    n_in, n_kept = len(arrs), len(kept)

    def body(*refs):
        kept_vals = [r[...] for r in refs[n_in:n_in + n_kept]]
        dy_ref = refs[n_in + n_kept]
        d_refs = refs[n_in + n_kept + 1:2 * n_in + n_kept + 1]
        ds_scr = refs[-1]

        @pl.when(pl.program_id(1) == 0)
        def _():
            ds_scr[...] = jnp.zeros_like(ds_scr)

        def fn(s0, *ins):
            return group_fn(s0, *ins, *kept_vals[1:])[:2]

        _, vjp = jax.vjp(fn, kept_vals[0], *[_split_heads(r[...]) for r in refs[:n_in]])
        grads = vjp((_split_heads(dy_ref[...]), ds_scr[...]))
        ds_scr[...] = grads[0]
        for ref, g in zip(d_refs, grads[1:]):
            ref[...] = _merge_heads(g)

    rev = pl.BlockSpec((CHUNK, SCAN_HB * LANES), lambda h, n: (n_chunks - 1 - n, h))
    per_chunk = pl.BlockSpec((SCAN_HB, None, LANES, LANES), lambda h, n: (h, n_chunks - 1 - n, 0, 0))
    return pl.pallas_call(
        body, grid=(heads // SCAN_HB, n_chunks), name=name,
        in_specs=_scan_specs(arrs, n_chunks, True) + [per_chunk] * n_kept + [rev],
        out_specs=[rev] * n_in,
        out_shape=[SDS((s, heads * LANES), F32)] * n_in,
        scratch_shapes=[pltpu.VMEM((SCAN_HB, LANES, LANES), F32)],
        compiler_params=pltpu.CompilerParams(dimension_semantics=("arbitrary", "arbitrary")),
    )(*[a for a, _ in arrs], *kept, dy)


def _col_spec(tr, width, cb):
    return pl.BlockSpec((tr, width), lambda i: (i, cb))


def _whole(p):
    return pl.BlockSpec(p.shape, lambda i: (0,) * p.ndim)


def _row_fwd(fn, name, tiles, params, outs, tr):
    rows = tiles[0][0].shape[0]
    nt, npar = len(tiles), len(params)

    def body(*refs):
        vals = [r[...].astype(F32) for r in refs[:nt + npar]]
        for ref, o in zip(refs[nt + npar:], fn(*vals)):
            ref[...] = o.astype(ref.dtype)

    return pl.pallas_call(
        body, grid=(rows // tr,), name=name,
        in_specs=[_col_spec(tr, w, cb) for _, w, cb in tiles] + [_whole(p) for p in params],
        out_specs=[_col_spec(tr, w, 0) for w, _ in outs],
        out_shape=[SDS((rows, w), dt) for w, dt in outs],
        compiler_params=pltpu.CompilerParams(dimension_semantics=("arbitrary",), vmem_limit_bytes=VMEM_LIMIT),
    )(*[a for a, _, _ in tiles], *params)


def _row_bwd(fn, name, tiles, params, cts, tr, want_tiles=None):
    rows = tiles[0][0].shape[0]
    nt, npar = len(tiles), len(params)
    want = list(range(nt)) if want_tiles is None else list(want_tiles)
    flat_cts = [c for group in cts for c in group]
    n_ct = len(flat_cts)

    def body(*refs):
        vals = [r[...].astype(F32) for r in refs[:nt + npar]]
        ct_refs = refs[nt + npar:nt + npar + n_ct]
        out_refs = refs[nt + npar + n_ct:]
        ct_vals, at = [], 0
        for group in cts:
            total = ct_refs[at][...].astype(F32)
            for r in ct_refs[at + 1:at + len(group)]:
                total = total + r[...].astype(F32)
            ct_vals.append(total)
            at += len(group)
        _, vjp = jax.vjp(lambda *a: tuple(fn(*a)), *vals)
        grads = vjp(tuple(ct_vals))
        for ref, t in zip(out_refs[:len(want)], want):
            ref[...] = grads[t]
        first = pl.program_id(0) == 0
        for ref, g in zip(out_refs[len(want):], grads[nt:]):
            @pl.when(first)
            def _(ref=ref, g=g):
                ref[...] = g

            @pl.when(jnp.logical_not(first))
            def _(ref=ref, g=g):
                ref[...] += g

    res = pl.pallas_call(
        body, grid=(rows // tr,), name=name,
        in_specs=[_col_spec(tr, w, cb) for _, w, cb in tiles] + [_whole(p) for p in params]
        + [_col_spec(tr, w, cb) for _, w, cb in flat_cts],
        out_specs=[_col_spec(tr, tiles[t][1], 0) for t in want] + [_whole(p) for p in params],
        out_shape=[SDS((rows, tiles[t][1]), F32) for t in want] + [SDS(p.shape, F32) for p in params],
        compiler_params=pltpu.CompilerParams(dimension_semantics=("arbitrary",), vmem_limit_bytes=VMEM_LIMIT),
    )(*[a for a, _, _ in tiles], *params, *[a for a, _, _ in flat_cts])
    return res[:len(want)], res[len(want):]


def _col_fwd(fn, name, x, first_block, n_blocks, params):
    rows = x.shape[0]

    def body(*refs):
        refs[-1][...] = fn(*[r[...] for r in refs[:-1]])

    return pl.pallas_call(
        body, grid=(n_blocks,), name=name,
        in_specs=[pl.BlockSpec((rows, LANES), lambda j: (0, first_block + j))]
        + [pl.BlockSpec((p.shape[0], LANES), lambda j: (0, j)) for p in params],
        out_specs=pl.BlockSpec((rows, LANES), lambda j: (0, j)),
        out_shape=SDS((rows, n_blocks * LANES), F32),
        compiler_params=pltpu.CompilerParams(dimension_semantics=("arbitrary",), vmem_limit_bytes=VMEM_LIMIT),
    )(x, *params)


def _col_bwd(fn, name, x, first_block, n_blocks, params, dy):
    rows = x.shape[0]
    npar = len(params)

    def body(*refs):
        vals = [r[...] for r in refs[:1 + npar]]
        _, vjp = jax.vjp(fn, *vals)
        grads = vjp(refs[1 + npar][...])
        for ref, g in zip(refs[2 + npar:], grads):
            ref[...] = g

    pspecs = [pl.BlockSpec((p.shape[0], LANES), lambda j: (0, j)) for p in params]
    blk = pl.BlockSpec((rows, LANES), lambda j: (0, j))
    res = pl.pallas_call(
        body, grid=(n_blocks,), name=name,
        in_specs=[pl.BlockSpec((rows, LANES), lambda j: (0, first_block + j))] + pspecs + [blk],
        out_specs=[blk] + pspecs,
        out_shape=[SDS((rows, n_blocks * LANES), F32)] + [SDS(p.shape, F32) for p in params],
        compiler_params=pltpu.CompilerParams(dimension_semantics=("arbitrary",), vmem_limit_bytes=VMEM_LIMIT),
    )(x, *params, dy)
    return res[0], res[1:]


def _conv_fn(x, w):
    acc = x * w[3:4, :]
    for j in range(3):
        acc = acc + shift_rows(x, 3 - j) * w[j:j + 1, :]
    return _silu(acc)


def _lerp_fn(x, mu):
    return x + (shift_rows(x, 1) - x) * mu[0:1, :]


def _seg_sum(x, width):
    if width == LANES:
        return jnp.sum(x, axis=1, keepdims=True)
    lo = lax.broadcasted_iota(jnp.int32, x.shape, 1) < width
    s0 = jnp.sum(jnp.where(lo, x, 0.0), axis=1, keepdims=True)
    s1 = jnp.sum(jnp.where(lo, 0.0, x), axis=1, keepdims=True)
    return jnp.where(lo, s0, s1)


def _per_block(fn, *xs):
    n = xs[0].shape[1] // LANES
    return jnp.concatenate([fn(*[x[:, LANES * b:LANES * (b + 1)] for x in xs]) for b in range(n)], axis=1)


def _head_expand(col0):
    r = lax.broadcasted_iota(jnp.int32, (LANES, DN_WIDTH), 0)
    c = lax.shift_right_logical(lax.broadcasted_iota(jnp.int32, (LANES, DN_WIDTH), 1), 7)
    return jnp.where(r == c + col0, 1.0, 0.0)


def _dn_pre_fn(cq, ck, gates, a_log, dt_bias):
    l2 = lambda x: x * lax.rsqrt(_seg_sum(x * x, LANES) + 1e-6)
    qh = _per_block(l2, cq) * (LANES ** -0.5)
    kh = _per_block(l2, ck)
    g = -jnp.exp(a_log) * _softplus(gates + dt_bias)
    gb = mm(g, _head_expand(0), "nn", True)
    bb = mm(_sigmoid(gates), _head_expand(DN_HEADS), "nn", True)
    return qh, kh, gb, bb, _cumsum_rows(gb)


def _dn_post_fn(o, z, nw):
    def one(ob, zb):
        return ob * lax.rsqrt(_seg_sum(ob * ob, LANES) * (1.0 / LANES) + RMS_EPS) * nw * _silu(zb)
    return (_per_block(one, o, z),)


def _rw_pre_fn(pr, pk, pv, pwa, pg, w0, a0, k_k, k_a, w2p, a2p, g2):
    log_w = -_softplus(-(w0 + mm(jnp.tanh(pwa), w2p))) - 0.5
    lw = -jnp.exp(log_w)
    a = _sigmoid(a0 + mm(pwa, a2p))
    gate = mm(_sigmoid(pg), g2)
    kk = pk * k_k
    kk = _per_block(lambda x: x / jnp.maximum(jnp.sqrt(_seg_sum(x * x, RW_HEAD)), 1e-12), kk)
    k = pk * (1.0 + (a - 1.0) * k_a)
    return pr, lw, k, pv, kk * a, -kk, gate, _cumsum_rows(lw)


def _rw_post_fn(y, r, k, v, gate, ln_w, ln_b, r_k):
    def one(yb, rb, kb, vb, gb, wb, bb, rkb):
        d = yb - _seg_sum(yb, RW_HEAD) * (1.0 / RW_HEAD)
        var = _seg_sum(d * d, RW_HEAD) * (1.0 / RW_HEAD)
        yn = d * lax.rsqrt(var + RW_GN_EPS) * wb + bb
        return (yn + _seg_sum(rb * kb * rkb, RW_HEAD) * vb) * gb
    return (_per_block(one, y, r, k, v, gate, ln_w, ln_b, r_k),)


def _rms_fn(h, w):
    return (h * lax.rsqrt(jnp.mean(h * h, axis=1, keepdims=True) + RMS_EPS) * w,)


def _xattn_fn(q, k, v):
    outs = []
    for h in range(XA_HEADS):
        sl = slice(LANES * h, LANES * (h + 1))
        s = mm(q[:, sl], k[:, sl], "nt") * (LANES ** -0.5)
        e = jnp.exp(s - jnp.max(s, axis=1, keepdims=True))
        outs.append(mm(e / jnp.sum(e, axis=1, keepdims=True), v[:, sl]))
    return (jnp.concatenate(outs, axis=1),)


def _fit(tile, dim):
    best = [t for t in range(LANES, min(tile, dim) + 1, LANES) if dim % t == 0]
    assert best, (tile, dim)
    return best[-1]


def _matmul(name, a, b, mode, out_dtypes, epilogue=None, extras=(), tm=1024, tn=1024, tk=2048, after=None):
    if mode == "tn":
        (k_dim, m), n = a.shape, b.shape[1]
    else:
        (m, k_dim), n = a.shape, (b.shape[1] if mode == "nn" else b.shape[0])
    tm, tn, tk = _fit(tm, m), _fit(tn, n), _fit(tk, k_dim)
    nk = k_dim // tk
    a_spec = (pl.BlockSpec((tk, tm), lambda i, j, k: (k, i)) if mode == "tn"
              else pl.BlockSpec((tm, tk), lambda i, j, k: (i, k)))
    b_spec = (pl.BlockSpec((tn, tk), lambda i, j, k: (j, k)) if mode == "nt"
              else pl.BlockSpec((tk, tn), lambda i, j, k: (k, j)))
    o_spec = pl.BlockSpec((tm, tn), lambda i, j, k: (i, j))
    n_ex, n_out = len(extras), len(out_dtypes)
    ties = [] if after is None else [after]

    def finish(total, rest):
        ex = [r[...].astype(F32) for r in rest[:n_ex]]
        res = epilogue(total, *ex) if epilogue else (total,)
        for ref, o in zip(rest[n_ex + len(ties):n_ex + len(ties) + n_out], res):
            ref[...] = o.astype(ref.dtype)

    def body_single(a_ref, b_ref, *rest):
        finish(_raw_dot(a_ref[...], b_ref[...], mode, False), rest)

    def body_acc(a_ref, b_ref, *rest):
        acc = rest[-1]
        k = pl.program_id(2)

        @pl.when(k == 0)
        def _():
            acc[...] = jnp.zeros_like(acc)

        acc[...] += _raw_dot(a_ref[...], b_ref[...], mode, False)

        @pl.when(k == nk - 1)
        def _():
            finish(acc[...], rest)

    res = pl.pallas_call(
        body_single if nk == 1 else body_acc, grid=(m // tm, n // tn, nk), name=name,
        in_specs=[a_spec, b_spec] + [o_spec] * n_ex + [pl.BlockSpec((8, LANES), lambda i, j, k: (0, 0))] * len(ties),
        out_specs=[o_spec] * n_out,
        out_shape=[SDS((m, n), dt) for dt in out_dtypes],
        scratch_shapes=[] if nk == 1 else [pltpu.VMEM((tm, tn), F32)],
        compiler_params=pltpu.CompilerParams(dimension_semantics=("parallel", "parallel", "arbitrary"),
                                             vmem_limit_bytes=VMEM_LIMIT),
    )(a, b, *extras, *ties)
    return res


def _loss_call(h, target, w, tr=256):
    rows, d = h.shape

    def fn(hv, wv, tv):
        y = _rms_fn(hv, wv)[0]
        return 0.5 * jnp.sum(jnp.mean(jnp.square(y - tv), axis=1, keepdims=True), axis=0, keepdims=True)

    def body(h_ref, t_ref, w_ref, loss_ref, dh_ref, dw_ref):
        tv = t_ref[...]
        val, vjp = jax.vjp(lambda hv, wv: fn(hv, wv, tv), h_ref[...], w_ref[...])
        dh, dw = vjp(jnp.ones((1, 1), F32))
        dh_ref[...] = dh
        first = pl.program_id(0) == 0

        @pl.when(first)
        def _():
            loss_ref[...] = jnp.broadcast_to(val, loss_ref.shape)
            dw_ref[...] = dw

        @pl.when(jnp.logical_not(first))
        def _():
            loss_ref[...] += jnp.broadcast_to(val, loss_ref.shape)
            dw_ref[...] += dw

    return pl.pallas_call(
        body, grid=(rows // tr,), name="loss_head",
        in_specs=[_col_spec(tr, d, 0), _col_spec(tr, d, 0), _whole(w)],
        out_specs=[pl.BlockSpec((8, LANES), lambda i: (0, 0)), _col_spec(tr, d, 0), _whole(w)],
        out_shape=[SDS((8, LANES), F32), SDS((rows, d), F32), SDS(w.shape, F32)],
        compiler_params=pltpu.CompilerParams(dimension_semantics=("arbitrary",), vmem_limit_bytes=VMEM_LIMIT),
    )(h, target, w)


def _adamw_vals(w, g, m, v):
    m = ADAM_B1 * m + (1.0 - ADAM_B1) * g
    v = ADAM_B2 * v + (1.0 - ADAM_B2) * jnp.square(g)
    m_hat = m / (1.0 - ADAM_B1 ** ADAM_STEP)
    v_hat = v / (1.0 - ADAM_B2 ** ADAM_STEP)
    delta = -ADAM_LR * (m_hat / (jnp.sqrt(v_hat) + ADAM_EPS) + ADAM_WD * w)
    return delta, m, v


def _sum_adamw(name, parts, w, m, v):
    r, c = w.shape
    tr = r
    for cand in (512, 256, 128, 64, 32, 16, 8):
        if r % cand == 0 and N_DEV * cand * c * 4 <= 6 * 1024 * 1024:
            tr = cand
            break

    def body(p_ref, w_ref, m_ref, v_ref, g_ref, d_ref, m2_ref, v2_ref):
        g = p_ref[0].astype(F32)
        for s in range(1, N_DEV):
            g = g + p_ref[s].astype(F32)
        g_ref[...] = g
        d_ref[...], m2_ref[...], v2_ref[...] = _adamw_vals(w_ref[...], g, m_ref[...], v_ref[...])

    blk = pl.BlockSpec((tr, c), lambda i: (i, 0))
    return pl.pallas_call(
        body, grid=(r // tr,), name=name,
        in_specs=[pl.BlockSpec((N_DEV, tr, c), lambda i: (0, i, 0)), blk, blk, blk],
        out_specs=[blk] * 4, out_shape=[SDS((r, c), F32)] * 4,
        compiler_params=pltpu.CompilerParams(dimension_semantics=("arbitrary",), vmem_limit_bytes=VMEM_LIMIT),
    )(parts, w, m, v)


def _peers():
    x, y, c = lax.axis_index("x"), lax.axis_index("y"), lax.axis_index("c")
    peers = []
    for k in range(1, N_DEV):
        px = 1 - x if k & 4 else x
        py = 1 - y if k & 2 else y
        pc = 1 - c if k & 1 else c
        peers.append(((px, py, pc), 4 * px + 2 * py + pc))
    return 4 * x + 2 * y + c, peers


def _slot(ref, idx, cols):
    if cols is None:
        return ref.at[idx]
    return ref.at[:, pl.ds(pl.multiple_of(idx * cols, LANES), cols)]


def _exchange(name, srcs, dsts, gather):
    n = len(srcs)

    def body(*refs):
        start, wait = _exchange_ops([c for _, c in srcs], [c for _, _, c in dsts], gather,
                                    refs[:n], refs[n:2 * n], *refs[2 * n:])
        start()
        wait()

    any_spec = pl.BlockSpec(memory_space=pl.ANY)
    return pl.pallas_call(
        body, name=name,
        in_specs=[any_spec] * n, out_specs=[any_spec] * n,
        out_shape=[SDS(shape, dt) for shape, dt, _ in dsts],
        scratch_shapes=_exchange_sems(n),
    )(*[a for a, _ in srcs])


def _gather_two_level(name, srcs, dsts):
    n = len(srcs)
    dst_cols = [c for _, _, c in dsts]

    def body(*refs):
        src_refs, out_refs = refs[:n], refs[n:2 * n]
        send_sems, recv_sems, local_sems = refs[2 * n:]
        x, y, c = lax.axis_index("x"), lax.axis_index("y"), lax.axis_index("c")
        index = lambda px, py, pc: 4 * px + 2 * py + pc
        me, sibling = index(x, y, c), (x, y, 1 - c)
        chips = [(x, 1 - y), (1 - x, y), (1 - x, 1 - y)]

        def copy(a, k, src, block, to):
            return pltpu.make_async_remote_copy(
                src_ref=src, dst_ref=_slot(out_refs[a], block, dst_cols[a]),
                send_sem=send_sems.at[a, k], recv_sem=recv_sems.at[a, k],
                device_id=to, device_id_type=pl.DeviceIdType.MESH)

        local, first, passed = [], [], []
        for a in range(n):
            cp = pltpu.make_async_copy(src_refs[a], _slot(out_refs[a], me, dst_cols[a]), local_sems.at[a])
            cp.start()
            local.append(cp)
            first.append(copy(a, 0, src_refs[a], me, sibling))
            first += [copy(a, 1 + j, src_refs[a], me, (*chip, c)) for j, chip in enumerate(chips)]
        for cp in first:
            cp.start()
        for a in range(n):
            for j, chip in enumerate(chips):
                block = index(*chip, c)
                arrived = _slot(out_refs[a], block, dst_cols[a])
                copy(a, 1 + j, arrived, block, (*chip, c)).wait_recv()
                passed.append(copy(a, 4 + j, arrived, block, sibling))
                passed[-1].start()
        for a in range(n):
            copy(a, 0, src_refs[a], index(x, y, 1 - c), sibling).wait_recv()
            for j, chip in enumerate(chips):
                block = index(*chip, 1 - c)
                copy(a, 4 + j, src_refs[a], block, sibling).wait_recv()
        for cp in first + passed:
            cp.wait_send()
        for cp in local:
            cp.wait()

    any_spec = pl.BlockSpec(memory_space=pl.ANY)
    return pl.pallas_call(
        body, name=name,
        in_specs=[any_spec] * n, out_specs=[any_spec] * n,
        out_shape=[SDS(shape, dt) for shape, dt, _ in dsts],
        scratch_shapes=_exchange_sems(n),
    )(*[a for a, _ in srcs])


_HBM = pl.BlockSpec(memory_space=pltpu.HBM)
_SEM = pl.BlockSpec(memory_space=pltpu.SEMAPHORE)
_EFFECT = pltpu.SideEffectType.DATAFLOW_SIDE_EFFECTING


def _split_copies(src_cols, dst_cols, gather, src_refs, land_refs, send_sems, recv_sems, landings):
    me, peers = _peers()
    n = len(src_cols)
    remote, local = [], []
    for a, (s_cols, d_cols) in enumerate(zip(src_cols, dst_cols)):
        mine = src_refs[a] if gather else _slot(src_refs[a], me, s_cols)
        local.append(pltpu.make_async_copy(mine, _slot(land_refs[a], me, d_cols),
                                           send_sems.at[n * (N_DEV - 1) + a]))
        for k, (pos, idx) in enumerate(peers):
            blk = src_refs[a] if gather else _slot(src_refs[a], idx, s_cols)
            remote.append(pltpu.make_async_remote_copy(
                src_ref=blk, dst_ref=_slot(land_refs[a], idx if landings else me, d_cols),
                send_sem=send_sems.at[a * (N_DEV - 1) + k], recv_sem=recv_sems.at[a * (N_DEV - 1) + k],
                device_id=pos, device_id_type=pl.DeviceIdType.MESH))
    return remote, local


def _exchange_start(name, srcs, dsts, gather, after):
    n = len(srcs)
    src_cols, dst_cols = [c for _, c in srcs], [c for _, _, c in dsts]

    def body(*refs):
        src_refs, land_refs = refs[:n], refs[n:2 * n]
        send_sems, recv_sems = refs[2 * n + 1:2 * n + 3]
        token = refs[-1]
        remote, local = _split_copies(src_cols, dst_cols, gather, src_refs, land_refs, send_sems, recv_sems, False)
        for cp in remote + local:
            cp.start()
        token[...] = jnp.zeros_like(token)

    hbm = lambda a: pltpu.with_memory_space_constraint(a, pltpu.HBM)
    lands = [hbm(lax.empty(shape, dt)) for shape, dt, _ in dsts]
    res = pl.pallas_call(
        body, name=name,
        out_shape=(pltpu.SemaphoreType.DMA((n * N_DEV,)), pltpu.SemaphoreType.DMA((n * (N_DEV - 1),)),
                   *[pltpu.HBM(a.shape, a.dtype) for a, _ in srcs], *[pltpu.HBM(a.shape, a.dtype) for a in lands],
                   SDS((8, LANES), F32)),
        in_specs=[_HBM] * (2 * n) + [pl.BlockSpec(memory_space=pl.ANY)],
        out_specs=(_SEM, _SEM, *[_HBM] * (2 * n), pl.BlockSpec(memory_space=pltpu.VMEM)),
        input_output_aliases={i: 2 + i for i in range(2 * n)},
        compiler_params=pltpu.CompilerParams(has_side_effects=_EFFECT),
    )(*[hbm(a) for a, _ in srcs], *lands, after)
    handle = (res[0], res[1], res[2:2 + n], res[2 + n:2 + 2 * n], src_cols, dst_cols, gather)
    return handle, res[-1]


def _exchange_wait(name, handle, after):
    send_sems, recv_sems, src_thru, land_thru, src_cols, dst_cols, gather = handle
    n = len(src_thru)

    def body(*refs):
        src_refs, land_refs = refs[:n], refs[n:2 * n]
        s_sems, r_sems = refs[2 * n:2 * n + 2]
        remote, local = _split_copies(src_cols, dst_cols, gather, src_refs, land_refs, s_sems, r_sems, True)
        for cp in remote:
            cp.wait_send()
            cp.wait_recv()
        for cp in local:
            cp.wait()

    res = pl.pallas_call(
        body, name=name,
        out_shape=tuple(pltpu.HBM(a.shape, a.dtype) for a in (*src_thru, *land_thru)),
        in_specs=[_HBM] * (2 * n) + [_SEM, _SEM, pl.BlockSpec(memory_space=pl.ANY)],
        out_specs=tuple([_HBM] * (2 * n)),
        input_output_aliases={i: i for i in range(2 * n)},
        compiler_params=pltpu.CompilerParams(has_side_effects=_EFFECT),
    )(*src_thru, *land_thru, send_sems, recv_sems, after)
    return res[n:]


def _exchange_sems(n):
    return [pltpu.SemaphoreType.DMA((n, N_DEV - 1)), pltpu.SemaphoreType.DMA((n, N_DEV - 1)),
            pltpu.SemaphoreType.DMA((n,))]


def _exchange_ops(src_cols, dst_cols, gather, src_refs, out_refs, send_sems, recv_sems, local_sems):
    def copies(with_landings):
        me, peers = _peers()
        local, sends, landings = [], [], []
        for a, (s_cols, d_cols) in enumerate(zip(src_cols, dst_cols)):
            mine = src_refs[a] if gather else _slot(src_refs[a], me, s_cols)
            local.append(pltpu.make_async_copy(mine, _slot(out_refs[a], me, d_cols), local_sems.at[a]))
            for k, (pos, idx) in enumerate(peers):
                out_blk = src_refs[a] if gather else _slot(src_refs[a], idx, s_cols)
                both = dict(src_ref=out_blk, send_sem=send_sems.at[a, k], recv_sem=recv_sems.at[a, k],
                            device_id=pos, device_id_type=pl.DeviceIdType.MESH)
                sends.append(pltpu.make_async_remote_copy(dst_ref=_slot(out_refs[a], me, d_cols), **both))
                if with_landings:
                    landings.append(pltpu.make_async_remote_copy(dst_ref=_slot(out_refs[a], idx, d_cols), **both))
        return local, sends, landings

    def start():
        local, sends, _ = copies(False)
        for cp in local + sends:
            cp.start()

    def wait():
        local, sends, landings = copies(True)
        for cp in landings:
            cp.wait_recv()
        for cp in sends:
            cp.wait_send()
        for cp in local:
            cp.wait()

    return start, wait


def _rms_res_fn(h, w):
    return _rms_fn(h, w)[0], h


def _add_epilogue(acc, res):
    return (acc + res,)


def _gather_plan(shards):
    srcs, dsts = [], []
    for n, sh in shards.items():
        r, c = sh.shape
        srcs.append((sh, None))
        if SHARDED[n] and c % LANES == 0:
            dsts.append(((r, N_DEV * c), sh.dtype, c))
        else:
            dsts.append(((N_DEV, r, c), sh.dtype, None))
    return srcs, dsts, True


def _w_in_segments():
    out = []
    for j in range(N_DEV):
        lo, hi = W_IN_SHARD * j, W_IN_SHARD * (j + 1)
        for a, b in ((lo, min(hi, DN_COLS)), (max(lo, DN_COLS), hi)):
            if a < b:
                out.append((j, a - lo, b - lo, a if a < DN_COLS else a + RW_OFF - DN_COLS))
    return out


def _w_in_to_padded(shards, tr=256):
    _, rows, _ = shards.shape

    def body(g_ref, o_ref):
        o_ref[...] = jnp.zeros_like(o_ref)
        for j, a, b, dst in _w_in_segments():
            o_ref[:, dst:dst + b - a] = g_ref[j, :, a:b]

    return pl.pallas_call(
        body, grid=(rows // tr,), name="w_in_to_padded",
        in_specs=[pl.BlockSpec((N_DEV, tr, W_IN_SHARD), lambda i: (0, i, 0))],
        out_specs=pl.BlockSpec((tr, IN_PAD), lambda i: (i, 0)),
        out_shape=SDS((rows, IN_PAD), shards.dtype),
        compiler_params=pltpu.CompilerParams(dimension_semantics=("arbitrary",), vmem_limit_bytes=VMEM_LIMIT),
    )(shards)


def _w_in_grad_to_shards(gw, tr=256):
    rows, _ = gw.shape

    def body(w_ref, o_ref):
        for j, a, b, dst in _w_in_segments():
            o_ref[j, :, a:b] = w_ref[:, dst:dst + b - a]

    return pl.pallas_call(
        body, grid=(rows // tr,), name="w_in_grad_to_shards",
        in_specs=[pl.BlockSpec((tr, IN_PAD), lambda i: (i, 0))],
        out_specs=pl.BlockSpec((N_DEV, tr, W_IN_SHARD), lambda i: (0, i, 0)),
        out_shape=SDS((N_DEV, rows, W_IN_SHARD), gw.dtype),
        compiler_params=pltpu.CompilerParams(dimension_semantics=("arbitrary",), vmem_limit_bytes=VMEM_LIMIT),
    )(gw)


def _gather_finish(names, outs):
    full = {}
    for n, arr in zip(names, outs):
        if n == "w_in":
            full[n] = _w_in_to_padded(arr)
        elif arr.ndim == 2:
            full[n] = arr
        elif SHARDED[n]:
            full[n] = arr.transpose(1, 0, 2).reshape(arr.shape[1], -1)
        else:
            full[n] = arr.reshape(-1, arr.shape[2])
    return full


def _scatter_plan(grads):
    srcs, dsts = [], []
    for n, gr in grads.items():
        if gr.ndim == 3:
            srcs.append((gr, None))
            dsts.append((gr.shape, gr.dtype, None))
            continue
        rows, cols = gr.shape
        if not SHARDED[n]:
            r, c = rows // N_DEV, cols
            srcs.append((gr.reshape(N_DEV, r, c), None))
        else:
            r, c = rows, cols // N_DEV
            if c % LANES == 0:
                srcs.append((gr, c))
            else:
                srcs.append((gr.reshape(r, N_DEV, c).transpose(1, 0, 2), None))
        dsts.append(((N_DEV, r, c), gr.dtype, None))
    return srcs, dsts, False


def _local_step(x, mem, target, wt, late):
    d = D_MODEL
    g = {}
    wt = dict(wt)
    grp_a = ("w_out", "xa_wq", "xa_wk", "xa_wv", "xa_wo")
    grp_b = ("ffn_w1", "ffn_w2")
    handle_a, tok_a = _exchange_start("late_gather_a_start", *_gather_plan({n: late[n] for n in grp_a}),
                                      wt["w_in"])
    handle_w1, tok_b = _exchange_start("late_gather_w1_start", *_gather_plan({"ffn_w1": late["ffn_w1"]}), tok_a)
    handle_w2, tok_c = _exchange_start("late_gather_w2_start", *_gather_plan({"ffn_w2": late["ffn_w2"]}), tok_b)
    mix_w = wt["mix_norm_w"] + (tok_a[0:1, 0:1] + tok_b[0:1, 0:1] + tok_c[0:1, 0:1])
    u = _row_fwd(_rms_fn, "mix_norm", [(x, d, 0)], [mix_w], [(d, BF16)], 256)[0]
    p = _matmul("in_proj", u, wt["w_in"], "nn", [F32], tn=1536)[0]
    c = _col_fwd(_conv_fn, "dn_conv", p, 0, 24, [wt["dn_conv_w"]])
    dn_pre_tiles = [(c, DN_WIDTH, 0), (c, DN_WIDTH, 1), (p, LANES, 32)]
    dn_pre_params = [wt["dn_a_log"], wt["dn_dt_bias"]]
    qh, kh, gb, bb, gcb = _row_fwd(_dn_pre_fn, "dn_pre", dn_pre_tiles, dn_pre_params, [(DN_WIDTH, F32)] * 5, CHUNK)
    dn_arrs = [(qh, 0), (kh, 0), (c, 16), (gb, 0), (bb, 0), (gcb, 0)]
    o, kept_dn = _scan_fwd(_gdn_group, "gdn_scan", dn_arrs, DN_HEADS, 1)
    dn_post_tiles = [(o, DN_WIDTH, 0), (p, DN_WIDTH, 3)]
    o_dn = _row_fwd(_dn_post_fn, "dn_post", dn_post_tiles, [wt["dn_norm_w"]], [(DN_WIDTH, BF16)], 256)[0]

    ps = _col_fwd(_lerp_fn, "rw_shift", p, RW_OFF // LANES, 26, [wt["rw_mu"]])
    rw_pre_tiles = [(ps, RW_WIDTH, 0), (ps, RW_WIDTH, 1), (ps, RW_WIDTH, 2), (ps, LANES, 24), (ps, LANES, 25)]
    rw_pre_params = [wt[n] for n in ("rw_w0", "rw_a0", "rw_k_k", "rw_k_a", "rw_w2", "rw_a2", "rw_g2")]
    r, lw, k, v, al, be, gate, gcw = _row_fwd(_rw_pre_fn, "rw_pre", rw_pre_tiles, rw_pre_params,
                                              [(RW_WIDTH, F32)] * 8, CHUNK)
    rw_arrs = [(r, 0), (lw, 0), (k, 0), (v, 0), (al, 0), (be, 0), (gcw, 0)]
    y, kept_rw = _scan_fwd(_rw_group, "rw_scan", rw_arrs, RW_WIDTH // LANES, 2)
    rw_post_tiles = [(t, RW_WIDTH, 0) for t in (y, r, k, v, gate)]
    rw_post_params = [wt["rw_ln_w"], wt["rw_ln_b"], wt["rw_r_k"]]
    o_rw = _row_fwd(_rw_post_fn, "rw_post", rw_post_tiles, rw_post_params, [(RW_WIDTH, BF16)], 128)[0]
    o_cat = jnp.concatenate([o_dn, o_rw], axis=1)
    wt.update(_gather_finish(grp_a, _exchange_wait("late_gather_a_wait", handle_a, o_cat)))
    h1 = _matmul("out_proj", o_cat, wt["w_out"], "nn", [F32], _add_epilogue, (x,))[0]

    hn = _row_fwd(_rms_fn, "xa_norm", [(h1, d, 0)], [wt["xa_norm_w"]], [(d, BF16)], 256)[0]
    mn = _row_fwd(_rms_fn, "mem_norm", [(mem, d, 0)], [wt["mem_norm_w"]], [(d, BF16)], 256)[0]
    q = _matmul("xa_q", hn, wt["xa_wq"], "nn", [F32])[0]
    kx = _matmul("xa_k", mn, wt["xa_wk"], "nn", [F32])[0]
    vx = _matmul("xa_v", mn, wt["xa_wv"], "nn", [F32])[0]
    ao = _row_fwd(_xattn_fn, "xattn", [(q, XA_WIDTH, 0)], [kx, vx], [(XA_WIDTH, BF16)], 256)[0]
    h2 = _matmul("xa_o", ao, wt["xa_wo"], "nn", [F32], _add_epilogue, (h1,))[0]

    f = _row_fwd(_rms_fn, "ffn_norm", [(h2, d, 0)], [wt["ffn_norm_w"]], [(d, BF16)], 256)[0]
    wt.update(_gather_finish(("ffn_w1",), _exchange_wait("late_gather_w1_wait", handle_w1, f)))
    a, hid = _matmul("ffn_up", f, wt["ffn_w1"], "nn", [F32, BF16],
                     lambda acc: (acc, jnp.square(jnp.maximum(acc, 0.0))))
    wt.update(_gather_finish(("ffn_w2",), _exchange_wait("late_gather_w2_wait", handle_w2, hid)))
    h3 = _matmul("ffn_down", hid, wt["ffn_w2"], "nn", [F32], _add_epilogue, (h2,))[0]
    loss8, dh3, g["final_norm_w"] = _loss_call(h3, target, wt["final_norm_w"])

    da = _matmul("ffn_down_dx", dh3, wt["ffn_w2"], "nt", [BF16],
                 lambda acc, av: (acc * 2.0 * jnp.maximum(av, 0.0),), (a,))[0]
    g["ffn_w2"] = _matmul("ffn_down_dw", hid, dh3, "tn", [BF16])[0]
    g["ffn_w1"] = _matmul("ffn_up_dw", f, da, "tn", [BF16])[0]
    df = _matmul("ffn_up_dx", da, wt["ffn_w1"], "nt", [F32])[0]
    pending = {}
    plan = _scatter_plan({n: g.pop(n) for n in grp_b})
    pending[grp_b], tok = _exchange_start("late_grad_b_start", *plan, loss8)
    (dh2,), (g["ffn_norm_w"],) = _row_bwd(_rms_res_fn, "ffn_norm_bwd", [(h2, d, 0)],
                                          [wt["ffn_norm_w"] + tok[0:1, 0:1]],
                                          [[(df, d, 0)], [(dh3, d, 0)]], 256)

    dao = _matmul("xa_o_dx", dh2, wt["xa_wo"], "nt", [F32])[0]
    g["xa_wo"] = _matmul("xa_o_dw", ao, dh2, "tn", [BF16])[0]
    (dq,), (dkx, dvx) = _row_bwd(_xattn_fn, "xattn_bwd", [(q, XA_WIDTH, 0)], [kx, vx], [[(dao, XA_WIDTH, 0)]], 256)
    dhn = _matmul("xa_q_dx", dq, wt["xa_wq"], "nt", [F32])[0]
    g["xa_wq"] = _matmul("xa_q_dw", hn, dq, "tn", [BF16])[0]
    g["xa_wk"] = _matmul("xa_k_dw", mn, dkx, "tn", [BF16])[0]
    g["xa_wv"] = _matmul("xa_v_dw", mn, dvx, "tn", [BF16])[0]
    dmn = _matmul("xa_k_dx", dkx, wt["xa_wk"], "nt", [F32])[0]
    dmn = _matmul("xa_v_dx", dvx, wt["xa_wv"], "nt", [F32], _add_epilogue, (dmn,))[0]
    _, (g["mem_norm_w"],) = _row_bwd(_rms_fn, "mem_norm_bwd", [(mem, d, 0)], [wt["mem_norm_w"]],
                                     [[(dmn, d, 0)]], 256, want_tiles=())
    (dh1,), (g["xa_norm_w"],) = _row_bwd(_rms_res_fn, "xa_norm_bwd", [(h1, d, 0)], [wt["xa_norm_w"]],
                                         [[(dhn, d, 0)], [(dh2, d, 0)]], 256)

    do_cat = _matmul("out_proj_dx", dh1, wt["w_out"], "nt", [F32])[0]
    g["w_out"] = _matmul("out_proj_dw", o_cat, dh1, "tn", [BF16])[0]

    plan = _scatter_plan({n: g.pop(n) for n in grp_a})
    pending[grp_a], tok = _exchange_start("late_grad_a_start", *plan, tok)
    (dy, dr1, dk1, dv1, dgate), (g["rw_ln_w"], g["rw_ln_b"], g["rw_r_k"]) = _row_bwd(
        _rw_post_fn, "rw_post_bwd", rw_post_tiles, [rw_post_params[0] + tok[0:1, 0:1]] + rw_post_params[1:],
        [[(do_cat, RW_WIDTH, 1)]], 128)
    dr2, dlw, dk2, dv2, dal, dbe, dgcw = _scan_bwd(_rw_group, "rw_scan_bwd", rw_arrs, kept_rw, dy,
                                                   RW_WIDTH // LANES)
    one = lambda t: [(t, RW_WIDTH, 0)]
    two = lambda s, t: [(s, RW_WIDTH, 0), (t, RW_WIDTH, 0)]
    d_ps, rw_pre_grads = _row_bwd(
        _rw_pre_fn, "rw_pre_bwd", rw_pre_tiles, rw_pre_params,
        [two(dr1, dr2), one(dlw), two(dk1, dk2), two(dv1, dv2), one(dal), one(dbe), one(dgate), one(dgcw)],
        CHUNK)
    for n, val in zip(("rw_w0", "rw_a0", "rw_k_k", "rw_k_a", "rw_w2", "rw_a2", "rw_g2"), rw_pre_grads):
        g[n] = val
    dp_rw, (g["rw_mu"],) = _col_bwd(_lerp_fn, "rw_shift_bwd", p, RW_OFF // LANES, 26, [wt["rw_mu"]],
                                    jnp.concatenate(d_ps, axis=1))

    (do, dz), (g["dn_norm_w"],) = _row_bwd(_dn_post_fn, "dn_post_bwd", dn_post_tiles, [wt["dn_norm_w"]],
                                           [[(do_cat, DN_WIDTH, 0)]], 256)
    dqh, dkh, dv_dn, dgb, dbb, dgcb = _scan_bwd(_gdn_group, "gdn_scan_bwd", dn_arrs, kept_dn, do, DN_HEADS)
    one = lambda t: [(t, DN_WIDTH, 0)]
    (dcq, dck, dgates), (g["dn_a_log"], g["dn_dt_bias"]) = _row_bwd(
        _dn_pre_fn, "dn_pre_bwd", dn_pre_tiles, dn_pre_params,
        [one(dqh), one(dkh), one(dgb), one(dbb), one(dgcb)], CHUNK)
    dp_qkv, (g["dn_conv_w"],) = _col_bwd(_conv_fn, "dn_conv_bwd", p, 0, 24, [wt["dn_conv_w"]],
                                         jnp.concatenate([dcq, dck, dv_dn], axis=1))
    dp = jnp.concatenate([dp_qkv, dz, dgates, dp_rw, jnp.zeros((x.shape[0], LANES), F32)], axis=1).astype(BF16)
    g["w_in"] = _matmul("in_proj_dw", u, dp, "tn", [BF16], tn=1536)[0]
    early = _logical_grads(g)
    pending[EARLY], tok = _exchange_start("early_grad_start", *_scatter_plan({n: early.pop(n) for n in EARLY}), tok)
    du = _matmul("in_proj_dx", dp, wt["w_in"], "nt", [F32], after=tok)[0]
    (dx,), (early["mix_norm_w"],) = _row_bwd(_rms_res_fn, "mix_norm_bwd", [(x, d, 0)], [wt["mix_norm_w"]],
                                             [[(du, d, 0)], [(dh1, d, 0)]], 256)
    return loss8, dx, early, pending, tok


WEIGHTS = ["mix_norm_w", "w_in", "dn_conv_w", "dn_a_log", "dn_dt_bias", "dn_norm_w", "rw_mu", "rw_w0", "rw_w2",
           "rw_a0", "rw_a2", "rw_g2", "rw_k_k", "rw_k_a", "rw_r_k", "rw_ln_w", "rw_ln_b", "w_out", "xa_norm_w",
           "mem_norm_w", "xa_wq", "xa_wk", "xa_wv", "xa_wo", "ffn_norm_w", "ffn_w1", "ffn_w2", "final_norm_w"]
SHARDED = {"w_in": True, "w_out": False, "xa_wq": False, "xa_wk": False, "xa_wv": False, "xa_wo": True,
           "ffn_w1": True, "ffn_w2": False, "dn_conv_w": True, "rw_w2": True, "rw_a2": True, "rw_g2": True}
BF16_PAYLOAD = ("w_in", "w_out", "xa_wq", "xa_wk", "xa_wv", "xa_wo", "ffn_w1", "ffn_w2")
REPLICATED = [n for n in WEIGHTS if n not in SHARDED]
EARLY = ("w_in", "dn_conv_w", "rw_w2", "rw_a2", "rw_g2")
RW_IN_COLS = IN_COLS - DN_COLS
W_IN_SHARD = IN_COLS // N_DEV


def _layout_weights(fw):
    wt = dict(fw)
    wt["dn_conv_w"] = jnp.pad(fw["dn_conv_w"], ((0, 4), (0, 0)))
    wt["dn_a_log"] = jnp.pad(fw["dn_a_log"], ((0, 0), (0, LANES - DN_HEADS)))
    wt["dn_dt_bias"] = jnp.pad(fw["dn_dt_bias"], ((0, 0), (0, LANES - DN_HEADS)))
    wt["rw_w2"] = jnp.pad(fw["rw_w2"], ((0, 64), (0, 0)))
    wt["rw_a2"] = jnp.pad(fw["rw_a2"], ((64, 0), (0, 0)))
    return wt


def _logical_grads(g):
    out = dict(g)
    out["w_in"] = _w_in_grad_to_shards(g["w_in"])
    out["dn_conv_w"] = g["dn_conv_w"][:4]
    out["dn_a_log"] = g["dn_a_log"][:, :DN_HEADS]
    out["dn_dt_bias"] = g["dn_dt_bias"][:, :DN_HEADS]
    out["rw_w2"] = g["rw_w2"][:64]
    out["rw_a2"] = g["rw_a2"][64:]
    return out


def _pack(vals):
    parts = []
    for v in vals:
        flat = v.reshape(-1)
        parts.append(jnp.pad(flat, (0, -flat.shape[0] % LANES)))
    flat = jnp.concatenate(parts)
    flat = jnp.pad(flat, (0, -flat.shape[0] % (8 * LANES)))
    return flat.reshape(-1, LANES)


def _unpack(packed, shapes):
    flat = packed.reshape(-1)
    out, at = [], 0
    for shp in shapes:
        size = math.prod(shp)
        out.append(flat[at:at + size].reshape(shp))
        at += size + (-size % LANES)
    return out


def kernel(x, mem, mix_norm_w, w_in, dn_conv_w, dn_a_log, dn_dt_bias, dn_norm_w, rw_mu, rw_w0, rw_w2, rw_a0, rw_a2, rw_g2, rw_k_k, rw_k_a, rw_r_k, rw_ln_w, rw_ln_b, w_out, xa_norm_w, mem_norm_w, xa_wq, xa_wk, xa_wv, xa_wo, ffn_norm_w, ffn_w1, ffn_w2, final_norm_w, loss_target, m_mix_norm_w, m_w_in, m_dn_conv_w, m_dn_a_log, m_dn_dt_bias, m_dn_norm_w, m_rw_mu, m_rw_w0, m_rw_w2, m_rw_a0, m_rw_a2, m_rw_g2, m_rw_k_k, m_rw_k_a, m_rw_r_k, m_rw_ln_w, m_rw_ln_b, m_w_out, m_xa_norm_w, m_mem_norm_w, m_xa_wq, m_xa_wk, m_xa_wv, m_xa_wo, m_ffn_norm_w, m_ffn_w1, m_ffn_w2, m_final_norm_w, v_mix_norm_w, v_w_in, v_dn_conv_w, v_dn_a_log, v_dn_dt_bias, v_dn_norm_w, v_rw_mu, v_rw_w0, v_rw_w2, v_rw_a0, v_rw_a2, v_rw_g2, v_rw_k_k, v_rw_k_a, v_rw_r_k, v_rw_ln_w, v_rw_ln_b, v_w_out, v_xa_norm_w, v_mem_norm_w, v_xa_wq, v_xa_wk, v_xa_wv, v_xa_wo, v_ffn_norm_w, v_ffn_w1, v_ffn_w2, v_final_norm_w):
    given = dict(locals())
    w = {n: given[n] for n in WEIGHTS}
    m = {n: given["m_" + n] for n in WEIGHTS}
    v = {n: given["v_" + n] for n in WEIGHTS}

    shards = {n: (w[n][0].astype(BF16) if n in BF16_PAYLOAD else w[n][0]) for n in SHARDED}
    srcs, dsts, _ = _gather_plan({n: shards[n] for n in EARLY})
    full = _gather_finish(EARLY, _gather_two_level("early_all_gather", srcs, dsts))
    for n in REPLICATED:
        full[n] = w[n].reshape(1, -1)

    loss8, dx, g, pending, after = _local_step(x[0], mem[0], loss_target[0], _layout_weights(full),
                                               {n: shards[n] for n in SHARDED if n not in EARLY})
    loss = lax.psum(loss8[0, 0], ("x", "y", "c"))

    packed = _pack([g[n] for n in REPLICATED])
    small, _ = _exchange_start("small_gather_start", [(packed, None)], [((N_DEV,) + packed.shape, F32, None)], True,
                               after)
    grad, delta, new_m, new_v = {}, {}, {}, {}
    done = [dx]

    def tie():
        return jnp.broadcast_to(sum(t[:1, :1] for t in done), (8, LANES))

    for names in sorted(pending, key=lambda names: names == EARLY):
        handle = pending[names]
        for n, parts in zip(names, _exchange_wait("grad_wait_" + names[0], handle, tie())):
            res = _sum_adamw("adamw_" + n, parts, w[n][0], m[n][0], v[n][0])
            grad[n], delta[n], new_m[n], new_v[n] = [t[None] for t in res]
            done.append(res[1])

    (parts,) = _exchange_wait("small_gather_wait", small, tie())
    res = _sum_adamw("adamw_small", parts, _pack([w[n] for n in REPLICATED]),
                     _pack([m[n] for n in REPLICATED]), _pack([v[n] for n in REPLICATED]))
    shapes = [w[n].shape for n in REPLICATED]
    for store, packed_out in zip((grad, delta, new_m, new_v), res):
        for n, val in zip(REPLICATED, _unpack(packed_out, shapes)):
            store[n] = val

    return (loss, dx[None], *[grad[n] for n in WEIGHTS], *[delta[n] for n in WEIGHTS],
            *[new_m[n] for n in WEIGHTS], *[new_v[n] for n in WEIGHTS])
```

```python
import functools
import math

import jax
import jax.numpy as jnp
from jax import lax
from jax.experimental import pallas as pl
from jax.experimental.pallas import tpu as pltpu

F32 = jnp.float32
BF16 = jnp.bfloat16
SDS = jax.ShapeDtypeStruct

N_DEV = 8
D_MODEL = 2048
LANES = 128
CHUNK = 128
DN_HEADS = 8
DN_WIDTH = 1024
RW_WIDTH = 1024
RW_HEAD = 64
XA_HEADS = 4
XA_WIDTH = 512
FFN_HIDDEN = 8192
IN_COLS = 7440
DN_COLS = 4112
IN_PAD = 7680
RW_OFF = 4224
RMS_EPS = 1e-6
RW_GN_EPS = 64e-5
VMEM_LIMIT = 56 * 1024 * 1024

ADAM_LR = 0.001
ADAM_B1 = 0.9
ADAM_B2 = 0.999
ADAM_EPS = 1e-08
ADAM_WD = 0.01
ADAM_STEP = 10

_DIMS = {"nn": (((1,), (0,)), ((), ())), "nt": (((1,), (1,)), ((), ())), "tn": (((0,), (0,)), ((), ()))}


def _raw_dot(a, b, mode, hi):
    if hi:
        return lax.dot_general(a, b, _DIMS[mode], precision=lax.Precision.HIGHEST, preferred_element_type=F32)
    return lax.dot_general(a.astype(BF16), b.astype(BF16), _DIMS[mode], preferred_element_type=F32)


@functools.partial(jax.custom_vjp, nondiff_argnums=(2, 3))
def mm(a, b, mode="nn", hi=False):
    return _raw_dot(a, b, mode, hi)


def _mm_fwd(a, b, mode, hi):
    return _raw_dot(a, b, mode, hi), (a, b)


def _mm_bwd(mode, hi, res, g):
    a, b = res
    if mode == "nn":
        return _raw_dot(g, b, "nt", hi), _raw_dot(a, g, "tn", hi)
    if mode == "nt":
        return _raw_dot(g, b, "nn", hi), _raw_dot(g, a, "tn", hi)
    return _raw_dot(b, g, "nt", hi), _raw_dot(a, g, "nn", hi)


mm.defvjp(_mm_fwd, _mm_bwd)


def _shift_rows_raw(x, k):
    n = x.shape[0]
    rolled = pltpu.roll(x, k % n, axis=0)
    row = lax.broadcasted_iota(jnp.int32, x.shape, 0)
    keep = row >= k if k > 0 else row < n + k
    return jnp.where(keep, rolled, 0.0)


@functools.partial(jax.custom_vjp, nondiff_argnums=(1,))
def shift_rows(x, k):
    return _shift_rows_raw(x, k)


shift_rows.defvjp(lambda x, k: (_shift_rows_raw(x, k), None), lambda k, _, g: (_shift_rows_raw(g, -k),))


def _softplus(x):
    return jnp.maximum(x, 0.0) + jnp.log(1.0 + jnp.exp(-jnp.abs(x)))


def _sigmoid(x):
    return 1.0 / (1.0 + jnp.exp(-x))


def _silu(x):
    return x * _sigmoid(x)


def _tri_masks(n):
    ii = lax.broadcasted_iota(jnp.int32, (n, n), 0)
    jj = lax.broadcasted_iota(jnp.int32, (n, n), 1)
    return ii >= jj, ii > jj, ii == jj


def _neumann_inv_raw(m):
    n = m.shape[0]
    _, _, eye = _tri_masks(n)
    eye = jnp.where(eye, 1.0, 0.0)
    p = eye + m
    mk = m
    for _ in range(int(math.log2(n)) - 1):
        mk = _raw_dot(mk, mk, "nn", False)
        p = p + _raw_dot(p, mk, "nn", False)
    resid = eye - p + _raw_dot(m, p, "nn", True)
    return p + _raw_dot(p, resid, "nn", False)


@jax.custom_vjp
def _neumann_inv(m):
    return _neumann_inv_raw(m)


def _neumann_inv_fwd(m):
    p = _neumann_inv_raw(m)
    return p, p


def _neumann_inv_bwd(p, g):
    return (_raw_dot(_raw_dot(p, g, "tn", False), p, "nt", False),)


_neumann_inv.defvjp(_neumann_inv_fwd, _neumann_inv_bwd)


@jax.custom_vjp
def _saved_inv(m, p):
    return p


_saved_inv.defvjp(lambda m, p: (p, p), lambda p, g: (_neumann_inv_bwd(p, g)[0], jnp.zeros_like(p)))


def _inverse(m, saved):
    return _neumann_inv(m) if saved is None else _saved_inv(m, saved)


def _cumsum_rows(x):
    causal, _, _ = _tri_masks(x.shape[0])
    return mm(jnp.where(causal, 1.0, 0.0), x, "nn", True)


def _gdn_group(s0, q, k, v, gb, bb, gc, *saved):
    diff = jnp.stack([gc[j] - gc[j].T for j in range(gc.shape[0])])
    return jax.vmap(_gdn_chunk)(s0, q, k, v, gb, bb, gc, diff, *saved)


def _rw_group(*args):
    return jax.vmap(_rw_chunk)(*args)


def _gdn_chunk(s0, q, k, v, gb, bb, gc, diff, saved=None):
    c = q.shape[0]
    causal, strict, _ = _tri_masks(c)
    decay = jnp.exp(jnp.where(causal, diff, -jnp.inf))
    kb = k * bb
    a = jnp.where(strict, mm(kb, k, "nt") * decay, 0.0)
    p = _inverse(-a, saved)
    u = mm(p, v * bb)
    w = mm(p, kb * jnp.exp(gc))
    attn = mm(q, k, "nt") * decay
    v_new = u - mm(w, s0)
    o = mm(q * jnp.exp(gc), s0) + mm(attn, v_new)
    g_last = jnp.sum(gb, axis=0, keepdims=True)
    s1 = s0 * jnp.exp(g_last) + mm(k * jnp.exp(g_last - gc), v_new, "tn")
    return o, s1, p


def _rw_chunk(s0, r, lw, k, v, al, be, gc, saved0=None, saved1=None):
    c = r.shape[0]
    causal, strict, _ = _tri_masks(c)
    gp = gc - lw
    row = lax.broadcasted_iota(jnp.int32, lw.shape, 0)
    lane = lax.broadcasted_iota(jnp.int32, lw.shape, 1)
    g_mid = jnp.sum(jnp.where(row < c // 2, lw, 0.0), axis=0, keepdims=True)
    g_last = jnp.sum(lw, axis=0, keepdims=True)
    e_n = jnp.exp(g_mid - gc)
    rg = r * jnp.exp(gc - g_mid)
    bg = be * jnp.exp(gp - g_mid)
    an = al * e_n
    kn = k * e_n
    bt = mm(be * jnp.exp(gp), s0, "nt")
    rt = mm(r * jnp.exp(gc), s0, "nt")
    us, ys, ps = [], [], []
    for h, saved in enumerate((saved0, saved1)):
        mine = (lane >= RW_HEAD) if h else (lane < RW_HEAD)
        bgh = jnp.where(mine, bg, 0.0)
        rgh = jnp.where(mine, rg, 0.0)
        a_ab = jnp.where(strict, mm(bgh, an, "nt"), 0.0)
        a_kb = jnp.where(strict, mm(bgh, kn, "nt"), 0.0)
        a_ra = jnp.where(causal, mm(rgh, an, "nt"), 0.0)
        a_rk = jnp.where(causal, mm(rgh, kn, "nt"), 0.0)
        p = _inverse(a_ab, saved)
        ps.append(p)
        u_h = mm(p, bt + mm(a_kb, v))
        us.append(u_h)
        ys.append(rt + mm(a_ra, u_h) + mm(a_rk, v))
    lo = lane < RW_HEAD
    u = jnp.where(lo, us[0], us[1])
    y = jnp.where(lo, ys[0], ys[1])
    tail = jnp.exp(g_last - gc)
    s1 = s0 * jnp.exp(g_last) + mm(u, al * tail, "tn") + mm(v, k * tail, "tn")
    vi = lax.broadcasted_iota(jnp.int32, s0.shape, 0)
    ki = lax.broadcasted_iota(jnp.int32, s0.shape, 1)
    s1 = jnp.where((vi < RW_HEAD) == (ki < RW_HEAD), s1, 0.0)
    return y, s1, ps[0], ps[1]


SCAN_HB = 8


def _scan_specs(arrs, n_chunks, reverse):
    def spec(off):
        assert off % SCAN_HB == 0
        if reverse:
            return pl.BlockSpec((CHUNK, SCAN_HB * LANES), lambda h, n: (n_chunks - 1 - n, off // SCAN_HB + h))
        return pl.BlockSpec((CHUNK, SCAN_HB * LANES), lambda h, n: (n, off // SCAN_HB + h))
    return [spec(off) for _, off in arrs]


def _split_heads(x):
    return jnp.stack([x[:, LANES * j:LANES * (j + 1)] for j in range(SCAN_HB)], axis=0)


def _merge_heads(x):
    return jnp.concatenate([x[j] for j in range(SCAN_HB)], axis=1)


def _scan_fwd(group_fn, name, arrs, heads, n_kept):
    s = arrs[0][0].shape[0]
    n_chunks = s // CHUNK
    n_in = len(arrs)

    def body(*refs):
        y_ref, st_ref = refs[n_in:n_in + 2]
        kept_refs, s_scr = refs[n_in + 2:-1], refs[-1]

        @pl.when(pl.program_id(1) == 0)
        def _():
            s_scr[...] = jnp.zeros_like(s_scr)

        s0 = s_scr[...]
        st_ref[...] = s0
        y, s1, *kept = group_fn(s0, *[_split_heads(r[...]) for r in refs[:n_in]])
        y_ref[...] = _merge_heads(y)
        s_scr[...] = s1
        for ref, val in zip(kept_refs, kept):
            ref[...] = val

    per_chunk = pl.BlockSpec((SCAN_HB, None, LANES, LANES), lambda h, n: (h, n, 0, 0))
    res = pl.pallas_call(
        body, grid=(heads // SCAN_HB, n_chunks), name=name,
        in_specs=_scan_specs(arrs, n_chunks, False),
        out_specs=[pl.BlockSpec((CHUNK, SCAN_HB * LANES), lambda h, n: (n, h))] + [per_chunk] * (1 + n_kept),
        out_shape=[SDS((s, heads * LANES), F32)] + [SDS((heads, n_chunks, LANES, LANES), F32)] * (1 + n_kept),
        scratch_shapes=[pltpu.VMEM((SCAN_HB, LANES, LANES), F32)],
        compiler_params=pltpu.CompilerParams(dimension_semantics=("arbitrary", "arbitrary")),
    )(*[a for a, _ in arrs])
    return res[0], res[1:]


def _scan_bwd(group_fn, name, arrs, kept, dy, heads):
    s = arrs[0][0].shape[0]
    n_chunks = s // CHUNK
    n_in, n_kept = len(arrs), len(kept)

    def body(*refs):
        kept_vals = [r[...] for r in refs[n_in:n_in + n_kept]]
        dy_ref = refs[n_in + n_kept]
        d_refs = refs[n_in + n_kept + 1:2 * n_in + n_kept + 1]
        ds_scr = refs[-1]

        @pl.when(pl.program_id(1) == 0)
        def _():
            ds_scr[...] = jnp.zeros_like(ds_scr)

        def fn(s0, *ins):
            return group_fn(s0, *ins, *kept_vals[1:])[:2]

        _, vjp = jax.vjp(fn, kept_vals[0], *[_split_heads(r[...]) for r in refs[:n_in]])
        grads = vjp((_split_heads(dy_ref[...]), ds_scr[...]))
        ds_scr[...] = grads[0]
        for ref, g in zip(d_refs, grads[1:]):
            ref[...] = _merge_heads(g)

    rev = pl.BlockSpec((CHUNK, SCAN_HB * LANES), lambda h, n: (n_chunks - 1 - n, h))
    per_chunk = pl.BlockSpec((SCAN_HB, None, LANES, LANES), lambda h, n: (h, n_chunks - 1 - n, 0, 0))
    return pl.pallas_call(
        body, grid=(heads // SCAN_HB, n_chunks), name=name,
        in_specs=_scan_specs(arrs, n_chunks, True) + [per_chunk] * n_kept + [rev],
        out_specs=[rev] * n_in,
        out_shape=[SDS((s, heads * LANES), F32)] * n_in,
        scratch_shapes=[pltpu.VMEM((SCAN_HB, LANES, LANES), F32)],
        compiler_params=pltpu.CompilerParams(dimension_semantics=("arbitrary", "arbitrary")),
    )(*[a for a, _ in arrs], *kept, dy)


def _col_spec(tr, width, cb):
    return pl.BlockSpec((tr, width), lambda i: (i, cb))


def _whole(p):
    return pl.BlockSpec(p.shape, lambda i: (0,) * p.ndim)


def _row_fwd(fn, name, tiles, params, outs, tr):
    rows = tiles[0][0].shape[0]
    nt, npar = len(tiles), len(params)

    def body(*refs):
        vals = [r[...].astype(F32) for r in refs[:nt + npar]]
        for ref, o in zip(refs[nt + npar:], fn(*vals)):
            ref[...] = o.astype(ref.dtype)

    return pl.pallas_call(
        body, grid=(rows // tr,), name=name,
        in_specs=[_col_spec(tr, w, cb) for _, w, cb in tiles] + [_whole(p) for p in params],
        out_specs=[_col_spec(tr, w, 0) for w, _ in outs],
        out_shape=[SDS((rows, w), dt) for w, dt in outs],
        compiler_params=pltpu.CompilerParams(dimension_semantics=("arbitrary",), vmem_limit_bytes=VMEM_LIMIT),
    )(*[a for a, _, _ in tiles], *params)


def _row_bwd(fn, name, tiles, params, cts, tr, want_tiles=None):
    rows = tiles[0][0].shape[0]
    nt, npar = len(tiles), len(params)
    want = list(range(nt)) if want_tiles is None else list(want_tiles)
    flat_cts = [c for group in cts for c in group]
    n_ct = len(flat_cts)

    def body(*refs):
        vals = [r[...].astype(F32) for r in refs[:nt + npar]]
        ct_refs = refs[nt + npar:nt + npar + n_ct]
        out_refs = refs[nt + npar + n_ct:]
        ct_vals, at = [], 0
        for group in cts:
            total = ct_refs[at][...].astype(F32)
            for r in ct_refs[at + 1:at + len(group)]:
                total = total + r[...].astype(F32)
            ct_vals.append(total)
            at += len(group)
        _, vjp = jax.vjp(lambda *a: tuple(fn(*a)), *vals)
        grads = vjp(tuple(ct_vals))
        for ref, t in zip(out_refs[:len(want)], want):
            ref[...] = grads[t]
        first = pl.program_id(0) == 0
        for ref, g in zip(out_refs[len(want):], grads[nt:]):
            @pl.when(first)
            def _(ref=ref, g=g):
                ref[...] = g

            @pl.when(jnp.logical_not(first))
            def _(ref=ref, g=g):
                ref[...] += g

    res = pl.pallas_call(
        body, grid=(rows // tr,), name=name,
        in_specs=[_col_spec(tr, w, cb) for _, w, cb in tiles] + [_whole(p) for p in params]
        + [_col_spec(tr, w, cb) for _, w, cb in flat_cts],
        out_specs=[_col_spec(tr, tiles[t][1], 0) for t in want] + [_whole(p) for p in params],
        out_shape=[SDS((rows, tiles[t][1]), F32) for t in want] + [SDS(p.shape, F32) for p in params],
        compiler_params=pltpu.CompilerParams(dimension_semantics=("arbitrary",), vmem_limit_bytes=VMEM_LIMIT),
    )(*[a for a, _, _ in tiles], *params, *[a for a, _, _ in flat_cts])
    return res[:len(want)], res[len(want):]


def _col_fwd(fn, name, x, first_block, n_blocks, params):
    rows = x.shape[0]

    def body(*refs):
        refs[-1][...] = fn(*[r[...] for r in refs[:-1]])

    return pl.pallas_call(
        body, grid=(n_blocks,), name=name,
        in_specs=[pl.BlockSpec((rows, LANES), lambda j: (0, first_block + j))]
        + [pl.BlockSpec((p.shape[0], LANES), lambda j: (0, j)) for p in params],
        out_specs=pl.BlockSpec((rows, LANES), lambda j: (0, j)),
        out_shape=SDS((rows, n_blocks * LANES), F32),
        compiler_params=pltpu.CompilerParams(dimension_semantics=("arbitrary",), vmem_limit_bytes=VMEM_LIMIT),
    )(x, *params)


def _col_bwd(fn, name, x, first_block, n_blocks, params, dy):
    rows = x.shape[0]
    npar = len(params)

    def body(*refs):
        vals = [r[...] for r in refs[:1 + npar]]
        _, vjp = jax.vjp(fn, *vals)
        grads = vjp(refs[1 + npar][...])
        for ref, g in zip(refs[2 + npar:], grads):
            ref[...] = g

    pspecs = [pl.BlockSpec((p.shape[0], LANES), lambda j: (0, j)) for p in params]
    blk = pl.BlockSpec((rows, LANES), lambda j: (0, j))
    res = pl.pallas_call(
        body, grid=(n_blocks,), name=name,
        in_specs=[pl.BlockSpec((rows, LANES), lambda j: (0, first_block + j))] + pspecs + [blk],
        out_specs=[blk] + pspecs,
        out_shape=[SDS((rows, n_blocks * LANES), F32)] + [SDS(p.shape, F32) for p in params],
        compiler_params=pltpu.CompilerParams(dimension_semantics=("arbitrary",), vmem_limit_bytes=VMEM_LIMIT),
    )(x, *params, dy)
    return res[0], res[1:]


def _conv_fn(x, w):
    acc = x * w[3:4, :]
    for j in range(3):
        acc = acc + shift_rows(x, 3 - j) * w[j:j + 1, :]
    return _silu(acc)


def _lerp_fn(x, mu):
    return x + (shift_rows(x, 1) - x) * mu[0:1, :]


def _seg_sum(x, width):
    if width == LANES:
        return jnp.sum(x, axis=1, keepdims=True)
    lo = lax.broadcasted_iota(jnp.int32, x.shape, 1) < width
    s0 = jnp.sum(jnp.where(lo, x, 0.0), axis=1, keepdims=True)
    s1 = jnp.sum(jnp.where(lo, 0.0, x), axis=1, keepdims=True)
    return jnp.where(lo, s0, s1)


def _per_block(fn, *xs):
    n = xs[0].shape[1] // LANES
    return jnp.concatenate([fn(*[x[:, LANES * b:LANES * (b + 1)] for x in xs]) for b in range(n)], axis=1)


def _head_expand(col0):
    r = lax.broadcasted_iota(jnp.int32, (LANES, DN_WIDTH), 0)
    c = lax.shift_right_logical(lax.broadcasted_iota(jnp.int32, (LANES, DN_WIDTH), 1), 7)
    return jnp.where(r == c + col0, 1.0, 0.0)


def _dn_pre_fn(cq, ck, gates, a_log, dt_bias):
    l2 = lambda x: x * lax.rsqrt(_seg_sum(x * x, LANES) + 1e-6)
    qh = _per_block(l2, cq) * (LANES ** -0.5)
    kh = _per_block(l2, ck)
    g = -jnp.exp(a_log) * _softplus(gates + dt_bias)
    gb = mm(g, _head_expand(0), "nn", True)
    bb = mm(_sigmoid(gates), _head_expand(DN_HEADS), "nn", True)
    return qh, kh, gb, bb, _cumsum_rows(gb)


def _dn_post_fn(o, z, nw):
    def one(ob, zb):
        return ob * lax.rsqrt(_seg_sum(ob * ob, LANES) * (1.0 / LANES) + RMS_EPS) * nw * _silu(zb)
    return (_per_block(one, o, z),)


def _rw_pre_fn(pr, pk, pv, pwa, pg, w0, a0, k_k, k_a, w2p, a2p, g2):
    log_w = -_softplus(-(w0 + mm(jnp.tanh(pwa), w2p))) - 0.5
    lw = -jnp.exp(log_w)
    a = _sigmoid(a0 + mm(pwa, a2p))
    gate = mm(_sigmoid(pg), g2)
    kk = pk * k_k
    kk = _per_block(lambda x: x / jnp.maximum(jnp.sqrt(_seg_sum(x * x, RW_HEAD)), 1e-12), kk)
    k = pk * (1.0 + (a - 1.0) * k_a)
    return pr, lw, k, pv, kk * a, -kk, gate, _cumsum_rows(lw)


def _rw_post_fn(y, r, k, v, gate, ln_w, ln_b, r_k):
    def one(yb, rb, kb, vb, gb, wb, bb, rkb):
        d = yb - _seg_sum(yb, RW_HEAD) * (1.0 / RW_HEAD)
        var = _seg_sum(d * d, RW_HEAD) * (1.0 / RW_HEAD)
        yn = d * lax.rsqrt(var + RW_GN_EPS) * wb + bb
        return (yn + _seg_sum(rb * kb * rkb, RW_HEAD) * vb) * gb
    return (_per_block(one, y, r, k, v, gate, ln_w, ln_b, r_k),)


def _rms_fn(h, w):
    return (h * lax.rsqrt(jnp.mean(h * h, axis=1, keepdims=True) + RMS_EPS) * w,)


def _xattn_fn(q, k, v):
    outs = []
    for h in range(XA_HEADS):
        sl = slice(LANES * h, LANES * (h + 1))
        s = mm(q[:, sl], k[:, sl], "nt") * (LANES ** -0.5)
        e = jnp.exp(s - jnp.max(s, axis=1, keepdims=True))
        outs.append(mm(e / jnp.sum(e, axis=1, keepdims=True), v[:, sl]))
    return (jnp.concatenate(outs, axis=1),)


def _fit(tile, dim):
    best = [t for t in range(LANES, min(tile, dim) + 1, LANES) if dim % t == 0]
    assert best, (tile, dim)
    return best[-1]


def _matmul(name, a, b, mode, out_dtypes, epilogue=None, extras=(), tm=1024, tn=1024, tk=2048, after=None):
    if mode == "tn":
        (k_dim, m), n = a.shape, b.shape[1]
    else:
        (m, k_dim), n = a.shape, (b.shape[1] if mode == "nn" else b.shape[0])
    tm, tn, tk = _fit(tm, m), _fit(tn, n), _fit(tk, k_dim)
    nk = k_dim // tk
    a_spec = (pl.BlockSpec((tk, tm), lambda i, j, k: (k, i)) if mode == "tn"
              else pl.BlockSpec((tm, tk), lambda i, j, k: (i, k)))
    b_spec = (pl.BlockSpec((tn, tk), lambda i, j, k: (j, k)) if mode == "nt"
              else pl.BlockSpec((tk, tn), lambda i, j, k: (k, j)))
    o_spec = pl.BlockSpec((tm, tn), lambda i, j, k: (i, j))
    n_ex, n_out = len(extras), len(out_dtypes)
    ties = [] if after is None else [after]

    def finish(total, rest):
        ex = [r[...].astype(F32) for r in rest[:n_ex]]
        res = epilogue(total, *ex) if epilogue else (total,)
        for ref, o in zip(rest[n_ex + len(ties):n_ex + len(ties) + n_out], res):
            ref[...] = o.astype(ref.dtype)

    def body_single(a_ref, b_ref, *rest):
        finish(_raw_dot(a_ref[...], b_ref[...], mode, False), rest)

    def body_acc(a_ref, b_ref, *rest):
        acc = rest[-1]
        k = pl.program_id(2)

        @pl.when(k == 0)
        def _():
            acc[...] = jnp.zeros_like(acc)

        acc[...] += _raw_dot(a_ref[...], b_ref[...], mode, False)

        @pl.when(k == nk - 1)
        def _():
            finish(acc[...], rest)

    res = pl.pallas_call(
        body_single if nk == 1 else body_acc, grid=(m // tm, n // tn, nk), name=name,
        in_specs=[a_spec, b_spec] + [o_spec] * n_ex + [pl.BlockSpec((8, LANES), lambda i, j, k: (0, 0))] * len(ties),
        out_specs=[o_spec] * n_out,
        out_shape=[SDS((m, n), dt) for dt in out_dtypes],
        scratch_shapes=[] if nk == 1 else [pltpu.VMEM((tm, tn), F32)],
        compiler_params=pltpu.CompilerParams(dimension_semantics=("parallel", "parallel", "arbitrary"),
                                             vmem_limit_bytes=VMEM_LIMIT),
    )(a, b, *extras, *ties)
    return res


def _loss_call(h, target, w, tr=256):
    rows, d = h.shape

    def fn(hv, wv, tv):
        y = _rms_fn(hv, wv)[0]
        return 0.5 * jnp.sum(jnp.mean(jnp.square(y - tv), axis=1, keepdims=True), axis=0, keepdims=True)

    def body(h_ref, t_ref, w_ref, loss_ref, dh_ref, dw_ref):
        tv = t_ref[...]
        val, vjp = jax.vjp(lambda hv, wv: fn(hv, wv, tv), h_ref[...], w_ref[...])
        dh, dw = vjp(jnp.ones((1, 1), F32))
        dh_ref[...] = dh
        first = pl.program_id(0) == 0

        @pl.when(first)
        def _():
            loss_ref[...] = jnp.broadcast_to(val, loss_ref.shape)
            dw_ref[...] = dw

        @pl.when(jnp.logical_not(first))
        def _():
            loss_ref[...] += jnp.broadcast_to(val, loss_ref.shape)
            dw_ref[...] += dw

    return pl.pallas_call(
        body, grid=(rows // tr,), name="loss_head",
        in_specs=[_col_spec(tr, d, 0), _col_spec(tr, d, 0), _whole(w)],
        out_specs=[pl.BlockSpec((8, LANES), lambda i: (0, 0)), _col_spec(tr, d, 0), _whole(w)],
        out_shape=[SDS((8, LANES), F32), SDS((rows, d), F32), SDS(w.shape, F32)],
        compiler_params=pltpu.CompilerParams(dimension_semantics=("arbitrary",), vmem_limit_bytes=VMEM_LIMIT),
    )(h, target, w)


def _adamw_vals(w, g, m, v):
    m = ADAM_B1 * m + (1.0 - ADAM_B1) * g
    v = ADAM_B2 * v + (1.0 - ADAM_B2) * jnp.square(g)
    m_hat = m / (1.0 - ADAM_B1 ** ADAM_STEP)
    v_hat = v / (1.0 - ADAM_B2 ** ADAM_STEP)
    delta = -ADAM_LR * (m_hat / (jnp.sqrt(v_hat) + ADAM_EPS) + ADAM_WD * w)
    return delta, m, v


def _sum_adamw(name, parts, w, m, v):
    r, c = w.shape
    tr = r
    for cand in (512, 256, 128, 64, 32, 16, 8):
        if r % cand == 0 and N_DEV * cand * c * 4 <= 6 * 1024 * 1024:
            tr = cand
            break

    def body(p_ref, w_ref, m_ref, v_ref, g_ref, d_ref, m2_ref, v2_ref):
        g = p_ref[0].astype(F32)
        for s in range(1, N_DEV):
            g = g + p_ref[s].astype(F32)
        g_ref[...] = g
        d_ref[...], m2_ref[...], v2_ref[...] = _adamw_vals(w_ref[...], g, m_ref[...], v_ref[...])

    blk = pl.BlockSpec((tr, c), lambda i: (i, 0))
    return pl.pallas_call(
        body, grid=(r // tr,), name=name,
        in_specs=[pl.BlockSpec((N_DEV, tr, c), lambda i: (0, i, 0)), blk, blk, blk],
        out_specs=[blk] * 4, out_shape=[SDS((r, c), F32)] * 4,
        compiler_params=pltpu.CompilerParams(dimension_semantics=("arbitrary",), vmem_limit_bytes=VMEM_LIMIT),
    )(parts, w, m, v)


def _peers():
    x, y, c = lax.axis_index("x"), lax.axis_index("y"), lax.axis_index("c")
    peers = []
    for k in range(1, N_DEV):
        px = 1 - x if k & 4 else x
        py = 1 - y if k & 2 else y
        pc = 1 - c if k & 1 else c
        peers.append(((px, py, pc), 4 * px + 2 * py + pc))
    return 4 * x + 2 * y + c, peers


def _slot(ref, idx, cols):
    if cols is None:
        return ref.at[idx]
    return ref.at[:, pl.ds(pl.multiple_of(idx * cols, LANES), cols)]


def _exchange(name, srcs, dsts, gather):
    n = len(srcs)

    def body(*refs):
        start, wait = _exchange_ops([c for _, c in srcs], [c for _, _, c in dsts], gather,
                                    refs[:n], refs[n:2 * n], *refs[2 * n:])
        start()
        wait()

    any_spec = pl.BlockSpec(memory_space=pl.ANY)
    return pl.pallas_call(
        body, name=name,
        in_specs=[any_spec] * n, out_specs=[any_spec] * n,
        out_shape=[SDS(shape, dt) for shape, dt, _ in dsts],
        scratch_shapes=_exchange_sems(n),
    )(*[a for a, _ in srcs])


def _gather_two_level(name, srcs, dsts):
    n = len(srcs)
    dst_cols = [c for _, _, c in dsts]

    def body(*refs):
        src_refs, out_refs = refs[:n], refs[n:2 * n]
        send_sems, recv_sems, local_sems = refs[2 * n:]
        x, y, c = lax.axis_index("x"), lax.axis_index("y"), lax.axis_index("c")
        index = lambda px, py, pc: 4 * px + 2 * py + pc
        me, sibling = index(x, y, c), (x, y, 1 - c)
        chips = [(x, 1 - y), (1 - x, y), (1 - x, 1 - y)]

        def copy(a, k, src, block, to):
            return pltpu.make_async_remote_copy(
                src_ref=src, dst_ref=_slot(out_refs[a], block, dst_cols[a]),
                send_sem=send_sems.at[a, k], recv_sem=recv_sems.at[a, k],
                device_id=to, device_id_type=pl.DeviceIdType.MESH)

        local, first, passed = [], [], []
        for a in range(n):
            cp = pltpu.make_async_copy(src_refs[a], _slot(out_refs[a], me, dst_cols[a]), local_sems.at[a])
            cp.start()
            local.append(cp)
            first.append(copy(a, 0, src_refs[a], me, sibling))
            first += [copy(a, 1 + j, src_refs[a], me, (*chip, c)) for j, chip in enumerate(chips)]
        for cp in first:
            cp.start()
        for a in range(n):
            for j, chip in enumerate(chips):
                block = index(*chip, c)
                arrived = _slot(out_refs[a], block, dst_cols[a])
                copy(a, 1 + j, arrived, block, (*chip, c)).wait_recv()
                passed.append(copy(a, 4 + j, arrived, block, sibling))
                passed[-1].start()
        for a in range(n):
            copy(a, 0, src_refs[a], index(x, y, 1 - c), sibling).wait_recv()
            for j, chip in enumerate(chips):
                block = index(*chip, 1 - c)
                copy(a, 4 + j, src_refs[a], block, sibling).wait_recv()
        for cp in first + passed:
            cp.wait_send()
        for cp in local:
            cp.wait()

    any_spec = pl.BlockSpec(memory_space=pl.ANY)
    return pl.pallas_call(
        body, name=name,
        in_specs=[any_spec] * n, out_specs=[any_spec] * n,
        out_shape=[SDS(shape, dt) for shape, dt, _ in dsts],
        scratch_shapes=_exchange_sems(n),
    )(*[a for a, _ in srcs])


_HBM = pl.BlockSpec(memory_space=pltpu.HBM)
_SEM = pl.BlockSpec(memory_space=pltpu.SEMAPHORE)
_EFFECT = pltpu.SideEffectType.DATAFLOW_SIDE_EFFECTING


def _split_copies(src_cols, dst_cols, gather, src_refs, land_refs, send_sems, recv_sems, landings):
    me, peers = _peers()
    n = len(src_cols)
    remote, local = [], []
    for a, (s_cols, d_cols) in enumerate(zip(src_cols, dst_cols)):
        mine = src_refs[a] if gather else _slot(src_refs[a], me, s_cols)
        local.append(pltpu.make_async_copy(mine, _slot(land_refs[a], me, d_cols),
                                           send_sems.at[n * (N_DEV - 1) + a]))
        for k, (pos, idx) in enumerate(peers):
            blk = src_refs[a] if gather else _slot(src_refs[a], idx, s_cols)
            remote.append(pltpu.make_async_remote_copy(
                src_ref=blk, dst_ref=_slot(land_refs[a], idx if landings else me, d_cols),
                send_sem=send_sems.at[a * (N_DEV - 1) + k], recv_sem=recv_sems.at[a * (N_DEV - 1) + k],
                device_id=pos, device_id_type=pl.DeviceIdType.MESH))
    return remote, local


def _exchange_start(name, srcs, dsts, gather, after):
    n = len(srcs)
    src_cols, dst_cols = [c for _, c in srcs], [c for _, _, c in dsts]

    def body(*refs):
        src_refs, land_refs = refs[:n], refs[n:2 * n]
        send_sems, recv_sems = refs[2 * n + 1:2 * n + 3]
        token = refs[-1]
        remote, local = _split_copies(src_cols, dst_cols, gather, src_refs, land_refs, send_sems, recv_sems, False)
        for cp in remote + local:
            cp.start()
        token[...] = jnp.zeros_like(token)

    hbm = lambda a: pltpu.with_memory_space_constraint(a, pltpu.HBM)
    lands = [hbm(lax.empty(shape, dt)) for shape, dt, _ in dsts]
    res = pl.pallas_call(
        body, name=name,
        out_shape=(pltpu.SemaphoreType.DMA((n * N_DEV,)), pltpu.SemaphoreType.DMA((n * (N_DEV - 1),)),
                   *[pltpu.HBM(a.shape, a.dtype) for a, _ in srcs], *[pltpu.HBM(a.shape, a.dtype) for a in lands],
                   SDS((8, LANES), F32)),
        in_specs=[_HBM] * (2 * n) + [pl.BlockSpec(memory_space=pl.ANY)],
        out_specs=(_SEM, _SEM, *[_HBM] * (2 * n), pl.BlockSpec(memory_space=pltpu.VMEM)),
        input_output_aliases={i: 2 + i for i in range(2 * n)},
        compiler_params=pltpu.CompilerParams(has_side_effects=_EFFECT),
    )(*[hbm(a) for a, _ in srcs], *lands, after)
    handle = (res[0], res[1], res[2:2 + n], res[2 + n:2 + 2 * n], src_cols, dst_cols, gather)
    return handle, res[-1]


def _exchange_wait(name, handle, after):
    send_sems, recv_sems, src_thru, land_thru, src_cols, dst_cols, gather = handle
    n = len(src_thru)

    def body(*refs):
        src_refs, land_refs = refs[:n], refs[n:2 * n]
        s_sems, r_sems = refs[2 * n:2 * n + 2]
        remote, local = _split_copies(src_cols, dst_cols, gather, src_refs, land_refs, s_sems, r_sems, True)
        for cp in remote:
            cp.wait_send()
            cp.wait_recv()
        for cp in local:
            cp.wait()

    res = pl.pallas_call(
        body, name=name,
        out_shape=tuple(pltpu.HBM(a.shape, a.dtype) for a in (*src_thru, *land_thru)),
        in_specs=[_HBM] * (2 * n) + [_SEM, _SEM, pl.BlockSpec(memory_space=pl.ANY)],
        out_specs=tuple([_HBM] * (2 * n)),
        input_output_aliases={i: i for i in range(2 * n)},
        compiler_params=pltpu.CompilerParams(has_side_effects=_EFFECT),
    )(*src_thru, *land_thru, send_sems, recv_sems, after)
    return res[n:]


def _my_index():
    return 4 * lax.axis_index("x") + 2 * lax.axis_index("y") + lax.axis_index("c")


def _two_level_copies(stage, dst_cols, src_refs, land_refs, send_sems, recv_sems, landings):
    x, y, c = lax.axis_index("x"), lax.axis_index("y"), lax.axis_index("c")

    def pos(k):
        return (1 - x if k & 4 else x, 1 - y if k & 2 else y, 1 - c if k & 1 else c)

    def idx(k):
        px, py, pc = pos(k)
        return 4 * px + 2 * py + pc

    out = []
    for a, cols in enumerate(dst_cols):
        if stage == 1:
            for i, k in enumerate((1, 2, 4, 6)):
                out.append(pltpu.make_async_remote_copy(
                    src_ref=src_refs[a], dst_ref=_slot(land_refs[a], idx(k) if landings else idx(0), cols),
                    send_sem=send_sems.at[4 * a + i], recv_sem=recv_sems.at[4 * a + i],
                    device_id=pos(k), device_id_type=pl.DeviceIdType.MESH))
        else:
            for i, k in enumerate((2, 4, 6)):
                out.append(pltpu.make_async_remote_copy(
                    src_ref=_slot(land_refs[a], idx(k), cols),
                    dst_ref=_slot(land_refs[a], idx(k ^ 1) if landings else idx(k), cols),
                    send_sem=send_sems.at[3 * a + i], recv_sem=recv_sems.at[3 * a + i],
                    device_id=pos(1), device_id_type=pl.DeviceIdType.MESH))
    return out


def _gather2_start(name, srcs, dsts, after):
    n = len(srcs)
    dst_cols = [c for _, _, c in dsts]

    def body(*refs):
        src_refs, land_refs = refs[:n], refs[n:2 * n]
        send_sems, recv_sems = refs[2 * n + 1:2 * n + 3]
        me = _my_index()
        for a in range(n):
            pltpu.make_async_copy(src_refs[a], _slot(land_refs[a], me, dst_cols[a]), send_sems.at[4 * n + a]).start()
        for cp in _two_level_copies(1, dst_cols, src_refs, land_refs, send_sems, recv_sems, False):
            cp.start()
        refs[-1][...] = jnp.zeros_like(refs[-1])

    hbm = lambda a: pltpu.with_memory_space_constraint(a, pltpu.HBM)
    lands = [hbm(lax.empty(shape, dt)) for shape, dt, _ in dsts]
    res = pl.pallas_call(
        body, name=name,
        out_shape=(pltpu.SemaphoreType.DMA((5 * n,)), pltpu.SemaphoreType.DMA((4 * n,)),
                   *[pltpu.HBM(a.shape, a.dtype) for a, _ in srcs], *[pltpu.HBM(a.shape, a.dtype) for a in lands],
                   SDS((8, LANES), F32)),
        in_specs=[_HBM] * (2 * n) + [pl.BlockSpec(memory_space=pl.ANY)],
        out_specs=(_SEM, _SEM, *[_HBM] * (2 * n), pl.BlockSpec(memory_space=pltpu.VMEM)),
        input_output_aliases={i: 2 + i for i in range(2 * n)},
        compiler_params=pltpu.CompilerParams(has_side_effects=_EFFECT),
    )(*[hbm(a) for a, _ in srcs], *lands, after)
    return (res[0], res[1], res[2:2 + n], res[2 + n:2 + 2 * n], dst_cols), res[-1]


def _gather2_pass(name, handle, after):
    send1, recv1, src_thru, land_thru, dst_cols = handle
    n = len(src_thru)

    def body(*refs):
        src_refs, land_refs = refs[:n], refs[n:2 * n]
        s1, r1 = refs[2 * n:2 * n + 2]
        send2, recv2 = refs[2 * n + 3:2 * n + 5]
        me = _my_index()
        for cp in _two_level_copies(1, dst_cols, src_refs, land_refs, s1, r1, True):
            cp.wait_send()
            cp.wait_recv()
        for a in range(n):
            pltpu.make_async_copy(src_refs[a], _slot(land_refs[a], me, dst_cols[a]), s1.at[4 * n + a]).wait()
        for cp in _two_level_copies(2, dst_cols, src_refs, land_refs, send2, recv2, False):
            cp.start()
        refs[-1][...] = jnp.zeros_like(refs[-1])

    res = pl.pallas_call(
        body, name=name,
        out_shape=(pltpu.SemaphoreType.DMA((3 * n,)), pltpu.SemaphoreType.DMA((3 * n,)),
                   *[pltpu.HBM(a.shape, a.dtype) for a in (*src_thru, *land_thru)], SDS((8, LANES), F32)),
        in_specs=[_HBM] * (2 * n) + [_SEM, _SEM, pl.BlockSpec(memory_space=pl.ANY)],
        out_specs=(_SEM, _SEM, *[_HBM] * (2 * n), pl.BlockSpec(memory_space=pltpu.VMEM)),
        input_output_aliases={i: 2 + i for i in range(2 * n)},
        compiler_params=pltpu.CompilerParams(has_side_effects=_EFFECT),
    )(*src_thru, *land_thru, send1, recv1, after)
    return (res[0], res[1], res[2:2 + n], res[2 + n:2 + 2 * n], dst_cols), res[-1]


def _gather2_wait(name, handle, after):
    send2, recv2, src_thru, land_thru, dst_cols = handle
    n = len(src_thru)

    def body(*refs):
        src_refs, land_refs = refs[:n], refs[n:2 * n]
        s2, r2 = refs[2 * n:2 * n + 2]
        for cp in _two_level_copies(2, dst_cols, src_refs, land_refs, s2, r2, True):
            cp.wait_send()
            cp.wait_recv()

    res = pl.pallas_call(
        body, name=name,
        out_shape=tuple(pltpu.HBM(a.shape, a.dtype) for a in (*src_thru, *land_thru)),
        in_specs=[_HBM] * (2 * n) + [_SEM, _SEM, pl.BlockSpec(memory_space=pl.ANY)],
        out_specs=tuple([_HBM] * (2 * n)),
        input_output_aliases={i: i for i in range(2 * n)},
        compiler_params=pltpu.CompilerParams(has_side_effects=_EFFECT),
    )(*src_thru, *land_thru, send2, recv2, after)
    return res[n:]


def _exchange_sems(n):
    return [pltpu.SemaphoreType.DMA((n, N_DEV - 1)), pltpu.SemaphoreType.DMA((n, N_DEV - 1)),
            pltpu.SemaphoreType.DMA((n,))]


def _exchange_ops(src_cols, dst_cols, gather, src_refs, out_refs, send_sems, recv_sems, local_sems):
    def copies(with_landings):
        me, peers = _peers()
        local, sends, landings = [], [], []
        for a, (s_cols, d_cols) in enumerate(zip(src_cols, dst_cols)):
            mine = src_refs[a] if gather else _slot(src_refs[a], me, s_cols)
            local.append(pltpu.make_async_copy(mine, _slot(out_refs[a], me, d_cols), local_sems.at[a]))
            for k, (pos, idx) in enumerate(peers):
                out_blk = src_refs[a] if gather else _slot(src_refs[a], idx, s_cols)
                both = dict(src_ref=out_blk, send_sem=send_sems.at[a, k], recv_sem=recv_sems.at[a, k],
                            device_id=pos, device_id_type=pl.DeviceIdType.MESH)
                sends.append(pltpu.make_async_remote_copy(dst_ref=_slot(out_refs[a], me, d_cols), **both))
                if with_landings:
                    landings.append(pltpu.make_async_remote_copy(dst_ref=_slot(out_refs[a], idx, d_cols), **both))
        return local, sends, landings

    def start():
        local, sends, _ = copies(False)
        for cp in local + sends:
            cp.start()

    def wait():
        local, sends, landings = copies(True)
        for cp in landings:
            cp.wait_recv()
        for cp in sends:
            cp.wait_send()
        for cp in local:
            cp.wait()

    return start, wait


def _rms_res_fn(h, w):
    return _rms_fn(h, w)[0], h


def _add_epilogue(acc, res):
    return (acc + res,)


def _gather_plan(shards):
    srcs, dsts = [], []
    for n, sh in shards.items():
        r, c = sh.shape
        srcs.append((sh, None))
        if SHARDED[n] and c % LANES == 0:
            dsts.append(((r, N_DEV * c), sh.dtype, c))
        else:
            dsts.append(((N_DEV, r, c), sh.dtype, None))
    return srcs, dsts, True


def _w_in_segments():
    out = []
    for j in range(N_DEV):
        lo, hi = W_IN_SHARD * j, W_IN_SHARD * (j + 1)
        for a, b in ((lo, min(hi, DN_COLS)), (max(lo, DN_COLS), hi)):
            if a < b:
                out.append((j, a - lo, b - lo, a if a < DN_COLS else a + RW_OFF - DN_COLS))
    return out


def _w_in_to_padded(shards, tr=256):
    _, rows, _ = shards.shape

    def body(g_ref, o_ref):
        o_ref[...] = jnp.zeros_like(o_ref)
        for j, a, b, dst in _w_in_segments():
            o_ref[:, dst:dst + b - a] = g_ref[j, :, a:b]

    return pl.pallas_call(
        body, grid=(rows // tr,), name="w_in_to_padded",
        in_specs=[pl.BlockSpec((N_DEV, tr, W_IN_SHARD), lambda i: (0, i, 0))],
        out_specs=pl.BlockSpec((tr, IN_PAD), lambda i: (i, 0)),
        out_shape=SDS((rows, IN_PAD), shards.dtype),
        compiler_params=pltpu.CompilerParams(dimension_semantics=("arbitrary",), vmem_limit_bytes=VMEM_LIMIT),
    )(shards)


def _w_in_grad_to_shards(gw, tr=256):
    rows, _ = gw.shape

    def body(w_ref, o_ref):
        for j, a, b, dst in _w_in_segments():
            o_ref[j, :, a:b] = w_ref[:, dst:dst + b - a]

    return pl.pallas_call(
        body, grid=(rows // tr,), name="w_in_grad_to_shards",
        in_specs=[pl.BlockSpec((tr, IN_PAD), lambda i: (i, 0))],
        out_specs=pl.BlockSpec((N_DEV, tr, W_IN_SHARD), lambda i: (0, i, 0)),
        out_shape=SDS((N_DEV, rows, W_IN_SHARD), gw.dtype),
        compiler_params=pltpu.CompilerParams(dimension_semantics=("arbitrary",), vmem_limit_bytes=VMEM_LIMIT),
    )(gw)


def _gather_finish(names, outs):
    full = {}
    for n, arr in zip(names, outs):
        if n == "w_in":
            full[n] = _w_in_to_padded(arr)
        elif arr.ndim == 2:
            full[n] = arr
        elif SHARDED[n]:
            full[n] = arr.transpose(1, 0, 2).reshape(arr.shape[1], -1)
        else:
            full[n] = arr.reshape(-1, arr.shape[2])
    return full


def _scatter_plan(grads):
    srcs, dsts = [], []
    for n, gr in grads.items():
        if gr.ndim == 3:
            srcs.append((gr, None))
            dsts.append((gr.shape, gr.dtype, None))
            continue
        rows, cols = gr.shape
        if not SHARDED[n]:
            r, c = rows // N_DEV, cols
            srcs.append((gr.reshape(N_DEV, r, c), None))
        else:
            r, c = rows, cols // N_DEV
            if c % LANES == 0:
                srcs.append((gr, c))
            else:
                srcs.append((gr.reshape(r, N_DEV, c).transpose(1, 0, 2), None))
        dsts.append(((N_DEV, r, c), gr.dtype, None))
    return srcs, dsts, False


def _local_step(x, mem, target, wt, late):
    d = D_MODEL
    g = {}
    wt = dict(wt)
    grp_a = ("w_out", "xa_wq", "xa_wk", "xa_wv", "xa_wo")
    grp_b = ("ffn_w1", "ffn_w2")
    plan = lambda names: _gather_plan({n: late[n] for n in names})[:2]
    handle_a, tok_a = _gather2_start("late_gather_a_start", *plan(grp_a), wt["w_in"])
    handle_w1, tok_b = _gather2_start("late_gather_w1_start", *plan(("ffn_w1",)), tok_a)
    handle_w2, tok_c = _gather2_start("late_gather_w2_start", *plan(("ffn_w2",)), tok_b)
    mix_w = wt["mix_norm_w"] + (tok_a[0:1, 0:1] + tok_b[0:1, 0:1] + tok_c[0:1, 0:1])
    u = _row_fwd(_rms_fn, "mix_norm", [(x, d, 0)], [mix_w], [(d, BF16)], 256)[0]
    p = _matmul("in_proj", u, wt["w_in"], "nn", [F32], tn=1536)[0]
    c = _col_fwd(_conv_fn, "dn_conv", p, 0, 24, [wt["dn_conv_w"]])
    handle_a, tok = _gather2_pass("late_gather_a_pass", handle_a, c)
    dn_pre_tiles = [(c, DN_WIDTH, 0), (c, DN_WIDTH, 1), (p, LANES, 32)]
    dn_pre_params = [wt["dn_a_log"], wt["dn_dt_bias"]]
    qh, kh, gb, bb, gcb = _row_fwd(_dn_pre_fn, "dn_pre", dn_pre_tiles, [dn_pre_params[0] + tok[0:1, :], dn_pre_params[1]],
                                   [(DN_WIDTH, F32)] * 5, CHUNK)
    dn_arrs = [(qh, 0), (kh, 0), (c, 16), (gb, 0), (bb, 0), (gcb, 0)]
    o, kept_dn = _scan_fwd(_gdn_group, "gdn_scan", dn_arrs, DN_HEADS, 1)
    dn_post_tiles = [(o, DN_WIDTH, 0), (p, DN_WIDTH, 3)]
    o_dn = _row_fwd(_dn_post_fn, "dn_post", dn_post_tiles, [wt["dn_norm_w"]], [(DN_WIDTH, BF16)], 256)[0]

    ps = _col_fwd(_lerp_fn, "rw_shift", p, RW_OFF // LANES, 26, [wt["rw_mu"]])
    rw_pre_tiles = [(ps, RW_WIDTH, 0), (ps, RW_WIDTH, 1), (ps, RW_WIDTH, 2), (ps, LANES, 24), (ps, LANES, 25)]
    rw_pre_params = [wt[n] for n in ("rw_w0", "rw_a0", "rw_k_k", "rw_k_a", "rw_w2", "rw_a2", "rw_g2")]
    r, lw, k, v, al, be, gate, gcw = _row_fwd(_rw_pre_fn, "rw_pre", rw_pre_tiles, rw_pre_params,
                                              [(RW_WIDTH, F32)] * 8, CHUNK)
    rw_arrs = [(r, 0), (lw, 0), (k, 0), (v, 0), (al, 0), (be, 0), (gcw, 0)]
    y, kept_rw = _scan_fwd(_rw_group, "rw_scan", rw_arrs, RW_WIDTH // LANES, 2)
    handle_w1, tok = _gather2_pass("late_gather_w1_pass", handle_w1, y)
    rw_post_tiles = [(t, RW_WIDTH, 0) for t in (y, r, k, v, gate)]
    rw_post_params = [wt["rw_ln_w"], wt["rw_ln_b"], wt["rw_r_k"]]
    o_rw = _row_fwd(_rw_post_fn, "rw_post", rw_post_tiles, [rw_post_params[0] + tok[0:1, 0:1]] + rw_post_params[1:],
                    [(RW_WIDTH, BF16)], 128)[0]
    o_cat = jnp.concatenate([o_dn, o_rw], axis=1)
    wt.update(_gather_finish(grp_a, _gather2_wait("late_gather_a_wait", handle_a, o_cat)))
    h1 = _matmul("out_proj", o_cat, wt["w_out"], "nn", [F32], _add_epilogue, (x,))[0]

    handle_w2, tok = _gather2_pass("late_gather_w2_pass", handle_w2, h1)
    hn = _row_fwd(_rms_fn, "xa_norm", [(h1, d, 0)], [wt["xa_norm_w"] + tok[0:1, 0:1]], [(d, BF16)], 256)[0]
    mn = _row_fwd(_rms_fn, "mem_norm", [(mem, d, 0)], [wt["mem_norm_w"]], [(d, BF16)], 256)[0]
    q = _matmul("xa_q", hn, wt["xa_wq"], "nn", [F32])[0]
    kx = _matmul("xa_k", mn, wt["xa_wk"], "nn", [F32])[0]
    vx = _matmul("xa_v", mn, wt["xa_wv"], "nn", [F32])[0]
    ao = _row_fwd(_xattn_fn, "xattn", [(q, XA_WIDTH, 0)], [kx, vx], [(XA_WIDTH, BF16)], 256)[0]
    h2 = _matmul("xa_o", ao, wt["xa_wo"], "nn", [F32], _add_epilogue, (h1,))[0]

    f = _row_fwd(_rms_fn, "ffn_norm", [(h2, d, 0)], [wt["ffn_norm_w"]], [(d, BF16)], 256)[0]
    wt.update(_gather_finish(("ffn_w1",), _gather2_wait("late_gather_w1_wait", handle_w1, f)))
    a, hid = _matmul("ffn_up", f, wt["ffn_w1"], "nn", [F32, BF16],
                     lambda acc: (acc, jnp.square(jnp.maximum(acc, 0.0))))
    wt.update(_gather_finish(("ffn_w2",), _gather2_wait("late_gather_w2_wait", handle_w2, hid)))
    h3 = _matmul("ffn_down", hid, wt["ffn_w2"], "nn", [F32], _add_epilogue, (h2,))[0]
    loss8, dh3, g["final_norm_w"] = _loss_call(h3, target, wt["final_norm_w"])

    da = _matmul("ffn_down_dx", dh3, wt["ffn_w2"], "nt", [BF16],
                 lambda acc, av: (acc * 2.0 * jnp.maximum(av, 0.0),), (a,))[0]
    g["ffn_w2"] = _matmul("ffn_down_dw", hid, dh3, "tn", [BF16])[0]
    g["ffn_w1"] = _matmul("ffn_up_dw", f, da, "tn", [BF16])[0]
    df = _matmul("ffn_up_dx", da, wt["ffn_w1"], "nt", [F32])[0]
    pending = {}
    plan = _scatter_plan({n: g.pop(n) for n in grp_b})
    pending[grp_b], tok = _exchange_start("late_grad_b_start", *plan, loss8)
    (dh2,), (g["ffn_norm_w"],) = _row_bwd(_rms_res_fn, "ffn_norm_bwd", [(h2, d, 0)],
                                          [wt["ffn_norm_w"] + tok[0:1, 0:1]],
                                          [[(df, d, 0)], [(dh3, d, 0)]], 256)

    dao = _matmul("xa_o_dx", dh2, wt["xa_wo"], "nt", [F32])[0]
    g["xa_wo"] = _matmul("xa_o_dw", ao, dh2, "tn", [BF16])[0]
    (dq,), (dkx, dvx) = _row_bwd(_xattn_fn, "xattn_bwd", [(q, XA_WIDTH, 0)], [kx, vx], [[(dao, XA_WIDTH, 0)]], 256)
    dhn = _matmul("xa_q_dx", dq, wt["xa_wq"], "nt", [F32])[0]
    g["xa_wq"] = _matmul("xa_q_dw", hn, dq, "tn", [BF16])[0]
    g["xa_wk"] = _matmul("xa_k_dw", mn, dkx, "tn", [BF16])[0]
    g["xa_wv"] = _matmul("xa_v_dw", mn, dvx, "tn", [BF16])[0]
    dmn = _matmul("xa_k_dx", dkx, wt["xa_wk"], "nt", [F32])[0]
    dmn = _matmul("xa_v_dx", dvx, wt["xa_wv"], "nt", [F32], _add_epilogue, (dmn,))[0]
    _, (g["mem_norm_w"],) = _row_bwd(_rms_fn, "mem_norm_bwd", [(mem, d, 0)], [wt["mem_norm_w"]],
                                     [[(dmn, d, 0)]], 256, want_tiles=())
    (dh1,), (g["xa_norm_w"],) = _row_bwd(_rms_res_fn, "xa_norm_bwd", [(h1, d, 0)], [wt["xa_norm_w"]],
                                         [[(dhn, d, 0)], [(dh2, d, 0)]], 256)

    do_cat = _matmul("out_proj_dx", dh1, wt["w_out"], "nt", [F32])[0]
    g["w_out"] = _matmul("out_proj_dw", o_cat, dh1, "tn", [BF16])[0]

    plan = _scatter_plan({n: g.pop(n) for n in grp_a})
    pending[grp_a], tok = _exchange_start("late_grad_a_start", *plan, tok)
    (dy, dr1, dk1, dv1, dgate), (g["rw_ln_w"], g["rw_ln_b"], g["rw_r_k"]) = _row_bwd(
        _rw_post_fn, "rw_post_bwd", rw_post_tiles, [rw_post_params[0] + tok[0:1, 0:1]] + rw_post_params[1:],
        [[(do_cat, RW_WIDTH, 1)]], 128)
    dr2, dlw, dk2, dv2, dal, dbe, dgcw = _scan_bwd(_rw_group, "rw_scan_bwd", rw_arrs, kept_rw, dy,
                                                   RW_WIDTH // LANES)
    one = lambda t: [(t, RW_WIDTH, 0)]
    two = lambda s, t: [(s, RW_WIDTH, 0), (t, RW_WIDTH, 0)]
    d_ps, rw_pre_grads = _row_bwd(
        _rw_pre_fn, "rw_pre_bwd", rw_pre_tiles, rw_pre_params,
        [two(dr1, dr2), one(dlw), two(dk1, dk2), two(dv1, dv2), one(dal), one(dbe), one(dgate), one(dgcw)],
        CHUNK)
    for n, val in zip(("rw_w0", "rw_a0", "rw_k_k", "rw_k_a", "rw_w2", "rw_a2", "rw_g2"), rw_pre_grads):
        g[n] = val
    dp_rw, (g["rw_mu"],) = _col_bwd(_lerp_fn, "rw_shift_bwd", p, RW_OFF // LANES, 26, [wt["rw_mu"]],
                                    jnp.concatenate(d_ps, axis=1))

    (do, dz), (g["dn_norm_w"],) = _row_bwd(_dn_post_fn, "dn_post_bwd", dn_post_tiles, [wt["dn_norm_w"]],
                                           [[(do_cat, DN_WIDTH, 0)]], 256)
    dqh, dkh, dv_dn, dgb, dbb, dgcb = _scan_bwd(_gdn_group, "gdn_scan_bwd", dn_arrs, kept_dn, do, DN_HEADS)
    one = lambda t: [(t, DN_WIDTH, 0)]
    (dcq, dck, dgates), (g["dn_a_log"], g["dn_dt_bias"]) = _row_bwd(
        _dn_pre_fn, "dn_pre_bwd", dn_pre_tiles, dn_pre_params,
        [one(dqh), one(dkh), one(dgb), one(dbb), one(dgcb)], CHUNK)
    dp_qkv, (g["dn_conv_w"],) = _col_bwd(_conv_fn, "dn_conv_bwd", p, 0, 24, [wt["dn_conv_w"]],
                                         jnp.concatenate([dcq, dck, dv_dn], axis=1))
    dp = jnp.concatenate([dp_qkv, dz, dgates, dp_rw, jnp.zeros((x.shape[0], LANES), F32)], axis=1).astype(BF16)
    g["w_in"] = _matmul("in_proj_dw", u, dp, "tn", [BF16], tn=1536)[0]
    early = _logical_grads(g)
    pending[EARLY], tok = _exchange_start("early_grad_start", *_scatter_plan({n: early.pop(n) for n in EARLY}), tok)
    du = _matmul("in_proj_dx", dp, wt["w_in"], "nt", [F32], after=tok)[0]
    (dx,), (early["mix_norm_w"],) = _row_bwd(_rms_res_fn, "mix_norm_bwd", [(x, d, 0)], [wt["mix_norm_w"]],
                                             [[(du, d, 0)], [(dh1, d, 0)]], 256)
    return loss8, dx, early, pending, tok


WEIGHTS = ["mix_norm_w", "w_in", "dn_conv_w", "dn_a_log", "dn_dt_bias", "dn_norm_w", "rw_mu", "rw_w0", "rw_w2",
           "rw_a0", "rw_a2", "rw_g2", "rw_k_k", "rw_k_a", "rw_r_k", "rw_ln_w", "rw_ln_b", "w_out", "xa_norm_w",
           "mem_norm_w", "xa_wq", "xa_wk", "xa_wv", "xa_wo", "ffn_norm_w", "ffn_w1", "ffn_w2", "final_norm_w"]
SHARDED = {"w_in": True, "w_out": False, "xa_wq": False, "xa_wk": False, "xa_wv": False, "xa_wo": True,
           "ffn_w1": True, "ffn_w2": False, "dn_conv_w": True, "rw_w2": True, "rw_a2": True, "rw_g2": True}
BF16_PAYLOAD = ("w_in", "w_out", "xa_wq", "xa_wk", "xa_wv", "xa_wo", "ffn_w1", "ffn_w2")
REPLICATED = [n for n in WEIGHTS if n not in SHARDED]
EARLY = ("w_in", "dn_conv_w", "rw_w2", "rw_a2", "rw_g2")
RW_IN_COLS = IN_COLS - DN_COLS
W_IN_SHARD = IN_COLS // N_DEV


def _layout_weights(fw):
    wt = dict(fw)
    wt["dn_conv_w"] = jnp.pad(fw["dn_conv_w"], ((0, 4), (0, 0)))
    wt["dn_a_log"] = jnp.pad(fw["dn_a_log"], ((0, 0), (0, LANES - DN_HEADS)))
    wt["dn_dt_bias"] = jnp.pad(fw["dn_dt_bias"], ((0, 0), (0, LANES - DN_HEADS)))
    wt["rw_w2"] = jnp.pad(fw["rw_w2"], ((0, 64), (0, 0)))
    wt["rw_a2"] = jnp.pad(fw["rw_a2"], ((64, 0), (0, 0)))
    return wt


def _logical_grads(g):
    out = dict(g)
    out["w_in"] = _w_in_grad_to_shards(g["w_in"])
    out["dn_conv_w"] = g["dn_conv_w"][:4]
    out["dn_a_log"] = g["dn_a_log"][:, :DN_HEADS]
    out["dn_dt_bias"] = g["dn_dt_bias"][:, :DN_HEADS]
    out["rw_w2"] = g["rw_w2"][:64]
    out["rw_a2"] = g["rw_a2"][64:]
    return out


def _pack(vals):
    parts = []
    for v in vals:
        flat = v.reshape(-1)
        parts.append(jnp.pad(flat, (0, -flat.shape[0] % LANES)))
    flat = jnp.concatenate(parts)
    flat = jnp.pad(flat, (0, -flat.shape[0] % (8 * LANES)))
    return flat.reshape(-1, LANES)


def _unpack(packed, shapes):
    flat = packed.reshape(-1)
    out, at = [], 0
    for shp in shapes:
        size = math.prod(shp)
        out.append(flat[at:at + size].reshape(shp))
        at += size + (-size % LANES)
    return out


def kernel(x, mem, mix_norm_w, w_in, dn_conv_w, dn_a_log, dn_dt_bias, dn_norm_w, rw_mu, rw_w0, rw_w2, rw_a0, rw_a2, rw_g2, rw_k_k, rw_k_a, rw_r_k, rw_ln_w, rw_ln_b, w_out, xa_norm_w, mem_norm_w, xa_wq, xa_wk, xa_wv, xa_wo, ffn_norm_w, ffn_w1, ffn_w2, final_norm_w, loss_target, m_mix_norm_w, m_w_in, m_dn_conv_w, m_dn_a_log, m_dn_dt_bias, m_dn_norm_w, m_rw_mu, m_rw_w0, m_rw_w2, m_rw_a0, m_rw_a2, m_rw_g2, m_rw_k_k, m_rw_k_a, m_rw_r_k, m_rw_ln_w, m_rw_ln_b, m_w_out, m_xa_norm_w, m_mem_norm_w, m_xa_wq, m_xa_wk, m_xa_wv, m_xa_wo, m_ffn_norm_w, m_ffn_w1, m_ffn_w2, m_final_norm_w, v_mix_norm_w, v_w_in, v_dn_conv_w, v_dn_a_log, v_dn_dt_bias, v_dn_norm_w, v_rw_mu, v_rw_w0, v_rw_w2, v_rw_a0, v_rw_a2, v_rw_g2, v_rw_k_k, v_rw_k_a, v_rw_r_k, v_rw_ln_w, v_rw_ln_b, v_w_out, v_xa_norm_w, v_mem_norm_w, v_xa_wq, v_xa_wk, v_xa_wv, v_xa_wo, v_ffn_norm_w, v_ffn_w1, v_ffn_w2, v_final_norm_w):
    given = dict(locals())
    w = {n: given[n] for n in WEIGHTS}
    m = {n: given["m_" + n] for n in WEIGHTS}
    v = {n: given["v_" + n] for n in WEIGHTS}

    shards = {n: (w[n][0].astype(BF16) if n in BF16_PAYLOAD else w[n][0]) for n in SHARDED}
    srcs, dsts, _ = _gather_plan({n: shards[n] for n in EARLY})
    full = _gather_finish(EARLY, _gather_two_level("early_all_gather", srcs, dsts))
    for n in REPLICATED:
        full[n] = w[n].reshape(1, -1)

    loss8, dx, g, pending, after = _local_step(x[0], mem[0], loss_target[0], _layout_weights(full),
                                               {n: shards[n] for n in SHARDED if n not in EARLY})
    loss = lax.psum(loss8[0, 0], ("x", "y", "c"))

    packed = _pack([g[n] for n in REPLICATED])
    small, _ = _exchange_start("small_gather_start", [(packed, None)], [((N_DEV,) + packed.shape, F32, None)], True,
                               after)
    grad, delta, new_m, new_v = {}, {}, {}, {}
    done = [dx]

    def tie():
        return jnp.broadcast_to(sum(t[:1, :1] for t in done), (8, LANES))

    for names in sorted(pending, key=lambda names: names == EARLY):
        handle = pending[names]
        for n, parts in zip(names, _exchange_wait("grad_wait_" + names[0], handle, tie())):
            res = _sum_adamw("adamw_" + n, parts, w[n][0], m[n][0], v[n][0])
            grad[n], delta[n], new_m[n], new_v[n] = [t[None] for t in res]
            done.append(res[1])

    (parts,) = _exchange_wait("small_gather_wait", small, tie())
    res = _sum_adamw("adamw_small", parts, _pack([w[n] for n in REPLICATED]),
                     _pack([m[n] for n in REPLICATED]), _pack([v[n] for n in REPLICATED]))
    shapes = [w[n].shape for n in REPLICATED]
    for store, packed_out in zip((grad, delta, new_m, new_v), res):
        for n, val in zip(REPLICATED, _unpack(packed_out, shapes)):
            store[n] = val

    return (loss, dx[None], *[grad[n] for n in WEIGHTS], *[delta[n] for n in WEIGHTS],
            *[new_m[n] for n in WEIGHTS], *[new_v[n] for n in WEIGHTS])
```

```python
import functools
import math

import jax
import jax.numpy as jnp
from jax import lax
from jax.experimental import pallas as pl
from jax.experimental.pallas import tpu as pltpu

F32 = jnp.float32
BF16 = jnp.bfloat16
SDS = jax.ShapeDtypeStruct

N_DEV = 8
D_MODEL = 2048
LANES = 128
CHUNK = 128
DN_HEADS = 8
DN_WIDTH = 1024
RW_WIDTH = 1024
RW_HEAD = 64
XA_HEADS = 4
XA_WIDTH = 512
FFN_HIDDEN = 8192
IN_COLS = 7440
DN_COLS = 4112
IN_PAD = 7680
RW_OFF = 4224
RMS_EPS = 1e-6
RW_GN_EPS = 64e-5
VMEM_LIMIT = 56 * 1024 * 1024

ADAM_LR = 0.001
ADAM_B1 = 0.9
ADAM_B2 = 0.999
ADAM_EPS = 1e-08
ADAM_WD = 0.01
ADAM_STEP = 10

_DIMS = {"nn": (((1,), (0,)), ((), ())), "nt": (((1,), (1,)), ((), ())), "tn": (((0,), (0,)), ((), ()))}


def _raw_dot(a, b, mode, hi):
    if hi:
        return lax.dot_general(a, b, _DIMS[mode], precision=lax.Precision.HIGHEST, preferred_element_type=F32)
    return lax.dot_general(a.astype(BF16), b.astype(BF16), _DIMS[mode], preferred_element_type=F32)


@functools.partial(jax.custom_vjp, nondiff_argnums=(2, 3))
def mm(a, b, mode="nn", hi=False):
    return _raw_dot(a, b, mode, hi)


def _mm_fwd(a, b, mode, hi):
    return _raw_dot(a, b, mode, hi), (a, b)


def _mm_bwd(mode, hi, res, g):
    a, b = res
    if mode == "nn":
        return _raw_dot(g, b, "nt", hi), _raw_dot(a, g, "tn", hi)
    if mode == "nt":
        return _raw_dot(g, b, "nn", hi), _raw_dot(g, a, "tn", hi)
    return _raw_dot(b, g, "nt", hi), _raw_dot(a, g, "nn", hi)


mm.defvjp(_mm_fwd, _mm_bwd)


def _shift_rows_raw(x, k):
    n = x.shape[0]
    rolled = pltpu.roll(x, k % n, axis=0)
    row = lax.broadcasted_iota(jnp.int32, x.shape, 0)
    keep = row >= k if k > 0 else row < n + k
    return jnp.where(keep, rolled, 0.0)


@functools.partial(jax.custom_vjp, nondiff_argnums=(1,))
def shift_rows(x, k):
    return _shift_rows_raw(x, k)


shift_rows.defvjp(lambda x, k: (_shift_rows_raw(x, k), None), lambda k, _, g: (_shift_rows_raw(g, -k),))


def _softplus(x):
    return jnp.maximum(x, 0.0) + jnp.log(1.0 + jnp.exp(-jnp.abs(x)))


def _sigmoid(x):
    return 1.0 / (1.0 + jnp.exp(-x))


def _silu(x):
    return x * _sigmoid(x)


def _tri_masks(n):
    ii = lax.broadcasted_iota(jnp.int32, (n, n), 0)
    jj = lax.broadcasted_iota(jnp.int32, (n, n), 1)
    return ii >= jj, ii > jj, ii == jj


def _neumann_inv_raw(m):
    n = m.shape[0]
    _, _, eye = _tri_masks(n)
    eye = jnp.where(eye, 1.0, 0.0)
    p = eye + m
    mk = m
    for _ in range(int(math.log2(n)) - 1):
        mk = _raw_dot(mk, mk, "nn", False)
        p = p + _raw_dot(p, mk, "nn", False)
    resid = eye - p + _raw_dot(m, p, "nn", True)
    return p + _raw_dot(p, resid, "nn", False)


@jax.custom_vjp
def _neumann_inv(m):
    return _neumann_inv_raw(m)


def _neumann_inv_fwd(m):
    p = _neumann_inv_raw(m)
    return p, p


def _neumann_inv_bwd(p, g):
    return (_raw_dot(_raw_dot(p, g, "tn", False), p, "nt", False),)


_neumann_inv.defvjp(_neumann_inv_fwd, _neumann_inv_bwd)


@jax.custom_vjp
def _saved_inv(m, p):
    return p


_saved_inv.defvjp(lambda m, p: (p, p), lambda p, g: (_neumann_inv_bwd(p, g)[0], jnp.zeros_like(p)))


def _inverse(m, saved):
    return _neumann_inv(m) if saved is None else _saved_inv(m, saved)


def _cumsum_rows(x):
    causal, _, _ = _tri_masks(x.shape[0])
    return mm(jnp.where(causal, 1.0, 0.0), x, "nn", True)


def _gdn_group(s0, q, k, v, gb, bb, gc, *saved):
    diff = jnp.stack([gc[j] - gc[j].T for j in range(gc.shape[0])])
    return jax.vmap(_gdn_chunk)(s0, q, k, v, gb, bb, gc, diff, *saved)


def _rw_group(*args):
    return jax.vmap(_rw_chunk)(*args)


def _gdn_chunk(s0, q, k, v, gb, bb, gc, diff, saved=None):
    c = q.shape[0]
    causal, strict, _ = _tri_masks(c)
    decay = jnp.exp(jnp.where(causal, diff, -jnp.inf))
    kb = k * bb
    a = jnp.where(strict, mm(kb, k, "nt") * decay, 0.0)
    p = _inverse(-a, saved)
    u = mm(p, v * bb)
    w = mm(p, kb * jnp.exp(gc))
    attn = mm(q, k, "nt") * decay
    v_new = u - mm(w, s0)
    o = mm(q * jnp.exp(gc), s0) + mm(attn, v_new)
    g_last = jnp.sum(gb, axis=0, keepdims=True)
    s1 = s0 * jnp.exp(g_last) + mm(k * jnp.exp(g_last - gc), v_new, "tn")
    return o, s1, p


def _rw_chunk(s0, r, lw, k, v, al, be, gc, saved0=None, saved1=None):
    c = r.shape[0]
    causal, strict, _ = _tri_masks(c)
    gp = gc - lw
    row = lax.broadcasted_iota(jnp.int32, lw.shape, 0)
    lane = lax.broadcasted_iota(jnp.int32, lw.shape, 1)
    g_mid = jnp.sum(jnp.where(row < c // 2, lw, 0.0), axis=0, keepdims=True)
    g_last = jnp.sum(lw, axis=0, keepdims=True)
    e_n = jnp.exp(g_mid - gc)
    rg = r * jnp.exp(gc - g_mid)
    bg = be * jnp.exp(gp - g_mid)
    an = al * e_n
    kn = k * e_n
    bt = mm(be * jnp.exp(gp), s0, "nt")
    rt = mm(r * jnp.exp(gc), s0, "nt")
    us, ys, ps = [], [], []
    for h, saved in enumerate((saved0, saved1)):
        mine = (lane >= RW_HEAD) if h else (lane < RW_HEAD)
        bgh = jnp.where(mine, bg, 0.0)
        rgh = jnp.where(mine, rg, 0.0)
        a_ab = jnp.where(strict, mm(bgh, an, "nt"), 0.0)
        a_kb = jnp.where(strict, mm(bgh, kn, "nt"), 0.0)
        a_ra = jnp.where(causal, mm(rgh, an, "nt"), 0.0)
        a_rk = jnp.where(causal, mm(rgh, kn, "nt"), 0.0)
        p = _inverse(a_ab, saved)
        ps.append(p)
        u_h = mm(p, bt + mm(a_kb, v))
        us.append(u_h)
        ys.append(rt + mm(a_ra, u_h) + mm(a_rk, v))
    lo = lane < RW_HEAD
    u = jnp.where(lo, us[0], us[1])
    y = jnp.where(lo, ys[0], ys[1])
    tail = jnp.exp(g_last - gc)
    s1 = s0 * jnp.exp(g_last) + mm(u, al * tail, "tn") + mm(v, k * tail, "tn")
    vi = lax.broadcasted_iota(jnp.int32, s0.shape, 0)
    ki = lax.broadcasted_iota(jnp.int32, s0.shape, 1)
    s1 = jnp.where((vi < RW_HEAD) == (ki < RW_HEAD), s1, 0.0)
    return y, s1, ps[0], ps[1]


SCAN_HB = 8


def _scan_specs(arrs, n_chunks, reverse):
    def spec(off):
        assert off % SCAN_HB == 0
        if reverse:
            return pl.BlockSpec((CHUNK, SCAN_HB * LANES), lambda h, n: (n_chunks - 1 - n, off // SCAN_HB + h))
        return pl.BlockSpec((CHUNK, SCAN_HB * LANES), lambda h, n: (n, off // SCAN_HB + h))
    return [spec(off) for _, off in arrs]


def _split_heads(x):
    return jnp.stack([x[:, LANES * j:LANES * (j + 1)] for j in range(SCAN_HB)], axis=0)


def _merge_heads(x):
    return jnp.concatenate([x[j] for j in range(SCAN_HB)], axis=1)


def _scan_fwd(group_fn, name, arrs, heads, n_kept):
    s = arrs[0][0].shape[0]
    n_chunks = s // CHUNK
    n_in = len(arrs)

    def body(*refs):
        y_ref, st_ref = refs[n_in:n_in + 2]
        kept_refs, s_scr = refs[n_in + 2:-1], refs[-1]

        @pl.when(pl.program_id(1) == 0)
        def _():
            s_scr[...] = jnp.zeros_like(s_scr)

        s0 = s_scr[...]
        st_ref[...] = s0
        y, s1, *kept = group_fn(s0, *[_split_heads(r[...]) for r in refs[:n_in]])
        y_ref[...] = _merge_heads(y)
        s_scr[...] = s1
        for ref, val in zip(kept_refs, kept):
            ref[...] = val

    per_chunk = pl.BlockSpec((SCAN_HB, None, LANES, LANES), lambda h, n: (h, n, 0, 0))
    res = pl.pallas_call(
        body, grid=(heads // SCAN_HB, n_chunks), name=name,
        in_specs=_scan_specs(arrs, n_chunks, False),
        out_specs=[pl.BlockSpec((CHUNK, SCAN_HB * LANES), lambda h, n: (n, h))] + [per_chunk] * (1 + n_kept),
        out_shape=[SDS((s, heads * LANES), F32)] + [SDS((heads, n_chunks, LANES, LANES), F32)] * (1 + n_kept),
        scratch_shapes=[pltpu.VMEM((SCAN_HB, LANES, LANES), F32)],
        compiler_params=pltpu.CompilerParams(dimension_semantics=("arbitrary", "arbitrary")),
    )(*[a for a, _ in arrs])
    return res[0], res[1:]


def _scan_bwd(group_fn, name, arrs, kept, dy, heads):
    s = arrs[0][0].shape[0]
    n_chunks = s // CHUNK
    n_in, n_kept = len(arrs), len(kept)

    def body(*refs):
        kept_vals = [r[...] for r in refs[n_in:n_in + n_kept]]
        dy_ref = refs[n_in + n_kept]
        d_refs = refs[n_in + n_kept + 1:2 * n_in + n_kept + 1]
        ds_scr = refs[-1]

        @pl.when(pl.program_id(1) == 0)
        def _():
            ds_scr[...] = jnp.zeros_like(ds_scr)

        def fn(s0, *ins):
            return group_fn(s0, *ins, *kept_vals[1:])[:2]

        _, vjp = jax.vjp(fn, kept_vals[0], *[_split_heads(r[...]) for r in refs[:n_in]])
        grads = vjp((_split_heads(dy_ref[...]), ds_scr[...]))
        ds_scr[...] = grads[0]
        for ref, g in zip(d_refs, grads[1:]):
            ref[...] = _merge_heads(g)

    rev = pl.BlockSpec((CHUNK, SCAN_HB * LANES), lambda h, n: (n_chunks - 1 - n, h))
    per_chunk = pl.BlockSpec((SCAN_HB, None, LANES, LANES), lambda h, n: (h, n_chunks - 1 - n, 0, 0))
    return pl.pallas_call(
        body, grid=(heads // SCAN_HB, n_chunks), name=name,
        in_specs=_scan_specs(arrs, n_chunks, True) + [per_chunk] * n_kept + [rev],
        out_specs=[rev] * n_in,
        out_shape=[SDS((s, heads * LANES), F32)] * n_in,
        scratch_shapes=[pltpu.VMEM((SCAN_HB, LANES, LANES), F32)],
        compiler_params=pltpu.CompilerParams(dimension_semantics=("arbitrary", "arbitrary")),
    )(*[a for a, _ in arrs], *kept, dy)


def _col_spec(tr, width, cb):
    return pl.BlockSpec((tr, width), lambda i: (i, cb))


def _whole(p):
    return pl.BlockSpec(p.shape, lambda i: (0,) * p.ndim)


def _row_fwd(fn, name, tiles, params, outs, tr):
    rows = tiles[0][0].shape[0]
    nt, npar = len(tiles), len(params)

    def body(*refs):
        vals = [r[...].astype(F32) for r in refs[:nt + npar]]
        for ref, o in zip(refs[nt + npar:], fn(*vals)):
            ref[...] = o.astype(ref.dtype)

    return pl.pallas_call(
        body, grid=(rows // tr,), name=name,
        in_specs=[_col_spec(tr, w, cb) for _, w, cb in tiles] + [_whole(p) for p in params],
        out_specs=[_col_spec(tr, w, 0) for w, _ in outs],
        out_shape=[SDS((rows, w), dt) for w, dt in outs],
        compiler_params=pltpu.CompilerParams(dimension_semantics=("arbitrary",), vmem_limit_bytes=VMEM_LIMIT),
    )(*[a for a, _, _ in tiles], *params)


def _row_bwd(fn, name, tiles, params, cts, tr, want_tiles=None):
    rows = tiles[0][0].shape[0]
    nt, npar = len(tiles), len(params)
    want = list(range(nt)) if want_tiles is None else list(want_tiles)
    flat_cts = [c for group in cts for c in group]
    n_ct = len(flat_cts)

    def body(*refs):
        vals = [r[...].astype(F32) for r in refs[:nt + npar]]
        ct_refs = refs[nt + npar:nt + npar + n_ct]
        out_refs = refs[nt + npar + n_ct:]
        ct_vals, at = [], 0
        for group in cts:
            total = ct_refs[at][...].astype(F32)
            for r in ct_refs[at + 1:at + len(group)]:
                total = total + r[...].astype(F32)
            ct_vals.append(total)
            at += len(group)
        _, vjp = jax.vjp(lambda *a: tuple(fn(*a)), *vals)
        grads = vjp(tuple(ct_vals))
        for ref, t in zip(out_refs[:len(want)], want):
            ref[...] = grads[t]
        first = pl.program_id(0) == 0
        for ref, g in zip(out_refs[len(want):], grads[nt:]):
            @pl.when(first)
            def _(ref=ref, g=g):
                ref[...] = g

            @pl.when(jnp.logical_not(first))
            def _(ref=ref, g=g):
                ref[...] += g

    res = pl.pallas_call(
        body, grid=(rows // tr,), name=name,
        in_specs=[_col_spec(tr, w, cb) for _, w, cb in tiles] + [_whole(p) for p in params]
        + [_col_spec(tr, w, cb) for _, w, cb in flat_cts],
        out_specs=[_col_spec(tr, tiles[t][1], 0) for t in want] + [_whole(p) for p in params],
        out_shape=[SDS((rows, tiles[t][1]), F32) for t in want] + [SDS(p.shape, F32) for p in params],
        compiler_params=pltpu.CompilerParams(dimension_semantics=("arbitrary",), vmem_limit_bytes=VMEM_LIMIT),
    )(*[a for a, _, _ in tiles], *params, *[a for a, _, _ in flat_cts])
    return res[:len(want)], res[len(want):]


def _col_fwd(fn, name, x, first_block, n_blocks, params):
    rows = x.shape[0]

    def body(*refs):
        refs[-1][...] = fn(*[r[...] for r in refs[:-1]])

    return pl.pallas_call(
        body, grid=(n_blocks,), name=name,
        in_specs=[pl.BlockSpec((rows, LANES), lambda j: (0, first_block + j))]
        + [pl.BlockSpec((p.shape[0], LANES), lambda j: (0, j)) for p in params],
        out_specs=pl.BlockSpec((rows, LANES), lambda j: (0, j)),
        out_shape=SDS((rows, n_blocks * LANES), F32),
        compiler_params=pltpu.CompilerParams(dimension_semantics=("arbitrary",), vmem_limit_bytes=VMEM_LIMIT),
    )(x, *params)


def _col_bwd(fn, name, x, first_block, n_blocks, params, dy):
    rows = x.shape[0]
    npar = len(params)

    def body(*refs):
        vals = [r[...] for r in refs[:1 + npar]]
        _, vjp = jax.vjp(fn, *vals)
        grads = vjp(refs[1 + npar][...])
        for ref, g in zip(refs[2 + npar:], grads):
            ref[...] = g

    pspecs = [pl.BlockSpec((p.shape[0], LANES), lambda j: (0, j)) for p in params]
    blk = pl.BlockSpec((rows, LANES), lambda j: (0, j))
    res = pl.pallas_call(
        body, grid=(n_blocks,), name=name,
        in_specs=[pl.BlockSpec((rows, LANES), lambda j: (0, first_block + j))] + pspecs + [blk],
        out_specs=[blk] + pspecs,
        out_shape=[SDS((rows, n_blocks * LANES), F32)] + [SDS(p.shape, F32) for p in params],
        compiler_params=pltpu.CompilerParams(dimension_semantics=("arbitrary",), vmem_limit_bytes=VMEM_LIMIT),
    )(x, *params, dy)
    return res[0], res[1:]


def _conv_fn(x, w):
    acc = x * w[3:4, :]
    for j in range(3):
        acc = acc + shift_rows(x, 3 - j) * w[j:j + 1, :]
    return _silu(acc)


def _lerp_fn(x, mu):
    return x + (shift_rows(x, 1) - x) * mu[0:1, :]


def _seg_sum(x, width):
    if width == LANES:
        return jnp.sum(x, axis=1, keepdims=True)
    lo = lax.broadcasted_iota(jnp.int32, x.shape, 1) < width
    s0 = jnp.sum(jnp.where(lo, x, 0.0), axis=1, keepdims=True)
    s1 = jnp.sum(jnp.where(lo, 0.0, x), axis=1, keepdims=True)
    return jnp.where(lo, s0, s1)


def _per_block(fn, *xs):
    n = xs[0].shape[1] // LANES
    return jnp.concatenate([fn(*[x[:, LANES * b:LANES * (b + 1)] for x in xs]) for b in range(n)], axis=1)


def _head_expand(col0):
    r = lax.broadcasted_iota(jnp.int32, (LANES, DN_WIDTH), 0)
    c = lax.shift_right_logical(lax.broadcasted_iota(jnp.int32, (LANES, DN_WIDTH), 1), 7)
    return jnp.where(r == c + col0, 1.0, 0.0)


def _dn_pre_fn(cq, ck, gates, a_log, dt_bias):
    l2 = lambda x: x * lax.rsqrt(_seg_sum(x * x, LANES) + 1e-6)
    qh = _per_block(l2, cq) * (LANES ** -0.5)
    kh = _per_block(l2, ck)
    g = -jnp.exp(a_log) * _softplus(gates + dt_bias)
    gb = mm(g, _head_expand(0), "nn", True)
    bb = mm(_sigmoid(gates), _head_expand(DN_HEADS), "nn", True)
    return qh, kh, gb, bb, _cumsum_rows(gb)


def _dn_post_fn(o, z, nw):
    def one(ob, zb):
        return ob * lax.rsqrt(_seg_sum(ob * ob, LANES) * (1.0 / LANES) + RMS_EPS) * nw * _silu(zb)
    return (_per_block(one, o, z),)


def _rw_pre_fn(pr, pk, pv, pwa, pg, w0, a0, k_k, k_a, w2p, a2p, g2):
    log_w = -_softplus(-(w0 + mm(jnp.tanh(pwa), w2p))) - 0.5
    lw = -jnp.exp(log_w)
    a = _sigmoid(a0 + mm(pwa, a2p))
    gate = mm(_sigmoid(pg), g2)
    kk = pk * k_k
    kk = _per_block(lambda x: x / jnp.maximum(jnp.sqrt(_seg_sum(x * x, RW_HEAD)), 1e-12), kk)
    k = pk * (1.0 + (a - 1.0) * k_a)
    return pr, lw, k, pv, kk * a, -kk, gate, _cumsum_rows(lw)


def _rw_post_fn(y, r, k, v, gate, ln_w, ln_b, r_k):
    def one(yb, rb, kb, vb, gb, wb, bb, rkb):
        d = yb - _seg_sum(yb, RW_HEAD) * (1.0 / RW_HEAD)
        var = _seg_sum(d * d, RW_HEAD) * (1.0 / RW_HEAD)
        yn = d * lax.rsqrt(var + RW_GN_EPS) * wb + bb
        return (yn + _seg_sum(rb * kb * rkb, RW_HEAD) * vb) * gb
    return (_per_block(one, y, r, k, v, gate, ln_w, ln_b, r_k),)


def _rms_fn(h, w):
    return (h * lax.rsqrt(jnp.mean(h * h, axis=1, keepdims=True) + RMS_EPS) * w,)


def _xattn_fn(q, k, v):
    outs = []
    for h in range(XA_HEADS):
        sl = slice(LANES * h, LANES * (h + 1))
        s = mm(q[:, sl], k[:, sl], "nt") * (LANES ** -0.5)
        e = jnp.exp(s - jnp.max(s, axis=1, keepdims=True))
        outs.append(mm(e / jnp.sum(e, axis=1, keepdims=True), v[:, sl]))
    return (jnp.concatenate(outs, axis=1),)


def _fit(tile, dim):
    best = [t for t in range(LANES, min(tile, dim) + 1, LANES) if dim % t == 0]
    assert best, (tile, dim)
    return best[-1]


def _matmul(name, a, b, mode, out_dtypes, epilogue=None, extras=(), tm=1024, tn=1024, tk=2048, after=None):
    if mode == "tn":
        (k_dim, m), n = a.shape, b.shape[1]
    else:
        (m, k_dim), n = a.shape, (b.shape[1] if mode == "nn" else b.shape[0])
    tm, tn, tk = _fit(tm, m), _fit(tn, n), _fit(tk, k_dim)
    nk = k_dim // tk
    a_spec = (pl.BlockSpec((tk, tm), lambda i, j, k: (k, i)) if mode == "tn"
              else pl.BlockSpec((tm, tk), lambda i, j, k: (i, k)))
    b_spec = (pl.BlockSpec((tn, tk), lambda i, j, k: (j, k)) if mode == "nt"
              else pl.BlockSpec((tk, tn), lambda i, j, k: (k, j)))
    o_spec = pl.BlockSpec((tm, tn), lambda i, j, k: (i, j))
    n_ex, n_out = len(extras), len(out_dtypes)
    ties = [] if after is None else [after]

    def finish(total, rest):
        ex = [r[...].astype(F32) for r in rest[:n_ex]]
        res = epilogue(total, *ex) if epilogue else (total,)
        for ref, o in zip(rest[n_ex + len(ties):n_ex + len(ties) + n_out], res):
            ref[...] = o.astype(ref.dtype)

    def body_single(a_ref, b_ref, *rest):
        finish(_raw_dot(a_ref[...], b_ref[...], mode, False), rest)

    def body_acc(a_ref, b_ref, *rest):
        acc = rest[-1]
        k = pl.program_id(2)

        @pl.when(k == 0)
        def _():
            acc[...] = jnp.zeros_like(acc)

        acc[...] += _raw_dot(a_ref[...], b_ref[...], mode, False)

        @pl.when(k == nk - 1)
        def _():
            finish(acc[...], rest)

    res = pl.pallas_call(
        body_single if nk == 1 else body_acc, grid=(m // tm, n // tn, nk), name=name,
        in_specs=[a_spec, b_spec] + [o_spec] * n_ex + [pl.BlockSpec((8, LANES), lambda i, j, k: (0, 0))] * len(ties),
        out_specs=[o_spec] * n_out,
        out_shape=[SDS((m, n), dt) for dt in out_dtypes],
        scratch_shapes=[] if nk == 1 else [pltpu.VMEM((tm, tn), F32)],
        compiler_params=pltpu.CompilerParams(dimension_semantics=("parallel", "parallel", "arbitrary"),
                                             vmem_limit_bytes=VMEM_LIMIT),
    )(a, b, *extras, *ties)
    return res


def _loss_call(h, target, w, tr=256):
    rows, d = h.shape

    def fn(hv, wv, tv):
        y = _rms_fn(hv, wv)[0]
        return 0.5 * jnp.sum(jnp.mean(jnp.square(y - tv), axis=1, keepdims=True), axis=0, keepdims=True)

    def body(h_ref, t_ref, w_ref, loss_ref, dh_ref, dw_ref):
        tv = t_ref[...]
        val, vjp = jax.vjp(lambda hv, wv: fn(hv, wv, tv), h_ref[...], w_ref[...])
        dh, dw = vjp(jnp.ones((1, 1), F32))
        dh_ref[...] = dh
        first = pl.program_id(0) == 0

        @pl.when(first)
        def _():
            loss_ref[...] = jnp.broadcast_to(val, loss_ref.shape)
            dw_ref[...] = dw

        @pl.when(jnp.logical_not(first))
        def _():
            loss_ref[...] += jnp.broadcast_to(val, loss_ref.shape)
            dw_ref[...] += dw

    return pl.pallas_call(
        body, grid=(rows // tr,), name="loss_head",
        in_specs=[_col_spec(tr, d, 0), _col_spec(tr, d, 0), _whole(w)],
        out_specs=[pl.BlockSpec((8, LANES), lambda i: (0, 0)), _col_spec(tr, d, 0), _whole(w)],
        out_shape=[SDS((8, LANES), F32), SDS((rows, d), F32), SDS(w.shape, F32)],
        compiler_params=pltpu.CompilerParams(dimension_semantics=("arbitrary",), vmem_limit_bytes=VMEM_LIMIT),
    )(h, target, w)


def _adamw_vals(w, g, m, v):
    m = ADAM_B1 * m + (1.0 - ADAM_B1) * g
    v = ADAM_B2 * v + (1.0 - ADAM_B2) * jnp.square(g)
    m_hat = m / (1.0 - ADAM_B1 ** ADAM_STEP)
    v_hat = v / (1.0 - ADAM_B2 ** ADAM_STEP)
    delta = -ADAM_LR * (m_hat / (jnp.sqrt(v_hat) + ADAM_EPS) + ADAM_WD * w)
    return delta, m, v


def _sum_adamw(name, parts, w, m, v):
    r, c = w.shape
    budget = 6 * 1024 * 1024
    tr, tc = r, c
    for cand in (512, 256, 128, 64, 32, 16, 8):
        if r % cand == 0 and N_DEV * cand * c * 4 <= budget:
            tr = cand
            break
    if N_DEV * tr * c * 4 > budget:
        tc = max(t for t in range(LANES, c + 1, LANES) if c % t == 0 and N_DEV * r * t * 4 <= budget)

    def body(p_ref, w_ref, m_ref, v_ref, g_ref, d_ref, m2_ref, v2_ref):
        g = p_ref[0].astype(F32)
        for s in range(1, N_DEV):
            g = g + p_ref[s].astype(F32)
        g_ref[...] = g
        d_ref[...], m2_ref[...], v2_ref[...] = _adamw_vals(w_ref[...], g, m_ref[...], v_ref[...])

    blk = pl.BlockSpec((tr, tc), lambda i: (i, 0)) if tc == c else pl.BlockSpec((tr, tc), lambda i: (0, i))
    parts_blk = (pl.BlockSpec((N_DEV, tr, tc), lambda i: (0, i, 0)) if tc == c
                 else pl.BlockSpec((N_DEV, tr, tc), lambda i: (0, 0, i)))
    return pl.pallas_call(
        body, grid=(r // tr if tc == c else c // tc,), name=name,
        in_specs=[parts_blk, blk, blk, blk],
        out_specs=[blk] * 4, out_shape=[SDS((r, c), F32)] * 4,
        compiler_params=pltpu.CompilerParams(dimension_semantics=("arbitrary",), vmem_limit_bytes=VMEM_LIMIT),
    )(parts, w, m, v)


def _peers():
    x, y, c = lax.axis_index("x"), lax.axis_index("y"), lax.axis_index("c")
    peers = []
    for k in range(1, N_DEV):
        px = 1 - x if k & 4 else x
        py = 1 - y if k & 2 else y
        pc = 1 - c if k & 1 else c
        peers.append(((px, py, pc), 4 * px + 2 * py + pc))
    return 4 * x + 2 * y + c, peers


def _slot(ref, idx, cols):
    if cols is None:
        return ref.at[idx]
    return ref.at[:, pl.ds(pl.multiple_of(idx * cols, LANES), cols)]


def _exchange(name, srcs, dsts, gather):
    n = len(srcs)

    def body(*refs):
        start, wait = _exchange_ops([c for _, c in srcs], [c for _, _, c in dsts], gather,
                                    refs[:n], refs[n:2 * n], *refs[2 * n:])
        start()
        wait()

    any_spec = pl.BlockSpec(memory_space=pl.ANY)
    return pl.pallas_call(
        body, name=name,
        in_specs=[any_spec] * n, out_specs=[any_spec] * n,
        out_shape=[SDS(shape, dt) for shape, dt, _ in dsts],
        scratch_shapes=_exchange_sems(n),
    )(*[a for a, _ in srcs])


def _gather_two_level(name, srcs, dsts):
    n = len(srcs)
    dst_cols = [c for _, _, c in dsts]

    def body(*refs):
        src_refs, out_refs = refs[:n], refs[n:2 * n]
        send_sems, recv_sems, local_sems = refs[2 * n:]
        x, y, c = lax.axis_index("x"), lax.axis_index("y"), lax.axis_index("c")
        index = lambda px, py, pc: 4 * px + 2 * py + pc
        me, sibling = index(x, y, c), (x, y, 1 - c)
        chips = [(x, 1 - y), (1 - x, y), (1 - x, 1 - y)]

        def copy(a, k, src, block, to):
            return pltpu.make_async_remote_copy(
                src_ref=src, dst_ref=_slot(out_refs[a], block, dst_cols[a]),
                send_sem=send_sems.at[a, k], recv_sem=recv_sems.at[a, k],
                device_id=to, device_id_type=pl.DeviceIdType.MESH)

        local, first, passed = [], [], []
        for a in range(n):
            cp = pltpu.make_async_copy(src_refs[a], _slot(out_refs[a], me, dst_cols[a]), local_sems.at[a])
            cp.start()
            local.append(cp)
            first.append(copy(a, 0, src_refs[a], me, sibling))
            first += [copy(a, 1 + j, src_refs[a], me, (*chip, c)) for j, chip in enumerate(chips)]
        for cp in first:
            cp.start()
        for a in range(n):
            for j, chip in enumerate(chips):
                block = index(*chip, c)
                arrived = _slot(out_refs[a], block, dst_cols[a])
                copy(a, 1 + j, arrived, block, (*chip, c)).wait_recv()
                passed.append(copy(a, 4 + j, arrived, block, sibling))
                passed[-1].start()
        for a in range(n):
            copy(a, 0, src_refs[a], index(x, y, 1 - c), sibling).wait_recv()
            for j, chip in enumerate(chips):
                block = index(*chip, 1 - c)
                copy(a, 4 + j, src_refs[a], block, sibling).wait_recv()
        for cp in first + passed:
            cp.wait_send()
        for cp in local:
            cp.wait()

    any_spec = pl.BlockSpec(memory_space=pl.ANY)
    return pl.pallas_call(
        body, name=name,
        in_specs=[any_spec] * n, out_specs=[any_spec] * n,
        out_shape=[SDS(shape, dt) for shape, dt, _ in dsts],
        scratch_shapes=_exchange_sems(n),
    )(*[a for a, _ in srcs])


_HBM = pl.BlockSpec(memory_space=pltpu.HBM)
_SEM = pl.BlockSpec(memory_space=pltpu.SEMAPHORE)
_EFFECT = pltpu.SideEffectType.DATAFLOW_SIDE_EFFECTING


def _split_copies(src_cols, dst_cols, gather, src_refs, land_refs, send_sems, recv_sems, landings):
    me, peers = _peers()
    n = len(src_cols)
    remote, local = [], []
    for a, (s_cols, d_cols) in enumerate(zip(src_cols, dst_cols)):
        mine = src_refs[a] if gather else _slot(src_refs[a], me, s_cols)
        local.append(pltpu.make_async_copy(mine, _slot(land_refs[a], me, d_cols),
                                           send_sems.at[n * (N_DEV - 1) + a]))
        for k, (pos, idx) in enumerate(peers):
            blk = src_refs[a] if gather else _slot(src_refs[a], idx, s_cols)
            remote.append(pltpu.make_async_remote_copy(
                src_ref=blk, dst_ref=_slot(land_refs[a], idx if landings else me, d_cols),
                send_sem=send_sems.at[a * (N_DEV - 1) + k], recv_sem=recv_sems.at[a * (N_DEV - 1) + k],
                device_id=pos, device_id_type=pl.DeviceIdType.MESH))
    return remote, local


def _exchange_start(name, srcs, dsts, gather, after):
    n = len(srcs)
    src_cols, dst_cols = [c for _, c in srcs], [c for _, _, c in dsts]

    def body(*refs):
        src_refs, land_refs = refs[:n], refs[n:2 * n]
        send_sems, recv_sems = refs[2 * n + 1:2 * n + 3]
        token = refs[-1]
        remote, local = _split_copies(src_cols, dst_cols, gather, src_refs, land_refs, send_sems, recv_sems, False)
        for cp in remote + local:
            cp.start()
        token[...] = jnp.zeros_like(token)

    hbm = lambda a: pltpu.with_memory_space_constraint(a, pltpu.HBM)
    lands = [hbm(lax.empty(shape, dt)) for shape, dt, _ in dsts]
    res = pl.pallas_call(
        body, name=name,
        out_shape=(pltpu.SemaphoreType.DMA((n * N_DEV,)), pltpu.SemaphoreType.DMA((n * (N_DEV - 1),)),
                   *[pltpu.HBM(a.shape, a.dtype) for a, _ in srcs], *[pltpu.HBM(a.shape, a.dtype) for a in lands],
                   SDS((8, LANES), F32)),
        in_specs=[_HBM] * (2 * n) + [pl.BlockSpec(memory_space=pl.ANY)],
        out_specs=(_SEM, _SEM, *[_HBM] * (2 * n), pl.BlockSpec(memory_space=pltpu.VMEM)),
        input_output_aliases={i: 2 + i for i in range(2 * n)},
        compiler_params=pltpu.CompilerParams(has_side_effects=_EFFECT),
    )(*[hbm(a) for a, _ in srcs], *lands, after)
    handle = (res[0], res[1], res[2:2 + n], res[2 + n:2 + 2 * n], src_cols, dst_cols, gather)
    return handle, res[-1]


def _exchange_wait(name, handle, after):
    send_sems, recv_sems, src_thru, land_thru, src_cols, dst_cols, gather = handle
    n = len(src_thru)

    def body(*refs):
        src_refs, land_refs = refs[:n], refs[n:2 * n]
        s_sems, r_sems = refs[2 * n:2 * n + 2]
        remote, local = _split_copies(src_cols, dst_cols, gather, src_refs, land_refs, s_sems, r_sems, True)
        for cp in remote:
            cp.wait_send()
            cp.wait_recv()
        for cp in local:
            cp.wait()

    res = pl.pallas_call(
        body, name=name,
        out_shape=tuple(pltpu.HBM(a.shape, a.dtype) for a in (*src_thru, *land_thru)),
        in_specs=[_HBM] * (2 * n) + [_SEM, _SEM, pl.BlockSpec(memory_space=pl.ANY)],
        out_specs=tuple([_HBM] * (2 * n)),
        input_output_aliases={i: i for i in range(2 * n)},
        compiler_params=pltpu.CompilerParams(has_side_effects=_EFFECT),
    )(*src_thru, *land_thru, send_sems, recv_sems, after)
    return res[n:]


def _my_index():
    return 4 * lax.axis_index("x") + 2 * lax.axis_index("y") + lax.axis_index("c")


def _two_level_copies(stage, dst_cols, src_refs, land_refs, send_sems, recv_sems, landings):
    x, y, c = lax.axis_index("x"), lax.axis_index("y"), lax.axis_index("c")

    def pos(k):
        return (1 - x if k & 4 else x, 1 - y if k & 2 else y, 1 - c if k & 1 else c)

    def idx(k):
        px, py, pc = pos(k)
        return 4 * px + 2 * py + pc

    out = []
    for a, cols in enumerate(dst_cols):
        if stage == 1:
            for i, k in enumerate((1, 2, 4, 6)):
                out.append(pltpu.make_async_remote_copy(
                    src_ref=src_refs[a], dst_ref=_slot(land_refs[a], idx(k) if landings else idx(0), cols),
                    send_sem=send_sems.at[4 * a + i], recv_sem=recv_sems.at[4 * a + i],
                    device_id=pos(k), device_id_type=pl.DeviceIdType.MESH))
        else:
            for i, k in enumerate((2, 4, 6)):
                out.append(pltpu.make_async_remote_copy(
                    src_ref=_slot(land_refs[a], idx(k), cols),
                    dst_ref=_slot(land_refs[a], idx(k ^ 1) if landings else idx(k), cols),
                    send_sem=send_sems.at[3 * a + i], recv_sem=recv_sems.at[3 * a + i],
                    device_id=pos(1), device_id_type=pl.DeviceIdType.MESH))
    return out


def _gather2_start(name, srcs, dsts, after):
    n = len(srcs)
    dst_cols = [c for _, _, c in dsts]

    def body(*refs):
        src_refs, land_refs = refs[:n], refs[n:2 * n]
        send_sems, recv_sems = refs[2 * n + 1:2 * n + 3]
        me = _my_index()
        for a in range(n):
            pltpu.make_async_copy(src_refs[a], _slot(land_refs[a], me, dst_cols[a]), send_sems.at[4 * n + a]).start()
        for cp in _two_level_copies(1, dst_cols, src_refs, land_refs, send_sems, recv_sems, False):
            cp.start()
        refs[-1][...] = jnp.zeros_like(refs[-1])

    hbm = lambda a: pltpu.with_memory_space_constraint(a, pltpu.HBM)
    lands = [hbm(lax.empty(shape, dt)) for shape, dt, _ in dsts]
    res = pl.pallas_call(
        body, name=name,
        out_shape=(pltpu.SemaphoreType.DMA((5 * n,)), pltpu.SemaphoreType.DMA((4 * n,)),
                   *[pltpu.HBM(a.shape, a.dtype) for a, _ in srcs], *[pltpu.HBM(a.shape, a.dtype) for a in lands],
                   SDS((8, LANES), F32)),
        in_specs=[_HBM] * (2 * n) + [pl.BlockSpec(memory_space=pl.ANY)],
        out_specs=(_SEM, _SEM, *[_HBM] * (2 * n), pl.BlockSpec(memory_space=pltpu.VMEM)),
        input_output_aliases={i: 2 + i for i in range(2 * n)},
        compiler_params=pltpu.CompilerParams(has_side_effects=_EFFECT),
    )(*[hbm(a) for a, _ in srcs], *lands, after)
    return (res[0], res[1], res[2:2 + n], res[2 + n:2 + 2 * n], dst_cols), res[-1]


def _gather2_pass(name, handle, after):
    send1, recv1, src_thru, land_thru, dst_cols = handle
    n = len(src_thru)

    def body(*refs):
        src_refs, land_refs = refs[:n], refs[n:2 * n]
        s1, r1 = refs[2 * n:2 * n + 2]
        send2, recv2 = refs[2 * n + 3:2 * n + 5]
        me = _my_index()
        for cp in _two_level_copies(1, dst_cols, src_refs, land_refs, s1, r1, True):
            cp.wait_send()
            cp.wait_recv()
        for a in range(n):
            pltpu.make_async_copy(src_refs[a], _slot(land_refs[a], me, dst_cols[a]), s1.at[4 * n + a]).wait()
        for cp in _two_level_copies(2, dst_cols, src_refs, land_refs, send2, recv2, False):
            cp.start()
        refs[-1][...] = jnp.zeros_like(refs[-1])

    res = pl.pallas_call(
        body, name=name,
        out_shape=(pltpu.SemaphoreType.DMA((3 * n,)), pltpu.SemaphoreType.DMA((3 * n,)),
                   *[pltpu.HBM(a.shape, a.dtype) for a in (*src_thru, *land_thru)], SDS((8, LANES), F32)),
        in_specs=[_HBM] * (2 * n) + [_SEM, _SEM, pl.BlockSpec(memory_space=pl.ANY)],
        out_specs=(_SEM, _SEM, *[_HBM] * (2 * n), pl.BlockSpec(memory_space=pltpu.VMEM)),
        input_output_aliases={i: 2 + i for i in range(2 * n)},
        compiler_params=pltpu.CompilerParams(has_side_effects=_EFFECT),
    )(*src_thru, *land_thru, send1, recv1, after)
    return (res[0], res[1], res[2:2 + n], res[2 + n:2 + 2 * n], dst_cols), res[-1]


def _gather2_wait(name, handle, after):
    send2, recv2, src_thru, land_thru, dst_cols = handle
    n = len(src_thru)

    def body(*refs):
        src_refs, land_refs = refs[:n], refs[n:2 * n]
        s2, r2 = refs[2 * n:2 * n + 2]
        for cp in _two_level_copies(2, dst_cols, src_refs, land_refs, s2, r2, True):
            cp.wait_send()
            cp.wait_recv()

    res = pl.pallas_call(
        body, name=name,
        out_shape=tuple(pltpu.HBM(a.shape, a.dtype) for a in (*src_thru, *land_thru)),
        in_specs=[_HBM] * (2 * n) + [_SEM, _SEM, pl.BlockSpec(memory_space=pl.ANY)],
        out_specs=tuple([_HBM] * (2 * n)),
        input_output_aliases={i: i for i in range(2 * n)},
        compiler_params=pltpu.CompilerParams(has_side_effects=_EFFECT),
    )(*src_thru, *land_thru, send2, recv2, after)
    return res[n:]


def _exchange_sems(n):
    return [pltpu.SemaphoreType.DMA((n, N_DEV - 1)), pltpu.SemaphoreType.DMA((n, N_DEV - 1)),
            pltpu.SemaphoreType.DMA((n,))]


def _exchange_ops(src_cols, dst_cols, gather, src_refs, out_refs, send_sems, recv_sems, local_sems):
    def copies(with_landings):
        me, peers = _peers()
        local, sends, landings = [], [], []
        for a, (s_cols, d_cols) in enumerate(zip(src_cols, dst_cols)):
            mine = src_refs[a] if gather else _slot(src_refs[a], me, s_cols)
            local.append(pltpu.make_async_copy(mine, _slot(out_refs[a], me, d_cols), local_sems.at[a]))
            for k, (pos, idx) in enumerate(peers):
                out_blk = src_refs[a] if gather else _slot(src_refs[a], idx, s_cols)
                both = dict(src_ref=out_blk, send_sem=send_sems.at[a, k], recv_sem=recv_sems.at[a, k],
                            device_id=pos, device_id_type=pl.DeviceIdType.MESH)
                sends.append(pltpu.make_async_remote_copy(dst_ref=_slot(out_refs[a], me, d_cols), **both))
                if with_landings:
                    landings.append(pltpu.make_async_remote_copy(dst_ref=_slot(out_refs[a], idx, d_cols), **both))
        return local, sends, landings

    def start():
        local, sends, _ = copies(False)
        for cp in local + sends:
            cp.start()

    def wait():
        local, sends, landings = copies(True)
        for cp in landings:
            cp.wait_recv()
        for cp in sends:
            cp.wait_send()
        for cp in local:
            cp.wait()

    return start, wait


def _rms_res_fn(h, w):
    return _rms_fn(h, w)[0], h


def _add_epilogue(acc, res):
    return (acc + res,)


def _gather_plan(shards):
    srcs, dsts = [], []
    for n, sh in shards.items():
        r, c = sh.shape
        srcs.append((sh, None))
        if SHARDED[n] and c % LANES == 0:
            dsts.append(((r, N_DEV * c), sh.dtype, c))
        else:
            dsts.append(((N_DEV, r, c), sh.dtype, None))
    return srcs, dsts, True


def _w_in_segments():
    out = []
    for j in range(N_DEV):
        lo, hi = W_IN_SHARD * j, W_IN_SHARD * (j + 1)
        for a, b in ((lo, min(hi, DN_COLS)), (max(lo, DN_COLS), hi)):
            if a < b:
                out.append((j, a - lo, b - lo, a if a < DN_COLS else a + RW_OFF - DN_COLS))
    return out


def _w_in_to_padded(shards, tc=512):
    _, _, cols = shards.shape

    def body(g_ref, o_ref):
        o_ref[...] = jnp.zeros_like(o_ref)
        for j, a, b, dst in _w_in_segments():
            o_ref[dst:dst + b - a, :] = g_ref[j, a:b, :]

    return pl.pallas_call(
        body, grid=(cols // tc,), name="w_in_to_padded",
        in_specs=[pl.BlockSpec((N_DEV, W_IN_SHARD, tc), lambda i: (0, 0, i))],
        out_specs=pl.BlockSpec((IN_PAD, tc), lambda i: (0, i)),
        out_shape=SDS((IN_PAD, cols), shards.dtype),
        compiler_params=pltpu.CompilerParams(dimension_semantics=("arbitrary",), vmem_limit_bytes=VMEM_LIMIT),
    )(shards)


def _w_in_grad_to_shards(gw, tc=512):
    _, cols = gw.shape

    def body(w_ref, o_ref):
        for j, a, b, dst in _w_in_segments():
            o_ref[j, a:b, :] = w_ref[dst:dst + b - a, :]

    return pl.pallas_call(
        body, grid=(cols // tc,), name="w_in_grad_to_shards",
        in_specs=[pl.BlockSpec((IN_PAD, tc), lambda i: (0, i))],
        out_specs=pl.BlockSpec((N_DEV, W_IN_SHARD, tc), lambda i: (0, 0, i)),
        out_shape=SDS((N_DEV, W_IN_SHARD, cols), gw.dtype),
        compiler_params=pltpu.CompilerParams(dimension_semantics=("arbitrary",), vmem_limit_bytes=VMEM_LIMIT),
    )(gw)


def _gather_finish(names, outs):
    full = {}
    for n, arr in zip(names, outs):
        if n == "w_in":
            full[n] = _w_in_to_padded(arr)
        elif arr.ndim == 2:
            full[n] = arr
        elif SHARDED[n]:
            full[n] = arr.transpose(1, 0, 2).reshape(arr.shape[1], -1)
        else:
            full[n] = arr.reshape(-1, arr.shape[2])
    return full


def _scatter_plan(grads):
    srcs, dsts = [], []
    for n, gr in grads.items():
        if gr.ndim == 3:
            srcs.append((gr, None))
            dsts.append((gr.shape, gr.dtype, None))
            continue
        rows, cols = gr.shape
        if not SHARDED[n]:
            r, c = rows // N_DEV, cols
            srcs.append((gr.reshape(N_DEV, r, c), None))
        else:
            r, c = rows, cols // N_DEV
            if c % LANES == 0:
                srcs.append((gr, c))
            else:
                srcs.append((gr.reshape(r, N_DEV, c).transpose(1, 0, 2), None))
        dsts.append(((N_DEV, r, c), gr.dtype, None))
    return srcs, dsts, False


def _local_step(x, mem, target, wt, late):
    d = D_MODEL
    g = {}
    wt = dict(wt)
    grp_a = ("w_out", "xa_wq", "xa_wk", "xa_wv", "xa_wo")
    grp_b = ("ffn_w1", "ffn_w2")
    plan = lambda names: _gather_plan({n: late[n] for n in names})[:2]
    handle_a, tok_a = _gather2_start("late_gather_a_start", *plan(grp_a), wt["w_in"])
    handle_w1, tok_b = _gather2_start("late_gather_w1_start", *plan(("ffn_w1",)), tok_a)
    handle_w2, tok_c = _gather2_start("late_gather_w2_start", *plan(("ffn_w2",)), tok_b)
    mix_w = wt["mix_norm_w"] + (tok_a[0:1, 0:1] + tok_b[0:1, 0:1] + tok_c[0:1, 0:1])
    u = _row_fwd(_rms_fn, "mix_norm", [(x, d, 0)], [mix_w], [(d, BF16)], 256)[0]
    p = _matmul("in_proj", u, wt["w_in"], "nt", [F32], tn=1536)[0]
    c = _col_fwd(_conv_fn, "dn_conv", p, 0, 24, [wt["dn_conv_w"]])
    handle_a, tok = _gather2_pass("late_gather_a_pass", handle_a, c)
    dn_pre_tiles = [(c, DN_WIDTH, 0), (c, DN_WIDTH, 1), (p, LANES, 32)]
    dn_pre_params = [wt["dn_a_log"], wt["dn_dt_bias"]]
    qh, kh, gb, bb, gcb = _row_fwd(_dn_pre_fn, "dn_pre", dn_pre_tiles, [dn_pre_params[0] + tok[0:1, :], dn_pre_params[1]],
                                   [(DN_WIDTH, F32)] * 5, CHUNK)
    dn_arrs = [(qh, 0), (kh, 0), (c, 16), (gb, 0), (bb, 0), (gcb, 0)]
    o, kept_dn = _scan_fwd(_gdn_group, "gdn_scan", dn_arrs, DN_HEADS, 1)
    dn_post_tiles = [(o, DN_WIDTH, 0), (p, DN_WIDTH, 3)]
    o_dn = _row_fwd(_dn_post_fn, "dn_post", dn_post_tiles, [wt["dn_norm_w"]], [(DN_WIDTH, BF16)], 256)[0]

    ps = _col_fwd(_lerp_fn, "rw_shift", p, RW_OFF // LANES, 26, [wt["rw_mu"]])
    rw_pre_tiles = [(ps, RW_WIDTH, 0), (ps, RW_WIDTH, 1), (ps, RW_WIDTH, 2), (ps, LANES, 24), (ps, LANES, 25)]
    rw_pre_params = [wt[n] for n in ("rw_w0", "rw_a0", "rw_k_k", "rw_k_a", "rw_w2", "rw_a2", "rw_g2")]
    r, lw, k, v, al, be, gate, gcw = _row_fwd(_rw_pre_fn, "rw_pre", rw_pre_tiles, rw_pre_params,
                                              [(RW_WIDTH, F32)] * 8, CHUNK)
    rw_arrs = [(r, 0), (lw, 0), (k, 0), (v, 0), (al, 0), (be, 0), (gcw, 0)]
    y, kept_rw = _scan_fwd(_rw_group, "rw_scan", rw_arrs, RW_WIDTH // LANES, 2)
    handle_w1, tok = _gather2_pass("late_gather_w1_pass", handle_w1, y)
    rw_post_tiles = [(t, RW_WIDTH, 0) for t in (y, r, k, v, gate)]
    rw_post_params = [wt["rw_ln_w"], wt["rw_ln_b"], wt["rw_r_k"]]
    o_rw = _row_fwd(_rw_post_fn, "rw_post", rw_post_tiles, [rw_post_params[0] + tok[0:1, 0:1]] + rw_post_params[1:],
                    [(RW_WIDTH, BF16)], 128)[0]
    o_cat = jnp.concatenate([o_dn, o_rw], axis=1)
    wt.update(_gather_finish(grp_a, _gather2_wait("late_gather_a_wait", handle_a, o_cat)))
    h1 = _matmul("out_proj", o_cat, wt["w_out"], "nn", [F32], _add_epilogue, (x,))[0]

    handle_w2, tok = _gather2_pass("late_gather_w2_pass", handle_w2, h1)
    hn = _row_fwd(_rms_fn, "xa_norm", [(h1, d, 0)], [wt["xa_norm_w"] + tok[0:1, 0:1]], [(d, BF16)], 256)[0]
    mn = _row_fwd(_rms_fn, "mem_norm", [(mem, d, 0)], [wt["mem_norm_w"]], [(d, BF16)], 256)[0]
    q = _matmul("xa_q", hn, wt["xa_wq"], "nn", [F32])[0]
    kx = _matmul("xa_k", mn, wt["xa_wk"], "nn", [F32])[0]
    vx = _matmul("xa_v", mn, wt["xa_wv"], "nn", [F32])[0]
    ao = _row_fwd(_xattn_fn, "xattn", [(q, XA_WIDTH, 0)], [kx, vx], [(XA_WIDTH, BF16)], 256)[0]
    h2 = _matmul("xa_o", ao, wt["xa_wo"], "nn", [F32], _add_epilogue, (h1,))[0]

    f = _row_fwd(_rms_fn, "ffn_norm", [(h2, d, 0)], [wt["ffn_norm_w"]], [(d, BF16)], 256)[0]
    wt.update(_gather_finish(("ffn_w1",), _gather2_wait("late_gather_w1_wait", handle_w1, f)))
    a, hid = _matmul("ffn_up", f, wt["ffn_w1"], "nn", [F32, BF16],
                     lambda acc: (acc, jnp.square(jnp.maximum(acc, 0.0))))
    wt.update(_gather_finish(("ffn_w2",), _gather2_wait("late_gather_w2_wait", handle_w2, hid)))
    h3 = _matmul("ffn_down", hid, wt["ffn_w2"], "nn", [F32], _add_epilogue, (h2,))[0]
    loss8, dh3, g["final_norm_w"] = _loss_call(h3, target, wt["final_norm_w"])

    da = _matmul("ffn_down_dx", dh3, wt["ffn_w2"], "nt", [BF16],
                 lambda acc, av: (acc * 2.0 * jnp.maximum(av, 0.0),), (a,))[0]
    g["ffn_w2"] = _matmul("ffn_down_dw", hid, dh3, "tn", [BF16])[0]
    g["ffn_w1"] = _matmul("ffn_up_dw", f, da, "tn", [BF16])[0]
    df = _matmul("ffn_up_dx", da, wt["ffn_w1"], "nt", [F32])[0]
    pending = {}
    plan = _scatter_plan({n: g.pop(n) for n in grp_b})
    pending[grp_b], tok = _exchange_start("late_grad_b_start", *plan, loss8)
    (dh2,), (g["ffn_norm_w"],) = _row_bwd(_rms_res_fn, "ffn_norm_bwd", [(h2, d, 0)],
                                          [wt["ffn_norm_w"] + tok[0:1, 0:1]],
                                          [[(df, d, 0)], [(dh3, d, 0)]], 256)

    dao = _matmul("xa_o_dx", dh2, wt["xa_wo"], "nt", [F32])[0]
    g["xa_wo"] = _matmul("xa_o_dw", ao, dh2, "tn", [BF16])[0]
    (dq,), (dkx, dvx) = _row_bwd(_xattn_fn, "xattn_bwd", [(q, XA_WIDTH, 0)], [kx, vx], [[(dao, XA_WIDTH, 0)]], 256)
    dhn = _matmul("xa_q_dx", dq, wt["xa_wq"], "nt", [F32])[0]
    g["xa_wq"] = _matmul("xa_q_dw", hn, dq, "tn", [BF16])[0]
    g["xa_wk"] = _matmul("xa_k_dw", mn, dkx, "tn", [BF16])[0]
    g["xa_wv"] = _matmul("xa_v_dw", mn, dvx, "tn", [BF16])[0]
    dmn = _matmul("xa_k_dx", dkx, wt["xa_wk"], "nt", [F32])[0]
    dmn = _matmul("xa_v_dx", dvx, wt["xa_wv"], "nt", [F32], _add_epilogue, (dmn,))[0]
    _, (g["mem_norm_w"],) = _row_bwd(_rms_fn, "mem_norm_bwd", [(mem, d, 0)], [wt["mem_norm_w"]],
                                     [[(dmn, d, 0)]], 256, want_tiles=())
    (dh1,), (g["xa_norm_w"],) = _row_bwd(_rms_res_fn, "xa_norm_bwd", [(h1, d, 0)], [wt["xa_norm_w"]],
                                         [[(dhn, d, 0)], [(dh2, d, 0)]], 256)

    do_cat = _matmul("out_proj_dx", dh1, wt["w_out"], "nt", [F32])[0]
    g["w_out"] = _matmul("out_proj_dw", o_cat, dh1, "tn", [BF16])[0]

    plan = _scatter_plan({n: g.pop(n) for n in grp_a})
    pending[grp_a], tok = _exchange_start("late_grad_a_start", *plan, tok)
    (dy, dr1, dk1, dv1, dgate), (g["rw_ln_w"], g["rw_ln_b"], g["rw_r_k"]) = _row_bwd(
        _rw_post_fn, "rw_post_bwd", rw_post_tiles, [rw_post_params[0] + tok[0:1, 0:1]] + rw_post_params[1:],
        [[(do_cat, RW_WIDTH, 1)]], 128)
    dr2, dlw, dk2, dv2, dal, dbe, dgcw = _scan_bwd(_rw_group, "rw_scan_bwd", rw_arrs, kept_rw, dy,
                                                   RW_WIDTH // LANES)
    one = lambda t: [(t, RW_WIDTH, 0)]
    two = lambda s, t: [(s, RW_WIDTH, 0), (t, RW_WIDTH, 0)]
    d_ps, rw_pre_grads = _row_bwd(
        _rw_pre_fn, "rw_pre_bwd", rw_pre_tiles, rw_pre_params,
        [two(dr1, dr2), one(dlw), two(dk1, dk2), two(dv1, dv2), one(dal), one(dbe), one(dgate), one(dgcw)],
        CHUNK)
    for n, val in zip(("rw_w0", "rw_a0", "rw_k_k", "rw_k_a", "rw_w2", "rw_a2", "rw_g2"), rw_pre_grads):
        g[n] = val
    dp_rw, (g["rw_mu"],) = _col_bwd(_lerp_fn, "rw_shift_bwd", p, RW_OFF // LANES, 26, [wt["rw_mu"]],
                                    jnp.concatenate(d_ps, axis=1))

    (do, dz), (g["dn_norm_w"],) = _row_bwd(_dn_post_fn, "dn_post_bwd", dn_post_tiles, [wt["dn_norm_w"]],
                                           [[(do_cat, DN_WIDTH, 0)]], 256)
    dqh, dkh, dv_dn, dgb, dbb, dgcb = _scan_bwd(_gdn_group, "gdn_scan_bwd", dn_arrs, kept_dn, do, DN_HEADS)
    one = lambda t: [(t, DN_WIDTH, 0)]
    (dcq, dck, dgates), (g["dn_a_log"], g["dn_dt_bias"]) = _row_bwd(
        _dn_pre_fn, "dn_pre_bwd", dn_pre_tiles, dn_pre_params,
        [one(dqh), one(dkh), one(dgb), one(dbb), one(dgcb)], CHUNK)
    dp_qkv, (g["dn_conv_w"],) = _col_bwd(_conv_fn, "dn_conv_bwd", p, 0, 24, [wt["dn_conv_w"]],
                                         jnp.concatenate([dcq, dck, dv_dn], axis=1))
    dp = jnp.concatenate([dp_qkv, dz, dgates, dp_rw, jnp.zeros((x.shape[0], LANES), F32)], axis=1).astype(BF16)
    g["w_in"] = _matmul("in_proj_dw", dp, u, "tn", [BF16], tm=1536)[0]
    early = _logical_grads(g)
    pending[EARLY], tok = _exchange_start("early_grad_start", *_scatter_plan({n: early.pop(n) for n in EARLY}), tok)
    du = _matmul("in_proj_dx", dp, wt["w_in"], "nn", [F32], after=tok)[0]
    (dx,), (early["mix_norm_w"],) = _row_bwd(_rms_res_fn, "mix_norm_bwd", [(x, d, 0)], [wt["mix_norm_w"]],
                                             [[(du, d, 0)], [(dh1, d, 0)]], 256)
    return loss8, dx, early, pending, tok


WEIGHTS = ["mix_norm_w", "w_in", "dn_conv_w", "dn_a_log", "dn_dt_bias", "dn_norm_w", "rw_mu", "rw_w0", "rw_w2",
           "rw_a0", "rw_a2", "rw_g2", "rw_k_k", "rw_k_a", "rw_r_k", "rw_ln_w", "rw_ln_b", "w_out", "xa_norm_w",
           "mem_norm_w", "xa_wq", "xa_wk", "xa_wv", "xa_wo", "ffn_norm_w", "ffn_w1", "ffn_w2", "final_norm_w"]
SHARDED = {"w_in": False, "w_out": False, "xa_wq": False, "xa_wk": False, "xa_wv": False, "xa_wo": True,
           "ffn_w1": True, "ffn_w2": False, "dn_conv_w": True, "rw_w2": True, "rw_a2": True, "rw_g2": True}
BF16_PAYLOAD = ("w_in", "w_out", "xa_wq", "xa_wk", "xa_wv", "xa_wo", "ffn_w1", "ffn_w2")
REPLICATED = [n for n in WEIGHTS if n not in SHARDED]
EARLY = ("w_in", "dn_conv_w", "rw_w2", "rw_a2", "rw_g2")
RW_IN_COLS = IN_COLS - DN_COLS
W_IN_SHARD = IN_COLS // N_DEV


def _layout_weights(fw):
    wt = dict(fw)
    wt["dn_conv_w"] = jnp.pad(fw["dn_conv_w"], ((0, 4), (0, 0)))
    wt["dn_a_log"] = jnp.pad(fw["dn_a_log"], ((0, 0), (0, LANES - DN_HEADS)))
    wt["dn_dt_bias"] = jnp.pad(fw["dn_dt_bias"], ((0, 0), (0, LANES - DN_HEADS)))
    wt["rw_w2"] = jnp.pad(fw["rw_w2"], ((0, 64), (0, 0)))
    wt["rw_a2"] = jnp.pad(fw["rw_a2"], ((64, 0), (0, 0)))
    return wt


def _logical_grads(g):
    out = dict(g)
    out["w_in"] = _w_in_grad_to_shards(g["w_in"])
    out["dn_conv_w"] = g["dn_conv_w"][:4]
    out["dn_a_log"] = g["dn_a_log"][:, :DN_HEADS]
    out["dn_dt_bias"] = g["dn_dt_bias"][:, :DN_HEADS]
    out["rw_w2"] = g["rw_w2"][:64]
    out["rw_a2"] = g["rw_a2"][64:]
    return out


def _pack(vals):
    parts = []
    for v in vals:
        flat = v.reshape(-1)
        parts.append(jnp.pad(flat, (0, -flat.shape[0] % LANES)))
    flat = jnp.concatenate(parts)
    flat = jnp.pad(flat, (0, -flat.shape[0] % (8 * LANES)))
    return flat.reshape(-1, LANES)


def _unpack(packed, shapes):
    flat = packed.reshape(-1)
    out, at = [], 0
    for shp in shapes:
        size = math.prod(shp)
        out.append(flat[at:at + size].reshape(shp))
        at += size + (-size % LANES)
    return out


def kernel(x, mem, mix_norm_w, w_in, dn_conv_w, dn_a_log, dn_dt_bias, dn_norm_w, rw_mu, rw_w0, rw_w2, rw_a0, rw_a2, rw_g2, rw_k_k, rw_k_a, rw_r_k, rw_ln_w, rw_ln_b, w_out, xa_norm_w, mem_norm_w, xa_wq, xa_wk, xa_wv, xa_wo, ffn_norm_w, ffn_w1, ffn_w2, final_norm_w, loss_target, m_mix_norm_w, m_w_in, m_dn_conv_w, m_dn_a_log, m_dn_dt_bias, m_dn_norm_w, m_rw_mu, m_rw_w0, m_rw_w2, m_rw_a0, m_rw_a2, m_rw_g2, m_rw_k_k, m_rw_k_a, m_rw_r_k, m_rw_ln_w, m_rw_ln_b, m_w_out, m_xa_norm_w, m_mem_norm_w, m_xa_wq, m_xa_wk, m_xa_wv, m_xa_wo, m_ffn_norm_w, m_ffn_w1, m_ffn_w2, m_final_norm_w, v_mix_norm_w, v_w_in, v_dn_conv_w, v_dn_a_log, v_dn_dt_bias, v_dn_norm_w, v_rw_mu, v_rw_w0, v_rw_w2, v_rw_a0, v_rw_a2, v_rw_g2, v_rw_k_k, v_rw_k_a, v_rw_r_k, v_rw_ln_w, v_rw_ln_b, v_w_out, v_xa_norm_w, v_mem_norm_w, v_xa_wq, v_xa_wk, v_xa_wv, v_xa_wo, v_ffn_norm_w, v_ffn_w1, v_ffn_w2, v_final_norm_w):
    given = dict(locals())
    w = {n: given[n] for n in WEIGHTS}
    m = {n: given["m_" + n] for n in WEIGHTS}
    v = {n: given["v_" + n] for n in WEIGHTS}

    local = {n: (lambda t: t[0].T) if n == "w_in" else (lambda t: t[0]) for n in SHARDED}
    shards = {n: (local[n](w[n]).astype(BF16) if n in BF16_PAYLOAD else local[n](w[n])) for n in SHARDED}
    srcs, dsts, _ = _gather_plan({n: shards[n] for n in EARLY})
    full = _gather_finish(EARLY, _gather_two_level("early_all_gather", srcs, dsts))
    for n in REPLICATED:
        full[n] = w[n].reshape(1, -1)

    loss8, dx, g, pending, after = _local_step(x[0], mem[0], loss_target[0], _layout_weights(full),
                                               {n: shards[n] for n in SHARDED if n not in EARLY})
    loss = lax.psum(loss8[0, 0], ("x", "y", "c"))

    packed = _pack([g[n] for n in REPLICATED])
    small, _ = _exchange_start("small_gather_start", [(packed, None)], [((N_DEV,) + packed.shape, F32, None)], True,
                               after)
    grad, delta, new_m, new_v = {}, {}, {}, {}
    done = [dx]

    def tie():
        return jnp.broadcast_to(sum(t[:1, :1] for t in done), (8, LANES))

    for names in sorted(pending, key=lambda names: names == EARLY):
        handle = pending[names]
        for n, parts in zip(names, _exchange_wait("grad_wait_" + names[0], handle, tie())):
            res = _sum_adamw("adamw_" + n, parts, local[n](w[n]), local[n](m[n]), local[n](v[n]))
            grad[n], delta[n], new_m[n], new_v[n] = [(t.T if n == "w_in" else t)[None] for t in res]
            done.append(res[1])

    (parts,) = _exchange_wait("small_gather_wait", small, tie())
    res = _sum_adamw("adamw_small", parts, _pack([w[n] for n in REPLICATED]),
                     _pack([m[n] for n in REPLICATED]), _pack([v[n] for n in REPLICATED]))
    shapes = [w[n].shape for n in REPLICATED]
    for store, packed_out in zip((grad, delta, new_m, new_v), res):
        for n, val in zip(REPLICATED, _unpack(packed_out, shapes)):
            store[n] = val

    return (loss, dx[None], *[grad[n] for n in WEIGHTS], *[delta[n] for n in WEIGHTS],
            *[new_m[n] for n in WEIGHTS], *[new_v[n] for n in WEIGHTS])
```

```python
import functools
import math

import jax
import jax.numpy as jnp
from jax import lax
from jax.experimental import pallas as pl
from jax.experimental.pallas import tpu as pltpu

F32 = jnp.float32
BF16 = jnp.bfloat16
SDS = jax.ShapeDtypeStruct

N_DEV = 8
D_MODEL = 2048
LANES = 128
CHUNK = 128
DN_HEADS = 8
DN_WIDTH = 1024
RW_WIDTH = 1024
RW_HEAD = 64
XA_HEADS = 4
XA_WIDTH = 512
FFN_HIDDEN = 8192
IN_COLS = 7440
DN_COLS = 4112
IN_PAD = 7680
RW_OFF = 4224
RMS_EPS = 1e-6
RW_GN_EPS = 64e-5
VMEM_LIMIT = 56 * 1024 * 1024

ADAM_LR = 0.001
ADAM_B1 = 0.9
ADAM_B2 = 0.999
ADAM_EPS = 1e-08
ADAM_WD = 0.01
ADAM_STEP = 10

_DIMS = {"nn": (((1,), (0,)), ((), ())), "nt": (((1,), (1,)), ((), ())), "tn": (((0,), (0,)), ((), ()))}


def _raw_dot(a, b, mode, hi):
    if hi:
        return lax.dot_general(a, b, _DIMS[mode], precision=lax.Precision.HIGHEST, preferred_element_type=F32)
    return lax.dot_general(a.astype(BF16), b.astype(BF16), _DIMS[mode], preferred_element_type=F32)


@functools.partial(jax.custom_vjp, nondiff_argnums=(2, 3))
def mm(a, b, mode="nn", hi=False):
    return _raw_dot(a, b, mode, hi)


def _mm_fwd(a, b, mode, hi):
    return _raw_dot(a, b, mode, hi), (a, b)


def _mm_bwd(mode, hi, res, g):
    a, b = res
    if mode == "nn":
        return _raw_dot(g, b, "nt", hi), _raw_dot(a, g, "tn", hi)
    if mode == "nt":
        return _raw_dot(g, b, "nn", hi), _raw_dot(g, a, "tn", hi)
    return _raw_dot(b, g, "nt", hi), _raw_dot(a, g, "nn", hi)


mm.defvjp(_mm_fwd, _mm_bwd)


def _shift_rows_raw(x, k):
    n = x.shape[0]
    rolled = pltpu.roll(x, k % n, axis=0)
    row = lax.broadcasted_iota(jnp.int32, x.shape, 0)
    keep = row >= k if k > 0 else row < n + k
    return jnp.where(keep, rolled, 0.0)


@functools.partial(jax.custom_vjp, nondiff_argnums=(1,))
def shift_rows(x, k):
    return _shift_rows_raw(x, k)


shift_rows.defvjp(lambda x, k: (_shift_rows_raw(x, k), None), lambda k, _, g: (_shift_rows_raw(g, -k),))


def _softplus(x):
    return jnp.maximum(x, 0.0) + jnp.log(1.0 + jnp.exp(-jnp.abs(x)))


def _sigmoid(x):
    return 1.0 / (1.0 + jnp.exp(-x))


def _silu(x):
    return x * _sigmoid(x)


def _tri_masks(n):
    ii = lax.broadcasted_iota(jnp.int32, (n, n), 0)
    jj = lax.broadcasted_iota(jnp.int32, (n, n), 1)
    return ii >= jj, ii > jj, ii == jj


def _neumann_inv_raw(m):
    n = m.shape[0]
    _, _, eye = _tri_masks(n)
    eye = jnp.where(eye, 1.0, 0.0)
    p = eye + m
    mk = m
    for _ in range(int(math.log2(n)) - 1):
        mk = _raw_dot(mk, mk, "nn", False)
        p = p + _raw_dot(p, mk, "nn", False)
    resid = eye - p + _raw_dot(m, p, "nn", True)
    return p + _raw_dot(p, resid, "nn", False)


@jax.custom_vjp
def _neumann_inv(m):
    return _neumann_inv_raw(m)


def _neumann_inv_fwd(m):
    p = _neumann_inv_raw(m)
    return p, p


def _neumann_inv_bwd(p, g):
    return (_raw_dot(_raw_dot(p, g, "tn", False), p, "nt", False),)


_neumann_inv.defvjp(_neumann_inv_fwd, _neumann_inv_bwd)


@jax.custom_vjp
def _saved_inv(m, p):
    return p


_saved_inv.defvjp(lambda m, p: (p, p), lambda p, g: (_neumann_inv_bwd(p, g)[0], jnp.zeros_like(p)))


def _inverse(m, saved):
    return _neumann_inv(m) if saved is None else _saved_inv(m, saved)


def _cumsum_rows(x):
    causal, _, _ = _tri_masks(x.shape[0])
    return mm(jnp.where(causal, 1.0, 0.0), x, "nn", True)


def _gdn_group(s0, q, k, v, gb, bb, gc, *saved):
    diff = jnp.stack([gc[j] - gc[j].T for j in range(gc.shape[0])])
    return jax.vmap(_gdn_chunk)(s0, q, k, v, gb, bb, gc, diff, *saved)


def _rw_group(*args):
    return jax.vmap(_rw_chunk)(*args)


def _gdn_chunk(s0, q, k, v, gb, bb, gc, diff, saved=None):
    c = q.shape[0]
    causal, strict, _ = _tri_masks(c)
    decay = jnp.exp(jnp.where(causal, diff, -jnp.inf))
    kb = k * bb
    a = jnp.where(strict, mm(kb, k, "nt") * decay, 0.0)
    p = _inverse(-a, saved)
    u = mm(p, v * bb)
    w = mm(p, kb * jnp.exp(gc))
    attn = mm(q, k, "nt") * decay
    v_new = u - mm(w, s0)
    o = mm(q * jnp.exp(gc), s0) + mm(attn, v_new)
    g_last = jnp.sum(gb, axis=0, keepdims=True)
    s1 = s0 * jnp.exp(g_last) + mm(k * jnp.exp(g_last - gc), v_new, "tn")
    return o, s1, p


def _rw_chunk(s0, r, lw, k, v, al, be, gc, saved0=None, saved1=None):
    c = r.shape[0]
    causal, strict, _ = _tri_masks(c)
    gp = gc - lw
    row = lax.broadcasted_iota(jnp.int32, lw.shape, 0)
    lane = lax.broadcasted_iota(jnp.int32, lw.shape, 1)
    g_mid = jnp.sum(jnp.where(row < c // 2, lw, 0.0), axis=0, keepdims=True)
    g_last = jnp.sum(lw, axis=0, keepdims=True)
    e_n = jnp.exp(g_mid - gc)
    rg = r * jnp.exp(gc - g_mid)
    bg = be * jnp.exp(gp - g_mid)
    an = al * e_n
    kn = k * e_n
    bt = mm(be * jnp.exp(gp), s0, "nt")
    rt = mm(r * jnp.exp(gc), s0, "nt")
    us, ys, ps = [], [], []
    for h, saved in enumerate((saved0, saved1)):
        mine = (lane >= RW_HEAD) if h else (lane < RW_HEAD)
        bgh = jnp.where(mine, bg, 0.0)
        rgh = jnp.where(mine, rg, 0.0)
        a_ab = jnp.where(strict, mm(bgh, an, "nt"), 0.0)
        a_kb = jnp.where(strict, mm(bgh, kn, "nt"), 0.0)
        a_ra = jnp.where(causal, mm(rgh, an, "nt"), 0.0)
        a_rk = jnp.where(causal, mm(rgh, kn, "nt"), 0.0)
        p = _inverse(a_ab, saved)
        ps.append(p)
        u_h = mm(p, bt + mm(a_kb, v))
        us.append(u_h)
        ys.append(rt + mm(a_ra, u_h) + mm(a_rk, v))
    lo = lane < RW_HEAD
    u = jnp.where(lo, us[0], us[1])
    y = jnp.where(lo, ys[0], ys[1])
    tail = jnp.exp(g_last - gc)
    s1 = s0 * jnp.exp(g_last) + mm(u, al * tail, "tn") + mm(v, k * tail, "tn")
    vi = lax.broadcasted_iota(jnp.int32, s0.shape, 0)
    ki = lax.broadcasted_iota(jnp.int32, s0.shape, 1)
    s1 = jnp.where((vi < RW_HEAD) == (ki < RW_HEAD), s1, 0.0)
    return y, s1, ps[0], ps[1]


SCAN_HB = 8


def _scan_specs(arrs, n_chunks, reverse):
    def spec(off):
        assert off % SCAN_HB == 0
        if reverse:
            return pl.BlockSpec((CHUNK, SCAN_HB * LANES), lambda h, n: (n_chunks - 1 - n, off // SCAN_HB + h))
        return pl.BlockSpec((CHUNK, SCAN_HB * LANES), lambda h, n: (n, off // SCAN_HB + h))
    return [spec(off) for _, off in arrs]


def _split_heads(x):
    return jnp.stack([x[:, LANES * j:LANES * (j + 1)] for j in range(SCAN_HB)], axis=0)


def _merge_heads(x):
    return jnp.concatenate([x[j] for j in range(SCAN_HB)], axis=1)


def _scan_fwd(group_fn, name, arrs, heads, n_kept):
    s = arrs[0][0].shape[0]
    n_chunks = s // CHUNK
    n_in = len(arrs)

    def body(*refs):
        y_ref, st_ref = refs[n_in:n_in + 2]
        kept_refs, s_scr = refs[n_in + 2:-1], refs[-1]

        @pl.when(pl.program_id(1) == 0)
        def _():
            s_scr[...] = jnp.zeros_like(s_scr)

        s0 = s_scr[...]
        st_ref[...] = s0
        y, s1, *kept = group_fn(s0, *[_split_heads(r[...]) for r in refs[:n_in]])
        y_ref[...] = _merge_heads(y)
        s_scr[...] = s1
        for ref, val in zip(kept_refs, kept):
            ref[...] = val

    per_chunk = pl.BlockSpec((SCAN_HB, None, LANES, LANES), lambda h, n: (h, n, 0, 0))
    res = pl.pallas_call(
        body, grid=(heads // SCAN_HB, n_chunks), name=name,
        in_specs=_scan_specs(arrs, n_chunks, False),
        out_specs=[pl.BlockSpec((CHUNK, SCAN_HB * LANES), lambda h, n: (n, h))] + [per_chunk] * (1 + n_kept),
        out_shape=[SDS((s, heads * LANES), F32)] + [SDS((heads, n_chunks, LANES, LANES), F32)] * (1 + n_kept),
        scratch_shapes=[pltpu.VMEM((SCAN_HB, LANES, LANES), F32)],
        compiler_params=pltpu.CompilerParams(dimension_semantics=("arbitrary", "arbitrary")),
    )(*[a for a, _ in arrs])
    return res[0], res[1:]


def _scan_bwd(group_fn, name, arrs, kept, dy, heads):
    s = arrs[0][0].shape[0]
    n_chunks = s // CHUNK
    n_in, n_kept = len(arrs), len(kept)

    def body(*refs):
        kept_vals = [r[...] for r in refs[n_in:n_in + n_kept]]
        dy_ref = refs[n_in + n_kept]
        d_refs = refs[n_in + n_kept + 1:2 * n_in + n_kept + 1]
        ds_scr = refs[-1]

        @pl.when(pl.program_id(1) == 0)
        def _():
            ds_scr[...] = jnp.zeros_like(ds_scr)

        def fn(s0, *ins):
            return group_fn(s0, *ins, *kept_vals[1:])[:2]

        _, vjp = jax.vjp(fn, kept_vals[0], *[_split_heads(r[...]) for r in refs[:n_in]])
        grads = vjp((_split_heads(dy_ref[...]), ds_scr[...]))
        ds_scr[...] = grads[0]
        for ref, g in zip(d_refs, grads[1:]):
            ref[...] = _merge_heads(g)

    rev = pl.BlockSpec((CHUNK, SCAN_HB * LANES), lambda h, n: (n_chunks - 1 - n, h))
    per_chunk = pl.BlockSpec((SCAN_HB, None, LANES, LANES), lambda h, n: (h, n_chunks - 1 - n, 0, 0))
    return pl.pallas_call(
        body, grid=(heads // SCAN_HB, n_chunks), name=name,
        in_specs=_scan_specs(arrs, n_chunks, True) + [per_chunk] * n_kept + [rev],
        out_specs=[rev] * n_in,
        out_shape=[SDS((s, heads * LANES), F32)] * n_in,
        scratch_shapes=[pltpu.VMEM((SCAN_HB, LANES, LANES), F32)],
        compiler_params=pltpu.CompilerParams(dimension_semantics=("arbitrary", "arbitrary")),
    )(*[a for a, _ in arrs], *kept, dy)


def _col_spec(tr, width, cb):
    return pl.BlockSpec((tr, width), lambda i: (i, cb))


def _whole(p):
    return pl.BlockSpec(p.shape, lambda i: (0,) * p.ndim)


def _row_fwd(fn, name, tiles, params, outs, tr):
    rows = tiles[0][0].shape[0]
    nt, npar = len(tiles), len(params)

    def body(*refs):
        vals = [r[...].astype(F32) for r in refs[:nt + npar]]
        for ref, o in zip(refs[nt + npar:], fn(*vals)):
            ref[...] = o.astype(ref.dtype)

    return pl.pallas_call(
        body, grid=(rows // tr,), name=name,
        in_specs=[_col_spec(tr, w, cb) for _, w, cb in tiles] + [_whole(p) for p in params],
        out_specs=[_col_spec(tr, w, 0) for w, _ in outs],
        out_shape=[SDS((rows, w), dt) for w, dt in outs],
        compiler_params=pltpu.CompilerParams(dimension_semantics=("arbitrary",), vmem_limit_bytes=VMEM_LIMIT),
    )(*[a for a, _, _ in tiles], *params)


def _row_bwd(fn, name, tiles, params, cts, tr, want_tiles=None):
    rows = tiles[0][0].shape[0]
    nt, npar = len(tiles), len(params)
    want = list(range(nt)) if want_tiles is None else list(want_tiles)
    flat_cts = [c for group in cts for c in group]
    n_ct = len(flat_cts)

    def body(*refs):
        vals = [r[...].astype(F32) for r in refs[:nt + npar]]
        ct_refs = refs[nt + npar:nt + npar + n_ct]
        out_refs = refs[nt + npar + n_ct:]
        ct_vals, at = [], 0
        for group in cts:
            total = ct_refs[at][...].astype(F32)
            for r in ct_refs[at + 1:at + len(group)]:
                total = total + r[...].astype(F32)
            ct_vals.append(total)
            at += len(group)
        _, vjp = jax.vjp(lambda *a: tuple(fn(*a)), *vals)
        grads = vjp(tuple(ct_vals))
        for ref, t in zip(out_refs[:len(want)], want):
            ref[...] = grads[t]
        first = pl.program_id(0) == 0
        for ref, g in zip(out_refs[len(want):], grads[nt:]):
            @pl.when(first)
            def _(ref=ref, g=g):
                ref[...] = g

            @pl.when(jnp.logical_not(first))
            def _(ref=ref, g=g):
                ref[...] += g

    res = pl.pallas_call(
        body, grid=(rows // tr,), name=name,
        in_specs=[_col_spec(tr, w, cb) for _, w, cb in tiles] + [_whole(p) for p in params]
        + [_col_spec(tr, w, cb) for _, w, cb in flat_cts],
        out_specs=[_col_spec(tr, tiles[t][1], 0) for t in want] + [_whole(p) for p in params],
        out_shape=[SDS((rows, tiles[t][1]), F32) for t in want] + [SDS(p.shape, F32) for p in params],
        compiler_params=pltpu.CompilerParams(dimension_semantics=("arbitrary",), vmem_limit_bytes=VMEM_LIMIT),
    )(*[a for a, _, _ in tiles], *params, *[a for a, _, _ in flat_cts])
    return res[:len(want)], res[len(want):]


def _col_fwd(fn, name, x, first_block, n_blocks, params):
    rows = x.shape[0]

    def body(*refs):
        refs[-1][...] = fn(*[r[...] for r in refs[:-1]])

    return pl.pallas_call(
        body, grid=(n_blocks,), name=name,
        in_specs=[pl.BlockSpec((rows, LANES), lambda j: (0, first_block + j))]
        + [pl.BlockSpec((p.shape[0], LANES), lambda j: (0, j)) for p in params],
        out_specs=pl.BlockSpec((rows, LANES), lambda j: (0, j)),
        out_shape=SDS((rows, n_blocks * LANES), F32),
        compiler_params=pltpu.CompilerParams(dimension_semantics=("arbitrary",), vmem_limit_bytes=VMEM_LIMIT),
    )(x, *params)


def _col_bwd(fn, name, x, first_block, n_blocks, params, dy):
    rows = x.shape[0]
    npar = len(params)

    def body(*refs):
        vals = [r[...] for r in refs[:1 + npar]]
        _, vjp = jax.vjp(fn, *vals)
        grads = vjp(refs[1 + npar][...])
        for ref, g in zip(refs[2 + npar:], grads):
            ref[...] = g

    pspecs = [pl.BlockSpec((p.shape[0], LANES), lambda j: (0, j)) for p in params]
    blk = pl.BlockSpec((rows, LANES), lambda j: (0, j))
    res = pl.pallas_call(
        body, grid=(n_blocks,), name=name,
        in_specs=[pl.BlockSpec((rows, LANES), lambda j: (0, first_block + j))] + pspecs + [blk],
        out_specs=[blk] + pspecs,
        out_shape=[SDS((rows, n_blocks * LANES), F32)] + [SDS(p.shape, F32) for p in params],
        compiler_params=pltpu.CompilerParams(dimension_semantics=("arbitrary",), vmem_limit_bytes=VMEM_LIMIT),
    )(x, *params, dy)
    return res[0], res[1:]


def _conv_fn(x, w):
    acc = x * w[3:4, :]
    for j in range(3):
        acc = acc + shift_rows(x, 3 - j) * w[j:j + 1, :]
    return _silu(acc)


def _lerp_fn(x, mu):
    return x + (shift_rows(x, 1) - x) * mu[0:1, :]


def _seg_sum(x, width):
    if width == LANES:
        return jnp.sum(x, axis=1, keepdims=True)
    lo = lax.broadcasted_iota(jnp.int32, x.shape, 1) < width
    s0 = jnp.sum(jnp.where(lo, x, 0.0), axis=1, keepdims=True)
    s1 = jnp.sum(jnp.where(lo, 0.0, x), axis=1, keepdims=True)
    return jnp.where(lo, s0, s1)


def _per_block(fn, *xs):
    n = xs[0].shape[1] // LANES
    return jnp.concatenate([fn(*[x[:, LANES * b:LANES * (b + 1)] for x in xs]) for b in range(n)], axis=1)


def _head_expand(col0):
    r = lax.broadcasted_iota(jnp.int32, (LANES, DN_WIDTH), 0)
    c = lax.shift_right_logical(lax.broadcasted_iota(jnp.int32, (LANES, DN_WIDTH), 1), 7)
    return jnp.where(r == c + col0, 1.0, 0.0)


def _dn_pre_fn(cq, ck, gates, a_log, dt_bias):
    l2 = lambda x: x * lax.rsqrt(_seg_sum(x * x, LANES) + 1e-6)
    qh = _per_block(l2, cq) * (LANES ** -0.5)
    kh = _per_block(l2, ck)
    g = -jnp.exp(a_log) * _softplus(gates + dt_bias)
    gb = mm(g, _head_expand(0), "nn", True)
    bb = mm(_sigmoid(gates), _head_expand(DN_HEADS), "nn", True)
    return qh, kh, gb, bb, _cumsum_rows(gb)


def _dn_post_fn(o, z, nw):
    def one(ob, zb):
        return ob * lax.rsqrt(_seg_sum(ob * ob, LANES) * (1.0 / LANES) + RMS_EPS) * nw * _silu(zb)
    return (_per_block(one, o, z),)


def _rw_pre_fn(pr, pk, pv, pwa, pg, w0, a0, k_k, k_a, w2p, a2p, g2):
    log_w = -_softplus(-(w0 + mm(jnp.tanh(pwa), w2p))) - 0.5
    lw = -jnp.exp(log_w)
    a = _sigmoid(a0 + mm(pwa, a2p))
    gate = mm(_sigmoid(pg), g2)
    kk = pk * k_k
    kk = _per_block(lambda x: x / jnp.maximum(jnp.sqrt(_seg_sum(x * x, RW_HEAD)), 1e-12), kk)
    k = pk * (1.0 + (a - 1.0) * k_a)
    return pr, lw, k, pv, kk * a, -kk, gate, _cumsum_rows(lw)


def _rw_post_fn(y, r, k, v, gate, ln_w, ln_b, r_k):
    def one(yb, rb, kb, vb, gb, wb, bb, rkb):
        d = yb - _seg_sum(yb, RW_HEAD) * (1.0 / RW_HEAD)
        var = _seg_sum(d * d, RW_HEAD) * (1.0 / RW_HEAD)
        yn = d * lax.rsqrt(var + RW_GN_EPS) * wb + bb
        return (yn + _seg_sum(rb * kb * rkb, RW_HEAD) * vb) * gb
    return (_per_block(one, y, r, k, v, gate, ln_w, ln_b, r_k),)


def _rms_fn(h, w):
    return (h * lax.rsqrt(jnp.mean(h * h, axis=1, keepdims=True) + RMS_EPS) * w,)


def _xattn_fn(q, k, v):
    outs = []
    for h in range(XA_HEADS):
        sl = slice(LANES * h, LANES * (h + 1))
        s = mm(q[:, sl], k[:, sl], "nt") * (LANES ** -0.5)
        e = jnp.exp(s - jnp.max(s, axis=1, keepdims=True))
        outs.append(mm(e / jnp.sum(e, axis=1, keepdims=True), v[:, sl]))
    return (jnp.concatenate(outs, axis=1),)


def _fit(tile, dim):
    best = [t for t in range(LANES, min(tile, dim) + 1, LANES) if dim % t == 0]
    assert best, (tile, dim)
    return best[-1]


def _matmul(name, a, b, mode, out_dtypes, epilogue=None, extras=(), tm=1024, tn=1024, tk=2048, after=None):
    if mode == "tn":
        (k_dim, m), n = a.shape, b.shape[1]
    else:
        (m, k_dim), n = a.shape, (b.shape[1] if mode == "nn" else b.shape[0])
    tm, tn, tk = _fit(tm, m), _fit(tn, n), _fit(tk, k_dim)
    nk = k_dim // tk
    a_spec = (pl.BlockSpec((tk, tm), lambda i, j, k: (k, i)) if mode == "tn"
              else pl.BlockSpec((tm, tk), lambda i, j, k: (i, k)))
    b_spec = (pl.BlockSpec((tn, tk), lambda i, j, k: (j, k)) if mode == "nt"
              else pl.BlockSpec((tk, tn), lambda i, j, k: (k, j)))
    o_spec = pl.BlockSpec((tm, tn), lambda i, j, k: (i, j))
    n_ex, n_out = len(extras), len(out_dtypes)
    ties = [] if after is None else [after]

    def finish(total, rest):
        ex = [r[...].astype(F32) for r in rest[:n_ex]]
        res = epilogue(total, *ex) if epilogue else (total,)
        for ref, o in zip(rest[n_ex + len(ties):n_ex + len(ties) + n_out], res):
            ref[...] = o.astype(ref.dtype)

    def body_single(a_ref, b_ref, *rest):
        finish(_raw_dot(a_ref[...], b_ref[...], mode, False), rest)

    def body_acc(a_ref, b_ref, *rest):
        acc = rest[-1]
        k = pl.program_id(2)

        @pl.when(k == 0)
        def _():
            acc[...] = jnp.zeros_like(acc)

        acc[...] += _raw_dot(a_ref[...], b_ref[...], mode, False)

        @pl.when(k == nk - 1)
        def _():
            finish(acc[...], rest)

    res = pl.pallas_call(
        body_single if nk == 1 else body_acc, grid=(m // tm, n // tn, nk), name=name,
        in_specs=[a_spec, b_spec] + [o_spec] * n_ex + [pl.BlockSpec((8, LANES), lambda i, j, k: (0, 0))] * len(ties),
        out_specs=[o_spec] * n_out,
        out_shape=[SDS((m, n), dt) for dt in out_dtypes],
        scratch_shapes=[] if nk == 1 else [pltpu.VMEM((tm, tn), F32)],
        compiler_params=pltpu.CompilerParams(dimension_semantics=("parallel", "parallel", "arbitrary"),
                                             vmem_limit_bytes=VMEM_LIMIT),
    )(a, b, *extras, *ties)
    return res


def _loss_call(h, target, w, tr=256):
    rows, d = h.shape

    def fn(hv, wv, tv):
        y = _rms_fn(hv, wv)[0]
        return 0.5 * jnp.sum(jnp.mean(jnp.square(y - tv), axis=1, keepdims=True), axis=0, keepdims=True)

    def body(h_ref, t_ref, w_ref, loss_ref, dh_ref, dw_ref):
        tv = t_ref[...]
        val, vjp = jax.vjp(lambda hv, wv: fn(hv, wv, tv), h_ref[...], w_ref[...])
        dh, dw = vjp(jnp.ones((1, 1), F32))
        dh_ref[...] = dh
        first = pl.program_id(0) == 0

        @pl.when(first)
        def _():
            loss_ref[...] = jnp.broadcast_to(val, loss_ref.shape)
            dw_ref[...] = dw

        @pl.when(jnp.logical_not(first))
        def _():
            loss_ref[...] += jnp.broadcast_to(val, loss_ref.shape)
            dw_ref[...] += dw

    return pl.pallas_call(
        body, grid=(rows // tr,), name="loss_head",
        in_specs=[_col_spec(tr, d, 0), _col_spec(tr, d, 0), _whole(w)],
        out_specs=[pl.BlockSpec((8, LANES), lambda i: (0, 0)), _col_spec(tr, d, 0), _whole(w)],
        out_shape=[SDS((8, LANES), F32), SDS((rows, d), F32), SDS(w.shape, F32)],
        compiler_params=pltpu.CompilerParams(dimension_semantics=("arbitrary",), vmem_limit_bytes=VMEM_LIMIT),
    )(h, target, w)


def _adamw_vals(w, g, m, v):
    m = ADAM_B1 * m + (1.0 - ADAM_B1) * g
    v = ADAM_B2 * v + (1.0 - ADAM_B2) * jnp.square(g)
    m_hat = m / (1.0 - ADAM_B1 ** ADAM_STEP)
    v_hat = v / (1.0 - ADAM_B2 ** ADAM_STEP)
    delta = -ADAM_LR * (m_hat / (jnp.sqrt(v_hat) + ADAM_EPS) + ADAM_WD * w)
    return delta, m, v


def _sum_adamw(name, parts, w, m, v):
    r, c = w.shape
    n_parts = parts.shape[0]
    budget = 6 * 1024 * 1024
    tr, tc = r, c
    for cand in (512, 256, 128, 64, 32, 16, 8):
        if r % cand == 0 and n_parts * cand * c * 4 <= budget:
            tr = cand
            break
    if n_parts * tr * c * 4 > budget:
        tc = max(t for t in range(LANES, c + 1, LANES) if c % t == 0 and n_parts * r * t * 4 <= budget)

    def body(p_ref, w_ref, m_ref, v_ref, g_ref, d_ref, m2_ref, v2_ref):
        g = p_ref[0].astype(F32)
        for s in range(1, n_parts):
            g = g + p_ref[s].astype(F32)
        g_ref[...] = g
        d_ref[...], m2_ref[...], v2_ref[...] = _adamw_vals(w_ref[...], g, m_ref[...], v_ref[...])

    blk = pl.BlockSpec((tr, tc), lambda i: (i, 0)) if tc == c else pl.BlockSpec((tr, tc), lambda i: (0, i))
    parts_blk = (pl.BlockSpec((n_parts, tr, tc), lambda i: (0, i, 0)) if tc == c
                 else pl.BlockSpec((n_parts, tr, tc), lambda i: (0, 0, i)))
    return pl.pallas_call(
        body, grid=(r // tr if tc == c else c // tc,), name=name,
        in_specs=[parts_blk, blk, blk, blk],
        out_specs=[blk] * 4, out_shape=[SDS((r, c), F32)] * 4,
        compiler_params=pltpu.CompilerParams(dimension_semantics=("arbitrary",), vmem_limit_bytes=VMEM_LIMIT),
    )(parts, w, m, v)


def _peers():
    x, y, c = lax.axis_index("x"), lax.axis_index("y"), lax.axis_index("c")
    peers = []
    for k in range(1, N_DEV):
        px = 1 - x if k & 4 else x
        py = 1 - y if k & 2 else y
        pc = 1 - c if k & 1 else c
        peers.append(((px, py, pc), 4 * px + 2 * py + pc))
    return 4 * x + 2 * y + c, peers


def _slot(ref, idx, cols):
    if cols is None:
        return ref.at[idx]
    return ref.at[:, pl.ds(pl.multiple_of(idx * cols, LANES), cols)]


def _exchange(name, srcs, dsts, gather):
    n = len(srcs)

    def body(*refs):
        start, wait = _exchange_ops([c for _, c in srcs], [c for _, _, c in dsts], gather,
                                    refs[:n], refs[n:2 * n], *refs[2 * n:])
        start()
        wait()

    any_spec = pl.BlockSpec(memory_space=pl.ANY)
    return pl.pallas_call(
        body, name=name,
        in_specs=[any_spec] * n, out_specs=[any_spec] * n,
        out_shape=[SDS(shape, dt) for shape, dt, _ in dsts],
        scratch_shapes=_exchange_sems(n),
    )(*[a for a, _ in srcs])


def _gather_two_level(name, srcs, dsts):
    n = len(srcs)
    dst_cols = [c for _, _, c in dsts]

    def body(*refs):
        src_refs, out_refs = refs[:n], refs[n:2 * n]
        send_sems, recv_sems, local_sems = refs[2 * n:]
        x, y, c = lax.axis_index("x"), lax.axis_index("y"), lax.axis_index("c")
        index = lambda px, py, pc: 4 * px + 2 * py + pc
        me, sibling = index(x, y, c), (x, y, 1 - c)
        chips = [(x, 1 - y), (1 - x, y), (1 - x, 1 - y)]

        def copy(a, k, src, block, to):
            return pltpu.make_async_remote_copy(
                src_ref=src, dst_ref=_slot(out_refs[a], block, dst_cols[a]),
                send_sem=send_sems.at[a, k], recv_sem=recv_sems.at[a, k],
                device_id=to, device_id_type=pl.DeviceIdType.MESH)

        local, first, passed = [], [], []
        for a in range(n):
            cp = pltpu.make_async_copy(src_refs[a], _slot(out_refs[a], me, dst_cols[a]), local_sems.at[a])
            cp.start()
            local.append(cp)
            first.append(copy(a, 0, src_refs[a], me, sibling))
            first += [copy(a, 1 + j, src_refs[a], me, (*chip, c)) for j, chip in enumerate(chips)]
        for cp in first:
            cp.start()
        for a in range(n):
            for j, chip in enumerate(chips):
                block = index(*chip, c)
                arrived = _slot(out_refs[a], block, dst_cols[a])
                copy(a, 1 + j, arrived, block, (*chip, c)).wait_recv()
                passed.append(copy(a, 4 + j, arrived, block, sibling))
                passed[-1].start()
        for a in range(n):
            copy(a, 0, src_refs[a], index(x, y, 1 - c), sibling).wait_recv()
            for j, chip in enumerate(chips):
                block = index(*chip, 1 - c)
                copy(a, 4 + j, src_refs[a], block, sibling).wait_recv()
        for cp in first + passed:
            cp.wait_send()
        for cp in local:
            cp.wait()

    any_spec = pl.BlockSpec(memory_space=pl.ANY)
    return pl.pallas_call(
        body, name=name,
        in_specs=[any_spec] * n, out_specs=[any_spec] * n,
        out_shape=[SDS(shape, dt) for shape, dt, _ in dsts],
        scratch_shapes=_exchange_sems(n),
    )(*[a for a, _ in srcs])


_HBM = pl.BlockSpec(memory_space=pltpu.HBM)
_SEM = pl.BlockSpec(memory_space=pltpu.SEMAPHORE)
_EFFECT = pltpu.SideEffectType.DATAFLOW_SIDE_EFFECTING


def _split_copies(src_cols, dst_cols, gather, chips, src_refs, land_refs, send_sems, recv_sems, landings):
    me, peers = _peers()
    if chips:
        me, peers = me // 2, [(pos, idx // 2) for k, (pos, idx) in enumerate(peers) if (k + 1) in (2, 4, 6)]
    n, width = len(src_cols), len(peers)
    remote, local = [], []
    for a, (s_cols, d_cols) in enumerate(zip(src_cols, dst_cols)):
        mine = src_refs[a] if gather else _slot(src_refs[a], me, s_cols)
        local.append(pltpu.make_async_copy(mine, _slot(land_refs[a], me, d_cols), send_sems.at[n * width + a]))
        for k, (pos, idx) in enumerate(peers):
            blk = src_refs[a] if gather else _slot(src_refs[a], idx, s_cols)
            remote.append(pltpu.make_async_remote_copy(
                src_ref=blk, dst_ref=_slot(land_refs[a], idx if landings else me, d_cols),
                send_sem=send_sems.at[a * width + k], recv_sem=recv_sems.at[a * width + k],
                device_id=pos, device_id_type=pl.DeviceIdType.MESH))
    return remote, local


def _exchange_start(name, srcs, dsts, gather, after, chips=False):
    n = len(srcs)
    src_cols, dst_cols = [c for _, c in srcs], [c for _, _, c in dsts]
    width = 3 if chips else N_DEV - 1

    def body(*refs):
        src_refs, land_refs = refs[:n], refs[n:2 * n]
        send_sems, recv_sems = refs[2 * n + 1:2 * n + 3]
        token = refs[-1]
        remote, local = _split_copies(src_cols, dst_cols, gather, chips, src_refs, land_refs, send_sems, recv_sems,
                                      False)
        for cp in remote + local:
            cp.start()
        token[...] = jnp.zeros_like(token)

    hbm = lambda a: pltpu.with_memory_space_constraint(a, pltpu.HBM)
    lands = [hbm(lax.empty(shape, dt)) for shape, dt, _ in dsts]
    res = pl.pallas_call(
        body, name=name,
        out_shape=(pltpu.SemaphoreType.DMA((n * (width + 1),)), pltpu.SemaphoreType.DMA((n * width,)),
                   *[pltpu.HBM(a.shape, a.dtype) for a, _ in srcs], *[pltpu.HBM(a.shape, a.dtype) for a in lands],
                   SDS((8, LANES), F32)),
        in_specs=[_HBM] * (2 * n) + [pl.BlockSpec(memory_space=pl.ANY)],
        out_specs=(_SEM, _SEM, *[_HBM] * (2 * n), pl.BlockSpec(memory_space=pltpu.VMEM)),
        input_output_aliases={i: 2 + i for i in range(2 * n)},
        compiler_params=pltpu.CompilerParams(has_side_effects=_EFFECT),
    )(*[hbm(a) for a, _ in srcs], *lands, after)
    handle = (res[0], res[1], res[2:2 + n], res[2 + n:2 + 2 * n], src_cols, dst_cols, gather, chips)
    return handle, res[-1]


def _exchange_wait(name, handle, after):
    send_sems, recv_sems, src_thru, land_thru, src_cols, dst_cols, gather, chips = handle
    n = len(src_thru)

    def body(*refs):
        src_refs, land_refs = refs[:n], refs[n:2 * n]
        s_sems, r_sems = refs[2 * n:2 * n + 2]
        remote, local = _split_copies(src_cols, dst_cols, gather, chips, src_refs, land_refs, s_sems, r_sems, True)
        for cp in remote:
            cp.wait_send()
            cp.wait_recv()
        for cp in local:
            cp.wait()

    res = pl.pallas_call(
        body, name=name,
        out_shape=tuple(pltpu.HBM(a.shape, a.dtype) for a in (*src_thru, *land_thru)),
        in_specs=[_HBM] * (2 * n) + [_SEM, _SEM, pl.BlockSpec(memory_space=pl.ANY)],
        out_specs=tuple([_HBM] * (2 * n)),
        input_output_aliases={i: i for i in range(2 * n)},
        compiler_params=pltpu.CompilerParams(has_side_effects=_EFFECT),
    )(*src_thru, *land_thru, send_sems, recv_sems, after)
    return res[n:]


def _pair_swap(name, arrs):
    n = len(arrs)

    def body(*refs):
        src_refs, out_refs = refs[:n], refs[n:2 * n]
        send_sems, recv_sems = refs[2 * n:]
        x, y, c = lax.axis_index("x"), lax.axis_index("y"), lax.axis_index("c")
        copies = [pltpu.make_async_remote_copy(
            src_ref=src_refs[a].at[:, 1 - c], dst_ref=out_refs[a], send_sem=send_sems.at[a], recv_sem=recv_sems.at[a],
            device_id=(x, y, 1 - c), device_id_type=pl.DeviceIdType.MESH) for a in range(n)]
        for cp in copies:
            cp.start()
        for cp in copies:
            cp.wait()

    any_spec = pl.BlockSpec(memory_space=pl.ANY)
    return pl.pallas_call(
        body, name=name,
        in_specs=[any_spec] * n, out_specs=[any_spec] * n,
        out_shape=[SDS((a.shape[0],) + a.shape[2:], a.dtype) for a in arrs],
        scratch_shapes=[pltpu.SemaphoreType.DMA((n,)), pltpu.SemaphoreType.DMA((n,))],
    )(*arrs)


def _pair_add(name, mine, theirs):
    four, _, r, c = mine.shape
    tr = r
    for cand in (512, 256, 128, 64, 32, 16, 8):
        if r % cand == 0:
            tr = cand
            break
    tc = c if 2 * tr * c * 4 <= 4 * 1024 * 1024 else LANES

    def body(m_ref, t_ref, o_ref):
        core = lax.axis_index("c")
        both = m_ref[...].astype(F32)
        own = jnp.where(core == 0, both[0], both[1])
        o_ref[...] = (own + t_ref[...].astype(F32)).astype(o_ref.dtype)

    return pl.pallas_call(
        body, grid=(four, r // tr, c // tc), name=name,
        in_specs=[pl.BlockSpec((None, 2, tr, tc), lambda i, j, k: (i, 0, j, k)),
                  pl.BlockSpec((None, tr, tc), lambda i, j, k: (i, j, k))],
        out_specs=pl.BlockSpec((None, tr, tc), lambda i, j, k: (i, j, k)),
        out_shape=SDS(theirs.shape, theirs.dtype),
        compiler_params=pltpu.CompilerParams(dimension_semantics=("arbitrary",) * 3, vmem_limit_bytes=VMEM_LIMIT),
    )(mine, theirs)


def _my_index():
    return 4 * lax.axis_index("x") + 2 * lax.axis_index("y") + lax.axis_index("c")


def _two_level_copies(stage, dst_cols, src_refs, land_refs, send_sems, recv_sems, landings):
    x, y, c = lax.axis_index("x"), lax.axis_index("y"), lax.axis_index("c")

    def pos(k):
        return (1 - x if k & 4 else x, 1 - y if k & 2 else y, 1 - c if k & 1 else c)

    def idx(k):
        px, py, pc = pos(k)
        return 4 * px + 2 * py + pc

    out = []
    for a, cols in enumerate(dst_cols):
        if stage == 1:
            for i, k in enumerate((1, 2, 4, 6)):
                out.append(pltpu.make_async_remote_copy(
                    src_ref=src_refs[a], dst_ref=_slot(land_refs[a], idx(k) if landings else idx(0), cols),
                    send_sem=send_sems.at[4 * a + i], recv_sem=recv_sems.at[4 * a + i],
                    device_id=pos(k), device_id_type=pl.DeviceIdType.MESH))
        else:
            for i, k in enumerate((2, 4, 6)):
                out.append(pltpu.make_async_remote_copy(
                    src_ref=_slot(land_refs[a], idx(k), cols),
                    dst_ref=_slot(land_refs[a], idx(k ^ 1) if landings else idx(k), cols),
                    send_sem=send_sems.at[3 * a + i], recv_sem=recv_sems.at[3 * a + i],
                    device_id=pos(1), device_id_type=pl.DeviceIdType.MESH))
    return out


def _gather2_start(name, srcs, dsts, after):
    n = len(srcs)
    dst_cols = [c for _, _, c in dsts]

    def body(*refs):
        src_refs, land_refs = refs[:n], refs[n:2 * n]
        send_sems, recv_sems = refs[2 * n + 1:2 * n + 3]
        me = _my_index()
        for a in range(n):
            pltpu.make_async_copy(src_refs[a], _slot(land_refs[a], me, dst_cols[a]), send_sems.at[4 * n + a]).start()
        for cp in _two_level_copies(1, dst_cols, src_refs, land_refs, send_sems, recv_sems, False):
            cp.start()
        refs[-1][...] = jnp.zeros_like(refs[-1])

    hbm = lambda a: pltpu.with_memory_space_constraint(a, pltpu.HBM)
    lands = [hbm(lax.empty(shape, dt)) for shape, dt, _ in dsts]
    res = pl.pallas_call(
        body, name=name,
        out_shape=(pltpu.SemaphoreType.DMA((5 * n,)), pltpu.SemaphoreType.DMA((4 * n,)),
                   *[pltpu.HBM(a.shape, a.dtype) for a, _ in srcs], *[pltpu.HBM(a.shape, a.dtype) for a in lands],
                   SDS((8, LANES), F32)),
        in_specs=[_HBM] * (2 * n) + [pl.BlockSpec(memory_space=pl.ANY)],
        out_specs=(_SEM, _SEM, *[_HBM] * (2 * n), pl.BlockSpec(memory_space=pltpu.VMEM)),
        input_output_aliases={i: 2 + i for i in range(2 * n)},
        compiler_params=pltpu.CompilerParams(has_side_effects=_EFFECT),
    )(*[hbm(a) for a, _ in srcs], *lands, after)
    return (res[0], res[1], res[2:2 + n], res[2 + n:2 + 2 * n], dst_cols), res[-1]


def _gather2_pass(name, handle, after):
    send1, recv1, src_thru, land_thru, dst_cols = handle
    n = len(src_thru)

    def body(*refs):
        src_refs, land_refs = refs[:n], refs[n:2 * n]
        s1, r1 = refs[2 * n:2 * n + 2]
        send2, recv2 = refs[2 * n + 3:2 * n + 5]
        me = _my_index()
        for cp in _two_level_copies(1, dst_cols, src_refs, land_refs, s1, r1, True):
            cp.wait_send()
            cp.wait_recv()
        for a in range(n):
            pltpu.make_async_copy(src_refs[a], _slot(land_refs[a], me, dst_cols[a]), s1.at[4 * n + a]).wait()
        for cp in _two_level_copies(2, dst_cols, src_refs, land_refs, send2, recv2, False):
            cp.start()
        refs[-1][...] = jnp.zeros_like(refs[-1])

    res = pl.pallas_call(
        body, name=name,
        out_shape=(pltpu.SemaphoreType.DMA((3 * n,)), pltpu.SemaphoreType.DMA((3 * n,)),
                   *[pltpu.HBM(a.shape, a.dtype) for a in (*src_thru, *land_thru)], SDS((8, LANES), F32)),
        in_specs=[_HBM] * (2 * n) + [_SEM, _SEM, pl.BlockSpec(memory_space=pl.ANY)],
        out_specs=(_SEM, _SEM, *[_HBM] * (2 * n), pl.BlockSpec(memory_space=pltpu.VMEM)),
        input_output_aliases={i: 2 + i for i in range(2 * n)},
        compiler_params=pltpu.CompilerParams(has_side_effects=_EFFECT),
    )(*src_thru, *land_thru, send1, recv1, after)
    return (res[0], res[1], res[2:2 + n], res[2 + n:2 + 2 * n], dst_cols), res[-1]


def _gather2_wait(name, handle, after):
    send2, recv2, src_thru, land_thru, dst_cols = handle
    n = len(src_thru)

    def body(*refs):
        src_refs, land_refs = refs[:n], refs[n:2 * n]
        s2, r2 = refs[2 * n:2 * n + 2]
        for cp in _two_level_copies(2, dst_cols, src_refs, land_refs, s2, r2, True):
            cp.wait_send()
            cp.wait_recv()

    res = pl.pallas_call(
        body, name=name,
        out_shape=tuple(pltpu.HBM(a.shape, a.dtype) for a in (*src_thru, *land_thru)),
        in_specs=[_HBM] * (2 * n) + [_SEM, _SEM, pl.BlockSpec(memory_space=pl.ANY)],
        out_specs=tuple([_HBM] * (2 * n)),
        input_output_aliases={i: i for i in range(2 * n)},
        compiler_params=pltpu.CompilerParams(has_side_effects=_EFFECT),
    )(*src_thru, *land_thru, send2, recv2, after)
    return res[n:]


def _exchange_sems(n):
    return [pltpu.SemaphoreType.DMA((n, N_DEV - 1)), pltpu.SemaphoreType.DMA((n, N_DEV - 1)),
            pltpu.SemaphoreType.DMA((n,))]


def _exchange_ops(src_cols, dst_cols, gather, src_refs, out_refs, send_sems, recv_sems, local_sems):
    def copies(with_landings):
        me, peers = _peers()
        local, sends, landings = [], [], []
        for a, (s_cols, d_cols) in enumerate(zip(src_cols, dst_cols)):
            mine = src_refs[a] if gather else _slot(src_refs[a], me, s_cols)
            local.append(pltpu.make_async_copy(mine, _slot(out_refs[a], me, d_cols), local_sems.at[a]))
            for k, (pos, idx) in enumerate(peers):
                out_blk = src_refs[a] if gather else _slot(src_refs[a], idx, s_cols)
                both = dict(src_ref=out_blk, send_sem=send_sems.at[a, k], recv_sem=recv_sems.at[a, k],
                            device_id=pos, device_id_type=pl.DeviceIdType.MESH)
                sends.append(pltpu.make_async_remote_copy(dst_ref=_slot(out_refs[a], me, d_cols), **both))
                if with_landings:
                    landings.append(pltpu.make_async_remote_copy(dst_ref=_slot(out_refs[a], idx, d_cols), **both))
        return local, sends, landings

    def start():
        local, sends, _ = copies(False)
        for cp in local + sends:
            cp.start()

    def wait():
        local, sends, landings = copies(True)
        for cp in landings:
            cp.wait_recv()
        for cp in sends:
            cp.wait_send()
        for cp in local:
            cp.wait()

    return start, wait


def _rms_res_fn(h, w):
    return _rms_fn(h, w)[0], h


def _add_epilogue(acc, res):
    return (acc + res,)


def _gather_plan(shards):
    srcs, dsts = [], []
    for n, sh in shards.items():
        r, c = sh.shape
        srcs.append((sh, None))
        if SHARDED[n] and c % LANES == 0:
            dsts.append(((r, N_DEV * c), sh.dtype, c))
        else:
            dsts.append(((N_DEV, r, c), sh.dtype, None))
    return srcs, dsts, True


def _w_in_segments():
    out = []
    for j in range(N_DEV):
        lo, hi = W_IN_SHARD * j, W_IN_SHARD * (j + 1)
        for a, b in ((lo, min(hi, DN_COLS)), (max(lo, DN_COLS), hi)):
            if a < b:
                out.append((j, a - lo, b - lo, a if a < DN_COLS else a + RW_OFF - DN_COLS))
    return out


def _w_in_to_padded(shards, tc=512):
    _, _, cols = shards.shape

    def body(g_ref, o_ref):
        o_ref[...] = jnp.zeros_like(o_ref)
        for j, a, b, dst in _w_in_segments():
            o_ref[dst:dst + b - a, :] = g_ref[j, a:b, :]

    return pl.pallas_call(
        body, grid=(cols // tc,), name="w_in_to_padded",
        in_specs=[pl.BlockSpec((N_DEV, W_IN_SHARD, tc), lambda i: (0, 0, i))],
        out_specs=pl.BlockSpec((IN_PAD, tc), lambda i: (0, i)),
        out_shape=SDS((IN_PAD, cols), shards.dtype),
        compiler_params=pltpu.CompilerParams(dimension_semantics=("arbitrary",), vmem_limit_bytes=VMEM_LIMIT),
    )(shards)


def _w_in_grad_to_shards(gw, tc=512):
    _, cols = gw.shape

    def body(w_ref, o_ref):
        for j, a, b, dst in _w_in_segments():
            o_ref[j, a:b, :] = w_ref[dst:dst + b - a, :]

    return pl.pallas_call(
        body, grid=(cols // tc,), name="w_in_grad_to_shards",
        in_specs=[pl.BlockSpec((IN_PAD, tc), lambda i: (0, i))],
        out_specs=pl.BlockSpec((N_DEV, W_IN_SHARD, tc), lambda i: (0, 0, i)),
        out_shape=SDS((N_DEV, W_IN_SHARD, cols), gw.dtype),
        compiler_params=pltpu.CompilerParams(dimension_semantics=("arbitrary",), vmem_limit_bytes=VMEM_LIMIT),
    )(gw)


def _gather_finish(names, outs):
    full = {}
    for n, arr in zip(names, outs):
        if n == "w_in":
            full[n] = _w_in_to_padded(arr)
        elif arr.ndim == 2:
            full[n] = arr
        elif SHARDED[n]:
            full[n] = arr.transpose(1, 0, 2).reshape(arr.shape[1], -1)
        else:
            full[n] = arr.reshape(-1, arr.shape[2])
    return full


def _scatter_plan(grads):
    srcs, dsts = [], []
    for n, gr in grads.items():
        if gr.ndim == 3:
            srcs.append((gr, None))
            dsts.append((gr.shape, gr.dtype, None))
            continue
        rows, cols = gr.shape
        if not SHARDED[n]:
            r, c = rows // N_DEV, cols
            srcs.append((gr.reshape(N_DEV, r, c), None))
        else:
            r, c = rows, cols // N_DEV
            if c % LANES == 0:
                srcs.append((gr, c))
            else:
                srcs.append((gr.reshape(r, N_DEV, c).transpose(1, 0, 2), None))
        dsts.append(((N_DEV, r, c), gr.dtype, None))
    return srcs, dsts, False


def _local_step(x, mem, target, wt, late):
    d = D_MODEL
    g = {}
    wt = dict(wt)
    grp_a = ("w_out", "xa_wq", "xa_wk", "xa_wv", "xa_wo")
    grp_b = ("ffn_w1", "ffn_w2")
    plan = lambda names: _gather_plan({n: late[n] for n in names})[:2]
    handle_a, tok_a = _gather2_start("late_gather_a_start", *plan(grp_a), wt["w_in"])
    handle_w1, tok_b = _gather2_start("late_gather_w1_start", *plan(("ffn_w1",)), tok_a)
    handle_w2, tok_c = _gather2_start("late_gather_w2_start", *plan(("ffn_w2",)), tok_b)
    mix_w = wt["mix_norm_w"] + (tok_a[0:1, 0:1] + tok_b[0:1, 0:1] + tok_c[0:1, 0:1])
    u = _row_fwd(_rms_fn, "mix_norm", [(x, d, 0)], [mix_w], [(d, BF16)], 256)[0]
    p = _matmul("in_proj", u, wt["w_in"], "nt", [F32], tn=1536)[0]
    c = _col_fwd(_conv_fn, "dn_conv", p, 0, 24, [wt["dn_conv_w"]])
    handle_a, tok = _gather2_pass("late_gather_a_pass", handle_a, c)
    dn_pre_tiles = [(c, DN_WIDTH, 0), (c, DN_WIDTH, 1), (p, LANES, 32)]
    dn_pre_params = [wt["dn_a_log"], wt["dn_dt_bias"]]
    qh, kh, gb, bb, gcb = _row_fwd(_dn_pre_fn, "dn_pre", dn_pre_tiles, [dn_pre_params[0] + tok[0:1, :], dn_pre_params[1]],
                                   [(DN_WIDTH, F32)] * 5, CHUNK)
    dn_arrs = [(qh, 0), (kh, 0), (c, 16), (gb, 0), (bb, 0), (gcb, 0)]
    o, kept_dn = _scan_fwd(_gdn_group, "gdn_scan", dn_arrs, DN_HEADS, 1)
    dn_post_tiles = [(o, DN_WIDTH, 0), (p, DN_WIDTH, 3)]
    o_dn = _row_fwd(_dn_post_fn, "dn_post", dn_post_tiles, [wt["dn_norm_w"]], [(DN_WIDTH, BF16)], 256)[0]

    ps = _col_fwd(_lerp_fn, "rw_shift", p, RW_OFF // LANES, 26, [wt["rw_mu"]])
    rw_pre_tiles = [(ps, RW_WIDTH, 0), (ps, RW_WIDTH, 1), (ps, RW_WIDTH, 2), (ps, LANES, 24), (ps, LANES, 25)]
    rw_pre_params = [wt[n] for n in ("rw_w0", "rw_a0", "rw_k_k", "rw_k_a", "rw_w2", "rw_a2", "rw_g2")]
    r, lw, k, v, al, be, gate, gcw = _row_fwd(_rw_pre_fn, "rw_pre", rw_pre_tiles, rw_pre_params,
                                              [(RW_WIDTH, F32)] * 8, CHUNK)
    rw_arrs = [(r, 0), (lw, 0), (k, 0), (v, 0), (al, 0), (be, 0), (gcw, 0)]
    y, kept_rw = _scan_fwd(_rw_group, "rw_scan", rw_arrs, RW_WIDTH // LANES, 2)
    handle_w1, tok = _gather2_pass("late_gather_w1_pass", handle_w1, y)
    rw_post_tiles = [(t, RW_WIDTH, 0) for t in (y, r, k, v, gate)]
    rw_post_params = [wt["rw_ln_w"], wt["rw_ln_b"], wt["rw_r_k"]]
    o_rw = _row_fwd(_rw_post_fn, "rw_post", rw_post_tiles, [rw_post_params[0] + tok[0:1, 0:1]] + rw_post_params[1:],
                    [(RW_WIDTH, BF16)], 128)[0]
    o_cat = jnp.concatenate([o_dn, o_rw], axis=1)
    wt.update(_gather_finish(grp_a, _gather2_wait("late_gather_a_wait", handle_a, o_cat)))
    h1 = _matmul("out_proj", o_cat, wt["w_out"], "nn", [F32], _add_epilogue, (x,))[0]

    handle_w2, tok = _gather2_pass("late_gather_w2_pass", handle_w2, h1)
    hn = _row_fwd(_rms_fn, "xa_norm", [(h1, d, 0)], [wt["xa_norm_w"] + tok[0:1, 0:1]], [(d, BF16)], 256)[0]
    mn = _row_fwd(_rms_fn, "mem_norm", [(mem, d, 0)], [wt["mem_norm_w"]], [(d, BF16)], 256)[0]
    q = _matmul("xa_q", hn, wt["xa_wq"], "nn", [F32])[0]
    kx = _matmul("xa_k", mn, wt["xa_wk"], "nn", [F32])[0]
    vx = _matmul("xa_v", mn, wt["xa_wv"], "nn", [F32])[0]
    ao = _row_fwd(_xattn_fn, "xattn", [(q, XA_WIDTH, 0)], [kx, vx], [(XA_WIDTH, BF16)], 256)[0]
    h2 = _matmul("xa_o", ao, wt["xa_wo"], "nn", [F32], _add_epilogue, (h1,))[0]

    f = _row_fwd(_rms_fn, "ffn_norm", [(h2, d, 0)], [wt["ffn_norm_w"]], [(d, BF16)], 256)[0]
    wt.update(_gather_finish(("ffn_w1",), _gather2_wait("late_gather_w1_wait", handle_w1, f)))
    a, hid = _matmul("ffn_up", f, wt["ffn_w1"], "nn", [F32, BF16],
                     lambda acc: (acc, jnp.square(jnp.maximum(acc, 0.0))))
    wt.update(_gather_finish(("ffn_w2",), _gather2_wait("late_gather_w2_wait", handle_w2, hid)))
    h3 = _matmul("ffn_down", hid, wt["ffn_w2"], "nn", [F32], _add_epilogue, (h2,))[0]
    loss8, dh3, g["final_norm_w"] = _loss_call(h3, target, wt["final_norm_w"])

    da = _matmul("ffn_down_dx", dh3, wt["ffn_w2"], "nt", [BF16],
                 lambda acc, av: (acc * 2.0 * jnp.maximum(av, 0.0),), (a,))[0]
    g["ffn_w2"] = _matmul("ffn_down_dw", hid, dh3, "tn", [BF16])[0]
    g["ffn_w1"] = _matmul("ffn_up_dw", f, da, "tn", [BF16])[0]
    df = _matmul("ffn_up_dx", da, wt["ffn_w1"], "nt", [F32])[0]
    pending = {}
    plan = _scatter_plan({n: g.pop(n) for n in grp_b})
    pending[grp_b], tok = _exchange_start("late_grad_b_start", *plan, loss8)
    (dh2,), (g["ffn_norm_w"],) = _row_bwd(_rms_res_fn, "ffn_norm_bwd", [(h2, d, 0)],
                                          [wt["ffn_norm_w"] + tok[0:1, 0:1]],
                                          [[(df, d, 0)], [(dh3, d, 0)]], 256)

    dao = _matmul("xa_o_dx", dh2, wt["xa_wo"], "nt", [F32])[0]
    g["xa_wo"] = _matmul("xa_o_dw", ao, dh2, "tn", [BF16])[0]
    (dq,), (dkx, dvx) = _row_bwd(_xattn_fn, "xattn_bwd", [(q, XA_WIDTH, 0)], [kx, vx], [[(dao, XA_WIDTH, 0)]], 256)
    dhn = _matmul("xa_q_dx", dq, wt["xa_wq"], "nt", [F32])[0]
    g["xa_wq"] = _matmul("xa_q_dw", hn, dq, "tn", [BF16])[0]
    g["xa_wk"] = _matmul("xa_k_dw", mn, dkx, "tn", [BF16])[0]
    g["xa_wv"] = _matmul("xa_v_dw", mn, dvx, "tn", [BF16])[0]
    dmn = _matmul("xa_k_dx", dkx, wt["xa_wk"], "nt", [F32])[0]
    dmn = _matmul("xa_v_dx", dvx, wt["xa_wv"], "nt", [F32], _add_epilogue, (dmn,))[0]
    _, (g["mem_norm_w"],) = _row_bwd(_rms_fn, "mem_norm_bwd", [(mem, d, 0)], [wt["mem_norm_w"]],
                                     [[(dmn, d, 0)]], 256, want_tiles=())
    (dh1,), (g["xa_norm_w"],) = _row_bwd(_rms_res_fn, "xa_norm_bwd", [(h1, d, 0)], [wt["xa_norm_w"]],
                                         [[(dhn, d, 0)], [(dh2, d, 0)]], 256)

    do_cat = _matmul("out_proj_dx", dh1, wt["w_out"], "nt", [F32])[0]
    g["w_out"] = _matmul("out_proj_dw", o_cat, dh1, "tn", [BF16])[0]

    plan = _scatter_plan({n: g.pop(n) for n in grp_a})
    pending[grp_a], tok = _exchange_start("late_grad_a_start", *plan, tok)
    (dy, dr1, dk1, dv1, dgate), (g["rw_ln_w"], g["rw_ln_b"], g["rw_r_k"]) = _row_bwd(
        _rw_post_fn, "rw_post_bwd", rw_post_tiles, [rw_post_params[0] + tok[0:1, 0:1]] + rw_post_params[1:],
        [[(do_cat, RW_WIDTH, 1)]], 128)
    dr2, dlw, dk2, dv2, dal, dbe, dgcw = _scan_bwd(_rw_group, "rw_scan_bwd", rw_arrs, kept_rw, dy,
                                                   RW_WIDTH // LANES)
    one = lambda t: [(t, RW_WIDTH, 0)]
    two = lambda s, t: [(s, RW_WIDTH, 0), (t, RW_WIDTH, 0)]
    d_ps, rw_pre_grads = _row_bwd(
        _rw_pre_fn, "rw_pre_bwd", rw_pre_tiles, rw_pre_params,
        [two(dr1, dr2), one(dlw), two(dk1, dk2), two(dv1, dv2), one(dal), one(dbe), one(dgate), one(dgcw)],
        CHUNK)
    for n, val in zip(("rw_w0", "rw_a0", "rw_k_k", "rw_k_a", "rw_w2", "rw_a2", "rw_g2"), rw_pre_grads):
        g[n] = val
    dp_rw, (g["rw_mu"],) = _col_bwd(_lerp_fn, "rw_shift_bwd", p, RW_OFF // LANES, 26, [wt["rw_mu"]],
                                    jnp.concatenate(d_ps, axis=1))

    (do, dz), (g["dn_norm_w"],) = _row_bwd(_dn_post_fn, "dn_post_bwd", dn_post_tiles, [wt["dn_norm_w"]],
                                           [[(do_cat, DN_WIDTH, 0)]], 256)
    dqh, dkh, dv_dn, dgb, dbb, dgcb = _scan_bwd(_gdn_group, "gdn_scan_bwd", dn_arrs, kept_dn, do, DN_HEADS)
    one = lambda t: [(t, DN_WIDTH, 0)]
    (dcq, dck, dgates), (g["dn_a_log"], g["dn_dt_bias"]) = _row_bwd(
        _dn_pre_fn, "dn_pre_bwd", dn_pre_tiles, dn_pre_params,
        [one(dqh), one(dkh), one(dgb), one(dbb), one(dgcb)], CHUNK)
    dp_qkv, (g["dn_conv_w"],) = _col_bwd(_conv_fn, "dn_conv_bwd", p, 0, 24, [wt["dn_conv_w"]],
                                         jnp.concatenate([dcq, dck, dv_dn], axis=1))
    dp = jnp.concatenate([dp_qkv, dz, dgates, dp_rw, jnp.zeros((x.shape[0], LANES), F32)], axis=1).astype(BF16)
    g["w_in"] = _matmul("in_proj_dw", dp, u, "tn", [BF16], tm=1536)[0]
    early = _logical_grads(g)
    blocks = []
    for src, cols in _scatter_plan({n: early.pop(n) for n in EARLY})[0]:
        if cols is not None:
            src = src.reshape(src.shape[0], N_DEV, cols).transpose(1, 0, 2)
        blocks.append(src.reshape((4, 2) + src.shape[1:]))
    sums = [_pair_add("early_grad_pair_add_%d" % i, mine, theirs)
            for i, (mine, theirs) in enumerate(zip(blocks, _pair_swap("early_grad_pair_swap", blocks)))]
    pending[EARLY], tok = _exchange_start("early_grad_start", [(t, None) for t in sums],
                                          [(t.shape, t.dtype, None) for t in sums], False, tok, chips=True)
    du = _matmul("in_proj_dx", dp, wt["w_in"], "nn", [F32], after=tok)[0]
    (dx,), (early["mix_norm_w"],) = _row_bwd(_rms_res_fn, "mix_norm_bwd", [(x, d, 0)], [wt["mix_norm_w"]],
                                             [[(du, d, 0)], [(dh1, d, 0)]], 256)
    return loss8, dx, early, pending, tok


WEIGHTS = ["mix_norm_w", "w_in", "dn_conv_w", "dn_a_log", "dn_dt_bias", "dn_norm_w", "rw_mu", "rw_w0", "rw_w2",
           "rw_a0", "rw_a2", "rw_g2", "rw_k_k", "rw_k_a", "rw_r_k", "rw_ln_w", "rw_ln_b", "w_out", "xa_norm_w",
           "mem_norm_w", "xa_wq", "xa_wk", "xa_wv", "xa_wo", "ffn_norm_w", "ffn_w1", "ffn_w2", "final_norm_w"]
SHARDED = {"w_in": False, "w_out": False, "xa_wq": False, "xa_wk": False, "xa_wv": False, "xa_wo": True,
           "ffn_w1": True, "ffn_w2": False, "dn_conv_w": True, "rw_w2": True, "rw_a2": True, "rw_g2": True}
BF16_PAYLOAD = ("w_in", "w_out", "xa_wq", "xa_wk", "xa_wv", "xa_wo", "ffn_w1", "ffn_w2")
REPLICATED = [n for n in WEIGHTS if n not in SHARDED]
EARLY = ("w_in", "dn_conv_w", "rw_w2", "rw_a2", "rw_g2")
RW_IN_COLS = IN_COLS - DN_COLS
W_IN_SHARD = IN_COLS // N_DEV


def _layout_weights(fw):
    wt = dict(fw)
    wt["dn_conv_w"] = jnp.pad(fw["dn_conv_w"], ((0, 4), (0, 0)))
    wt["dn_a_log"] = jnp.pad(fw["dn_a_log"], ((0, 0), (0, LANES - DN_HEADS)))
    wt["dn_dt_bias"] = jnp.pad(fw["dn_dt_bias"], ((0, 0), (0, LANES - DN_HEADS)))
    wt["rw_w2"] = jnp.pad(fw["rw_w2"], ((0, 64), (0, 0)))
    wt["rw_a2"] = jnp.pad(fw["rw_a2"], ((64, 0), (0, 0)))
    return wt


def _logical_grads(g):
    out = dict(g)
    out["w_in"] = _w_in_grad_to_shards(g["w_in"])
    out["dn_conv_w"] = g["dn_conv_w"][:4]
    out["dn_a_log"] = g["dn_a_log"][:, :DN_HEADS]
    out["dn_dt_bias"] = g["dn_dt_bias"][:, :DN_HEADS]
    out["rw_w2"] = g["rw_w2"][:64]
    out["rw_a2"] = g["rw_a2"][64:]
    return out


def _pack(vals):
    parts = []
    for v in vals:
        flat = v.reshape(-1)
        parts.append(jnp.pad(flat, (0, -flat.shape[0] % LANES)))
    flat = jnp.concatenate(parts)
    flat = jnp.pad(flat, (0, -flat.shape[0] % (8 * LANES)))
    return flat.reshape(-1, LANES)


def _unpack(packed, shapes):
    flat = packed.reshape(-1)
    out, at = [], 0
    for shp in shapes:
        size = math.prod(shp)
        out.append(flat[at:at + size].reshape(shp))
        at += size + (-size % LANES)
    return out


def kernel(x, mem, mix_norm_w, w_in, dn_conv_w, dn_a_log, dn_dt_bias, dn_norm_w, rw_mu, rw_w0, rw_w2, rw_a0, rw_a2, rw_g2, rw_k_k, rw_k_a, rw_r_k, rw_ln_w, rw_ln_b, w_out, xa_norm_w, mem_norm_w, xa_wq, xa_wk, xa_wv, xa_wo, ffn_norm_w, ffn_w1, ffn_w2, final_norm_w, loss_target, m_mix_norm_w, m_w_in, m_dn_conv_w, m_dn_a_log, m_dn_dt_bias, m_dn_norm_w, m_rw_mu, m_rw_w0, m_rw_w2, m_rw_a0, m_rw_a2, m_rw_g2, m_rw_k_k, m_rw_k_a, m_rw_r_k, m_rw_ln_w, m_rw_ln_b, m_w_out, m_xa_norm_w, m_mem_norm_w, m_xa_wq, m_xa_wk, m_xa_wv, m_xa_wo, m_ffn_norm_w, m_ffn_w1, m_ffn_w2, m_final_norm_w, v_mix_norm_w, v_w_in, v_dn_conv_w, v_dn_a_log, v_dn_dt_bias, v_dn_norm_w, v_rw_mu, v_rw_w0, v_rw_w2, v_rw_a0, v_rw_a2, v_rw_g2, v_rw_k_k, v_rw_k_a, v_rw_r_k, v_rw_ln_w, v_rw_ln_b, v_w_out, v_xa_norm_w, v_mem_norm_w, v_xa_wq, v_xa_wk, v_xa_wv, v_xa_wo, v_ffn_norm_w, v_ffn_w1, v_ffn_w2, v_final_norm_w):
    given = dict(locals())
    w = {n: given[n] for n in WEIGHTS}
    m = {n: given["m_" + n] for n in WEIGHTS}
    v = {n: given["v_" + n] for n in WEIGHTS}

    local = {n: (lambda t: t[0].T) if n == "w_in" else (lambda t: t[0]) for n in SHARDED}
    shards = {n: (local[n](w[n]).astype(BF16) if n in BF16_PAYLOAD else local[n](w[n])) for n in SHARDED}
    srcs, dsts, _ = _gather_plan({n: shards[n] for n in EARLY})
    full = _gather_finish(EARLY, _gather_two_level("early_all_gather", srcs, dsts))
    for n in REPLICATED:
        full[n] = w[n].reshape(1, -1)

    loss8, dx, g, pending, after = _local_step(x[0], mem[0], loss_target[0], _layout_weights(full),
                                               {n: shards[n] for n in SHARDED if n not in EARLY})
    loss = lax.psum(loss8[0, 0], ("x", "y", "c"))

    packed = _pack([g[n] for n in REPLICATED])
    small, _ = _exchange_start("small_gather_start", [(packed, None)], [((N_DEV,) + packed.shape, F32, None)], True,
                               after)
    grad, delta, new_m, new_v = {}, {}, {}, {}
    done = [dx]

    def tie():
        return jnp.broadcast_to(sum(t[:1, :1] for t in done), (8, LANES))

    for names in sorted(pending, key=lambda names: names == EARLY):
        handle = pending[names]
        for n, parts in zip(names, _exchange_wait("grad_wait_" + names[0], handle, tie())):
            res = _sum_adamw("adamw_" + n, parts, local[n](w[n]), local[n](m[n]), local[n](v[n]))
            grad[n], delta[n], new_m[n], new_v[n] = [(t.T if n == "w_in" else t)[None] for t in res]
            done.append(res[1])

    (parts,) = _exchange_wait("small_gather_wait", small, tie())
    res = _sum_adamw("adamw_small", parts, _pack([w[n] for n in REPLICATED]),
                     _pack([m[n] for n in REPLICATED]), _pack([v[n] for n in REPLICATED]))
    shapes = [w[n].shape for n in REPLICATED]
    for store, packed_out in zip((grad, delta, new_m, new_v), res):
        for n, val in zip(REPLICATED, _unpack(packed_out, shapes)):
            store[n] = val

    return (loss, dx[None], *[grad[n] for n in WEIGHTS], *[delta[n] for n in WEIGHTS],
            *[new_m[n] for n in WEIGHTS], *[new_v[n] for n in WEIGHTS])
```

```python
import functools
import math

import jax
import jax.numpy as jnp
from jax import lax
from jax.experimental import pallas as pl
from jax.experimental.pallas import tpu as pltpu

F32 = jnp.float32
BF16 = jnp.bfloat16
SDS = jax.ShapeDtypeStruct

N_DEV = 8
D_MODEL = 2048
LANES = 128
CHUNK = 128
DN_HEADS = 8
DN_WIDTH = 1024
RW_WIDTH = 1024
RW_HEAD = 64
XA_HEADS = 4
XA_WIDTH = 512
FFN_HIDDEN = 8192
IN_COLS = 7440
DN_COLS = 4112
IN_PAD = 7680
RW_OFF = 4224
RMS_EPS = 1e-6
RW_GN_EPS = 64e-5
VMEM_LIMIT = 56 * 1024 * 1024

ADAM_LR = 0.001
ADAM_B1 = 0.9
ADAM_B2 = 0.999
ADAM_EPS = 1e-08
ADAM_WD = 0.01
ADAM_STEP = 10

_DIMS = {"nn": (((1,), (0,)), ((), ())), "nt": (((1,), (1,)), ((), ())), "tn": (((0,), (0,)), ((), ()))}


def _raw_dot(a, b, mode, hi):
    if hi:
        return lax.dot_general(a, b, _DIMS[mode], precision=lax.Precision.HIGHEST, preferred_element_type=F32)
    return lax.dot_general(a.astype(BF16), b.astype(BF16), _DIMS[mode], preferred_element_type=F32)


@functools.partial(jax.custom_vjp, nondiff_argnums=(2, 3))
def mm(a, b, mode="nn", hi=False):
    return _raw_dot(a, b, mode, hi)


def _mm_fwd(a, b, mode, hi):
    return _raw_dot(a, b, mode, hi), (a, b)


def _mm_bwd(mode, hi, res, g):
    a, b = res
    if mode == "nn":
        return _raw_dot(g, b, "nt", hi), _raw_dot(a, g, "tn", hi)
    if mode == "nt":
        return _raw_dot(g, b, "nn", hi), _raw_dot(g, a, "tn", hi)
    return _raw_dot(b, g, "nt", hi), _raw_dot(a, g, "nn", hi)


mm.defvjp(_mm_fwd, _mm_bwd)


def _shift_rows_raw(x, k):
    n = x.shape[0]
    rolled = pltpu.roll(x, k % n, axis=0)
    row = lax.broadcasted_iota(jnp.int32, x.shape, 0)
    keep = row >= k if k > 0 else row < n + k
    return jnp.where(keep, rolled, 0.0)


@functools.partial(jax.custom_vjp, nondiff_argnums=(1,))
def shift_rows(x, k):
    return _shift_rows_raw(x, k)


shift_rows.defvjp(lambda x, k: (_shift_rows_raw(x, k), None), lambda k, _, g: (_shift_rows_raw(g, -k),))


def _softplus(x):
    return jnp.maximum(x, 0.0) + jnp.log(1.0 + jnp.exp(-jnp.abs(x)))


def _sigmoid(x):
    return 1.0 / (1.0 + jnp.exp(-x))


def _silu(x):
    return x * _sigmoid(x)


def _tri_masks(n):
    ii = lax.broadcasted_iota(jnp.int32, (n, n), 0)
    jj = lax.broadcasted_iota(jnp.int32, (n, n), 1)
    return ii >= jj, ii > jj, ii == jj


def _neumann_inv_raw(m):
    n = m.shape[0]
    _, _, eye = _tri_masks(n)
    eye = jnp.where(eye, 1.0, 0.0)
    p = eye + m
    mk = m
    for _ in range(int(math.log2(n)) - 1):
        mk = _raw_dot(mk, mk, "nn", False)
        p = p + _raw_dot(p, mk, "nn", False)
    resid = eye - p + _raw_dot(m, p, "nn", True)
    return p + _raw_dot(p, resid, "nn", False)


@jax.custom_vjp
def _neumann_inv(m):
    return _neumann_inv_raw(m)


def _neumann_inv_fwd(m):
    p = _neumann_inv_raw(m)
    return p, p


def _neumann_inv_bwd(p, g):
    return (_raw_dot(_raw_dot(p, g, "tn", False), p, "nt", False),)


_neumann_inv.defvjp(_neumann_inv_fwd, _neumann_inv_bwd)


@jax.custom_vjp
def _saved_inv(m, p):
    return p


_saved_inv.defvjp(lambda m, p: (p, p), lambda p, g: (_neumann_inv_bwd(p, g)[0], jnp.zeros_like(p)))


def _inverse(m, saved):
    return _neumann_inv(m) if saved is None else _saved_inv(m, saved)


def _cumsum_rows(x):
    causal, _, _ = _tri_masks(x.shape[0])
    return mm(jnp.where(causal, 1.0, 0.0), x, "nn", True)


def _gdn_group(s0, q, k, v, gb, bb, gc, *saved):
    diff = jnp.stack([gc[j] - gc[j].T for j in range(gc.shape[0])])
    return jax.vmap(_gdn_chunk)(s0, q, k, v, gb, bb, gc, diff, *saved)


def _rw_group(*args):
    return jax.vmap(_rw_chunk)(*args)


def _gdn_chunk(s0, q, k, v, gb, bb, gc, diff, saved=None):
    c = q.shape[0]
    causal, strict, _ = _tri_masks(c)
    decay = jnp.exp(jnp.where(causal, diff, -jnp.inf))
    kb = k * bb
    a = jnp.where(strict, mm(kb, k, "nt") * decay, 0.0)
    p = _inverse(-a, saved)
    u = mm(p, v * bb)
    w = mm(p, kb * jnp.exp(gc))
    attn = mm(q, k, "nt") * decay
    v_new = u - mm(w, s0)
    o = mm(q * jnp.exp(gc), s0) + mm(attn, v_new)
    g_last = jnp.sum(gb, axis=0, keepdims=True)
    s1 = s0 * jnp.exp(g_last) + mm(k * jnp.exp(g_last - gc), v_new, "tn")
    return o, s1, p


def _rw_chunk(s0, r, lw, k, v, al, be, gc, saved0=None, saved1=None):
    c = r.shape[0]
    causal, strict, _ = _tri_masks(c)
    gp = gc - lw
    row = lax.broadcasted_iota(jnp.int32, lw.shape, 0)
    lane = lax.broadcasted_iota(jnp.int32, lw.shape, 1)
    g_mid = jnp.sum(jnp.where(row < c // 2, lw, 0.0), axis=0, keepdims=True)
    g_last = jnp.sum(lw, axis=0, keepdims=True)
    e_n = jnp.exp(g_mid - gc)
    rg = r * jnp.exp(gc - g_mid)
    bg = be * jnp.exp(gp - g_mid)
    an = al * e_n
    kn = k * e_n
    bt = mm(be * jnp.exp(gp), s0, "nt")
    rt = mm(r * jnp.exp(gc), s0, "nt")
    us, ys, ps = [], [], []
    for h, saved in enumerate((saved0, saved1)):
        mine = (lane >= RW_HEAD) if h else (lane < RW_HEAD)
        bgh = jnp.where(mine, bg, 0.0)
        rgh = jnp.where(mine, rg, 0.0)
        a_ab = jnp.where(strict, mm(bgh, an, "nt"), 0.0)
        a_kb = jnp.where(strict, mm(bgh, kn, "nt"), 0.0)
        a_ra = jnp.where(causal, mm(rgh, an, "nt"), 0.0)
        a_rk = jnp.where(causal, mm(rgh, kn, "nt"), 0.0)
        p = _inverse(a_ab, saved)
        ps.append(p)
        u_h = mm(p, bt + mm(a_kb, v))
        us.append(u_h)
        ys.append(rt + mm(a_ra, u_h) + mm(a_rk, v))
    lo = lane < RW_HEAD
    u = jnp.where(lo, us[0], us[1])
    y = jnp.where(lo, ys[0], ys[1])
    tail = jnp.exp(g_last - gc)
    s1 = s0 * jnp.exp(g_last) + mm(u, al * tail, "tn") + mm(v, k * tail, "tn")
    vi = lax.broadcasted_iota(jnp.int32, s0.shape, 0)
    ki = lax.broadcasted_iota(jnp.int32, s0.shape, 1)
    s1 = jnp.where((vi < RW_HEAD) == (ki < RW_HEAD), s1, 0.0)
    return y, s1, ps[0], ps[1]


SCAN_HB = 8


def _scan_specs(arrs, n_chunks, reverse):
    def spec(off):
        assert off % SCAN_HB == 0
        if reverse:
            return pl.BlockSpec((CHUNK, SCAN_HB * LANES), lambda h, n: (n_chunks - 1 - n, off // SCAN_HB + h))
        return pl.BlockSpec((CHUNK, SCAN_HB * LANES), lambda h, n: (n, off // SCAN_HB + h))
    return [spec(off) for _, off in arrs]


def _split_heads(x):
    return jnp.stack([x[:, LANES * j:LANES * (j + 1)] for j in range(SCAN_HB)], axis=0)


def _merge_heads(x):
    return jnp.concatenate([x[j] for j in range(SCAN_HB)], axis=1)


def _scan_fwd(group_fn, name, arrs, heads, n_kept):
    s = arrs[0][0].shape[0]
    n_chunks = s // CHUNK
    n_in = len(arrs)

    def body(*refs):
        y_ref, st_ref = refs[n_in:n_in + 2]
        kept_refs, s_scr = refs[n_in + 2:-1], refs[-1]

        @pl.when(pl.program_id(1) == 0)
        def _():
            s_scr[...] = jnp.zeros_like(s_scr)

        s0 = s_scr[...]
        st_ref[...] = s0
        y, s1, *kept = group_fn(s0, *[_split_heads(r[...]) for r in refs[:n_in]])
        y_ref[...] = _merge_heads(y)
        s_scr[...] = s1
        for ref, val in zip(kept_refs, kept):
            ref[...] = val

    per_chunk = pl.BlockSpec((SCAN_HB, None, LANES, LANES), lambda h, n: (h, n, 0, 0))
    res = pl.pallas_call(
        body, grid=(heads // SCAN_HB, n_chunks), name=name,
        in_specs=_scan_specs(arrs, n_chunks, False),
        out_specs=[pl.BlockSpec((CHUNK, SCAN_HB * LANES), lambda h, n: (n, h))] + [per_chunk] * (1 + n_kept),
        out_shape=[SDS((s, heads * LANES), F32)] + [SDS((heads, n_chunks, LANES, LANES), F32)] * (1 + n_kept),
        scratch_shapes=[pltpu.VMEM((SCAN_HB, LANES, LANES), F32)],
        compiler_params=pltpu.CompilerParams(dimension_semantics=("arbitrary", "arbitrary")),
    )(*[a for a, _ in arrs])
    return res[0], res[1:]


def _scan_bwd(group_fn, name, arrs, kept, dy, heads):
    s = arrs[0][0].shape[0]
    n_chunks = s // CHUNK
    n_in, n_kept = len(arrs), len(kept)

    def body(*refs):
        kept_vals = [r[...] for r in refs[n_in:n_in + n_kept]]
        dy_ref = refs[n_in + n_kept]
        d_refs = refs[n_in + n_kept + 1:2 * n_in + n_kept + 1]
        ds_scr = refs[-1]

        @pl.when(pl.program_id(1) == 0)
        def _():
            ds_scr[...] = jnp.zeros_like(ds_scr)

        def fn(s0, *ins):
            return group_fn(s0, *ins, *kept_vals[1:])[:2]

        _, vjp = jax.vjp(fn, kept_vals[0], *[_split_heads(r[...]) for r in refs[:n_in]])
        grads = vjp((_split_heads(dy_ref[...]), ds_scr[...]))
        ds_scr[...] = grads[0]
        for ref, g in zip(d_refs, grads[1:]):
            ref[...] = _merge_heads(g)

    rev = pl.BlockSpec((CHUNK, SCAN_HB * LANES), lambda h, n: (n_chunks - 1 - n, h))
    per_chunk = pl.BlockSpec((SCAN_HB, None, LANES, LANES), lambda h, n: (h, n_chunks - 1 - n, 0, 0))
    return pl.pallas_call(
        body, grid=(heads // SCAN_HB, n_chunks), name=name,
        in_specs=_scan_specs(arrs, n_chunks, True) + [per_chunk] * n_kept + [rev],
        out_specs=[rev] * n_in,
        out_shape=[SDS((s, heads * LANES), F32)] * n_in,
        scratch_shapes=[pltpu.VMEM((SCAN_HB, LANES, LANES), F32)],
        compiler_params=pltpu.CompilerParams(dimension_semantics=("arbitrary", "arbitrary")),
    )(*[a for a, _ in arrs], *kept, dy)


def _col_spec(tr, width, cb):
    return pl.BlockSpec((tr, width), lambda i: (i, cb))


def _whole(p):
    return pl.BlockSpec(p.shape, lambda i: (0,) * p.ndim)


def _row_fwd(fn, name, tiles, params, outs, tr):
    rows = tiles[0][0].shape[0]
    nt, npar = len(tiles), len(params)

    def body(*refs):
        vals = [r[...].astype(F32) for r in refs[:nt + npar]]
        for ref, o in zip(refs[nt + npar:], fn(*vals)):
            ref[...] = o.astype(ref.dtype)

    return pl.pallas_call(
        body, grid=(rows // tr,), name=name,
        in_specs=[_col_spec(tr, w, cb) for _, w, cb in tiles] + [_whole(p) for p in params],
        out_specs=[_col_spec(tr, w, 0) for w, _ in outs],
        out_shape=[SDS((rows, w), dt) for w, dt in outs],
        compiler_params=pltpu.CompilerParams(dimension_semantics=("arbitrary",), vmem_limit_bytes=VMEM_LIMIT),
    )(*[a for a, _, _ in tiles], *params)


def _row_bwd(fn, name, tiles, params, cts, tr, want_tiles=None):
    rows = tiles[0][0].shape[0]
    nt, npar = len(tiles), len(params)
    want = list(range(nt)) if want_tiles is None else list(want_tiles)
    flat_cts = [c for group in cts for c in group]
    n_ct = len(flat_cts)

    def body(*refs):
        vals = [r[...].astype(F32) for r in refs[:nt + npar]]
        ct_refs = refs[nt + npar:nt + npar + n_ct]
        out_refs = refs[nt + npar + n_ct:]
        ct_vals, at = [], 0
        for group in cts:
            total = ct_refs[at][...].astype(F32)
            for r in ct_refs[at + 1:at + len(group)]:
                total = total + r[...].astype(F32)
            ct_vals.append(total)
            at += len(group)
        _, vjp = jax.vjp(lambda *a: tuple(fn(*a)), *vals)
        grads = vjp(tuple(ct_vals))
        for ref, t in zip(out_refs[:len(want)], want):
            ref[...] = grads[t]
        first = pl.program_id(0) == 0
        for ref, g in zip(out_refs[len(want):], grads[nt:]):
            @pl.when(first)
            def _(ref=ref, g=g):
                ref[...] = g

            @pl.when(jnp.logical_not(first))
            def _(ref=ref, g=g):
                ref[...] += g

    res = pl.pallas_call(
        body, grid=(rows // tr,), name=name,
        in_specs=[_col_spec(tr, w, cb) for _, w, cb in tiles] + [_whole(p) for p in params]
        + [_col_spec(tr, w, cb) for _, w, cb in flat_cts],
        out_specs=[_col_spec(tr, tiles[t][1], 0) for t in want] + [_whole(p) for p in params],
        out_shape=[SDS((rows, tiles[t][1]), F32) for t in want] + [SDS(p.shape, F32) for p in params],
        compiler_params=pltpu.CompilerParams(dimension_semantics=("arbitrary",), vmem_limit_bytes=VMEM_LIMIT),
    )(*[a for a, _, _ in tiles], *params, *[a for a, _, _ in flat_cts])
    return res[:len(want)], res[len(want):]


def _col_fwd(fn, name, x, first_block, n_blocks, params):
    rows = x.shape[0]

    def body(*refs):
        refs[-1][...] = fn(*[r[...] for r in refs[:-1]])

    return pl.pallas_call(
        body, grid=(n_blocks,), name=name,
        in_specs=[pl.BlockSpec((rows, LANES), lambda j: (0, first_block + j))]
        + [pl.BlockSpec((p.shape[0], LANES), lambda j: (0, j)) for p in params],
        out_specs=pl.BlockSpec((rows, LANES), lambda j: (0, j)),
        out_shape=SDS((rows, n_blocks * LANES), F32),
        compiler_params=pltpu.CompilerParams(dimension_semantics=("arbitrary",), vmem_limit_bytes=VMEM_LIMIT),
    )(x, *params)


def _col_bwd(fn, name, x, first_block, n_blocks, params, dy):
    rows = x.shape[0]
    npar = len(params)

    def body(*refs):
        vals = [r[...] for r in refs[:1 + npar]]
        _, vjp = jax.vjp(fn, *vals)
        grads = vjp(refs[1 + npar][...])
        for ref, g in zip(refs[2 + npar:], grads):
            ref[...] = g

    pspecs = [pl.BlockSpec((p.shape[0], LANES), lambda j: (0, j)) for p in params]
    blk = pl.BlockSpec((rows, LANES), lambda j: (0, j))
    res = pl.pallas_call(
        body, grid=(n_blocks,), name=name,
        in_specs=[pl.BlockSpec((rows, LANES), lambda j: (0, first_block + j))] + pspecs + [blk],
        out_specs=[blk] + pspecs,
        out_shape=[SDS((rows, n_blocks * LANES), F32)] + [SDS(p.shape, F32) for p in params],
        compiler_params=pltpu.CompilerParams(dimension_semantics=("arbitrary",), vmem_limit_bytes=VMEM_LIMIT),
    )(x, *params, dy)
    return res[0], res[1:]


def _conv_fn(x, w):
    acc = x * w[3:4, :]
    for j in range(3):
        acc = acc + shift_rows(x, 3 - j) * w[j:j + 1, :]
    return _silu(acc)


def _lerp_fn(x, mu):
    return x + (shift_rows(x, 1) - x) * mu[0:1, :]


def _seg_sum(x, width):
    if width == LANES:
        return jnp.sum(x, axis=1, keepdims=True)
    lo = lax.broadcasted_iota(jnp.int32, x.shape, 1) < width
    s0 = jnp.sum(jnp.where(lo, x, 0.0), axis=1, keepdims=True)
    s1 = jnp.sum(jnp.where(lo, 0.0, x), axis=1, keepdims=True)
    return jnp.where(lo, s0, s1)


def _per_block(fn, *xs):
    n = xs[0].shape[1] // LANES
    return jnp.concatenate([fn(*[x[:, LANES * b:LANES * (b + 1)] for x in xs]) for b in range(n)], axis=1)


def _head_expand(col0):
    r = lax.broadcasted_iota(jnp.int32, (LANES, DN_WIDTH), 0)
    c = lax.shift_right_logical(lax.broadcasted_iota(jnp.int32, (LANES, DN_WIDTH), 1), 7)
    return jnp.where(r == c + col0, 1.0, 0.0)


def _dn_pre_fn(cq, ck, gates, a_log, dt_bias):
    l2 = lambda x: x * lax.rsqrt(_seg_sum(x * x, LANES) + 1e-6)
    qh = _per_block(l2, cq) * (LANES ** -0.5)
    kh = _per_block(l2, ck)
    g = -jnp.exp(a_log) * _softplus(gates + dt_bias)
    gb = mm(g, _head_expand(0), "nn", True)
    bb = mm(_sigmoid(gates), _head_expand(DN_HEADS), "nn", True)
    return qh, kh, gb, bb, _cumsum_rows(gb)


def _dn_post_fn(o, z, nw):
    def one(ob, zb):
        return ob * lax.rsqrt(_seg_sum(ob * ob, LANES) * (1.0 / LANES) + RMS_EPS) * nw * _silu(zb)
    return (_per_block(one, o, z),)


def _rw_pre_fn(pr, pk, pv, pwa, pg, w0, a0, k_k, k_a, w2p, a2p, g2):
    log_w = -_softplus(-(w0 + mm(jnp.tanh(pwa), w2p))) - 0.5
    lw = -jnp.exp(log_w)
    a = _sigmoid(a0 + mm(pwa, a2p))
    gate = mm(_sigmoid(pg), g2)
    kk = pk * k_k
    kk = _per_block(lambda x: x / jnp.maximum(jnp.sqrt(_seg_sum(x * x, RW_HEAD)), 1e-12), kk)
    k = pk * (1.0 + (a - 1.0) * k_a)
    return pr, lw, k, pv, kk * a, -kk, gate, _cumsum_rows(lw)


def _rw_post_fn(y, r, k, v, gate, ln_w, ln_b, r_k):
    def one(yb, rb, kb, vb, gb, wb, bb, rkb):
        d = yb - _seg_sum(yb, RW_HEAD) * (1.0 / RW_HEAD)
        var = _seg_sum(d * d, RW_HEAD) * (1.0 / RW_HEAD)
        yn = d * lax.rsqrt(var + RW_GN_EPS) * wb + bb
        return (yn + _seg_sum(rb * kb * rkb, RW_HEAD) * vb) * gb
    return (_per_block(one, y, r, k, v, gate, ln_w, ln_b, r_k),)


def _rms_fn(h, w):
    return (h * lax.rsqrt(jnp.mean(h * h, axis=1, keepdims=True) + RMS_EPS) * w,)


def _xattn_fn(q, k, v):
    outs = []
    for h in range(XA_HEADS):
        sl = slice(LANES * h, LANES * (h + 1))
        s = mm(q[:, sl], k[:, sl], "nt") * (LANES ** -0.5)
        e = jnp.exp(s - jnp.max(s, axis=1, keepdims=True))
        outs.append(mm(e / jnp.sum(e, axis=1, keepdims=True), v[:, sl]))
    return (jnp.concatenate(outs, axis=1),)


def _fit(tile, dim):
    best = [t for t in range(LANES, min(tile, dim) + 1, LANES) if dim % t == 0]
    assert best, (tile, dim)
    return best[-1]


def _matmul(name, a, b, mode, out_dtypes, epilogue=None, extras=(), tm=1024, tn=1024, tk=2048, after=None):
    if mode == "tn":
        (k_dim, m), n = a.shape, b.shape[1]
    else:
        (m, k_dim), n = a.shape, (b.shape[1] if mode == "nn" else b.shape[0])
    tm, tn, tk = _fit(tm, m), _fit(tn, n), _fit(tk, k_dim)
    nk = k_dim // tk
    a_spec = (pl.BlockSpec((tk, tm), lambda i, j, k: (k, i)) if mode == "tn"
              else pl.BlockSpec((tm, tk), lambda i, j, k: (i, k)))
    b_spec = (pl.BlockSpec((tn, tk), lambda i, j, k: (j, k)) if mode == "nt"
              else pl.BlockSpec((tk, tn), lambda i, j, k: (k, j)))
    o_spec = pl.BlockSpec((tm, tn), lambda i, j, k: (i, j))
    n_ex, n_out = len(extras), len(out_dtypes)
    ties = [] if after is None else [after]

    def finish(total, rest):
        ex = [r[...].astype(F32) for r in rest[:n_ex]]
        res = epilogue(total, *ex) if epilogue else (total,)
        for ref, o in zip(rest[n_ex + len(ties):n_ex + len(ties) + n_out], res):
            ref[...] = o.astype(ref.dtype)

    def body_single(a_ref, b_ref, *rest):
        finish(_raw_dot(a_ref[...], b_ref[...], mode, False), rest)

    def body_acc(a_ref, b_ref, *rest):
        acc = rest[-1]
        k = pl.program_id(2)

        @pl.when(k == 0)
        def _():
            acc[...] = jnp.zeros_like(acc)

        acc[...] += _raw_dot(a_ref[...], b_ref[...], mode, False)

        @pl.when(k == nk - 1)
        def _():
            finish(acc[...], rest)

    res = pl.pallas_call(
        body_single if nk == 1 else body_acc, grid=(m // tm, n // tn, nk), name=name,
        in_specs=[a_spec, b_spec] + [o_spec] * n_ex + [pl.BlockSpec((8, LANES), lambda i, j, k: (0, 0))] * len(ties),
        out_specs=[o_spec] * n_out,
        out_shape=[SDS((m, n), dt) for dt in out_dtypes],
        scratch_shapes=[] if nk == 1 else [pltpu.VMEM((tm, tn), F32)],
        compiler_params=pltpu.CompilerParams(dimension_semantics=("parallel", "parallel", "arbitrary"),
                                             vmem_limit_bytes=VMEM_LIMIT),
    )(a, b, *extras, *ties)
    return res


def _loss_call(h, target, w, tr=256):
    rows, d = h.shape

    def fn(hv, wv, tv):
        y = _rms_fn(hv, wv)[0]
        return 0.5 * jnp.sum(jnp.mean(jnp.square(y - tv), axis=1, keepdims=True), axis=0, keepdims=True)

    def body(h_ref, t_ref, w_ref, loss_ref, dh_ref, dw_ref):
        tv = t_ref[...]
        val, vjp = jax.vjp(lambda hv, wv: fn(hv, wv, tv), h_ref[...], w_ref[...])
        dh, dw = vjp(jnp.ones((1, 1), F32))
        dh_ref[...] = dh
        first = pl.program_id(0) == 0

        @pl.when(first)
        def _():
            loss_ref[...] = jnp.broadcast_to(val, loss_ref.shape)
            dw_ref[...] = dw

        @pl.when(jnp.logical_not(first))
        def _():
            loss_ref[...] += jnp.broadcast_to(val, loss_ref.shape)
            dw_ref[...] += dw

    return pl.pallas_call(
        body, grid=(rows // tr,), name="loss_head",
        in_specs=[_col_spec(tr, d, 0), _col_spec(tr, d, 0), _whole(w)],
        out_specs=[pl.BlockSpec((8, LANES), lambda i: (0, 0)), _col_spec(tr, d, 0), _whole(w)],
        out_shape=[SDS((8, LANES), F32), SDS((rows, d), F32), SDS(w.shape, F32)],
        compiler_params=pltpu.CompilerParams(dimension_semantics=("arbitrary",), vmem_limit_bytes=VMEM_LIMIT),
    )(h, target, w)


def _adamw_vals(w, g, m, v):
    m = ADAM_B1 * m + (1.0 - ADAM_B1) * g
    v = ADAM_B2 * v + (1.0 - ADAM_B2) * jnp.square(g)
    m_hat = m / (1.0 - ADAM_B1 ** ADAM_STEP)
    v_hat = v / (1.0 - ADAM_B2 ** ADAM_STEP)
    delta = -ADAM_LR * (m_hat / (jnp.sqrt(v_hat) + ADAM_EPS) + ADAM_WD * w)
    return delta, m, v


def _sum_adamw(name, parts, w, m, v):
    r, c = w.shape
    n_parts = parts.shape[0]
    budget = 6 * 1024 * 1024
    tr, tc = r, c
    for cand in (512, 256, 128, 64, 32, 16, 8):
        if r % cand == 0 and n_parts * cand * c * 4 <= budget:
            tr = cand
            break
    if n_parts * tr * c * 4 > budget:
        tc = max(t for t in range(LANES, c + 1, LANES) if c % t == 0 and n_parts * r * t * 4 <= budget)

    def body(p_ref, w_ref, m_ref, v_ref, g_ref, d_ref, m2_ref, v2_ref):
        g = p_ref[0].astype(F32)
        for s in range(1, n_parts):
            g = g + p_ref[s].astype(F32)
        g_ref[...] = g
        d_ref[...], m2_ref[...], v2_ref[...] = _adamw_vals(w_ref[...], g, m_ref[...], v_ref[...])

    blk = pl.BlockSpec((tr, tc), lambda i: (i, 0)) if tc == c else pl.BlockSpec((tr, tc), lambda i: (0, i))
    parts_blk = (pl.BlockSpec((n_parts, tr, tc), lambda i: (0, i, 0)) if tc == c
                 else pl.BlockSpec((n_parts, tr, tc), lambda i: (0, 0, i)))
    return pl.pallas_call(
        body, grid=(r // tr if tc == c else c // tc,), name=name,
        in_specs=[parts_blk, blk, blk, blk],
        out_specs=[blk] * 4, out_shape=[SDS((r, c), F32)] * 4,
        compiler_params=pltpu.CompilerParams(dimension_semantics=("arbitrary",), vmem_limit_bytes=VMEM_LIMIT),
    )(parts, w, m, v)


def _peers():
    x, y, c = lax.axis_index("x"), lax.axis_index("y"), lax.axis_index("c")
    peers = []
    for k in range(1, N_DEV):
        px = 1 - x if k & 4 else x
        py = 1 - y if k & 2 else y
        pc = 1 - c if k & 1 else c
        peers.append(((px, py, pc), 4 * px + 2 * py + pc))
    return 4 * x + 2 * y + c, peers


def _slot(ref, idx, cols):
    if cols is None:
        return ref.at[idx]
    return ref.at[:, pl.ds(pl.multiple_of(idx * cols, LANES), cols)]


def _gather_two_level(name, srcs, dsts):
    n = len(srcs)
    dst_cols = [c for _, _, c in dsts]

    def body(*refs):
        src_refs, out_refs = refs[:n], refs[n:2 * n]
        send_sems, recv_sems, local_sems = refs[2 * n:]
        x, y, c = lax.axis_index("x"), lax.axis_index("y"), lax.axis_index("c")
        index = lambda px, py, pc: 4 * px + 2 * py + pc
        me, sibling = index(x, y, c), (x, y, 1 - c)
        chips = [(x, 1 - y), (1 - x, y), (1 - x, 1 - y)]

        def copy(a, k, src, block, to):
            return pltpu.make_async_remote_copy(
                src_ref=src, dst_ref=_slot(out_refs[a], block, dst_cols[a]),
                send_sem=send_sems.at[a, k], recv_sem=recv_sems.at[a, k],
                device_id=to, device_id_type=pl.DeviceIdType.MESH)

        local, first, passed = [], [], []
        for a in range(n):
            cp = pltpu.make_async_copy(src_refs[a], _slot(out_refs[a], me, dst_cols[a]), local_sems.at[a])
            cp.start()
            local.append(cp)
            first.append(copy(a, 0, src_refs[a], me, sibling))
            first += [copy(a, 1 + j, src_refs[a], me, (*chip, c)) for j, chip in enumerate(chips)]
        for cp in first:
            cp.start()
        for a in range(n):
            for j, chip in enumerate(chips):
                block = index(*chip, c)
                arrived = _slot(out_refs[a], block, dst_cols[a])
                copy(a, 1 + j, arrived, block, (*chip, c)).wait_recv()
                passed.append(copy(a, 4 + j, arrived, block, sibling))
                passed[-1].start()
        for a in range(n):
            copy(a, 0, src_refs[a], index(x, y, 1 - c), sibling).wait_recv()
            for j, chip in enumerate(chips):
                block = index(*chip, 1 - c)
                copy(a, 4 + j, src_refs[a], block, sibling).wait_recv()
        for cp in first + passed:
            cp.wait_send()
        for cp in local:
            cp.wait()

    any_spec = pl.BlockSpec(memory_space=pl.ANY)
    return pl.pallas_call(
        body, name=name,
        in_specs=[any_spec] * n, out_specs=[any_spec] * n,
        out_shape=[SDS(shape, dt) for shape, dt, _ in dsts],
        scratch_shapes=_exchange_sems(n),
    )(*[a for a, _ in srcs])


_HBM = pl.BlockSpec(memory_space=pltpu.HBM)
_SEM = pl.BlockSpec(memory_space=pltpu.SEMAPHORE)
_EFFECT = pltpu.SideEffectType.DATAFLOW_SIDE_EFFECTING


def _split_copies(src_cols, dst_cols, gather, chips, src_refs, land_refs, send_sems, recv_sems, landings):
    me, peers = _peers()
    if chips:
        me, peers = me // 2, [(pos, idx // 2) for k, (pos, idx) in enumerate(peers) if (k + 1) in (2, 4, 6)]
    n, width = len(src_cols), len(peers)
    remote, local = [], []
    for a, (s_cols, d_cols) in enumerate(zip(src_cols, dst_cols)):
        mine = src_refs[a] if gather else _slot(src_refs[a], me, s_cols)
        local.append(pltpu.make_async_copy(mine, _slot(land_refs[a], me, d_cols), send_sems.at[n * width + a]))
        for k, (pos, idx) in enumerate(peers):
            blk = src_refs[a] if gather else _slot(src_refs[a], idx, s_cols)
            remote.append(pltpu.make_async_remote_copy(
                src_ref=blk, dst_ref=_slot(land_refs[a], idx if landings else me, d_cols),
                send_sem=send_sems.at[a * width + k], recv_sem=recv_sems.at[a * width + k],
                device_id=pos, device_id_type=pl.DeviceIdType.MESH))
    return remote, local


def _exchange_start(name, srcs, dsts, gather, after, chips=False):
    n = len(srcs)
    src_cols, dst_cols = [c for _, c in srcs], [c for _, _, c in dsts]
    width = 3 if chips else N_DEV - 1

    def body(*refs):
        src_refs, land_refs = refs[:n], refs[n:2 * n]
        send_sems, recv_sems = refs[2 * n + 1:2 * n + 3]
        token = refs[-1]
        remote, local = _split_copies(src_cols, dst_cols, gather, chips, src_refs, land_refs, send_sems, recv_sems,
                                      False)
        for cp in remote + local:
            cp.start()
        token[...] = jnp.zeros_like(token)

    hbm = lambda a: pltpu.with_memory_space_constraint(a, pltpu.HBM)
    lands = [hbm(lax.empty(shape, dt)) for shape, dt, _ in dsts]
    res = pl.pallas_call(
        body, name=name,
        out_shape=(pltpu.SemaphoreType.DMA((n * (width + 1),)), pltpu.SemaphoreType.DMA((n * width,)),
                   *[pltpu.HBM(a.shape, a.dtype) for a, _ in srcs], *[pltpu.HBM(a.shape, a.dtype) for a in lands],
                   SDS((8, LANES), F32)),
        in_specs=[_HBM] * (2 * n) + [pl.BlockSpec(memory_space=pl.ANY)],
        out_specs=(_SEM, _SEM, *[_HBM] * (2 * n), pl.BlockSpec(memory_space=pltpu.VMEM)),
        input_output_aliases={i: 2 + i for i in range(2 * n)},
        compiler_params=pltpu.CompilerParams(has_side_effects=_EFFECT),
    )(*[hbm(a) for a, _ in srcs], *lands, after)
    handle = (res[0], res[1], res[2:2 + n], res[2 + n:2 + 2 * n], src_cols, dst_cols, gather, chips)
    return handle, res[-1]


def _exchange_wait(name, handle, after):
    send_sems, recv_sems, src_thru, land_thru, src_cols, dst_cols, gather, chips = handle
    n = len(src_thru)

    def body(*refs):
        src_refs, land_refs = refs[:n], refs[n:2 * n]
        s_sems, r_sems = refs[2 * n:2 * n + 2]
        remote, local = _split_copies(src_cols, dst_cols, gather, chips, src_refs, land_refs, s_sems, r_sems, True)
        for cp in remote:
            cp.wait_send()
            cp.wait_recv()
        for cp in local:
            cp.wait()

    res = pl.pallas_call(
        body, name=name,
        out_shape=tuple(pltpu.HBM(a.shape, a.dtype) for a in (*src_thru, *land_thru)),
        in_specs=[_HBM] * (2 * n) + [_SEM, _SEM, pl.BlockSpec(memory_space=pl.ANY)],
        out_specs=tuple([_HBM] * (2 * n)),
        input_output_aliases={i: i for i in range(2 * n)},
        compiler_params=pltpu.CompilerParams(has_side_effects=_EFFECT),
    )(*src_thru, *land_thru, send_sems, recv_sems, after)
    return res[n:]


def _pair_swap(name, arrs):
    n = len(arrs)

    def body(*refs):
        src_refs, out_refs = refs[:n], refs[n:2 * n]
        send_sems, recv_sems = refs[2 * n:]
        x, y, c = lax.axis_index("x"), lax.axis_index("y"), lax.axis_index("c")
        copies = [pltpu.make_async_remote_copy(
            src_ref=src_refs[a].at[:, 1 - c], dst_ref=out_refs[a], send_sem=send_sems.at[a], recv_sem=recv_sems.at[a],
            device_id=(x, y, 1 - c), device_id_type=pl.DeviceIdType.MESH) for a in range(n)]
        for cp in copies:
            cp.start()
        for cp in copies:
            cp.wait()

    any_spec = pl.BlockSpec(memory_space=pl.ANY)
    return pl.pallas_call(
        body, name=name,
        in_specs=[any_spec] * n, out_specs=[any_spec] * n,
        out_shape=[SDS((a.shape[0],) + a.shape[2:], a.dtype) for a in arrs],
        scratch_shapes=[pltpu.SemaphoreType.DMA((n,)), pltpu.SemaphoreType.DMA((n,))],
    )(*arrs)


def _pair_add(name, mine, theirs):
    four, _, r, c = mine.shape
    tr = r
    for cand in (512, 256, 128, 64, 32, 16, 8):
        if r % cand == 0:
            tr = cand
            break
    tc = max(t for t in range(LANES, c + 1, LANES) if c % t == 0 and (t == LANES or 2 * tr * t * 4 <= 4 * 1024 * 1024))

    def body(m_ref, t_ref, o_ref):
        core = lax.axis_index("c")
        both = m_ref[...].astype(F32)
        own = jnp.where(core == 0, both[0], both[1])
        o_ref[...] = (own + t_ref[...].astype(F32)).astype(o_ref.dtype)

    return pl.pallas_call(
        body, grid=(four, r // tr, c // tc), name=name,
        in_specs=[pl.BlockSpec((None, 2, tr, tc), lambda i, j, k: (i, 0, j, k)),
                  pl.BlockSpec((None, tr, tc), lambda i, j, k: (i, j, k))],
        out_specs=pl.BlockSpec((None, tr, tc), lambda i, j, k: (i, j, k)),
        out_shape=SDS(theirs.shape, theirs.dtype),
        compiler_params=pltpu.CompilerParams(dimension_semantics=("arbitrary",) * 3, vmem_limit_bytes=VMEM_LIMIT),
    )(mine, theirs)


def _my_index():
    return 4 * lax.axis_index("x") + 2 * lax.axis_index("y") + lax.axis_index("c")


def _two_level_copies(stage, dst_cols, src_refs, land_refs, send_sems, recv_sems, landings):
    x, y, c = lax.axis_index("x"), lax.axis_index("y"), lax.axis_index("c")

    def pos(k):
        return (1 - x if k & 4 else x, 1 - y if k & 2 else y, 1 - c if k & 1 else c)

    def idx(k):
        px, py, pc = pos(k)
        return 4 * px + 2 * py + pc

    out = []
    for a, cols in enumerate(dst_cols):
        if stage == 1:
            for i, k in enumerate((1, 2, 4, 6)):
                out.append(pltpu.make_async_remote_copy(
                    src_ref=src_refs[a], dst_ref=_slot(land_refs[a], idx(k) if landings else idx(0), cols),
                    send_sem=send_sems.at[4 * a + i], recv_sem=recv_sems.at[4 * a + i],
                    device_id=pos(k), device_id_type=pl.DeviceIdType.MESH))
        else:
            for i, k in enumerate((2, 4, 6)):
                out.append(pltpu.make_async_remote_copy(
                    src_ref=_slot(land_refs[a], idx(k), cols),
                    dst_ref=_slot(land_refs[a], idx(k ^ 1) if landings else idx(k), cols),
                    send_sem=send_sems.at[3 * a + i], recv_sem=recv_sems.at[3 * a + i],
                    device_id=pos(1), device_id_type=pl.DeviceIdType.MESH))
    return out


def _gather2_start(name, srcs, dsts, after):
    n = len(srcs)
    dst_cols = [c for _, _, c in dsts]

    def body(*refs):
        src_refs, land_refs = refs[:n], refs[n:2 * n]
        send_sems, recv_sems = refs[2 * n + 1:2 * n + 3]
        me = _my_index()
        for a in range(n):
            pltpu.make_async_copy(src_refs[a], _slot(land_refs[a], me, dst_cols[a]), send_sems.at[4 * n + a]).start()
        for cp in _two_level_copies(1, dst_cols, src_refs, land_refs, send_sems, recv_sems, False):
            cp.start()
        refs[-1][...] = jnp.zeros_like(refs[-1])

    hbm = lambda a: pltpu.with_memory_space_constraint(a, pltpu.HBM)
    lands = [hbm(lax.empty(shape, dt)) for shape, dt, _ in dsts]
    res = pl.pallas_call(
        body, name=name,
        out_shape=(pltpu.SemaphoreType.DMA((5 * n,)), pltpu.SemaphoreType.DMA((4 * n,)),
                   *[pltpu.HBM(a.shape, a.dtype) for a, _ in srcs], *[pltpu.HBM(a.shape, a.dtype) for a in lands],
                   SDS((8, LANES), F32)),
        in_specs=[_HBM] * (2 * n) + [pl.BlockSpec(memory_space=pl.ANY)],
        out_specs=(_SEM, _SEM, *[_HBM] * (2 * n), pl.BlockSpec(memory_space=pltpu.VMEM)),
        input_output_aliases={i: 2 + i for i in range(2 * n)},
        compiler_params=pltpu.CompilerParams(has_side_effects=_EFFECT),
    )(*[hbm(a) for a, _ in srcs], *lands, after)
    return (res[0], res[1], res[2:2 + n], res[2 + n:2 + 2 * n], dst_cols), res[-1]


def _gather2_pass(name, handle, after):
    send1, recv1, src_thru, land_thru, dst_cols = handle
    n = len(src_thru)

    def body(*refs):
        src_refs, land_refs = refs[:n], refs[n:2 * n]
        s1, r1 = refs[2 * n:2 * n + 2]
        send2, recv2 = refs[2 * n + 3:2 * n + 5]
        me = _my_index()
        for cp in _two_level_copies(1, dst_cols, src_refs, land_refs, s1, r1, True):
            cp.wait_send()
            cp.wait_recv()
        for a in range(n):
            pltpu.make_async_copy(src_refs[a], _slot(land_refs[a], me, dst_cols[a]), s1.at[4 * n + a]).wait()
        for cp in _two_level_copies(2, dst_cols, src_refs, land_refs, send2, recv2, False):
            cp.start()
        refs[-1][...] = jnp.zeros_like(refs[-1])

    res = pl.pallas_call(
        body, name=name,
        out_shape=(pltpu.SemaphoreType.DMA((3 * n,)), pltpu.SemaphoreType.DMA((3 * n,)),
                   *[pltpu.HBM(a.shape, a.dtype) for a in (*src_thru, *land_thru)], SDS((8, LANES), F32)),
        in_specs=[_HBM] * (2 * n) + [_SEM, _SEM, pl.BlockSpec(memory_space=pl.ANY)],
        out_specs=(_SEM, _SEM, *[_HBM] * (2 * n), pl.BlockSpec(memory_space=pltpu.VMEM)),
        input_output_aliases={i: 2 + i for i in range(2 * n)},
        compiler_params=pltpu.CompilerParams(has_side_effects=_EFFECT),
    )(*src_thru, *land_thru, send1, recv1, after)
    return (res[0], res[1], res[2:2 + n], res[2 + n:2 + 2 * n], dst_cols), res[-1]


def _gather2_wait(name, handle, after):
    send2, recv2, src_thru, land_thru, dst_cols = handle
    n = len(src_thru)

    def body(*refs):
        src_refs, land_refs = refs[:n], refs[n:2 * n]
        s2, r2 = refs[2 * n:2 * n + 2]
        for cp in _two_level_copies(2, dst_cols, src_refs, land_refs, s2, r2, True):
            cp.wait_send()
            cp.wait_recv()

    res = pl.pallas_call(
        body, name=name,
        out_shape=tuple(pltpu.HBM(a.shape, a.dtype) for a in (*src_thru, *land_thru)),
        in_specs=[_HBM] * (2 * n) + [_SEM, _SEM, pl.BlockSpec(memory_space=pl.ANY)],
        out_specs=tuple([_HBM] * (2 * n)),
        input_output_aliases={i: i for i in range(2 * n)},
        compiler_params=pltpu.CompilerParams(has_side_effects=_EFFECT),
    )(*src_thru, *land_thru, send2, recv2, after)
    return res[n:]


def _exchange_sems(n):
    return [pltpu.SemaphoreType.DMA((n, N_DEV - 1)), pltpu.SemaphoreType.DMA((n, N_DEV - 1)),
            pltpu.SemaphoreType.DMA((n,))]


def _rms_res_fn(h, w):
    return _rms_fn(h, w)[0], h


def _add_epilogue(acc, res):
    return (acc + res,)


def _gather_plan(shards):
    srcs, dsts = [], []
    for n, sh in shards.items():
        r, c = sh.shape
        srcs.append((sh, None))
        if SHARDED[n] and c % LANES == 0:
            dsts.append(((r, N_DEV * c), sh.dtype, c))
        else:
            dsts.append(((N_DEV, r, c), sh.dtype, None))
    return srcs, dsts, True


def _w_in_segments():
    out = []
    for j in range(N_DEV):
        lo, hi = W_IN_SHARD * j, W_IN_SHARD * (j + 1)
        for a, b in ((lo, min(hi, DN_COLS)), (max(lo, DN_COLS), hi)):
            if a < b:
                out.append((j, a - lo, b - lo, a if a < DN_COLS else a + RW_OFF - DN_COLS))
    return out


def _w_in_to_padded(shards, tc=512):
    _, _, cols = shards.shape

    def body(g_ref, o_ref):
        o_ref[...] = jnp.zeros_like(o_ref)
        for j, a, b, dst in _w_in_segments():
            o_ref[dst:dst + b - a, :] = g_ref[j, a:b, :]

    return pl.pallas_call(
        body, grid=(cols // tc,), name="w_in_to_padded",
        in_specs=[pl.BlockSpec((N_DEV, W_IN_SHARD, tc), lambda i: (0, 0, i))],
        out_specs=pl.BlockSpec((IN_PAD, tc), lambda i: (0, i)),
        out_shape=SDS((IN_PAD, cols), shards.dtype),
        compiler_params=pltpu.CompilerParams(dimension_semantics=("arbitrary",), vmem_limit_bytes=VMEM_LIMIT),
    )(shards)


def _w_in_grad_to_shards(gw, tc=512):
    _, cols = gw.shape

    def body(w_ref, o_ref):
        for j, a, b, dst in _w_in_segments():
            o_ref[j, a:b, :] = w_ref[dst:dst + b - a, :]

    return pl.pallas_call(
        body, grid=(cols // tc,), name="w_in_grad_to_shards",
        in_specs=[pl.BlockSpec((IN_PAD, tc), lambda i: (0, i))],
        out_specs=pl.BlockSpec((N_DEV, W_IN_SHARD, tc), lambda i: (0, 0, i)),
        out_shape=SDS((N_DEV, W_IN_SHARD, cols), gw.dtype),
        compiler_params=pltpu.CompilerParams(dimension_semantics=("arbitrary",), vmem_limit_bytes=VMEM_LIMIT),
    )(gw)


def _gather_finish(names, outs):
    full = {}
    for n, arr in zip(names, outs):
        if n == "w_in":
            full[n] = _w_in_to_padded(arr)
        elif arr.ndim == 2:
            full[n] = arr
        elif SHARDED[n]:
            full[n] = arr.transpose(1, 0, 2).reshape(arr.shape[1], -1)
        else:
            full[n] = arr.reshape(-1, arr.shape[2])
    return full


def _scatter_plan(grads):
    srcs, dsts = [], []
    for n, gr in grads.items():
        if gr.ndim == 3:
            srcs.append((gr, None))
            dsts.append((gr.shape, gr.dtype, None))
            continue
        rows, cols = gr.shape
        if not SHARDED[n]:
            r, c = rows // N_DEV, cols
            srcs.append((gr.reshape(N_DEV, r, c), None))
        else:
            r, c = rows, cols // N_DEV
            if c % LANES == 0:
                srcs.append((gr, c))
            else:
                srcs.append((gr.reshape(r, N_DEV, c).transpose(1, 0, 2), None))
        dsts.append(((N_DEV, r, c), gr.dtype, None))
    return srcs, dsts, False


def _local_step(x, mem, target, wt, late):
    d = D_MODEL
    g = {}
    wt = dict(wt)
    grp_a = ("w_out", "xa_wq", "xa_wk", "xa_wv", "xa_wo")
    grp_b = ("ffn_w1", "ffn_w2")
    plan = lambda names: _gather_plan({n: late[n] for n in names})[:2]
    handle_a, tok_a = _gather2_start("late_gather_a_start", *plan(grp_a), wt["w_in"])
    handle_w1, tok_b = _gather2_start("late_gather_w1_start", *plan(("ffn_w1",)), tok_a)
    handle_w2, tok_c = _gather2_start("late_gather_w2_start", *plan(("ffn_w2",)), tok_b)
    mix_w = wt["mix_norm_w"] + (tok_a[0:1, 0:1] + tok_b[0:1, 0:1] + tok_c[0:1, 0:1])
    u = _row_fwd(_rms_fn, "mix_norm", [(x, d, 0)], [mix_w], [(d, BF16)], 256)[0]
    p = _matmul("in_proj", u, wt["w_in"], "nt", [F32], tn=1536)[0]
    c = _col_fwd(_conv_fn, "dn_conv", p, 0, 24, [wt["dn_conv_w"]])
    handle_a, tok = _gather2_pass("late_gather_a_pass", handle_a, c)
    dn_pre_tiles = [(c, DN_WIDTH, 0), (c, DN_WIDTH, 1), (p, LANES, 32)]
    dn_pre_params = [wt["dn_a_log"], wt["dn_dt_bias"]]
    qh, kh, gb, bb, gcb = _row_fwd(_dn_pre_fn, "dn_pre", dn_pre_tiles, [dn_pre_params[0] + tok[0:1, :], dn_pre_params[1]],
                                   [(DN_WIDTH, F32)] * 5, CHUNK)
    dn_arrs = [(qh, 0), (kh, 0), (c, 16), (gb, 0), (bb, 0), (gcb, 0)]
    o, kept_dn = _scan_fwd(_gdn_group, "gdn_scan", dn_arrs, DN_HEADS, 1)
    dn_post_tiles = [(o, DN_WIDTH, 0), (p, DN_WIDTH, 3)]
    o_dn = _row_fwd(_dn_post_fn, "dn_post", dn_post_tiles, [wt["dn_norm_w"]], [(DN_WIDTH, BF16)], 256)[0]

    ps = _col_fwd(_lerp_fn, "rw_shift", p, RW_OFF // LANES, 26, [wt["rw_mu"]])
    rw_pre_tiles = [(ps, RW_WIDTH, 0), (ps, RW_WIDTH, 1), (ps, RW_WIDTH, 2), (ps, LANES, 24), (ps, LANES, 25)]
    rw_pre_params = [wt[n] for n in ("rw_w0", "rw_a0", "rw_k_k", "rw_k_a", "rw_w2", "rw_a2", "rw_g2")]
    r, lw, k, v, al, be, gate, gcw = _row_fwd(_rw_pre_fn, "rw_pre", rw_pre_tiles, rw_pre_params,
                                              [(RW_WIDTH, F32)] * 8, CHUNK)
    rw_arrs = [(r, 0), (lw, 0), (k, 0), (v, 0), (al, 0), (be, 0), (gcw, 0)]
    y, kept_rw = _scan_fwd(_rw_group, "rw_scan", rw_arrs, RW_WIDTH // LANES, 2)
    handle_w1, tok = _gather2_pass("late_gather_w1_pass", handle_w1, y)
    rw_post_tiles = [(t, RW_WIDTH, 0) for t in (y, r, k, v, gate)]
    rw_post_params = [wt["rw_ln_w"], wt["rw_ln_b"], wt["rw_r_k"]]
    o_rw = _row_fwd(_rw_post_fn, "rw_post", rw_post_tiles, [rw_post_params[0] + tok[0:1, 0:1]] + rw_post_params[1:],
                    [(RW_WIDTH, BF16)], 128)[0]
    o_cat = jnp.concatenate([o_dn, o_rw], axis=1)
    wt.update(_gather_finish(grp_a, _gather2_wait("late_gather_a_wait", handle_a, o_cat)))
    h1 = _matmul("out_proj", o_cat, wt["w_out"], "nn", [F32], _add_epilogue, (x,))[0]

    handle_w2, tok = _gather2_pass("late_gather_w2_pass", handle_w2, h1)
    hn = _row_fwd(_rms_fn, "xa_norm", [(h1, d, 0)], [wt["xa_norm_w"] + tok[0:1, 0:1]], [(d, BF16)], 256)[0]
    mn = _row_fwd(_rms_fn, "mem_norm", [(mem, d, 0)], [wt["mem_norm_w"]], [(d, BF16)], 256)[0]
    q = _matmul("xa_q", hn, wt["xa_wq"], "nn", [F32])[0]
    kx = _matmul("xa_k", mn, wt["xa_wk"], "nn", [F32])[0]
    vx = _matmul("xa_v", mn, wt["xa_wv"], "nn", [F32])[0]
    ao = _row_fwd(_xattn_fn, "xattn", [(q, XA_WIDTH, 0)], [kx, vx], [(XA_WIDTH, BF16)], 256)[0]
    h2 = _matmul("xa_o", ao, wt["xa_wo"], "nn", [F32], _add_epilogue, (h1,))[0]

    f = _row_fwd(_rms_fn, "ffn_norm", [(h2, d, 0)], [wt["ffn_norm_w"]], [(d, BF16)], 256)[0]
    wt.update(_gather_finish(("ffn_w1",), _gather2_wait("late_gather_w1_wait", handle_w1, f)))
    a, hid = _matmul("ffn_up", f, wt["ffn_w1"], "nn", [F32, BF16],
                     lambda acc: (acc, jnp.square(jnp.maximum(acc, 0.0))))
    wt.update(_gather_finish(("ffn_w2",), _gather2_wait("late_gather_w2_wait", handle_w2, hid)))
    h3 = _matmul("ffn_down", hid, wt["ffn_w2"], "nn", [F32], _add_epilogue, (h2,))[0]
    loss8, dh3, g["final_norm_w"] = _loss_call(h3, target, wt["final_norm_w"])

    da = _matmul("ffn_down_dx", dh3, wt["ffn_w2"], "nt", [BF16],
                 lambda acc, av: (acc * 2.0 * jnp.maximum(av, 0.0),), (a,))[0]
    g["ffn_w2"] = _matmul("ffn_down_dw", hid, dh3, "tn", [BF16])[0]
    g["ffn_w1"] = _matmul("ffn_up_dw", f, da, "tn", [BF16])[0]
    df = _matmul("ffn_up_dx", da, wt["ffn_w1"], "nt", [F32])[0]
    pending = {}
    plan = _scatter_plan({n: g.pop(n) for n in grp_b})
    pending[grp_b], tok = _exchange_start("late_grad_b_start", *plan, loss8)
    (dh2,), (g["ffn_norm_w"],) = _row_bwd(_rms_res_fn, "ffn_norm_bwd", [(h2, d, 0)],
                                          [wt["ffn_norm_w"] + tok[0:1, 0:1]],
                                          [[(df, d, 0)], [(dh3, d, 0)]], 256)

    dao = _matmul("xa_o_dx", dh2, wt["xa_wo"], "nt", [F32])[0]
    g["xa_wo"] = _matmul("xa_o_dw", ao, dh2, "tn", [BF16])[0]
    (dq,), (dkx, dvx) = _row_bwd(_xattn_fn, "xattn_bwd", [(q, XA_WIDTH, 0)], [kx, vx], [[(dao, XA_WIDTH, 0)]], 256)
    dhn = _matmul("xa_q_dx", dq, wt["xa_wq"], "nt", [F32])[0]
    g["xa_wq"] = _matmul("xa_q_dw", hn, dq, "tn", [BF16])[0]
    g["xa_wk"] = _matmul("xa_k_dw", mn, dkx, "tn", [BF16])[0]
    g["xa_wv"] = _matmul("xa_v_dw", mn, dvx, "tn", [BF16])[0]
    dmn = _matmul("xa_k_dx", dkx, wt["xa_wk"], "nt", [F32])[0]
    dmn = _matmul("xa_v_dx", dvx, wt["xa_wv"], "nt", [F32], _add_epilogue, (dmn,))[0]
    _, (g["mem_norm_w"],) = _row_bwd(_rms_fn, "mem_norm_bwd", [(mem, d, 0)], [wt["mem_norm_w"]],
                                     [[(dmn, d, 0)]], 256, want_tiles=())
    (dh1,), (g["xa_norm_w"],) = _row_bwd(_rms_res_fn, "xa_norm_bwd", [(h1, d, 0)], [wt["xa_norm_w"]],
                                         [[(dhn, d, 0)], [(dh2, d, 0)]], 256)

    do_cat = _matmul("out_proj_dx", dh1, wt["w_out"], "nt", [F32])[0]
    g["w_out"] = _matmul("out_proj_dw", o_cat, dh1, "tn", [BF16])[0]

    plan = _scatter_plan({n: g.pop(n) for n in grp_a})
    pending[grp_a], tok = _exchange_start("late_grad_a_start", *plan, tok)
    (dy, dr1, dk1, dv1, dgate), (g["rw_ln_w"], g["rw_ln_b"], g["rw_r_k"]) = _row_bwd(
        _rw_post_fn, "rw_post_bwd", rw_post_tiles, [rw_post_params[0] + tok[0:1, 0:1]] + rw_post_params[1:],
        [[(do_cat, RW_WIDTH, 1)]], 128)
    dr2, dlw, dk2, dv2, dal, dbe, dgcw = _scan_bwd(_rw_group, "rw_scan_bwd", rw_arrs, kept_rw, dy,
                                                   RW_WIDTH // LANES)
    one = lambda t: [(t, RW_WIDTH, 0)]
    two = lambda s, t: [(s, RW_WIDTH, 0), (t, RW_WIDTH, 0)]
    d_ps, rw_pre_grads = _row_bwd(
        _rw_pre_fn, "rw_pre_bwd", rw_pre_tiles, rw_pre_params,
        [two(dr1, dr2), one(dlw), two(dk1, dk2), two(dv1, dv2), one(dal), one(dbe), one(dgate), one(dgcw)],
        CHUNK)
    for n, val in zip(("rw_w0", "rw_a0", "rw_k_k", "rw_k_a", "rw_w2", "rw_a2", "rw_g2"), rw_pre_grads):
        g[n] = val
    dp_rw, (g["rw_mu"],) = _col_bwd(_lerp_fn, "rw_shift_bwd", p, RW_OFF // LANES, 26, [wt["rw_mu"]],
                                    jnp.concatenate(d_ps, axis=1))

    (do, dz), (g["dn_norm_w"],) = _row_bwd(_dn_post_fn, "dn_post_bwd", dn_post_tiles, [wt["dn_norm_w"]],
                                           [[(do_cat, DN_WIDTH, 0)]], 256)
    dqh, dkh, dv_dn, dgb, dbb, dgcb = _scan_bwd(_gdn_group, "gdn_scan_bwd", dn_arrs, kept_dn, do, DN_HEADS)
    one = lambda t: [(t, DN_WIDTH, 0)]
    (dcq, dck, dgates), (g["dn_a_log"], g["dn_dt_bias"]) = _row_bwd(
        _dn_pre_fn, "dn_pre_bwd", dn_pre_tiles, dn_pre_params,
        [one(dqh), one(dkh), one(dgb), one(dbb), one(dgcb)], CHUNK)
    dp_qkv, (g["dn_conv_w"],) = _col_bwd(_conv_fn, "dn_conv_bwd", p, 0, 24, [wt["dn_conv_w"]],
                                         jnp.concatenate([dcq, dck, dv_dn], axis=1))
    dp = jnp.concatenate([dp_qkv, dz, dgates, dp_rw, jnp.zeros((x.shape[0], LANES), F32)], axis=1).astype(BF16)
    g["w_in"] = _matmul("in_proj_dw", dp, u, "tn", [BF16], tm=1536)[0]
    early = _logical_grads(g)
    blocks = []
    for src, cols in _scatter_plan({n: early.pop(n) for n in EARLY})[0]:
        if cols is not None:
            src = src.reshape(src.shape[0], N_DEV, cols).transpose(1, 0, 2)
        blocks.append(src.reshape((4, 2) + src.shape[1:]))
    sums = [_pair_add("early_grad_pair_add_%d" % i, mine, theirs)
            for i, (mine, theirs) in enumerate(zip(blocks, _pair_swap("early_grad_pair_swap", blocks)))]
    pending[EARLY], tok = _exchange_start("early_grad_start", [(t, None) for t in sums],
                                          [(t.shape, t.dtype, None) for t in sums], False, tok, chips=True)
    du = _matmul("in_proj_dx", dp, wt["w_in"], "nn", [F32], after=tok)[0]
    (dx,), (early["mix_norm_w"],) = _row_bwd(_rms_res_fn, "mix_norm_bwd", [(x, d, 0)], [wt["mix_norm_w"]],
                                             [[(du, d, 0)], [(dh1, d, 0)]], 256)
    return loss8, dx, early, pending, tok


WEIGHTS = ["mix_norm_w", "w_in", "dn_conv_w", "dn_a_log", "dn_dt_bias", "dn_norm_w", "rw_mu", "rw_w0", "rw_w2",
           "rw_a0", "rw_a2", "rw_g2", "rw_k_k", "rw_k_a", "rw_r_k", "rw_ln_w", "rw_ln_b", "w_out", "xa_norm_w",
           "mem_norm_w", "xa_wq", "xa_wk", "xa_wv", "xa_wo", "ffn_norm_w", "ffn_w1", "ffn_w2", "final_norm_w"]
SHARDED = {"w_in": False, "w_out": False, "xa_wq": False, "xa_wk": False, "xa_wv": False, "xa_wo": True,
           "ffn_w1": True, "ffn_w2": False, "dn_conv_w": True, "rw_w2": True, "rw_a2": True, "rw_g2": True}
BF16_PAYLOAD = ("w_in", "w_out", "xa_wq", "xa_wk", "xa_wv", "xa_wo", "ffn_w1", "ffn_w2")
REPLICATED = [n for n in WEIGHTS if n not in SHARDED]
EARLY = ("w_in", "dn_conv_w", "rw_w2", "rw_a2", "rw_g2")
RW_IN_COLS = IN_COLS - DN_COLS
W_IN_SHARD = IN_COLS // N_DEV


def _layout_weights(fw):
    wt = dict(fw)
    wt["dn_conv_w"] = jnp.pad(fw["dn_conv_w"], ((0, 4), (0, 0)))
    wt["dn_a_log"] = jnp.pad(fw["dn_a_log"], ((0, 0), (0, LANES - DN_HEADS)))
    wt["dn_dt_bias"] = jnp.pad(fw["dn_dt_bias"], ((0, 0), (0, LANES - DN_HEADS)))
    wt["rw_w2"] = jnp.pad(fw["rw_w2"], ((0, 64), (0, 0)))
    wt["rw_a2"] = jnp.pad(fw["rw_a2"], ((64, 0), (0, 0)))
    return wt


def _logical_grads(g):
    out = dict(g)
    out["w_in"] = _w_in_grad_to_shards(g["w_in"])
    out["dn_conv_w"] = g["dn_conv_w"][:4]
    out["dn_a_log"] = g["dn_a_log"][:, :DN_HEADS]
    out["dn_dt_bias"] = g["dn_dt_bias"][:, :DN_HEADS]
    out["rw_w2"] = g["rw_w2"][:64]
    out["rw_a2"] = g["rw_a2"][64:]
    return out


def _pack(vals):
    parts = []
    for v in vals:
        flat = v.reshape(-1)
        parts.append(jnp.pad(flat, (0, -flat.shape[0] % LANES)))
    flat = jnp.concatenate(parts)
    flat = jnp.pad(flat, (0, -flat.shape[0] % (8 * LANES)))
    return flat.reshape(-1, LANES)


def _unpack(packed, shapes):
    flat = packed.reshape(-1)
    out, at = [], 0
    for shp in shapes:
        size = math.prod(shp)
        out.append(flat[at:at + size].reshape(shp))
        at += size + (-size % LANES)
    return out


def kernel(x, mem, mix_norm_w, w_in, dn_conv_w, dn_a_log, dn_dt_bias, dn_norm_w, rw_mu, rw_w0, rw_w2, rw_a0, rw_a2, rw_g2, rw_k_k, rw_k_a, rw_r_k, rw_ln_w, rw_ln_b, w_out, xa_norm_w, mem_norm_w, xa_wq, xa_wk, xa_wv, xa_wo, ffn_norm_w, ffn_w1, ffn_w2, final_norm_w, loss_target, m_mix_norm_w, m_w_in, m_dn_conv_w, m_dn_a_log, m_dn_dt_bias, m_dn_norm_w, m_rw_mu, m_rw_w0, m_rw_w2, m_rw_a0, m_rw_a2, m_rw_g2, m_rw_k_k, m_rw_k_a, m_rw_r_k, m_rw_ln_w, m_rw_ln_b, m_w_out, m_xa_norm_w, m_mem_norm_w, m_xa_wq, m_xa_wk, m_xa_wv, m_xa_wo, m_ffn_norm_w, m_ffn_w1, m_ffn_w2, m_final_norm_w, v_mix_norm_w, v_w_in, v_dn_conv_w, v_dn_a_log, v_dn_dt_bias, v_dn_norm_w, v_rw_mu, v_rw_w0, v_rw_w2, v_rw_a0, v_rw_a2, v_rw_g2, v_rw_k_k, v_rw_k_a, v_rw_r_k, v_rw_ln_w, v_rw_ln_b, v_w_out, v_xa_norm_w, v_mem_norm_w, v_xa_wq, v_xa_wk, v_xa_wv, v_xa_wo, v_ffn_norm_w, v_ffn_w1, v_ffn_w2, v_final_norm_w):
    given = dict(locals())
    w = {n: given[n] for n in WEIGHTS}
    m = {n: given["m_" + n] for n in WEIGHTS}
    v = {n: given["v_" + n] for n in WEIGHTS}

    local = {n: (lambda t: t[0].T) if n == "w_in" else (lambda t: t[0]) for n in SHARDED}
    shards = {n: (local[n](w[n]).astype(BF16) if n in BF16_PAYLOAD else local[n](w[n])) for n in SHARDED}
    srcs, dsts, _ = _gather_plan({n: shards[n] for n in EARLY})
    full = _gather_finish(EARLY, _gather_two_level("early_all_gather", srcs, dsts))
    for n in REPLICATED:
        full[n] = w[n].reshape(1, -1)

    loss8, dx, g, pending, after = _local_step(x[0], mem[0], loss_target[0], _layout_weights(full),
                                               {n: shards[n] for n in SHARDED if n not in EARLY})
    loss = lax.psum(loss8[0, 0], ("x", "y", "c"))

    packed = _pack([g[n] for n in REPLICATED])
    small, _ = _exchange_start("small_gather_start", [(packed, None)], [((N_DEV,) + packed.shape, F32, None)], True,
                               after)
    grad, delta, new_m, new_v = {}, {}, {}, {}
    done = [dx]

    def tie():
        return jnp.broadcast_to(sum(t[:1, :1] for t in done), (8, LANES))

    for names in sorted(pending, key=lambda names: names == EARLY):
        handle = pending[names]
        for n, parts in zip(names, _exchange_wait("grad_wait_" + names[0], handle, tie())):
            res = _sum_adamw("adamw_" + n, parts, local[n](w[n]), local[n](m[n]), local[n](v[n]))
            grad[n], delta[n], new_m[n], new_v[n] = [(t.T if n == "w_in" else t)[None] for t in res]
            done.append(res[1])

    (parts,) = _exchange_wait("small_gather_wait", small, tie())
    res = _sum_adamw("adamw_small", parts, _pack([w[n] for n in REPLICATED]),
                     _pack([m[n] for n in REPLICATED]), _pack([v[n] for n in REPLICATED]))
    shapes = [w[n].shape for n in REPLICATED]
    for store, packed_out in zip((grad, delta, new_m, new_v), res):
        for n, val in zip(REPLICATED, _unpack(packed_out, shapes)):
            store[n] = val

    return (loss, dx[None], *[grad[n] for n in WEIGHTS], *[delta[n] for n in WEIGHTS],
            *[new_m[n] for n in WEIGHTS], *[new_v[n] for n in WEIGHTS])
```

```python
import functools
import math

import jax
import jax.numpy as jnp
from jax import lax
from jax.experimental import pallas as pl
from jax.experimental.pallas import tpu as pltpu

F32 = jnp.float32
BF16 = jnp.bfloat16
SDS = jax.ShapeDtypeStruct

N_DEV = 8
D_MODEL = 2048
LANES = 128
CHUNK = 128
DN_HEADS = 8
DN_WIDTH = 1024
RW_WIDTH = 1024
RW_HEAD = 64
XA_HEADS = 4
XA_WIDTH = 512
FFN_HIDDEN = 8192
IN_COLS = 7440
DN_COLS = 4112
IN_PAD = 7680
RW_OFF = 4224
RMS_EPS = 1e-6
RW_GN_EPS = 64e-5
VMEM_LIMIT = 56 * 1024 * 1024

ADAM_LR = 0.001
ADAM_B1 = 0.9
ADAM_B2 = 0.999
ADAM_EPS = 1e-08
ADAM_WD = 0.01
ADAM_STEP = 10

_DIMS = {"nn": (((1,), (0,)), ((), ())), "nt": (((1,), (1,)), ((), ())), "tn": (((0,), (0,)), ((), ()))}


def _raw_dot(a, b, mode, hi):
    if hi:
        return lax.dot_general(a, b, _DIMS[mode], precision=lax.Precision.HIGHEST, preferred_element_type=F32)
    return lax.dot_general(a.astype(BF16), b.astype(BF16), _DIMS[mode], preferred_element_type=F32)


@functools.partial(jax.custom_vjp, nondiff_argnums=(2, 3))
def mm(a, b, mode="nn", hi=False):
    return _raw_dot(a, b, mode, hi)


def _mm_fwd(a, b, mode, hi):
    return _raw_dot(a, b, mode, hi), (a, b)


def _mm_bwd(mode, hi, res, g):
    a, b = res
    if mode == "nn":
        return _raw_dot(g, b, "nt", hi), _raw_dot(a, g, "tn", hi)
    if mode == "nt":
        return _raw_dot(g, b, "nn", hi), _raw_dot(g, a, "tn", hi)
    return _raw_dot(b, g, "nt", hi), _raw_dot(a, g, "nn", hi)


mm.defvjp(_mm_fwd, _mm_bwd)


def _shift_rows_raw(x, k):
    n = x.shape[0]
    rolled = pltpu.roll(x, k % n, axis=0)
    row = lax.broadcasted_iota(jnp.int32, x.shape, 0)
    keep = row >= k if k > 0 else row < n + k
    return jnp.where(keep, rolled, 0.0)


@functools.partial(jax.custom_vjp, nondiff_argnums=(1,))
def shift_rows(x, k):
    return _shift_rows_raw(x, k)


shift_rows.defvjp(lambda x, k: (_shift_rows_raw(x, k), None), lambda k, _, g: (_shift_rows_raw(g, -k),))


def _softplus(x):
    return jnp.maximum(x, 0.0) + jnp.log(1.0 + jnp.exp(-jnp.abs(x)))


def _sigmoid(x):
    return 1.0 / (1.0 + jnp.exp(-x))


def _silu(x):
    return x * _sigmoid(x)


def _tri_masks(n):
    ii = lax.broadcasted_iota(jnp.int32, (n, n), 0)
    jj = lax.broadcasted_iota(jnp.int32, (n, n), 1)
    return ii >= jj, ii > jj, ii == jj


def _neumann_inv_raw(m):
    n = m.shape[0]
    _, _, eye = _tri_masks(n)
    eye = jnp.where(eye, 1.0, 0.0)
    p = eye + m
    mk = m
    for _ in range(int(math.log2(n)) - 1):
        mk = _raw_dot(mk, mk, "nn", False)
        p = p + _raw_dot(p, mk, "nn", False)
    resid = eye - p + _raw_dot(m, p, "nn", True)
    return p + _raw_dot(p, resid, "nn", False)


@jax.custom_vjp
def _neumann_inv(m):
    return _neumann_inv_raw(m)


def _neumann_inv_fwd(m):
    p = _neumann_inv_raw(m)
    return p, p


def _neumann_inv_bwd(p, g):
    return (_raw_dot(_raw_dot(p, g, "tn", False), p, "nt", False),)


_neumann_inv.defvjp(_neumann_inv_fwd, _neumann_inv_bwd)


@jax.custom_vjp
def _saved_inv(m, p):
    return p


_saved_inv.defvjp(lambda m, p: (p, p), lambda p, g: (_neumann_inv_bwd(p, g)[0], jnp.zeros_like(p)))


def _inverse(m, saved):
    return _neumann_inv(m) if saved is None else _saved_inv(m, saved)


def _cumsum_rows(x):
    causal, _, _ = _tri_masks(x.shape[0])
    return mm(jnp.where(causal, 1.0, 0.0), x, "nn", True)


def _gdn_group(s0, q, k, v, gb, bb, gc, *saved):
    diff = jnp.stack([gc[j] - gc[j].T for j in range(gc.shape[0])])
    return jax.vmap(_gdn_chunk)(s0, q, k, v, gb, bb, gc, diff, *saved)


def _rw_group(*args):
    return jax.vmap(_rw_chunk)(*args)


def _gdn_chunk(s0, q, k, v, gb, bb, gc, diff, saved=None):
    c = q.shape[0]
    causal, strict, _ = _tri_masks(c)
    decay = jnp.exp(jnp.where(causal, diff, -jnp.inf))
    kb = k * bb
    a = jnp.where(strict, mm(kb, k, "nt") * decay, 0.0)
    p = _inverse(-a, saved)
    u = mm(p, v * bb)
    w = mm(p, kb * jnp.exp(gc))
    attn = mm(q, k, "nt") * decay
    v_new = u - mm(w, s0)
    o = mm(q * jnp.exp(gc), s0) + mm(attn, v_new)
    g_last = jnp.sum(gb, axis=0, keepdims=True)
    s1 = s0 * jnp.exp(g_last) + mm(k * jnp.exp(g_last - gc), v_new, "tn")
    return o, s1, p


def _rw_chunk(s0, r, lw, k, v, al, be, gc, saved0=None, saved1=None):
    c = r.shape[0]
    causal, strict, _ = _tri_masks(c)
    gp = gc - lw
    row = lax.broadcasted_iota(jnp.int32, lw.shape, 0)
    lane = lax.broadcasted_iota(jnp.int32, lw.shape, 1)
    g_mid = jnp.sum(jnp.where(row < c // 2, lw, 0.0), axis=0, keepdims=True)
    g_last = jnp.sum(lw, axis=0, keepdims=True)
    e_n = jnp.exp(g_mid - gc)
    rg = r * jnp.exp(gc - g_mid)
    bg = be * jnp.exp(gp - g_mid)
    an = al * e_n
    kn = k * e_n
    bt = mm(be * jnp.exp(gp), s0, "nt")
    rt = mm(r * jnp.exp(gc), s0, "nt")
    us, ys, ps = [], [], []
    for h, saved in enumerate((saved0, saved1)):
        mine = (lane >= RW_HEAD) if h else (lane < RW_HEAD)
        bgh = jnp.where(mine, bg, 0.0)
        rgh = jnp.where(mine, rg, 0.0)
        a_ab = jnp.where(strict, mm(bgh, an, "nt"), 0.0)
        a_kb = jnp.where(strict, mm(bgh, kn, "nt"), 0.0)
        a_ra = jnp.where(causal, mm(rgh, an, "nt"), 0.0)
        a_rk = jnp.where(causal, mm(rgh, kn, "nt"), 0.0)
        p = _inverse(a_ab, saved)
        ps.append(p)
        u_h = mm(p, bt + mm(a_kb, v))
        us.append(u_h)
        ys.append(rt + mm(a_ra, u_h) + mm(a_rk, v))
    lo = lane < RW_HEAD
    u = jnp.where(lo, us[0], us[1])
    y = jnp.where(lo, ys[0], ys[1])
    tail = jnp.exp(g_last - gc)
    s1 = s0 * jnp.exp(g_last) + mm(u, al * tail, "tn") + mm(v, k * tail, "tn")
    vi = lax.broadcasted_iota(jnp.int32, s0.shape, 0)
    ki = lax.broadcasted_iota(jnp.int32, s0.shape, 1)
    s1 = jnp.where((vi < RW_HEAD) == (ki < RW_HEAD), s1, 0.0)
    return y, s1, ps[0], ps[1]


SCAN_HB = 8


def _scan_specs(arrs, n_chunks, reverse):
    def spec(off):
        assert off % SCAN_HB == 0
        if reverse:
            return pl.BlockSpec((CHUNK, SCAN_HB * LANES), lambda h, n: (n_chunks - 1 - n, off // SCAN_HB + h))
        return pl.BlockSpec((CHUNK, SCAN_HB * LANES), lambda h, n: (n, off // SCAN_HB + h))
    return [spec(off) for _, off in arrs]


def _split_heads(x):
    return jnp.stack([x[:, LANES * j:LANES * (j + 1)] for j in range(SCAN_HB)], axis=0)


def _merge_heads(x):
    return jnp.concatenate([x[j] for j in range(SCAN_HB)], axis=1)


def _scan_fwd(group_fn, name, arrs, heads, n_kept):
    s = arrs[0][0].shape[0]
    n_chunks = s // CHUNK
    n_in = len(arrs)

    def body(*refs):
        y_ref, st_ref = refs[n_in:n_in + 2]
        kept_refs, s_scr = refs[n_in + 2:-1], refs[-1]

        @pl.when(pl.program_id(1) == 0)
        def _():
            s_scr[...] = jnp.zeros_like(s_scr)

        s0 = s_scr[...]
        st_ref[...] = s0
        y, s1, *kept = group_fn(s0, *[_split_heads(r[...]) for r in refs[:n_in]])
        y_ref[...] = _merge_heads(y)
        s_scr[...] = s1
        for ref, val in zip(kept_refs, kept):
            ref[...] = val

    per_chunk = pl.BlockSpec((SCAN_HB, None, LANES, LANES), lambda h, n: (h, n, 0, 0))
    res = pl.pallas_call(
        body, grid=(heads // SCAN_HB, n_chunks), name=name,
        in_specs=_scan_specs(arrs, n_chunks, False),
        out_specs=[pl.BlockSpec((CHUNK, SCAN_HB * LANES), lambda h, n: (n, h))] + [per_chunk] * (1 + n_kept),
        out_shape=[SDS((s, heads * LANES), F32)] + [SDS((heads, n_chunks, LANES, LANES), F32)] * (1 + n_kept),
        scratch_shapes=[pltpu.VMEM((SCAN_HB, LANES, LANES), F32)],
        compiler_params=pltpu.CompilerParams(dimension_semantics=("arbitrary", "arbitrary")),
    )(*[a for a, _ in arrs])
    return res[0], res[1:]


def _scan_bwd(group_fn, name, arrs, kept, dy, heads):
    s = arrs[0][0].shape[0]
    n_chunks = s // CHUNK
    n_in, n_kept = len(arrs), len(kept)

    def body(*refs):
        kept_vals = [r[...] for r in refs[n_in:n_in + n_kept]]
        dy_ref = refs[n_in + n_kept]
        d_refs = refs[n_in + n_kept + 1:2 * n_in + n_kept + 1]
        ds_scr = refs[-1]

        @pl.when(pl.program_id(1) == 0)
        def _():
            ds_scr[...] = jnp.zeros_like(ds_scr)

        def fn(s0, *ins):
            return group_fn(s0, *ins, *kept_vals[1:])[:2]

        _, vjp = jax.vjp(fn, kept_vals[0], *[_split_heads(r[...]) for r in refs[:n_in]])
        grads = vjp((_split_heads(dy_ref[...]), ds_scr[...]))
        ds_scr[...] = grads[0]
        for ref, g in zip(d_refs, grads[1:]):
            ref[...] = _merge_heads(g)

    rev = pl.BlockSpec((CHUNK, SCAN_HB * LANES), lambda h, n: (n_chunks - 1 - n, h))
    per_chunk = pl.BlockSpec((SCAN_HB, None, LANES, LANES), lambda h, n: (h, n_chunks - 1 - n, 0, 0))
    return pl.pallas_call(
        body, grid=(heads // SCAN_HB, n_chunks), name=name,
        in_specs=_scan_specs(arrs, n_chunks, True) + [per_chunk] * n_kept + [rev],
        out_specs=[rev] * n_in,
        out_shape=[SDS((s, heads * LANES), F32)] * n_in,
        scratch_shapes=[pltpu.VMEM((SCAN_HB, LANES, LANES), F32)],
        compiler_params=pltpu.CompilerParams(dimension_semantics=("arbitrary", "arbitrary")),
    )(*[a for a, _ in arrs], *kept, dy)


def _col_spec(tr, width, cb):
    return pl.BlockSpec((tr, width), lambda i: (i, cb))


def _whole(p):
    return pl.BlockSpec(p.shape, lambda i: (0,) * p.ndim)


def _row_fwd(fn, name, tiles, params, outs, tr):
    rows = tiles[0][0].shape[0]
    nt, npar = len(tiles), len(params)

    def body(*refs):
        vals = [r[...].astype(F32) for r in refs[:nt + npar]]
        for ref, o in zip(refs[nt + npar:], fn(*vals)):
            ref[...] = o.astype(ref.dtype)

    return pl.pallas_call(
        body, grid=(rows // tr,), name=name,
        in_specs=[_col_spec(tr, w, cb) for _, w, cb in tiles] + [_whole(p) for p in params],
        out_specs=[_col_spec(tr, w, 0) for w, _ in outs],
        out_shape=[SDS((rows, w), dt) for w, dt in outs],
        compiler_params=pltpu.CompilerParams(dimension_semantics=("arbitrary",), vmem_limit_bytes=VMEM_LIMIT),
    )(*[a for a, _, _ in tiles], *params)


def _row_bwd(fn, name, tiles, params, cts, tr, want_tiles=None):
    rows = tiles[0][0].shape[0]
    nt, npar = len(tiles), len(params)
    want = list(range(nt)) if want_tiles is None else list(want_tiles)
    flat_cts = [c for group in cts for c in group]
    n_ct = len(flat_cts)

    def body(*refs):
        vals = [r[...].astype(F32) for r in refs[:nt + npar]]
        ct_refs = refs[nt + npar:nt + npar + n_ct]
        out_refs = refs[nt + npar + n_ct:]
        ct_vals, at = [], 0
        for group in cts:
            total = ct_refs[at][...].astype(F32)
            for r in ct_refs[at + 1:at + len(group)]:
                total = total + r[...].astype(F32)
            ct_vals.append(total)
            at += len(group)
        _, vjp = jax.vjp(lambda *a: tuple(fn(*a)), *vals)
        grads = vjp(tuple(ct_vals))
        for ref, t in zip(out_refs[:len(want)], want):
            ref[...] = grads[t]
        first = pl.program_id(0) == 0
        for ref, g in zip(out_refs[len(want):], grads[nt:]):
            @pl.when(first)
            def _(ref=ref, g=g):
                ref[...] = g

            @pl.when(jnp.logical_not(first))
            def _(ref=ref, g=g):
                ref[...] += g

    res = pl.pallas_call(
        body, grid=(rows // tr,), name=name,
        in_specs=[_col_spec(tr, w, cb) for _, w, cb in tiles] + [_whole(p) for p in params]
        + [_col_spec(tr, w, cb) for _, w, cb in flat_cts],
        out_specs=[_col_spec(tr, tiles[t][1], 0) for t in want] + [_whole(p) for p in params],
        out_shape=[SDS((rows, tiles[t][1]), F32) for t in want] + [SDS(p.shape, F32) for p in params],
        compiler_params=pltpu.CompilerParams(dimension_semantics=("arbitrary",), vmem_limit_bytes=VMEM_LIMIT),
    )(*[a for a, _, _ in tiles], *params, *[a for a, _, _ in flat_cts])
    return res[:len(want)], res[len(want):]


def _col_fwd(fn, name, x, first_block, n_blocks, params):
    rows = x.shape[0]

    def body(*refs):
        refs[-1][...] = fn(*[r[...] for r in refs[:-1]])

    return pl.pallas_call(
        body, grid=(n_blocks,), name=name,
        in_specs=[pl.BlockSpec((rows, LANES), lambda j: (0, first_block + j))]
        + [pl.BlockSpec((p.shape[0], LANES), lambda j: (0, j)) for p in params],
        out_specs=pl.BlockSpec((rows, LANES), lambda j: (0, j)),
        out_shape=SDS((rows, n_blocks * LANES), F32),
        compiler_params=pltpu.CompilerParams(dimension_semantics=("arbitrary",), vmem_limit_bytes=VMEM_LIMIT),
    )(x, *params)


def _col_bwd(fn, name, x, first_block, n_blocks, params, dy):
    rows = x.shape[0]
    npar = len(params)

    def body(*refs):
        vals = [r[...] for r in refs[:1 + npar]]
        _, vjp = jax.vjp(fn, *vals)
        grads = vjp(refs[1 + npar][...])
        for ref, g in zip(refs[2 + npar:], grads):
            ref[...] = g

    pspecs = [pl.BlockSpec((p.shape[0], LANES), lambda j: (0, j)) for p in params]
    blk = pl.BlockSpec((rows, LANES), lambda j: (0, j))
    res = pl.pallas_call(
        body, grid=(n_blocks,), name=name,
        in_specs=[pl.BlockSpec((rows, LANES), lambda j: (0, first_block + j))] + pspecs + [blk],
        out_specs=[blk] + pspecs,
        out_shape=[SDS((rows, n_blocks * LANES), F32)] + [SDS(p.shape, F32) for p in params],
        compiler_params=pltpu.CompilerParams(dimension_semantics=("arbitrary",), vmem_limit_bytes=VMEM_LIMIT),
    )(x, *params, dy)
    return res[0], res[1:]


def _conv_fn(x, w):
    acc = x * w[3:4, :]
    for j in range(3):
        acc = acc + shift_rows(x, 3 - j) * w[j:j + 1, :]
    return _silu(acc)


def _lerp_fn(x, mu):
    return x + (shift_rows(x, 1) - x) * mu[0:1, :]


def _seg_sum(x, width):
    if width == LANES:
        return jnp.sum(x, axis=1, keepdims=True)
    lo = lax.broadcasted_iota(jnp.int32, x.shape, 1) < width
    s0 = jnp.sum(jnp.where(lo, x, 0.0), axis=1, keepdims=True)
    s1 = jnp.sum(jnp.where(lo, 0.0, x), axis=1, keepdims=True)
    return jnp.where(lo, s0, s1)


def _per_block(fn, *xs):
    n = xs[0].shape[1] // LANES
    return jnp.concatenate([fn(*[x[:, LANES * b:LANES * (b + 1)] for x in xs]) for b in range(n)], axis=1)


def _head_expand(col0):
    r = lax.broadcasted_iota(jnp.int32, (LANES, DN_WIDTH), 0)
    c = lax.shift_right_logical(lax.broadcasted_iota(jnp.int32, (LANES, DN_WIDTH), 1), 7)
    return jnp.where(r == c + col0, 1.0, 0.0)


def _dn_pre_fn(cq, ck, gates, a_log, dt_bias):
    l2 = lambda x: x * lax.rsqrt(_seg_sum(x * x, LANES) + 1e-6)
    qh = _per_block(l2, cq) * (LANES ** -0.5)
    kh = _per_block(l2, ck)
    g = -jnp.exp(a_log) * _softplus(gates + dt_bias)
    gb = mm(g, _head_expand(0), "nn", True)
    bb = mm(_sigmoid(gates), _head_expand(DN_HEADS), "nn", True)
    return qh, kh, gb, bb, _cumsum_rows(gb)


def _dn_post_fn(o, z, nw):
    def one(ob, zb):
        return ob * lax.rsqrt(_seg_sum(ob * ob, LANES) * (1.0 / LANES) + RMS_EPS) * nw * _silu(zb)
    return (_per_block(one, o, z),)


def _rw_pre_fn(pr, pk, pv, pwa, pg, w0, a0, k_k, k_a, w2p, a2p, g2):
    log_w = -_softplus(-(w0 + mm(jnp.tanh(pwa), w2p))) - 0.5
    lw = -jnp.exp(log_w)
    a = _sigmoid(a0 + mm(pwa, a2p))
    gate = mm(_sigmoid(pg), g2)
    kk = pk * k_k
    kk = _per_block(lambda x: x / jnp.maximum(jnp.sqrt(_seg_sum(x * x, RW_HEAD)), 1e-12), kk)
    k = pk * (1.0 + (a - 1.0) * k_a)
    return pr, lw, k, pv, kk * a, -kk, gate, _cumsum_rows(lw)


def _rw_post_fn(y, r, k, v, gate, ln_w, ln_b, r_k):
    def one(yb, rb, kb, vb, gb, wb, bb, rkb):
        d = yb - _seg_sum(yb, RW_HEAD) * (1.0 / RW_HEAD)
        var = _seg_sum(d * d, RW_HEAD) * (1.0 / RW_HEAD)
        yn = d * lax.rsqrt(var + RW_GN_EPS) * wb + bb
        return (yn + _seg_sum(rb * kb * rkb, RW_HEAD) * vb) * gb
    return (_per_block(one, y, r, k, v, gate, ln_w, ln_b, r_k),)


def _rms_fn(h, w):
    return (h * lax.rsqrt(jnp.mean(h * h, axis=1, keepdims=True) + RMS_EPS) * w,)


def _xattn_fn(q, k, v):
    outs = []
    for h in range(XA_HEADS):
        sl = slice(LANES * h, LANES * (h + 1))
        s = mm(q[:, sl], k[:, sl], "nt") * (LANES ** -0.5)
        e = jnp.exp(s - jnp.max(s, axis=1, keepdims=True))
        outs.append(mm(e / jnp.sum(e, axis=1, keepdims=True), v[:, sl]))
    return (jnp.concatenate(outs, axis=1),)


def _fit(tile, dim):
    best = [t for t in range(LANES, min(tile, dim) + 1, LANES) if dim % t == 0]
    assert best, (tile, dim)
    return best[-1]


def _matmul(name, a, b, mode, out_dtypes, epilogue=None, extras=(), tm=1024, tn=1024, tk=2048, after=None):
    if mode == "tn":
        (k_dim, m), n = a.shape, b.shape[1]
    else:
        (m, k_dim), n = a.shape, (b.shape[1] if mode == "nn" else b.shape[0])
    tm, tn, tk = _fit(tm, m), _fit(tn, n), _fit(tk, k_dim)
    nk = k_dim // tk
    a_spec = (pl.BlockSpec((tk, tm), lambda i, j, k: (k, i)) if mode == "tn"
              else pl.BlockSpec((tm, tk), lambda i, j, k: (i, k)))
    b_spec = (pl.BlockSpec((tn, tk), lambda i, j, k: (j, k)) if mode == "nt"
              else pl.BlockSpec((tk, tn), lambda i, j, k: (k, j)))
    o_spec = pl.BlockSpec((tm, tn), lambda i, j, k: (i, j))
    n_ex, n_out = len(extras), len(out_dtypes)
    ties = [] if after is None else [after]

    def finish(total, rest):
        ex = [r[...].astype(F32) for r in rest[:n_ex]]
        res = epilogue(total, *ex) if epilogue else (total,)
        for ref, o in zip(rest[n_ex + len(ties):n_ex + len(ties) + n_out], res):
            ref[...] = o.astype(ref.dtype)

    def body_single(a_ref, b_ref, *rest):
        finish(_raw_dot(a_ref[...], b_ref[...], mode, False), rest)

    def body_acc(a_ref, b_ref, *rest):
        acc = rest[-1]
        k = pl.program_id(2)

        @pl.when(k == 0)
        def _():
            acc[...] = jnp.zeros_like(acc)

        acc[...] += _raw_dot(a_ref[...], b_ref[...], mode, False)

        @pl.when(k == nk - 1)
        def _():
            finish(acc[...], rest)

    res = pl.pallas_call(
        body_single if nk == 1 else body_acc, grid=(m // tm, n // tn, nk), name=name,
        in_specs=[a_spec, b_spec] + [o_spec] * n_ex + [pl.BlockSpec((8, LANES), lambda i, j, k: (0, 0))] * len(ties),
        out_specs=[o_spec] * n_out,
        out_shape=[SDS((m, n), dt) for dt in out_dtypes],
        scratch_shapes=[] if nk == 1 else [pltpu.VMEM((tm, tn), F32)],
        compiler_params=pltpu.CompilerParams(dimension_semantics=("parallel", "parallel", "arbitrary"),
                                             vmem_limit_bytes=VMEM_LIMIT),
    )(a, b, *extras, *ties)
    return res


def _matmul_norm_bwd(name, a, b, mode, h, w, dres, after=None, tm=512, tk=512):
    m, n = h.shape
    k_dim = a.shape[1]
    tm, tk = _fit(tm, m), _fit(tk, k_dim)
    nk = k_dim // tk
    ties = [] if after is None else [after]
    a_spec = pl.BlockSpec((tm, tk), lambda i, k: (i, k))
    b_spec = pl.BlockSpec((n, tk), lambda i, k: (0, k)) if mode == "nt" else pl.BlockSpec((tk, n), lambda i, k: (k, 0))
    row = pl.BlockSpec((tm, n), lambda i, k: (i, 0))
    w_spec = pl.BlockSpec((1, n), lambda i, k: (0, 0))

    def body(a_ref, b_ref, h_ref, w_ref, dres_ref, *rest):
        dh_ref, dw_ref, acc = rest[len(ties):]
        i, k = pl.program_id(0), pl.program_id(1)

        @pl.when(k == 0)
        def _():
            acc[...] = jnp.zeros_like(acc)

        acc[...] += _raw_dot(a_ref[...], b_ref[...], mode, False)

        @pl.when(k == nk - 1)
        def _():
            _, vjp = jax.vjp(_rms_res_fn, h_ref[...], w_ref[...])
            dh, dw = vjp((acc[...], dres_ref[...]))
            dh_ref[...] = dh

            @pl.when(i == 0)
            def _():
                dw_ref[...] = dw

            @pl.when(i != 0)
            def _():
                dw_ref[...] += dw

    return pl.pallas_call(
        body, grid=(m // tm, nk), name=name,
        in_specs=[a_spec, b_spec, row, w_spec, row] + [pl.BlockSpec((8, LANES), lambda i, k: (0, 0))] * len(ties),
        out_specs=[row, w_spec],
        out_shape=[SDS((m, n), F32), SDS((1, n), F32)],
        scratch_shapes=[pltpu.VMEM((tm, n), F32)],
        compiler_params=pltpu.CompilerParams(dimension_semantics=("arbitrary", "arbitrary"),
                                             vmem_limit_bytes=VMEM_LIMIT),
    )(a, b, h, w, dres, *ties)


def _loss_call(h, target, w, tr=256):
    rows, d = h.shape

    def fn(hv, wv, tv):
        y = _rms_fn(hv, wv)[0]
        return 0.5 * jnp.sum(jnp.mean(jnp.square(y - tv), axis=1, keepdims=True), axis=0, keepdims=True)

    def body(h_ref, t_ref, w_ref, loss_ref, dh_ref, dw_ref):
        tv = t_ref[...]
        val, vjp = jax.vjp(lambda hv, wv: fn(hv, wv, tv), h_ref[...], w_ref[...])
        dh, dw = vjp(jnp.ones((1, 1), F32))
        dh_ref[...] = dh
        first = pl.program_id(0) == 0

        @pl.when(first)
        def _():
            loss_ref[...] = jnp.broadcast_to(val, loss_ref.shape)
            dw_ref[...] = dw

        @pl.when(jnp.logical_not(first))
        def _():
            loss_ref[...] += jnp.broadcast_to(val, loss_ref.shape)
            dw_ref[...] += dw

    return pl.pallas_call(
        body, grid=(rows // tr,), name="loss_head",
        in_specs=[_col_spec(tr, d, 0), _col_spec(tr, d, 0), _whole(w)],
        out_specs=[pl.BlockSpec((8, LANES), lambda i: (0, 0)), _col_spec(tr, d, 0), _whole(w)],
        out_shape=[SDS((8, LANES), F32), SDS((rows, d), F32), SDS(w.shape, F32)],
        compiler_params=pltpu.CompilerParams(dimension_semantics=("arbitrary",), vmem_limit_bytes=VMEM_LIMIT),
    )(h, target, w)


def _adamw_vals(w, g, m, v):
    m = ADAM_B1 * m + (1.0 - ADAM_B1) * g
    v = ADAM_B2 * v + (1.0 - ADAM_B2) * jnp.square(g)
    m_hat = m / (1.0 - ADAM_B1 ** ADAM_STEP)
    v_hat = v / (1.0 - ADAM_B2 ** ADAM_STEP)
    delta = -ADAM_LR * (m_hat / (jnp.sqrt(v_hat) + ADAM_EPS) + ADAM_WD * w)
    return delta, m, v


def _sum_adamw(name, parts, w, m, v):
    r, c = w.shape
    n_parts = parts.shape[0]
    budget = 6 * 1024 * 1024
    tr, tc = r, c
    for cand in (512, 256, 128, 64, 32, 16, 8):
        if r % cand == 0 and n_parts * cand * c * 4 <= budget:
            tr = cand
            break
    if n_parts * tr * c * 4 > budget:
        tc = max(t for t in range(LANES, c + 1, LANES) if c % t == 0 and n_parts * r * t * 4 <= budget)

    def body(p_ref, w_ref, m_ref, v_ref, g_ref, d_ref, m2_ref, v2_ref):
        g = p_ref[0].astype(F32)
        for s in range(1, n_parts):
            g = g + p_ref[s].astype(F32)
        g_ref[...] = g
        d_ref[...], m2_ref[...], v2_ref[...] = _adamw_vals(w_ref[...], g, m_ref[...], v_ref[...])

    blk = pl.BlockSpec((tr, tc), lambda i: (i, 0)) if tc == c else pl.BlockSpec((tr, tc), lambda i: (0, i))
    parts_blk = (pl.BlockSpec((n_parts, tr, tc), lambda i: (0, i, 0)) if tc == c
                 else pl.BlockSpec((n_parts, tr, tc), lambda i: (0, 0, i)))
    return pl.pallas_call(
        body, grid=(r // tr if tc == c else c // tc,), name=name,
        in_specs=[parts_blk, blk, blk, blk],
        out_specs=[blk] * 4, out_shape=[SDS((r, c), F32)] * 4,
        compiler_params=pltpu.CompilerParams(dimension_semantics=("arbitrary",), vmem_limit_bytes=VMEM_LIMIT),
    )(parts, w, m, v)


def _peers():
    x, y, c = lax.axis_index("x"), lax.axis_index("y"), lax.axis_index("c")
    peers = []
    for k in range(1, N_DEV):
        px = 1 - x if k & 4 else x
        py = 1 - y if k & 2 else y
        pc = 1 - c if k & 1 else c
        peers.append(((px, py, pc), 4 * px + 2 * py + pc))
    return 4 * x + 2 * y + c, peers


def _slot(ref, idx, cols):
    if cols is None:
        return ref.at[idx]
    return ref.at[:, pl.ds(pl.multiple_of(idx * cols, LANES), cols)]


def _gather_two_level(name, srcs, dsts):
    n = len(srcs)
    dst_cols = [c for _, _, c in dsts]

    def body(*refs):
        src_refs, out_refs = refs[:n], refs[n:2 * n]
        send_sems, recv_sems, local_sems = refs[2 * n:]
        x, y, c = lax.axis_index("x"), lax.axis_index("y"), lax.axis_index("c")
        index = lambda px, py, pc: 4 * px + 2 * py + pc
        me, sibling = index(x, y, c), (x, y, 1 - c)
        chips = [(x, 1 - y), (1 - x, y), (1 - x, 1 - y)]

        def copy(a, k, src, block, to):
            return pltpu.make_async_remote_copy(
                src_ref=src, dst_ref=_slot(out_refs[a], block, dst_cols[a]),
                send_sem=send_sems.at[a, k], recv_sem=recv_sems.at[a, k],
                device_id=to, device_id_type=pl.DeviceIdType.MESH)

        local, first, passed = [], [], []
        for a in range(n):
            cp = pltpu.make_async_copy(src_refs[a], _slot(out_refs[a], me, dst_cols[a]), local_sems.at[a])
            cp.start()
            local.append(cp)
            first.append(copy(a, 0, src_refs[a], me, sibling))
            first += [copy(a, 1 + j, src_refs[a], me, (*chip, c)) for j, chip in enumerate(chips)]
        for cp in first:
            cp.start()
        for a in range(n):
            for j, chip in enumerate(chips):
                block = index(*chip, c)
                arrived = _slot(out_refs[a], block, dst_cols[a])
                copy(a, 1 + j, arrived, block, (*chip, c)).wait_recv()
                passed.append(copy(a, 4 + j, arrived, block, sibling))
                passed[-1].start()
        for a in range(n):
            copy(a, 0, src_refs[a], index(x, y, 1 - c), sibling).wait_recv()
            for j, chip in enumerate(chips):
                block = index(*chip, 1 - c)
                copy(a, 4 + j, src_refs[a], block, sibling).wait_recv()
        for cp in first + passed:
            cp.wait_send()
        for cp in local:
            cp.wait()

    any_spec = pl.BlockSpec(memory_space=pl.ANY)
    return pl.pallas_call(
        body, name=name,
        in_specs=[any_spec] * n, out_specs=[any_spec] * n,
        out_shape=[SDS(shape, dt) for shape, dt, _ in dsts],
        scratch_shapes=_exchange_sems(n),
    )(*[a for a, _ in srcs])


_HBM = pl.BlockSpec(memory_space=pltpu.HBM)
_SEM = pl.BlockSpec(memory_space=pltpu.SEMAPHORE)
_EFFECT = pltpu.SideEffectType.DATAFLOW_SIDE_EFFECTING


def _split_copies(src_cols, dst_cols, gather, chips, src_refs, land_refs, send_sems, recv_sems, landings):
    me, peers = _peers()
    if chips:
        me, peers = me // 2, [(pos, idx // 2) for k, (pos, idx) in enumerate(peers) if (k + 1) in (2, 4, 6)]
    n, width = len(src_cols), len(peers)
    remote, local = [], []
    for a, (s_cols, d_cols) in enumerate(zip(src_cols, dst_cols)):
        mine = src_refs[a] if gather else _slot(src_refs[a], me, s_cols)
        local.append(pltpu.make_async_copy(mine, _slot(land_refs[a], me, d_cols), send_sems.at[n * width + a]))
        for k, (pos, idx) in enumerate(peers):
            blk = src_refs[a] if gather else _slot(src_refs[a], idx, s_cols)
            remote.append(pltpu.make_async_remote_copy(
                src_ref=blk, dst_ref=_slot(land_refs[a], idx if landings else me, d_cols),
                send_sem=send_sems.at[a * width + k], recv_sem=recv_sems.at[a * width + k],
                device_id=pos, device_id_type=pl.DeviceIdType.MESH))
    return remote, local


def _exchange_start(name, srcs, dsts, gather, after, chips=False):
    n = len(srcs)
    src_cols, dst_cols = [c for _, c in srcs], [c for _, _, c in dsts]
    width = 3 if chips else N_DEV - 1

    def body(*refs):
        src_refs, land_refs = refs[:n], refs[n:2 * n]
        send_sems, recv_sems = refs[2 * n + 1:2 * n + 3]
        token = refs[-1]
        remote, local = _split_copies(src_cols, dst_cols, gather, chips, src_refs, land_refs, send_sems, recv_sems,
                                      False)
        for cp in remote + local:
            cp.start()
        token[...] = jnp.zeros_like(token)

    hbm = lambda a: pltpu.with_memory_space_constraint(a, pltpu.HBM)
    lands = [hbm(lax.empty(shape, dt)) for shape, dt, _ in dsts]
    res = pl.pallas_call(
        body, name=name,
        out_shape=(pltpu.SemaphoreType.DMA((n * (width + 1),)), pltpu.SemaphoreType.DMA((n * width,)),
                   *[pltpu.HBM(a.shape, a.dtype) for a, _ in srcs], *[pltpu.HBM(a.shape, a.dtype) for a in lands],
                   SDS((8, LANES), F32)),
        in_specs=[_HBM] * (2 * n) + [pl.BlockSpec(memory_space=pl.ANY)],
        out_specs=(_SEM, _SEM, *[_HBM] * (2 * n), pl.BlockSpec(memory_space=pltpu.VMEM)),
        input_output_aliases={i: 2 + i for i in range(2 * n)},
        compiler_params=pltpu.CompilerParams(has_side_effects=_EFFECT),
    )(*[hbm(a) for a, _ in srcs], *lands, after)
    handle = (res[0], res[1], res[2:2 + n], res[2 + n:2 + 2 * n], src_cols, dst_cols, gather, chips)
    return handle, res[-1]


def _exchange_wait(name, handle, after):
    send_sems, recv_sems, src_thru, land_thru, src_cols, dst_cols, gather, chips = handle
    n = len(src_thru)

    def body(*refs):
        src_refs, land_refs = refs[:n], refs[n:2 * n]
        s_sems, r_sems = refs[2 * n:2 * n + 2]
        remote, local = _split_copies(src_cols, dst_cols, gather, chips, src_refs, land_refs, s_sems, r_sems, True)
        for cp in remote:
            cp.wait_send()
            cp.wait_recv()
        for cp in local:
            cp.wait()

    res = pl.pallas_call(
        body, name=name,
        out_shape=tuple(pltpu.HBM(a.shape, a.dtype) for a in (*src_thru, *land_thru)),
        in_specs=[_HBM] * (2 * n) + [_SEM, _SEM, pl.BlockSpec(memory_space=pl.ANY)],
        out_specs=tuple([_HBM] * (2 * n)),
        input_output_aliases={i: i for i in range(2 * n)},
        compiler_params=pltpu.CompilerParams(has_side_effects=_EFFECT),
    )(*src_thru, *land_thru, send_sems, recv_sems, after)
    return res[n:]


def _pair_swap(name, arrs):
    n = len(arrs)

    def body(*refs):
        src_refs, out_refs = refs[:n], refs[n:2 * n]
        send_sems, recv_sems = refs[2 * n:]
        x, y, c = lax.axis_index("x"), lax.axis_index("y"), lax.axis_index("c")
        copies = [pltpu.make_async_remote_copy(
            src_ref=src_refs[a].at[:, 1 - c], dst_ref=out_refs[a], send_sem=send_sems.at[a], recv_sem=recv_sems.at[a],
            device_id=(x, y, 1 - c), device_id_type=pl.DeviceIdType.MESH) for a in range(n)]
        for cp in copies:
            cp.start()
        for cp in copies:
            cp.wait()

    any_spec = pl.BlockSpec(memory_space=pl.ANY)
    return pl.pallas_call(
        body, name=name,
        in_specs=[any_spec] * n, out_specs=[any_spec] * n,
        out_shape=[SDS((a.shape[0],) + a.shape[2:], a.dtype) for a in arrs],
        scratch_shapes=[pltpu.SemaphoreType.DMA((n,)), pltpu.SemaphoreType.DMA((n,))],
    )(*arrs)


def _pair_add(name, mine, theirs):
    four, _, r, c = mine.shape
    tr = r
    for cand in (512, 256, 128, 64, 32, 16, 8):
        if r % cand == 0:
            tr = cand
            break
    tc = max(t for t in range(LANES, c + 1, LANES) if c % t == 0 and (t == LANES or 2 * tr * t * 4 <= 4 * 1024 * 1024))

    def body(m_ref, t_ref, o_ref):
        core = lax.axis_index("c")
        both = m_ref[...].astype(F32)
        own = jnp.where(core == 0, both[0], both[1])
        o_ref[...] = (own + t_ref[...].astype(F32)).astype(o_ref.dtype)

    return pl.pallas_call(
        body, grid=(four, r // tr, c // tc), name=name,
        in_specs=[pl.BlockSpec((None, 2, tr, tc), lambda i, j, k: (i, 0, j, k)),
                  pl.BlockSpec((None, tr, tc), lambda i, j, k: (i, j, k))],
        out_specs=pl.BlockSpec((None, tr, tc), lambda i, j, k: (i, j, k)),
        out_shape=SDS(theirs.shape, theirs.dtype),
        compiler_params=pltpu.CompilerParams(dimension_semantics=("arbitrary",) * 3, vmem_limit_bytes=VMEM_LIMIT),
    )(mine, theirs)


def _my_index():
    return 4 * lax.axis_index("x") + 2 * lax.axis_index("y") + lax.axis_index("c")


def _two_level_copies(stage, dst_cols, src_refs, land_refs, send_sems, recv_sems, landings):
    x, y, c = lax.axis_index("x"), lax.axis_index("y"), lax.axis_index("c")

    def pos(k):
        return (1 - x if k & 4 else x, 1 - y if k & 2 else y, 1 - c if k & 1 else c)

    def idx(k):
        px, py, pc = pos(k)
        return 4 * px + 2 * py + pc

    out = []
    for a, cols in enumerate(dst_cols):
        if stage == 1:
            for i, k in enumerate((1, 2, 4, 6)):
                out.append(pltpu.make_async_remote_copy(
                    src_ref=src_refs[a], dst_ref=_slot(land_refs[a], idx(k) if landings else idx(0), cols),
                    send_sem=send_sems.at[4 * a + i], recv_sem=recv_sems.at[4 * a + i],
                    device_id=pos(k), device_id_type=pl.DeviceIdType.MESH))
        else:
            for i, k in enumerate((2, 4, 6)):
                out.append(pltpu.make_async_remote_copy(
                    src_ref=_slot(land_refs[a], idx(k), cols),
                    dst_ref=_slot(land_refs[a], idx(k ^ 1) if landings else idx(k), cols),
                    send_sem=send_sems.at[3 * a + i], recv_sem=recv_sems.at[3 * a + i],
                    device_id=pos(1), device_id_type=pl.DeviceIdType.MESH))
    return out


def _gather2_start(name, srcs, dsts, after):
    n = len(srcs)
    dst_cols = [c for _, _, c in dsts]

    def body(*refs):
        src_refs, land_refs = refs[:n], refs[n:2 * n]
        send_sems, recv_sems = refs[2 * n + 1:2 * n + 3]
        me = _my_index()
        for a in range(n):
            pltpu.make_async_copy(src_refs[a], _slot(land_refs[a], me, dst_cols[a]), send_sems.at[4 * n + a]).start()
        for cp in _two_level_copies(1, dst_cols, src_refs, land_refs, send_sems, recv_sems, False):
            cp.start()
        refs[-1][...] = jnp.zeros_like(refs[-1])

    hbm = lambda a: pltpu.with_memory_space_constraint(a, pltpu.HBM)
    lands = [hbm(lax.empty(shape, dt)) for shape, dt, _ in dsts]
    res = pl.pallas_call(
        body, name=name,
        out_shape=(pltpu.SemaphoreType.DMA((5 * n,)), pltpu.SemaphoreType.DMA((4 * n,)),
                   *[pltpu.HBM(a.shape, a.dtype) for a, _ in srcs], *[pltpu.HBM(a.shape, a.dtype) for a in lands],
                   SDS((8, LANES), F32)),
        in_specs=[_HBM] * (2 * n) + [pl.BlockSpec(memory_space=pl.ANY)],
        out_specs=(_SEM, _SEM, *[_HBM] * (2 * n), pl.BlockSpec(memory_space=pltpu.VMEM)),
        input_output_aliases={i: 2 + i for i in range(2 * n)},
        compiler_params=pltpu.CompilerParams(has_side_effects=_EFFECT),
    )(*[hbm(a) for a, _ in srcs], *lands, after)
    return (res[0], res[1], res[2:2 + n], res[2 + n:2 + 2 * n], dst_cols), res[-1]


def _gather2_pass(name, handle, after):
    send1, recv1, src_thru, land_thru, dst_cols = handle
    n = len(src_thru)

    def body(*refs):
        src_refs, land_refs = refs[:n], refs[n:2 * n]
        s1, r1 = refs[2 * n:2 * n + 2]
        send2, recv2 = refs[2 * n + 3:2 * n + 5]
        me = _my_index()
        for cp in _two_level_copies(1, dst_cols, src_refs, land_refs, s1, r1, True):
            cp.wait_send()
            cp.wait_recv()
        for a in range(n):
            pltpu.make_async_copy(src_refs[a], _slot(land_refs[a], me, dst_cols[a]), s1.at[4 * n + a]).wait()
        for cp in _two_level_copies(2, dst_cols, src_refs, land_refs, send2, recv2, False):
            cp.start()
        refs[-1][...] = jnp.zeros_like(refs[-1])

    res = pl.pallas_call(
        body, name=name,
        out_shape=(pltpu.SemaphoreType.DMA((3 * n,)), pltpu.SemaphoreType.DMA((3 * n,)),
                   *[pltpu.HBM(a.shape, a.dtype) for a in (*src_thru, *land_thru)], SDS((8, LANES), F32)),
        in_specs=[_HBM] * (2 * n) + [_SEM, _SEM, pl.BlockSpec(memory_space=pl.ANY)],
        out_specs=(_SEM, _SEM, *[_HBM] * (2 * n), pl.BlockSpec(memory_space=pltpu.VMEM)),
        input_output_aliases={i: 2 + i for i in range(2 * n)},
        compiler_params=pltpu.CompilerParams(has_side_effects=_EFFECT),
    )(*src_thru, *land_thru, send1, recv1, after)
    return (res[0], res[1], res[2:2 + n], res[2 + n:2 + 2 * n], dst_cols), res[-1]


def _gather2_wait(name, handle, after):
    send2, recv2, src_thru, land_thru, dst_cols = handle
    n = len(src_thru)

    def body(*refs):
        src_refs, land_refs = refs[:n], refs[n:2 * n]
        s2, r2 = refs[2 * n:2 * n + 2]
        for cp in _two_level_copies(2, dst_cols, src_refs, land_refs, s2, r2, True):
            cp.wait_send()
            cp.wait_recv()

    res = pl.pallas_call(
        body, name=name,
        out_shape=tuple(pltpu.HBM(a.shape, a.dtype) for a in (*src_thru, *land_thru)),
        in_specs=[_HBM] * (2 * n) + [_SEM, _SEM, pl.BlockSpec(memory_space=pl.ANY)],
        out_specs=tuple([_HBM] * (2 * n)),
        input_output_aliases={i: i for i in range(2 * n)},
        compiler_params=pltpu.CompilerParams(has_side_effects=_EFFECT),
    )(*src_thru, *land_thru, send2, recv2, after)
    return res[n:]


def _exchange_sems(n):
    return [pltpu.SemaphoreType.DMA((n, N_DEV - 1)), pltpu.SemaphoreType.DMA((n, N_DEV - 1)),
            pltpu.SemaphoreType.DMA((n,))]


def _rms_res_fn(h, w):
    return _rms_fn(h, w)[0], h


def _add_epilogue(acc, res):
    return (acc + res,)


def _gather_plan(shards):
    srcs, dsts = [], []
    for n, sh in shards.items():
        r, c = sh.shape
        srcs.append((sh, None))
        if SHARDED[n] and c % LANES == 0:
            dsts.append(((r, N_DEV * c), sh.dtype, c))
        else:
            dsts.append(((N_DEV, r, c), sh.dtype, None))
    return srcs, dsts, True


def _w_in_segments():
    out = []
    for j in range(N_DEV):
        lo, hi = W_IN_SHARD * j, W_IN_SHARD * (j + 1)
        for a, b in ((lo, min(hi, DN_COLS)), (max(lo, DN_COLS), hi)):
            if a < b:
                out.append((j, a - lo, b - lo, a if a < DN_COLS else a + RW_OFF - DN_COLS))
    return out


def _w_in_to_padded(shards, tc=512):
    _, _, cols = shards.shape

    def body(g_ref, o_ref):
        o_ref[...] = jnp.zeros_like(o_ref)
        for j, a, b, dst in _w_in_segments():
            o_ref[dst:dst + b - a, :] = g_ref[j, a:b, :]

    return pl.pallas_call(
        body, grid=(cols // tc,), name="w_in_to_padded",
        in_specs=[pl.BlockSpec((N_DEV, W_IN_SHARD, tc), lambda i: (0, 0, i))],
        out_specs=pl.BlockSpec((IN_PAD, tc), lambda i: (0, i)),
        out_shape=SDS((IN_PAD, cols), shards.dtype),
        compiler_params=pltpu.CompilerParams(dimension_semantics=("arbitrary",), vmem_limit_bytes=VMEM_LIMIT),
    )(shards)


def _w_in_grad_to_shards(gw, tc=512):
    _, cols = gw.shape

    def body(w_ref, o_ref):
        for j, a, b, dst in _w_in_segments():
            o_ref[j, a:b, :] = w_ref[dst:dst + b - a, :]

    return pl.pallas_call(
        body, grid=(cols // tc,), name="w_in_grad_to_shards",
        in_specs=[pl.BlockSpec((IN_PAD, tc), lambda i: (0, i))],
        out_specs=pl.BlockSpec((N_DEV, W_IN_SHARD, tc), lambda i: (0, 0, i)),
        out_shape=SDS((N_DEV, W_IN_SHARD, cols), gw.dtype),
        compiler_params=pltpu.CompilerParams(dimension_semantics=("arbitrary",), vmem_limit_bytes=VMEM_LIMIT),
    )(gw)


def _gather_finish(names, outs):
    full = {}
    for n, arr in zip(names, outs):
        if n == "w_in":
            full[n] = _w_in_to_padded(arr)
        elif arr.ndim == 2:
            full[n] = arr
        elif SHARDED[n]:
            full[n] = arr.transpose(1, 0, 2).reshape(arr.shape[1], -1)
        else:
            full[n] = arr.reshape(-1, arr.shape[2])
    return full


def _scatter_plan(grads):
    srcs, dsts = [], []
    for n, gr in grads.items():
        if gr.ndim == 3:
            srcs.append((gr, None))
            dsts.append((gr.shape, gr.dtype, None))
            continue
        rows, cols = gr.shape
        if not SHARDED[n]:
            r, c = rows // N_DEV, cols
            srcs.append((gr.reshape(N_DEV, r, c), None))
        else:
            r, c = rows, cols // N_DEV
            if c % LANES == 0:
                srcs.append((gr, c))
            else:
                srcs.append((gr.reshape(r, N_DEV, c).transpose(1, 0, 2), None))
        dsts.append(((N_DEV, r, c), gr.dtype, None))
    return srcs, dsts, False


def _local_step(x, mem, target, wt, late):
    d = D_MODEL
    g = {}
    wt = dict(wt)
    grp_a = ("w_out", "xa_wq", "xa_wk", "xa_wv", "xa_wo")
    grp_b = ("ffn_w1", "ffn_w2")
    plan = lambda names: _gather_plan({n: late[n] for n in names})[:2]
    handle_a, tok_a = _gather2_start("late_gather_a_start", *plan(grp_a), wt["w_in"])
    handle_w1, tok_b = _gather2_start("late_gather_w1_start", *plan(("ffn_w1",)), tok_a)
    handle_w2, tok_c = _gather2_start("late_gather_w2_start", *plan(("ffn_w2",)), tok_b)
    mix_w = wt["mix_norm_w"] + (tok_a[0:1, 0:1] + tok_b[0:1, 0:1] + tok_c[0:1, 0:1])
    u = _row_fwd(_rms_fn, "mix_norm", [(x, d, 0)], [mix_w], [(d, BF16)], 256)[0]
    p = _matmul("in_proj", u, wt["w_in"], "nt", [F32], tn=1536)[0]
    c = _col_fwd(_conv_fn, "dn_conv", p, 0, 24, [wt["dn_conv_w"]])
    handle_a, tok = _gather2_pass("late_gather_a_pass", handle_a, c)
    dn_pre_tiles = [(c, DN_WIDTH, 0), (c, DN_WIDTH, 1), (p, LANES, 32)]
    dn_pre_params = [wt["dn_a_log"], wt["dn_dt_bias"]]
    qh, kh, gb, bb, gcb = _row_fwd(_dn_pre_fn, "dn_pre", dn_pre_tiles, [dn_pre_params[0] + tok[0:1, :], dn_pre_params[1]],
                                   [(DN_WIDTH, F32)] * 5, CHUNK)
    dn_arrs = [(qh, 0), (kh, 0), (c, 16), (gb, 0), (bb, 0), (gcb, 0)]
    o, kept_dn = _scan_fwd(_gdn_group, "gdn_scan", dn_arrs, DN_HEADS, 1)
    dn_post_tiles = [(o, DN_WIDTH, 0), (p, DN_WIDTH, 3)]
    o_dn = _row_fwd(_dn_post_fn, "dn_post", dn_post_tiles, [wt["dn_norm_w"]], [(DN_WIDTH, BF16)], 256)[0]

    ps = _col_fwd(_lerp_fn, "rw_shift", p, RW_OFF // LANES, 26, [wt["rw_mu"]])
    rw_pre_tiles = [(ps, RW_WIDTH, 0), (ps, RW_WIDTH, 1), (ps, RW_WIDTH, 2), (ps, LANES, 24), (ps, LANES, 25)]
    rw_pre_params = [wt[n] for n in ("rw_w0", "rw_a0", "rw_k_k", "rw_k_a", "rw_w2", "rw_a2", "rw_g2")]
    r, lw, k, v, al, be, gate, gcw = _row_fwd(_rw_pre_fn, "rw_pre", rw_pre_tiles, rw_pre_params,
                                              [(RW_WIDTH, F32)] * 8, CHUNK)
    rw_arrs = [(r, 0), (lw, 0), (k, 0), (v, 0), (al, 0), (be, 0), (gcw, 0)]
    y, kept_rw = _scan_fwd(_rw_group, "rw_scan", rw_arrs, RW_WIDTH // LANES, 2)
    handle_w1, tok = _gather2_pass("late_gather_w1_pass", handle_w1, y)
    rw_post_tiles = [(t, RW_WIDTH, 0) for t in (y, r, k, v, gate)]
    rw_post_params = [wt["rw_ln_w"], wt["rw_ln_b"], wt["rw_r_k"]]
    o_rw = _row_fwd(_rw_post_fn, "rw_post", rw_post_tiles, [rw_post_params[0] + tok[0:1, 0:1]] + rw_post_params[1:],
                    [(RW_WIDTH, BF16)], 128)[0]
    o_cat = jnp.concatenate([o_dn, o_rw], axis=1)
    wt.update(_gather_finish(grp_a, _gather2_wait("late_gather_a_wait", handle_a, o_cat)))
    h1 = _matmul("out_proj", o_cat, wt["w_out"], "nn", [F32], _add_epilogue, (x,))[0]

    handle_w2, tok = _gather2_pass("late_gather_w2_pass", handle_w2, h1)
    hn = _row_fwd(_rms_fn, "xa_norm", [(h1, d, 0)], [wt["xa_norm_w"] + tok[0:1, 0:1]], [(d, BF16)], 256)[0]
    mn = _row_fwd(_rms_fn, "mem_norm", [(mem, d, 0)], [wt["mem_norm_w"]], [(d, BF16)], 256)[0]
    q = _matmul("xa_q", hn, wt["xa_wq"], "nn", [F32])[0]
    kx = _matmul("xa_k", mn, wt["xa_wk"], "nn", [F32])[0]
    vx = _matmul("xa_v", mn, wt["xa_wv"], "nn", [F32])[0]
    ao = _row_fwd(_xattn_fn, "xattn", [(q, XA_WIDTH, 0)], [kx, vx], [(XA_WIDTH, BF16)], 256)[0]
    h2 = _matmul("xa_o", ao, wt["xa_wo"], "nn", [F32], _add_epilogue, (h1,))[0]

    f = _row_fwd(_rms_fn, "ffn_norm", [(h2, d, 0)], [wt["ffn_norm_w"]], [(d, BF16)], 256)[0]
    wt.update(_gather_finish(("ffn_w1",), _gather2_wait("late_gather_w1_wait", handle_w1, f)))
    a, hid = _matmul("ffn_up", f, wt["ffn_w1"], "nn", [F32, BF16],
                     lambda acc: (acc, jnp.square(jnp.maximum(acc, 0.0))))
    wt.update(_gather_finish(("ffn_w2",), _gather2_wait("late_gather_w2_wait", handle_w2, hid)))
    h3 = _matmul("ffn_down", hid, wt["ffn_w2"], "nn", [F32], _add_epilogue, (h2,))[0]
    loss8, dh3, g["final_norm_w"] = _loss_call(h3, target, wt["final_norm_w"])

    da = _matmul("ffn_down_dx", dh3, wt["ffn_w2"], "nt", [BF16],
                 lambda acc, av: (acc * 2.0 * jnp.maximum(av, 0.0),), (a,))[0]
    g["ffn_w2"] = _matmul("ffn_down_dw", hid, dh3, "tn", [BF16])[0]
    g["ffn_w1"] = _matmul("ffn_up_dw", f, da, "tn", [BF16])[0]
    pending = {}
    plan = _scatter_plan({n: g.pop(n) for n in grp_b})
    pending[grp_b], tok = _exchange_start("late_grad_b_start", *plan, loss8)
    dh2, g["ffn_norm_w"] = _matmul_norm_bwd("ffn_up_dx", da, wt["ffn_w1"], "nt", h2, wt["ffn_norm_w"], dh3, tok)

    dao = _matmul("xa_o_dx", dh2, wt["xa_wo"], "nt", [F32])[0]
    g["xa_wo"] = _matmul("xa_o_dw", ao, dh2, "tn", [BF16])[0]
    (dq,), (dkx, dvx) = _row_bwd(_xattn_fn, "xattn_bwd", [(q, XA_WIDTH, 0)], [kx, vx], [[(dao, XA_WIDTH, 0)]], 256)
    dh1, g["xa_norm_w"] = _matmul_norm_bwd("xa_q_dx", dq, wt["xa_wq"], "nt", h1, wt["xa_norm_w"], dh2)
    g["xa_wq"] = _matmul("xa_q_dw", hn, dq, "tn", [BF16])[0]
    g["xa_wk"] = _matmul("xa_k_dw", mn, dkx, "tn", [BF16])[0]
    g["xa_wv"] = _matmul("xa_v_dw", mn, dvx, "tn", [BF16])[0]
    dmn = _matmul("xa_k_dx", dkx, wt["xa_wk"], "nt", [F32])[0]
    dmn = _matmul("xa_v_dx", dvx, wt["xa_wv"], "nt", [F32], _add_epilogue, (dmn,))[0]
    _, (g["mem_norm_w"],) = _row_bwd(_rms_fn, "mem_norm_bwd", [(mem, d, 0)], [wt["mem_norm_w"]],
                                     [[(dmn, d, 0)]], 256, want_tiles=())

    do_cat = _matmul("out_proj_dx", dh1, wt["w_out"], "nt", [F32])[0]
    g["w_out"] = _matmul("out_proj_dw", o_cat, dh1, "tn", [BF16])[0]

    plan = _scatter_plan({n: g.pop(n) for n in grp_a})
    pending[grp_a], tok = _exchange_start("late_grad_a_start", *plan, tok)
    (dy, dr1, dk1, dv1, dgate), (g["rw_ln_w"], g["rw_ln_b"], g["rw_r_k"]) = _row_bwd(
        _rw_post_fn, "rw_post_bwd", rw_post_tiles, [rw_post_params[0] + tok[0:1, 0:1]] + rw_post_params[1:],
        [[(do_cat, RW_WIDTH, 1)]], 128)
    dr2, dlw, dk2, dv2, dal, dbe, dgcw = _scan_bwd(_rw_group, "rw_scan_bwd", rw_arrs, kept_rw, dy,
                                                   RW_WIDTH // LANES)
    one = lambda t: [(t, RW_WIDTH, 0)]
    two = lambda s, t: [(s, RW_WIDTH, 0), (t, RW_WIDTH, 0)]
    d_ps, rw_pre_grads = _row_bwd(
        _rw_pre_fn, "rw_pre_bwd", rw_pre_tiles, rw_pre_params,
        [two(dr1, dr2), one(dlw), two(dk1, dk2), two(dv1, dv2), one(dal), one(dbe), one(dgate), one(dgcw)],
        CHUNK)
    for n, val in zip(("rw_w0", "rw_a0", "rw_k_k", "rw_k_a", "rw_w2", "rw_a2", "rw_g2"), rw_pre_grads):
        g[n] = val
    dp_rw, (g["rw_mu"],) = _col_bwd(_lerp_fn, "rw_shift_bwd", p, RW_OFF // LANES, 26, [wt["rw_mu"]],
                                    jnp.concatenate(d_ps, axis=1))

    (do, dz), (g["dn_norm_w"],) = _row_bwd(_dn_post_fn, "dn_post_bwd", dn_post_tiles, [wt["dn_norm_w"]],
                                           [[(do_cat, DN_WIDTH, 0)]], 256)
    dqh, dkh, dv_dn, dgb, dbb, dgcb = _scan_bwd(_gdn_group, "gdn_scan_bwd", dn_arrs, kept_dn, do, DN_HEADS)
    one = lambda t: [(t, DN_WIDTH, 0)]
    (dcq, dck, dgates), (g["dn_a_log"], g["dn_dt_bias"]) = _row_bwd(
        _dn_pre_fn, "dn_pre_bwd", dn_pre_tiles, dn_pre_params,
        [one(dqh), one(dkh), one(dgb), one(dbb), one(dgcb)], CHUNK)
    dp_qkv, (g["dn_conv_w"],) = _col_bwd(_conv_fn, "dn_conv_bwd", p, 0, 24, [wt["dn_conv_w"]],
                                         jnp.concatenate([dcq, dck, dv_dn], axis=1))
    dp = jnp.concatenate([dp_qkv, dz, dgates, dp_rw, jnp.zeros((x.shape[0], LANES), F32)], axis=1).astype(BF16)
    g["w_in"] = _matmul("in_proj_dw", dp, u, "tn", [BF16], tm=1536)[0]
    early = _logical_grads(g)
    blocks = []
    for src, cols in _scatter_plan({n: early.pop(n) for n in EARLY})[0]:
        if cols is not None:
            src = src.reshape(src.shape[0], N_DEV, cols).transpose(1, 0, 2)
        blocks.append(src.reshape((4, 2) + src.shape[1:]))
    sums = [_pair_add("early_grad_pair_add_%d" % i, mine, theirs)
            for i, (mine, theirs) in enumerate(zip(blocks, _pair_swap("early_grad_pair_swap", blocks)))]
    pending[EARLY], tok = _exchange_start("early_grad_start", [(t, None) for t in sums],
                                          [(t.shape, t.dtype, None) for t in sums], False, tok, chips=True)
    dx, early["mix_norm_w"] = _matmul_norm_bwd("in_proj_dx", dp, wt["w_in"], "nn", x, wt["mix_norm_w"], dh1, tok)
    return loss8, dx, early, pending, tok


WEIGHTS = ["mix_norm_w", "w_in", "dn_conv_w", "dn_a_log", "dn_dt_bias", "dn_norm_w", "rw_mu", "rw_w0", "rw_w2",
           "rw_a0", "rw_a2", "rw_g2", "rw_k_k", "rw_k_a", "rw_r_k", "rw_ln_w", "rw_ln_b", "w_out", "xa_norm_w",
           "mem_norm_w", "xa_wq", "xa_wk", "xa_wv", "xa_wo", "ffn_norm_w", "ffn_w1", "ffn_w2", "final_norm_w"]
SHARDED = {"w_in": False, "w_out": False, "xa_wq": False, "xa_wk": False, "xa_wv": False, "xa_wo": True,
           "ffn_w1": True, "ffn_w2": False, "dn_conv_w": True, "rw_w2": True, "rw_a2": True, "rw_g2": True}
BF16_PAYLOAD = ("w_in", "w_out", "xa_wq", "xa_wk", "xa_wv", "xa_wo", "ffn_w1", "ffn_w2")
REPLICATED = [n for n in WEIGHTS if n not in SHARDED]
EARLY = ("w_in", "dn_conv_w", "rw_w2", "rw_a2", "rw_g2")
RW_IN_COLS = IN_COLS - DN_COLS
W_IN_SHARD = IN_COLS // N_DEV


def _layout_weights(fw):
    wt = dict(fw)
    wt["dn_conv_w"] = jnp.pad(fw["dn_conv_w"], ((0, 4), (0, 0)))
    wt["dn_a_log"] = jnp.pad(fw["dn_a_log"], ((0, 0), (0, LANES - DN_HEADS)))
    wt["dn_dt_bias"] = jnp.pad(fw["dn_dt_bias"], ((0, 0), (0, LANES - DN_HEADS)))
    wt["rw_w2"] = jnp.pad(fw["rw_w2"], ((0, 64), (0, 0)))
    wt["rw_a2"] = jnp.pad(fw["rw_a2"], ((64, 0), (0, 0)))
    return wt


def _logical_grads(g):
    out = dict(g)
    out["w_in"] = _w_in_grad_to_shards(g["w_in"])
    out["dn_conv_w"] = g["dn_conv_w"][:4]
    out["dn_a_log"] = g["dn_a_log"][:, :DN_HEADS]
    out["dn_dt_bias"] = g["dn_dt_bias"][:, :DN_HEADS]
    out["rw_w2"] = g["rw_w2"][:64]
    out["rw_a2"] = g["rw_a2"][64:]
    return out


def _pack(vals):
    parts = []
    for v in vals:
        flat = v.reshape(-1)
        parts.append(jnp.pad(flat, (0, -flat.shape[0] % LANES)))
    flat = jnp.concatenate(parts)
    flat = jnp.pad(flat, (0, -flat.shape[0] % (8 * LANES)))
    return flat.reshape(-1, LANES)


def _unpack(packed, shapes):
    flat = packed.reshape(-1)
    out, at = [], 0
    for shp in shapes:
        size = math.prod(shp)
        out.append(flat[at:at + size].reshape(shp))
        at += size + (-size % LANES)
    return out


def kernel(x, mem, mix_norm_w, w_in, dn_conv_w, dn_a_log, dn_dt_bias, dn_norm_w, rw_mu, rw_w0, rw_w2, rw_a0, rw_a2, rw_g2, rw_k_k, rw_k_a, rw_r_k, rw_ln_w, rw_ln_b, w_out, xa_norm_w, mem_norm_w, xa_wq, xa_wk, xa_wv, xa_wo, ffn_norm_w, ffn_w1, ffn_w2, final_norm_w, loss_target, m_mix_norm_w, m_w_in, m_dn_conv_w, m_dn_a_log, m_dn_dt_bias, m_dn_norm_w, m_rw_mu, m_rw_w0, m_rw_w2, m_rw_a0, m_rw_a2, m_rw_g2, m_rw_k_k, m_rw_k_a, m_rw_r_k, m_rw_ln_w, m_rw_ln_b, m_w_out, m_xa_norm_w, m_mem_norm_w, m_xa_wq, m_xa_wk, m_xa_wv, m_xa_wo, m_ffn_norm_w, m_ffn_w1, m_ffn_w2, m_final_norm_w, v_mix_norm_w, v_w_in, v_dn_conv_w, v_dn_a_log, v_dn_dt_bias, v_dn_norm_w, v_rw_mu, v_rw_w0, v_rw_w2, v_rw_a0, v_rw_a2, v_rw_g2, v_rw_k_k, v_rw_k_a, v_rw_r_k, v_rw_ln_w, v_rw_ln_b, v_w_out, v_xa_norm_w, v_mem_norm_w, v_xa_wq, v_xa_wk, v_xa_wv, v_xa_wo, v_ffn_norm_w, v_ffn_w1, v_ffn_w2, v_final_norm_w):
    given = dict(locals())
    w = {n: given[n] for n in WEIGHTS}
    m = {n: given["m_" + n] for n in WEIGHTS}
    v = {n: given["v_" + n] for n in WEIGHTS}

    local = {n: (lambda t: t[0].T) if n == "w_in" else (lambda t: t[0]) for n in SHARDED}
    shards = {n: (local[n](w[n]).astype(BF16) if n in BF16_PAYLOAD else local[n](w[n])) for n in SHARDED}
    srcs, dsts, _ = _gather_plan({n: shards[n] for n in EARLY})
    full = _gather_finish(EARLY, _gather_two_level("early_all_gather", srcs, dsts))
    for n in REPLICATED:
        full[n] = w[n].reshape(1, -1)

    loss8, dx, g, pending, after = _local_step(x[0], mem[0], loss_target[0], _layout_weights(full),
                                               {n: shards[n] for n in SHARDED if n not in EARLY})
    loss = lax.psum(loss8[0, 0], ("x", "y", "c"))

    packed = _pack([g[n] for n in REPLICATED])
    small, _ = _exchange_start("small_gather_start", [(packed, None)], [((N_DEV,) + packed.shape, F32, None)], True,
                               after)
    grad, delta, new_m, new_v = {}, {}, {}, {}
    done = [dx]

    def tie():
        return jnp.broadcast_to(sum(t[:1, :1] for t in done), (8, LANES))

    for names in sorted(pending, key=lambda names: names == EARLY):
        handle = pending[names]
        for n, parts in zip(names, _exchange_wait("grad_wait_" + names[0], handle, tie())):
            res = _sum_adamw("adamw_" + n, parts, local[n](w[n]), local[n](m[n]), local[n](v[n]))
            grad[n], delta[n], new_m[n], new_v[n] = [(t.T if n == "w_in" else t)[None] for t in res]
            done.append(res[1])

    (parts,) = _exchange_wait("small_gather_wait", small, tie())
    res = _sum_adamw("adamw_small", parts, _pack([w[n] for n in REPLICATED]),
                     _pack([m[n] for n in REPLICATED]), _pack([v[n] for n in REPLICATED]))
    shapes = [w[n].shape for n in REPLICATED]
    for store, packed_out in zip((grad, delta, new_m, new_v), res):
        for n, val in zip(REPLICATED, _unpack(packed_out, shapes)):
            store[n] = val

    return (loss, dx[None], *[grad[n] for n in WEIGHTS], *[delta[n] for n in WEIGHTS],
            *[new_m[n] for n in WEIGHTS], *[new_v[n] for n in WEIGHTS])
```

```python
import functools
import math

import jax
import jax.numpy as jnp
from jax import lax
from jax.experimental import pallas as pl
from jax.experimental.pallas import tpu as pltpu

F32 = jnp.float32
BF16 = jnp.bfloat16
SDS = jax.ShapeDtypeStruct

N_DEV = 8
D_MODEL = 2048
LANES = 128
CHUNK = 128
DN_HEADS = 8
DN_WIDTH = 1024
RW_WIDTH = 1024
RW_HEAD = 64
XA_HEADS = 4
XA_WIDTH = 512
FFN_HIDDEN = 8192
IN_COLS = 7440
DN_COLS = 4112
IN_PAD = 7680
RW_OFF = 4224
RMS_EPS = 1e-6
RW_GN_EPS = 64e-5
VMEM_LIMIT = 56 * 1024 * 1024

ADAM_LR = 0.001
ADAM_B1 = 0.9
ADAM_B2 = 0.999
ADAM_EPS = 1e-08
ADAM_WD = 0.01
ADAM_STEP = 10

_DIMS = {"nn": (((1,), (0,)), ((), ())), "nt": (((1,), (1,)), ((), ())), "tn": (((0,), (0,)), ((), ()))}


def _raw_dot(a, b, mode, hi):
    if hi:
        return lax.dot_general(a, b, _DIMS[mode], precision=lax.Precision.HIGHEST, preferred_element_type=F32)
    return lax.dot_general(a.astype(BF16), b.astype(BF16), _DIMS[mode], preferred_element_type=F32)


@functools.partial(jax.custom_vjp, nondiff_argnums=(2, 3))
def mm(a, b, mode="nn", hi=False):
    return _raw_dot(a, b, mode, hi)


def _mm_fwd(a, b, mode, hi):
    return _raw_dot(a, b, mode, hi), (a, b)


def _mm_bwd(mode, hi, res, g):
    a, b = res
    if mode == "nn":
        return _raw_dot(g, b, "nt", hi), _raw_dot(a, g, "tn", hi)
    if mode == "nt":
        return _raw_dot(g, b, "nn", hi), _raw_dot(g, a, "tn", hi)
    return _raw_dot(b, g, "nt", hi), _raw_dot(a, g, "nn", hi)


mm.defvjp(_mm_fwd, _mm_bwd)


def _shift_rows_raw(x, k):
    n = x.shape[0]
    rolled = pltpu.roll(x, k % n, axis=0)
    row = lax.broadcasted_iota(jnp.int32, x.shape, 0)
    keep = row >= k if k > 0 else row < n + k
    return jnp.where(keep, rolled, 0.0)


@functools.partial(jax.custom_vjp, nondiff_argnums=(1,))
def shift_rows(x, k):
    return _shift_rows_raw(x, k)


shift_rows.defvjp(lambda x, k: (_shift_rows_raw(x, k), None), lambda k, _, g: (_shift_rows_raw(g, -k),))


def _softplus(x):
    return jnp.maximum(x, 0.0) + jnp.log(1.0 + jnp.exp(-jnp.abs(x)))


def _sigmoid(x):
    return 1.0 / (1.0 + jnp.exp(-x))


def _silu(x):
    return x * _sigmoid(x)


def _tri_masks(n):
    ii = lax.broadcasted_iota(jnp.int32, (n, n), 0)
    jj = lax.broadcasted_iota(jnp.int32, (n, n), 1)
    return ii >= jj, ii > jj, ii == jj


def _neumann_inv_raw(m):
    n = m.shape[0]
    _, _, eye = _tri_masks(n)
    eye = jnp.where(eye, 1.0, 0.0)
    p = eye + m
    mk = m
    for _ in range(int(math.log2(n)) - 1):
        mk = _raw_dot(mk, mk, "nn", False)
        p = p + _raw_dot(p, mk, "nn", False)
    resid = eye - p + _raw_dot(m, p, "nn", True)
    return p + _raw_dot(p, resid, "nn", False)


@jax.custom_vjp
def _neumann_inv(m):
    return _neumann_inv_raw(m)


def _neumann_inv_fwd(m):
    p = _neumann_inv_raw(m)
    return p, p


def _neumann_inv_bwd(p, g):
    return (_raw_dot(_raw_dot(p, g, "tn", False), p, "nt", False),)


_neumann_inv.defvjp(_neumann_inv_fwd, _neumann_inv_bwd)


@jax.custom_vjp
def _saved_inv(m, p):
    return p


_saved_inv.defvjp(lambda m, p: (p, p), lambda p, g: (_neumann_inv_bwd(p, g)[0], jnp.zeros_like(p)))


def _inverse(m, saved):
    return _neumann_inv(m) if saved is None else _saved_inv(m, saved)


def _cumsum_rows(x):
    causal, _, _ = _tri_masks(x.shape[0])
    return mm(jnp.where(causal, 1.0, 0.0), x, "nn", True)


def _gdn_group(s0, q, k, v, gb, bb, gc, *saved):
    diff = jnp.stack([gc[j] - gc[j].T for j in range(gc.shape[0])])
    return jax.vmap(_gdn_chunk)(s0, q, k, v, gb, bb, gc, diff, *saved)


def _rw_group(*args):
    return jax.vmap(_rw_chunk)(*args)


def _gdn_chunk(s0, q, k, v, gb, bb, gc, diff, saved=None):
    c = q.shape[0]
    causal, strict, _ = _tri_masks(c)
    decay = jnp.exp(jnp.where(causal, diff, -jnp.inf))
    kb = k * bb
    a = jnp.where(strict, mm(kb, k, "nt") * decay, 0.0)
    p = _inverse(-a, saved)
    u = mm(p, v * bb)
    w = mm(p, kb * jnp.exp(gc))
    attn = mm(q, k, "nt") * decay
    v_new = u - mm(w, s0)
    o = mm(q * jnp.exp(gc), s0) + mm(attn, v_new)
    g_last = jnp.sum(gb, axis=0, keepdims=True)
    s1 = s0 * jnp.exp(g_last) + mm(k * jnp.exp(g_last - gc), v_new, "tn")
    return o, s1, p


def _rw_chunk(s0, r, lw, k, v, al, be, gc, saved0=None, saved1=None):
    c = r.shape[0]
    causal, strict, _ = _tri_masks(c)
    gp = gc - lw
    row = lax.broadcasted_iota(jnp.int32, lw.shape, 0)
    lane = lax.broadcasted_iota(jnp.int32, lw.shape, 1)
    g_mid = jnp.sum(jnp.where(row < c // 2, lw, 0.0), axis=0, keepdims=True)
    g_last = jnp.sum(lw, axis=0, keepdims=True)
    e_n = jnp.exp(g_mid - gc)
    rg = r * jnp.exp(gc - g_mid)
    bg = be * jnp.exp(gp - g_mid)
    an = al * e_n
    kn = k * e_n
    bt = mm(be * jnp.exp(gp), s0, "nt")
    rt = mm(r * jnp.exp(gc), s0, "nt")
    us, ys, ps = [], [], []
    for h, saved in enumerate((saved0, saved1)):
        mine = (lane >= RW_HEAD) if h else (lane < RW_HEAD)
        bgh = jnp.where(mine, bg, 0.0)
        rgh = jnp.where(mine, rg, 0.0)
        a_ab = jnp.where(strict, mm(bgh, an, "nt"), 0.0)
        a_kb = jnp.where(strict, mm(bgh, kn, "nt"), 0.0)
        a_ra = jnp.where(causal, mm(rgh, an, "nt"), 0.0)
        a_rk = jnp.where(causal, mm(rgh, kn, "nt"), 0.0)
        p = _inverse(a_ab, saved)
        ps.append(p)
        u_h = mm(p, bt + mm(a_kb, v))
        us.append(u_h)
        ys.append(rt + mm(a_ra, u_h) + mm(a_rk, v))
    lo = lane < RW_HEAD
    u = jnp.where(lo, us[0], us[1])
    y = jnp.where(lo, ys[0], ys[1])
    tail = jnp.exp(g_last - gc)
    s1 = s0 * jnp.exp(g_last) + mm(u, al * tail, "tn") + mm(v, k * tail, "tn")
    vi = lax.broadcasted_iota(jnp.int32, s0.shape, 0)
    ki = lax.broadcasted_iota(jnp.int32, s0.shape, 1)
    s1 = jnp.where((vi < RW_HEAD) == (ki < RW_HEAD), s1, 0.0)
    return y, s1, ps[0], ps[1]


SCAN_HB = 8


def _scan_specs(arrs, n_chunks, reverse):
    def spec(off):
        assert off % SCAN_HB == 0
        if reverse:
            return pl.BlockSpec((CHUNK, SCAN_HB * LANES), lambda h, n: (n_chunks - 1 - n, off // SCAN_HB + h))
        return pl.BlockSpec((CHUNK, SCAN_HB * LANES), lambda h, n: (n, off // SCAN_HB + h))
    return [spec(off) for _, off in arrs]


def _split_heads(x):
    return jnp.stack([x[:, LANES * j:LANES * (j + 1)] for j in range(SCAN_HB)], axis=0)


def _merge_heads(x):
    return jnp.concatenate([x[j] for j in range(SCAN_HB)], axis=1)


def _scan_fwd(group_fn, name, arrs, heads, n_kept):
    s = arrs[0][0].shape[0]
    n_chunks = s // CHUNK
    n_in = len(arrs)

    def body(*refs):
        y_ref, st_ref = refs[n_in:n_in + 2]
        kept_refs, s_scr = refs[n_in + 2:-1], refs[-1]

        @pl.when(pl.program_id(1) == 0)
        def _():
            s_scr[...] = jnp.zeros_like(s_scr)

        s0 = s_scr[...]
        st_ref[...] = s0
        y, s1, *kept = group_fn(s0, *[_split_heads(r[...]) for r in refs[:n_in]])
        y_ref[...] = _merge_heads(y)
        s_scr[...] = s1
        for ref, val in zip(kept_refs, kept):
            ref[...] = val

    per_chunk = pl.BlockSpec((SCAN_HB, None, LANES, LANES), lambda h, n: (h, n, 0, 0))
    res = pl.pallas_call(
        body, grid=(heads // SCAN_HB, n_chunks), name=name,
        in_specs=_scan_specs(arrs, n_chunks, False),
        out_specs=[pl.BlockSpec((CHUNK, SCAN_HB * LANES), lambda h, n: (n, h))] + [per_chunk] * (1 + n_kept),
        out_shape=[SDS((s, heads * LANES), F32)] + [SDS((heads, n_chunks, LANES, LANES), F32)] * (1 + n_kept),
        scratch_shapes=[pltpu.VMEM((SCAN_HB, LANES, LANES), F32)],
        compiler_params=pltpu.CompilerParams(dimension_semantics=("arbitrary", "arbitrary")),
    )(*[a for a, _ in arrs])
    return res[0], res[1:]


def _scan_bwd(group_fn, name, arrs, kept, dy, heads):
    s = arrs[0][0].shape[0]
    n_chunks = s // CHUNK
    n_in, n_kept = len(arrs), len(kept)

    def body(*refs):
        kept_vals = [r[...] for r in refs[n_in:n_in + n_kept]]
        dy_ref = refs[n_in + n_kept]
        d_refs = refs[n_in + n_kept + 1:2 * n_in + n_kept + 1]
        ds_scr = refs[-1]

        @pl.when(pl.program_id(1) == 0)
        def _():
            ds_scr[...] = jnp.zeros_like(ds_scr)

        def fn(s0, *ins):
            return group_fn(s0, *ins, *kept_vals[1:])[:2]

        _, vjp = jax.vjp(fn, kept_vals[0], *[_split_heads(r[...]) for r in refs[:n_in]])
        grads = vjp((_split_heads(dy_ref[...]), ds_scr[...]))
        ds_scr[...] = grads[0]
        for ref, g in zip(d_refs, grads[1:]):
            ref[...] = _merge_heads(g)

    rev = pl.BlockSpec((CHUNK, SCAN_HB * LANES), lambda h, n: (n_chunks - 1 - n, h))
    per_chunk = pl.BlockSpec((SCAN_HB, None, LANES, LANES), lambda h, n: (h, n_chunks - 1 - n, 0, 0))
    return pl.pallas_call(
        body, grid=(heads // SCAN_HB, n_chunks), name=name,
        in_specs=_scan_specs(arrs, n_chunks, True) + [per_chunk] * n_kept + [rev],
        out_specs=[rev] * n_in,
        out_shape=[SDS((s, heads * LANES), F32)] * n_in,
        scratch_shapes=[pltpu.VMEM((SCAN_HB, LANES, LANES), F32)],
        compiler_params=pltpu.CompilerParams(dimension_semantics=("arbitrary", "arbitrary")),
    )(*[a for a, _ in arrs], *kept, dy)


def _col_spec(tr, width, cb):
    return pl.BlockSpec((tr, width), lambda i: (i, cb))


def _whole(p):
    return pl.BlockSpec(p.shape, lambda i: (0,) * p.ndim)


def _row_fwd(fn, name, tiles, params, outs, tr):
    rows = tiles[0][0].shape[0]
    nt, npar = len(tiles), len(params)

    def body(*refs):
        vals = [r[...].astype(F32) for r in refs[:nt + npar]]
        for ref, o in zip(refs[nt + npar:], fn(*vals)):
            ref[...] = o.astype(ref.dtype)

    return pl.pallas_call(
        body, grid=(rows // tr,), name=name,
        in_specs=[_col_spec(tr, w, cb) for _, w, cb in tiles] + [_whole(p) for p in params],
        out_specs=[_col_spec(tr, w, 0) for w, _ in outs],
        out_shape=[SDS((rows, w), dt) for w, dt in outs],
        compiler_params=pltpu.CompilerParams(dimension_semantics=("arbitrary",), vmem_limit_bytes=VMEM_LIMIT),
    )(*[a for a, _, _ in tiles], *params)


def _row_bwd(fn, name, tiles, params, cts, tr, want_tiles=None):
    rows = tiles[0][0].shape[0]
    nt, npar = len(tiles), len(params)
    want = list(range(nt)) if want_tiles is None else list(want_tiles)
    flat_cts = [c for group in cts for c in group]
    n_ct = len(flat_cts)

    def body(*refs):
        vals = [r[...].astype(F32) for r in refs[:nt + npar]]
        ct_refs = refs[nt + npar:nt + npar + n_ct]
        out_refs = refs[nt + npar + n_ct:]
        ct_vals, at = [], 0
        for group in cts:
            total = ct_refs[at][...].astype(F32)
            for r in ct_refs[at + 1:at + len(group)]:
                total = total + r[...].astype(F32)
            ct_vals.append(total)
            at += len(group)
        _, vjp = jax.vjp(lambda *a: tuple(fn(*a)), *vals)
        grads = vjp(tuple(ct_vals))
        for ref, t in zip(out_refs[:len(want)], want):
            ref[...] = grads[t]
        first = pl.program_id(0) == 0
        for ref, g in zip(out_refs[len(want):], grads[nt:]):
            @pl.when(first)
            def _(ref=ref, g=g):
                ref[...] = g

            @pl.when(jnp.logical_not(first))
            def _(ref=ref, g=g):
                ref[...] += g

    res = pl.pallas_call(
        body, grid=(rows // tr,), name=name,
        in_specs=[_col_spec(tr, w, cb) for _, w, cb in tiles] + [_whole(p) for p in params]
        + [_col_spec(tr, w, cb) for _, w, cb in flat_cts],
        out_specs=[_col_spec(tr, tiles[t][1], 0) for t in want] + [_whole(p) for p in params],
        out_shape=[SDS((rows, tiles[t][1]), F32) for t in want] + [SDS(p.shape, F32) for p in params],
        compiler_params=pltpu.CompilerParams(dimension_semantics=("arbitrary",), vmem_limit_bytes=VMEM_LIMIT),
    )(*[a for a, _, _ in tiles], *params, *[a for a, _, _ in flat_cts])
    return res[:len(want)], res[len(want):]


def _col_fwd(fn, name, x, first_block, n_blocks, params):
    rows = x.shape[0]

    def body(*refs):
        refs[-1][...] = fn(*[r[...] for r in refs[:-1]])

    return pl.pallas_call(
        body, grid=(n_blocks,), name=name,
        in_specs=[pl.BlockSpec((rows, LANES), lambda j: (0, first_block + j))]
        + [pl.BlockSpec((p.shape[0], LANES), lambda j: (0, j)) for p in params],
        out_specs=pl.BlockSpec((rows, LANES), lambda j: (0, j)),
        out_shape=SDS((rows, n_blocks * LANES), F32),
        compiler_params=pltpu.CompilerParams(dimension_semantics=("arbitrary",), vmem_limit_bytes=VMEM_LIMIT),
    )(x, *params)


def _col_bwd(fn, name, x, first_block, n_blocks, params, dy):
    rows = x.shape[0]
    npar = len(params)

    def body(*refs):
        vals = [r[...] for r in refs[:1 + npar]]
        _, vjp = jax.vjp(fn, *vals)
        grads = vjp(refs[1 + npar][...])
        for ref, g in zip(refs[2 + npar:], grads):
            ref[...] = g

    pspecs = [pl.BlockSpec((p.shape[0], LANES), lambda j: (0, j)) for p in params]
    blk = pl.BlockSpec((rows, LANES), lambda j: (0, j))
    res = pl.pallas_call(
        body, grid=(n_blocks,), name=name,
        in_specs=[pl.BlockSpec((rows, LANES), lambda j: (0, first_block + j))] + pspecs + [blk],
        out_specs=[blk] + pspecs,
        out_shape=[SDS((rows, n_blocks * LANES), F32)] + [SDS(p.shape, F32) for p in params],
        compiler_params=pltpu.CompilerParams(dimension_semantics=("arbitrary",), vmem_limit_bytes=VMEM_LIMIT),
    )(x, *params, dy)
    return res[0], res[1:]


def _conv_fn(x, w):
    acc = x * w[3:4, :]
    for j in range(3):
        acc = acc + shift_rows(x, 3 - j) * w[j:j + 1, :]
    return _silu(acc)


def _lerp_fn(x, mu):
    return x + (shift_rows(x, 1) - x) * mu[0:1, :]


def _seg_sum(x, width):
    if width == LANES:
        return jnp.sum(x, axis=1, keepdims=True)
    lo = lax.broadcasted_iota(jnp.int32, x.shape, 1) < width
    s0 = jnp.sum(jnp.where(lo, x, 0.0), axis=1, keepdims=True)
    s1 = jnp.sum(jnp.where(lo, 0.0, x), axis=1, keepdims=True)
    return jnp.where(lo, s0, s1)


def _per_block(fn, *xs):
    n = xs[0].shape[1] // LANES
    return jnp.concatenate([fn(*[x[:, LANES * b:LANES * (b + 1)] for x in xs]) for b in range(n)], axis=1)


def _head_expand(col0):
    r = lax.broadcasted_iota(jnp.int32, (LANES, DN_WIDTH), 0)
    c = lax.shift_right_logical(lax.broadcasted_iota(jnp.int32, (LANES, DN_WIDTH), 1), 7)
    return jnp.where(r == c + col0, 1.0, 0.0)


def _dn_pre_fn(cq, ck, gates, a_log, dt_bias):
    l2 = lambda x: x * lax.rsqrt(_seg_sum(x * x, LANES) + 1e-6)
    qh = _per_block(l2, cq) * (LANES ** -0.5)
    kh = _per_block(l2, ck)
    g = -jnp.exp(a_log) * _softplus(gates + dt_bias)
    gb = mm(g, _head_expand(0), "nn", True)
    bb = mm(_sigmoid(gates), _head_expand(DN_HEADS), "nn", True)
    return qh, kh, gb, bb, _cumsum_rows(gb)


def _dn_post_fn(o, z, nw):
    def one(ob, zb):
        return ob * lax.rsqrt(_seg_sum(ob * ob, LANES) * (1.0 / LANES) + RMS_EPS) * nw * _silu(zb)
    return (_per_block(one, o, z),)


def _rw_pre_fn(pr, pk, pv, pwa, pg, w0, a0, k_k, k_a, w2p, a2p, g2):
    log_w = -_softplus(-(w0 + mm(jnp.tanh(pwa), w2p))) - 0.5
    lw = -jnp.exp(log_w)
    a = _sigmoid(a0 + mm(pwa, a2p))
    gate = mm(_sigmoid(pg), g2)
    kk = pk * k_k
    kk = _per_block(lambda x: x / jnp.maximum(jnp.sqrt(_seg_sum(x * x, RW_HEAD)), 1e-12), kk)
    k = pk * (1.0 + (a - 1.0) * k_a)
    return pr, lw, k, pv, kk * a, -kk, gate, _cumsum_rows(lw)


def _rw_post_fn(y, r, k, v, gate, ln_w, ln_b, r_k):
    def one(yb, rb, kb, vb, gb, wb, bb, rkb):
        d = yb - _seg_sum(yb, RW_HEAD) * (1.0 / RW_HEAD)
        var = _seg_sum(d * d, RW_HEAD) * (1.0 / RW_HEAD)
        yn = d * lax.rsqrt(var + RW_GN_EPS) * wb + bb
        return (yn + _seg_sum(rb * kb * rkb, RW_HEAD) * vb) * gb
    return (_per_block(one, y, r, k, v, gate, ln_w, ln_b, r_k),)


def _rms_fn(h, w):
    return (h * lax.rsqrt(jnp.mean(h * h, axis=1, keepdims=True) + RMS_EPS) * w,)


def _xattn_fn(q, k, v):
    outs = []
    for h in range(XA_HEADS):
        sl = slice(LANES * h, LANES * (h + 1))
        s = mm(q[:, sl], k[:, sl], "nt") * (LANES ** -0.5)
        e = jnp.exp(s - jnp.max(s, axis=1, keepdims=True))
        outs.append(mm(e / jnp.sum(e, axis=1, keepdims=True), v[:, sl]))
    return (jnp.concatenate(outs, axis=1),)


def _fit(tile, dim):
    best = [t for t in range(LANES, min(tile, dim) + 1, LANES) if dim % t == 0]
    assert best, (tile, dim)
    return best[-1]


def _matmul(name, a, b, mode, out_dtypes, epilogue=None, extras=(), tm=1024, tn=1024, tk=2048, after=None):
    if mode == "tn":
        (k_dim, m), n = a.shape, b.shape[1]
    else:
        (m, k_dim), n = a.shape, (b.shape[1] if mode == "nn" else b.shape[0])
    tm, tn, tk = _fit(tm, m), _fit(tn, n), _fit(tk, k_dim)
    nk = k_dim // tk
    a_spec = (pl.BlockSpec((tk, tm), lambda i, j, k: (k, i)) if mode == "tn"
              else pl.BlockSpec((tm, tk), lambda i, j, k: (i, k)))
    b_spec = (pl.BlockSpec((tn, tk), lambda i, j, k: (j, k)) if mode == "nt"
              else pl.BlockSpec((tk, tn), lambda i, j, k: (k, j)))
    o_spec = pl.BlockSpec((tm, tn), lambda i, j, k: (i, j))
    n_ex, n_out = len(extras), len(out_dtypes)
    ties = [] if after is None else [after]

    def finish(total, rest):
        ex = [r[...].astype(F32) for r in rest[:n_ex]]
        res = epilogue(total, *ex) if epilogue else (total,)
        for ref, o in zip(rest[n_ex + len(ties):n_ex + len(ties) + n_out], res):
            ref[...] = o.astype(ref.dtype)

    def body_single(a_ref, b_ref, *rest):
        finish(_raw_dot(a_ref[...], b_ref[...], mode, False), rest)

    def body_acc(a_ref, b_ref, *rest):
        acc = rest[-1]
        k = pl.program_id(2)

        @pl.when(k == 0)
        def _():
            acc[...] = jnp.zeros_like(acc)

        acc[...] += _raw_dot(a_ref[...], b_ref[...], mode, False)

        @pl.when(k == nk - 1)
        def _():
            finish(acc[...], rest)

    res = pl.pallas_call(
        body_single if nk == 1 else body_acc, grid=(m // tm, n // tn, nk), name=name,
        in_specs=[a_spec, b_spec] + [o_spec] * n_ex + [pl.BlockSpec((8, LANES), lambda i, j, k: (0, 0))] * len(ties),
        out_specs=[o_spec] * n_out,
        out_shape=[SDS((m, n), dt) for dt in out_dtypes],
        scratch_shapes=[] if nk == 1 else [pltpu.VMEM((tm, tn), F32)],
        compiler_params=pltpu.CompilerParams(dimension_semantics=("parallel", "parallel", "arbitrary"),
                                             vmem_limit_bytes=VMEM_LIMIT),
    )(a, b, *extras, *ties)
    return res


def _matmul_norm_bwd(name, a, b, mode, h, w, dres, after=None, tm=512, tk=1024):
    m, n = h.shape
    k_dim = a.shape[1]
    tm, tk = _fit(tm, m), _fit(tk, k_dim)
    nk = k_dim // tk
    ties = [] if after is None else [after]
    a_spec = pl.BlockSpec((tm, tk), lambda i, k: (i, k))
    b_spec = pl.BlockSpec((n, tk), lambda i, k: (0, k)) if mode == "nt" else pl.BlockSpec((tk, n), lambda i, k: (k, 0))
    row = pl.BlockSpec((tm, n), lambda i, k: (i, 0))
    w_spec = pl.BlockSpec((1, n), lambda i, k: (0, 0))

    def body(a_ref, b_ref, h_ref, w_ref, dres_ref, *rest):
        dh_ref, dw_ref, acc = rest[len(ties):]
        i, k = pl.program_id(0), pl.program_id(1)

        @pl.when(k == 0)
        def _():
            acc[...] = jnp.zeros_like(acc)

        acc[...] += _raw_dot(a_ref[...], b_ref[...], mode, False)

        @pl.when(k == nk - 1)
        def _():
            _, vjp = jax.vjp(_rms_res_fn, h_ref[...], w_ref[...])
            dh, dw = vjp((acc[...], dres_ref[...]))
            dh_ref[...] = dh

            @pl.when(i == 0)
            def _():
                dw_ref[...] = dw

            @pl.when(i != 0)
            def _():
                dw_ref[...] += dw

    return pl.pallas_call(
        body, grid=(m // tm, nk), name=name,
        in_specs=[a_spec, b_spec, row, w_spec, row] + [pl.BlockSpec((8, LANES), lambda i, k: (0, 0))] * len(ties),
        out_specs=[row, w_spec],
        out_shape=[SDS((m, n), F32), SDS((1, n), F32)],
        scratch_shapes=[pltpu.VMEM((tm, n), F32)],
        compiler_params=pltpu.CompilerParams(dimension_semantics=("arbitrary", "arbitrary"),
                                             vmem_limit_bytes=VMEM_LIMIT),
    )(a, b, h, w, dres, *ties)


def _loss_call(h, target, w, tr=256):
    rows, d = h.shape

    def fn(hv, wv, tv):
        y = _rms_fn(hv, wv)[0]
        return 0.5 * jnp.sum(jnp.mean(jnp.square(y - tv), axis=1, keepdims=True), axis=0, keepdims=True)

    def body(h_ref, t_ref, w_ref, loss_ref, dh_ref, dw_ref):
        tv = t_ref[...]
        val, vjp = jax.vjp(lambda hv, wv: fn(hv, wv, tv), h_ref[...], w_ref[...])
        dh, dw = vjp(jnp.ones((1, 1), F32))
        dh_ref[...] = dh
        first = pl.program_id(0) == 0

        @pl.when(first)
        def _():
            loss_ref[...] = jnp.broadcast_to(val, loss_ref.shape)
            dw_ref[...] = dw

        @pl.when(jnp.logical_not(first))
        def _():
            loss_ref[...] += jnp.broadcast_to(val, loss_ref.shape)
            dw_ref[...] += dw

    return pl.pallas_call(
        body, grid=(rows // tr,), name="loss_head",
        in_specs=[_col_spec(tr, d, 0), _col_spec(tr, d, 0), _whole(w)],
        out_specs=[pl.BlockSpec((8, LANES), lambda i: (0, 0)), _col_spec(tr, d, 0), _whole(w)],
        out_shape=[SDS((8, LANES), F32), SDS((rows, d), F32), SDS(w.shape, F32)],
        compiler_params=pltpu.CompilerParams(dimension_semantics=("arbitrary",), vmem_limit_bytes=VMEM_LIMIT),
    )(h, target, w)


def _adamw_vals(w, g, m, v):
    m = ADAM_B1 * m + (1.0 - ADAM_B1) * g
    v = ADAM_B2 * v + (1.0 - ADAM_B2) * jnp.square(g)
    m_hat = m / (1.0 - ADAM_B1 ** ADAM_STEP)
    v_hat = v / (1.0 - ADAM_B2 ** ADAM_STEP)
    delta = -ADAM_LR * (m_hat / (jnp.sqrt(v_hat) + ADAM_EPS) + ADAM_WD * w)
    return delta, m, v


def _sum_adamw(name, parts, w, m, v):
    r, c = w.shape
    n_parts = parts.shape[0]
    budget = 6 * 1024 * 1024
    tr, tc = r, c
    for cand in (512, 256, 128, 64, 32, 16, 8):
        if r % cand == 0 and n_parts * cand * c * 4 <= budget:
            tr = cand
            break
    if n_parts * tr * c * 4 > budget:
        tc = max(t for t in range(LANES, c + 1, LANES) if c % t == 0 and n_parts * r * t * 4 <= budget)

    def body(p_ref, w_ref, m_ref, v_ref, g_ref, d_ref, m2_ref, v2_ref):
        g = p_ref[0].astype(F32)
        for s in range(1, n_parts):
            g = g + p_ref[s].astype(F32)
        g_ref[...] = g
        d_ref[...], m2_ref[...], v2_ref[...] = _adamw_vals(w_ref[...], g, m_ref[...], v_ref[...])

    blk = pl.BlockSpec((tr, tc), lambda i: (i, 0)) if tc == c else pl.BlockSpec((tr, tc), lambda i: (0, i))
    parts_blk = (pl.BlockSpec((n_parts, tr, tc), lambda i: (0, i, 0)) if tc == c
                 else pl.BlockSpec((n_parts, tr, tc), lambda i: (0, 0, i)))
    return pl.pallas_call(
        body, grid=(r // tr if tc == c else c // tc,), name=name,
        in_specs=[parts_blk, blk, blk, blk],
        out_specs=[blk] * 4, out_shape=[SDS((r, c), F32)] * 4,
        compiler_params=pltpu.CompilerParams(dimension_semantics=("arbitrary",), vmem_limit_bytes=VMEM_LIMIT),
    )(parts, w, m, v)


def _peers():
    x, y, c = lax.axis_index("x"), lax.axis_index("y"), lax.axis_index("c")
    peers = []
    for k in range(1, N_DEV):
        px = 1 - x if k & 4 else x
        py = 1 - y if k & 2 else y
        pc = 1 - c if k & 1 else c
        peers.append(((px, py, pc), 4 * px + 2 * py + pc))
    return 4 * x + 2 * y + c, peers


def _slot(ref, idx, cols):
    if cols is None:
        return ref.at[idx]
    return ref.at[:, pl.ds(pl.multiple_of(idx * cols, LANES), cols)]


def _gather_two_level(name, srcs, dsts):
    n = len(srcs)
    dst_cols = [c for _, _, c in dsts]

    def body(*refs):
        src_refs, out_refs = refs[:n], refs[n:2 * n]
        send_sems, recv_sems, local_sems = refs[2 * n:]
        x, y, c = lax.axis_index("x"), lax.axis_index("y"), lax.axis_index("c")
        index = lambda px, py, pc: 4 * px + 2 * py + pc
        me, sibling = index(x, y, c), (x, y, 1 - c)
        chips = [(x, 1 - y), (1 - x, y), (1 - x, 1 - y)]

        def copy(a, k, src, block, to):
            return pltpu.make_async_remote_copy(
                src_ref=src, dst_ref=_slot(out_refs[a], block, dst_cols[a]),
                send_sem=send_sems.at[a, k], recv_sem=recv_sems.at[a, k],
                device_id=to, device_id_type=pl.DeviceIdType.MESH)

        local, first, passed = [], [], []
        for a in range(n):
            cp = pltpu.make_async_copy(src_refs[a], _slot(out_refs[a], me, dst_cols[a]), local_sems.at[a])
            cp.start()
            local.append(cp)
            first.append(copy(a, 0, src_refs[a], me, sibling))
            first += [copy(a, 1 + j, src_refs[a], me, (*chip, c)) for j, chip in enumerate(chips)]
        for cp in first:
            cp.start()
        for a in range(n):
            for j, chip in enumerate(chips):
                block = index(*chip, c)
                arrived = _slot(out_refs[a], block, dst_cols[a])
                copy(a, 1 + j, arrived, block, (*chip, c)).wait_recv()
                passed.append(copy(a, 4 + j, arrived, block, sibling))
                passed[-1].start()
        for a in range(n):
            copy(a, 0, src_refs[a], index(x, y, 1 - c), sibling).wait_recv()
            for j, chip in enumerate(chips):
                block = index(*chip, 1 - c)
                copy(a, 4 + j, src_refs[a], block, sibling).wait_recv()
        for cp in first + passed:
            cp.wait_send()
        for cp in local:
            cp.wait()

    any_spec = pl.BlockSpec(memory_space=pl.ANY)
    return pl.pallas_call(
        body, name=name,
        in_specs=[any_spec] * n, out_specs=[any_spec] * n,
        out_shape=[SDS(shape, dt) for shape, dt, _ in dsts],
        scratch_shapes=_exchange_sems(n),
    )(*[a for a, _ in srcs])


_HBM = pl.BlockSpec(memory_space=pltpu.HBM)
_SEM = pl.BlockSpec(memory_space=pltpu.SEMAPHORE)
_EFFECT = pltpu.SideEffectType.DATAFLOW_SIDE_EFFECTING


def _split_copies(src_cols, dst_cols, gather, chips, src_refs, land_refs, send_sems, recv_sems, landings):
    me, peers = _peers()
    if chips:
        me, peers = me // 2, [(pos, idx // 2) for k, (pos, idx) in enumerate(peers) if (k + 1) in (2, 4, 6)]
    n, width = len(src_cols), len(peers)
    remote, local = [], []
    for a, (s_cols, d_cols) in enumerate(zip(src_cols, dst_cols)):
        mine = src_refs[a] if gather else _slot(src_refs[a], me, s_cols)
        local.append(pltpu.make_async_copy(mine, _slot(land_refs[a], me, d_cols), send_sems.at[n * width + a]))
        for k, (pos, idx) in enumerate(peers):
            blk = src_refs[a] if gather else _slot(src_refs[a], idx, s_cols)
            remote.append(pltpu.make_async_remote_copy(
                src_ref=blk, dst_ref=_slot(land_refs[a], idx if landings else me, d_cols),
                send_sem=send_sems.at[a * width + k], recv_sem=recv_sems.at[a * width + k],
                device_id=pos, device_id_type=pl.DeviceIdType.MESH))
    return remote, local


def _exchange_start(name, srcs, dsts, gather, after, chips=False):
    n = len(srcs)
    src_cols, dst_cols = [c for _, c in srcs], [c for _, _, c in dsts]
    width = 3 if chips else N_DEV - 1

    def body(*refs):
        src_refs, land_refs = refs[:n], refs[n:2 * n]
        send_sems, recv_sems = refs[2 * n + 1:2 * n + 3]
        token = refs[-1]
        remote, local = _split_copies(src_cols, dst_cols, gather, chips, src_refs, land_refs, send_sems, recv_sems,
                                      False)
        for cp in remote + local:
            cp.start()
        token[...] = jnp.zeros_like(token)

    hbm = lambda a: pltpu.with_memory_space_constraint(a, pltpu.HBM)
    lands = [hbm(lax.empty(shape, dt)) for shape, dt, _ in dsts]
    res = pl.pallas_call(
        body, name=name,
        out_shape=(pltpu.SemaphoreType.DMA((n * (width + 1),)), pltpu.SemaphoreType.DMA((n * width,)),
                   *[pltpu.HBM(a.shape, a.dtype) for a, _ in srcs], *[pltpu.HBM(a.shape, a.dtype) for a in lands],
                   SDS((8, LANES), F32)),
        in_specs=[_HBM] * (2 * n) + [pl.BlockSpec(memory_space=pl.ANY)],
        out_specs=(_SEM, _SEM, *[_HBM] * (2 * n), pl.BlockSpec(memory_space=pltpu.VMEM)),
        input_output_aliases={i: 2 + i for i in range(2 * n)},
        compiler_params=pltpu.CompilerParams(has_side_effects=_EFFECT),
    )(*[hbm(a) for a, _ in srcs], *lands, after)
    handle = (res[0], res[1], res[2:2 + n], res[2 + n:2 + 2 * n], src_cols, dst_cols, gather, chips)
    return handle, res[-1]


def _exchange_wait(name, handle, after):
    send_sems, recv_sems, src_thru, land_thru, src_cols, dst_cols, gather, chips = handle
    n = len(src_thru)

    def body(*refs):
        src_refs, land_refs = refs[:n], refs[n:2 * n]
        s_sems, r_sems = refs[2 * n:2 * n + 2]
        remote, local = _split_copies(src_cols, dst_cols, gather, chips, src_refs, land_refs, s_sems, r_sems, True)
        for cp in remote:
            cp.wait_send()
            cp.wait_recv()
        for cp in local:
            cp.wait()

    res = pl.pallas_call(
        body, name=name,
        out_shape=tuple(pltpu.HBM(a.shape, a.dtype) for a in (*src_thru, *land_thru)),
        in_specs=[_HBM] * (2 * n) + [_SEM, _SEM, pl.BlockSpec(memory_space=pl.ANY)],
        out_specs=tuple([_HBM] * (2 * n)),
        input_output_aliases={i: i for i in range(2 * n)},
        compiler_params=pltpu.CompilerParams(has_side_effects=_EFFECT),
    )(*src_thru, *land_thru, send_sems, recv_sems, after)
    return res[n:]


def _pair_swap(name, arrs):
    n = len(arrs)

    def body(*refs):
        src_refs, out_refs = refs[:n], refs[n:2 * n]
        send_sems, recv_sems = refs[2 * n:]
        x, y, c = lax.axis_index("x"), lax.axis_index("y"), lax.axis_index("c")
        copies = [pltpu.make_async_remote_copy(
            src_ref=src_refs[a].at[:, 1 - c], dst_ref=out_refs[a], send_sem=send_sems.at[a], recv_sem=recv_sems.at[a],
            device_id=(x, y, 1 - c), device_id_type=pl.DeviceIdType.MESH) for a in range(n)]
        for cp in copies:
            cp.start()
        for cp in copies:
            cp.wait()

    any_spec = pl.BlockSpec(memory_space=pl.ANY)
    return pl.pallas_call(
        body, name=name,
        in_specs=[any_spec] * n, out_specs=[any_spec] * n,
        out_shape=[SDS((a.shape[0],) + a.shape[2:], a.dtype) for a in arrs],
        scratch_shapes=[pltpu.SemaphoreType.DMA((n,)), pltpu.SemaphoreType.DMA((n,))],
    )(*arrs)


def _pair_add(name, mine, theirs):
    four, _, r, c = mine.shape
    tr = r
    for cand in (512, 256, 128, 64, 32, 16, 8):
        if r % cand == 0:
            tr = cand
            break
    tc = max(t for t in range(LANES, c + 1, LANES) if c % t == 0 and (t == LANES or 2 * tr * t * 4 <= 4 * 1024 * 1024))

    def body(m_ref, t_ref, o_ref):
        core = lax.axis_index("c")
        both = m_ref[...].astype(F32)
        own = jnp.where(core == 0, both[0], both[1])
        o_ref[...] = (own + t_ref[...].astype(F32)).astype(o_ref.dtype)

    return pl.pallas_call(
        body, grid=(four, r // tr, c // tc), name=name,
        in_specs=[pl.BlockSpec((None, 2, tr, tc), lambda i, j, k: (i, 0, j, k)),
                  pl.BlockSpec((None, tr, tc), lambda i, j, k: (i, j, k))],
        out_specs=pl.BlockSpec((None, tr, tc), lambda i, j, k: (i, j, k)),
        out_shape=SDS(theirs.shape, theirs.dtype),
        compiler_params=pltpu.CompilerParams(dimension_semantics=("arbitrary",) * 3, vmem_limit_bytes=VMEM_LIMIT),
    )(mine, theirs)


def _my_index():
    return 4 * lax.axis_index("x") + 2 * lax.axis_index("y") + lax.axis_index("c")


def _two_level_copies(stage, dst_cols, src_refs, land_refs, send_sems, recv_sems, landings):
    x, y, c = lax.axis_index("x"), lax.axis_index("y"), lax.axis_index("c")

    def pos(k):
        return (1 - x if k & 4 else x, 1 - y if k & 2 else y, 1 - c if k & 1 else c)

    def idx(k):
        px, py, pc = pos(k)
        return 4 * px + 2 * py + pc

    out = []
    for a, cols in enumerate(dst_cols):
        if stage == 1:
            for i, k in enumerate((1, 2, 4, 6)):
                out.append(pltpu.make_async_remote_copy(
                    src_ref=src_refs[a], dst_ref=_slot(land_refs[a], idx(k) if landings else idx(0), cols),
                    send_sem=send_sems.at[4 * a + i], recv_sem=recv_sems.at[4 * a + i],
                    device_id=pos(k), device_id_type=pl.DeviceIdType.MESH))
        else:
            for i, k in enumerate((2, 4, 6)):
                out.append(pltpu.make_async_remote_copy(
                    src_ref=_slot(land_refs[a], idx(k), cols),
                    dst_ref=_slot(land_refs[a], idx(k ^ 1) if landings else idx(k), cols),
                    send_sem=send_sems.at[3 * a + i], recv_sem=recv_sems.at[3 * a + i],
                    device_id=pos(1), device_id_type=pl.DeviceIdType.MESH))
    return out


def _gather2_start(name, srcs, dsts, after):
    n = len(srcs)
    dst_cols = [c for _, _, c in dsts]

    def body(*refs):
        src_refs, land_refs = refs[:n], refs[n:2 * n]
        send_sems, recv_sems = refs[2 * n + 1:2 * n + 3]
        me = _my_index()
        for a in range(n):
            pltpu.make_async_copy(src_refs[a], _slot(land_refs[a], me, dst_cols[a]), send_sems.at[4 * n + a]).start()
        for cp in _two_level_copies(1, dst_cols, src_refs, land_refs, send_sems, recv_sems, False):
            cp.start()
        refs[-1][...] = jnp.zeros_like(refs[-1])

    hbm = lambda a: pltpu.with_memory_space_constraint(a, pltpu.HBM)
    lands = [hbm(lax.empty(shape, dt)) for shape, dt, _ in dsts]
    res = pl.pallas_call(
        body, name=name,
        out_shape=(pltpu.SemaphoreType.DMA((5 * n,)), pltpu.SemaphoreType.DMA((4 * n,)),
                   *[pltpu.HBM(a.shape, a.dtype) for a, _ in srcs], *[pltpu.HBM(a.shape, a.dtype) for a in lands],
                   SDS((8, LANES), F32)),
        in_specs=[_HBM] * (2 * n) + [pl.BlockSpec(memory_space=pl.ANY)],
        out_specs=(_SEM, _SEM, *[_HBM] * (2 * n), pl.BlockSpec(memory_space=pltpu.VMEM)),
        input_output_aliases={i: 2 + i for i in range(2 * n)},
        compiler_params=pltpu.CompilerParams(has_side_effects=_EFFECT),
    )(*[hbm(a) for a, _ in srcs], *lands, after)
    return (res[0], res[1], res[2:2 + n], res[2 + n:2 + 2 * n], dst_cols), res[-1]


def _gather2_pass(name, handle, after):
    send1, recv1, src_thru, land_thru, dst_cols = handle
    n = len(src_thru)

    def body(*refs):
        src_refs, land_refs = refs[:n], refs[n:2 * n]
        s1, r1 = refs[2 * n:2 * n + 2]
        send2, recv2 = refs[2 * n + 3:2 * n + 5]
        me = _my_index()
        for cp in _two_level_copies(1, dst_cols, src_refs, land_refs, s1, r1, True):
            cp.wait_send()
            cp.wait_recv()
        for a in range(n):
            pltpu.make_async_copy(src_refs[a], _slot(land_refs[a], me, dst_cols[a]), s1.at[4 * n + a]).wait()
        for cp in _two_level_copies(2, dst_cols, src_refs, land_refs, send2, recv2, False):
            cp.start()
        refs[-1][...] = jnp.zeros_like(refs[-1])

    res = pl.pallas_call(
        body, name=name,
        out_shape=(pltpu.SemaphoreType.DMA((3 * n,)), pltpu.SemaphoreType.DMA((3 * n,)),
                   *[pltpu.HBM(a.shape, a.dtype) for a in (*src_thru, *land_thru)], SDS((8, LANES), F32)),
        in_specs=[_HBM] * (2 * n) + [_SEM, _SEM, pl.BlockSpec(memory_space=pl.ANY)],
        out_specs=(_SEM, _SEM, *[_HBM] * (2 * n), pl.BlockSpec(memory_space=pltpu.VMEM)),
        input_output_aliases={i: 2 + i for i in range(2 * n)},
        compiler_params=pltpu.CompilerParams(has_side_effects=_EFFECT),
    )(*src_thru, *land_thru, send1, recv1, after)
    return (res[0], res[1], res[2:2 + n], res[2 + n:2 + 2 * n], dst_cols), res[-1]


def _gather2_wait(name, handle, after):
    send2, recv2, src_thru, land_thru, dst_cols = handle
    n = len(src_thru)

    def body(*refs):
        src_refs, land_refs = refs[:n], refs[n:2 * n]
        s2, r2 = refs[2 * n:2 * n + 2]
        for cp in _two_level_copies(2, dst_cols, src_refs, land_refs, s2, r2, True):
            cp.wait_send()
            cp.wait_recv()

    res = pl.pallas_call(
        body, name=name,
        out_shape=tuple(pltpu.HBM(a.shape, a.dtype) for a in (*src_thru, *land_thru)),
        in_specs=[_HBM] * (2 * n) + [_SEM, _SEM, pl.BlockSpec(memory_space=pl.ANY)],
        out_specs=tuple([_HBM] * (2 * n)),
        input_output_aliases={i: i for i in range(2 * n)},
        compiler_params=pltpu.CompilerParams(has_side_effects=_EFFECT),
    )(*src_thru, *land_thru, send2, recv2, after)
    return res[n:]


def _exchange_sems(n):
    return [pltpu.SemaphoreType.DMA((n, N_DEV - 1)), pltpu.SemaphoreType.DMA((n, N_DEV - 1)),
            pltpu.SemaphoreType.DMA((n,))]


def _rms_res_fn(h, w):
    return _rms_fn(h, w)[0], h


def _add_epilogue(acc, res):
    return (acc + res,)


def _gather_plan(shards):
    srcs, dsts = [], []
    for n, sh in shards.items():
        r, c = sh.shape
        srcs.append((sh, None))
        if SHARDED[n] and c % LANES == 0:
            dsts.append(((r, N_DEV * c), sh.dtype, c))
        else:
            dsts.append(((N_DEV, r, c), sh.dtype, None))
    return srcs, dsts, True


def _w_in_segments():
    out = []
    for j in range(N_DEV):
        lo, hi = W_IN_SHARD * j, W_IN_SHARD * (j + 1)
        for a, b in ((lo, min(hi, DN_COLS)), (max(lo, DN_COLS), hi)):
            if a < b:
                out.append((j, a - lo, b - lo, a if a < DN_COLS else a + RW_OFF - DN_COLS))
    return out


def _w_in_to_padded(shards, tc=512):
    _, _, cols = shards.shape

    def body(g_ref, o_ref):
        o_ref[...] = jnp.zeros_like(o_ref)
        for j, a, b, dst in _w_in_segments():
            o_ref[dst:dst + b - a, :] = g_ref[j, a:b, :]

    return pl.pallas_call(
        body, grid=(cols // tc,), name="w_in_to_padded",
        in_specs=[pl.BlockSpec((N_DEV, W_IN_SHARD, tc), lambda i: (0, 0, i))],
        out_specs=pl.BlockSpec((IN_PAD, tc), lambda i: (0, i)),
        out_shape=SDS((IN_PAD, cols), shards.dtype),
        compiler_params=pltpu.CompilerParams(dimension_semantics=("arbitrary",), vmem_limit_bytes=VMEM_LIMIT),
    )(shards)


def _w_in_grad_to_shards(gw, tc=512):
    _, cols = gw.shape

    def body(w_ref, o_ref):
        for j, a, b, dst in _w_in_segments():
            o_ref[j, a:b, :] = w_ref[dst:dst + b - a, :]

    return pl.pallas_call(
        body, grid=(cols // tc,), name="w_in_grad_to_shards",
        in_specs=[pl.BlockSpec((IN_PAD, tc), lambda i: (0, i))],
        out_specs=pl.BlockSpec((N_DEV, W_IN_SHARD, tc), lambda i: (0, 0, i)),
        out_shape=SDS((N_DEV, W_IN_SHARD, cols), gw.dtype),
        compiler_params=pltpu.CompilerParams(dimension_semantics=("arbitrary",), vmem_limit_bytes=VMEM_LIMIT),
    )(gw)


def _gather_finish(names, outs):
    full = {}
    for n, arr in zip(names, outs):
        if n == "w_in":
            full[n] = _w_in_to_padded(arr)
        elif arr.ndim == 2:
            full[n] = arr
        elif SHARDED[n]:
            full[n] = arr.transpose(1, 0, 2).reshape(arr.shape[1], -1)
        else:
            full[n] = arr.reshape(-1, arr.shape[2])
    return full


def _scatter_plan(grads):
    srcs, dsts = [], []
    for n, gr in grads.items():
        if gr.ndim == 3:
            srcs.append((gr, None))
            dsts.append((gr.shape, gr.dtype, None))
            continue
        rows, cols = gr.shape
        if not SHARDED[n]:
            r, c = rows // N_DEV, cols
            srcs.append((gr.reshape(N_DEV, r, c), None))
        else:
            r, c = rows, cols // N_DEV
            if c % LANES == 0:
                srcs.append((gr, c))
            else:
                srcs.append((gr.reshape(r, N_DEV, c).transpose(1, 0, 2), None))
        dsts.append(((N_DEV, r, c), gr.dtype, None))
    return srcs, dsts, False


def _local_step(x, mem, target, wt, late):
    d = D_MODEL
    g = {}
    wt = dict(wt)
    grp_a = ("w_out", "xa_wq", "xa_wk", "xa_wv", "xa_wo")
    grp_b = ("ffn_w1", "ffn_w2")
    plan = lambda names: _gather_plan({n: late[n] for n in names})[:2]
    handle_a, tok_a = _gather2_start("late_gather_a_start", *plan(grp_a), wt["w_in"])
    handle_w1, tok_b = _gather2_start("late_gather_w1_start", *plan(("ffn_w1",)), tok_a)
    handle_w2, tok_c = _gather2_start("late_gather_w2_start", *plan(("ffn_w2",)), tok_b)
    mix_w = wt["mix_norm_w"] + (tok_a[0:1, 0:1] + tok_b[0:1, 0:1] + tok_c[0:1, 0:1])
    u = _row_fwd(_rms_fn, "mix_norm", [(x, d, 0)], [mix_w], [(d, BF16)], 256)[0]
    p = _matmul("in_proj", u, wt["w_in"], "nt", [F32], tn=1536)[0]
    c = _col_fwd(_conv_fn, "dn_conv", p, 0, 24, [wt["dn_conv_w"]])
    handle_a, tok = _gather2_pass("late_gather_a_pass", handle_a, c)
    dn_pre_tiles = [(c, DN_WIDTH, 0), (c, DN_WIDTH, 1), (p, LANES, 32)]
    dn_pre_params = [wt["dn_a_log"], wt["dn_dt_bias"]]
    qh, kh, gb, bb, gcb = _row_fwd(_dn_pre_fn, "dn_pre", dn_pre_tiles, [dn_pre_params[0] + tok[0:1, :], dn_pre_params[1]],
                                   [(DN_WIDTH, F32)] * 5, CHUNK)
    dn_arrs = [(qh, 0), (kh, 0), (c, 16), (gb, 0), (bb, 0), (gcb, 0)]
    o, kept_dn = _scan_fwd(_gdn_group, "gdn_scan", dn_arrs, DN_HEADS, 1)
    dn_post_tiles = [(o, DN_WIDTH, 0), (p, DN_WIDTH, 3)]
    o_dn = _row_fwd(_dn_post_fn, "dn_post", dn_post_tiles, [wt["dn_norm_w"]], [(DN_WIDTH, BF16)], 256)[0]

    ps = _col_fwd(_lerp_fn, "rw_shift", p, RW_OFF // LANES, 26, [wt["rw_mu"]])
    rw_pre_tiles = [(ps, RW_WIDTH, 0), (ps, RW_WIDTH, 1), (ps, RW_WIDTH, 2), (ps, LANES, 24), (ps, LANES, 25)]
    rw_pre_params = [wt[n] for n in ("rw_w0", "rw_a0", "rw_k_k", "rw_k_a", "rw_w2", "rw_a2", "rw_g2")]
    r, lw, k, v, al, be, gate, gcw = _row_fwd(_rw_pre_fn, "rw_pre", rw_pre_tiles, rw_pre_params,
                                              [(RW_WIDTH, F32)] * 8, CHUNK)
    rw_arrs = [(r, 0), (lw, 0), (k, 0), (v, 0), (al, 0), (be, 0), (gcw, 0)]
    y, kept_rw = _scan_fwd(_rw_group, "rw_scan", rw_arrs, RW_WIDTH // LANES, 2)
    handle_w1, tok = _gather2_pass("late_gather_w1_pass", handle_w1, y)
    rw_post_tiles = [(t, RW_WIDTH, 0) for t in (y, r, k, v, gate)]
    rw_post_params = [wt["rw_ln_w"], wt["rw_ln_b"], wt["rw_r_k"]]
    o_rw = _row_fwd(_rw_post_fn, "rw_post", rw_post_tiles, [rw_post_params[0] + tok[0:1, 0:1]] + rw_post_params[1:],
                    [(RW_WIDTH, BF16)], 128)[0]
    o_cat = jnp.concatenate([o_dn, o_rw], axis=1)
    wt.update(_gather_finish(grp_a, _gather2_wait("late_gather_a_wait", handle_a, o_cat)))
    h1 = _matmul("out_proj", o_cat, wt["w_out"], "nn", [F32], _add_epilogue, (x,))[0]

    handle_w2, tok = _gather2_pass("late_gather_w2_pass", handle_w2, h1)
    hn = _row_fwd(_rms_fn, "xa_norm", [(h1, d, 0)], [wt["xa_norm_w"] + tok[0:1, 0:1]], [(d, BF16)], 256)[0]
    mn = _row_fwd(_rms_fn, "mem_norm", [(mem, d, 0)], [wt["mem_norm_w"]], [(d, BF16)], 256)[0]
    q = _matmul("xa_q", hn, wt["xa_wq"], "nn", [F32])[0]
    kx = _matmul("xa_k", mn, wt["xa_wk"], "nn", [F32])[0]
    vx = _matmul("xa_v", mn, wt["xa_wv"], "nn", [F32])[0]
    ao = _row_fwd(_xattn_fn, "xattn", [(q, XA_WIDTH, 0)], [kx, vx], [(XA_WIDTH, BF16)], 256)[0]
    h2 = _matmul("xa_o", ao, wt["xa_wo"], "nn", [F32], _add_epilogue, (h1,))[0]

    f = _row_fwd(_rms_fn, "ffn_norm", [(h2, d, 0)], [wt["ffn_norm_w"]], [(d, BF16)], 256)[0]
    wt.update(_gather_finish(("ffn_w1",), _gather2_wait("late_gather_w1_wait", handle_w1, f)))
    a, hid = _matmul("ffn_up", f, wt["ffn_w1"], "nn", [F32, BF16],
                     lambda acc: (acc, jnp.square(jnp.maximum(acc, 0.0))))
    wt.update(_gather_finish(("ffn_w2",), _gather2_wait("late_gather_w2_wait", handle_w2, hid)))
    h3 = _matmul("ffn_down", hid, wt["ffn_w2"], "nn", [F32], _add_epilogue, (h2,))[0]
    loss8, dh3, g["final_norm_w"] = _loss_call(h3, target, wt["final_norm_w"])

    da = _matmul("ffn_down_dx", dh3, wt["ffn_w2"], "nt", [BF16],
                 lambda acc, av: (acc * 2.0 * jnp.maximum(av, 0.0),), (a,))[0]
    g["ffn_w2"] = _matmul("ffn_down_dw", hid, dh3, "tn", [BF16])[0]
    g["ffn_w1"] = _matmul("ffn_up_dw", f, da, "tn", [BF16])[0]
    pending = {}
    plan = _scatter_plan({n: g.pop(n) for n in grp_b})
    pending[grp_b], tok = _exchange_start("late_grad_b_start", *plan, loss8)
    dh2, g["ffn_norm_w"] = _matmul_norm_bwd("ffn_up_dx", da, wt["ffn_w1"], "nt", h2, wt["ffn_norm_w"], dh3, tok)

    dao = _matmul("xa_o_dx", dh2, wt["xa_wo"], "nt", [F32])[0]
    g["xa_wo"] = _matmul("xa_o_dw", ao, dh2, "tn", [BF16])[0]
    (dq,), (dkx, dvx) = _row_bwd(_xattn_fn, "xattn_bwd", [(q, XA_WIDTH, 0)], [kx, vx], [[(dao, XA_WIDTH, 0)]], 256)
    dh1, g["xa_norm_w"] = _matmul_norm_bwd("xa_q_dx", dq, wt["xa_wq"], "nt", h1, wt["xa_norm_w"], dh2)
    g["xa_wq"] = _matmul("xa_q_dw", hn, dq, "tn", [BF16])[0]
    g["xa_wk"] = _matmul("xa_k_dw", mn, dkx, "tn", [BF16])[0]
    g["xa_wv"] = _matmul("xa_v_dw", mn, dvx, "tn", [BF16])[0]
    dmn = _matmul("xa_k_dx", dkx, wt["xa_wk"], "nt", [F32])[0]
    dmn = _matmul("xa_v_dx", dvx, wt["xa_wv"], "nt", [F32], _add_epilogue, (dmn,))[0]
    _, (g["mem_norm_w"],) = _row_bwd(_rms_fn, "mem_norm_bwd", [(mem, d, 0)], [wt["mem_norm_w"]],
                                     [[(dmn, d, 0)]], 256, want_tiles=())

    do_cat = _matmul("out_proj_dx", dh1, wt["w_out"], "nt", [F32])[0]
    g["w_out"] = _matmul("out_proj_dw", o_cat, dh1, "tn", [BF16])[0]

    plan = _scatter_plan({n: g.pop(n) for n in grp_a})
    pending[grp_a], tok = _exchange_start("late_grad_a_start", *plan, tok)
    (dy, dr1, dk1, dv1, dgate), (g["rw_ln_w"], g["rw_ln_b"], g["rw_r_k"]) = _row_bwd(
        _rw_post_fn, "rw_post_bwd", rw_post_tiles, [rw_post_params[0] + tok[0:1, 0:1]] + rw_post_params[1:],
        [[(do_cat, RW_WIDTH, 1)]], 128)
    dr2, dlw, dk2, dv2, dal, dbe, dgcw = _scan_bwd(_rw_group, "rw_scan_bwd", rw_arrs, kept_rw, dy,
                                                   RW_WIDTH // LANES)
    one = lambda t: [(t, RW_WIDTH, 0)]
    two = lambda s, t: [(s, RW_WIDTH, 0), (t, RW_WIDTH, 0)]
    d_ps, rw_pre_grads = _row_bwd(
        _rw_pre_fn, "rw_pre_bwd", rw_pre_tiles, rw_pre_params,
        [two(dr1, dr2), one(dlw), two(dk1, dk2), two(dv1, dv2), one(dal), one(dbe), one(dgate), one(dgcw)],
        CHUNK)
    for n, val in zip(("rw_w0", "rw_a0", "rw_k_k", "rw_k_a", "rw_w2", "rw_a2", "rw_g2"), rw_pre_grads):
        g[n] = val
    dp_rw, (g["rw_mu"],) = _col_bwd(_lerp_fn, "rw_shift_bwd", p, RW_OFF // LANES, 26, [wt["rw_mu"]],
                                    jnp.concatenate(d_ps, axis=1))

    (do, dz), (g["dn_norm_w"],) = _row_bwd(_dn_post_fn, "dn_post_bwd", dn_post_tiles, [wt["dn_norm_w"]],
                                           [[(do_cat, DN_WIDTH, 0)]], 256)
    dqh, dkh, dv_dn, dgb, dbb, dgcb = _scan_bwd(_gdn_group, "gdn_scan_bwd", dn_arrs, kept_dn, do, DN_HEADS)
    one = lambda t: [(t, DN_WIDTH, 0)]
    (dcq, dck, dgates), (g["dn_a_log"], g["dn_dt_bias"]) = _row_bwd(
        _dn_pre_fn, "dn_pre_bwd", dn_pre_tiles, dn_pre_params,
        [one(dqh), one(dkh), one(dgb), one(dbb), one(dgcb)], CHUNK)
    dp_qkv, (g["dn_conv_w"],) = _col_bwd(_conv_fn, "dn_conv_bwd", p, 0, 24, [wt["dn_conv_w"]],
                                         jnp.concatenate([dcq, dck, dv_dn], axis=1))
    dp = jnp.concatenate([dp_qkv, dz, dgates, dp_rw, jnp.zeros((x.shape[0], LANES), F32)], axis=1).astype(BF16)
    g["w_in"] = _matmul("in_proj_dw", dp, u, "tn", [BF16], tm=1536)[0]
    early = _logical_grads(g)
    blocks = []
    for src, cols in _scatter_plan({n: early.pop(n) for n in EARLY})[0]:
        if cols is not None:
            src = src.reshape(src.shape[0], N_DEV, cols).transpose(1, 0, 2)
        blocks.append(src.reshape((4, 2) + src.shape[1:]))
    sums = [_pair_add("early_grad_pair_add_%d" % i, mine, theirs)
            for i, (mine, theirs) in enumerate(zip(blocks, _pair_swap("early_grad_pair_swap", blocks)))]
    pending[EARLY], tok = _exchange_start("early_grad_start", [(t, None) for t in sums],
                                          [(t.shape, t.dtype, None) for t in sums], False, tok, chips=True)
    dx, early["mix_norm_w"] = _matmul_norm_bwd("in_proj_dx", dp, wt["w_in"], "nn", x, wt["mix_norm_w"], dh1, tok)
    return loss8, dx, early, pending, tok


WEIGHTS = ["mix_norm_w", "w_in", "dn_conv_w", "dn_a_log", "dn_dt_bias", "dn_norm_w", "rw_mu", "rw_w0", "rw_w2",
           "rw_a0", "rw_a2", "rw_g2", "rw_k_k", "rw_k_a", "rw_r_k", "rw_ln_w", "rw_ln_b", "w_out", "xa_norm_w",
           "mem_norm_w", "xa_wq", "xa_wk", "xa_wv", "xa_wo", "ffn_norm_w", "ffn_w1", "ffn_w2", "final_norm_w"]
SHARDED = {"w_in": False, "w_out": False, "xa_wq": False, "xa_wk": False, "xa_wv": False, "xa_wo": True,
           "ffn_w1": True, "ffn_w2": False, "dn_conv_w": True, "rw_w2": True, "rw_a2": True, "rw_g2": True}
BF16_PAYLOAD = ("w_in", "w_out", "xa_wq", "xa_wk", "xa_wv", "xa_wo", "ffn_w1", "ffn_w2")
REPLICATED = [n for n in WEIGHTS if n not in SHARDED]
EARLY = ("w_in", "dn_conv_w", "rw_w2", "rw_a2", "rw_g2")
RW_IN_COLS = IN_COLS - DN_COLS
W_IN_SHARD = IN_COLS // N_DEV


def _layout_weights(fw):
    wt = dict(fw)
    wt["dn_conv_w"] = jnp.pad(fw["dn_conv_w"], ((0, 4), (0, 0)))
    wt["dn_a_log"] = jnp.pad(fw["dn_a_log"], ((0, 0), (0, LANES - DN_HEADS)))
    wt["dn_dt_bias"] = jnp.pad(fw["dn_dt_bias"], ((0, 0), (0, LANES - DN_HEADS)))
    wt["rw_w2"] = jnp.pad(fw["rw_w2"], ((0, 64), (0, 0)))
    wt["rw_a2"] = jnp.pad(fw["rw_a2"], ((64, 0), (0, 0)))
    return wt


def _logical_grads(g):
    out = dict(g)
    out["w_in"] = _w_in_grad_to_shards(g["w_in"])
    out["dn_conv_w"] = g["dn_conv_w"][:4]
    out["dn_a_log"] = g["dn_a_log"][:, :DN_HEADS]
    out["dn_dt_bias"] = g["dn_dt_bias"][:, :DN_HEADS]
    out["rw_w2"] = g["rw_w2"][:64]
    out["rw_a2"] = g["rw_a2"][64:]
    return out


def _pack(vals):
    parts = []
    for v in vals:
        flat = v.reshape(-1)
        parts.append(jnp.pad(flat, (0, -flat.shape[0] % LANES)))
    flat = jnp.concatenate(parts)
    flat = jnp.pad(flat, (0, -flat.shape[0] % (8 * LANES)))
    return flat.reshape(-1, LANES)


def _unpack(packed, shapes):
    flat = packed.reshape(-1)
    out, at = [], 0
    for shp in shapes:
        size = math.prod(shp)
        out.append(flat[at:at + size].reshape(shp))
        at += size + (-size % LANES)
    return out


def kernel(x, mem, mix_norm_w, w_in, dn_conv_w, dn_a_log, dn_dt_bias, dn_norm_w, rw_mu, rw_w0, rw_w2, rw_a0, rw_a2, rw_g2, rw_k_k, rw_k_a, rw_r_k, rw_ln_w, rw_ln_b, w_out, xa_norm_w, mem_norm_w, xa_wq, xa_wk, xa_wv, xa_wo, ffn_norm_w, ffn_w1, ffn_w2, final_norm_w, loss_target, m_mix_norm_w, m_w_in, m_dn_conv_w, m_dn_a_log, m_dn_dt_bias, m_dn_norm_w, m_rw_mu, m_rw_w0, m_rw_w2, m_rw_a0, m_rw_a2, m_rw_g2, m_rw_k_k, m_rw_k_a, m_rw_r_k, m_rw_ln_w, m_rw_ln_b, m_w_out, m_xa_norm_w, m_mem_norm_w, m_xa_wq, m_xa_wk, m_xa_wv, m_xa_wo, m_ffn_norm_w, m_ffn_w1, m_ffn_w2, m_final_norm_w, v_mix_norm_w, v_w_in, v_dn_conv_w, v_dn_a_log, v_dn_dt_bias, v_dn_norm_w, v_rw_mu, v_rw_w0, v_rw_w2, v_rw_a0, v_rw_a2, v_rw_g2, v_rw_k_k, v_rw_k_a, v_rw_r_k, v_rw_ln_w, v_rw_ln_b, v_w_out, v_xa_norm_w, v_mem_norm_w, v_xa_wq, v_xa_wk, v_xa_wv, v_xa_wo, v_ffn_norm_w, v_ffn_w1, v_ffn_w2, v_final_norm_w):
    given = dict(locals())
    w = {n: given[n] for n in WEIGHTS}
    m = {n: given["m_" + n] for n in WEIGHTS}
    v = {n: given["v_" + n] for n in WEIGHTS}

    local = {n: (lambda t: t[0].T) if n == "w_in" else (lambda t: t[0]) for n in SHARDED}
    shards = {n: (local[n](w[n]).astype(BF16) if n in BF16_PAYLOAD else local[n](w[n])) for n in SHARDED}
    srcs, dsts, _ = _gather_plan({n: shards[n] for n in EARLY})
    full = _gather_finish(EARLY, _gather_two_level("early_all_gather", srcs, dsts))
    for n in REPLICATED:
        full[n] = w[n].reshape(1, -1)

    loss8, dx, g, pending, after = _local_step(x[0], mem[0], loss_target[0], _layout_weights(full),
                                               {n: shards[n] for n in SHARDED if n not in EARLY})
    loss = lax.psum(loss8[0, 0], ("x", "y", "c"))

    packed = _pack([g[n] for n in REPLICATED])
    small, _ = _exchange_start("small_gather_start", [(packed, None)], [((N_DEV,) + packed.shape, F32, None)], True,
                               after)
    grad, delta, new_m, new_v = {}, {}, {}, {}
    done = [dx]

    def tie():
        return jnp.broadcast_to(sum(t[:1, :1] for t in done), (8, LANES))

    for names in sorted(pending, key=lambda names: names == EARLY):
        handle = pending[names]
        for n, parts in zip(names, _exchange_wait("grad_wait_" + names[0], handle, tie())):
            res = _sum_adamw("adamw_" + n, parts, local[n](w[n]), local[n](m[n]), local[n](v[n]))
            grad[n], delta[n], new_m[n], new_v[n] = [(t.T if n == "w_in" else t)[None] for t in res]
            done.append(res[1])

    (parts,) = _exchange_wait("small_gather_wait", small, tie())
    res = _sum_adamw("adamw_small", parts, _pack([w[n] for n in REPLICATED]),
                     _pack([m[n] for n in REPLICATED]), _pack([v[n] for n in REPLICATED]))
    shapes = [w[n].shape for n in REPLICATED]
    for store, packed_out in zip((grad, delta, new_m, new_v), res):
        for n, val in zip(REPLICATED, _unpack(packed_out, shapes)):
            store[n] = val

    return (loss, dx[None], *[grad[n] for n in WEIGHTS], *[delta[n] for n in WEIGHTS],
            *[new_m[n] for n in WEIGHTS], *[new_v[n] for n in WEIGHTS])
```

```python
import functools
import math

import jax
import jax.numpy as jnp
from jax import lax
from jax.experimental import pallas as pl
from jax.experimental.pallas import tpu as pltpu

F32 = jnp.float32
BF16 = jnp.bfloat16
SDS = jax.ShapeDtypeStruct

N_DEV = 8
D_MODEL = 2048
LANES = 128
CHUNK = 128
DN_HEADS = 8
DN_WIDTH = 1024
RW_WIDTH = 1024
RW_HEAD = 64
XA_HEADS = 4
XA_WIDTH = 512
FFN_HIDDEN = 8192
IN_COLS = 7440
DN_COLS = 4112
IN_PAD = 7680
RW_OFF = 4224
RMS_EPS = 1e-6
RW_GN_EPS = 64e-5
VMEM_LIMIT = 56 * 1024 * 1024

ADAM_LR = 0.001
ADAM_B1 = 0.9
ADAM_B2 = 0.999
ADAM_EPS = 1e-08
ADAM_WD = 0.01
ADAM_STEP = 10

_DIMS = {"nn": (((1,), (0,)), ((), ())), "nt": (((1,), (1,)), ((), ())), "tn": (((0,), (0,)), ((), ()))}


def _raw_dot(a, b, mode, hi):
    if hi:
        return lax.dot_general(a, b, _DIMS[mode], precision=lax.Precision.HIGHEST, preferred_element_type=F32)
    return lax.dot_general(a.astype(BF16), b.astype(BF16), _DIMS[mode], preferred_element_type=F32)


@functools.partial(jax.custom_vjp, nondiff_argnums=(2, 3))
def mm(a, b, mode="nn", hi=False):
    return _raw_dot(a, b, mode, hi)


def _mm_fwd(a, b, mode, hi):
    return _raw_dot(a, b, mode, hi), (a, b)


def _mm_bwd(mode, hi, res, g):
    a, b = res
    if mode == "nn":
        return _raw_dot(g, b, "nt", hi), _raw_dot(a, g, "tn", hi)
    if mode == "nt":
        return _raw_dot(g, b, "nn", hi), _raw_dot(g, a, "tn", hi)
    return _raw_dot(b, g, "nt", hi), _raw_dot(a, g, "nn", hi)


mm.defvjp(_mm_fwd, _mm_bwd)


def _shift_rows_raw(x, k):
    n = x.shape[0]
    rolled = pltpu.roll(x, k % n, axis=0)
    row = lax.broadcasted_iota(jnp.int32, x.shape, 0)
    keep = row >= k if k > 0 else row < n + k
    return jnp.where(keep, rolled, 0.0)


@functools.partial(jax.custom_vjp, nondiff_argnums=(1,))
def shift_rows(x, k):
    return _shift_rows_raw(x, k)


shift_rows.defvjp(lambda x, k: (_shift_rows_raw(x, k), None), lambda k, _, g: (_shift_rows_raw(g, -k),))


def _softplus(x):
    return jnp.maximum(x, 0.0) + jnp.log(1.0 + jnp.exp(-jnp.abs(x)))


def _sigmoid(x):
    return 1.0 / (1.0 + jnp.exp(-x))


def _silu(x):
    return x * _sigmoid(x)


def _tri_masks(n):
    ii = lax.broadcasted_iota(jnp.int32, (n, n), 0)
    jj = lax.broadcasted_iota(jnp.int32, (n, n), 1)
    return ii >= jj, ii > jj, ii == jj


def _neumann_inv_raw(m):
    n = m.shape[0]
    _, _, eye = _tri_masks(n)
    eye = jnp.where(eye, 1.0, 0.0)
    p = eye + m
    mk = m
    for _ in range(int(math.log2(n)) - 1):
        mk = _raw_dot(mk, mk, "nn", False)
        p = p + _raw_dot(p, mk, "nn", False)
    resid = eye - p + _raw_dot(m, p, "nn", True)
    return p + _raw_dot(p, resid, "nn", False)


@jax.custom_vjp
def _neumann_inv(m):
    return _neumann_inv_raw(m)


def _neumann_inv_fwd(m):
    p = _neumann_inv_raw(m)
    return p, p


def _neumann_inv_bwd(p, g):
    return (_raw_dot(_raw_dot(p, g, "tn", False), p, "nt", False),)


_neumann_inv.defvjp(_neumann_inv_fwd, _neumann_inv_bwd)


@jax.custom_vjp
def _saved_inv(m, p):
    return p


_saved_inv.defvjp(lambda m, p: (p, p), lambda p, g: (_neumann_inv_bwd(p, g)[0], jnp.zeros_like(p)))


def _inverse(m, saved):
    return _neumann_inv(m) if saved is None else _saved_inv(m, saved)


def _cumsum_rows(x):
    causal, _, _ = _tri_masks(x.shape[0])
    return mm(jnp.where(causal, 1.0, 0.0), x, "nn", True)


def _gdn_group(s0, q, k, v, gb, bb, gc, *saved):
    diff = jnp.stack([gc[j] - gc[j].T for j in range(gc.shape[0])])
    return jax.vmap(_gdn_chunk)(s0, q, k, v, gb, bb, gc, diff, *saved)


def _rw_group(*args):
    return jax.vmap(_rw_chunk)(*args)


def _gdn_chunk(s0, q, k, v, gb, bb, gc, diff, saved=None):
    c = q.shape[0]
    causal, strict, _ = _tri_masks(c)
    decay = jnp.exp(jnp.where(causal, diff, -jnp.inf))
    kb = k * bb
    a = jnp.where(strict, mm(kb, k, "nt") * decay, 0.0)
    p = _inverse(-a, saved)
    u = mm(p, v * bb)
    w = mm(p, kb * jnp.exp(gc))
    attn = mm(q, k, "nt") * decay
    v_new = u - mm(w, s0)
    o = mm(q * jnp.exp(gc), s0) + mm(attn, v_new)
    g_last = jnp.sum(gb, axis=0, keepdims=True)
    s1 = s0 * jnp.exp(g_last) + mm(k * jnp.exp(g_last - gc), v_new, "tn")
    return o, s1, p


def _rw_chunk(s0, r, lw, k, v, al, be, gc, saved0=None, saved1=None):
    c = r.shape[0]
    causal, strict, _ = _tri_masks(c)
    gp = gc - lw
    row = lax.broadcasted_iota(jnp.int32, lw.shape, 0)
    lane = lax.broadcasted_iota(jnp.int32, lw.shape, 1)
    g_mid = jnp.sum(jnp.where(row < c // 2, lw, 0.0), axis=0, keepdims=True)
    g_last = jnp.sum(lw, axis=0, keepdims=True)
    e_n = jnp.exp(g_mid - gc)
    rg = r * jnp.exp(gc - g_mid)
    bg = be * jnp.exp(gp - g_mid)
    an = al * e_n
    kn = k * e_n
    bt = mm(be * jnp.exp(gp), s0, "nt")
    rt = mm(r * jnp.exp(gc), s0, "nt")
    us, ys, ps = [], [], []
    for h, saved in enumerate((saved0, saved1)):
        mine = (lane >= RW_HEAD) if h else (lane < RW_HEAD)
        bgh = jnp.where(mine, bg, 0.0)
        rgh = jnp.where(mine, rg, 0.0)
        a_ab = jnp.where(strict, mm(bgh, an, "nt"), 0.0)
        a_kb = jnp.where(strict, mm(bgh, kn, "nt"), 0.0)
        a_ra = jnp.where(causal, mm(rgh, an, "nt"), 0.0)
        a_rk = jnp.where(causal, mm(rgh, kn, "nt"), 0.0)
        p = _inverse(a_ab, saved)
        ps.append(p)
        u_h = mm(p, bt + mm(a_kb, v))
        us.append(u_h)
        ys.append(rt + mm(a_ra, u_h) + mm(a_rk, v))
    lo = lane < RW_HEAD
    u = jnp.where(lo, us[0], us[1])
    y = jnp.where(lo, ys[0], ys[1])
    tail = jnp.exp(g_last - gc)
    s1 = s0 * jnp.exp(g_last) + mm(u, al * tail, "tn") + mm(v, k * tail, "tn")
    vi = lax.broadcasted_iota(jnp.int32, s0.shape, 0)
    ki = lax.broadcasted_iota(jnp.int32, s0.shape, 1)
    s1 = jnp.where((vi < RW_HEAD) == (ki < RW_HEAD), s1, 0.0)
    return y, s1, ps[0], ps[1]


SCAN_HB = 8


def _scan_specs(arrs, n_chunks, reverse):
    def spec(off):
        assert off % SCAN_HB == 0
        if reverse:
            return pl.BlockSpec((CHUNK, SCAN_HB * LANES), lambda h, n: (n_chunks - 1 - n, off // SCAN_HB + h))
        return pl.BlockSpec((CHUNK, SCAN_HB * LANES), lambda h, n: (n, off // SCAN_HB + h))
    return [spec(off) for _, off in arrs]


def _split_heads(x):
    return jnp.stack([x[:, LANES * j:LANES * (j + 1)] for j in range(SCAN_HB)], axis=0)


def _merge_heads(x):
    return jnp.concatenate([x[j] for j in range(SCAN_HB)], axis=1)


def _scan_fwd(group_fn, name, arrs, heads, n_kept):
    s = arrs[0][0].shape[0]
    n_chunks = s // CHUNK
    n_in = len(arrs)

    def body(*refs):
        y_ref, st_ref = refs[n_in:n_in + 2]
        kept_refs, s_scr = refs[n_in + 2:-1], refs[-1]

        @pl.when(pl.program_id(1) == 0)
        def _():
            s_scr[...] = jnp.zeros_like(s_scr)

        s0 = s_scr[...]
        st_ref[...] = s0
        y, s1, *kept = group_fn(s0, *[_split_heads(r[...]) for r in refs[:n_in]])
        y_ref[...] = _merge_heads(y)
        s_scr[...] = s1
        for ref, val in zip(kept_refs, kept):
            ref[...] = val

    per_chunk = pl.BlockSpec((SCAN_HB, None, LANES, LANES), lambda h, n: (h, n, 0, 0))
    res = pl.pallas_call(
        body, grid=(heads // SCAN_HB, n_chunks), name=name,
        in_specs=_scan_specs(arrs, n_chunks, False),
        out_specs=[pl.BlockSpec((CHUNK, SCAN_HB * LANES), lambda h, n: (n, h))] + [per_chunk] * (1 + n_kept),
        out_shape=[SDS((s, heads * LANES), F32)] + [SDS((heads, n_chunks, LANES, LANES), F32)] * (1 + n_kept),
        scratch_shapes=[pltpu.VMEM((SCAN_HB, LANES, LANES), F32)],
        compiler_params=pltpu.CompilerParams(dimension_semantics=("arbitrary", "arbitrary")),
    )(*[a for a, _ in arrs])
    return res[0], res[1:]


def _scan_bwd(group_fn, name, arrs, kept, dy, heads):
    s = arrs[0][0].shape[0]
    n_chunks = s // CHUNK
    n_in, n_kept = len(arrs), len(kept)

    def body(*refs):
        kept_vals = [r[...] for r in refs[n_in:n_in + n_kept]]
        dy_ref = refs[n_in + n_kept]
        d_refs = refs[n_in + n_kept + 1:2 * n_in + n_kept + 1]
        ds_scr = refs[-1]

        @pl.when(pl.program_id(1) == 0)
        def _():
            ds_scr[...] = jnp.zeros_like(ds_scr)

        def fn(s0, *ins):
            return group_fn(s0, *ins, *kept_vals[1:])[:2]

        _, vjp = jax.vjp(fn, kept_vals[0], *[_split_heads(r[...]) for r in refs[:n_in]])
        grads = vjp((_split_heads(dy_ref[...]), ds_scr[...]))
        ds_scr[...] = grads[0]
        for ref, g in zip(d_refs, grads[1:]):
            ref[...] = _merge_heads(g)

    rev = pl.BlockSpec((CHUNK, SCAN_HB * LANES), lambda h, n: (n_chunks - 1 - n, h))
    per_chunk = pl.BlockSpec((SCAN_HB, None, LANES, LANES), lambda h, n: (h, n_chunks - 1 - n, 0, 0))
    return pl.pallas_call(
        body, grid=(heads // SCAN_HB, n_chunks), name=name,
        in_specs=_scan_specs(arrs, n_chunks, True) + [per_chunk] * n_kept + [rev],
        out_specs=[rev] * n_in,
        out_shape=[SDS((s, heads * LANES), F32)] * n_in,
        scratch_shapes=[pltpu.VMEM((SCAN_HB, LANES, LANES), F32)],
        compiler_params=pltpu.CompilerParams(dimension_semantics=("arbitrary", "arbitrary")),
    )(*[a for a, _ in arrs], *kept, dy)


def _col_spec(tr, width, cb):
    return pl.BlockSpec((tr, width), lambda i: (i, cb))


def _whole(p):
    return pl.BlockSpec(p.shape, lambda i: (0,) * p.ndim)


def _row_fwd(fn, name, tiles, params, outs, tr):
    rows = tiles[0][0].shape[0]
    nt, npar = len(tiles), len(params)

    def body(*refs):
        vals = [r[...].astype(F32) for r in refs[:nt + npar]]
        for ref, o in zip(refs[nt + npar:], fn(*vals)):
            ref[...] = o.astype(ref.dtype)

    return pl.pallas_call(
        body, grid=(rows // tr,), name=name,
        in_specs=[_col_spec(tr, w, cb) for _, w, cb in tiles] + [_whole(p) for p in params],
        out_specs=[_col_spec(tr, w, 0) for w, _ in outs],
        out_shape=[SDS((rows, w), dt) for w, dt in outs],
        compiler_params=pltpu.CompilerParams(dimension_semantics=("arbitrary",), vmem_limit_bytes=VMEM_LIMIT),
    )(*[a for a, _, _ in tiles], *params)


def _row_bwd(fn, name, tiles, params, cts, tr, want_tiles=None):
    rows = tiles[0][0].shape[0]
    nt, npar = len(tiles), len(params)
    want = list(range(nt)) if want_tiles is None else list(want_tiles)
    flat_cts = [c for group in cts for c in group]
    n_ct = len(flat_cts)

    def body(*refs):
        vals = [r[...].astype(F32) for r in refs[:nt + npar]]
        ct_refs = refs[nt + npar:nt + npar + n_ct]
        out_refs = refs[nt + npar + n_ct:]
        ct_vals, at = [], 0
        for group in cts:
            total = ct_refs[at][...].astype(F32)
            for r in ct_refs[at + 1:at + len(group)]:
                total = total + r[...].astype(F32)
            ct_vals.append(total)
            at += len(group)
        _, vjp = jax.vjp(lambda *a: tuple(fn(*a)), *vals)
        grads = vjp(tuple(ct_vals))
        for ref, t in zip(out_refs[:len(want)], want):
            ref[...] = grads[t]
        first = pl.program_id(0) == 0
        for ref, g in zip(out_refs[len(want):], grads[nt:]):
            @pl.when(first)
            def _(ref=ref, g=g):
                ref[...] = g

            @pl.when(jnp.logical_not(first))
            def _(ref=ref, g=g):
                ref[...] += g

    res = pl.pallas_call(
        body, grid=(rows // tr,), name=name,
        in_specs=[_col_spec(tr, w, cb) for _, w, cb in tiles] + [_whole(p) for p in params]
        + [_col_spec(tr, w, cb) for _, w, cb in flat_cts],
        out_specs=[_col_spec(tr, tiles[t][1], 0) for t in want] + [_whole(p) for p in params],
        out_shape=[SDS((rows, tiles[t][1]), F32) for t in want] + [SDS(p.shape, F32) for p in params],
        compiler_params=pltpu.CompilerParams(dimension_semantics=("arbitrary",), vmem_limit_bytes=VMEM_LIMIT),
    )(*[a for a, _, _ in tiles], *params, *[a for a, _, _ in flat_cts])
    return res[:len(want)], res[len(want):]


def _col_fwd(fn, name, x, first_block, n_blocks, params):
    rows = x.shape[0]

    def body(*refs):
        refs[-1][...] = fn(*[r[...] for r in refs[:-1]])

    return pl.pallas_call(
        body, grid=(n_blocks,), name=name,
        in_specs=[pl.BlockSpec((rows, LANES), lambda j: (0, first_block + j))]
        + [pl.BlockSpec((p.shape[0], LANES), lambda j: (0, j)) for p in params],
        out_specs=pl.BlockSpec((rows, LANES), lambda j: (0, j)),
        out_shape=SDS((rows, n_blocks * LANES), F32),
        compiler_params=pltpu.CompilerParams(dimension_semantics=("arbitrary",), vmem_limit_bytes=VMEM_LIMIT),
    )(x, *params)


def _col_bwd(fn, name, x, first_block, n_blocks, params, dy):
    rows = x.shape[0]
    npar = len(params)

    def body(*refs):
        vals = [r[...] for r in refs[:1 + npar]]
        _, vjp = jax.vjp(fn, *vals)
        grads = vjp(refs[1 + npar][...])
        for ref, g in zip(refs[2 + npar:], grads):
            ref[...] = g

    pspecs = [pl.BlockSpec((p.shape[0], LANES), lambda j: (0, j)) for p in params]
    blk = pl.BlockSpec((rows, LANES), lambda j: (0, j))
    res = pl.pallas_call(
        body, grid=(n_blocks,), name=name,
        in_specs=[pl.BlockSpec((rows, LANES), lambda j: (0, first_block + j))] + pspecs + [blk],
        out_specs=[blk] + pspecs,
        out_shape=[SDS((rows, n_blocks * LANES), F32)] + [SDS(p.shape, F32) for p in params],
        compiler_params=pltpu.CompilerParams(dimension_semantics=("arbitrary",), vmem_limit_bytes=VMEM_LIMIT),
    )(x, *params, dy)
    return res[0], res[1:]


def _conv_fn(x, w):
    acc = x * w[3:4, :]
    for j in range(3):
        acc = acc + shift_rows(x, 3 - j) * w[j:j + 1, :]
    return _silu(acc)


def _lerp_fn(x, mu):
    return x + (shift_rows(x, 1) - x) * mu[0:1, :]


def _seg_sum(x, width):
    if width == LANES:
        return jnp.sum(x, axis=1, keepdims=True)
    lo = lax.broadcasted_iota(jnp.int32, x.shape, 1) < width
    s0 = jnp.sum(jnp.where(lo, x, 0.0), axis=1, keepdims=True)
    s1 = jnp.sum(jnp.where(lo, 0.0, x), axis=1, keepdims=True)
    return jnp.where(lo, s0, s1)


def _per_block(fn, *xs):
    n = xs[0].shape[1] // LANES
    return jnp.concatenate([fn(*[x[:, LANES * b:LANES * (b + 1)] for x in xs]) for b in range(n)], axis=1)


def _head_expand(col0):
    r = lax.broadcasted_iota(jnp.int32, (LANES, DN_WIDTH), 0)
    c = lax.shift_right_logical(lax.broadcasted_iota(jnp.int32, (LANES, DN_WIDTH), 1), 7)
    return jnp.where(r == c + col0, 1.0, 0.0)


def _dn_pre_fn(cq, ck, gates, a_log, dt_bias):
    l2 = lambda x: x * lax.rsqrt(_seg_sum(x * x, LANES) + 1e-6)
    qh = _per_block(l2, cq) * (LANES ** -0.5)
    kh = _per_block(l2, ck)
    g = -jnp.exp(a_log) * _softplus(gates + dt_bias)
    gb = mm(g, _head_expand(0), "nn", True)
    bb = mm(_sigmoid(gates), _head_expand(DN_HEADS), "nn", True)
    return qh, kh, gb, bb, _cumsum_rows(gb)


def _dn_post_fn(o, z, nw):
    def one(ob, zb):
        return ob * lax.rsqrt(_seg_sum(ob * ob, LANES) * (1.0 / LANES) + RMS_EPS) * nw * _silu(zb)
    return (_per_block(one, o, z),)


def _rw_pre_fn(pr, pk, pv, pwa, pg, w0, a0, k_k, k_a, w2p, a2p, g2):
    log_w = -_softplus(-(w0 + mm(jnp.tanh(pwa), w2p))) - 0.5
    lw = -jnp.exp(log_w)
    a = _sigmoid(a0 + mm(pwa, a2p))
    gate = mm(_sigmoid(pg), g2)
    kk = pk * k_k
    kk = _per_block(lambda x: x / jnp.maximum(jnp.sqrt(_seg_sum(x * x, RW_HEAD)), 1e-12), kk)
    k = pk * (1.0 + (a - 1.0) * k_a)
    return pr, lw, k, pv, kk * a, -kk, gate, _cumsum_rows(lw)


def _rw_post_fn(y, r, k, v, gate, ln_w, ln_b, r_k):
    def one(yb, rb, kb, vb, gb, wb, bb, rkb):
        d = yb - _seg_sum(yb, RW_HEAD) * (1.0 / RW_HEAD)
        var = _seg_sum(d * d, RW_HEAD) * (1.0 / RW_HEAD)
        yn = d * lax.rsqrt(var + RW_GN_EPS) * wb + bb
        return (yn + _seg_sum(rb * kb * rkb, RW_HEAD) * vb) * gb
    return (_per_block(one, y, r, k, v, gate, ln_w, ln_b, r_k),)


def _rms_fn(h, w):
    return (h * lax.rsqrt(jnp.mean(h * h, axis=1, keepdims=True) + RMS_EPS) * w,)


def _xattn_fn(q, k, v):
    outs = []
    for h in range(XA_HEADS):
        sl = slice(LANES * h, LANES * (h + 1))
        s = mm(q[:, sl], k[:, sl], "nt") * (LANES ** -0.5)
        e = jnp.exp(s - jnp.max(s, axis=1, keepdims=True))
        outs.append(mm(e / jnp.sum(e, axis=1, keepdims=True), v[:, sl]))
    return (jnp.concatenate(outs, axis=1),)


def _fit(tile, dim):
    best = [t for t in range(LANES, min(tile, dim) + 1, LANES) if dim % t == 0]
    assert best, (tile, dim)
    return best[-1]


def _matmul(name, a, b, mode, out_dtypes, epilogue=None, extras=(), tm=1024, tn=1024, tk=2048, after=None):
    if mode == "tn":
        (k_dim, m), n = a.shape, b.shape[1]
    else:
        (m, k_dim), n = a.shape, (b.shape[1] if mode == "nn" else b.shape[0])
    tm, tn, tk = _fit(tm, m), _fit(tn, n), _fit(tk, k_dim)
    nk = k_dim // tk
    a_spec = (pl.BlockSpec((tk, tm), lambda i, j, k: (k, i)) if mode == "tn"
              else pl.BlockSpec((tm, tk), lambda i, j, k: (i, k)))
    b_spec = (pl.BlockSpec((tn, tk), lambda i, j, k: (j, k)) if mode == "nt"
              else pl.BlockSpec((tk, tn), lambda i, j, k: (k, j)))
    o_spec = pl.BlockSpec((tm, tn), lambda i, j, k: (i, j))
    n_ex, n_out = len(extras), len(out_dtypes)
    ties = [] if after is None else [after]

    def finish(total, rest):
        ex = [r[...].astype(F32) for r in rest[:n_ex]]
        res = epilogue(total, *ex) if epilogue else (total,)
        for ref, o in zip(rest[n_ex + len(ties):n_ex + len(ties) + n_out], res):
            ref[...] = o.astype(ref.dtype)

    def body_single(a_ref, b_ref, *rest):
        finish(_raw_dot(a_ref[...], b_ref[...], mode, False), rest)

    def body_acc(a_ref, b_ref, *rest):
        acc = rest[-1]
        k = pl.program_id(2)

        @pl.when(k == 0)
        def _():
            acc[...] = jnp.zeros_like(acc)

        acc[...] += _raw_dot(a_ref[...], b_ref[...], mode, False)

        @pl.when(k == nk - 1)
        def _():
            finish(acc[...], rest)

    res = pl.pallas_call(
        body_single if nk == 1 else body_acc, grid=(m // tm, n // tn, nk), name=name,
        in_specs=[a_spec, b_spec] + [o_spec] * n_ex + [pl.BlockSpec((8, LANES), lambda i, j, k: (0, 0))] * len(ties),
        out_specs=[o_spec] * n_out,
        out_shape=[SDS((m, n), dt) for dt in out_dtypes],
        scratch_shapes=[] if nk == 1 else [pltpu.VMEM((tm, tn), F32)],
        compiler_params=pltpu.CompilerParams(dimension_semantics=("parallel", "parallel", "arbitrary"),
                                             vmem_limit_bytes=VMEM_LIMIT),
    )(a, b, *extras, *ties)
    return res


def _matmul_norm_bwd(name, a, b, mode, h, w, dres, after=None, tm=512, tk=1024):
    m, n = h.shape
    k_dim = a.shape[1]
    tm, tk = _fit(tm, m), _fit(tk, k_dim)
    nk = k_dim // tk
    ties = [] if after is None else [after]
    a_spec = pl.BlockSpec((tm, tk), lambda i, k: (i, k))
    b_spec = pl.BlockSpec((n, tk), lambda i, k: (0, k)) if mode == "nt" else pl.BlockSpec((tk, n), lambda i, k: (k, 0))
    row = pl.BlockSpec((tm, n), lambda i, k: (i, 0))
    w_spec = pl.BlockSpec((1, n), lambda i, k: (0, 0))

    def body(a_ref, b_ref, h_ref, w_ref, dres_ref, *rest):
        dh_ref, dw_ref, acc = rest[len(ties):]
        i, k = pl.program_id(0), pl.program_id(1)

        @pl.when(k == 0)
        def _():
            acc[...] = jnp.zeros_like(acc)

        acc[...] += _raw_dot(a_ref[...], b_ref[...], mode, False)

        @pl.when(k == nk - 1)
        def _():
            _, vjp = jax.vjp(_rms_res_fn, h_ref[...], w_ref[...])
            dh, dw = vjp((acc[...], dres_ref[...]))
            dh_ref[...] = dh

            @pl.when(i == 0)
            def _():
                dw_ref[...] = dw

            @pl.when(i != 0)
            def _():
                dw_ref[...] += dw

    return pl.pallas_call(
        body, grid=(m // tm, nk), name=name,
        in_specs=[a_spec, b_spec, row, w_spec, row] + [pl.BlockSpec((8, LANES), lambda i, k: (0, 0))] * len(ties),
        out_specs=[row, w_spec],
        out_shape=[SDS((m, n), F32), SDS((1, n), F32)],
        scratch_shapes=[pltpu.VMEM((tm, n), F32)],
        compiler_params=pltpu.CompilerParams(dimension_semantics=("arbitrary", "arbitrary"),
                                             vmem_limit_bytes=VMEM_LIMIT),
    )(a, b, h, w, dres, *ties)


def _loss_call(h, target, w, tr=256):
    rows, d = h.shape

    def fn(hv, wv, tv):
        y = _rms_fn(hv, wv)[0]
        return 0.5 * jnp.sum(jnp.mean(jnp.square(y - tv), axis=1, keepdims=True), axis=0, keepdims=True)

    def body(h_ref, t_ref, w_ref, loss_ref, dh_ref, dw_ref):
        tv = t_ref[...]
        val, vjp = jax.vjp(lambda hv, wv: fn(hv, wv, tv), h_ref[...], w_ref[...])
        dh, dw = vjp(jnp.ones((1, 1), F32))
        dh_ref[...] = dh
        first = pl.program_id(0) == 0

        @pl.when(first)
        def _():
            loss_ref[...] = jnp.broadcast_to(val, loss_ref.shape)
            dw_ref[...] = dw

        @pl.when(jnp.logical_not(first))
        def _():
            loss_ref[...] += jnp.broadcast_to(val, loss_ref.shape)
            dw_ref[...] += dw

    return pl.pallas_call(
        body, grid=(rows // tr,), name="loss_head",
        in_specs=[_col_spec(tr, d, 0), _col_spec(tr, d, 0), _whole(w)],
        out_specs=[pl.BlockSpec((8, LANES), lambda i: (0, 0)), _col_spec(tr, d, 0), _whole(w)],
        out_shape=[SDS((8, LANES), F32), SDS((rows, d), F32), SDS(w.shape, F32)],
        compiler_params=pltpu.CompilerParams(dimension_semantics=("arbitrary",), vmem_limit_bytes=VMEM_LIMIT),
    )(h, target, w)


def _adamw_vals(w, g, m, v):
    m = ADAM_B1 * m + (1.0 - ADAM_B1) * g
    v = ADAM_B2 * v + (1.0 - ADAM_B2) * jnp.square(g)
    m_hat = m / (1.0 - ADAM_B1 ** ADAM_STEP)
    v_hat = v / (1.0 - ADAM_B2 ** ADAM_STEP)
    delta = -ADAM_LR * (m_hat / (jnp.sqrt(v_hat) + ADAM_EPS) + ADAM_WD * w)
    return delta, m, v


def _sum_adamw(name, parts, w, m, v):
    r, c = w.shape
    n_parts = parts.shape[0]
    budget = 6 * 1024 * 1024
    tr, tc = r, c
    for cand in (512, 256, 128, 64, 32, 16, 8):
        if r % cand == 0 and n_parts * cand * c * 4 <= budget:
            tr = cand
            break
    if n_parts * tr * c * 4 > budget:
        tc = max(t for t in range(LANES, c + 1, LANES) if c % t == 0 and n_parts * r * t * 4 <= budget)

    def body(p_ref, w_ref, m_ref, v_ref, g_ref, d_ref, m2_ref, v2_ref):
        g = p_ref[0].astype(F32)
        for s in range(1, n_parts):
            g = g + p_ref[s].astype(F32)
        g_ref[...] = g
        d_ref[...], m2_ref[...], v2_ref[...] = _adamw_vals(w_ref[...], g, m_ref[...], v_ref[...])

    blk = pl.BlockSpec((tr, tc), lambda i: (i, 0)) if tc == c else pl.BlockSpec((tr, tc), lambda i: (0, i))
    parts_blk = (pl.BlockSpec((n_parts, tr, tc), lambda i: (0, i, 0)) if tc == c
                 else pl.BlockSpec((n_parts, tr, tc), lambda i: (0, 0, i)))
    return pl.pallas_call(
        body, grid=(r // tr if tc == c else c // tc,), name=name,
        in_specs=[parts_blk, blk, blk, blk],
        out_specs=[blk] * 4, out_shape=[SDS((r, c), F32)] * 4,
        compiler_params=pltpu.CompilerParams(dimension_semantics=("arbitrary",), vmem_limit_bytes=VMEM_LIMIT),
    )(parts, w, m, v)


def _peers():
    x, y, c = lax.axis_index("x"), lax.axis_index("y"), lax.axis_index("c")
    peers = []
    for k in range(1, N_DEV):
        px = 1 - x if k & 4 else x
        py = 1 - y if k & 2 else y
        pc = 1 - c if k & 1 else c
        peers.append(((px, py, pc), 4 * px + 2 * py + pc))
    return 4 * x + 2 * y + c, peers


def _slot(ref, idx, cols):
    if cols is None:
        return ref.at[idx]
    return ref.at[:, pl.ds(pl.multiple_of(idx * cols, LANES), cols)]


def _gather_two_level(name, srcs, dsts):
    n = len(srcs)
    dst_cols = [c for _, _, c in dsts]

    def body(*refs):
        src_refs, out_refs = refs[:n], refs[n:2 * n]
        send_sems, recv_sems, local_sems = refs[2 * n:]
        x, y, c = lax.axis_index("x"), lax.axis_index("y"), lax.axis_index("c")
        index = lambda px, py, pc: 4 * px + 2 * py + pc
        me, sibling = index(x, y, c), (x, y, 1 - c)
        chips = [(x, 1 - y), (1 - x, y), (1 - x, 1 - y)]

        def copy(a, k, src, block, to):
            return pltpu.make_async_remote_copy(
                src_ref=src, dst_ref=_slot(out_refs[a], block, dst_cols[a]),
                send_sem=send_sems.at[a, k], recv_sem=recv_sems.at[a, k],
                device_id=to, device_id_type=pl.DeviceIdType.MESH)

        local, first, passed = [], [], []
        for a in range(n):
            cp = pltpu.make_async_copy(src_refs[a], _slot(out_refs[a], me, dst_cols[a]), local_sems.at[a])
            cp.start()
            local.append(cp)
            first.append(copy(a, 0, src_refs[a], me, sibling))
            first += [copy(a, 1 + j, src_refs[a], me, (*chip, c)) for j, chip in enumerate(chips)]
        for cp in first:
            cp.start()
        for a in range(n):
            for j, chip in enumerate(chips):
                block = index(*chip, c)
                arrived = _slot(out_refs[a], block, dst_cols[a])
                copy(a, 1 + j, arrived, block, (*chip, c)).wait_recv()
                passed.append(copy(a, 4 + j, arrived, block, sibling))
                passed[-1].start()
        for a in range(n):
            copy(a, 0, src_refs[a], index(x, y, 1 - c), sibling).wait_recv()
            for j, chip in enumerate(chips):
                block = index(*chip, 1 - c)
                copy(a, 4 + j, src_refs[a], block, sibling).wait_recv()
        for cp in first + passed:
            cp.wait_send()
        for cp in local:
            cp.wait()

    any_spec = pl.BlockSpec(memory_space=pl.ANY)
    return pl.pallas_call(
        body, name=name,
        in_specs=[any_spec] * n, out_specs=[any_spec] * n,
        out_shape=[SDS(shape, dt) for shape, dt, _ in dsts],
        scratch_shapes=_exchange_sems(n),
    )(*[a for a, _ in srcs])


_HBM = pl.BlockSpec(memory_space=pltpu.HBM)
_SEM = pl.BlockSpec(memory_space=pltpu.SEMAPHORE)
_EFFECT = pltpu.SideEffectType.DATAFLOW_SIDE_EFFECTING


def _split_copies(src_cols, dst_cols, gather, chips, src_refs, land_refs, send_sems, recv_sems, landings):
    me, peers = _peers()
    if chips:
        me, peers = me // 2, [(pos, idx // 2) for k, (pos, idx) in enumerate(peers) if (k + 1) in (2, 4, 6)]
    n, width = len(src_cols), len(peers)
    remote, local = [], []
    for a, (s_cols, d_cols) in enumerate(zip(src_cols, dst_cols)):
        mine = src_refs[a] if gather else _slot(src_refs[a], me, s_cols)
        local.append(pltpu.make_async_copy(mine, _slot(land_refs[a], me, d_cols), send_sems.at[n * width + a]))
        for k, (pos, idx) in enumerate(peers):
            blk = src_refs[a] if gather else _slot(src_refs[a], idx, s_cols)
            remote.append(pltpu.make_async_remote_copy(
                src_ref=blk, dst_ref=_slot(land_refs[a], idx if landings else me, d_cols),
                send_sem=send_sems.at[a * width + k], recv_sem=recv_sems.at[a * width + k],
                device_id=pos, device_id_type=pl.DeviceIdType.MESH))
    return remote, local


def _exchange_start(name, srcs, dsts, gather, after, chips=False):
    n = len(srcs)
    src_cols, dst_cols = [c for _, c in srcs], [c for _, _, c in dsts]
    width = 3 if chips else N_DEV - 1

    def body(*refs):
        src_refs, land_refs = refs[:n], refs[n:2 * n]
        send_sems, recv_sems = refs[2 * n + 1:2 * n + 3]
        token = refs[-1]
        remote, local = _split_copies(src_cols, dst_cols, gather, chips, src_refs, land_refs, send_sems, recv_sems,
                                      False)
        for cp in remote + local:
            cp.start()
        token[...] = jnp.zeros_like(token)

    hbm = lambda a: pltpu.with_memory_space_constraint(a, pltpu.HBM)
    lands = [hbm(lax.empty(shape, dt)) for shape, dt, _ in dsts]
    res = pl.pallas_call(
        body, name=name,
        out_shape=(pltpu.SemaphoreType.DMA((n * (width + 1),)), pltpu.SemaphoreType.DMA((n * width,)),
                   *[pltpu.HBM(a.shape, a.dtype) for a, _ in srcs], *[pltpu.HBM(a.shape, a.dtype) for a in lands],
                   SDS((8, LANES), F32)),
        in_specs=[_HBM] * (2 * n) + [pl.BlockSpec(memory_space=pl.ANY)],
        out_specs=(_SEM, _SEM, *[_HBM] * (2 * n), pl.BlockSpec(memory_space=pltpu.VMEM)),
        input_output_aliases={i: 2 + i for i in range(2 * n)},
        compiler_params=pltpu.CompilerParams(has_side_effects=_EFFECT),
    )(*[hbm(a) for a, _ in srcs], *lands, after)
    handle = (res[0], res[1], res[2:2 + n], res[2 + n:2 + 2 * n], src_cols, dst_cols, gather, chips)
    return handle, res[-1]


def _exchange_wait(name, handle, after):
    send_sems, recv_sems, src_thru, land_thru, src_cols, dst_cols, gather, chips = handle
    n = len(src_thru)

    def body(*refs):
        src_refs, land_refs = refs[:n], refs[n:2 * n]
        s_sems, r_sems = refs[2 * n:2 * n + 2]
        remote, local = _split_copies(src_cols, dst_cols, gather, chips, src_refs, land_refs, s_sems, r_sems, True)
        for cp in remote:
            cp.wait_send()
            cp.wait_recv()
        for cp in local:
            cp.wait()

    res = pl.pallas_call(
        body, name=name,
        out_shape=tuple(pltpu.HBM(a.shape, a.dtype) for a in (*src_thru, *land_thru)),
        in_specs=[_HBM] * (2 * n) + [_SEM, _SEM, pl.BlockSpec(memory_space=pl.ANY)],
        out_specs=tuple([_HBM] * (2 * n)),
        input_output_aliases={i: i for i in range(2 * n)},
        compiler_params=pltpu.CompilerParams(has_side_effects=_EFFECT),
    )(*src_thru, *land_thru, send_sems, recv_sems, after)
    return res[n:]


def _pair_swap(name, arrs):
    n = len(arrs)

    def body(*refs):
        src_refs, out_refs = refs[:n], refs[n:2 * n]
        send_sems, recv_sems = refs[2 * n:]
        x, y, c = lax.axis_index("x"), lax.axis_index("y"), lax.axis_index("c")
        copies = [pltpu.make_async_remote_copy(
            src_ref=src_refs[a].at[:, 1 - c], dst_ref=out_refs[a], send_sem=send_sems.at[a], recv_sem=recv_sems.at[a],
            device_id=(x, y, 1 - c), device_id_type=pl.DeviceIdType.MESH) for a in range(n)]
        for cp in copies:
            cp.start()
        for cp in copies:
            cp.wait()

    any_spec = pl.BlockSpec(memory_space=pl.ANY)
    return pl.pallas_call(
        body, name=name,
        in_specs=[any_spec] * n, out_specs=[any_spec] * n,
        out_shape=[SDS((a.shape[0],) + a.shape[2:], a.dtype) for a in arrs],
        scratch_shapes=[pltpu.SemaphoreType.DMA((n,)), pltpu.SemaphoreType.DMA((n,))],
    )(*arrs)


def _pair_add(name, mine, theirs):
    four, _, r, c = mine.shape
    tr = r
    for cand in (512, 256, 128, 64, 32, 16, 8):
        if r % cand == 0:
            tr = cand
            break
    tc = max(t for t in range(LANES, c + 1, LANES) if c % t == 0 and (t == LANES or 2 * tr * t * 4 <= 4 * 1024 * 1024))

    def body(m_ref, t_ref, o_ref):
        core = lax.axis_index("c")
        both = m_ref[...].astype(F32)
        own = jnp.where(core == 0, both[0], both[1])
        o_ref[...] = (own + t_ref[...].astype(F32)).astype(o_ref.dtype)

    return pl.pallas_call(
        body, grid=(four, r // tr, c // tc), name=name,
        in_specs=[pl.BlockSpec((None, 2, tr, tc), lambda i, j, k: (i, 0, j, k)),
                  pl.BlockSpec((None, tr, tc), lambda i, j, k: (i, j, k))],
        out_specs=pl.BlockSpec((None, tr, tc), lambda i, j, k: (i, j, k)),
        out_shape=SDS(theirs.shape, theirs.dtype),
        compiler_params=pltpu.CompilerParams(dimension_semantics=("arbitrary",) * 3, vmem_limit_bytes=VMEM_LIMIT),
    )(mine, theirs)


def _my_index():
    return 4 * lax.axis_index("x") + 2 * lax.axis_index("y") + lax.axis_index("c")


def _two_level_copies(stage, dst_cols, src_refs, land_refs, send_sems, recv_sems, landings):
    x, y, c = lax.axis_index("x"), lax.axis_index("y"), lax.axis_index("c")

    def pos(k):
        return (1 - x if k & 4 else x, 1 - y if k & 2 else y, 1 - c if k & 1 else c)

    def idx(k):
        px, py, pc = pos(k)
        return 4 * px + 2 * py + pc

    out = []
    for a, cols in enumerate(dst_cols):
        if stage == 1:
            for i, k in enumerate((1, 2, 4, 6)):
                out.append(pltpu.make_async_remote_copy(
                    src_ref=src_refs[a], dst_ref=_slot(land_refs[a], idx(k) if landings else idx(0), cols),
                    send_sem=send_sems.at[4 * a + i], recv_sem=recv_sems.at[4 * a + i],
                    device_id=pos(k), device_id_type=pl.DeviceIdType.MESH))
        else:
            for i, k in enumerate((2, 4, 6)):
                out.append(pltpu.make_async_remote_copy(
                    src_ref=_slot(land_refs[a], idx(k), cols),
                    dst_ref=_slot(land_refs[a], idx(k ^ 1) if landings else idx(k), cols),
                    send_sem=send_sems.at[3 * a + i], recv_sem=recv_sems.at[3 * a + i],
                    device_id=pos(1), device_id_type=pl.DeviceIdType.MESH))
    return out


def _gather2_start(name, srcs, dsts, after):
    n = len(srcs)
    dst_cols = [c for _, _, c in dsts]

    def body(*refs):
        src_refs, land_refs = refs[:n], refs[n:2 * n]
        send_sems, recv_sems = refs[2 * n + 1:2 * n + 3]
        me = _my_index()
        for a in range(n):
            pltpu.make_async_copy(src_refs[a], _slot(land_refs[a], me, dst_cols[a]), send_sems.at[4 * n + a]).start()
        for cp in _two_level_copies(1, dst_cols, src_refs, land_refs, send_sems, recv_sems, False):
            cp.start()
        refs[-1][...] = jnp.zeros_like(refs[-1])

    hbm = lambda a: pltpu.with_memory_space_constraint(a, pltpu.HBM)
    lands = [hbm(lax.empty(shape, dt)) for shape, dt, _ in dsts]
    res = pl.pallas_call(
        body, name=name,
        out_shape=(pltpu.SemaphoreType.DMA((5 * n,)), pltpu.SemaphoreType.DMA((4 * n,)),
                   *[pltpu.HBM(a.shape, a.dtype) for a, _ in srcs], *[pltpu.HBM(a.shape, a.dtype) for a in lands],
                   SDS((8, LANES), F32)),
        in_specs=[_HBM] * (2 * n) + [pl.BlockSpec(memory_space=pl.ANY)],
        out_specs=(_SEM, _SEM, *[_HBM] * (2 * n), pl.BlockSpec(memory_space=pltpu.VMEM)),
        input_output_aliases={i: 2 + i for i in range(2 * n)},
        compiler_params=pltpu.CompilerParams(has_side_effects=_EFFECT),
    )(*[hbm(a) for a, _ in srcs], *lands, after)
    return (res[0], res[1], res[2:2 + n], res[2 + n:2 + 2 * n], dst_cols), res[-1]


def _gather2_pass(name, handle, after):
    send1, recv1, src_thru, land_thru, dst_cols = handle
    n = len(src_thru)

    def body(*refs):
        src_refs, land_refs = refs[:n], refs[n:2 * n]
        s1, r1 = refs[2 * n:2 * n + 2]
        send2, recv2 = refs[2 * n + 3:2 * n + 5]
        me = _my_index()
        for cp in _two_level_copies(1, dst_cols, src_refs, land_refs, s1, r1, True):
            cp.wait_send()
            cp.wait_recv()
        for a in range(n):
            pltpu.make_async_copy(src_refs[a], _slot(land_refs[a], me, dst_cols[a]), s1.at[4 * n + a]).wait()
        for cp in _two_level_copies(2, dst_cols, src_refs, land_refs, send2, recv2, False):
            cp.start()
        refs[-1][...] = jnp.zeros_like(refs[-1])

    res = pl.pallas_call(
        body, name=name,
        out_shape=(pltpu.SemaphoreType.DMA((3 * n,)), pltpu.SemaphoreType.DMA((3 * n,)),
                   *[pltpu.HBM(a.shape, a.dtype) for a in (*src_thru, *land_thru)], SDS((8, LANES), F32)),
        in_specs=[_HBM] * (2 * n) + [_SEM, _SEM, pl.BlockSpec(memory_space=pl.ANY)],
        out_specs=(_SEM, _SEM, *[_HBM] * (2 * n), pl.BlockSpec(memory_space=pltpu.VMEM)),
        input_output_aliases={i: 2 + i for i in range(2 * n)},
        compiler_params=pltpu.CompilerParams(has_side_effects=_EFFECT),
    )(*src_thru, *land_thru, send1, recv1, after)
    return (res[0], res[1], res[2:2 + n], res[2 + n:2 + 2 * n], dst_cols), res[-1]


def _gather2_wait(name, handle, after):
    send2, recv2, src_thru, land_thru, dst_cols = handle
    n = len(src_thru)

    def body(*refs):
        src_refs, land_refs = refs[:n], refs[n:2 * n]
        s2, r2 = refs[2 * n:2 * n + 2]
        for cp in _two_level_copies(2, dst_cols, src_refs, land_refs, s2, r2, True):
            cp.wait_send()
            cp.wait_recv()

    res = pl.pallas_call(
        body, name=name,
        out_shape=tuple(pltpu.HBM(a.shape, a.dtype) for a in (*src_thru, *land_thru)),
        in_specs=[_HBM] * (2 * n) + [_SEM, _SEM, pl.BlockSpec(memory_space=pl.ANY)],
        out_specs=tuple([_HBM] * (2 * n)),
        input_output_aliases={i: i for i in range(2 * n)},
        compiler_params=pltpu.CompilerParams(has_side_effects=_EFFECT),
    )(*src_thru, *land_thru, send2, recv2, after)
    return res[n:]


def _exchange_sems(n):
    return [pltpu.SemaphoreType.DMA((n, N_DEV - 1)), pltpu.SemaphoreType.DMA((n, N_DEV - 1)),
            pltpu.SemaphoreType.DMA((n,))]


def _rms_res_fn(h, w):
    return _rms_fn(h, w)[0], h


def _add_epilogue(acc, res):
    return (acc + res,)


def _gather_plan(shards):
    srcs, dsts = [], []
    for n, sh in shards.items():
        r, c = sh.shape
        srcs.append((sh, None))
        if SHARDED[n] and c % LANES == 0:
            dsts.append(((r, N_DEV * c), sh.dtype, c))
        else:
            dsts.append(((N_DEV, r, c), sh.dtype, None))
    return srcs, dsts, True


def _w_in_segments():
    out = []
    for j in range(N_DEV):
        lo, hi = W_IN_SHARD * j, W_IN_SHARD * (j + 1)
        for a, b in ((lo, min(hi, DN_COLS)), (max(lo, DN_COLS), hi)):
            if a < b:
                out.append((j, a - lo, b - lo, a if a < DN_COLS else a + RW_OFF - DN_COLS))
    return out


def _w_in_to_padded(shards, tc=512):
    _, _, cols = shards.shape

    def body(g_ref, o_ref):
        o_ref[...] = jnp.zeros_like(o_ref)
        for j, a, b, dst in _w_in_segments():
            o_ref[dst:dst + b - a, :] = g_ref[j, a:b, :]

    return pl.pallas_call(
        body, grid=(cols // tc,), name="w_in_to_padded",
        in_specs=[pl.BlockSpec((N_DEV, W_IN_SHARD, tc), lambda i: (0, 0, i))],
        out_specs=pl.BlockSpec((IN_PAD, tc), lambda i: (0, i)),
        out_shape=SDS((IN_PAD, cols), shards.dtype),
        compiler_params=pltpu.CompilerParams(dimension_semantics=("arbitrary",), vmem_limit_bytes=VMEM_LIMIT),
    )(shards)


def _w_in_grad_to_shards(gw, tc=512):
    _, cols = gw.shape

    def body(w_ref, o_ref):
        for j, a, b, dst in _w_in_segments():
            o_ref[j, a:b, :] = w_ref[dst:dst + b - a, :]

    return pl.pallas_call(
        body, grid=(cols // tc,), name="w_in_grad_to_shards",
        in_specs=[pl.BlockSpec((IN_PAD, tc), lambda i: (0, i))],
        out_specs=pl.BlockSpec((N_DEV, W_IN_SHARD, tc), lambda i: (0, 0, i)),
        out_shape=SDS((N_DEV, W_IN_SHARD, cols), gw.dtype),
        compiler_params=pltpu.CompilerParams(dimension_semantics=("arbitrary",), vmem_limit_bytes=VMEM_LIMIT),
    )(gw)


def _gather_finish(names, outs):
    full = {}
    for n, arr in zip(names, outs):
        if n == "w_in":
            full[n] = _w_in_to_padded(arr)
        elif arr.ndim == 2:
            full[n] = arr
        elif SHARDED[n]:
            full[n] = arr.transpose(1, 0, 2).reshape(arr.shape[1], -1)
        else:
            full[n] = arr.reshape(-1, arr.shape[2])
    return full


def _scatter_plan(grads):
    srcs, dsts = [], []
    for n, gr in grads.items():
        if gr.ndim == 3:
            srcs.append((gr, None))
            dsts.append((gr.shape, gr.dtype, None))
            continue
        rows, cols = gr.shape
        if not SHARDED[n]:
            r, c = rows // N_DEV, cols
            srcs.append((gr.reshape(N_DEV, r, c), None))
        else:
            r, c = rows, cols // N_DEV
            if c % LANES == 0:
                srcs.append((gr, c))
            else:
                srcs.append((gr.reshape(r, N_DEV, c).transpose(1, 0, 2), None))
        dsts.append(((N_DEV, r, c), gr.dtype, None))
    return srcs, dsts, False


def _local_step(x, mem, target, wt, late):
    d = D_MODEL
    g = {}
    wt = dict(wt)
    grp_a = ("w_out", "xa_wq", "xa_wk", "xa_wv", "xa_wo")
    grp_b = ("ffn_w1", "ffn_w2")
    plan = lambda names: _gather_plan({n: late[n] for n in names})[:2]
    handle_a, tok_a = _gather2_start("late_gather_a_start", *plan(grp_a), wt["w_in"])
    handle_w1, tok_b = _gather2_start("late_gather_w1_start", *plan(("ffn_w1",)), tok_a)
    handle_w2, tok_c = _gather2_start("late_gather_w2_start", *plan(("ffn_w2",)), tok_b)
    mix_w = wt["mix_norm_w"] + (tok_a[0:1, 0:1] + tok_b[0:1, 0:1] + tok_c[0:1, 0:1])
    u = _row_fwd(_rms_fn, "mix_norm", [(x, d, 0)], [mix_w], [(d, BF16)], 256)[0]
    p = _matmul("in_proj", u, wt["w_in"], "nt", [F32], tn=1536)[0]
    c = _col_fwd(_conv_fn, "dn_conv", p, 0, 24, [wt["dn_conv_w"]])
    handle_a, tok = _gather2_pass("late_gather_a_pass", handle_a, c)
    dn_pre_tiles = [(c, DN_WIDTH, 0), (c, DN_WIDTH, 1), (p, LANES, 32)]
    dn_pre_params = [wt["dn_a_log"], wt["dn_dt_bias"]]
    qh, kh, gb, bb, gcb = _row_fwd(_dn_pre_fn, "dn_pre", dn_pre_tiles, [dn_pre_params[0] + tok[0:1, :], dn_pre_params[1]],
                                   [(DN_WIDTH, F32)] * 5, CHUNK)
    dn_arrs = [(qh, 0), (kh, 0), (c, 16), (gb, 0), (bb, 0), (gcb, 0)]
    o, kept_dn = _scan_fwd(_gdn_group, "gdn_scan", dn_arrs, DN_HEADS, 1)
    dn_post_tiles = [(o, DN_WIDTH, 0), (p, DN_WIDTH, 3)]
    o_dn = _row_fwd(_dn_post_fn, "dn_post", dn_post_tiles, [wt["dn_norm_w"]], [(DN_WIDTH, BF16)], 256)[0]

    ps = _col_fwd(_lerp_fn, "rw_shift", p, RW_OFF // LANES, 26, [wt["rw_mu"]])
    rw_pre_tiles = [(ps, RW_WIDTH, 0), (ps, RW_WIDTH, 1), (ps, RW_WIDTH, 2), (ps, LANES, 24), (ps, LANES, 25)]
    rw_pre_params = [wt[n] for n in ("rw_w0", "rw_a0", "rw_k_k", "rw_k_a", "rw_w2", "rw_a2", "rw_g2")]
    r, lw, k, v, al, be, gate, gcw = _row_fwd(_rw_pre_fn, "rw_pre", rw_pre_tiles, rw_pre_params,
                                              [(RW_WIDTH, F32)] * 8, CHUNK)
    rw_arrs = [(r, 0), (lw, 0), (k, 0), (v, 0), (al, 0), (be, 0), (gcw, 0)]
    y, kept_rw = _scan_fwd(_rw_group, "rw_scan", rw_arrs, RW_WIDTH // LANES, 2)
    handle_w1, tok = _gather2_pass("late_gather_w1_pass", handle_w1, y)
    rw_post_tiles = [(t, RW_WIDTH, 0) for t in (y, r, k, v, gate)]
    rw_post_params = [wt["rw_ln_w"], wt["rw_ln_b"], wt["rw_r_k"]]
    o_rw = _row_fwd(_rw_post_fn, "rw_post", rw_post_tiles, [rw_post_params[0] + tok[0:1, 0:1]] + rw_post_params[1:],
                    [(RW_WIDTH, BF16)], 128)[0]
    o_cat = jnp.concatenate([o_dn, o_rw], axis=1)
    wt.update(_gather_finish(grp_a, _gather2_wait("late_gather_a_wait", handle_a, o_cat)))
    h1 = _matmul("out_proj", o_cat, wt["w_out"], "nn", [F32], _add_epilogue, (x,))[0]

    handle_w2, tok = _gather2_pass("late_gather_w2_pass", handle_w2, h1)
    hn = _row_fwd(_rms_fn, "xa_norm", [(h1, d, 0)], [wt["xa_norm_w"] + tok[0:1, 0:1]], [(d, BF16)], 256)[0]
    mn = _row_fwd(_rms_fn, "mem_norm", [(mem, d, 0)], [wt["mem_norm_w"]], [(d, BF16)], 256)[0]
    q = _matmul("xa_q", hn, wt["xa_wq"], "nn", [F32])[0]
    kx = _matmul("xa_k", mn, wt["xa_wk"], "nn", [F32])[0]
    vx = _matmul("xa_v", mn, wt["xa_wv"], "nn", [F32])[0]
    ao = _row_fwd(_xattn_fn, "xattn", [(q, XA_WIDTH, 0)], [kx, vx], [(XA_WIDTH, BF16)], 256)[0]
    h2 = _matmul("xa_o", ao, wt["xa_wo"], "nn", [F32], _add_epilogue, (h1,))[0]

    f = _row_fwd(_rms_fn, "ffn_norm", [(h2, d, 0)], [wt["ffn_norm_w"]], [(d, BF16)], 256)[0]
    wt.update(_gather_finish(("ffn_w1",), _gather2_wait("late_gather_w1_wait", handle_w1, f)))
    a, hid = _matmul("ffn_up", f, wt["ffn_w1"], "nn", [F32, BF16],
                     lambda acc: (acc, jnp.square(jnp.maximum(acc, 0.0))))
    wt.update(_gather_finish(("ffn_w2",), _gather2_wait("late_gather_w2_wait", handle_w2, hid)))
    h3 = _matmul("ffn_down", hid, wt["ffn_w2"], "nn", [F32], _add_epilogue, (h2,))[0]
    loss8, dh3, g["final_norm_w"] = _loss_call(h3, target, wt["final_norm_w"])

    da = _matmul("ffn_down_dx", dh3, wt["ffn_w2"], "nt", [BF16],
                 lambda acc, av: (acc * 2.0 * jnp.maximum(av, 0.0),), (a,))[0]
    g["ffn_w2"] = _matmul("ffn_down_dw", hid, dh3, "tn", [BF16])[0]
    g["ffn_w1"] = _matmul("ffn_up_dw", f, da, "tn", [BF16])[0]
    pending = {}
    plan = _scatter_plan({n: g.pop(n) for n in grp_b})
    pending[grp_b], tok = _exchange_start("late_grad_b_start", *plan, loss8)
    dh2, g["ffn_norm_w"] = _matmul_norm_bwd("ffn_up_dx", da, wt["ffn_w1"], "nt", h2, wt["ffn_norm_w"], dh3, tok)

    dao = _matmul("xa_o_dx", dh2, wt["xa_wo"], "nt", [F32])[0]
    g["xa_wo"] = _matmul("xa_o_dw", ao, dh2, "tn", [BF16])[0]
    (dq,), (dkx, dvx) = _row_bwd(_xattn_fn, "xattn_bwd", [(q, XA_WIDTH, 0)], [kx, vx], [[(dao, XA_WIDTH, 0)]], 256)
    dh1, g["xa_norm_w"] = _matmul_norm_bwd("xa_q_dx", dq, wt["xa_wq"], "nt", h1, wt["xa_norm_w"], dh2)
    g["xa_wq"] = _matmul("xa_q_dw", hn, dq, "tn", [BF16])[0]
    g["xa_wk"] = _matmul("xa_k_dw", mn, dkx, "tn", [BF16])[0]
    g["xa_wv"] = _matmul("xa_v_dw", mn, dvx, "tn", [BF16])[0]
    dmn = _matmul("xa_k_dx", dkx, wt["xa_wk"], "nt", [F32])[0]
    dmn = _matmul("xa_v_dx", dvx, wt["xa_wv"], "nt", [F32], _add_epilogue, (dmn,))[0]
    _, (g["mem_norm_w"],) = _row_bwd(_rms_fn, "mem_norm_bwd", [(mem, d, 0)], [wt["mem_norm_w"]],
                                     [[(dmn, d, 0)]], 256, want_tiles=())

    do_cat = _matmul("out_proj_dx", dh1, wt["w_out"], "nt", [F32])[0]
    g["w_out"] = _matmul("out_proj_dw", o_cat, dh1, "tn", [BF16])[0]

    plan = _scatter_plan({n: g.pop(n) for n in grp_a})
    pending[grp_a], tok = _exchange_start("late_grad_a_start", *plan, tok)
    (dy, dr1, dk1, dv1, dgate), (g["rw_ln_w"], g["rw_ln_b"], g["rw_r_k"]) = _row_bwd(
        _rw_post_fn, "rw_post_bwd", rw_post_tiles, [rw_post_params[0] + tok[0:1, 0:1]] + rw_post_params[1:],
        [[(do_cat, RW_WIDTH, 1)]], 128)
    dr2, dlw, dk2, dv2, dal, dbe, dgcw = _scan_bwd(_rw_group, "rw_scan_bwd", rw_arrs, kept_rw, dy,
                                                   RW_WIDTH // LANES)
    one = lambda t: [(t, RW_WIDTH, 0)]
    two = lambda s, t: [(s, RW_WIDTH, 0), (t, RW_WIDTH, 0)]
    d_ps, rw_pre_grads = _row_bwd(
        _rw_pre_fn, "rw_pre_bwd", rw_pre_tiles, rw_pre_params,
        [two(dr1, dr2), one(dlw), two(dk1, dk2), two(dv1, dv2), one(dal), one(dbe), one(dgate), one(dgcw)],
        CHUNK)
    for n, val in zip(("rw_w0", "rw_a0", "rw_k_k", "rw_k_a", "rw_w2", "rw_a2", "rw_g2"), rw_pre_grads):
        g[n] = val
    dp_rw, (g["rw_mu"],) = _col_bwd(_lerp_fn, "rw_shift_bwd", p, RW_OFF // LANES, 26, [wt["rw_mu"]],
                                    jnp.concatenate(d_ps, axis=1))

    (do, dz), (g["dn_norm_w"],) = _row_bwd(_dn_post_fn, "dn_post_bwd", dn_post_tiles, [wt["dn_norm_w"]],
                                           [[(do_cat, DN_WIDTH, 0)]], 256)
    dqh, dkh, dv_dn, dgb, dbb, dgcb = _scan_bwd(_gdn_group, "gdn_scan_bwd", dn_arrs, kept_dn, do, DN_HEADS)
    one = lambda t: [(t, DN_WIDTH, 0)]
    (dcq, dck, dgates), (g["dn_a_log"], g["dn_dt_bias"]) = _row_bwd(
        _dn_pre_fn, "dn_pre_bwd", dn_pre_tiles, dn_pre_params,
        [one(dqh), one(dkh), one(dgb), one(dbb), one(dgcb)], CHUNK)
    dp_qkv, (g["dn_conv_w"],) = _col_bwd(_conv_fn, "dn_conv_bwd", p, 0, 24, [wt["dn_conv_w"]],
                                         jnp.concatenate([dcq, dck, dv_dn], axis=1))
    dp = jnp.concatenate([dp_qkv, dz, dgates, dp_rw, jnp.zeros((x.shape[0], LANES), F32)], axis=1).astype(BF16)
    g["w_in"] = _matmul("in_proj_dw", dp, u, "tn", [BF16], tm=1536)[0]
    early = _logical_grads(g)
    blocks = []
    for src, cols in _scatter_plan({n: early.pop(n) for n in EARLY})[0]:
        if cols is not None:
            src = src.reshape(src.shape[0], N_DEV, cols).transpose(1, 0, 2)
        blocks.append(src.reshape((4, 2) + src.shape[1:]))
    sums = [_pair_add("early_grad_pair_add_%d" % i, mine, theirs)
            for i, (mine, theirs) in enumerate(zip(blocks, _pair_swap("early_grad_pair_swap", blocks)))]
    pending[EARLY], tok = _exchange_start("early_grad_start", [(t, None) for t in sums],
                                          [(t.shape, t.dtype, None) for t in sums], False, tok, chips=True)
    dx, early["mix_norm_w"] = _matmul_norm_bwd("in_proj_dx", dp, wt["w_in"], "nn", x, wt["mix_norm_w"], dh1, tok)
    return loss8, dx, early, pending, tok


WEIGHTS = ["mix_norm_w", "w_in", "dn_conv_w", "dn_a_log", "dn_dt_bias", "dn_norm_w", "rw_mu", "rw_w0", "rw_w2",
           "rw_a0", "rw_a2", "rw_g2", "rw_k_k", "rw_k_a", "rw_r_k", "rw_ln_w", "rw_ln_b", "w_out", "xa_norm_w",
           "mem_norm_w", "xa_wq", "xa_wk", "xa_wv", "xa_wo", "ffn_norm_w", "ffn_w1", "ffn_w2", "final_norm_w"]
SHARDED = {"w_in": False, "w_out": False, "xa_wq": False, "xa_wk": False, "xa_wv": False, "xa_wo": True,
           "ffn_w1": True, "ffn_w2": False, "dn_conv_w": True, "rw_w2": True, "rw_a2": True, "rw_g2": True}
BF16_PAYLOAD = ("w_in", "w_out", "xa_wq", "xa_wk", "xa_wv", "xa_wo", "ffn_w1", "ffn_w2")
REPLICATED = [n for n in WEIGHTS if n not in SHARDED]
EARLY = ("w_in", "dn_conv_w", "rw_w2", "rw_a2", "rw_g2")
RW_IN_COLS = IN_COLS - DN_COLS
W_IN_SHARD = IN_COLS // N_DEV


def _layout_weights(fw):
    wt = dict(fw)
    wt["dn_conv_w"] = jnp.pad(fw["dn_conv_w"], ((0, 4), (0, 0)))
    wt["dn_a_log"] = jnp.pad(fw["dn_a_log"], ((0, 0), (0, LANES - DN_HEADS)))
    wt["dn_dt_bias"] = jnp.pad(fw["dn_dt_bias"], ((0, 0), (0, LANES - DN_HEADS)))
    wt["rw_w2"] = jnp.pad(fw["rw_w2"], ((0, 64), (0, 0)))
    wt["rw_a2"] = jnp.pad(fw["rw_a2"], ((64, 0), (0, 0)))
    return wt


def _logical_grads(g):
    out = dict(g)
    out["w_in"] = _w_in_grad_to_shards(g["w_in"])
    out["dn_conv_w"] = g["dn_conv_w"][:4]
    out["dn_a_log"] = g["dn_a_log"][:, :DN_HEADS]
    out["dn_dt_bias"] = g["dn_dt_bias"][:, :DN_HEADS]
    out["rw_w2"] = g["rw_w2"][:64]
    out["rw_a2"] = g["rw_a2"][64:]
    return out


def _pack(vals):
    parts = []
    for v in vals:
        flat = v.reshape(-1)
        parts.append(jnp.pad(flat, (0, -flat.shape[0] % LANES)))
    flat = jnp.concatenate(parts)
    flat = jnp.pad(flat, (0, -flat.shape[0] % (8 * LANES)))
    return flat.reshape(-1, LANES)


def _unpack(packed, shapes):
    flat = packed.reshape(-1)
    out, at = [], 0
    for shp in shapes:
        size = math.prod(shp)
        out.append(flat[at:at + size].reshape(shp))
        at += size + (-size % LANES)
    return out


def kernel(x, mem, mix_norm_w, w_in, dn_conv_w, dn_a_log, dn_dt_bias, dn_norm_w, rw_mu, rw_w0, rw_w2, rw_a0, rw_a2, rw_g2, rw_k_k, rw_k_a, rw_r_k, rw_ln_w, rw_ln_b, w_out, xa_norm_w, mem_norm_w, xa_wq, xa_wk, xa_wv, xa_wo, ffn_norm_w, ffn_w1, ffn_w2, final_norm_w, loss_target, m_mix_norm_w, m_w_in, m_dn_conv_w, m_dn_a_log, m_dn_dt_bias, m_dn_norm_w, m_rw_mu, m_rw_w0, m_rw_w2, m_rw_a0, m_rw_a2, m_rw_g2, m_rw_k_k, m_rw_k_a, m_rw_r_k, m_rw_ln_w, m_rw_ln_b, m_w_out, m_xa_norm_w, m_mem_norm_w, m_xa_wq, m_xa_wk, m_xa_wv, m_xa_wo, m_ffn_norm_w, m_ffn_w1, m_ffn_w2, m_final_norm_w, v_mix_norm_w, v_w_in, v_dn_conv_w, v_dn_a_log, v_dn_dt_bias, v_dn_norm_w, v_rw_mu, v_rw_w0, v_rw_w2, v_rw_a0, v_rw_a2, v_rw_g2, v_rw_k_k, v_rw_k_a, v_rw_r_k, v_rw_ln_w, v_rw_ln_b, v_w_out, v_xa_norm_w, v_mem_norm_w, v_xa_wq, v_xa_wk, v_xa_wv, v_xa_wo, v_ffn_norm_w, v_ffn_w1, v_ffn_w2, v_final_norm_w):
    given = dict(locals())
    w = {n: given[n] for n in WEIGHTS}
    m = {n: given["m_" + n] for n in WEIGHTS}
    v = {n: given["v_" + n] for n in WEIGHTS}

    local = {n: (lambda t: t[0].T) if n == "w_in" else (lambda t: t[0]) for n in SHARDED}
    shards = {n: (local[n](w[n]).astype(BF16) if n in BF16_PAYLOAD else local[n](w[n])) for n in SHARDED}
    srcs, dsts, _ = _gather_plan({n: shards[n] for n in EARLY})
    full = _gather_finish(EARLY, _gather_two_level("early_all_gather", srcs, dsts))
    for n in REPLICATED:
        full[n] = w[n].reshape(1, -1)

    loss8, dx, g, pending, after = _local_step(x[0], mem[0], loss_target[0], _layout_weights(full),
                                               {n: shards[n] for n in SHARDED if n not in EARLY})

    packed = _pack([g[n] for n in REPLICATED] + [loss8[:1, :1]])
    small, _ = _exchange_start("small_gather_start", [(packed, None)], [((N_DEV,) + packed.shape, F32, None)], True,
                               after)
    grad, delta, new_m, new_v = {}, {}, {}, {}
    done = [dx]

    def tie():
        return jnp.broadcast_to(sum(t[:1, :1] for t in done), (8, LANES))

    for names in sorted(pending, key=lambda names: names == EARLY):
        handle = pending[names]
        for n, parts in zip(names, _exchange_wait("grad_wait_" + names[0], handle, tie())):
            res = _sum_adamw("adamw_" + n, parts, local[n](w[n]), local[n](m[n]), local[n](v[n]))
            grad[n], delta[n], new_m[n], new_v[n] = [(t.T if n == "w_in" else t)[None] for t in res]
            done.append(res[1])

    (parts,) = _exchange_wait("small_gather_wait", small, tie())
    blank = [jnp.zeros((1, 1), F32)]
    res = _sum_adamw("adamw_small", parts, _pack([w[n] for n in REPLICATED] + blank),
                     _pack([m[n] for n in REPLICATED] + blank), _pack([v[n] for n in REPLICATED] + blank))
    shapes = [w[n].shape for n in REPLICATED] + [()]
    loss = _unpack(res[0], shapes)[-1]
    for store, packed_out in zip((grad, delta, new_m, new_v), res):
        for n, val in zip(REPLICATED, _unpack(packed_out, shapes)):
            store[n] = val

    return (loss, dx[None], *[grad[n] for n in WEIGHTS], *[delta[n] for n in WEIGHTS],
            *[new_m[n] for n in WEIGHTS], *[new_v[n] for n in WEIGHTS])
```

```python
import functools
import math

import jax
import jax.numpy as jnp
from jax import lax
from jax.experimental import pallas as pl
from jax.experimental.pallas import tpu as pltpu

F32 = jnp.float32
BF16 = jnp.bfloat16
SDS = jax.ShapeDtypeStruct

N_DEV = 8
D_MODEL = 2048
LANES = 128
CHUNK = 128
DN_HEADS = 8
DN_WIDTH = 1024
RW_WIDTH = 1024
RW_HEAD = 64
XA_HEADS = 4
XA_WIDTH = 512
FFN_HIDDEN = 8192
IN_COLS = 7440
DN_COLS = 4112
IN_PAD = 7680
RW_OFF = 4224
RMS_EPS = 1e-6
RW_GN_EPS = 64e-5
VMEM_LIMIT = 56 * 1024 * 1024

ADAM_LR = 0.001
ADAM_B1 = 0.9
ADAM_B2 = 0.999
ADAM_EPS = 1e-08
ADAM_WD = 0.01
ADAM_STEP = 10

_DIMS = {"nn": (((1,), (0,)), ((), ())), "nt": (((1,), (1,)), ((), ())), "tn": (((0,), (0,)), ((), ()))}


def _raw_dot(a, b, mode, hi):
    if hi:
        return lax.dot_general(a, b, _DIMS[mode], precision=lax.Precision.HIGHEST, preferred_element_type=F32)
    return lax.dot_general(a.astype(BF16), b.astype(BF16), _DIMS[mode], preferred_element_type=F32)


@functools.partial(jax.custom_vjp, nondiff_argnums=(2, 3))
def mm(a, b, mode="nn", hi=False):
    return _raw_dot(a, b, mode, hi)


def _mm_fwd(a, b, mode, hi):
    return _raw_dot(a, b, mode, hi), (a, b)


def _mm_bwd(mode, hi, res, g):
    a, b = res
    if mode == "nn":
        return _raw_dot(g, b, "nt", hi), _raw_dot(a, g, "tn", hi)
    if mode == "nt":
        return _raw_dot(g, b, "nn", hi), _raw_dot(g, a, "tn", hi)
    return _raw_dot(b, g, "nt", hi), _raw_dot(a, g, "nn", hi)


mm.defvjp(_mm_fwd, _mm_bwd)


def _shift_rows_raw(x, k):
    n = x.shape[0]
    rolled = pltpu.roll(x, k % n, axis=0)
    row = lax.broadcasted_iota(jnp.int32, x.shape, 0)
    keep = row >= k if k > 0 else row < n + k
    return jnp.where(keep, rolled, 0.0)


@functools.partial(jax.custom_vjp, nondiff_argnums=(1,))
def shift_rows(x, k):
    return _shift_rows_raw(x, k)


shift_rows.defvjp(lambda x, k: (_shift_rows_raw(x, k), None), lambda k, _, g: (_shift_rows_raw(g, -k),))


def _softplus(x):
    return jnp.maximum(x, 0.0) + jnp.log(1.0 + jnp.exp(-jnp.abs(x)))


def _sigmoid(x):
    return 1.0 / (1.0 + jnp.exp(-x))


def _silu(x):
    return x * _sigmoid(x)


def _tri_masks(n):
    ii = lax.broadcasted_iota(jnp.int32, (n, n), 0)
    jj = lax.broadcasted_iota(jnp.int32, (n, n), 1)
    return ii >= jj, ii > jj, ii == jj


def _neumann_inv_raw(m):
    n = m.shape[0]
    _, _, eye = _tri_masks(n)
    eye = jnp.where(eye, 1.0, 0.0)
    p = eye + m
    mk = m
    for _ in range(int(math.log2(n)) - 1):
        mk = _raw_dot(mk, mk, "nn", False)
        p = p + _raw_dot(p, mk, "nn", False)
    resid = eye - p + _raw_dot(m, p, "nn", True)
    return p + _raw_dot(p, resid, "nn", False)


@jax.custom_vjp
def _neumann_inv(m):
    return _neumann_inv_raw(m)


def _neumann_inv_fwd(m):
    p = _neumann_inv_raw(m)
    return p, p


def _neumann_inv_bwd(p, g):
    return (_raw_dot(_raw_dot(p, g, "tn", False), p, "nt", False),)


_neumann_inv.defvjp(_neumann_inv_fwd, _neumann_inv_bwd)


@jax.custom_vjp
def _saved_inv(m, p):
    return p


_saved_inv.defvjp(lambda m, p: (p, p), lambda p, g: (_neumann_inv_bwd(p, g)[0], jnp.zeros_like(p)))


def _inverse(m, saved):
    return _neumann_inv(m) if saved is None else _saved_inv(m, saved)


def _cumsum_rows(x):
    causal, _, _ = _tri_masks(x.shape[0])
    return mm(jnp.where(causal, 1.0, 0.0), x, "nn", True)


def _gdn_group(s0, q, k, v, gb, bb, gc, *saved):
    diff = jnp.stack([gc[j] - gc[j].T for j in range(gc.shape[0])])
    return jax.vmap(_gdn_chunk)(s0, q, k, v, gb, bb, gc, diff, *saved)


def _rw_group(*args):
    return jax.vmap(_rw_chunk)(*args)


def _gdn_chunk(s0, q, k, v, gb, bb, gc, diff, saved=None):
    c = q.shape[0]
    causal, strict, _ = _tri_masks(c)
    decay = jnp.exp(jnp.where(causal, diff, -jnp.inf))
    kb = k * bb
    a = jnp.where(strict, mm(kb, k, "nt") * decay, 0.0)
    p = _inverse(-a, saved)
    u = mm(p, v * bb)
    w = mm(p, kb * jnp.exp(gc))
    attn = mm(q, k, "nt") * decay
    v_new = u - mm(w, s0)
    o = mm(q * jnp.exp(gc), s0) + mm(attn, v_new)
    g_last = jnp.sum(gb, axis=0, keepdims=True)
    s1 = s0 * jnp.exp(g_last) + mm(k * jnp.exp(g_last - gc), v_new, "tn")
    return o, s1, p


def _rw_chunk(s0, r, lw, k, v, al, be, gc, saved0=None, saved1=None):
    c = r.shape[0]
    causal, strict, _ = _tri_masks(c)
    gp = gc - lw
    row = lax.broadcasted_iota(jnp.int32, lw.shape, 0)
    lane = lax.broadcasted_iota(jnp.int32, lw.shape, 1)
    g_mid = jnp.sum(jnp.where(row < c // 2, lw, 0.0), axis=0, keepdims=True)
    g_last = jnp.sum(lw, axis=0, keepdims=True)
    e_n = jnp.exp(g_mid - gc)
    rg = r * jnp.exp(gc - g_mid)
    bg = be * jnp.exp(gp - g_mid)
    an = al * e_n
    kn = k * e_n
    bt = mm(be * jnp.exp(gp), s0, "nt")
    rt = mm(r * jnp.exp(gc), s0, "nt")
    us, ys, ps = [], [], []
    for h, saved in enumerate((saved0, saved1)):
        mine = (lane >= RW_HEAD) if h else (lane < RW_HEAD)
        bgh = jnp.where(mine, bg, 0.0)
        rgh = jnp.where(mine, rg, 0.0)
        a_ab = jnp.where(strict, mm(bgh, an, "nt"), 0.0)
        a_kb = jnp.where(strict, mm(bgh, kn, "nt"), 0.0)
        a_ra = jnp.where(causal, mm(rgh, an, "nt"), 0.0)
        a_rk = jnp.where(causal, mm(rgh, kn, "nt"), 0.0)
        p = _inverse(a_ab, saved)
        ps.append(p)
        u_h = mm(p, bt + mm(a_kb, v))
        us.append(u_h)
        ys.append(rt + mm(a_ra, u_h) + mm(a_rk, v))
    lo = lane < RW_HEAD
    u = jnp.where(lo, us[0], us[1])
    y = jnp.where(lo, ys[0], ys[1])
    tail = jnp.exp(g_last - gc)
    s1 = s0 * jnp.exp(g_last) + mm(u, al * tail, "tn") + mm(v, k * tail, "tn")
    vi = lax.broadcasted_iota(jnp.int32, s0.shape, 0)
    ki = lax.broadcasted_iota(jnp.int32, s0.shape, 1)
    s1 = jnp.where((vi < RW_HEAD) == (ki < RW_HEAD), s1, 0.0)
    return y, s1, ps[0], ps[1]


SCAN_HB = 8


def _scan_specs(arrs, n_chunks, reverse):
    def spec(off):
        assert off % SCAN_HB == 0
        if reverse:
            return pl.BlockSpec((CHUNK, SCAN_HB * LANES), lambda h, n: (n_chunks - 1 - n, off // SCAN_HB + h))
        return pl.BlockSpec((CHUNK, SCAN_HB * LANES), lambda h, n: (n, off // SCAN_HB + h))
    return [spec(off) for _, off in arrs]


def _split_heads(x):
    return jnp.stack([x[:, LANES * j:LANES * (j + 1)] for j in range(SCAN_HB)], axis=0)


def _merge_heads(x):
    return jnp.concatenate([x[j] for j in range(SCAN_HB)], axis=1)


def _scan_fwd(group_fn, name, arrs, heads, n_kept):
    s = arrs[0][0].shape[0]
    n_chunks = s // CHUNK
    n_in = len(arrs)

    def body(*refs):
        y_ref, st_ref = refs[n_in:n_in + 2]
        kept_refs, s_scr = refs[n_in + 2:-1], refs[-1]

        @pl.when(pl.program_id(1) == 0)
        def _():
            s_scr[...] = jnp.zeros_like(s_scr)

        s0 = s_scr[...]
        st_ref[...] = s0
        y, s1, *kept = group_fn(s0, *[_split_heads(r[...]) for r in refs[:n_in]])
        y_ref[...] = _merge_heads(y)
        s_scr[...] = s1
        for ref, val in zip(kept_refs, kept):
            ref[...] = val

    per_chunk = pl.BlockSpec((SCAN_HB, None, LANES, LANES), lambda h, n: (h, n, 0, 0))
    res = pl.pallas_call(
        body, grid=(heads // SCAN_HB, n_chunks), name=name,
        in_specs=_scan_specs(arrs, n_chunks, False),
        out_specs=[pl.BlockSpec((CHUNK, SCAN_HB * LANES), lambda h, n: (n, h))] + [per_chunk] * (1 + n_kept),
        out_shape=[SDS((s, heads * LANES), F32)] + [SDS((heads, n_chunks, LANES, LANES), F32)] * (1 + n_kept),
        scratch_shapes=[pltpu.VMEM((SCAN_HB, LANES, LANES), F32)],
        compiler_params=pltpu.CompilerParams(dimension_semantics=("arbitrary", "arbitrary")),
    )(*[a for a, _ in arrs])
    return res[0], res[1:]


def _scan_bwd(group_fn, name, arrs, kept, dy, heads):
    s = arrs[0][0].shape[0]
    n_chunks = s // CHUNK
    n_in, n_kept = len(arrs), len(kept)

    def body(*refs):
        kept_vals = [r[...] for r in refs[n_in:n_in + n_kept]]
        dy_ref = refs[n_in + n_kept]
        d_refs = refs[n_in + n_kept + 1:2 * n_in + n_kept + 1]
        ds_scr = refs[-1]

        @pl.when(pl.program_id(1) == 0)
        def _():
            ds_scr[...] = jnp.zeros_like(ds_scr)

        def fn(s0, *ins):
            return group_fn(s0, *ins, *kept_vals[1:])[:2]

        _, vjp = jax.vjp(fn, kept_vals[0], *[_split_heads(r[...]) for r in refs[:n_in]])
        grads = vjp((_split_heads(dy_ref[...]), ds_scr[...]))
        ds_scr[...] = grads[0]
        for ref, g in zip(d_refs, grads[1:]):
            ref[...] = _merge_heads(g)

    rev = pl.BlockSpec((CHUNK, SCAN_HB * LANES), lambda h, n: (n_chunks - 1 - n, h))
    per_chunk = pl.BlockSpec((SCAN_HB, None, LANES, LANES), lambda h, n: (h, n_chunks - 1 - n, 0, 0))
    return pl.pallas_call(
        body, grid=(heads // SCAN_HB, n_chunks), name=name,
        in_specs=_scan_specs(arrs, n_chunks, True) + [per_chunk] * n_kept + [rev],
        out_specs=[rev] * n_in,
        out_shape=[SDS((s, heads * LANES), F32)] * n_in,
        scratch_shapes=[pltpu.VMEM((SCAN_HB, LANES, LANES), F32)],
        compiler_params=pltpu.CompilerParams(dimension_semantics=("arbitrary", "arbitrary")),
    )(*[a for a, _ in arrs], *kept, dy)


def _col_spec(tr, width, cb):
    return pl.BlockSpec((tr, width), lambda i: (i, cb))


def _whole(p):
    return pl.BlockSpec(p.shape, lambda i: (0,) * p.ndim)


def _row_fwd(fn, name, tiles, params, outs, tr):
    rows = tiles[0][0].shape[0]
    nt, npar = len(tiles), len(params)

    def body(*refs):
        vals = [r[...].astype(F32) for r in refs[:nt + npar]]
        for ref, o in zip(refs[nt + npar:], fn(*vals)):
            ref[...] = o.astype(ref.dtype)

    return pl.pallas_call(
        body, grid=(rows // tr,), name=name,
        in_specs=[_col_spec(tr, w, cb) for _, w, cb in tiles] + [_whole(p) for p in params],
        out_specs=[_col_spec(tr, w, 0) for w, _ in outs],
        out_shape=[SDS((rows, w), dt) for w, dt in outs],
        compiler_params=pltpu.CompilerParams(dimension_semantics=("arbitrary",), vmem_limit_bytes=VMEM_LIMIT),
    )(*[a for a, _, _ in tiles], *params)


def _row_bwd(fn, name, tiles, params, cts, tr, want_tiles=None):
    rows = tiles[0][0].shape[0]
    nt, npar = len(tiles), len(params)
    want = list(range(nt)) if want_tiles is None else list(want_tiles)
    flat_cts = [c for group in cts for c in group]
    n_ct = len(flat_cts)

    def body(*refs):
        vals = [r[...].astype(F32) for r in refs[:nt + npar]]
        ct_refs = refs[nt + npar:nt + npar + n_ct]
        out_refs = refs[nt + npar + n_ct:]
        ct_vals, at = [], 0
        for group in cts:
            total = ct_refs[at][...].astype(F32)
            for r in ct_refs[at + 1:at + len(group)]:
                total = total + r[...].astype(F32)
            ct_vals.append(total)
            at += len(group)
        _, vjp = jax.vjp(lambda *a: tuple(fn(*a)), *vals)
        grads = vjp(tuple(ct_vals))
        for ref, t in zip(out_refs[:len(want)], want):
            ref[...] = grads[t]
        first = pl.program_id(0) == 0
        for ref, g in zip(out_refs[len(want):], grads[nt:]):
            @pl.when(first)
            def _(ref=ref, g=g):
                ref[...] = g

            @pl.when(jnp.logical_not(first))
            def _(ref=ref, g=g):
                ref[...] += g

    res = pl.pallas_call(
        body, grid=(rows // tr,), name=name,
        in_specs=[_col_spec(tr, w, cb) for _, w, cb in tiles] + [_whole(p) for p in params]
        + [_col_spec(tr, w, cb) for _, w, cb in flat_cts],
        out_specs=[_col_spec(tr, tiles[t][1], 0) for t in want] + [_whole(p) for p in params],
        out_shape=[SDS((rows, tiles[t][1]), F32) for t in want] + [SDS(p.shape, F32) for p in params],
        compiler_params=pltpu.CompilerParams(dimension_semantics=("arbitrary",), vmem_limit_bytes=VMEM_LIMIT),
    )(*[a for a, _, _ in tiles], *params, *[a for a, _, _ in flat_cts])
    return res[:len(want)], res[len(want):]


def _col_fwd(fn, name, x, first_block, n_blocks, params):
    rows = x.shape[0]

    def body(*refs):
        refs[-1][...] = fn(*[r[...] for r in refs[:-1]])

    return pl.pallas_call(
        body, grid=(n_blocks,), name=name,
        in_specs=[pl.BlockSpec((rows, LANES), lambda j: (0, first_block + j))]
        + [pl.BlockSpec((p.shape[0], LANES), lambda j: (0, j)) for p in params],
        out_specs=pl.BlockSpec((rows, LANES), lambda j: (0, j)),
        out_shape=SDS((rows, n_blocks * LANES), F32),
        compiler_params=pltpu.CompilerParams(dimension_semantics=("arbitrary",), vmem_limit_bytes=VMEM_LIMIT),
    )(x, *params)


def _col_bwd(fn, name, x, first_block, n_blocks, params, dys):
    rows = x.shape[0]
    npar, nd = len(params), len(dys)
    starts = [sum(t.shape[1] for t in dys[:i]) // LANES for i in range(nd + 1)]

    def body(*refs):
        vals = [r[...] for r in refs[:1 + npar]]
        j = pl.program_id(0)
        dy = refs[1 + npar][...]
        for i in range(1, nd):
            dy = jnp.where(j >= starts[i], refs[1 + npar + i][...], dy)
        _, vjp = jax.vjp(fn, *vals)
        grads = vjp(dy)
        for ref, g in zip(refs[1 + npar + nd:], grads):
            ref[...] = g.astype(ref.dtype)

    def piece(i):
        last = starts[i + 1] - starts[i] - 1
        return pl.BlockSpec((rows, LANES), lambda j: (0, jnp.clip(j - starts[i], 0, last)))

    pspecs = [pl.BlockSpec((p.shape[0], LANES), lambda j: (0, j)) for p in params]
    blk = pl.BlockSpec((rows, LANES), lambda j: (0, j))
    res = pl.pallas_call(
        body, grid=(n_blocks,), name=name,
        in_specs=[pl.BlockSpec((rows, LANES), lambda j: (0, first_block + j))] + pspecs + [piece(i) for i in range(nd)],
        out_specs=[blk] + pspecs,
        out_shape=[SDS((rows, n_blocks * LANES), BF16)] + [SDS(p.shape, F32) for p in params],
        compiler_params=pltpu.CompilerParams(dimension_semantics=("arbitrary",), vmem_limit_bytes=VMEM_LIMIT),
    )(x, *params, *dys)
    return res[0], res[1:]


def _conv_fn(x, w):
    acc = x * w[3:4, :]
    for j in range(3):
        acc = acc + shift_rows(x, 3 - j) * w[j:j + 1, :]
    return _silu(acc)


def _lerp_fn(x, mu):
    return x + (shift_rows(x, 1) - x) * mu[0:1, :]


def _seg_sum(x, width):
    if width == LANES:
        return jnp.sum(x, axis=1, keepdims=True)
    lo = lax.broadcasted_iota(jnp.int32, x.shape, 1) < width
    s0 = jnp.sum(jnp.where(lo, x, 0.0), axis=1, keepdims=True)
    s1 = jnp.sum(jnp.where(lo, 0.0, x), axis=1, keepdims=True)
    return jnp.where(lo, s0, s1)


def _per_block(fn, *xs):
    n = xs[0].shape[1] // LANES
    return jnp.concatenate([fn(*[x[:, LANES * b:LANES * (b + 1)] for x in xs]) for b in range(n)], axis=1)


def _head_expand(col0):
    r = lax.broadcasted_iota(jnp.int32, (LANES, DN_WIDTH), 0)
    c = lax.shift_right_logical(lax.broadcasted_iota(jnp.int32, (LANES, DN_WIDTH), 1), 7)
    return jnp.where(r == c + col0, 1.0, 0.0)


def _dn_pre_fn(cq, ck, gates, a_log, dt_bias):
    l2 = lambda x: x * lax.rsqrt(_seg_sum(x * x, LANES) + 1e-6)
    qh = _per_block(l2, cq) * (LANES ** -0.5)
    kh = _per_block(l2, ck)
    g = -jnp.exp(a_log) * _softplus(gates + dt_bias)
    gb = mm(g, _head_expand(0), "nn", True)
    bb = mm(_sigmoid(gates), _head_expand(DN_HEADS), "nn", True)
    return qh, kh, gb, bb, _cumsum_rows(gb)


def _dn_post_fn(o, z, nw):
    def one(ob, zb):
        return ob * lax.rsqrt(_seg_sum(ob * ob, LANES) * (1.0 / LANES) + RMS_EPS) * nw * _silu(zb)
    return (_per_block(one, o, z),)


def _rw_pre_fn(pr, pk, pv, pwa, pg, w0, a0, k_k, k_a, w2p, a2p, g2):
    log_w = -_softplus(-(w0 + mm(jnp.tanh(pwa), w2p))) - 0.5
    lw = -jnp.exp(log_w)
    a = _sigmoid(a0 + mm(pwa, a2p))
    gate = mm(_sigmoid(pg), g2)
    kk = pk * k_k
    kk = _per_block(lambda x: x / jnp.maximum(jnp.sqrt(_seg_sum(x * x, RW_HEAD)), 1e-12), kk)
    k = pk * (1.0 + (a - 1.0) * k_a)
    return pr, lw, k, pv, kk * a, -kk, gate, _cumsum_rows(lw)


def _rw_post_fn(y, r, k, v, gate, ln_w, ln_b, r_k):
    def one(yb, rb, kb, vb, gb, wb, bb, rkb):
        d = yb - _seg_sum(yb, RW_HEAD) * (1.0 / RW_HEAD)
        var = _seg_sum(d * d, RW_HEAD) * (1.0 / RW_HEAD)
        yn = d * lax.rsqrt(var + RW_GN_EPS) * wb + bb
        return (yn + _seg_sum(rb * kb * rkb, RW_HEAD) * vb) * gb
    return (_per_block(one, y, r, k, v, gate, ln_w, ln_b, r_k),)


def _rms_fn(h, w):
    return (h * lax.rsqrt(jnp.mean(h * h, axis=1, keepdims=True) + RMS_EPS) * w,)


def _xattn_fn(q, k, v):
    outs = []
    for h in range(XA_HEADS):
        sl = slice(LANES * h, LANES * (h + 1))
        s = mm(q[:, sl], k[:, sl], "nt") * (LANES ** -0.5)
        e = jnp.exp(s - jnp.max(s, axis=1, keepdims=True))
        outs.append(mm(e / jnp.sum(e, axis=1, keepdims=True), v[:, sl]))
    return (jnp.concatenate(outs, axis=1),)


def _fit(tile, dim):
    best = [t for t in range(LANES, min(tile, dim) + 1, LANES) if dim % t == 0]
    assert best, (tile, dim)
    return best[-1]


def _matmul(name, a, b, mode, out_dtypes, epilogue=None, extras=(), tm=1024, tn=1024, tk=2048, after=None):
    if mode == "tn":
        (k_dim, m), n = a.shape, b.shape[1]
    else:
        (m, k_dim), n = a.shape, (b.shape[1] if mode == "nn" else b.shape[0])
    tm, tn, tk = _fit(tm, m), _fit(tn, n), _fit(tk, k_dim)
    nk = k_dim // tk
    a_spec = (pl.BlockSpec((tk, tm), lambda i, j, k: (k, i)) if mode == "tn"
              else pl.BlockSpec((tm, tk), lambda i, j, k: (i, k)))
    b_spec = (pl.BlockSpec((tn, tk), lambda i, j, k: (j, k)) if mode == "nt"
              else pl.BlockSpec((tk, tn), lambda i, j, k: (k, j)))
    o_spec = pl.BlockSpec((tm, tn), lambda i, j, k: (i, j))
    n_ex, n_out = len(extras), len(out_dtypes)
    ties = [] if after is None else [after]

    def finish(total, rest):
        ex = [r[...].astype(F32) for r in rest[:n_ex]]
        res = epilogue(total, *ex) if epilogue else (total,)
        for ref, o in zip(rest[n_ex + len(ties):n_ex + len(ties) + n_out], res):
            ref[...] = o.astype(ref.dtype)

    def body_single(a_ref, b_ref, *rest):
        finish(_raw_dot(a_ref[...], b_ref[...], mode, False), rest)

    def body_acc(a_ref, b_ref, *rest):
        acc = rest[-1]
        k = pl.program_id(2)

        @pl.when(k == 0)
        def _():
            acc[...] = jnp.zeros_like(acc)

        acc[...] += _raw_dot(a_ref[...], b_ref[...], mode, False)

        @pl.when(k == nk - 1)
        def _():
            finish(acc[...], rest)

    res = pl.pallas_call(
        body_single if nk == 1 else body_acc, grid=(m // tm, n // tn, nk), name=name,
        in_specs=[a_spec, b_spec] + [o_spec] * n_ex + [pl.BlockSpec((8, LANES), lambda i, j, k: (0, 0))] * len(ties),
        out_specs=[o_spec] * n_out,
        out_shape=[SDS((m, n), dt) for dt in out_dtypes],
        scratch_shapes=[] if nk == 1 else [pltpu.VMEM((tm, tn), F32)],
        compiler_params=pltpu.CompilerParams(dimension_semantics=("parallel", "parallel", "arbitrary"),
                                             vmem_limit_bytes=VMEM_LIMIT),
    )(a, b, *extras, *ties)
    return res


def _matmul_norm_bwd(name, a, b, mode, h, w, dres, after=None, tm=512, tk=1024):
    m, n = h.shape
    k_dim = a.shape[1]
    tm, tk = _fit(tm, m), _fit(tk, k_dim)
    nk = k_dim // tk
    ties = [] if after is None else [after]
    a_spec = pl.BlockSpec((tm, tk), lambda i, k: (i, k))
    b_spec = pl.BlockSpec((n, tk), lambda i, k: (0, k)) if mode == "nt" else pl.BlockSpec((tk, n), lambda i, k: (k, 0))
    row = pl.BlockSpec((tm, n), lambda i, k: (i, 0))
    w_spec = pl.BlockSpec((1, n), lambda i, k: (0, 0))

    def body(a_ref, b_ref, h_ref, w_ref, dres_ref, *rest):
        dh_ref, dw_ref, acc = rest[len(ties):]
        i, k = pl.program_id(0), pl.program_id(1)

        @pl.when(k == 0)
        def _():
            acc[...] = jnp.zeros_like(acc)

        acc[...] += _raw_dot(a_ref[...], b_ref[...], mode, False)

        @pl.when(k == nk - 1)
        def _():
            _, vjp = jax.vjp(_rms_res_fn, h_ref[...], w_ref[...])
            dh, dw = vjp((acc[...], dres_ref[...]))
            dh_ref[...] = dh

            @pl.when(i == 0)
            def _():
                dw_ref[...] = dw

            @pl.when(i != 0)
            def _():
                dw_ref[...] += dw

    return pl.pallas_call(
        body, grid=(m // tm, nk), name=name,
        in_specs=[a_spec, b_spec, row, w_spec, row] + [pl.BlockSpec((8, LANES), lambda i, k: (0, 0))] * len(ties),
        out_specs=[row, w_spec],
        out_shape=[SDS((m, n), F32), SDS((1, n), F32)],
        scratch_shapes=[pltpu.VMEM((tm, n), F32)],
        compiler_params=pltpu.CompilerParams(dimension_semantics=("arbitrary", "arbitrary"),
                                             vmem_limit_bytes=VMEM_LIMIT),
    )(a, b, h, w, dres, *ties)


def _loss_call(h, target, w, tr=256):
    rows, d = h.shape

    def fn(hv, wv, tv):
        y = _rms_fn(hv, wv)[0]
        return 0.5 * jnp.sum(jnp.mean(jnp.square(y - tv), axis=1, keepdims=True), axis=0, keepdims=True)

    def body(h_ref, t_ref, w_ref, loss_ref, dh_ref, dw_ref):
        tv = t_ref[...]
        val, vjp = jax.vjp(lambda hv, wv: fn(hv, wv, tv), h_ref[...], w_ref[...])
        dh, dw = vjp(jnp.ones((1, 1), F32))
        dh_ref[...] = dh
        first = pl.program_id(0) == 0

        @pl.when(first)
        def _():
            loss_ref[...] = jnp.broadcast_to(val, loss_ref.shape)
            dw_ref[...] = dw

        @pl.when(jnp.logical_not(first))
        def _():
            loss_ref[...] += jnp.broadcast_to(val, loss_ref.shape)
            dw_ref[...] += dw

    return pl.pallas_call(
        body, grid=(rows // tr,), name="loss_head",
        in_specs=[_col_spec(tr, d, 0), _col_spec(tr, d, 0), _whole(w)],
        out_specs=[pl.BlockSpec((8, LANES), lambda i: (0, 0)), _col_spec(tr, d, 0), _whole(w)],
        out_shape=[SDS((8, LANES), F32), SDS((rows, d), F32), SDS(w.shape, F32)],
        compiler_params=pltpu.CompilerParams(dimension_semantics=("arbitrary",), vmem_limit_bytes=VMEM_LIMIT),
    )(h, target, w)


def _adamw_vals(w, g, m, v):
    m = ADAM_B1 * m + (1.0 - ADAM_B1) * g
    v = ADAM_B2 * v + (1.0 - ADAM_B2) * jnp.square(g)
    m_hat = m / (1.0 - ADAM_B1 ** ADAM_STEP)
    v_hat = v / (1.0 - ADAM_B2 ** ADAM_STEP)
    delta = -ADAM_LR * (m_hat / (jnp.sqrt(v_hat) + ADAM_EPS) + ADAM_WD * w)
    return delta, m, v


def _sum_adamw(name, parts, w, m, v):
    r, c = w.shape
    n_parts = parts.shape[0]
    budget = 6 * 1024 * 1024
    tr, tc = r, c
    for cand in (512, 256, 128, 64, 32, 16, 8):
        if r % cand == 0 and n_parts * cand * c * 4 <= budget:
            tr = cand
            break
    if n_parts * tr * c * 4 > budget:
        tc = max(t for t in range(LANES, c + 1, LANES) if c % t == 0 and n_parts * r * t * 4 <= budget)

    def body(p_ref, w_ref, m_ref, v_ref, g_ref, d_ref, m2_ref, v2_ref):
        g = p_ref[0].astype(F32)
        for s in range(1, n_parts):
            g = g + p_ref[s].astype(F32)
        g_ref[...] = g
        d_ref[...], m2_ref[...], v2_ref[...] = _adamw_vals(w_ref[...], g, m_ref[...], v_ref[...])

    blk = pl.BlockSpec((tr, tc), lambda i: (i, 0)) if tc == c else pl.BlockSpec((tr, tc), lambda i: (0, i))
    parts_blk = (pl.BlockSpec((n_parts, tr, tc), lambda i: (0, i, 0)) if tc == c
                 else pl.BlockSpec((n_parts, tr, tc), lambda i: (0, 0, i)))
    return pl.pallas_call(
        body, grid=(r // tr if tc == c else c // tc,), name=name,
        in_specs=[parts_blk, blk, blk, blk],
        out_specs=[blk] * 4, out_shape=[SDS((r, c), F32)] * 4,
        compiler_params=pltpu.CompilerParams(dimension_semantics=("arbitrary",), vmem_limit_bytes=VMEM_LIMIT),
    )(parts, w, m, v)


def _peers():
    x, y, c = lax.axis_index("x"), lax.axis_index("y"), lax.axis_index("c")
    peers = []
    for k in range(1, N_DEV):
        px = 1 - x if k & 4 else x
        py = 1 - y if k & 2 else y
        pc = 1 - c if k & 1 else c
        peers.append(((px, py, pc), 4 * px + 2 * py + pc))
    return 4 * x + 2 * y + c, peers


def _slot(ref, idx, cols):
    if cols is None:
        return ref.at[idx]
    return ref.at[:, pl.ds(pl.multiple_of(idx * cols, LANES), cols)]


def _gather_two_level(name, srcs, dsts):
    n = len(srcs)
    dst_cols = [c for _, _, c in dsts]

    def body(*refs):
        src_refs, out_refs = refs[:n], refs[n:2 * n]
        send_sems, recv_sems, local_sems = refs[2 * n:]
        x, y, c = lax.axis_index("x"), lax.axis_index("y"), lax.axis_index("c")
        index = lambda px, py, pc: 4 * px + 2 * py + pc
        me, sibling = index(x, y, c), (x, y, 1 - c)
        chips = [(x, 1 - y), (1 - x, y), (1 - x, 1 - y)]

        def copy(a, k, src, block, to):
            return pltpu.make_async_remote_copy(
                src_ref=src, dst_ref=_slot(out_refs[a], block, dst_cols[a]),
                send_sem=send_sems.at[a, k], recv_sem=recv_sems.at[a, k],
                device_id=to, device_id_type=pl.DeviceIdType.MESH)

        local, first, passed = [], [], []
        for a in range(n):
            cp = pltpu.make_async_copy(src_refs[a], _slot(out_refs[a], me, dst_cols[a]), local_sems.at[a])
            cp.start()
            local.append(cp)
            first.append(copy(a, 0, src_refs[a], me, sibling))
            first += [copy(a, 1 + j, src_refs[a], me, (*chip, c)) for j, chip in enumerate(chips)]
        for cp in first:
            cp.start()
        for a in range(n):
            for j, chip in enumerate(chips):
                block = index(*chip, c)
                arrived = _slot(out_refs[a], block, dst_cols[a])
                copy(a, 1 + j, arrived, block, (*chip, c)).wait_recv()
                passed.append(copy(a, 4 + j, arrived, block, sibling))
                passed[-1].start()
        for a in range(n):
            copy(a, 0, src_refs[a], index(x, y, 1 - c), sibling).wait_recv()
            for j, chip in enumerate(chips):
                block = index(*chip, 1 - c)
                copy(a, 4 + j, src_refs[a], block, sibling).wait_recv()
        for cp in first + passed:
            cp.wait_send()
        for cp in local:
            cp.wait()

    any_spec = pl.BlockSpec(memory_space=pl.ANY)
    return pl.pallas_call(
        body, name=name,
        in_specs=[any_spec] * n, out_specs=[any_spec] * n,
        out_shape=[SDS(shape, dt) for shape, dt, _ in dsts],
        scratch_shapes=_exchange_sems(n),
    )(*[a for a, _ in srcs])


_HBM = pl.BlockSpec(memory_space=pltpu.HBM)
_SEM = pl.BlockSpec(memory_space=pltpu.SEMAPHORE)
_EFFECT = pltpu.SideEffectType.DATAFLOW_SIDE_EFFECTING


def _split_copies(src_cols, dst_cols, gather, chips, src_refs, land_refs, send_sems, recv_sems, landings):
    me, peers = _peers()
    if chips:
        me, peers = me // 2, [(pos, idx // 2) for k, (pos, idx) in enumerate(peers) if (k + 1) in (2, 4, 6)]
    n, width = len(src_cols), len(peers)
    remote, local = [], []
    for a, (s_cols, d_cols) in enumerate(zip(src_cols, dst_cols)):
        mine = src_refs[a] if gather else _slot(src_refs[a], me, s_cols)
        local.append(pltpu.make_async_copy(mine, _slot(land_refs[a], me, d_cols), send_sems.at[n * width + a]))
        for k, (pos, idx) in enumerate(peers):
            blk = src_refs[a] if gather else _slot(src_refs[a], idx, s_cols)
            remote.append(pltpu.make_async_remote_copy(
                src_ref=blk, dst_ref=_slot(land_refs[a], idx if landings else me, d_cols),
                send_sem=send_sems.at[a * width + k], recv_sem=recv_sems.at[a * width + k],
                device_id=pos, device_id_type=pl.DeviceIdType.MESH))
    return remote, local


def _exchange_start(name, srcs, dsts, gather, after, chips=False):
    n = len(srcs)
    src_cols, dst_cols = [c for _, c in srcs], [c for _, _, c in dsts]
    width = 3 if chips else N_DEV - 1

    def body(*refs):
        src_refs, land_refs = refs[:n], refs[n:2 * n]
        send_sems, recv_sems = refs[2 * n + 1:2 * n + 3]
        token = refs[-1]
        remote, local = _split_copies(src_cols, dst_cols, gather, chips, src_refs, land_refs, send_sems, recv_sems,
                                      False)
        for cp in remote + local:
            cp.start()
        token[...] = jnp.zeros_like(token)

    hbm = lambda a: pltpu.with_memory_space_constraint(a, pltpu.HBM)
    lands = [hbm(lax.empty(shape, dt)) for shape, dt, _ in dsts]
    res = pl.pallas_call(
        body, name=name,
        out_shape=(pltpu.SemaphoreType.DMA((n * (width + 1),)), pltpu.SemaphoreType.DMA((n * width,)),
                   *[pltpu.HBM(a.shape, a.dtype) for a, _ in srcs], *[pltpu.HBM(a.shape, a.dtype) for a in lands],
                   SDS((8, LANES), F32)),
        in_specs=[_HBM] * (2 * n) + [pl.BlockSpec(memory_space=pl.ANY)],
        out_specs=(_SEM, _SEM, *[_HBM] * (2 * n), pl.BlockSpec(memory_space=pltpu.VMEM)),
        input_output_aliases={i: 2 + i for i in range(2 * n)},
        compiler_params=pltpu.CompilerParams(has_side_effects=_EFFECT),
    )(*[hbm(a) for a, _ in srcs], *lands, after)
    handle = (res[0], res[1], res[2:2 + n], res[2 + n:2 + 2 * n], src_cols, dst_cols, gather, chips)
    return handle, res[-1]


def _exchange_wait(name, handle, after):
    send_sems, recv_sems, src_thru, land_thru, src_cols, dst_cols, gather, chips = handle
    n = len(src_thru)

    def body(*refs):
        src_refs, land_refs = refs[:n], refs[n:2 * n]
        s_sems, r_sems = refs[2 * n:2 * n + 2]
        remote, local = _split_copies(src_cols, dst_cols, gather, chips, src_refs, land_refs, s_sems, r_sems, True)
        for cp in remote:
            cp.wait_send()
            cp.wait_recv()
        for cp in local:
            cp.wait()

    res = pl.pallas_call(
        body, name=name,
        out_shape=tuple(pltpu.HBM(a.shape, a.dtype) for a in (*src_thru, *land_thru)),
        in_specs=[_HBM] * (2 * n) + [_SEM, _SEM, pl.BlockSpec(memory_space=pl.ANY)],
        out_specs=tuple([_HBM] * (2 * n)),
        input_output_aliases={i: i for i in range(2 * n)},
        compiler_params=pltpu.CompilerParams(has_side_effects=_EFFECT),
    )(*src_thru, *land_thru, send_sems, recv_sems, after)
    return res[n:]


def _pair_swap(name, arrs):
    n = len(arrs)

    def body(*refs):
        src_refs, out_refs = refs[:n], refs[n:2 * n]
        send_sems, recv_sems = refs[2 * n:]
        x, y, c = lax.axis_index("x"), lax.axis_index("y"), lax.axis_index("c")
        copies = [pltpu.make_async_remote_copy(
            src_ref=src_refs[a].at[:, 1 - c], dst_ref=out_refs[a], send_sem=send_sems.at[a], recv_sem=recv_sems.at[a],
            device_id=(x, y, 1 - c), device_id_type=pl.DeviceIdType.MESH) for a in range(n)]
        for cp in copies:
            cp.start()
        for cp in copies:
            cp.wait()

    any_spec = pl.BlockSpec(memory_space=pl.ANY)
    return pl.pallas_call(
        body, name=name,
        in_specs=[any_spec] * n, out_specs=[any_spec] * n,
        out_shape=[SDS((a.shape[0],) + a.shape[2:], a.dtype) for a in arrs],
        scratch_shapes=[pltpu.SemaphoreType.DMA((n,)), pltpu.SemaphoreType.DMA((n,))],
    )(*arrs)


def _pair_add(name, mine, theirs):
    four, _, r, c = mine.shape
    tr = r
    for cand in (512, 256, 128, 64, 32, 16, 8):
        if r % cand == 0:
            tr = cand
            break
    tc = max(t for t in range(LANES, c + 1, LANES) if c % t == 0 and (t == LANES or 2 * tr * t * 4 <= 4 * 1024 * 1024))

    def body(m_ref, t_ref, o_ref):
        core = lax.axis_index("c")
        both = m_ref[...].astype(F32)
        own = jnp.where(core == 0, both[0], both[1])
        o_ref[...] = (own + t_ref[...].astype(F32)).astype(o_ref.dtype)

    return pl.pallas_call(
        body, grid=(four, r // tr, c // tc), name=name,
        in_specs=[pl.BlockSpec((None, 2, tr, tc), lambda i, j, k: (i, 0, j, k)),
                  pl.BlockSpec((None, tr, tc), lambda i, j, k: (i, j, k))],
        out_specs=pl.BlockSpec((None, tr, tc), lambda i, j, k: (i, j, k)),
        out_shape=SDS(theirs.shape, theirs.dtype),
        compiler_params=pltpu.CompilerParams(dimension_semantics=("arbitrary",) * 3, vmem_limit_bytes=VMEM_LIMIT),
    )(mine, theirs)


def _my_index():
    return 4 * lax.axis_index("x") + 2 * lax.axis_index("y") + lax.axis_index("c")


def _two_level_copies(stage, dst_cols, src_refs, land_refs, send_sems, recv_sems, landings):
    x, y, c = lax.axis_index("x"), lax.axis_index("y"), lax.axis_index("c")

    def pos(k):
        return (1 - x if k & 4 else x, 1 - y if k & 2 else y, 1 - c if k & 1 else c)

    def idx(k):
        px, py, pc = pos(k)
        return 4 * px + 2 * py + pc

    out = []
    for a, cols in enumerate(dst_cols):
        if stage == 1:
            for i, k in enumerate((1, 2, 4, 6)):
                out.append(pltpu.make_async_remote_copy(
                    src_ref=src_refs[a], dst_ref=_slot(land_refs[a], idx(k) if landings else idx(0), cols),
                    send_sem=send_sems.at[4 * a + i], recv_sem=recv_sems.at[4 * a + i],
                    device_id=pos(k), device_id_type=pl.DeviceIdType.MESH))
        else:
            for i, k in enumerate((2, 4, 6)):
                out.append(pltpu.make_async_remote_copy(
                    src_ref=_slot(land_refs[a], idx(k), cols),
                    dst_ref=_slot(land_refs[a], idx(k ^ 1) if landings else idx(k), cols),
                    send_sem=send_sems.at[3 * a + i], recv_sem=recv_sems.at[3 * a + i],
                    device_id=pos(1), device_id_type=pl.DeviceIdType.MESH))
    return out


def _gather2_start(name, srcs, dsts, after):
    n = len(srcs)
    dst_cols = [c for _, _, c in dsts]

    def body(*refs):
        src_refs, land_refs = refs[:n], refs[n:2 * n]
        send_sems, recv_sems = refs[2 * n + 1:2 * n + 3]
        me = _my_index()
        for a in range(n):
            pltpu.make_async_copy(src_refs[a], _slot(land_refs[a], me, dst_cols[a]), send_sems.at[4 * n + a]).start()
        for cp in _two_level_copies(1, dst_cols, src_refs, land_refs, send_sems, recv_sems, False):
            cp.start()
        refs[-1][...] = jnp.zeros_like(refs[-1])

    hbm = lambda a: pltpu.with_memory_space_constraint(a, pltpu.HBM)
    lands = [hbm(lax.empty(shape, dt)) for shape, dt, _ in dsts]
    res = pl.pallas_call(
        body, name=name,
        out_shape=(pltpu.SemaphoreType.DMA((5 * n,)), pltpu.SemaphoreType.DMA((4 * n,)),
                   *[pltpu.HBM(a.shape, a.dtype) for a, _ in srcs], *[pltpu.HBM(a.shape, a.dtype) for a in lands],
                   SDS((8, LANES), F32)),
        in_specs=[_HBM] * (2 * n) + [pl.BlockSpec(memory_space=pl.ANY)],
        out_specs=(_SEM, _SEM, *[_HBM] * (2 * n), pl.BlockSpec(memory_space=pltpu.VMEM)),
        input_output_aliases={i: 2 + i for i in range(2 * n)},
        compiler_params=pltpu.CompilerParams(has_side_effects=_EFFECT),
    )(*[hbm(a) for a, _ in srcs], *lands, after)
    return (res[0], res[1], res[2:2 + n], res[2 + n:2 + 2 * n], dst_cols), res[-1]


def _gather2_pass(name, handle, after):
    send1, recv1, src_thru, land_thru, dst_cols = handle
    n = len(src_thru)

    def body(*refs):
        src_refs, land_refs = refs[:n], refs[n:2 * n]
        s1, r1 = refs[2 * n:2 * n + 2]
        send2, recv2 = refs[2 * n + 3:2 * n + 5]
        me = _my_index()
        for cp in _two_level_copies(1, dst_cols, src_refs, land_refs, s1, r1, True):
            cp.wait_send()
            cp.wait_recv()
        for a in range(n):
            pltpu.make_async_copy(src_refs[a], _slot(land_refs[a], me, dst_cols[a]), s1.at[4 * n + a]).wait()
        for cp in _two_level_copies(2, dst_cols, src_refs, land_refs, send2, recv2, False):
            cp.start()
        refs[-1][...] = jnp.zeros_like(refs[-1])

    res = pl.pallas_call(
        body, name=name,
        out_shape=(pltpu.SemaphoreType.DMA((3 * n,)), pltpu.SemaphoreType.DMA((3 * n,)),
                   *[pltpu.HBM(a.shape, a.dtype) for a in (*src_thru, *land_thru)], SDS((8, LANES), F32)),
        in_specs=[_HBM] * (2 * n) + [_SEM, _SEM, pl.BlockSpec(memory_space=pl.ANY)],
        out_specs=(_SEM, _SEM, *[_HBM] * (2 * n), pl.BlockSpec(memory_space=pltpu.VMEM)),
        input_output_aliases={i: 2 + i for i in range(2 * n)},
        compiler_params=pltpu.CompilerParams(has_side_effects=_EFFECT),
    )(*src_thru, *land_thru, send1, recv1, after)
    return (res[0], res[1], res[2:2 + n], res[2 + n:2 + 2 * n], dst_cols), res[-1]


def _gather2_wait(name, handle, after):
    send2, recv2, src_thru, land_thru, dst_cols = handle
    n = len(src_thru)

    def body(*refs):
        src_refs, land_refs = refs[:n], refs[n:2 * n]
        s2, r2 = refs[2 * n:2 * n + 2]
        for cp in _two_level_copies(2, dst_cols, src_refs, land_refs, s2, r2, True):
            cp.wait_send()
            cp.wait_recv()

    res = pl.pallas_call(
        body, name=name,
        out_shape=tuple(pltpu.HBM(a.shape, a.dtype) for a in (*src_thru, *land_thru)),
        in_specs=[_HBM] * (2 * n) + [_SEM, _SEM, pl.BlockSpec(memory_space=pl.ANY)],
        out_specs=tuple([_HBM] * (2 * n)),
        input_output_aliases={i: i for i in range(2 * n)},
        compiler_params=pltpu.CompilerParams(has_side_effects=_EFFECT),
    )(*src_thru, *land_thru, send2, recv2, after)
    return res[n:]


def _exchange_sems(n):
    return [pltpu.SemaphoreType.DMA((n, N_DEV - 1)), pltpu.SemaphoreType.DMA((n, N_DEV - 1)),
            pltpu.SemaphoreType.DMA((n,))]


def _rms_res_fn(h, w):
    return _rms_fn(h, w)[0], h


def _add_epilogue(acc, res):
    return (acc + res,)


def _gather_plan(shards):
    srcs, dsts = [], []
    for n, sh in shards.items():
        r, c = sh.shape
        srcs.append((sh, None))
        if SHARDED[n] and c % LANES == 0:
            dsts.append(((r, N_DEV * c), sh.dtype, c))
        else:
            dsts.append(((N_DEV, r, c), sh.dtype, None))
    return srcs, dsts, True


def _w_in_segments():
    out = []
    for j in range(N_DEV):
        lo, hi = W_IN_SHARD * j, W_IN_SHARD * (j + 1)
        for a, b in ((lo, min(hi, DN_COLS)), (max(lo, DN_COLS), hi)):
            if a < b:
                out.append((j, a - lo, b - lo, a if a < DN_COLS else a + RW_OFF - DN_COLS))
    return out


def _w_in_to_padded(shards, tc=512):
    _, _, cols = shards.shape

    def body(g_ref, o_ref):
        o_ref[...] = jnp.zeros_like(o_ref)
        for j, a, b, dst in _w_in_segments():
            o_ref[dst:dst + b - a, :] = g_ref[j, a:b, :]

    return pl.pallas_call(
        body, grid=(cols // tc,), name="w_in_to_padded",
        in_specs=[pl.BlockSpec((N_DEV, W_IN_SHARD, tc), lambda i: (0, 0, i))],
        out_specs=pl.BlockSpec((IN_PAD, tc), lambda i: (0, i)),
        out_shape=SDS((IN_PAD, cols), shards.dtype),
        compiler_params=pltpu.CompilerParams(dimension_semantics=("arbitrary",), vmem_limit_bytes=VMEM_LIMIT),
    )(shards)


def _w_in_grad_to_shards(gw, tc=512):
    _, cols = gw.shape

    def body(w_ref, o_ref):
        for j, a, b, dst in _w_in_segments():
            o_ref[j, a:b, :] = w_ref[dst:dst + b - a, :]

    return pl.pallas_call(
        body, grid=(cols // tc,), name="w_in_grad_to_shards",
        in_specs=[pl.BlockSpec((IN_PAD, tc), lambda i: (0, i))],
        out_specs=pl.BlockSpec((N_DEV, W_IN_SHARD, tc), lambda i: (0, 0, i)),
        out_shape=SDS((N_DEV, W_IN_SHARD, cols), gw.dtype),
        compiler_params=pltpu.CompilerParams(dimension_semantics=("arbitrary",), vmem_limit_bytes=VMEM_LIMIT),
    )(gw)


def _gather_finish(names, outs):
    full = {}
    for n, arr in zip(names, outs):
        if n == "w_in":
            full[n] = _w_in_to_padded(arr)
        elif arr.ndim == 2:
            full[n] = arr
        elif SHARDED[n]:
            full[n] = arr.transpose(1, 0, 2).reshape(arr.shape[1], -1)
        else:
            full[n] = arr.reshape(-1, arr.shape[2])
    return full


def _scatter_plan(grads):
    srcs, dsts = [], []
    for n, gr in grads.items():
        if gr.ndim == 3:
            srcs.append((gr, None))
            dsts.append((gr.shape, gr.dtype, None))
            continue
        rows, cols = gr.shape
        if not SHARDED[n]:
            r, c = rows // N_DEV, cols
            srcs.append((gr.reshape(N_DEV, r, c), None))
        else:
            r, c = rows, cols // N_DEV
            if c % LANES == 0:
                srcs.append((gr, c))
            else:
                srcs.append((gr.reshape(r, N_DEV, c).transpose(1, 0, 2), None))
        dsts.append(((N_DEV, r, c), gr.dtype, None))
    return srcs, dsts, False


def _local_step(x, mem, target, wt, late):
    d = D_MODEL
    g = {}
    wt = dict(wt)
    grp_a = ("w_out", "xa_wq", "xa_wk", "xa_wv", "xa_wo")
    grp_b = ("ffn_w1", "ffn_w2")
    plan = lambda names: _gather_plan({n: late[n] for n in names})[:2]
    handle_a, tok_a = _gather2_start("late_gather_a_start", *plan(grp_a), wt["w_in"])
    handle_w1, tok_b = _gather2_start("late_gather_w1_start", *plan(("ffn_w1",)), tok_a)
    handle_w2, tok_c = _gather2_start("late_gather_w2_start", *plan(("ffn_w2",)), tok_b)
    mix_w = wt["mix_norm_w"] + (tok_a[0:1, 0:1] + tok_b[0:1, 0:1] + tok_c[0:1, 0:1])
    u = _row_fwd(_rms_fn, "mix_norm", [(x, d, 0)], [mix_w], [(d, BF16)], 256)[0]
    p = _matmul("in_proj", u, wt["w_in"], "nt", [F32], tn=1536)[0]
    c = _col_fwd(_conv_fn, "dn_conv", p, 0, 24, [wt["dn_conv_w"]])
    handle_a, tok = _gather2_pass("late_gather_a_pass", handle_a, c)
    dn_pre_tiles = [(c, DN_WIDTH, 0), (c, DN_WIDTH, 1), (p, LANES, 32)]
    dn_pre_params = [wt["dn_a_log"], wt["dn_dt_bias"]]
    qh, kh, gb, bb, gcb = _row_fwd(_dn_pre_fn, "dn_pre", dn_pre_tiles, [dn_pre_params[0] + tok[0:1, :], dn_pre_params[1]],
                                   [(DN_WIDTH, F32)] * 5, CHUNK)
    dn_arrs = [(qh, 0), (kh, 0), (c, 16), (gb, 0), (bb, 0), (gcb, 0)]
    o, kept_dn = _scan_fwd(_gdn_group, "gdn_scan", dn_arrs, DN_HEADS, 1)
    dn_post_tiles = [(o, DN_WIDTH, 0), (p, DN_WIDTH, 3)]
    o_dn = _row_fwd(_dn_post_fn, "dn_post", dn_post_tiles, [wt["dn_norm_w"]], [(DN_WIDTH, BF16)], 256)[0]

    ps = _col_fwd(_lerp_fn, "rw_shift", p, RW_OFF // LANES, 26, [wt["rw_mu"]])
    rw_pre_tiles = [(ps, RW_WIDTH, 0), (ps, RW_WIDTH, 1), (ps, RW_WIDTH, 2), (ps, LANES, 24), (ps, LANES, 25)]
    rw_pre_params = [wt[n] for n in ("rw_w0", "rw_a0", "rw_k_k", "rw_k_a", "rw_w2", "rw_a2", "rw_g2")]
    r, lw, k, v, al, be, gate, gcw = _row_fwd(_rw_pre_fn, "rw_pre", rw_pre_tiles, rw_pre_params,
                                              [(RW_WIDTH, F32)] * 8, CHUNK)
    rw_arrs = [(r, 0), (lw, 0), (k, 0), (v, 0), (al, 0), (be, 0), (gcw, 0)]
    y, kept_rw = _scan_fwd(_rw_group, "rw_scan", rw_arrs, RW_WIDTH // LANES, 2)
    handle_w1, tok = _gather2_pass("late_gather_w1_pass", handle_w1, y)
    rw_post_tiles = [(t, RW_WIDTH, 0) for t in (y, r, k, v, gate)]
    rw_post_params = [wt["rw_ln_w"], wt["rw_ln_b"], wt["rw_r_k"]]
    o_rw = _row_fwd(_rw_post_fn, "rw_post", rw_post_tiles, [rw_post_params[0] + tok[0:1, 0:1]] + rw_post_params[1:],
                    [(RW_WIDTH, BF16)], 128)[0]
    o_cat = jnp.concatenate([o_dn, o_rw], axis=1)
    wt.update(_gather_finish(grp_a, _gather2_wait("late_gather_a_wait", handle_a, o_cat)))
    h1 = _matmul("out_proj", o_cat, wt["w_out"], "nn", [F32], _add_epilogue, (x,))[0]

    handle_w2, tok = _gather2_pass("late_gather_w2_pass", handle_w2, h1)
    hn = _row_fwd(_rms_fn, "xa_norm", [(h1, d, 0)], [wt["xa_norm_w"] + tok[0:1, 0:1]], [(d, BF16)], 256)[0]
    mn = _row_fwd(_rms_fn, "mem_norm", [(mem, d, 0)], [wt["mem_norm_w"]], [(d, BF16)], 256)[0]
    q = _matmul("xa_q", hn, wt["xa_wq"], "nn", [F32])[0]
    kx = _matmul("xa_k", mn, wt["xa_wk"], "nn", [F32])[0]
    vx = _matmul("xa_v", mn, wt["xa_wv"], "nn", [F32])[0]
    ao = _row_fwd(_xattn_fn, "xattn", [(q, XA_WIDTH, 0)], [kx, vx], [(XA_WIDTH, BF16)], 256)[0]
    h2 = _matmul("xa_o", ao, wt["xa_wo"], "nn", [F32], _add_epilogue, (h1,))[0]

    f = _row_fwd(_rms_fn, "ffn_norm", [(h2, d, 0)], [wt["ffn_norm_w"]], [(d, BF16)], 256)[0]
    wt.update(_gather_finish(("ffn_w1",), _gather2_wait("late_gather_w1_wait", handle_w1, f)))
    a, hid = _matmul("ffn_up", f, wt["ffn_w1"], "nn", [F32, BF16],
                     lambda acc: (acc, jnp.square(jnp.maximum(acc, 0.0))))
    wt.update(_gather_finish(("ffn_w2",), _gather2_wait("late_gather_w2_wait", handle_w2, hid)))
    h3 = _matmul("ffn_down", hid, wt["ffn_w2"], "nn", [F32], _add_epilogue, (h2,))[0]
    loss8, dh3, g["final_norm_w"] = _loss_call(h3, target, wt["final_norm_w"])

    da = _matmul("ffn_down_dx", dh3, wt["ffn_w2"], "nt", [BF16],
                 lambda acc, av: (acc * 2.0 * jnp.maximum(av, 0.0),), (a,))[0]
    g["ffn_w2"] = _matmul("ffn_down_dw", hid, dh3, "tn", [BF16])[0]
    g["ffn_w1"] = _matmul("ffn_up_dw", f, da, "tn", [BF16])[0]
    pending = {}
    plan = _scatter_plan({n: g.pop(n) for n in grp_b})
    pending[grp_b], tok = _exchange_start("late_grad_b_start", *plan, loss8)
    dh2, g["ffn_norm_w"] = _matmul_norm_bwd("ffn_up_dx", da, wt["ffn_w1"], "nt", h2, wt["ffn_norm_w"], dh3, tok)

    dao = _matmul("xa_o_dx", dh2, wt["xa_wo"], "nt", [F32])[0]
    g["xa_wo"] = _matmul("xa_o_dw", ao, dh2, "tn", [BF16])[0]
    (dq,), (dkx, dvx) = _row_bwd(_xattn_fn, "xattn_bwd", [(q, XA_WIDTH, 0)], [kx, vx], [[(dao, XA_WIDTH, 0)]], 256)
    dh1, g["xa_norm_w"] = _matmul_norm_bwd("xa_q_dx", dq, wt["xa_wq"], "nt", h1, wt["xa_norm_w"], dh2)
    g["xa_wq"] = _matmul("xa_q_dw", hn, dq, "tn", [BF16])[0]
    g["xa_wk"] = _matmul("xa_k_dw", mn, dkx, "tn", [BF16])[0]
    g["xa_wv"] = _matmul("xa_v_dw", mn, dvx, "tn", [BF16])[0]
    dmn = _matmul("xa_k_dx", dkx, wt["xa_wk"], "nt", [F32])[0]
    dmn = _matmul("xa_v_dx", dvx, wt["xa_wv"], "nt", [F32], _add_epilogue, (dmn,))[0]
    _, (g["mem_norm_w"],) = _row_bwd(_rms_fn, "mem_norm_bwd", [(mem, d, 0)], [wt["mem_norm_w"]],
                                     [[(dmn, d, 0)]], 256, want_tiles=())

    do_cat = _matmul("out_proj_dx", dh1, wt["w_out"], "nt", [F32])[0]
    g["w_out"] = _matmul("out_proj_dw", o_cat, dh1, "tn", [BF16])[0]

    plan = _scatter_plan({n: g.pop(n) for n in grp_a})
    pending[grp_a], tok = _exchange_start("late_grad_a_start", *plan, tok)
    (dy, dr1, dk1, dv1, dgate), (g["rw_ln_w"], g["rw_ln_b"], g["rw_r_k"]) = _row_bwd(
        _rw_post_fn, "rw_post_bwd", rw_post_tiles, [rw_post_params[0] + tok[0:1, 0:1]] + rw_post_params[1:],
        [[(do_cat, RW_WIDTH, 1)]], 128)
    dr2, dlw, dk2, dv2, dal, dbe, dgcw = _scan_bwd(_rw_group, "rw_scan_bwd", rw_arrs, kept_rw, dy,
                                                   RW_WIDTH // LANES)
    one = lambda t: [(t, RW_WIDTH, 0)]
    two = lambda s, t: [(s, RW_WIDTH, 0), (t, RW_WIDTH, 0)]
    d_ps, rw_pre_grads = _row_bwd(
        _rw_pre_fn, "rw_pre_bwd", rw_pre_tiles, rw_pre_params,
        [two(dr1, dr2), one(dlw), two(dk1, dk2), two(dv1, dv2), one(dal), one(dbe), one(dgate), one(dgcw)],
        CHUNK)
    for n, val in zip(("rw_w0", "rw_a0", "rw_k_k", "rw_k_a", "rw_w2", "rw_a2", "rw_g2"), rw_pre_grads):
        g[n] = val
    dp_rw, (g["rw_mu"],) = _col_bwd(_lerp_fn, "rw_shift_bwd", p, RW_OFF // LANES, 26, [wt["rw_mu"]], list(d_ps))

    (do, dz), (g["dn_norm_w"],) = _row_bwd(_dn_post_fn, "dn_post_bwd", dn_post_tiles, [wt["dn_norm_w"]],
                                           [[(do_cat, DN_WIDTH, 0)]], 256)
    dqh, dkh, dv_dn, dgb, dbb, dgcb = _scan_bwd(_gdn_group, "gdn_scan_bwd", dn_arrs, kept_dn, do, DN_HEADS)
    one = lambda t: [(t, DN_WIDTH, 0)]
    (dcq, dck, dgates), (g["dn_a_log"], g["dn_dt_bias"]) = _row_bwd(
        _dn_pre_fn, "dn_pre_bwd", dn_pre_tiles, dn_pre_params,
        [one(dqh), one(dkh), one(dgb), one(dbb), one(dgcb)], CHUNK)
    dp_qkv, (g["dn_conv_w"],) = _col_bwd(_conv_fn, "dn_conv_bwd", p, 0, 24, [wt["dn_conv_w"]], [dcq, dck, dv_dn])
    dp = jnp.concatenate([t.astype(BF16) for t in (dp_qkv, dz, dgates, dp_rw, jnp.zeros((x.shape[0], LANES), F32))],
                         axis=1)
    g["w_in"] = _matmul("in_proj_dw", dp, u, "tn", [BF16], tm=1536)[0]
    early = _logical_grads(g)
    blocks = []
    for src, cols in _scatter_plan({n: early.pop(n) for n in EARLY})[0]:
        if cols is not None:
            src = src.reshape(src.shape[0], N_DEV, cols).transpose(1, 0, 2)
        blocks.append(src.reshape((4, 2) + src.shape[1:]))
    sums = [_pair_add("early_grad_pair_add_%d" % i, mine, theirs)
            for i, (mine, theirs) in enumerate(zip(blocks, _pair_swap("early_grad_pair_swap", blocks)))]
    pending[EARLY], tok = _exchange_start("early_grad_start", [(t, None) for t in sums],
                                          [(t.shape, t.dtype, None) for t in sums], False, tok, chips=True)
    dx, early["mix_norm_w"] = _matmul_norm_bwd("in_proj_dx", dp, wt["w_in"], "nn", x, wt["mix_norm_w"], dh1, tok)
    return loss8, dx, early, pending, tok


WEIGHTS = ["mix_norm_w", "w_in", "dn_conv_w", "dn_a_log", "dn_dt_bias", "dn_norm_w", "rw_mu", "rw_w0", "rw_w2",
           "rw_a0", "rw_a2", "rw_g2", "rw_k_k", "rw_k_a", "rw_r_k", "rw_ln_w", "rw_ln_b", "w_out", "xa_norm_w",
           "mem_norm_w", "xa_wq", "xa_wk", "xa_wv", "xa_wo", "ffn_norm_w", "ffn_w1", "ffn_w2", "final_norm_w"]
SHARDED = {"w_in": False, "w_out": False, "xa_wq": False, "xa_wk": False, "xa_wv": False, "xa_wo": True,
           "ffn_w1": True, "ffn_w2": False, "dn_conv_w": True, "rw_w2": True, "rw_a2": True, "rw_g2": True}
BF16_PAYLOAD = ("w_in", "w_out", "xa_wq", "xa_wk", "xa_wv", "xa_wo", "ffn_w1", "ffn_w2")
REPLICATED = [n for n in WEIGHTS if n not in SHARDED]
EARLY = ("w_in", "dn_conv_w", "rw_w2", "rw_a2", "rw_g2")
RW_IN_COLS = IN_COLS - DN_COLS
W_IN_SHARD = IN_COLS // N_DEV


def _layout_weights(fw):
    wt = dict(fw)
    wt["dn_conv_w"] = jnp.pad(fw["dn_conv_w"], ((0, 4), (0, 0)))
    wt["dn_a_log"] = jnp.pad(fw["dn_a_log"], ((0, 0), (0, LANES - DN_HEADS)))
    wt["dn_dt_bias"] = jnp.pad(fw["dn_dt_bias"], ((0, 0), (0, LANES - DN_HEADS)))
    wt["rw_w2"] = jnp.pad(fw["rw_w2"], ((0, 64), (0, 0)))
    wt["rw_a2"] = jnp.pad(fw["rw_a2"], ((64, 0), (0, 0)))
    return wt


def _logical_grads(g):
    out = dict(g)
    out["w_in"] = _w_in_grad_to_shards(g["w_in"])
    out["dn_conv_w"] = g["dn_conv_w"][:4]
    out["dn_a_log"] = g["dn_a_log"][:, :DN_HEADS]
    out["dn_dt_bias"] = g["dn_dt_bias"][:, :DN_HEADS]
    out["rw_w2"] = g["rw_w2"][:64]
    out["rw_a2"] = g["rw_a2"][64:]
    return out


def _pack(vals):
    parts = []
    for v in vals:
        flat = v.reshape(-1)
        parts.append(jnp.pad(flat, (0, -flat.shape[0] % LANES)))
    flat = jnp.concatenate(parts)
    flat = jnp.pad(flat, (0, -flat.shape[0] % (8 * LANES)))
    return flat.reshape(-1, LANES)


def _unpack(packed, shapes):
    flat = packed.reshape(-1)
    out, at = [], 0
    for shp in shapes:
        size = math.prod(shp)
        out.append(flat[at:at + size].reshape(shp))
        at += size + (-size % LANES)
    return out


def kernel(x, mem, mix_norm_w, w_in, dn_conv_w, dn_a_log, dn_dt_bias, dn_norm_w, rw_mu, rw_w0, rw_w2, rw_a0, rw_a2, rw_g2, rw_k_k, rw_k_a, rw_r_k, rw_ln_w, rw_ln_b, w_out, xa_norm_w, mem_norm_w, xa_wq, xa_wk, xa_wv, xa_wo, ffn_norm_w, ffn_w1, ffn_w2, final_norm_w, loss_target, m_mix_norm_w, m_w_in, m_dn_conv_w, m_dn_a_log, m_dn_dt_bias, m_dn_norm_w, m_rw_mu, m_rw_w0, m_rw_w2, m_rw_a0, m_rw_a2, m_rw_g2, m_rw_k_k, m_rw_k_a, m_rw_r_k, m_rw_ln_w, m_rw_ln_b, m_w_out, m_xa_norm_w, m_mem_norm_w, m_xa_wq, m_xa_wk, m_xa_wv, m_xa_wo, m_ffn_norm_w, m_ffn_w1, m_ffn_w2, m_final_norm_w, v_mix_norm_w, v_w_in, v_dn_conv_w, v_dn_a_log, v_dn_dt_bias, v_dn_norm_w, v_rw_mu, v_rw_w0, v_rw_w2, v_rw_a0, v_rw_a2, v_rw_g2, v_rw_k_k, v_rw_k_a, v_rw_r_k, v_rw_ln_w, v_rw_ln_b, v_w_out, v_xa_norm_w, v_mem_norm_w, v_xa_wq, v_xa_wk, v_xa_wv, v_xa_wo, v_ffn_norm_w, v_ffn_w1, v_ffn_w2, v_final_norm_w):
    given = dict(locals())
    w = {n: given[n] for n in WEIGHTS}
    m = {n: given["m_" + n] for n in WEIGHTS}
    v = {n: given["v_" + n] for n in WEIGHTS}

    local = {n: (lambda t: t[0].T) if n == "w_in" else (lambda t: t[0]) for n in SHARDED}
    shards = {n: (local[n](w[n]).astype(BF16) if n in BF16_PAYLOAD else local[n](w[n])) for n in SHARDED}
    srcs, dsts, _ = _gather_plan({n: shards[n] for n in EARLY})
    full = _gather_finish(EARLY, _gather_two_level("early_all_gather", srcs, dsts))
    for n in REPLICATED:
        full[n] = w[n].reshape(1, -1)

    loss8, dx, g, pending, after = _local_step(x[0], mem[0], loss_target[0], _layout_weights(full),
                                               {n: shards[n] for n in SHARDED if n not in EARLY})

    packed = _pack([g[n] for n in REPLICATED] + [loss8[:1, :1]])
    small, _ = _exchange_start("small_gather_start", [(packed, None)], [((N_DEV,) + packed.shape, F32, None)], True,
                               after)
    grad, delta, new_m, new_v = {}, {}, {}, {}
    done = [dx]

    def tie():
        return jnp.broadcast_to(sum(t[:1, :1] for t in done), (8, LANES))

    for names in sorted(pending, key=lambda names: names == EARLY):
        handle = pending[names]
        for n, parts in zip(names, _exchange_wait("grad_wait_" + names[0], handle, tie())):
            res = _sum_adamw("adamw_" + n, parts, local[n](w[n]), local[n](m[n]), local[n](v[n]))
            grad[n], delta[n], new_m[n], new_v[n] = [(t.T if n == "w_in" else t)[None] for t in res]
            done.append(res[1])

    (parts,) = _exchange_wait("small_gather_wait", small, tie())
    blank = [jnp.zeros((1, 1), F32)]
    res = _sum_adamw("adamw_small", parts, _pack([w[n] for n in REPLICATED] + blank),
                     _pack([m[n] for n in REPLICATED] + blank), _pack([v[n] for n in REPLICATED] + blank))
    shapes = [w[n].shape for n in REPLICATED] + [()]
    loss = _unpack(res[0], shapes)[-1]
    for store, packed_out in zip((grad, delta, new_m, new_v), res):
        for n, val in zip(REPLICATED, _unpack(packed_out, shapes)):
            store[n] = val

    return (loss, dx[None], *[grad[n] for n in WEIGHTS], *[delta[n] for n in WEIGHTS],
            *[new_m[n] for n in WEIGHTS], *[new_v[n] for n in WEIGHTS])
```

```python
import functools
import math

import jax
import jax.numpy as jnp
from jax import lax
from jax.experimental import pallas as pl
from jax.experimental.pallas import tpu as pltpu

F32 = jnp.float32
BF16 = jnp.bfloat16
SDS = jax.ShapeDtypeStruct

N_DEV = 8
D_MODEL = 2048
LANES = 128
CHUNK = 128
DN_HEADS = 8
DN_WIDTH = 1024
RW_WIDTH = 1024
RW_HEAD = 64
XA_HEADS = 4
XA_WIDTH = 512
FFN_HIDDEN = 8192
IN_COLS = 7440
DN_COLS = 4112
IN_PAD = 7680
RW_OFF = 4224
RMS_EPS = 1e-6
RW_GN_EPS = 64e-5
VMEM_LIMIT = 56 * 1024 * 1024

ADAM_LR = 0.001
ADAM_B1 = 0.9
ADAM_B2 = 0.999
ADAM_EPS = 1e-08
ADAM_WD = 0.01
ADAM_STEP = 10

_DIMS = {"nn": (((1,), (0,)), ((), ())), "nt": (((1,), (1,)), ((), ())), "tn": (((0,), (0,)), ((), ()))}


def _raw_dot(a, b, mode, hi):
    if hi:
        return lax.dot_general(a, b, _DIMS[mode], precision=lax.Precision.HIGHEST, preferred_element_type=F32)
    return lax.dot_general(a.astype(BF16), b.astype(BF16), _DIMS[mode], preferred_element_type=F32)


@functools.partial(jax.custom_vjp, nondiff_argnums=(2, 3))
def mm(a, b, mode="nn", hi=False):
    return _raw_dot(a, b, mode, hi)


def _mm_fwd(a, b, mode, hi):
    return _raw_dot(a, b, mode, hi), (a, b)


def _mm_bwd(mode, hi, res, g):
    a, b = res
    if mode == "nn":
        return _raw_dot(g, b, "nt", hi), _raw_dot(a, g, "tn", hi)
    if mode == "nt":
        return _raw_dot(g, b, "nn", hi), _raw_dot(g, a, "tn", hi)
    return _raw_dot(b, g, "nt", hi), _raw_dot(a, g, "nn", hi)


mm.defvjp(_mm_fwd, _mm_bwd)


def _shift_rows_raw(x, k):
    n = x.shape[0]
    rolled = pltpu.roll(x, k % n, axis=0)
    row = lax.broadcasted_iota(jnp.int32, x.shape, 0)
    keep = row >= k if k > 0 else row < n + k
    return jnp.where(keep, rolled, 0.0)


@functools.partial(jax.custom_vjp, nondiff_argnums=(1,))
def shift_rows(x, k):
    return _shift_rows_raw(x, k)


shift_rows.defvjp(lambda x, k: (_shift_rows_raw(x, k), None), lambda k, _, g: (_shift_rows_raw(g, -k),))


@jax.custom_vjp
def _sigmoid(x):
    return 1.0 / (1.0 + jnp.exp(-x))


def _sigmoid_fwd(x):
    s = 1.0 / (1.0 + jnp.exp(-x))
    return s, s


_sigmoid.defvjp(_sigmoid_fwd, lambda s, g: (g * s * (1.0 - s),))


@jax.custom_vjp
def _softplus(x):
    return jnp.maximum(x, 0.0) + jnp.log(1.0 + jnp.exp(-jnp.abs(x)))


_softplus.defvjp(lambda x: (_softplus(x), x), lambda x, g: (g / (1.0 + jnp.exp(-x)),))


@jax.custom_vjp
def _silu(x):
    return x / (1.0 + jnp.exp(-x))


def _silu_fwd(x):
    s = 1.0 / (1.0 + jnp.exp(-x))
    return x * s, (x, s)


_silu.defvjp(_silu_fwd, lambda res, g: (g * res[1] * (1.0 + res[0] * (1.0 - res[1])),))


def _tri_masks(n):
    ii = lax.broadcasted_iota(jnp.int32, (n, n), 0)
    jj = lax.broadcasted_iota(jnp.int32, (n, n), 1)
    return ii >= jj, ii > jj, ii == jj


def _neumann_inv_raw(m):
    n = m.shape[0]
    _, _, eye = _tri_masks(n)
    eye = jnp.where(eye, 1.0, 0.0)
    p = eye + m
    mk = m
    for _ in range(int(math.log2(n)) - 1):
        mk = _raw_dot(mk, mk, "nn", False)
        p = p + _raw_dot(p, mk, "nn", False)
    resid = eye - p + _raw_dot(m, p, "nn", True)
    return p + _raw_dot(p, resid, "nn", False)


@jax.custom_vjp
def _neumann_inv(m):
    return _neumann_inv_raw(m)


def _neumann_inv_fwd(m):
    p = _neumann_inv_raw(m)
    return p, p


def _neumann_inv_bwd(p, g):
    return (_raw_dot(_raw_dot(p, g, "tn", False), p, "nt", False),)


_neumann_inv.defvjp(_neumann_inv_fwd, _neumann_inv_bwd)


@jax.custom_vjp
def _saved_inv(m, p):
    return p


_saved_inv.defvjp(lambda m, p: (p, p), lambda p, g: (_neumann_inv_bwd(p, g)[0], jnp.zeros_like(p)))


def _inverse(m, saved):
    return _neumann_inv(m) if saved is None else _saved_inv(m, saved)


def _cumsum_rows(x):
    causal, _, _ = _tri_masks(x.shape[0])
    return mm(jnp.where(causal, 1.0, 0.0), x, "nn", True)


def _gdn_group(s0, q, k, v, gb, bb, gc, *saved):
    diff = jnp.stack([gc[j] - gc[j].T for j in range(gc.shape[0])])
    return jax.vmap(_gdn_chunk)(s0, q, k, v, gb, bb, gc, diff, *saved)


def _rw_group(*args):
    return jax.vmap(_rw_chunk)(*args)


def _gdn_chunk(s0, q, k, v, gb, bb, gc, diff, saved=None):
    c = q.shape[0]
    causal, strict, _ = _tri_masks(c)
    decay = jnp.exp(jnp.where(causal, diff, -jnp.inf))
    kb = k * bb
    a = jnp.where(strict, mm(kb, k, "nt") * decay, 0.0)
    p = _inverse(-a, saved)
    u = mm(p, v * bb)
    w = mm(p, kb * jnp.exp(gc))
    attn = mm(q, k, "nt") * decay
    v_new = u - mm(w, s0)
    o = mm(q * jnp.exp(gc), s0) + mm(attn, v_new)
    g_last = jnp.sum(gb, axis=0, keepdims=True)
    s1 = s0 * jnp.exp(g_last) + mm(k * jnp.exp(g_last - gc), v_new, "tn")
    return o, s1, p


def _rw_chunk(s0, r, lw, k, v, al, be, gc, saved0=None, saved1=None):
    c = r.shape[0]
    causal, strict, _ = _tri_masks(c)
    gp = gc - lw
    row = lax.broadcasted_iota(jnp.int32, lw.shape, 0)
    lane = lax.broadcasted_iota(jnp.int32, lw.shape, 1)
    g_mid = jnp.sum(jnp.where(row < c // 2, lw, 0.0), axis=0, keepdims=True)
    g_last = jnp.sum(lw, axis=0, keepdims=True)
    e_n = jnp.exp(g_mid - gc)
    rg = r * jnp.exp(gc - g_mid)
    bg = be * jnp.exp(gp - g_mid)
    an = al * e_n
    kn = k * e_n
    bt = mm(be * jnp.exp(gp), s0, "nt")
    rt = mm(r * jnp.exp(gc), s0, "nt")
    us, ys, ps = [], [], []
    for h, saved in enumerate((saved0, saved1)):
        mine = (lane >= RW_HEAD) if h else (lane < RW_HEAD)
        bgh = jnp.where(mine, bg, 0.0)
        rgh = jnp.where(mine, rg, 0.0)
        a_ab = jnp.where(strict, mm(bgh, an, "nt"), 0.0)
        a_kb = jnp.where(strict, mm(bgh, kn, "nt"), 0.0)
        a_ra = jnp.where(causal, mm(rgh, an, "nt"), 0.0)
        a_rk = jnp.where(causal, mm(rgh, kn, "nt"), 0.0)
        p = _inverse(a_ab, saved)
        ps.append(p)
        u_h = mm(p, bt + mm(a_kb, v))
        us.append(u_h)
        ys.append(rt + mm(a_ra, u_h) + mm(a_rk, v))
    lo = lane < RW_HEAD
    u = jnp.where(lo, us[0], us[1])
    y = jnp.where(lo, ys[0], ys[1])
    tail = jnp.exp(g_last - gc)
    s1 = s0 * jnp.exp(g_last) + mm(u, al * tail, "tn") + mm(v, k * tail, "tn")
    vi = lax.broadcasted_iota(jnp.int32, s0.shape, 0)
    ki = lax.broadcasted_iota(jnp.int32, s0.shape, 1)
    s1 = jnp.where((vi < RW_HEAD) == (ki < RW_HEAD), s1, 0.0)
    return y, s1, ps[0], ps[1]


SCAN_HB = 8


def _scan_specs(arrs, n_chunks, reverse):
    def spec(off):
        assert off % SCAN_HB == 0
        if reverse:
            return pl.BlockSpec((CHUNK, SCAN_HB * LANES), lambda h, n: (n_chunks - 1 - n, off // SCAN_HB + h))
        return pl.BlockSpec((CHUNK, SCAN_HB * LANES), lambda h, n: (n, off // SCAN_HB + h))
    return [spec(off) for _, off in arrs]


def _split_heads(x):
    return jnp.stack([x[:, LANES * j:LANES * (j + 1)] for j in range(SCAN_HB)], axis=0)


def _merge_heads(x):
    return jnp.concatenate([x[j] for j in range(SCAN_HB)], axis=1)


def _scan_fwd(group_fn, name, arrs, heads, n_kept):
    s = arrs[0][0].shape[0]
    n_chunks = s // CHUNK
    n_in = len(arrs)

    def body(*refs):
        y_ref, st_ref = refs[n_in:n_in + 2]
        kept_refs, s_scr = refs[n_in + 2:-1], refs[-1]

        @pl.when(pl.program_id(1) == 0)
        def _():
            s_scr[...] = jnp.zeros_like(s_scr)

        s0 = s_scr[...]
        st_ref[...] = s0
        y, s1, *kept = group_fn(s0, *[_split_heads(r[...]) for r in refs[:n_in]])
        y_ref[...] = _merge_heads(y)
        s_scr[...] = s1
        for ref, val in zip(kept_refs, kept):
            ref[...] = val

    per_chunk = pl.BlockSpec((SCAN_HB, None, LANES, LANES), lambda h, n: (h, n, 0, 0))
    res = pl.pallas_call(
        body, grid=(heads // SCAN_HB, n_chunks), name=name,
        in_specs=_scan_specs(arrs, n_chunks, False),
        out_specs=[pl.BlockSpec((CHUNK, SCAN_HB * LANES), lambda h, n: (n, h))] + [per_chunk] * (1 + n_kept),
        out_shape=[SDS((s, heads * LANES), F32)] + [SDS((heads, n_chunks, LANES, LANES), F32)] * (1 + n_kept),
        scratch_shapes=[pltpu.VMEM((SCAN_HB, LANES, LANES), F32)],
        compiler_params=pltpu.CompilerParams(dimension_semantics=("arbitrary", "arbitrary")),
    )(*[a for a, _ in arrs])
    return res[0], res[1:]


def _scan_bwd(group_fn, name, arrs, kept, dy, heads):
    s = arrs[0][0].shape[0]
    n_chunks = s // CHUNK
    n_in, n_kept = len(arrs), len(kept)

    def body(*refs):
        kept_vals = [r[...] for r in refs[n_in:n_in + n_kept]]
        dy_ref = refs[n_in + n_kept]
        d_refs = refs[n_in + n_kept + 1:2 * n_in + n_kept + 1]
        ds_scr = refs[-1]

        @pl.when(pl.program_id(1) == 0)
        def _():
            ds_scr[...] = jnp.zeros_like(ds_scr)

        def fn(s0, *ins):
            return group_fn(s0, *ins, *kept_vals[1:])[:2]

        _, vjp = jax.vjp(fn, kept_vals[0], *[_split_heads(r[...]) for r in refs[:n_in]])
        grads = vjp((_split_heads(dy_ref[...]), ds_scr[...]))
        ds_scr[...] = grads[0]
        for ref, g in zip(d_refs, grads[1:]):
            ref[...] = _merge_heads(g)

    rev = pl.BlockSpec((CHUNK, SCAN_HB * LANES), lambda h, n: (n_chunks - 1 - n, h))
    per_chunk = pl.BlockSpec((SCAN_HB, None, LANES, LANES), lambda h, n: (h, n_chunks - 1 - n, 0, 0))
    return pl.pallas_call(
        body, grid=(heads // SCAN_HB, n_chunks), name=name,
        in_specs=_scan_specs(arrs, n_chunks, True) + [per_chunk] * n_kept + [rev],
        out_specs=[rev] * n_in,
        out_shape=[SDS((s, heads * LANES), F32)] * n_in,
        scratch_shapes=[pltpu.VMEM((SCAN_HB, LANES, LANES), F32)],
        compiler_params=pltpu.CompilerParams(dimension_semantics=("arbitrary", "arbitrary")),
    )(*[a for a, _ in arrs], *kept, dy)


def _col_spec(tr, width, cb):
    return pl.BlockSpec((tr, width), lambda i: (i, cb))


def _whole(p):
    return pl.BlockSpec(p.shape, lambda i: (0,) * p.ndim)


def _row_fwd(fn, name, tiles, params, outs, tr):
    rows = tiles[0][0].shape[0]
    nt, npar = len(tiles), len(params)

    def body(*refs):
        vals = [r[...].astype(F32) for r in refs[:nt + npar]]
        for ref, o in zip(refs[nt + npar:], fn(*vals)):
            ref[...] = o.astype(ref.dtype)

    return pl.pallas_call(
        body, grid=(rows // tr,), name=name,
        in_specs=[_col_spec(tr, w, cb) for _, w, cb in tiles] + [_whole(p) for p in params],
        out_specs=[_col_spec(tr, w, 0) for w, _ in outs],
        out_shape=[SDS((rows, w), dt) for w, dt in outs],
        compiler_params=pltpu.CompilerParams(dimension_semantics=("arbitrary",), vmem_limit_bytes=VMEM_LIMIT),
    )(*[a for a, _, _ in tiles], *params)


def _row_bwd(fn, name, tiles, params, cts, tr, want_tiles=None):
    rows = tiles[0][0].shape[0]
    nt, npar = len(tiles), len(params)
    want = list(range(nt)) if want_tiles is None else list(want_tiles)
    flat_cts = [c for group in cts for c in group]
    n_ct = len(flat_cts)

    def body(*refs):
        vals = [r[...].astype(F32) for r in refs[:nt + npar]]
        ct_refs = refs[nt + npar:nt + npar + n_ct]
        out_refs = refs[nt + npar + n_ct:]
        ct_vals, at = [], 0
        for group in cts:
            total = ct_refs[at][...].astype(F32)
            for r in ct_refs[at + 1:at + len(group)]:
                total = total + r[...].astype(F32)
            ct_vals.append(total)
            at += len(group)
        _, vjp = jax.vjp(lambda *a: tuple(fn(*a)), *vals)
        grads = vjp(tuple(ct_vals))
        for ref, t in zip(out_refs[:len(want)], want):
            ref[...] = grads[t]
        first = pl.program_id(0) == 0
        for ref, g in zip(out_refs[len(want):], grads[nt:]):
            @pl.when(first)
            def _(ref=ref, g=g):
                ref[...] = g

            @pl.when(jnp.logical_not(first))
            def _(ref=ref, g=g):
                ref[...] += g

    res = pl.pallas_call(
        body, grid=(rows // tr,), name=name,
        in_specs=[_col_spec(tr, w, cb) for _, w, cb in tiles] + [_whole(p) for p in params]
        + [_col_spec(tr, w, cb) for _, w, cb in flat_cts],
        out_specs=[_col_spec(tr, tiles[t][1], 0) for t in want] + [_whole(p) for p in params],
        out_shape=[SDS((rows, tiles[t][1]), F32) for t in want] + [SDS(p.shape, F32) for p in params],
        compiler_params=pltpu.CompilerParams(dimension_semantics=("arbitrary",), vmem_limit_bytes=VMEM_LIMIT),
    )(*[a for a, _, _ in tiles], *params, *[a for a, _, _ in flat_cts])
    return res[:len(want)], res[len(want):]


def _col_fwd(fn, name, x, first_block, n_blocks, params):
    rows = x.shape[0]

    def body(*refs):
        refs[-1][...] = fn(*[r[...] for r in refs[:-1]])

    return pl.pallas_call(
        body, grid=(n_blocks,), name=name,
        in_specs=[pl.BlockSpec((rows, LANES), lambda j: (0, first_block + j))]
        + [pl.BlockSpec((p.shape[0], LANES), lambda j: (0, j)) for p in params],
        out_specs=pl.BlockSpec((rows, LANES), lambda j: (0, j)),
        out_shape=SDS((rows, n_blocks * LANES), F32),
        compiler_params=pltpu.CompilerParams(dimension_semantics=("arbitrary",), vmem_limit_bytes=VMEM_LIMIT),
    )(x, *params)


def _col_bwd(fn, name, x, first_block, n_blocks, params, dys):
    rows = x.shape[0]
    npar, nd = len(params), len(dys)
    starts = [sum(t.shape[1] for t in dys[:i]) // LANES for i in range(nd + 1)]

    def body(*refs):
        vals = [r[...] for r in refs[:1 + npar]]
        j = pl.program_id(0)
        dy = refs[1 + npar][...]
        for i in range(1, nd):
            dy = jnp.where(j >= starts[i], refs[1 + npar + i][...], dy)
        _, vjp = jax.vjp(fn, *vals)
        grads = vjp(dy)
        for ref, g in zip(refs[1 + npar + nd:], grads):
            ref[...] = g.astype(ref.dtype)

    def piece(i):
        last = starts[i + 1] - starts[i] - 1
        return pl.BlockSpec((rows, LANES), lambda j: (0, jnp.clip(j - starts[i], 0, last)))

    pspecs = [pl.BlockSpec((p.shape[0], LANES), lambda j: (0, j)) for p in params]
    blk = pl.BlockSpec((rows, LANES), lambda j: (0, j))
    res = pl.pallas_call(
        body, grid=(n_blocks,), name=name,
        in_specs=[pl.BlockSpec((rows, LANES), lambda j: (0, first_block + j))] + pspecs + [piece(i) for i in range(nd)],
        out_specs=[blk] + pspecs,
        out_shape=[SDS((rows, n_blocks * LANES), BF16)] + [SDS(p.shape, F32) for p in params],
        compiler_params=pltpu.CompilerParams(dimension_semantics=("arbitrary",), vmem_limit_bytes=VMEM_LIMIT),
    )(x, *params, *dys)
    return res[0], res[1:]


def _conv_fn(x, w):
    acc = x * w[3:4, :]
    for j in range(3):
        acc = acc + shift_rows(x, 3 - j) * w[j:j + 1, :]
    return _silu(acc)


def _lerp_fn(x, mu):
    return x + (shift_rows(x, 1) - x) * mu[0:1, :]


def _seg_sum(x, width):
    if width == LANES:
        return jnp.sum(x, axis=1, keepdims=True)
    lo = lax.broadcasted_iota(jnp.int32, x.shape, 1) < width
    s0 = jnp.sum(jnp.where(lo, x, 0.0), axis=1, keepdims=True)
    s1 = jnp.sum(jnp.where(lo, 0.0, x), axis=1, keepdims=True)
    return jnp.where(lo, s0, s1)


def _per_block(fn, *xs):
    n = xs[0].shape[1] // LANES
    return jnp.concatenate([fn(*[x[:, LANES * b:LANES * (b + 1)] for x in xs]) for b in range(n)], axis=1)


def _head_expand(col0):
    r = lax.broadcasted_iota(jnp.int32, (LANES, DN_WIDTH), 0)
    c = lax.shift_right_logical(lax.broadcasted_iota(jnp.int32, (LANES, DN_WIDTH), 1), 7)
    return jnp.where(r == c + col0, 1.0, 0.0)


def _dn_pre_fn(cq, ck, gates, a_log, dt_bias):
    l2 = lambda x: x * lax.rsqrt(_seg_sum(x * x, LANES) + 1e-6)
    qh = _per_block(l2, cq) * (LANES ** -0.5)
    kh = _per_block(l2, ck)
    g = -jnp.exp(a_log) * _softplus(gates + dt_bias)
    gb = mm(g, _head_expand(0), "nn", True)
    bb = mm(_sigmoid(gates), _head_expand(DN_HEADS), "nn", True)
    return qh, kh, gb, bb, _cumsum_rows(gb)


def _dn_post_fn(o, z, nw):
    def one(ob, zb):
        return ob * lax.rsqrt(_seg_sum(ob * ob, LANES) * (1.0 / LANES) + RMS_EPS) * nw * _silu(zb)
    return (_per_block(one, o, z),)


def _rw_pre_fn(pr, pk, pv, pwa, pg, w0, a0, k_k, k_a, w2p, a2p, g2):
    log_w = -_softplus(-(w0 + mm(jnp.tanh(pwa), w2p))) - 0.5
    lw = -jnp.exp(log_w)
    a = _sigmoid(a0 + mm(pwa, a2p))
    gate = mm(_sigmoid(pg), g2)
    kk = pk * k_k
    kk = _per_block(lambda x: x / jnp.maximum(jnp.sqrt(_seg_sum(x * x, RW_HEAD)), 1e-12), kk)
    k = pk * (1.0 + (a - 1.0) * k_a)
    return pr, lw, k, pv, kk * a, -kk, gate, _cumsum_rows(lw)


def _rw_post_fn(y, r, k, v, gate, ln_w, ln_b, r_k):
    def one(yb, rb, kb, vb, gb, wb, bb, rkb):
        d = yb - _seg_sum(yb, RW_HEAD) * (1.0 / RW_HEAD)
        var = _seg_sum(d * d, RW_HEAD) * (1.0 / RW_HEAD)
        yn = d * lax.rsqrt(var + RW_GN_EPS) * wb + bb
        return (yn + _seg_sum(rb * kb * rkb, RW_HEAD) * vb) * gb
    return (_per_block(one, y, r, k, v, gate, ln_w, ln_b, r_k),)


def _rms_fn(h, w):
    return (h * lax.rsqrt(jnp.mean(h * h, axis=1, keepdims=True) + RMS_EPS) * w,)


def _xattn_fn(q, k, v):
    outs = []
    for h in range(XA_HEADS):
        sl = slice(LANES * h, LANES * (h + 1))
        s = mm(q[:, sl], k[:, sl], "nt") * (LANES ** -0.5)
        e = jnp.exp(s - jnp.max(s, axis=1, keepdims=True))
        outs.append(mm(e / jnp.sum(e, axis=1, keepdims=True), v[:, sl]))
    return (jnp.concatenate(outs, axis=1),)


def _fit(tile, dim):
    best = [t for t in range(LANES, min(tile, dim) + 1, LANES) if dim % t == 0]
    assert best, (tile, dim)
    return best[-1]


def _matmul(name, a, b, mode, out_dtypes, epilogue=None, extras=(), tm=1024, tn=1024, tk=2048, after=None):
    if mode == "tn":
        (k_dim, m), n = a.shape, b.shape[1]
    else:
        (m, k_dim), n = a.shape, (b.shape[1] if mode == "nn" else b.shape[0])
    tm, tn, tk = _fit(tm, m), _fit(tn, n), _fit(tk, k_dim)
    nk = k_dim // tk
    a_spec = (pl.BlockSpec((tk, tm), lambda i, j, k: (k, i)) if mode == "tn"
              else pl.BlockSpec((tm, tk), lambda i, j, k: (i, k)))
    b_spec = (pl.BlockSpec((tn, tk), lambda i, j, k: (j, k)) if mode == "nt"
              else pl.BlockSpec((tk, tn), lambda i, j, k: (k, j)))
    o_spec = pl.BlockSpec((tm, tn), lambda i, j, k: (i, j))
    n_ex, n_out = len(extras), len(out_dtypes)
    ties = [] if after is None else [after]

    def finish(total, rest):
        ex = [r[...].astype(F32) for r in rest[:n_ex]]
        res = epilogue(total, *ex) if epilogue else (total,)
        for ref, o in zip(rest[n_ex + len(ties):n_ex + len(ties) + n_out], res):
            ref[...] = o.astype(ref.dtype)

    def body_single(a_ref, b_ref, *rest):
        finish(_raw_dot(a_ref[...], b_ref[...], mode, False), rest)

    def body_acc(a_ref, b_ref, *rest):
        acc = rest[-1]
        k = pl.program_id(2)

        @pl.when(k == 0)
        def _():
            acc[...] = jnp.zeros_like(acc)

        acc[...] += _raw_dot(a_ref[...], b_ref[...], mode, False)

        @pl.when(k == nk - 1)
        def _():
            finish(acc[...], rest)

    res = pl.pallas_call(
        body_single if nk == 1 else body_acc, grid=(m // tm, n // tn, nk), name=name,
        in_specs=[a_spec, b_spec] + [o_spec] * n_ex + [pl.BlockSpec((8, LANES), lambda i, j, k: (0, 0))] * len(ties),
        out_specs=[o_spec] * n_out,
        out_shape=[SDS((m, n), dt) for dt in out_dtypes],
        scratch_shapes=[] if nk == 1 else [pltpu.VMEM((tm, tn), F32)],
        compiler_params=pltpu.CompilerParams(dimension_semantics=("parallel", "parallel", "arbitrary"),
                                             vmem_limit_bytes=VMEM_LIMIT),
    )(a, b, *extras, *ties)
    return res


def _matmul_norm_bwd(name, a, b, mode, h, w, dres, after=None, tm=512, tk=1024):
    m, n = h.shape
    k_dim = a.shape[1]
    tm, tk = _fit(tm, m), _fit(tk, k_dim)
    nk = k_dim // tk
    ties = [] if after is None else [after]
    a_spec = pl.BlockSpec((tm, tk), lambda i, k: (i, k))
    b_spec = pl.BlockSpec((n, tk), lambda i, k: (0, k)) if mode == "nt" else pl.BlockSpec((tk, n), lambda i, k: (k, 0))
    row = pl.BlockSpec((tm, n), lambda i, k: (i, 0))
    w_spec = pl.BlockSpec((1, n), lambda i, k: (0, 0))

    def body(a_ref, b_ref, h_ref, w_ref, dres_ref, *rest):
        dh_ref, dw_ref, acc = rest[len(ties):]
        i, k = pl.program_id(0), pl.program_id(1)

        @pl.when(k == 0)
        def _():
            acc[...] = jnp.zeros_like(acc)

        acc[...] += _raw_dot(a_ref[...], b_ref[...], mode, False)

        @pl.when(k == nk - 1)
        def _():
            _, vjp = jax.vjp(_rms_res_fn, h_ref[...], w_ref[...])
            dh, dw = vjp((acc[...], dres_ref[...]))
            dh_ref[...] = dh

            @pl.when(i == 0)
            def _():
                dw_ref[...] = dw

            @pl.when(i != 0)
            def _():
                dw_ref[...] += dw

    return pl.pallas_call(
        body, grid=(m // tm, nk), name=name,
        in_specs=[a_spec, b_spec, row, w_spec, row] + [pl.BlockSpec((8, LANES), lambda i, k: (0, 0))] * len(ties),
        out_specs=[row, w_spec],
        out_shape=[SDS((m, n), F32), SDS((1, n), F32)],
        scratch_shapes=[pltpu.VMEM((tm, n), F32)],
        compiler_params=pltpu.CompilerParams(dimension_semantics=("arbitrary", "arbitrary"),
                                             vmem_limit_bytes=VMEM_LIMIT),
    )(a, b, h, w, dres, *ties)


def _loss_call(h, target, w, tr=256):
    rows, d = h.shape

    def fn(hv, wv, tv):
        y = _rms_fn(hv, wv)[0]
        return 0.5 * jnp.sum(jnp.mean(jnp.square(y - tv), axis=1, keepdims=True), axis=0, keepdims=True)

    def body(h_ref, t_ref, w_ref, loss_ref, dh_ref, dw_ref):
        tv = t_ref[...]
        val, vjp = jax.vjp(lambda hv, wv: fn(hv, wv, tv), h_ref[...], w_ref[...])
        dh, dw = vjp(jnp.ones((1, 1), F32))
        dh_ref[...] = dh
        first = pl.program_id(0) == 0

        @pl.when(first)
        def _():
            loss_ref[...] = jnp.broadcast_to(val, loss_ref.shape)
            dw_ref[...] = dw

        @pl.when(jnp.logical_not(first))
        def _():
            loss_ref[...] += jnp.broadcast_to(val, loss_ref.shape)
            dw_ref[...] += dw

    return pl.pallas_call(
        body, grid=(rows // tr,), name="loss_head",
        in_specs=[_col_spec(tr, d, 0), _col_spec(tr, d, 0), _whole(w)],
        out_specs=[pl.BlockSpec((8, LANES), lambda i: (0, 0)), _col_spec(tr, d, 0), _whole(w)],
        out_shape=[SDS((8, LANES), F32), SDS((rows, d), F32), SDS(w.shape, F32)],
        compiler_params=pltpu.CompilerParams(dimension_semantics=("arbitrary",), vmem_limit_bytes=VMEM_LIMIT),
    )(h, target, w)


def _adamw_vals(w, g, m, v):
    m = ADAM_B1 * m + (1.0 - ADAM_B1) * g
    v = ADAM_B2 * v + (1.0 - ADAM_B2) * jnp.square(g)
    m_hat = m / (1.0 - ADAM_B1 ** ADAM_STEP)
    v_hat = v / (1.0 - ADAM_B2 ** ADAM_STEP)
    delta = -ADAM_LR * (m_hat / (jnp.sqrt(v_hat) + ADAM_EPS) + ADAM_WD * w)
    return delta, m, v


def _sum_adamw(name, parts, w, m, v):
    r, c = w.shape
    n_parts = parts.shape[0]
    budget = 6 * 1024 * 1024
    tr, tc = r, c
    for cand in (512, 256, 128, 64, 32, 16, 8):
        if r % cand == 0 and n_parts * cand * c * 4 <= budget:
            tr = cand
            break
    if n_parts * tr * c * 4 > budget:
        tc = max(t for t in range(LANES, c + 1, LANES) if c % t == 0 and n_parts * r * t * 4 <= budget)

    def body(p_ref, w_ref, m_ref, v_ref, g_ref, d_ref, m2_ref, v2_ref):
        g = p_ref[0].astype(F32)
        for s in range(1, n_parts):
            g = g + p_ref[s].astype(F32)
        g_ref[...] = g
        d_ref[...], m2_ref[...], v2_ref[...] = _adamw_vals(w_ref[...], g, m_ref[...], v_ref[...])

    blk = pl.BlockSpec((tr, tc), lambda i: (i, 0)) if tc == c else pl.BlockSpec((tr, tc), lambda i: (0, i))
    parts_blk = (pl.BlockSpec((n_parts, tr, tc), lambda i: (0, i, 0)) if tc == c
                 else pl.BlockSpec((n_parts, tr, tc), lambda i: (0, 0, i)))
    return pl.pallas_call(
        body, grid=(r // tr if tc == c else c // tc,), name=name,
        in_specs=[parts_blk, blk, blk, blk],
        out_specs=[blk] * 4, out_shape=[SDS((r, c), F32)] * 4,
        compiler_params=pltpu.CompilerParams(dimension_semantics=("arbitrary",), vmem_limit_bytes=VMEM_LIMIT),
    )(parts, w, m, v)


def _peers():
    x, y, c = lax.axis_index("x"), lax.axis_index("y"), lax.axis_index("c")
    peers = []
    for k in range(1, N_DEV):
        px = 1 - x if k & 4 else x
        py = 1 - y if k & 2 else y
        pc = 1 - c if k & 1 else c
        peers.append(((px, py, pc), 4 * px + 2 * py + pc))
    return 4 * x + 2 * y + c, peers


def _slot(ref, idx, cols):
    if cols is None:
        return ref.at[idx]
    return ref.at[:, pl.ds(pl.multiple_of(idx * cols, LANES), cols)]


def _gather_two_level(name, srcs, dsts):
    n = len(srcs)
    dst_cols = [c for _, _, c in dsts]

    def body(*refs):
        src_refs, out_refs = refs[:n], refs[n:2 * n]
        send_sems, recv_sems, local_sems = refs[2 * n:]
        x, y, c = lax.axis_index("x"), lax.axis_index("y"), lax.axis_index("c")
        index = lambda px, py, pc: 4 * px + 2 * py + pc
        me, sibling = index(x, y, c), (x, y, 1 - c)
        chips = [(x, 1 - y), (1 - x, y), (1 - x, 1 - y)]

        def copy(a, k, src, block, to):
            return pltpu.make_async_remote_copy(
                src_ref=src, dst_ref=_slot(out_refs[a], block, dst_cols[a]),
                send_sem=send_sems.at[a, k], recv_sem=recv_sems.at[a, k],
                device_id=to, device_id_type=pl.DeviceIdType.MESH)

        local, first, passed = [], [], []
        for a in range(n):
            cp = pltpu.make_async_copy(src_refs[a], _slot(out_refs[a], me, dst_cols[a]), local_sems.at[a])
            cp.start()
            local.append(cp)
            first.append(copy(a, 0, src_refs[a], me, sibling))
            first += [copy(a, 1 + j, src_refs[a], me, (*chip, c)) for j, chip in enumerate(chips)]
        for cp in first:
            cp.start()
        for a in range(n):
            for j, chip in enumerate(chips):
                block = index(*chip, c)
                arrived = _slot(out_refs[a], block, dst_cols[a])
                copy(a, 1 + j, arrived, block, (*chip, c)).wait_recv()
                passed.append(copy(a, 4 + j, arrived, block, sibling))
                passed[-1].start()
        for a in range(n):
            copy(a, 0, src_refs[a], index(x, y, 1 - c), sibling).wait_recv()
            for j, chip in enumerate(chips):
                block = index(*chip, 1 - c)
                copy(a, 4 + j, src_refs[a], block, sibling).wait_recv()
        for cp in first + passed:
            cp.wait_send()
        for cp in local:
            cp.wait()

    any_spec = pl.BlockSpec(memory_space=pl.ANY)
    return pl.pallas_call(
        body, name=name,
        in_specs=[any_spec] * n, out_specs=[any_spec] * n,
        out_shape=[SDS(shape, dt) for shape, dt, _ in dsts],
        scratch_shapes=_exchange_sems(n),
    )(*[a for a, _ in srcs])


_HBM = pl.BlockSpec(memory_space=pltpu.HBM)
_SEM = pl.BlockSpec(memory_space=pltpu.SEMAPHORE)
_EFFECT = pltpu.SideEffectType.DATAFLOW_SIDE_EFFECTING


def _split_copies(src_cols, dst_cols, gather, chips, src_refs, land_refs, send_sems, recv_sems, landings):
    me, peers = _peers()
    if chips:
        me, peers = me // 2, [(pos, idx // 2) for k, (pos, idx) in enumerate(peers) if (k + 1) in (2, 4, 6)]
    n, width = len(src_cols), len(peers)
    remote, local = [], []
    for a, (s_cols, d_cols) in enumerate(zip(src_cols, dst_cols)):
        mine = src_refs[a] if gather else _slot(src_refs[a], me, s_cols)
        local.append(pltpu.make_async_copy(mine, _slot(land_refs[a], me, d_cols), send_sems.at[n * width + a]))
        for k, (pos, idx) in enumerate(peers):
            blk = src_refs[a] if gather else _slot(src_refs[a], idx, s_cols)
            remote.append(pltpu.make_async_remote_copy(
                src_ref=blk, dst_ref=_slot(land_refs[a], idx if landings else me, d_cols),
                send_sem=send_sems.at[a * width + k], recv_sem=recv_sems.at[a * width + k],
                device_id=pos, device_id_type=pl.DeviceIdType.MESH))
    return remote, local


def _exchange_start(name, srcs, dsts, gather, after, chips=False):
    n = len(srcs)
    src_cols, dst_cols = [c for _, c in srcs], [c for _, _, c in dsts]
    width = 3 if chips else N_DEV - 1

    def body(*refs):
        src_refs, land_refs = refs[:n], refs[n:2 * n]
        send_sems, recv_sems = refs[2 * n + 1:2 * n + 3]
        token = refs[-1]
        remote, local = _split_copies(src_cols, dst_cols, gather, chips, src_refs, land_refs, send_sems, recv_sems,
                                      False)
        for cp in remote + local:
            cp.start()
        token[...] = jnp.zeros_like(token)

    hbm = lambda a: pltpu.with_memory_space_constraint(a, pltpu.HBM)
    lands = [hbm(lax.empty(shape, dt)) for shape, dt, _ in dsts]
    res = pl.pallas_call(
        body, name=name,
        out_shape=(pltpu.SemaphoreType.DMA((n * (width + 1),)), pltpu.SemaphoreType.DMA((n * width,)),
                   *[pltpu.HBM(a.shape, a.dtype) for a, _ in srcs], *[pltpu.HBM(a.shape, a.dtype) for a in lands],
                   SDS((8, LANES), F32)),
        in_specs=[_HBM] * (2 * n) + [pl.BlockSpec(memory_space=pl.ANY)],
        out_specs=(_SEM, _SEM, *[_HBM] * (2 * n), pl.BlockSpec(memory_space=pltpu.VMEM)),
        input_output_aliases={i: 2 + i for i in range(2 * n)},
        compiler_params=pltpu.CompilerParams(has_side_effects=_EFFECT),
    )(*[hbm(a) for a, _ in srcs], *lands, after)
    handle = (res[0], res[1], res[2:2 + n], res[2 + n:2 + 2 * n], src_cols, dst_cols, gather, chips)
    return handle, res[-1]


def _exchange_wait(name, handle, after):
    send_sems, recv_sems, src_thru, land_thru, src_cols, dst_cols, gather, chips = handle
    n = len(src_thru)

    def body(*refs):
        src_refs, land_refs = refs[:n], refs[n:2 * n]
        s_sems, r_sems = refs[2 * n:2 * n + 2]
        remote, local = _split_copies(src_cols, dst_cols, gather, chips, src_refs, land_refs, s_sems, r_sems, True)
        for cp in remote:
            cp.wait_send()
            cp.wait_recv()
        for cp in local:
            cp.wait()

    res = pl.pallas_call(
        body, name=name,
        out_shape=tuple(pltpu.HBM(a.shape, a.dtype) for a in (*src_thru, *land_thru)),
        in_specs=[_HBM] * (2 * n) + [_SEM, _SEM, pl.BlockSpec(memory_space=pl.ANY)],
        out_specs=tuple([_HBM] * (2 * n)),
        input_output_aliases={i: i for i in range(2 * n)},
        compiler_params=pltpu.CompilerParams(has_side_effects=_EFFECT),
    )(*src_thru, *land_thru, send_sems, recv_sems, after)
    return res[n:]


def _pair_swap(name, arrs):
    n = len(arrs)

    def body(*refs):
        src_refs, out_refs = refs[:n], refs[n:2 * n]
        send_sems, recv_sems = refs[2 * n:]
        x, y, c = lax.axis_index("x"), lax.axis_index("y"), lax.axis_index("c")
        copies = [pltpu.make_async_remote_copy(
            src_ref=src_refs[a].at[:, 1 - c], dst_ref=out_refs[a], send_sem=send_sems.at[a], recv_sem=recv_sems.at[a],
            device_id=(x, y, 1 - c), device_id_type=pl.DeviceIdType.MESH) for a in range(n)]
        for cp in copies:
            cp.start()
        for cp in copies:
            cp.wait()

    any_spec = pl.BlockSpec(memory_space=pl.ANY)
    return pl.pallas_call(
        body, name=name,
        in_specs=[any_spec] * n, out_specs=[any_spec] * n,
        out_shape=[SDS((a.shape[0],) + a.shape[2:], a.dtype) for a in arrs],
        scratch_shapes=[pltpu.SemaphoreType.DMA((n,)), pltpu.SemaphoreType.DMA((n,))],
    )(*arrs)


def _pair_add(name, mine, theirs):
    four, _, r, c = mine.shape
    tr = r
    for cand in (512, 256, 128, 64, 32, 16, 8):
        if r % cand == 0:
            tr = cand
            break
    tc = max(t for t in range(LANES, c + 1, LANES) if c % t == 0 and (t == LANES or 2 * tr * t * 4 <= 4 * 1024 * 1024))

    def body(m_ref, t_ref, o_ref):
        core = lax.axis_index("c")
        both = m_ref[...].astype(F32)
        own = jnp.where(core == 0, both[0], both[1])
        o_ref[...] = (own + t_ref[...].astype(F32)).astype(o_ref.dtype)

    return pl.pallas_call(
        body, grid=(four, r // tr, c // tc), name=name,
        in_specs=[pl.BlockSpec((None, 2, tr, tc), lambda i, j, k: (i, 0, j, k)),
                  pl.BlockSpec((None, tr, tc), lambda i, j, k: (i, j, k))],
        out_specs=pl.BlockSpec((None, tr, tc), lambda i, j, k: (i, j, k)),
        out_shape=SDS(theirs.shape, theirs.dtype),
        compiler_params=pltpu.CompilerParams(dimension_semantics=("arbitrary",) * 3, vmem_limit_bytes=VMEM_LIMIT),
    )(mine, theirs)


def _my_index():
    return 4 * lax.axis_index("x") + 2 * lax.axis_index("y") + lax.axis_index("c")


def _two_level_copies(stage, dst_cols, src_refs, land_refs, send_sems, recv_sems, landings):
    x, y, c = lax.axis_index("x"), lax.axis_index("y"), lax.axis_index("c")

    def pos(k):
        return (1 - x if k & 4 else x, 1 - y if k & 2 else y, 1 - c if k & 1 else c)

    def idx(k):
        px, py, pc = pos(k)
        return 4 * px + 2 * py + pc

    out = []
    for a, cols in enumerate(dst_cols):
        if stage == 1:
            for i, k in enumerate((1, 2, 4, 6)):
                out.append(pltpu.make_async_remote_copy(
                    src_ref=src_refs[a], dst_ref=_slot(land_refs[a], idx(k) if landings else idx(0), cols),
                    send_sem=send_sems.at[4 * a + i], recv_sem=recv_sems.at[4 * a + i],
                    device_id=pos(k), device_id_type=pl.DeviceIdType.MESH))
        else:
            for i, k in enumerate((2, 4, 6)):
                out.append(pltpu.make_async_remote_copy(
                    src_ref=_slot(land_refs[a], idx(k), cols),
                    dst_ref=_slot(land_refs[a], idx(k ^ 1) if landings else idx(k), cols),
                    send_sem=send_sems.at[3 * a + i], recv_sem=recv_sems.at[3 * a + i],
                    device_id=pos(1), device_id_type=pl.DeviceIdType.MESH))
    return out


def _gather2_start(name, srcs, dsts, after):
    n = len(srcs)
    dst_cols = [c for _, _, c in dsts]

    def body(*refs):
        src_refs, land_refs = refs[:n], refs[n:2 * n]
        send_sems, recv_sems = refs[2 * n + 1:2 * n + 3]
        me = _my_index()
        for a in range(n):
            pltpu.make_async_copy(src_refs[a], _slot(land_refs[a], me, dst_cols[a]), send_sems.at[4 * n + a]).start()
        for cp in _two_level_copies(1, dst_cols, src_refs, land_refs, send_sems, recv_sems, False):
            cp.start()
        refs[-1][...] = jnp.zeros_like(refs[-1])

    hbm = lambda a: pltpu.with_memory_space_constraint(a, pltpu.HBM)
    lands = [hbm(lax.empty(shape, dt)) for shape, dt, _ in dsts]
    res = pl.pallas_call(
        body, name=name,
        out_shape=(pltpu.SemaphoreType.DMA((5 * n,)), pltpu.SemaphoreType.DMA((4 * n,)),
                   *[pltpu.HBM(a.shape, a.dtype) for a, _ in srcs], *[pltpu.HBM(a.shape, a.dtype) for a in lands],
                   SDS((8, LANES), F32)),
        in_specs=[_HBM] * (2 * n) + [pl.BlockSpec(memory_space=pl.ANY)],
        out_specs=(_SEM, _SEM, *[_HBM] * (2 * n), pl.BlockSpec(memory_space=pltpu.VMEM)),
        input_output_aliases={i: 2 + i for i in range(2 * n)},
        compiler_params=pltpu.CompilerParams(has_side_effects=_EFFECT),
    )(*[hbm(a) for a, _ in srcs], *lands, after)
    return (res[0], res[1], res[2:2 + n], res[2 + n:2 + 2 * n], dst_cols), res[-1]


def _gather2_pass(name, handle, after):
    send1, recv1, src_thru, land_thru, dst_cols = handle
    n = len(src_thru)

    def body(*refs):
        src_refs, land_refs = refs[:n], refs[n:2 * n]
        s1, r1 = refs[2 * n:2 * n + 2]
        send2, recv2 = refs[2 * n + 3:2 * n + 5]
        me = _my_index()
        for cp in _two_level_copies(1, dst_cols, src_refs, land_refs, s1, r1, True):
            cp.wait_send()
            cp.wait_recv()
        for a in range(n):
            pltpu.make_async_copy(src_refs[a], _slot(land_refs[a], me, dst_cols[a]), s1.at[4 * n + a]).wait()
        for cp in _two_level_copies(2, dst_cols, src_refs, land_refs, send2, recv2, False):
            cp.start()
        refs[-1][...] = jnp.zeros_like(refs[-1])

    res = pl.pallas_call(
        body, name=name,
        out_shape=(pltpu.SemaphoreType.DMA((3 * n,)), pltpu.SemaphoreType.DMA((3 * n,)),
                   *[pltpu.HBM(a.shape, a.dtype) for a in (*src_thru, *land_thru)], SDS((8, LANES), F32)),
        in_specs=[_HBM] * (2 * n) + [_SEM, _SEM, pl.BlockSpec(memory_space=pl.ANY)],
        out_specs=(_SEM, _SEM, *[_HBM] * (2 * n), pl.BlockSpec(memory_space=pltpu.VMEM)),
        input_output_aliases={i: 2 + i for i in range(2 * n)},
        compiler_params=pltpu.CompilerParams(has_side_effects=_EFFECT),
    )(*src_thru, *land_thru, send1, recv1, after)
    return (res[0], res[1], res[2:2 + n], res[2 + n:2 + 2 * n], dst_cols), res[-1]


def _gather2_wait(name, handle, after):
    send2, recv2, src_thru, land_thru, dst_cols = handle
    n = len(src_thru)

    def body(*refs):
        src_refs, land_refs = refs[:n], refs[n:2 * n]
        s2, r2 = refs[2 * n:2 * n + 2]
        for cp in _two_level_copies(2, dst_cols, src_refs, land_refs, s2, r2, True):
            cp.wait_send()
            cp.wait_recv()

    res = pl.pallas_call(
        body, name=name,
        out_shape=tuple(pltpu.HBM(a.shape, a.dtype) for a in (*src_thru, *land_thru)),
        in_specs=[_HBM] * (2 * n) + [_SEM, _SEM, pl.BlockSpec(memory_space=pl.ANY)],
        out_specs=tuple([_HBM] * (2 * n)),
        input_output_aliases={i: i for i in range(2 * n)},
        compiler_params=pltpu.CompilerParams(has_side_effects=_EFFECT),
    )(*src_thru, *land_thru, send2, recv2, after)
    return res[n:]


def _exchange_sems(n):
    return [pltpu.SemaphoreType.DMA((n, N_DEV - 1)), pltpu.SemaphoreType.DMA((n, N_DEV - 1)),
            pltpu.SemaphoreType.DMA((n,))]


def _rms_res_fn(h, w):
    return _rms_fn(h, w)[0], h


def _add_epilogue(acc, res):
    return (acc + res,)


def _gather_plan(shards):
    srcs, dsts = [], []
    for n, sh in shards.items():
        r, c = sh.shape
        srcs.append((sh, None))
        if SHARDED[n] and c % LANES == 0:
            dsts.append(((r, N_DEV * c), sh.dtype, c))
        else:
            dsts.append(((N_DEV, r, c), sh.dtype, None))
    return srcs, dsts, True


def _w_in_segments():
    out = []
    for j in range(N_DEV):
        lo, hi = W_IN_SHARD * j, W_IN_SHARD * (j + 1)
        for a, b in ((lo, min(hi, DN_COLS)), (max(lo, DN_COLS), hi)):
            if a < b:
                out.append((j, a - lo, b - lo, a if a < DN_COLS else a + RW_OFF - DN_COLS))
    return out


def _w_in_to_padded(shards, tc=512):
    _, _, cols = shards.shape

    def body(g_ref, o_ref):
        o_ref[...] = jnp.zeros_like(o_ref)
        for j, a, b, dst in _w_in_segments():
            o_ref[dst:dst + b - a, :] = g_ref[j, a:b, :]

    return pl.pallas_call(
        body, grid=(cols // tc,), name="w_in_to_padded",
        in_specs=[pl.BlockSpec((N_DEV, W_IN_SHARD, tc), lambda i: (0, 0, i))],
        out_specs=pl.BlockSpec((IN_PAD, tc), lambda i: (0, i)),
        out_shape=SDS((IN_PAD, cols), shards.dtype),
        compiler_params=pltpu.CompilerParams(dimension_semantics=("arbitrary",), vmem_limit_bytes=VMEM_LIMIT),
    )(shards)


def _w_in_grad_to_shards(gw, tc=512):
    _, cols = gw.shape

    def body(w_ref, o_ref):
        for j, a, b, dst in _w_in_segments():
            o_ref[j, a:b, :] = w_ref[dst:dst + b - a, :]

    return pl.pallas_call(
        body, grid=(cols // tc,), name="w_in_grad_to_shards",
        in_specs=[pl.BlockSpec((IN_PAD, tc), lambda i: (0, i))],
        out_specs=pl.BlockSpec((N_DEV, W_IN_SHARD, tc), lambda i: (0, 0, i)),
        out_shape=SDS((N_DEV, W_IN_SHARD, cols), gw.dtype),
        compiler_params=pltpu.CompilerParams(dimension_semantics=("arbitrary",), vmem_limit_bytes=VMEM_LIMIT),
    )(gw)


def _gather_finish(names, outs):
    full = {}
    for n, arr in zip(names, outs):
        if n == "w_in":
            full[n] = _w_in_to_padded(arr)
        elif arr.ndim == 2:
            full[n] = arr
        elif SHARDED[n]:
            full[n] = arr.transpose(1, 0, 2).reshape(arr.shape[1], -1)
        else:
            full[n] = arr.reshape(-1, arr.shape[2])
    return full


def _scatter_plan(grads):
    srcs, dsts = [], []
    for n, gr in grads.items():
        if gr.ndim == 3:
            srcs.append((gr, None))
            dsts.append((gr.shape, gr.dtype, None))
            continue
        rows, cols = gr.shape
        if not SHARDED[n]:
            r, c = rows // N_DEV, cols
            srcs.append((gr.reshape(N_DEV, r, c), None))
        else:
            r, c = rows, cols // N_DEV
            if c % LANES == 0:
                srcs.append((gr, c))
            else:
                srcs.append((gr.reshape(r, N_DEV, c).transpose(1, 0, 2), None))
        dsts.append(((N_DEV, r, c), gr.dtype, None))
    return srcs, dsts, False


def _local_step(x, mem, target, wt, late):
    d = D_MODEL
    g = {}
    wt = dict(wt)
    grp_a = ("w_out", "xa_wq", "xa_wk", "xa_wv", "xa_wo")
    grp_b = ("ffn_w1", "ffn_w2")
    plan = lambda names: _gather_plan({n: late[n] for n in names})[:2]
    handle_a, tok_a = _gather2_start("late_gather_a_start", *plan(grp_a), wt["w_in"])
    handle_w1, tok_b = _gather2_start("late_gather_w1_start", *plan(("ffn_w1",)), tok_a)
    handle_w2, tok_c = _gather2_start("late_gather_w2_start", *plan(("ffn_w2",)), tok_b)
    mix_w = wt["mix_norm_w"] + (tok_a[0:1, 0:1] + tok_b[0:1, 0:1] + tok_c[0:1, 0:1])
    u = _row_fwd(_rms_fn, "mix_norm", [(x, d, 0)], [mix_w], [(d, BF16)], 256)[0]
    p = _matmul("in_proj", u, wt["w_in"], "nt", [F32], tn=1536)[0]
    c = _col_fwd(_conv_fn, "dn_conv", p, 0, 24, [wt["dn_conv_w"]])
    handle_a, tok = _gather2_pass("late_gather_a_pass", handle_a, c)
    dn_pre_tiles = [(c, DN_WIDTH, 0), (c, DN_WIDTH, 1), (p, LANES, 32)]
    dn_pre_params = [wt["dn_a_log"], wt["dn_dt_bias"]]
    qh, kh, gb, bb, gcb = _row_fwd(_dn_pre_fn, "dn_pre", dn_pre_tiles, [dn_pre_params[0] + tok[0:1, :], dn_pre_params[1]],
                                   [(DN_WIDTH, F32)] * 5, CHUNK)
    dn_arrs = [(qh, 0), (kh, 0), (c, 16), (gb, 0), (bb, 0), (gcb, 0)]
    o, kept_dn = _scan_fwd(_gdn_group, "gdn_scan", dn_arrs, DN_HEADS, 1)
    dn_post_tiles = [(o, DN_WIDTH, 0), (p, DN_WIDTH, 3)]
    o_dn = _row_fwd(_dn_post_fn, "dn_post", dn_post_tiles, [wt["dn_norm_w"]], [(DN_WIDTH, BF16)], 256)[0]

    ps = _col_fwd(_lerp_fn, "rw_shift", p, RW_OFF // LANES, 26, [wt["rw_mu"]])
    rw_pre_tiles = [(ps, RW_WIDTH, 0), (ps, RW_WIDTH, 1), (ps, RW_WIDTH, 2), (ps, LANES, 24), (ps, LANES, 25)]
    rw_pre_params = [wt[n] for n in ("rw_w0", "rw_a0", "rw_k_k", "rw_k_a", "rw_w2", "rw_a2", "rw_g2")]
    r, lw, k, v, al, be, gate, gcw = _row_fwd(_rw_pre_fn, "rw_pre", rw_pre_tiles, rw_pre_params,
                                              [(RW_WIDTH, F32)] * 8, CHUNK)
    rw_arrs = [(r, 0), (lw, 0), (k, 0), (v, 0), (al, 0), (be, 0), (gcw, 0)]
    y, kept_rw = _scan_fwd(_rw_group, "rw_scan", rw_arrs, RW_WIDTH // LANES, 2)
    handle_w1, tok = _gather2_pass("late_gather_w1_pass", handle_w1, y)
    rw_post_tiles = [(t, RW_WIDTH, 0) for t in (y, r, k, v, gate)]
    rw_post_params = [wt["rw_ln_w"], wt["rw_ln_b"], wt["rw_r_k"]]
    o_rw = _row_fwd(_rw_post_fn, "rw_post", rw_post_tiles, [rw_post_params[0] + tok[0:1, 0:1]] + rw_post_params[1:],
                    [(RW_WIDTH, BF16)], 128)[0]
    o_cat = jnp.concatenate([o_dn, o_rw], axis=1)
    wt.update(_gather_finish(grp_a, _gather2_wait("late_gather_a_wait", handle_a, o_cat)))
    h1 = _matmul("out_proj", o_cat, wt["w_out"], "nn", [F32], _add_epilogue, (x,))[0]

    handle_w2, tok = _gather2_pass("late_gather_w2_pass", handle_w2, h1)
    hn = _row_fwd(_rms_fn, "xa_norm", [(h1, d, 0)], [wt["xa_norm_w"] + tok[0:1, 0:1]], [(d, BF16)], 256)[0]
    mn = _row_fwd(_rms_fn, "mem_norm", [(mem, d, 0)], [wt["mem_norm_w"]], [(d, BF16)], 256)[0]
    q = _matmul("xa_q", hn, wt["xa_wq"], "nn", [F32])[0]
    kx = _matmul("xa_k", mn, wt["xa_wk"], "nn", [F32])[0]
    vx = _matmul("xa_v", mn, wt["xa_wv"], "nn", [F32])[0]
    ao = _row_fwd(_xattn_fn, "xattn", [(q, XA_WIDTH, 0)], [kx, vx], [(XA_WIDTH, BF16)], 256)[0]
    h2 = _matmul("xa_o", ao, wt["xa_wo"], "nn", [F32], _add_epilogue, (h1,))[0]

    f = _row_fwd(_rms_fn, "ffn_norm", [(h2, d, 0)], [wt["ffn_norm_w"]], [(d, BF16)], 256)[0]
    wt.update(_gather_finish(("ffn_w1",), _gather2_wait("late_gather_w1_wait", handle_w1, f)))
    a, hid = _matmul("ffn_up", f, wt["ffn_w1"], "nn", [F32, BF16],
                     lambda acc: (acc, jnp.square(jnp.maximum(acc, 0.0))))
    wt.update(_gather_finish(("ffn_w2",), _gather2_wait("late_gather_w2_wait", handle_w2, hid)))
    h3 = _matmul("ffn_down", hid, wt["ffn_w2"], "nn", [F32], _add_epilogue, (h2,))[0]
    loss8, dh3, g["final_norm_w"] = _loss_call(h3, target, wt["final_norm_w"])

    da = _matmul("ffn_down_dx", dh3, wt["ffn_w2"], "nt", [BF16],
                 lambda acc, av: (acc * 2.0 * jnp.maximum(av, 0.0),), (a,))[0]
    g["ffn_w2"] = _matmul("ffn_down_dw", hid, dh3, "tn", [BF16])[0]
    g["ffn_w1"] = _matmul("ffn_up_dw", f, da, "tn", [BF16])[0]
    pending = {}
    plan = _scatter_plan({n: g.pop(n) for n in grp_b})
    pending[grp_b], tok = _exchange_start("late_grad_b_start", *plan, loss8)
    dh2, g["ffn_norm_w"] = _matmul_norm_bwd("ffn_up_dx", da, wt["ffn_w1"], "nt", h2, wt["ffn_norm_w"], dh3, tok)

    dao = _matmul("xa_o_dx", dh2, wt["xa_wo"], "nt", [F32])[0]
    g["xa_wo"] = _matmul("xa_o_dw", ao, dh2, "tn", [BF16])[0]
    (dq,), (dkx, dvx) = _row_bwd(_xattn_fn, "xattn_bwd", [(q, XA_WIDTH, 0)], [kx, vx], [[(dao, XA_WIDTH, 0)]], 256)
    dh1, g["xa_norm_w"] = _matmul_norm_bwd("xa_q_dx", dq, wt["xa_wq"], "nt", h1, wt["xa_norm_w"], dh2)
    g["xa_wq"] = _matmul("xa_q_dw", hn, dq, "tn", [BF16])[0]
    g["xa_wk"] = _matmul("xa_k_dw", mn, dkx, "tn", [BF16])[0]
    g["xa_wv"] = _matmul("xa_v_dw", mn, dvx, "tn", [BF16])[0]
    dmn = _matmul("xa_k_dx", dkx, wt["xa_wk"], "nt", [F32])[0]
    dmn = _matmul("xa_v_dx", dvx, wt["xa_wv"], "nt", [F32], _add_epilogue, (dmn,))[0]
    _, (g["mem_norm_w"],) = _row_bwd(_rms_fn, "mem_norm_bwd", [(mem, d, 0)], [wt["mem_norm_w"]],
                                     [[(dmn, d, 0)]], 256, want_tiles=())

    do_cat = _matmul("out_proj_dx", dh1, wt["w_out"], "nt", [F32])[0]
    g["w_out"] = _matmul("out_proj_dw", o_cat, dh1, "tn", [BF16])[0]

    plan = _scatter_plan({n: g.pop(n) for n in grp_a})
    pending[grp_a], tok = _exchange_start("late_grad_a_start", *plan, tok)
    (dy, dr1, dk1, dv1, dgate), (g["rw_ln_w"], g["rw_ln_b"], g["rw_r_k"]) = _row_bwd(
        _rw_post_fn, "rw_post_bwd", rw_post_tiles, [rw_post_params[0] + tok[0:1, 0:1]] + rw_post_params[1:],
        [[(do_cat, RW_WIDTH, 1)]], 128)
    dr2, dlw, dk2, dv2, dal, dbe, dgcw = _scan_bwd(_rw_group, "rw_scan_bwd", rw_arrs, kept_rw, dy,
                                                   RW_WIDTH // LANES)
    one = lambda t: [(t, RW_WIDTH, 0)]
    two = lambda s, t: [(s, RW_WIDTH, 0), (t, RW_WIDTH, 0)]
    d_ps, rw_pre_grads = _row_bwd(
        _rw_pre_fn, "rw_pre_bwd", rw_pre_tiles, rw_pre_params,
        [two(dr1, dr2), one(dlw), two(dk1, dk2), two(dv1, dv2), one(dal), one(dbe), one(dgate), one(dgcw)],
        CHUNK)
    for n, val in zip(("rw_w0", "rw_a0", "rw_k_k", "rw_k_a", "rw_w2", "rw_a2", "rw_g2"), rw_pre_grads):
        g[n] = val
    dp_rw, (g["rw_mu"],) = _col_bwd(_lerp_fn, "rw_shift_bwd", p, RW_OFF // LANES, 26, [wt["rw_mu"]], list(d_ps))

    (do, dz), (g["dn_norm_w"],) = _row_bwd(_dn_post_fn, "dn_post_bwd", dn_post_tiles, [wt["dn_norm_w"]],
                                           [[(do_cat, DN_WIDTH, 0)]], 256)
    dqh, dkh, dv_dn, dgb, dbb, dgcb = _scan_bwd(_gdn_group, "gdn_scan_bwd", dn_arrs, kept_dn, do, DN_HEADS)
    one = lambda t: [(t, DN_WIDTH, 0)]
    (dcq, dck, dgates), (g["dn_a_log"], g["dn_dt_bias"]) = _row_bwd(
        _dn_pre_fn, "dn_pre_bwd", dn_pre_tiles, dn_pre_params,
        [one(dqh), one(dkh), one(dgb), one(dbb), one(dgcb)], CHUNK)
    dp_qkv, (g["dn_conv_w"],) = _col_bwd(_conv_fn, "dn_conv_bwd", p, 0, 24, [wt["dn_conv_w"]], [dcq, dck, dv_dn])
    dp = jnp.concatenate([t.astype(BF16) for t in (dp_qkv, dz, dgates, dp_rw, jnp.zeros((x.shape[0], LANES), F32))],
                         axis=1)
    g["w_in"] = _matmul("in_proj_dw", dp, u, "tn", [BF16], tm=1536)[0]
    early = _logical_grads(g)
    blocks = []
    for src, cols in _scatter_plan({n: early.pop(n) for n in EARLY})[0]:
        if cols is not None:
            src = src.reshape(src.shape[0], N_DEV, cols).transpose(1, 0, 2)
        blocks.append(src.reshape((4, 2) + src.shape[1:]))
    sums = [_pair_add("early_grad_pair_add_%d" % i, mine, theirs)
            for i, (mine, theirs) in enumerate(zip(blocks, _pair_swap("early_grad_pair_swap", blocks)))]
    pending[EARLY], tok = _exchange_start("early_grad_start", [(t, None) for t in sums],
                                          [(t.shape, t.dtype, None) for t in sums], False, tok, chips=True)
    dx, early["mix_norm_w"] = _matmul_norm_bwd("in_proj_dx", dp, wt["w_in"], "nn", x, wt["mix_norm_w"], dh1, tok)
    return loss8, dx, early, pending, tok


WEIGHTS = ["mix_norm_w", "w_in", "dn_conv_w", "dn_a_log", "dn_dt_bias", "dn_norm_w", "rw_mu", "rw_w0", "rw_w2",
           "rw_a0", "rw_a2", "rw_g2", "rw_k_k", "rw_k_a", "rw_r_k", "rw_ln_w", "rw_ln_b", "w_out", "xa_norm_w",
           "mem_norm_w", "xa_wq", "xa_wk", "xa_wv", "xa_wo", "ffn_norm_w", "ffn_w1", "ffn_w2", "final_norm_w"]
SHARDED = {"w_in": False, "w_out": False, "xa_wq": False, "xa_wk": False, "xa_wv": False, "xa_wo": True,
           "ffn_w1": True, "ffn_w2": False, "dn_conv_w": True, "rw_w2": True, "rw_a2": True, "rw_g2": True}
BF16_PAYLOAD = ("w_in", "w_out", "xa_wq", "xa_wk", "xa_wv", "xa_wo", "ffn_w1", "ffn_w2")
REPLICATED = [n for n in WEIGHTS if n not in SHARDED]
EARLY = ("w_in", "dn_conv_w", "rw_w2", "rw_a2", "rw_g2")
RW_IN_COLS = IN_COLS - DN_COLS
W_IN_SHARD = IN_COLS // N_DEV


def _layout_weights(fw):
    wt = dict(fw)
    wt["dn_conv_w"] = jnp.pad(fw["dn_conv_w"], ((0, 4), (0, 0)))
    wt["dn_a_log"] = jnp.pad(fw["dn_a_log"], ((0, 0), (0, LANES - DN_HEADS)))
    wt["dn_dt_bias"] = jnp.pad(fw["dn_dt_bias"], ((0, 0), (0, LANES - DN_HEADS)))
    wt["rw_w2"] = jnp.pad(fw["rw_w2"], ((0, 64), (0, 0)))
    wt["rw_a2"] = jnp.pad(fw["rw_a2"], ((64, 0), (0, 0)))
    return wt


def _logical_grads(g):
    out = dict(g)
    out["w_in"] = _w_in_grad_to_shards(g["w_in"])
    out["dn_conv_w"] = g["dn_conv_w"][:4]
    out["dn_a_log"] = g["dn_a_log"][:, :DN_HEADS]
    out["dn_dt_bias"] = g["dn_dt_bias"][:, :DN_HEADS]
    out["rw_w2"] = g["rw_w2"][:64]
    out["rw_a2"] = g["rw_a2"][64:]
    return out


def _pack(vals):
    parts = []
    for v in vals:
        flat = v.reshape(-1)
        parts.append(jnp.pad(flat, (0, -flat.shape[0] % LANES)))
    flat = jnp.concatenate(parts)
    flat = jnp.pad(flat, (0, -flat.shape[0] % (8 * LANES)))
    return flat.reshape(-1, LANES)


def _unpack(packed, shapes):
    flat = packed.reshape(-1)
    out, at = [], 0
    for shp in shapes:
        size = math.prod(shp)
        out.append(flat[at:at + size].reshape(shp))
        at += size + (-size % LANES)
    return out


def kernel(x, mem, mix_norm_w, w_in, dn_conv_w, dn_a_log, dn_dt_bias, dn_norm_w, rw_mu, rw_w0, rw_w2, rw_a0, rw_a2, rw_g2, rw_k_k, rw_k_a, rw_r_k, rw_ln_w, rw_ln_b, w_out, xa_norm_w, mem_norm_w, xa_wq, xa_wk, xa_wv, xa_wo, ffn_norm_w, ffn_w1, ffn_w2, final_norm_w, loss_target, m_mix_norm_w, m_w_in, m_dn_conv_w, m_dn_a_log, m_dn_dt_bias, m_dn_norm_w, m_rw_mu, m_rw_w0, m_rw_w2, m_rw_a0, m_rw_a2, m_rw_g2, m_rw_k_k, m_rw_k_a, m_rw_r_k, m_rw_ln_w, m_rw_ln_b, m_w_out, m_xa_norm_w, m_mem_norm_w, m_xa_wq, m_xa_wk, m_xa_wv, m_xa_wo, m_ffn_norm_w, m_ffn_w1, m_ffn_w2, m_final_norm_w, v_mix_norm_w, v_w_in, v_dn_conv_w, v_dn_a_log, v_dn_dt_bias, v_dn_norm_w, v_rw_mu, v_rw_w0, v_rw_w2, v_rw_a0, v_rw_a2, v_rw_g2, v_rw_k_k, v_rw_k_a, v_rw_r_k, v_rw_ln_w, v_rw_ln_b, v_w_out, v_xa_norm_w, v_mem_norm_w, v_xa_wq, v_xa_wk, v_xa_wv, v_xa_wo, v_ffn_norm_w, v_ffn_w1, v_ffn_w2, v_final_norm_w):
    given = dict(locals())
    w = {n: given[n] for n in WEIGHTS}
    m = {n: given["m_" + n] for n in WEIGHTS}
    v = {n: given["v_" + n] for n in WEIGHTS}

    local = {n: (lambda t: t[0].T) if n == "w_in" else (lambda t: t[0]) for n in SHARDED}
    shards = {n: (local[n](w[n]).astype(BF16) if n in BF16_PAYLOAD else local[n](w[n])) for n in SHARDED}
    srcs, dsts, _ = _gather_plan({n: shards[n] for n in EARLY})
    full = _gather_finish(EARLY, _gather_two_level("early_all_gather", srcs, dsts))
    for n in REPLICATED:
        full[n] = w[n].reshape(1, -1)

    loss8, dx, g, pending, after = _local_step(x[0], mem[0], loss_target[0], _layout_weights(full),
                                               {n: shards[n] for n in SHARDED if n not in EARLY})

    packed = _pack([g[n] for n in REPLICATED] + [loss8[:1, :1]])
    small, _ = _exchange_start("small_gather_start", [(packed, None)], [((N_DEV,) + packed.shape, F32, None)], True,
                               after)
    grad, delta, new_m, new_v = {}, {}, {}, {}
    done = [dx]

    def tie():
        return jnp.broadcast_to(sum(t[:1, :1] for t in done), (8, LANES))

    for names in sorted(pending, key=lambda names: names == EARLY):
        handle = pending[names]
        for n, parts in zip(names, _exchange_wait("grad_wait_" + names[0], handle, tie())):
            res = _sum_adamw("adamw_" + n, parts, local[n](w[n]), local[n](m[n]), local[n](v[n]))
            grad[n], delta[n], new_m[n], new_v[n] = [(t.T if n == "w_in" else t)[None] for t in res]
            done.append(res[1])

    (parts,) = _exchange_wait("small_gather_wait", small, tie())
    blank = [jnp.zeros((1, 1), F32)]
    res = _sum_adamw("adamw_small", parts, _pack([w[n] for n in REPLICATED] + blank),
                     _pack([m[n] for n in REPLICATED] + blank), _pack([v[n] for n in REPLICATED] + blank))
    shapes = [w[n].shape for n in REPLICATED] + [()]
    loss = _unpack(res[0], shapes)[-1]
    for store, packed_out in zip((grad, delta, new_m, new_v), res):
        for n, val in zip(REPLICATED, _unpack(packed_out, shapes)):
            store[n] = val

    return (loss, dx[None], *[grad[n] for n in WEIGHTS], *[delta[n] for n in WEIGHTS],
            *[new_m[n] for n in WEIGHTS], *[new_v[n] for n in WEIGHTS])
```

```python
import functools
import math

import jax
import jax.numpy as jnp
from jax import lax
from jax.experimental import pallas as pl
from jax.experimental.pallas import tpu as pltpu

F32 = jnp.float32
BF16 = jnp.bfloat16
SDS = jax.ShapeDtypeStruct

N_DEV = 8
D_MODEL = 2048
LANES = 128
CHUNK = 128
DN_HEADS = 8
DN_WIDTH = 1024
RW_WIDTH = 1024
RW_HEAD = 64
XA_HEADS = 4
XA_WIDTH = 512
FFN_HIDDEN = 8192
IN_COLS = 7440
DN_COLS = 4112
IN_PAD = 7680
RW_OFF = 4224
RMS_EPS = 1e-6
RW_GN_EPS = 64e-5
VMEM_LIMIT = 56 * 1024 * 1024

ADAM_LR = 0.001
ADAM_B1 = 0.9
ADAM_B2 = 0.999
ADAM_EPS = 1e-08
ADAM_WD = 0.01
ADAM_STEP = 10

_DIMS = {"nn": (((1,), (0,)), ((), ())), "nt": (((1,), (1,)), ((), ())), "tn": (((0,), (0,)), ((), ()))}


def _raw_dot(a, b, mode, hi):
    if hi:
        return lax.dot_general(a, b, _DIMS[mode], precision=lax.Precision.HIGHEST, preferred_element_type=F32)
    return lax.dot_general(a.astype(BF16), b.astype(BF16), _DIMS[mode], preferred_element_type=F32)


@functools.partial(jax.custom_vjp, nondiff_argnums=(2, 3))
def mm(a, b, mode="nn", hi=False):
    return _raw_dot(a, b, mode, hi)


def _mm_fwd(a, b, mode, hi):
    return _raw_dot(a, b, mode, hi), (a, b)


def _mm_bwd(mode, hi, res, g):
    a, b = res
    if mode == "nn":
        return _raw_dot(g, b, "nt", hi), _raw_dot(a, g, "tn", hi)
    if mode == "nt":
        return _raw_dot(g, b, "nn", hi), _raw_dot(g, a, "tn", hi)
    return _raw_dot(b, g, "nt", hi), _raw_dot(a, g, "nn", hi)


mm.defvjp(_mm_fwd, _mm_bwd)


def _shift_rows_raw(x, k):
    n = x.shape[0]
    rolled = pltpu.roll(x, k % n, axis=0)
    row = lax.broadcasted_iota(jnp.int32, x.shape, 0)
    keep = row >= k if k > 0 else row < n + k
    return jnp.where(keep, rolled, 0.0)


@functools.partial(jax.custom_vjp, nondiff_argnums=(1,))
def shift_rows(x, k):
    return _shift_rows_raw(x, k)


shift_rows.defvjp(lambda x, k: (_shift_rows_raw(x, k), None), lambda k, _, g: (_shift_rows_raw(g, -k),))


@jax.custom_vjp
def _sigmoid(x):
    return 1.0 / (1.0 + jnp.exp(-x))


def _sigmoid_fwd(x):
    s = 1.0 / (1.0 + jnp.exp(-x))
    return s, s


_sigmoid.defvjp(_sigmoid_fwd, lambda s, g: (g * s * (1.0 - s),))


@jax.custom_vjp
def _softplus(x):
    return jnp.maximum(x, 0.0) + jnp.log(1.0 + jnp.exp(-jnp.abs(x)))


_softplus.defvjp(lambda x: (_softplus(x), x), lambda x, g: (g / (1.0 + jnp.exp(-x)),))


@jax.custom_vjp
def _silu(x):
    return x / (1.0 + jnp.exp(-x))


def _silu_fwd(x):
    s = 1.0 / (1.0 + jnp.exp(-x))
    return x * s, (x, s)


_silu.defvjp(_silu_fwd, lambda res, g: (g * res[1] * (1.0 + res[0] * (1.0 - res[1])),))


def _tri_masks(n):
    ii = lax.broadcasted_iota(jnp.int32, (n, n), 0)
    jj = lax.broadcasted_iota(jnp.int32, (n, n), 1)
    return ii >= jj, ii > jj, ii == jj


def _neumann_inv_raw(m):
    n = m.shape[0]
    _, _, eye = _tri_masks(n)
    eye = jnp.where(eye, 1.0, 0.0)
    p = eye + m
    mk = m
    for _ in range(int(math.log2(n)) - 1):
        mk = _raw_dot(mk, mk, "nn", False)
        p = p + _raw_dot(p, mk, "nn", False)
    resid = eye - p + _raw_dot(m, p, "nn", True)
    return p + _raw_dot(p, resid, "nn", False)


@jax.custom_vjp
def _neumann_inv(m):
    return _neumann_inv_raw(m)


def _neumann_inv_fwd(m):
    p = _neumann_inv_raw(m)
    return p, p


def _neumann_inv_bwd(p, g):
    return (_raw_dot(_raw_dot(p, g, "tn", False), p, "nt", False),)


_neumann_inv.defvjp(_neumann_inv_fwd, _neumann_inv_bwd)


@jax.custom_vjp
def _saved_inv(m, p):
    return p


_saved_inv.defvjp(lambda m, p: (p, p), lambda p, g: (_neumann_inv_bwd(p, g)[0], jnp.zeros_like(p)))


def _inverse(m, saved):
    return _neumann_inv(m) if saved is None else _saved_inv(m, saved)


def _cumsum_rows(x):
    causal, _, _ = _tri_masks(x.shape[0])
    return mm(jnp.where(causal, 1.0, 0.0), x, "nn", True)


def _gdn_group(s0, q, k, v, gb, bb, gc, *saved):
    diff = jnp.stack([gc[j] - gc[j].T for j in range(gc.shape[0])])
    return jax.vmap(_gdn_chunk)(s0, q, k, v, gb, bb, gc, diff, *saved)


def _rw_group(*args):
    return jax.vmap(_rw_chunk)(*args)


def _gdn_chunk(s0, q, k, v, gb, bb, gc, diff, saved=None):
    c = q.shape[0]
    causal, strict, _ = _tri_masks(c)
    decay = jnp.exp(jnp.where(causal, diff, -jnp.inf))
    kb = k * bb
    a = jnp.where(strict, mm(kb, k, "nt") * decay, 0.0)
    p = _inverse(-a, saved)
    uw = mm(p, jnp.concatenate([v * bb, kb * jnp.exp(gc)], axis=1))
    u, w = uw[:, :LANES], uw[:, LANES:]
    attn = mm(q, k, "nt") * decay
    v_new = u - mm(w, s0)
    o = mm(q * jnp.exp(gc), s0) + mm(attn, v_new)
    g_last = jnp.sum(gb, axis=0, keepdims=True)
    s1 = s0 * jnp.exp(g_last) + mm(k * jnp.exp(g_last - gc), v_new, "tn")
    return o, s1, p


def _rw_chunk(s0, r, lw, k, v, al, be, gc, saved0=None, saved1=None):
    c = r.shape[0]
    causal, strict, _ = _tri_masks(c)
    gp = gc - lw
    row = lax.broadcasted_iota(jnp.int32, lw.shape, 0)
    lane = lax.broadcasted_iota(jnp.int32, lw.shape, 1)
    g_mid = jnp.sum(jnp.where(row < c // 2, lw, 0.0), axis=0, keepdims=True)
    g_last = jnp.sum(lw, axis=0, keepdims=True)
    e_n = jnp.exp(g_mid - gc)
    rg = r * jnp.exp(gc - g_mid)
    bg = be * jnp.exp(gp - g_mid)
    an = al * e_n
    kn = k * e_n
    bt = mm(be * jnp.exp(gp), s0, "nt")
    rt = mm(r * jnp.exp(gc), s0, "nt")
    us, ys, ps = [], [], []
    ank = jnp.concatenate([an, kn], axis=0)
    for h, saved in enumerate((saved0, saved1)):
        mine = (lane >= RW_HEAD) if h else (lane < RW_HEAD)
        from_b = mm(jnp.where(mine, bg, 0.0), ank, "nt")
        from_r = mm(jnp.where(mine, rg, 0.0), ank, "nt")
        a_ab = jnp.where(strict, from_b[:, :c], 0.0)
        a_kb = jnp.where(strict, from_b[:, c:], 0.0)
        a_ra = jnp.where(causal, from_r[:, :c], 0.0)
        a_rk = jnp.where(causal, from_r[:, c:], 0.0)
        p = _inverse(a_ab, saved)
        ps.append(p)
        u_h = mm(p, bt + mm(a_kb, v))
        us.append(u_h)
        ys.append(rt + mm(a_ra, u_h) + mm(a_rk, v))
    lo = lane < RW_HEAD
    u = jnp.where(lo, us[0], us[1])
    y = jnp.where(lo, ys[0], ys[1])
    tail = jnp.exp(g_last - gc)
    s1 = s0 * jnp.exp(g_last) + mm(u, al * tail, "tn") + mm(v, k * tail, "tn")
    vi = lax.broadcasted_iota(jnp.int32, s0.shape, 0)
    ki = lax.broadcasted_iota(jnp.int32, s0.shape, 1)
    s1 = jnp.where((vi < RW_HEAD) == (ki < RW_HEAD), s1, 0.0)
    return y, s1, ps[0], ps[1]


SCAN_HB = 8


def _scan_specs(arrs, n_chunks, reverse):
    def spec(off):
        assert off % SCAN_HB == 0
        if reverse:
            return pl.BlockSpec((CHUNK, SCAN_HB * LANES), lambda h, n: (n_chunks - 1 - n, off // SCAN_HB + h))
        return pl.BlockSpec((CHUNK, SCAN_HB * LANES), lambda h, n: (n, off // SCAN_HB + h))
    return [spec(off) for _, off in arrs]


def _split_heads(x):
    return jnp.stack([x[:, LANES * j:LANES * (j + 1)] for j in range(SCAN_HB)], axis=0)


def _merge_heads(x):
    return jnp.concatenate([x[j] for j in range(SCAN_HB)], axis=1)


def _scan_fwd(group_fn, name, arrs, heads, n_kept):
    s = arrs[0][0].shape[0]
    n_chunks = s // CHUNK
    n_in = len(arrs)

    def body(*refs):
        y_ref, st_ref = refs[n_in:n_in + 2]
        kept_refs, s_scr = refs[n_in + 2:-1], refs[-1]

        @pl.when(pl.program_id(1) == 0)
        def _():
            s_scr[...] = jnp.zeros_like(s_scr)

        s0 = s_scr[...]
        st_ref[...] = s0
        y, s1, *kept = group_fn(s0, *[_split_heads(r[...]) for r in refs[:n_in]])
        y_ref[...] = _merge_heads(y)
        s_scr[...] = s1
        for ref, val in zip(kept_refs, kept):
            ref[...] = val

    per_chunk = pl.BlockSpec((SCAN_HB, None, LANES, LANES), lambda h, n: (h, n, 0, 0))
    res = pl.pallas_call(
        body, grid=(heads // SCAN_HB, n_chunks), name=name,
        in_specs=_scan_specs(arrs, n_chunks, False),
        out_specs=[pl.BlockSpec((CHUNK, SCAN_HB * LANES), lambda h, n: (n, h))] + [per_chunk] * (1 + n_kept),
        out_shape=[SDS((s, heads * LANES), F32)] + [SDS((heads, n_chunks, LANES, LANES), F32)] * (1 + n_kept),
        scratch_shapes=[pltpu.VMEM((SCAN_HB, LANES, LANES), F32)],
        compiler_params=pltpu.CompilerParams(dimension_semantics=("arbitrary", "arbitrary")),
    )(*[a for a, _ in arrs])
    return res[0], res[1:]


def _scan_bwd(group_fn, name, arrs, kept, dy, heads):
    s = arrs[0][0].shape[0]
    n_chunks = s // CHUNK
    n_in, n_kept = len(arrs), len(kept)

    def body(*refs):
        kept_vals = [r[...] for r in refs[n_in:n_in + n_kept]]
        dy_ref = refs[n_in + n_kept]
        d_refs = refs[n_in + n_kept + 1:2 * n_in + n_kept + 1]
        ds_scr = refs[-1]

        @pl.when(pl.program_id(1) == 0)
        def _():
            ds_scr[...] = jnp.zeros_like(ds_scr)

        def fn(s0, *ins):
            return group_fn(s0, *ins, *kept_vals[1:])[:2]

        _, vjp = jax.vjp(fn, kept_vals[0], *[_split_heads(r[...]) for r in refs[:n_in]])
        grads = vjp((_split_heads(dy_ref[...]), ds_scr[...]))
        ds_scr[...] = grads[0]
        for ref, g in zip(d_refs, grads[1:]):
            ref[...] = _merge_heads(g)

    rev = pl.BlockSpec((CHUNK, SCAN_HB * LANES), lambda h, n: (n_chunks - 1 - n, h))
    per_chunk = pl.BlockSpec((SCAN_HB, None, LANES, LANES), lambda h, n: (h, n_chunks - 1 - n, 0, 0))
    return pl.pallas_call(
        body, grid=(heads // SCAN_HB, n_chunks), name=name,
        in_specs=_scan_specs(arrs, n_chunks, True) + [per_chunk] * n_kept + [rev],
        out_specs=[rev] * n_in,
        out_shape=[SDS((s, heads * LANES), F32)] * n_in,
        scratch_shapes=[pltpu.VMEM((SCAN_HB, LANES, LANES), F32)],
        compiler_params=pltpu.CompilerParams(dimension_semantics=("arbitrary", "arbitrary")),
    )(*[a for a, _ in arrs], *kept, dy)


def _col_spec(tr, width, cb):
    return pl.BlockSpec((tr, width), lambda i: (i, cb))


def _whole(p):
    return pl.BlockSpec(p.shape, lambda i: (0,) * p.ndim)


def _row_fwd(fn, name, tiles, params, outs, tr):
    rows = tiles[0][0].shape[0]
    nt, npar = len(tiles), len(params)

    def body(*refs):
        vals = [r[...].astype(F32) for r in refs[:nt + npar]]
        for ref, o in zip(refs[nt + npar:], fn(*vals)):
            ref[...] = o.astype(ref.dtype)

    return pl.pallas_call(
        body, grid=(rows // tr,), name=name,
        in_specs=[_col_spec(tr, w, cb) for _, w, cb in tiles] + [_whole(p) for p in params],
        out_specs=[_col_spec(tr, w, 0) for w, _ in outs],
        out_shape=[SDS((rows, w), dt) for w, dt in outs],
        compiler_params=pltpu.CompilerParams(dimension_semantics=("arbitrary",), vmem_limit_bytes=VMEM_LIMIT),
    )(*[a for a, _, _ in tiles], *params)


def _row_bwd(fn, name, tiles, params, cts, tr, want_tiles=None):
    rows = tiles[0][0].shape[0]
    nt, npar = len(tiles), len(params)
    want = list(range(nt)) if want_tiles is None else list(want_tiles)
    flat_cts = [c for group in cts for c in group]
    n_ct = len(flat_cts)

    def body(*refs):
        vals = [r[...].astype(F32) for r in refs[:nt + npar]]
        ct_refs = refs[nt + npar:nt + npar + n_ct]
        out_refs = refs[nt + npar + n_ct:]
        ct_vals, at = [], 0
        for group in cts:
            total = ct_refs[at][...].astype(F32)
            for r in ct_refs[at + 1:at + len(group)]:
                total = total + r[...].astype(F32)
            ct_vals.append(total)
            at += len(group)
        _, vjp = jax.vjp(lambda *a: tuple(fn(*a)), *vals)
        grads = vjp(tuple(ct_vals))
        for ref, t in zip(out_refs[:len(want)], want):
            ref[...] = grads[t]
        first = pl.program_id(0) == 0
        for ref, g in zip(out_refs[len(want):], grads[nt:]):
            @pl.when(first)
            def _(ref=ref, g=g):
                ref[...] = g

            @pl.when(jnp.logical_not(first))
            def _(ref=ref, g=g):
                ref[...] += g

    res = pl.pallas_call(
        body, grid=(rows // tr,), name=name,
        in_specs=[_col_spec(tr, w, cb) for _, w, cb in tiles] + [_whole(p) for p in params]
        + [_col_spec(tr, w, cb) for _, w, cb in flat_cts],
        out_specs=[_col_spec(tr, tiles[t][1], 0) for t in want] + [_whole(p) for p in params],
        out_shape=[SDS((rows, tiles[t][1]), F32) for t in want] + [SDS(p.shape, F32) for p in params],
        compiler_params=pltpu.CompilerParams(dimension_semantics=("arbitrary",), vmem_limit_bytes=VMEM_LIMIT),
    )(*[a for a, _, _ in tiles], *params, *[a for a, _, _ in flat_cts])
    return res[:len(want)], res[len(want):]


def _col_fwd(fn, name, x, first_block, n_blocks, params):
    rows = x.shape[0]

    def body(*refs):
        refs[-1][...] = fn(*[r[...] for r in refs[:-1]])

    return pl.pallas_call(
        body, grid=(n_blocks,), name=name,
        in_specs=[pl.BlockSpec((rows, LANES), lambda j: (0, first_block + j))]
        + [pl.BlockSpec((p.shape[0], LANES), lambda j: (0, j)) for p in params],
        out_specs=pl.BlockSpec((rows, LANES), lambda j: (0, j)),
        out_shape=SDS((rows, n_blocks * LANES), F32),
        compiler_params=pltpu.CompilerParams(dimension_semantics=("arbitrary",), vmem_limit_bytes=VMEM_LIMIT),
    )(x, *params)


def _col_bwd(fn, name, x, first_block, n_blocks, params, dys):
    rows = x.shape[0]
    npar, nd = len(params), len(dys)
    starts = [sum(t.shape[1] for t in dys[:i]) // LANES for i in range(nd + 1)]

    def body(*refs):
        vals = [r[...] for r in refs[:1 + npar]]
        j = pl.program_id(0)
        dy = refs[1 + npar][...]
        for i in range(1, nd):
            dy = jnp.where(j >= starts[i], refs[1 + npar + i][...], dy)
        _, vjp = jax.vjp(fn, *vals)
        grads = vjp(dy)
        for ref, g in zip(refs[1 + npar + nd:], grads):
            ref[...] = g.astype(ref.dtype)

    def piece(i):
        last = starts[i + 1] - starts[i] - 1
        return pl.BlockSpec((rows, LANES), lambda j: (0, jnp.clip(j - starts[i], 0, last)))

    pspecs = [pl.BlockSpec((p.shape[0], LANES), lambda j: (0, j)) for p in params]
    blk = pl.BlockSpec((rows, LANES), lambda j: (0, j))
    res = pl.pallas_call(
        body, grid=(n_blocks,), name=name,
        in_specs=[pl.BlockSpec((rows, LANES), lambda j: (0, first_block + j))] + pspecs + [piece(i) for i in range(nd)],
        out_specs=[blk] + pspecs,
        out_shape=[SDS((rows, n_blocks * LANES), BF16)] + [SDS(p.shape, F32) for p in params],
        compiler_params=pltpu.CompilerParams(dimension_semantics=("arbitrary",), vmem_limit_bytes=VMEM_LIMIT),
    )(x, *params, *dys)
    return res[0], res[1:]


def _conv_fn(x, w):
    acc = x * w[3:4, :]
    for j in range(3):
        acc = acc + shift_rows(x, 3 - j) * w[j:j + 1, :]
    return _silu(acc)


def _lerp_fn(x, mu):
    return x + (shift_rows(x, 1) - x) * mu[0:1, :]


def _seg_sum(x, width):
    if width == LANES:
        return jnp.sum(x, axis=1, keepdims=True)
    lo = lax.broadcasted_iota(jnp.int32, x.shape, 1) < width
    s0 = jnp.sum(jnp.where(lo, x, 0.0), axis=1, keepdims=True)
    s1 = jnp.sum(jnp.where(lo, 0.0, x), axis=1, keepdims=True)
    return jnp.where(lo, s0, s1)


def _per_block(fn, *xs):
    n = xs[0].shape[1] // LANES
    return jnp.concatenate([fn(*[x[:, LANES * b:LANES * (b + 1)] for x in xs]) for b in range(n)], axis=1)


def _head_expand(col0):
    r = lax.broadcasted_iota(jnp.int32, (LANES, DN_WIDTH), 0)
    c = lax.shift_right_logical(lax.broadcasted_iota(jnp.int32, (LANES, DN_WIDTH), 1), 7)
    return jnp.where(r == c + col0, 1.0, 0.0)


def _dn_pre_fn(cq, ck, gates, a_log, dt_bias):
    l2 = lambda x: x * lax.rsqrt(_seg_sum(x * x, LANES) + 1e-6)
    qh = _per_block(l2, cq) * (LANES ** -0.5)
    kh = _per_block(l2, ck)
    g = -jnp.exp(a_log) * _softplus(gates + dt_bias)
    gb = mm(g, _head_expand(0), "nn", True)
    bb = mm(_sigmoid(gates), _head_expand(DN_HEADS), "nn", True)
    return qh, kh, gb, bb, _cumsum_rows(gb)


def _dn_post_fn(o, z, nw):
    def one(ob, zb):
        return ob * lax.rsqrt(_seg_sum(ob * ob, LANES) * (1.0 / LANES) + RMS_EPS) * nw * _silu(zb)
    return (_per_block(one, o, z),)


def _rw_pre_fn(pr, pk, pv, pwa, pg, w0, a0, k_k, k_a, w2p, a2p, g2):
    log_w = -_softplus(-(w0 + mm(jnp.tanh(pwa), w2p))) - 0.5
    lw = -jnp.exp(log_w)
    a = _sigmoid(a0 + mm(pwa, a2p))
    gate = mm(_sigmoid(pg), g2)
    kk = pk * k_k
    kk = _per_block(lambda x: x / jnp.maximum(jnp.sqrt(_seg_sum(x * x, RW_HEAD)), 1e-12), kk)
    k = pk * (1.0 + (a - 1.0) * k_a)
    return pr, lw, k, pv, kk * a, -kk, gate, _cumsum_rows(lw)


def _rw_post_fn(y, r, k, v, gate, ln_w, ln_b, r_k):
    def one(yb, rb, kb, vb, gb, wb, bb, rkb):
        d = yb - _seg_sum(yb, RW_HEAD) * (1.0 / RW_HEAD)
        var = _seg_sum(d * d, RW_HEAD) * (1.0 / RW_HEAD)
        yn = d * lax.rsqrt(var + RW_GN_EPS) * wb + bb
        return (yn + _seg_sum(rb * kb * rkb, RW_HEAD) * vb) * gb
    return (_per_block(one, y, r, k, v, gate, ln_w, ln_b, r_k),)


def _rms_fn(h, w):
    return (h * lax.rsqrt(jnp.mean(h * h, axis=1, keepdims=True) + RMS_EPS) * w,)


def _xattn_fn(q, k, v):
    outs = []
    for h in range(XA_HEADS):
        sl = slice(LANES * h, LANES * (h + 1))
        s = mm(q[:, sl], k[:, sl], "nt") * (LANES ** -0.5)
        e = jnp.exp(s - jnp.max(s, axis=1, keepdims=True))
        outs.append(mm(e / jnp.sum(e, axis=1, keepdims=True), v[:, sl]))
    return (jnp.concatenate(outs, axis=1),)


def _fit(tile, dim):
    best = [t for t in range(LANES, min(tile, dim) + 1, LANES) if dim % t == 0]
    assert best, (tile, dim)
    return best[-1]


def _matmul(name, a, b, mode, out_dtypes, epilogue=None, extras=(), tm=1024, tn=1024, tk=2048, after=None):
    if mode == "tn":
        (k_dim, m), n = a.shape, b.shape[1]
    else:
        (m, k_dim), n = a.shape, (b.shape[1] if mode == "nn" else b.shape[0])
    tm, tn, tk = _fit(tm, m), _fit(tn, n), _fit(tk, k_dim)
    nk = k_dim // tk
    a_spec = (pl.BlockSpec((tk, tm), lambda i, j, k: (k, i)) if mode == "tn"
              else pl.BlockSpec((tm, tk), lambda i, j, k: (i, k)))
    b_spec = (pl.BlockSpec((tn, tk), lambda i, j, k: (j, k)) if mode == "nt"
              else pl.BlockSpec((tk, tn), lambda i, j, k: (k, j)))
    o_spec = pl.BlockSpec((tm, tn), lambda i, j, k: (i, j))
    n_ex, n_out = len(extras), len(out_dtypes)
    ties = [] if after is None else [after]

    def finish(total, rest):
        ex = [r[...].astype(F32) for r in rest[:n_ex]]
        res = epilogue(total, *ex) if epilogue else (total,)
        for ref, o in zip(rest[n_ex + len(ties):n_ex + len(ties) + n_out], res):
            ref[...] = o.astype(ref.dtype)

    def body_single(a_ref, b_ref, *rest):
        finish(_raw_dot(a_ref[...], b_ref[...], mode, False), rest)

    def body_acc(a_ref, b_ref, *rest):
        acc = rest[-1]
        k = pl.program_id(2)

        @pl.when(k == 0)
        def _():
            acc[...] = jnp.zeros_like(acc)

        acc[...] += _raw_dot(a_ref[...], b_ref[...], mode, False)

        @pl.when(k == nk - 1)
        def _():
            finish(acc[...], rest)

    res = pl.pallas_call(
        body_single if nk == 1 else body_acc, grid=(m // tm, n // tn, nk), name=name,
        in_specs=[a_spec, b_spec] + [o_spec] * n_ex + [pl.BlockSpec((8, LANES), lambda i, j, k: (0, 0))] * len(ties),
        out_specs=[o_spec] * n_out,
        out_shape=[SDS((m, n), dt) for dt in out_dtypes],
        scratch_shapes=[] if nk == 1 else [pltpu.VMEM((tm, tn), F32)],
        compiler_params=pltpu.CompilerParams(dimension_semantics=("parallel", "parallel", "arbitrary"),
                                             vmem_limit_bytes=VMEM_LIMIT),
    )(a, b, *extras, *ties)
    return res


def _matmul_norm_bwd(name, a, b, mode, h, w, dres, after=None, tm=512, tk=1024):
    m, n = h.shape
    k_dim = a.shape[1]
    tm, tk = _fit(tm, m), _fit(tk, k_dim)
    nk = k_dim // tk
    ties = [] if after is None else [after]
    a_spec = pl.BlockSpec((tm, tk), lambda i, k: (i, k))
    b_spec = pl.BlockSpec((n, tk), lambda i, k: (0, k)) if mode == "nt" else pl.BlockSpec((tk, n), lambda i, k: (k, 0))
    row = pl.BlockSpec((tm, n), lambda i, k: (i, 0))
    w_spec = pl.BlockSpec((1, n), lambda i, k: (0, 0))

    def body(a_ref, b_ref, h_ref, w_ref, dres_ref, *rest):
        dh_ref, dw_ref, acc = rest[len(ties):]
        i, k = pl.program_id(0), pl.program_id(1)

        @pl.when(k == 0)
        def _():
            acc[...] = jnp.zeros_like(acc)

        acc[...] += _raw_dot(a_ref[...], b_ref[...], mode, False)

        @pl.when(k == nk - 1)
        def _():
            _, vjp = jax.vjp(_rms_res_fn, h_ref[...], w_ref[...])
            dh, dw = vjp((acc[...], dres_ref[...]))
            dh_ref[...] = dh

            @pl.when(i == 0)
            def _():
                dw_ref[...] = dw

            @pl.when(i != 0)
            def _():
                dw_ref[...] += dw

    return pl.pallas_call(
        body, grid=(m // tm, nk), name=name,
        in_specs=[a_spec, b_spec, row, w_spec, row] + [pl.BlockSpec((8, LANES), lambda i, k: (0, 0))] * len(ties),
        out_specs=[row, w_spec],
        out_shape=[SDS((m, n), F32), SDS((1, n), F32)],
        scratch_shapes=[pltpu.VMEM((tm, n), F32)],
        compiler_params=pltpu.CompilerParams(dimension_semantics=("arbitrary", "arbitrary"),
                                             vmem_limit_bytes=VMEM_LIMIT),
    )(a, b, h, w, dres, *ties)


def _loss_call(h, target, w, tr=256):
    rows, d = h.shape

    def fn(hv, wv, tv):
        y = _rms_fn(hv, wv)[0]
        return 0.5 * jnp.sum(jnp.mean(jnp.square(y - tv), axis=1, keepdims=True), axis=0, keepdims=True)

    def body(h_ref, t_ref, w_ref, loss_ref, dh_ref, dw_ref):
        tv = t_ref[...]
        val, vjp = jax.vjp(lambda hv, wv: fn(hv, wv, tv), h_ref[...], w_ref[...])
        dh, dw = vjp(jnp.ones((1, 1), F32))
        dh_ref[...] = dh
        first = pl.program_id(0) == 0

        @pl.when(first)
        def _():
            loss_ref[...] = jnp.broadcast_to(val, loss_ref.shape)
            dw_ref[...] = dw

        @pl.when(jnp.logical_not(first))
        def _():
            loss_ref[...] += jnp.broadcast_to(val, loss_ref.shape)
            dw_ref[...] += dw

    return pl.pallas_call(
        body, grid=(rows // tr,), name="loss_head",
        in_specs=[_col_spec(tr, d, 0), _col_spec(tr, d, 0), _whole(w)],
        out_specs=[pl.BlockSpec((8, LANES), lambda i: (0, 0)), _col_spec(tr, d, 0), _whole(w)],
        out_shape=[SDS((8, LANES), F32), SDS((rows, d), F32), SDS(w.shape, F32)],
        compiler_params=pltpu.CompilerParams(dimension_semantics=("arbitrary",), vmem_limit_bytes=VMEM_LIMIT),
    )(h, target, w)


def _adamw_vals(w, g, m, v):
    m = ADAM_B1 * m + (1.0 - ADAM_B1) * g
    v = ADAM_B2 * v + (1.0 - ADAM_B2) * jnp.square(g)
    m_hat = m / (1.0 - ADAM_B1 ** ADAM_STEP)
    v_hat = v / (1.0 - ADAM_B2 ** ADAM_STEP)
    delta = -ADAM_LR * (m_hat / (jnp.sqrt(v_hat) + ADAM_EPS) + ADAM_WD * w)
    return delta, m, v


def _sum_adamw(name, parts, w, m, v):
    r, c = w.shape
    n_parts = parts.shape[0]
    budget = 6 * 1024 * 1024
    tr, tc = r, c
    for cand in (512, 256, 128, 64, 32, 16, 8):
        if r % cand == 0 and n_parts * cand * c * 4 <= budget:
            tr = cand
            break
    if n_parts * tr * c * 4 > budget:
        tc = max(t for t in range(LANES, c + 1, LANES) if c % t == 0 and n_parts * r * t * 4 <= budget)

    def body(p_ref, w_ref, m_ref, v_ref, g_ref, d_ref, m2_ref, v2_ref):
        g = p_ref[0].astype(F32)
        for s in range(1, n_parts):
            g = g + p_ref[s].astype(F32)
        g_ref[...] = g
        d_ref[...], m2_ref[...], v2_ref[...] = _adamw_vals(w_ref[...], g, m_ref[...], v_ref[...])

    blk = pl.BlockSpec((tr, tc), lambda i: (i, 0)) if tc == c else pl.BlockSpec((tr, tc), lambda i: (0, i))
    parts_blk = (pl.BlockSpec((n_parts, tr, tc), lambda i: (0, i, 0)) if tc == c
                 else pl.BlockSpec((n_parts, tr, tc), lambda i: (0, 0, i)))
    return pl.pallas_call(
        body, grid=(r // tr if tc == c else c // tc,), name=name,
        in_specs=[parts_blk, blk, blk, blk],
        out_specs=[blk] * 4, out_shape=[SDS((r, c), F32)] * 4,
        compiler_params=pltpu.CompilerParams(dimension_semantics=("arbitrary",), vmem_limit_bytes=VMEM_LIMIT),
    )(parts, w, m, v)


def _peers():
    x, y, c = lax.axis_index("x"), lax.axis_index("y"), lax.axis_index("c")
    peers = []
    for k in range(1, N_DEV):
        px = 1 - x if k & 4 else x
        py = 1 - y if k & 2 else y
        pc = 1 - c if k & 1 else c
        peers.append(((px, py, pc), 4 * px + 2 * py + pc))
    return 4 * x + 2 * y + c, peers


def _slot(ref, idx, cols):
    if cols is None:
        return ref.at[idx]
    return ref.at[:, pl.ds(pl.multiple_of(idx * cols, LANES), cols)]


def _gather_two_level(name, srcs, dsts):
    n = len(srcs)
    dst_cols = [c for _, _, c in dsts]

    def body(*refs):
        src_refs, out_refs = refs[:n], refs[n:2 * n]
        send_sems, recv_sems, local_sems = refs[2 * n:]
        x, y, c = lax.axis_index("x"), lax.axis_index("y"), lax.axis_index("c")
        index = lambda px, py, pc: 4 * px + 2 * py + pc
        me, sibling = index(x, y, c), (x, y, 1 - c)
        chips = [(x, 1 - y), (1 - x, y), (1 - x, 1 - y)]

        def copy(a, k, src, block, to):
            return pltpu.make_async_remote_copy(
                src_ref=src, dst_ref=_slot(out_refs[a], block, dst_cols[a]),
                send_sem=send_sems.at[a, k], recv_sem=recv_sems.at[a, k],
                device_id=to, device_id_type=pl.DeviceIdType.MESH)

        local, first, passed = [], [], []
        for a in range(n):
            cp = pltpu.make_async_copy(src_refs[a], _slot(out_refs[a], me, dst_cols[a]), local_sems.at[a])
            cp.start()
            local.append(cp)
            first.append(copy(a, 0, src_refs[a], me, sibling))
            first += [copy(a, 1 + j, src_refs[a], me, (*chip, c)) for j, chip in enumerate(chips)]
        for cp in first:
            cp.start()
        for a in range(n):
            for j, chip in enumerate(chips):
                block = index(*chip, c)
                arrived = _slot(out_refs[a], block, dst_cols[a])
                copy(a, 1 + j, arrived, block, (*chip, c)).wait_recv()
                passed.append(copy(a, 4 + j, arrived, block, sibling))
                passed[-1].start()
        for a in range(n):
            copy(a, 0, src_refs[a], index(x, y, 1 - c), sibling).wait_recv()
            for j, chip in enumerate(chips):
                block = index(*chip, 1 - c)
                copy(a, 4 + j, src_refs[a], block, sibling).wait_recv()
        for cp in first + passed:
            cp.wait_send()
        for cp in local:
            cp.wait()

    any_spec = pl.BlockSpec(memory_space=pl.ANY)
    return pl.pallas_call(
        body, name=name,
        in_specs=[any_spec] * n, out_specs=[any_spec] * n,
        out_shape=[SDS(shape, dt) for shape, dt, _ in dsts],
        scratch_shapes=_exchange_sems(n),
    )(*[a for a, _ in srcs])


_HBM = pl.BlockSpec(memory_space=pltpu.HBM)
_SEM = pl.BlockSpec(memory_space=pltpu.SEMAPHORE)
_EFFECT = pltpu.SideEffectType.DATAFLOW_SIDE_EFFECTING


def _split_copies(src_cols, dst_cols, gather, chips, src_refs, land_refs, send_sems, recv_sems, landings):
    me, peers = _peers()
    if chips:
        me, peers = me // 2, [(pos, idx // 2) for k, (pos, idx) in enumerate(peers) if (k + 1) in (2, 4, 6)]
    n, width = len(src_cols), len(peers)
    remote, local = [], []
    for a, (s_cols, d_cols) in enumerate(zip(src_cols, dst_cols)):
        mine = src_refs[a] if gather else _slot(src_refs[a], me, s_cols)
        local.append(pltpu.make_async_copy(mine, _slot(land_refs[a], me, d_cols), send_sems.at[n * width + a]))
        for k, (pos, idx) in enumerate(peers):
            blk = src_refs[a] if gather else _slot(src_refs[a], idx, s_cols)
            remote.append(pltpu.make_async_remote_copy(
                src_ref=blk, dst_ref=_slot(land_refs[a], idx if landings else me, d_cols),
                send_sem=send_sems.at[a * width + k], recv_sem=recv_sems.at[a * width + k],
                device_id=pos, device_id_type=pl.DeviceIdType.MESH))
    return remote, local


def _exchange_start(name, srcs, dsts, gather, after, chips=False):
    n = len(srcs)
    src_cols, dst_cols = [c for _, c in srcs], [c for _, _, c in dsts]
    width = 3 if chips else N_DEV - 1

    def body(*refs):
        src_refs, land_refs = refs[:n], refs[n:2 * n]
        send_sems, recv_sems = refs[2 * n + 1:2 * n + 3]
        token = refs[-1]
        remote, local = _split_copies(src_cols, dst_cols, gather, chips, src_refs, land_refs, send_sems, recv_sems,
                                      False)
        for cp in remote + local:
            cp.start()
        token[...] = jnp.zeros_like(token)

    hbm = lambda a: pltpu.with_memory_space_constraint(a, pltpu.HBM)
    lands = [hbm(lax.empty(shape, dt)) for shape, dt, _ in dsts]
    res = pl.pallas_call(
        body, name=name,
        out_shape=(pltpu.SemaphoreType.DMA((n * (width + 1),)), pltpu.SemaphoreType.DMA((n * width,)),
                   *[pltpu.HBM(a.shape, a.dtype) for a, _ in srcs], *[pltpu.HBM(a.shape, a.dtype) for a in lands],
                   SDS((8, LANES), F32)),
        in_specs=[_HBM] * (2 * n) + [pl.BlockSpec(memory_space=pl.ANY)],
        out_specs=(_SEM, _SEM, *[_HBM] * (2 * n), pl.BlockSpec(memory_space=pltpu.VMEM)),
        input_output_aliases={i: 2 + i for i in range(2 * n)},
        compiler_params=pltpu.CompilerParams(has_side_effects=_EFFECT),
    )(*[hbm(a) for a, _ in srcs], *lands, after)
    handle = (res[0], res[1], res[2:2 + n], res[2 + n:2 + 2 * n], src_cols, dst_cols, gather, chips)
    return handle, res[-1]


def _exchange_wait(name, handle, after):
    send_sems, recv_sems, src_thru, land_thru, src_cols, dst_cols, gather, chips = handle
    n = len(src_thru)

    def body(*refs):
        src_refs, land_refs = refs[:n], refs[n:2 * n]
        s_sems, r_sems = refs[2 * n:2 * n + 2]
        remote, local = _split_copies(src_cols, dst_cols, gather, chips, src_refs, land_refs, s_sems, r_sems, True)
        for cp in remote:
            cp.wait_send()
            cp.wait_recv()
        for cp in local:
            cp.wait()

    res = pl.pallas_call(
        body, name=name,
        out_shape=tuple(pltpu.HBM(a.shape, a.dtype) for a in (*src_thru, *land_thru)),
        in_specs=[_HBM] * (2 * n) + [_SEM, _SEM, pl.BlockSpec(memory_space=pl.ANY)],
        out_specs=tuple([_HBM] * (2 * n)),
        input_output_aliases={i: i for i in range(2 * n)},
        compiler_params=pltpu.CompilerParams(has_side_effects=_EFFECT),
    )(*src_thru, *land_thru, send_sems, recv_sems, after)
    return res[n:]


def _pair_swap(name, arrs):
    n = len(arrs)

    def body(*refs):
        src_refs, out_refs = refs[:n], refs[n:2 * n]
        send_sems, recv_sems = refs[2 * n:]
        x, y, c = lax.axis_index("x"), lax.axis_index("y"), lax.axis_index("c")
        copies = [pltpu.make_async_remote_copy(
            src_ref=src_refs[a].at[:, 1 - c], dst_ref=out_refs[a], send_sem=send_sems.at[a], recv_sem=recv_sems.at[a],
            device_id=(x, y, 1 - c), device_id_type=pl.DeviceIdType.MESH) for a in range(n)]
        for cp in copies:
            cp.start()
        for cp in copies:
            cp.wait()

    any_spec = pl.BlockSpec(memory_space=pl.ANY)
    return pl.pallas_call(
        body, name=name,
        in_specs=[any_spec] * n, out_specs=[any_spec] * n,
        out_shape=[SDS((a.shape[0],) + a.shape[2:], a.dtype) for a in arrs],
        scratch_shapes=[pltpu.SemaphoreType.DMA((n,)), pltpu.SemaphoreType.DMA((n,))],
    )(*arrs)


def _pair_add(name, mine, theirs):
    four, _, r, c = mine.shape
    tr = r
    for cand in (512, 256, 128, 64, 32, 16, 8):
        if r % cand == 0:
            tr = cand
            break
    tc = max(t for t in range(LANES, c + 1, LANES) if c % t == 0 and (t == LANES or 2 * tr * t * 4 <= 4 * 1024 * 1024))

    def body(m_ref, t_ref, o_ref):
        core = lax.axis_index("c")
        both = m_ref[...].astype(F32)
        own = jnp.where(core == 0, both[0], both[1])
        o_ref[...] = (own + t_ref[...].astype(F32)).astype(o_ref.dtype)

    return pl.pallas_call(
        body, grid=(four, r // tr, c // tc), name=name,
        in_specs=[pl.BlockSpec((None, 2, tr, tc), lambda i, j, k: (i, 0, j, k)),
                  pl.BlockSpec((None, tr, tc), lambda i, j, k: (i, j, k))],
        out_specs=pl.BlockSpec((None, tr, tc), lambda i, j, k: (i, j, k)),
        out_shape=SDS(theirs.shape, theirs.dtype),
        compiler_params=pltpu.CompilerParams(dimension_semantics=("arbitrary",) * 3, vmem_limit_bytes=VMEM_LIMIT),
    )(mine, theirs)


def _my_index():
    return 4 * lax.axis_index("x") + 2 * lax.axis_index("y") + lax.axis_index("c")


def _two_level_copies(stage, dst_cols, src_refs, land_refs, send_sems, recv_sems, landings):
    x, y, c = lax.axis_index("x"), lax.axis_index("y"), lax.axis_index("c")

    def pos(k):
        return (1 - x if k & 4 else x, 1 - y if k & 2 else y, 1 - c if k & 1 else c)

    def idx(k):
        px, py, pc = pos(k)
        return 4 * px + 2 * py + pc

    out = []
    for a, cols in enumerate(dst_cols):
        if stage == 1:
            for i, k in enumerate((1, 2, 4, 6)):
                out.append(pltpu.make_async_remote_copy(
                    src_ref=src_refs[a], dst_ref=_slot(land_refs[a], idx(k) if landings else idx(0), cols),
                    send_sem=send_sems.at[4 * a + i], recv_sem=recv_sems.at[4 * a + i],
                    device_id=pos(k), device_id_type=pl.DeviceIdType.MESH))
        else:
            for i, k in enumerate((2, 4, 6)):
                out.append(pltpu.make_async_remote_copy(
                    src_ref=_slot(land_refs[a], idx(k), cols),
                    dst_ref=_slot(land_refs[a], idx(k ^ 1) if landings else idx(k), cols),
                    send_sem=send_sems.at[3 * a + i], recv_sem=recv_sems.at[3 * a + i],
                    device_id=pos(1), device_id_type=pl.DeviceIdType.MESH))
    return out


def _gather2_start(name, srcs, dsts, after):
    n = len(srcs)
    dst_cols = [c for _, _, c in dsts]

    def body(*refs):
        src_refs, land_refs = refs[:n], refs[n:2 * n]
        send_sems, recv_sems = refs[2 * n + 1:2 * n + 3]
        me = _my_index()
        for a in range(n):
            pltpu.make_async_copy(src_refs[a], _slot(land_refs[a], me, dst_cols[a]), send_sems.at[4 * n + a]).start()
        for cp in _two_level_copies(1, dst_cols, src_refs, land_refs, send_sems, recv_sems, False):
            cp.start()
        refs[-1][...] = jnp.zeros_like(refs[-1])

    hbm = lambda a: pltpu.with_memory_space_constraint(a, pltpu.HBM)
    lands = [hbm(lax.empty(shape, dt)) for shape, dt, _ in dsts]
    res = pl.pallas_call(
        body, name=name,
        out_shape=(pltpu.SemaphoreType.DMA((5 * n,)), pltpu.SemaphoreType.DMA((4 * n,)),
                   *[pltpu.HBM(a.shape, a.dtype) for a, _ in srcs], *[pltpu.HBM(a.shape, a.dtype) for a in lands],
                   SDS((8, LANES), F32)),
        in_specs=[_HBM] * (2 * n) + [pl.BlockSpec(memory_space=pl.ANY)],
        out_specs=(_SEM, _SEM, *[_HBM] * (2 * n), pl.BlockSpec(memory_space=pltpu.VMEM)),
        input_output_aliases={i: 2 + i for i in range(2 * n)},
        compiler_params=pltpu.CompilerParams(has_side_effects=_EFFECT),
    )(*[hbm(a) for a, _ in srcs], *lands, after)
    return (res[0], res[1], res[2:2 + n], res[2 + n:2 + 2 * n], dst_cols), res[-1]


def _gather2_pass(name, handle, after):
    send1, recv1, src_thru, land_thru, dst_cols = handle
    n = len(src_thru)

    def body(*refs):
        src_refs, land_refs = refs[:n], refs[n:2 * n]
        s1, r1 = refs[2 * n:2 * n + 2]
        send2, recv2 = refs[2 * n + 3:2 * n + 5]
        me = _my_index()
        for cp in _two_level_copies(1, dst_cols, src_refs, land_refs, s1, r1, True):
            cp.wait_send()
            cp.wait_recv()
        for a in range(n):
            pltpu.make_async_copy(src_refs[a], _slot(land_refs[a], me, dst_cols[a]), s1.at[4 * n + a]).wait()
        for cp in _two_level_copies(2, dst_cols, src_refs, land_refs, send2, recv2, False):
            cp.start()
        refs[-1][...] = jnp.zeros_like(refs[-1])

    res = pl.pallas_call(
        body, name=name,
        out_shape=(pltpu.SemaphoreType.DMA((3 * n,)), pltpu.SemaphoreType.DMA((3 * n,)),
                   *[pltpu.HBM(a.shape, a.dtype) for a in (*src_thru, *land_thru)], SDS((8, LANES), F32)),
        in_specs=[_HBM] * (2 * n) + [_SEM, _SEM, pl.BlockSpec(memory_space=pl.ANY)],
        out_specs=(_SEM, _SEM, *[_HBM] * (2 * n), pl.BlockSpec(memory_space=pltpu.VMEM)),
        input_output_aliases={i: 2 + i for i in range(2 * n)},
        compiler_params=pltpu.CompilerParams(has_side_effects=_EFFECT),
    )(*src_thru, *land_thru, send1, recv1, after)
    return (res[0], res[1], res[2:2 + n], res[2 + n:2 + 2 * n], dst_cols), res[-1]


def _gather2_wait(name, handle, after):
    send2, recv2, src_thru, land_thru, dst_cols = handle
    n = len(src_thru)

    def body(*refs):
        src_refs, land_refs = refs[:n], refs[n:2 * n]
        s2, r2 = refs[2 * n:2 * n + 2]
        for cp in _two_level_copies(2, dst_cols, src_refs, land_refs, s2, r2, True):
            cp.wait_send()
            cp.wait_recv()

    res = pl.pallas_call(
        body, name=name,
        out_shape=tuple(pltpu.HBM(a.shape, a.dtype) for a in (*src_thru, *land_thru)),
        in_specs=[_HBM] * (2 * n) + [_SEM, _SEM, pl.BlockSpec(memory_space=pl.ANY)],
        out_specs=tuple([_HBM] * (2 * n)),
        input_output_aliases={i: i for i in range(2 * n)},
        compiler_params=pltpu.CompilerParams(has_side_effects=_EFFECT),
    )(*src_thru, *land_thru, send2, recv2, after)
    return res[n:]


def _exchange_sems(n):
    return [pltpu.SemaphoreType.DMA((n, N_DEV - 1)), pltpu.SemaphoreType.DMA((n, N_DEV - 1)),
            pltpu.SemaphoreType.DMA((n,))]


def _rms_res_fn(h, w):
    return _rms_fn(h, w)[0], h


def _add_epilogue(acc, res):
    return (acc + res,)


def _gather_plan(shards):
    srcs, dsts = [], []
    for n, sh in shards.items():
        r, c = sh.shape
        srcs.append((sh, None))
        if SHARDED[n] and c % LANES == 0:
            dsts.append(((r, N_DEV * c), sh.dtype, c))
        else:
            dsts.append(((N_DEV, r, c), sh.dtype, None))
    return srcs, dsts, True


def _w_in_segments():
    out = []
    for j in range(N_DEV):
        lo, hi = W_IN_SHARD * j, W_IN_SHARD * (j + 1)
        for a, b in ((lo, min(hi, DN_COLS)), (max(lo, DN_COLS), hi)):
            if a < b:
                out.append((j, a - lo, b - lo, a if a < DN_COLS else a + RW_OFF - DN_COLS))
    return out


def _w_in_to_padded(shards, tc=512):
    _, _, cols = shards.shape

    def body(g_ref, o_ref):
        o_ref[...] = jnp.zeros_like(o_ref)
        for j, a, b, dst in _w_in_segments():
            o_ref[dst:dst + b - a, :] = g_ref[j, a:b, :]

    return pl.pallas_call(
        body, grid=(cols // tc,), name="w_in_to_padded",
        in_specs=[pl.BlockSpec((N_DEV, W_IN_SHARD, tc), lambda i: (0, 0, i))],
        out_specs=pl.BlockSpec((IN_PAD, tc), lambda i: (0, i)),
        out_shape=SDS((IN_PAD, cols), shards.dtype),
        compiler_params=pltpu.CompilerParams(dimension_semantics=("arbitrary",), vmem_limit_bytes=VMEM_LIMIT),
    )(shards)


def _w_in_grad_to_shards(gw, tc=512):
    _, cols = gw.shape

    def body(w_ref, o_ref):
        for j, a, b, dst in _w_in_segments():
            o_ref[j, a:b, :] = w_ref[dst:dst + b - a, :]

    return pl.pallas_call(
        body, grid=(cols // tc,), name="w_in_grad_to_shards",
        in_specs=[pl.BlockSpec((IN_PAD, tc), lambda i: (0, i))],
        out_specs=pl.BlockSpec((N_DEV, W_IN_SHARD, tc), lambda i: (0, 0, i)),
        out_shape=SDS((N_DEV, W_IN_SHARD, cols), gw.dtype),
        compiler_params=pltpu.CompilerParams(dimension_semantics=("arbitrary",), vmem_limit_bytes=VMEM_LIMIT),
    )(gw)


def _gather_finish(names, outs):
    full = {}
    for n, arr in zip(names, outs):
        if n == "w_in":
            full[n] = _w_in_to_padded(arr)
        elif arr.ndim == 2:
            full[n] = arr
        elif SHARDED[n]:
            full[n] = arr.transpose(1, 0, 2).reshape(arr.shape[1], -1)
        else:
            full[n] = arr.reshape(-1, arr.shape[2])
    return full


def _scatter_plan(grads):
    srcs, dsts = [], []
    for n, gr in grads.items():
        if gr.ndim == 3:
            srcs.append((gr, None))
            dsts.append((gr.shape, gr.dtype, None))
            continue
        rows, cols = gr.shape
        if not SHARDED[n]:
            r, c = rows // N_DEV, cols
            srcs.append((gr.reshape(N_DEV, r, c), None))
        else:
            r, c = rows, cols // N_DEV
            if c % LANES == 0:
                srcs.append((gr, c))
            else:
                srcs.append((gr.reshape(r, N_DEV, c).transpose(1, 0, 2), None))
        dsts.append(((N_DEV, r, c), gr.dtype, None))
    return srcs, dsts, False


def _local_step(x, mem, target, wt, late):
    d = D_MODEL
    g = {}
    wt = dict(wt)
    grp_a = ("w_out", "xa_wq", "xa_wk", "xa_wv", "xa_wo")
    grp_b = ("ffn_w1", "ffn_w2")
    plan = lambda names: _gather_plan({n: late[n] for n in names})[:2]
    handle_a, tok_a = _gather2_start("late_gather_a_start", *plan(grp_a), wt["w_in"])
    handle_w1, tok_b = _gather2_start("late_gather_w1_start", *plan(("ffn_w1",)), tok_a)
    handle_w2, tok_c = _gather2_start("late_gather_w2_start", *plan(("ffn_w2",)), tok_b)
    mix_w = wt["mix_norm_w"] + (tok_a[0:1, 0:1] + tok_b[0:1, 0:1] + tok_c[0:1, 0:1])
    u = _row_fwd(_rms_fn, "mix_norm", [(x, d, 0)], [mix_w], [(d, BF16)], 256)[0]
    p = _matmul("in_proj", u, wt["w_in"], "nt", [F32], tn=1536)[0]
    c = _col_fwd(_conv_fn, "dn_conv", p, 0, 24, [wt["dn_conv_w"]])
    handle_a, tok = _gather2_pass("late_gather_a_pass", handle_a, c)
    dn_pre_tiles = [(c, DN_WIDTH, 0), (c, DN_WIDTH, 1), (p, LANES, 32)]
    dn_pre_params = [wt["dn_a_log"], wt["dn_dt_bias"]]
    qh, kh, gb, bb, gcb = _row_fwd(_dn_pre_fn, "dn_pre", dn_pre_tiles, [dn_pre_params[0] + tok[0:1, :], dn_pre_params[1]],
                                   [(DN_WIDTH, F32)] * 5, CHUNK)
    dn_arrs = [(qh, 0), (kh, 0), (c, 16), (gb, 0), (bb, 0), (gcb, 0)]
    o, kept_dn = _scan_fwd(_gdn_group, "gdn_scan", dn_arrs, DN_HEADS, 1)
    dn_post_tiles = [(o, DN_WIDTH, 0), (p, DN_WIDTH, 3)]
    o_dn = _row_fwd(_dn_post_fn, "dn_post", dn_post_tiles, [wt["dn_norm_w"]], [(DN_WIDTH, BF16)], 256)[0]

    ps = _col_fwd(_lerp_fn, "rw_shift", p, RW_OFF // LANES, 26, [wt["rw_mu"]])
    rw_pre_tiles = [(ps, RW_WIDTH, 0), (ps, RW_WIDTH, 1), (ps, RW_WIDTH, 2), (ps, LANES, 24), (ps, LANES, 25)]
    rw_pre_params = [wt[n] for n in ("rw_w0", "rw_a0", "rw_k_k", "rw_k_a", "rw_w2", "rw_a2", "rw_g2")]
    r, lw, k, v, al, be, gate, gcw = _row_fwd(_rw_pre_fn, "rw_pre", rw_pre_tiles, rw_pre_params,
                                              [(RW_WIDTH, F32)] * 8, CHUNK)
    rw_arrs = [(r, 0), (lw, 0), (k, 0), (v, 0), (al, 0), (be, 0), (gcw, 0)]
    y, kept_rw = _scan_fwd(_rw_group, "rw_scan", rw_arrs, RW_WIDTH // LANES, 2)
    handle_w1, tok = _gather2_pass("late_gather_w1_pass", handle_w1, y)
    rw_post_tiles = [(t, RW_WIDTH, 0) for t in (y, r, k, v, gate)]
    rw_post_params = [wt["rw_ln_w"], wt["rw_ln_b"], wt["rw_r_k"]]
    o_rw = _row_fwd(_rw_post_fn, "rw_post", rw_post_tiles, [rw_post_params[0] + tok[0:1, 0:1]] + rw_post_params[1:],
                    [(RW_WIDTH, BF16)], 128)[0]
    o_cat = jnp.concatenate([o_dn, o_rw], axis=1)
    wt.update(_gather_finish(grp_a, _gather2_wait("late_gather_a_wait", handle_a, o_cat)))
    h1 = _matmul("out_proj", o_cat, wt["w_out"], "nn", [F32], _add_epilogue, (x,))[0]

    handle_w2, tok = _gather2_pass("late_gather_w2_pass", handle_w2, h1)
    hn = _row_fwd(_rms_fn, "xa_norm", [(h1, d, 0)], [wt["xa_norm_w"] + tok[0:1, 0:1]], [(d, BF16)], 256)[0]
    mn = _row_fwd(_rms_fn, "mem_norm", [(mem, d, 0)], [wt["mem_norm_w"]], [(d, BF16)], 256)[0]
    q = _matmul("xa_q", hn, wt["xa_wq"], "nn", [F32])[0]
    kx = _matmul("xa_k", mn, wt["xa_wk"], "nn", [F32])[0]
    vx = _matmul("xa_v", mn, wt["xa_wv"], "nn", [F32])[0]
    ao = _row_fwd(_xattn_fn, "xattn", [(q, XA_WIDTH, 0)], [kx, vx], [(XA_WIDTH, BF16)], 256)[0]
    h2 = _matmul("xa_o", ao, wt["xa_wo"], "nn", [F32], _add_epilogue, (h1,))[0]

    f = _row_fwd(_rms_fn, "ffn_norm", [(h2, d, 0)], [wt["ffn_norm_w"]], [(d, BF16)], 256)[0]
    wt.update(_gather_finish(("ffn_w1",), _gather2_wait("late_gather_w1_wait", handle_w1, f)))
    a, hid = _matmul("ffn_up", f, wt["ffn_w1"], "nn", [F32, BF16],
                     lambda acc: (acc, jnp.square(jnp.maximum(acc, 0.0))))
    wt.update(_gather_finish(("ffn_w2",), _gather2_wait("late_gather_w2_wait", handle_w2, hid)))
    h3 = _matmul("ffn_down", hid, wt["ffn_w2"], "nn", [F32], _add_epilogue, (h2,))[0]
    loss8, dh3, g["final_norm_w"] = _loss_call(h3, target, wt["final_norm_w"])

    da = _matmul("ffn_down_dx", dh3, wt["ffn_w2"], "nt", [BF16],
                 lambda acc, av: (acc * 2.0 * jnp.maximum(av, 0.0),), (a,))[0]
    g["ffn_w2"] = _matmul("ffn_down_dw", hid, dh3, "tn", [BF16])[0]
    g["ffn_w1"] = _matmul("ffn_up_dw", f, da, "tn", [BF16])[0]
    pending = {}
    plan = _scatter_plan({n: g.pop(n) for n in grp_b})
    pending[grp_b], tok = _exchange_start("late_grad_b_start", *plan, loss8)
    dh2, g["ffn_norm_w"] = _matmul_norm_bwd("ffn_up_dx", da, wt["ffn_w1"], "nt", h2, wt["ffn_norm_w"], dh3, tok)

    dao = _matmul("xa_o_dx", dh2, wt["xa_wo"], "nt", [F32])[0]
    g["xa_wo"] = _matmul("xa_o_dw", ao, dh2, "tn", [BF16])[0]
    (dq,), (dkx, dvx) = _row_bwd(_xattn_fn, "xattn_bwd", [(q, XA_WIDTH, 0)], [kx, vx], [[(dao, XA_WIDTH, 0)]], 256)
    dh1, g["xa_norm_w"] = _matmul_norm_bwd("xa_q_dx", dq, wt["xa_wq"], "nt", h1, wt["xa_norm_w"], dh2)
    g["xa_wq"] = _matmul("xa_q_dw", hn, dq, "tn", [BF16])[0]
    g["xa_wk"] = _matmul("xa_k_dw", mn, dkx, "tn", [BF16])[0]
    g["xa_wv"] = _matmul("xa_v_dw", mn, dvx, "tn", [BF16])[0]
    dmn = _matmul("xa_k_dx", dkx, wt["xa_wk"], "nt", [F32])[0]
    dmn = _matmul("xa_v_dx", dvx, wt["xa_wv"], "nt", [F32], _add_epilogue, (dmn,))[0]
    _, (g["mem_norm_w"],) = _row_bwd(_rms_fn, "mem_norm_bwd", [(mem, d, 0)], [wt["mem_norm_w"]],
                                     [[(dmn, d, 0)]], 256, want_tiles=())

    do_cat = _matmul("out_proj_dx", dh1, wt["w_out"], "nt", [F32])[0]
    g["w_out"] = _matmul("out_proj_dw", o_cat, dh1, "tn", [BF16])[0]

    plan = _scatter_plan({n: g.pop(n) for n in grp_a})
    pending[grp_a], tok = _exchange_start("late_grad_a_start", *plan, tok)
    (dy, dr1, dk1, dv1, dgate), (g["rw_ln_w"], g["rw_ln_b"], g["rw_r_k"]) = _row_bwd(
        _rw_post_fn, "rw_post_bwd", rw_post_tiles, [rw_post_params[0] + tok[0:1, 0:1]] + rw_post_params[1:],
        [[(do_cat, RW_WIDTH, 1)]], 128)
    dr2, dlw, dk2, dv2, dal, dbe, dgcw = _scan_bwd(_rw_group, "rw_scan_bwd", rw_arrs, kept_rw, dy,
                                                   RW_WIDTH // LANES)
    one = lambda t: [(t, RW_WIDTH, 0)]
    two = lambda s, t: [(s, RW_WIDTH, 0), (t, RW_WIDTH, 0)]
    d_ps, rw_pre_grads = _row_bwd(
        _rw_pre_fn, "rw_pre_bwd", rw_pre_tiles, rw_pre_params,
        [two(dr1, dr2), one(dlw), two(dk1, dk2), two(dv1, dv2), one(dal), one(dbe), one(dgate), one(dgcw)],
        CHUNK)
    for n, val in zip(("rw_w0", "rw_a0", "rw_k_k", "rw_k_a", "rw_w2", "rw_a2", "rw_g2"), rw_pre_grads):
        g[n] = val
    dp_rw, (g["rw_mu"],) = _col_bwd(_lerp_fn, "rw_shift_bwd", p, RW_OFF // LANES, 26, [wt["rw_mu"]], list(d_ps))

    (do, dz), (g["dn_norm_w"],) = _row_bwd(_dn_post_fn, "dn_post_bwd", dn_post_tiles, [wt["dn_norm_w"]],
                                           [[(do_cat, DN_WIDTH, 0)]], 256)
    dqh, dkh, dv_dn, dgb, dbb, dgcb = _scan_bwd(_gdn_group, "gdn_scan_bwd", dn_arrs, kept_dn, do, DN_HEADS)
    one = lambda t: [(t, DN_WIDTH, 0)]
    (dcq, dck, dgates), (g["dn_a_log"], g["dn_dt_bias"]) = _row_bwd(
        _dn_pre_fn, "dn_pre_bwd", dn_pre_tiles, dn_pre_params,
        [one(dqh), one(dkh), one(dgb), one(dbb), one(dgcb)], CHUNK)
    dp_qkv, (g["dn_conv_w"],) = _col_bwd(_conv_fn, "dn_conv_bwd", p, 0, 24, [wt["dn_conv_w"]], [dcq, dck, dv_dn])
    dp = jnp.concatenate([t.astype(BF16) for t in (dp_qkv, dz, dgates, dp_rw, jnp.zeros((x.shape[0], LANES), F32))],
                         axis=1)
    g["w_in"] = _matmul("in_proj_dw", dp, u, "tn", [BF16], tm=1536)[0]
    early = _logical_grads(g)
    blocks = []
    for src, cols in _scatter_plan({n: early.pop(n) for n in EARLY})[0]:
        if cols is not None:
            src = src.reshape(src.shape[0], N_DEV, cols).transpose(1, 0, 2)
        blocks.append(src.reshape((4, 2) + src.shape[1:]))
    sums = [_pair_add("early_grad_pair_add_%d" % i, mine, theirs)
            for i, (mine, theirs) in enumerate(zip(blocks, _pair_swap("early_grad_pair_swap", blocks)))]
    pending[EARLY], tok = _exchange_start("early_grad_start", [(t, None) for t in sums],
                                          [(t.shape, t.dtype, None) for t in sums], False, tok, chips=True)
    dx, early["mix_norm_w"] = _matmul_norm_bwd("in_proj_dx", dp, wt["w_in"], "nn", x, wt["mix_norm_w"], dh1, tok)
    return loss8, dx, early, pending, tok


WEIGHTS = ["mix_norm_w", "w_in", "dn_conv_w", "dn_a_log", "dn_dt_bias", "dn_norm_w", "rw_mu", "rw_w0", "rw_w2",
           "rw_a0", "rw_a2", "rw_g2", "rw_k_k", "rw_k_a", "rw_r_k", "rw_ln_w", "rw_ln_b", "w_out", "xa_norm_w",
           "mem_norm_w", "xa_wq", "xa_wk", "xa_wv", "xa_wo", "ffn_norm_w", "ffn_w1", "ffn_w2", "final_norm_w"]
SHARDED = {"w_in": False, "w_out": False, "xa_wq": False, "xa_wk": False, "xa_wv": False, "xa_wo": True,
           "ffn_w1": True, "ffn_w2": False, "dn_conv_w": True, "rw_w2": True, "rw_a2": True, "rw_g2": True}
BF16_PAYLOAD = ("w_in", "w_out", "xa_wq", "xa_wk", "xa_wv", "xa_wo", "ffn_w1", "ffn_w2")
REPLICATED = [n for n in WEIGHTS if n not in SHARDED]
EARLY = ("w_in", "dn_conv_w", "rw_w2", "rw_a2", "rw_g2")
RW_IN_COLS = IN_COLS - DN_COLS
W_IN_SHARD = IN_COLS // N_DEV


def _layout_weights(fw):
    wt = dict(fw)
    wt["dn_conv_w"] = jnp.pad(fw["dn_conv_w"], ((0, 4), (0, 0)))
    wt["dn_a_log"] = jnp.pad(fw["dn_a_log"], ((0, 0), (0, LANES - DN_HEADS)))
    wt["dn_dt_bias"] = jnp.pad(fw["dn_dt_bias"], ((0, 0), (0, LANES - DN_HEADS)))
    wt["rw_w2"] = jnp.pad(fw["rw_w2"], ((0, 64), (0, 0)))
    wt["rw_a2"] = jnp.pad(fw["rw_a2"], ((64, 0), (0, 0)))
    return wt


def _logical_grads(g):
    out = dict(g)
    out["w_in"] = _w_in_grad_to_shards(g["w_in"])
    out["dn_conv_w"] = g["dn_conv_w"][:4]
    out["dn_a_log"] = g["dn_a_log"][:, :DN_HEADS]
    out["dn_dt_bias"] = g["dn_dt_bias"][:, :DN_HEADS]
    out["rw_w2"] = g["rw_w2"][:64]
    out["rw_a2"] = g["rw_a2"][64:]
    return out


def _pack(vals):
    parts = []
    for v in vals:
        flat = v.reshape(-1)
        parts.append(jnp.pad(flat, (0, -flat.shape[0] % LANES)))
    flat = jnp.concatenate(parts)
    flat = jnp.pad(flat, (0, -flat.shape[0] % (8 * LANES)))
    return flat.reshape(-1, LANES)


def _unpack(packed, shapes):
    flat = packed.reshape(-1)
    out, at = [], 0
    for shp in shapes:
        size = math.prod(shp)
        out.append(flat[at:at + size].reshape(shp))
        at += size + (-size % LANES)
    return out


def kernel(x, mem, mix_norm_w, w_in, dn_conv_w, dn_a_log, dn_dt_bias, dn_norm_w, rw_mu, rw_w0, rw_w2, rw_a0, rw_a2, rw_g2, rw_k_k, rw_k_a, rw_r_k, rw_ln_w, rw_ln_b, w_out, xa_norm_w, mem_norm_w, xa_wq, xa_wk, xa_wv, xa_wo, ffn_norm_w, ffn_w1, ffn_w2, final_norm_w, loss_target, m_mix_norm_w, m_w_in, m_dn_conv_w, m_dn_a_log, m_dn_dt_bias, m_dn_norm_w, m_rw_mu, m_rw_w0, m_rw_w2, m_rw_a0, m_rw_a2, m_rw_g2, m_rw_k_k, m_rw_k_a, m_rw_r_k, m_rw_ln_w, m_rw_ln_b, m_w_out, m_xa_norm_w, m_mem_norm_w, m_xa_wq, m_xa_wk, m_xa_wv, m_xa_wo, m_ffn_norm_w, m_ffn_w1, m_ffn_w2, m_final_norm_w, v_mix_norm_w, v_w_in, v_dn_conv_w, v_dn_a_log, v_dn_dt_bias, v_dn_norm_w, v_rw_mu, v_rw_w0, v_rw_w2, v_rw_a0, v_rw_a2, v_rw_g2, v_rw_k_k, v_rw_k_a, v_rw_r_k, v_rw_ln_w, v_rw_ln_b, v_w_out, v_xa_norm_w, v_mem_norm_w, v_xa_wq, v_xa_wk, v_xa_wv, v_xa_wo, v_ffn_norm_w, v_ffn_w1, v_ffn_w2, v_final_norm_w):
    given = dict(locals())
    w = {n: given[n] for n in WEIGHTS}
    m = {n: given["m_" + n] for n in WEIGHTS}
    v = {n: given["v_" + n] for n in WEIGHTS}

    local = {n: (lambda t: t[0].T) if n == "w_in" else (lambda t: t[0]) for n in SHARDED}
    shards = {n: (local[n](w[n]).astype(BF16) if n in BF16_PAYLOAD else local[n](w[n])) for n in SHARDED}
    srcs, dsts, _ = _gather_plan({n: shards[n] for n in EARLY})
    full = _gather_finish(EARLY, _gather_two_level("early_all_gather", srcs, dsts))
    for n in REPLICATED:
        full[n] = w[n].reshape(1, -1)

    loss8, dx, g, pending, after = _local_step(x[0], mem[0], loss_target[0], _layout_weights(full),
                                               {n: shards[n] for n in SHARDED if n not in EARLY})

    packed = _pack([g[n] for n in REPLICATED] + [loss8[:1, :1]])
    small, _ = _exchange_start("small_gather_start", [(packed, None)], [((N_DEV,) + packed.shape, F32, None)], True,
                               after)
    grad, delta, new_m, new_v = {}, {}, {}, {}
    done = [dx]

    def tie():
        return jnp.broadcast_to(sum(t[:1, :1] for t in done), (8, LANES))

    for names in sorted(pending, key=lambda names: names == EARLY):
        handle = pending[names]
        for n, parts in zip(names, _exchange_wait("grad_wait_" + names[0], handle, tie())):
            res = _sum_adamw("adamw_" + n, parts, local[n](w[n]), local[n](m[n]), local[n](v[n]))
            grad[n], delta[n], new_m[n], new_v[n] = [(t.T if n == "w_in" else t)[None] for t in res]
            done.append(res[1])

    (parts,) = _exchange_wait("small_gather_wait", small, tie())
    blank = [jnp.zeros((1, 1), F32)]
    res = _sum_adamw("adamw_small", parts, _pack([w[n] for n in REPLICATED] + blank),
                     _pack([m[n] for n in REPLICATED] + blank), _pack([v[n] for n in REPLICATED] + blank))
    shapes = [w[n].shape for n in REPLICATED] + [()]
    loss = _unpack(res[0], shapes)[-1]
    for store, packed_out in zip((grad, delta, new_m, new_v), res):
        for n, val in zip(REPLICATED, _unpack(packed_out, shapes)):
            store[n] = val

    return (loss, dx[None], *[grad[n] for n in WEIGHTS], *[delta[n] for n in WEIGHTS],
            *[new_m[n] for n in WEIGHTS], *[new_v[n] for n in WEIGHTS])
```

```python
import functools
import math

import jax
import jax.numpy as jnp
from jax import lax
from jax.experimental import pallas as pl
from jax.experimental.pallas import tpu as pltpu

F32 = jnp.float32
BF16 = jnp.bfloat16
SDS = jax.ShapeDtypeStruct

N_DEV = 8
D_MODEL = 2048
LANES = 128
CHUNK = 128
DN_HEADS = 8
DN_WIDTH = 1024
RW_WIDTH = 1024
RW_HEAD = 64
XA_HEADS = 4
XA_WIDTH = 512
FFN_HIDDEN = 8192
IN_COLS = 7440
DN_COLS = 4112
IN_PAD = 7680
RW_OFF = 4224
RMS_EPS = 1e-6
RW_GN_EPS = 64e-5
VMEM_LIMIT = 56 * 1024 * 1024

ADAM_LR = 0.001
ADAM_B1 = 0.9
ADAM_B2 = 0.999
ADAM_EPS = 1e-08
ADAM_WD = 0.01
ADAM_STEP = 10

_DIMS = {"nn": (((1,), (0,)), ((), ())), "nt": (((1,), (1,)), ((), ())), "tn": (((0,), (0,)), ((), ()))}


def _raw_dot(a, b, mode, hi):
    if hi:
        return lax.dot_general(a, b, _DIMS[mode], precision=lax.Precision.HIGHEST, preferred_element_type=F32)
    return lax.dot_general(a.astype(BF16), b.astype(BF16), _DIMS[mode], preferred_element_type=F32)


@functools.partial(jax.custom_vjp, nondiff_argnums=(2, 3))
def mm(a, b, mode="nn", hi=False):
    return _raw_dot(a, b, mode, hi)


def _mm_fwd(a, b, mode, hi):
    return _raw_dot(a, b, mode, hi), (a, b)


def _mm_bwd(mode, hi, res, g):
    a, b = res
    if mode == "nn":
        return _raw_dot(g, b, "nt", hi), _raw_dot(a, g, "tn", hi)
    if mode == "nt":
        return _raw_dot(g, b, "nn", hi), _raw_dot(g, a, "tn", hi)
    return _raw_dot(b, g, "nt", hi), _raw_dot(a, g, "nn", hi)


mm.defvjp(_mm_fwd, _mm_bwd)


def _shift_rows_raw(x, k):
    n = x.shape[0]
    rolled = pltpu.roll(x, k % n, axis=0)
    row = lax.broadcasted_iota(jnp.int32, x.shape, 0)
    keep = row >= k if k > 0 else row < n + k
    return jnp.where(keep, rolled, 0.0)


@functools.partial(jax.custom_vjp, nondiff_argnums=(1,))
def shift_rows(x, k):
    return _shift_rows_raw(x, k)


shift_rows.defvjp(lambda x, k: (_shift_rows_raw(x, k), None), lambda k, _, g: (_shift_rows_raw(g, -k),))


@jax.custom_vjp
def _sigmoid(x):
    return 1.0 / (1.0 + jnp.exp(-x))


def _sigmoid_fwd(x):
    s = 1.0 / (1.0 + jnp.exp(-x))
    return s, s


_sigmoid.defvjp(_sigmoid_fwd, lambda s, g: (g * s * (1.0 - s),))


@jax.custom_vjp
def _softplus(x):
    return jnp.maximum(x, 0.0) + jnp.log(1.0 + jnp.exp(-jnp.abs(x)))


_softplus.defvjp(lambda x: (_softplus(x), x), lambda x, g: (g / (1.0 + jnp.exp(-x)),))


@jax.custom_vjp
def _silu(x):
    return x / (1.0 + jnp.exp(-x))


def _silu_fwd(x):
    s = 1.0 / (1.0 + jnp.exp(-x))
    return x * s, (x, s)


_silu.defvjp(_silu_fwd, lambda res, g: (g * res[1] * (1.0 + res[0] * (1.0 - res[1])),))


def _tri_masks(n):
    ii = lax.broadcasted_iota(jnp.int32, (n, n), 0)
    jj = lax.broadcasted_iota(jnp.int32, (n, n), 1)
    return ii >= jj, ii > jj, ii == jj


def _neumann_inv_raw(m):
    n = m.shape[0]
    _, _, eye = _tri_masks(n)
    eye = jnp.where(eye, 1.0, 0.0)
    p = eye + m
    mk = m
    for _ in range(int(math.log2(n)) - 1):
        mk = _raw_dot(mk, mk, "nn", False)
        p = p + _raw_dot(p, mk, "nn", False)
    m_hi, p_hi = m.astype(BF16), p.astype(BF16)
    m_lo, p_lo = m - m_hi.astype(F32), p - p_hi.astype(F32)
    mp = _raw_dot(m_hi, p_hi, "nn", False) + _raw_dot(m_hi, p_lo, "nn", False) + _raw_dot(m_lo, p_hi, "nn", False)
    return p + _raw_dot(p, eye - p + mp, "nn", False)


@jax.custom_vjp
def _neumann_inv(m):
    return _neumann_inv_raw(m)


def _neumann_inv_fwd(m):
    p = _neumann_inv_raw(m)
    return p, p


def _neumann_inv_bwd(p, g):
    return (_raw_dot(_raw_dot(p, g, "tn", False), p, "nt", False),)


_neumann_inv.defvjp(_neumann_inv_fwd, _neumann_inv_bwd)


@jax.custom_vjp
def _saved_inv(m, p):
    return p


_saved_inv.defvjp(lambda m, p: (p, p), lambda p, g: (_neumann_inv_bwd(p, g)[0], jnp.zeros_like(p)))


def _inverse(m, saved):
    return _neumann_inv(m) if saved is None else _saved_inv(m, saved)


def _cumsum_rows(x):
    causal, _, _ = _tri_masks(x.shape[0])
    return mm(jnp.where(causal, 1.0, 0.0), x, "nn", True)


def _gdn_group(s0, q, k, v, gb, bb, gc, *saved):
    diff = jnp.stack([gc[j] - gc[j].T for j in range(gc.shape[0])])
    return jax.vmap(_gdn_chunk)(s0, q, k, v, gb, bb, gc, diff, *saved)


def _rw_group(*args):
    return jax.vmap(_rw_chunk)(*args)


def _gdn_chunk(s0, q, k, v, gb, bb, gc, diff, saved=None):
    c = q.shape[0]
    causal, strict, _ = _tri_masks(c)
    decay = jnp.exp(jnp.where(causal, diff, -jnp.inf))
    kb = k * bb
    a = jnp.where(strict, mm(kb, k, "nt") * decay, 0.0)
    p = _inverse(-a, saved)
    uw = mm(p, jnp.concatenate([v * bb, kb * jnp.exp(gc)], axis=1))
    u, w = uw[:, :LANES], uw[:, LANES:]
    attn = mm(q, k, "nt") * decay
    v_new = u - mm(w, s0)
    o = mm(q * jnp.exp(gc), s0) + mm(attn, v_new)
    g_last = jnp.sum(gb, axis=0, keepdims=True)
    s1 = s0 * jnp.exp(g_last) + mm(k * jnp.exp(g_last - gc), v_new, "tn")
    return o, s1, p


def _rw_chunk(s0, r, lw, k, v, al, be, gc, saved0=None, saved1=None):
    c = r.shape[0]
    causal, strict, _ = _tri_masks(c)
    gp = gc - lw
    row = lax.broadcasted_iota(jnp.int32, lw.shape, 0)
    lane = lax.broadcasted_iota(jnp.int32, lw.shape, 1)
    g_mid = jnp.sum(jnp.where(row < c // 2, lw, 0.0), axis=0, keepdims=True)
    g_last = jnp.sum(lw, axis=0, keepdims=True)
    e_n = jnp.exp(g_mid - gc)
    rg = r * jnp.exp(gc - g_mid)
    bg = be * jnp.exp(gp - g_mid)
    an = al * e_n
    kn = k * e_n
    bt = mm(be * jnp.exp(gp), s0, "nt")
    rt = mm(r * jnp.exp(gc), s0, "nt")
    us, ys, ps = [], [], []
    ank = jnp.concatenate([an, kn], axis=0)
    for h, saved in enumerate((saved0, saved1)):
        mine = (lane >= RW_HEAD) if h else (lane < RW_HEAD)
        from_b = mm(jnp.where(mine, bg, 0.0), ank, "nt")
        from_r = mm(jnp.where(mine, rg, 0.0), ank, "nt")
        a_ab = jnp.where(strict, from_b[:, :c], 0.0)
        a_kb = jnp.where(strict, from_b[:, c:], 0.0)
        a_ra = jnp.where(causal, from_r[:, :c], 0.0)
        a_rk = jnp.where(causal, from_r[:, c:], 0.0)
        p = _inverse(a_ab, saved)
        ps.append(p)
        u_h = mm(p, bt + mm(a_kb, v))
        us.append(u_h)
        ys.append(rt + mm(a_ra, u_h) + mm(a_rk, v))
    lo = lane < RW_HEAD
    u = jnp.where(lo, us[0], us[1])
    y = jnp.where(lo, ys[0], ys[1])
    tail = jnp.exp(g_last - gc)
    s1 = s0 * jnp.exp(g_last) + mm(u, al * tail, "tn") + mm(v, k * tail, "tn")
    vi = lax.broadcasted_iota(jnp.int32, s0.shape, 0)
    ki = lax.broadcasted_iota(jnp.int32, s0.shape, 1)
    s1 = jnp.where((vi < RW_HEAD) == (ki < RW_HEAD), s1, 0.0)
    return y, s1, ps[0], ps[1]


SCAN_HB = 8


def _scan_specs(arrs, n_chunks, reverse):
    def spec(off):
        assert off % SCAN_HB == 0
        if reverse:
            return pl.BlockSpec((CHUNK, SCAN_HB * LANES), lambda h, n: (n_chunks - 1 - n, off // SCAN_HB + h))
        return pl.BlockSpec((CHUNK, SCAN_HB * LANES), lambda h, n: (n, off // SCAN_HB + h))
    return [spec(off) for _, off in arrs]


def _split_heads(x):
    return jnp.stack([x[:, LANES * j:LANES * (j + 1)] for j in range(SCAN_HB)], axis=0)


def _merge_heads(x):
    return jnp.concatenate([x[j] for j in range(SCAN_HB)], axis=1)


def _scan_fwd(group_fn, name, arrs, heads, n_kept):
    s = arrs[0][0].shape[0]
    n_chunks = s // CHUNK
    n_in = len(arrs)

    def body(*refs):
        y_ref, st_ref = refs[n_in:n_in + 2]
        kept_refs, s_scr = refs[n_in + 2:-1], refs[-1]

        @pl.when(pl.program_id(1) == 0)
        def _():
            s_scr[...] = jnp.zeros_like(s_scr)

        s0 = s_scr[...]
        st_ref[...] = s0
        y, s1, *kept = group_fn(s0, *[_split_heads(r[...]) for r in refs[:n_in]])
        y_ref[...] = _merge_heads(y)
        s_scr[...] = s1
        for ref, val in zip(kept_refs, kept):
            ref[...] = val

    per_chunk = pl.BlockSpec((SCAN_HB, None, LANES, LANES), lambda h, n: (h, n, 0, 0))
    res = pl.pallas_call(
        body, grid=(heads // SCAN_HB, n_chunks), name=name,
        in_specs=_scan_specs(arrs, n_chunks, False),
        out_specs=[pl.BlockSpec((CHUNK, SCAN_HB * LANES), lambda h, n: (n, h))] + [per_chunk] * (1 + n_kept),
        out_shape=[SDS((s, heads * LANES), F32)] + [SDS((heads, n_chunks, LANES, LANES), F32)] * (1 + n_kept),
        scratch_shapes=[pltpu.VMEM((SCAN_HB, LANES, LANES), F32)],
        compiler_params=pltpu.CompilerParams(dimension_semantics=("arbitrary", "arbitrary")),
    )(*[a for a, _ in arrs])
    return res[0], res[1:]


def _scan_bwd(group_fn, name, arrs, kept, dy, heads):
    s = arrs[0][0].shape[0]
    n_chunks = s // CHUNK
    n_in, n_kept = len(arrs), len(kept)

    def body(*refs):
        kept_vals = [r[...] for r in refs[n_in:n_in + n_kept]]
        dy_ref = refs[n_in + n_kept]
        d_refs = refs[n_in + n_kept + 1:2 * n_in + n_kept + 1]
        ds_scr = refs[-1]

        @pl.when(pl.program_id(1) == 0)
        def _():
            ds_scr[...] = jnp.zeros_like(ds_scr)

        def fn(s0, *ins):
            return group_fn(s0, *ins, *kept_vals[1:])[:2]

        _, vjp = jax.vjp(fn, kept_vals[0], *[_split_heads(r[...]) for r in refs[:n_in]])
        grads = vjp((_split_heads(dy_ref[...]), ds_scr[...]))
        ds_scr[...] = grads[0]
        for ref, g in zip(d_refs, grads[1:]):
            ref[...] = _merge_heads(g)

    rev = pl.BlockSpec((CHUNK, SCAN_HB * LANES), lambda h, n: (n_chunks - 1 - n, h))
    per_chunk = pl.BlockSpec((SCAN_HB, None, LANES, LANES), lambda h, n: (h, n_chunks - 1 - n, 0, 0))
    return pl.pallas_call(
        body, grid=(heads // SCAN_HB, n_chunks), name=name,
        in_specs=_scan_specs(arrs, n_chunks, True) + [per_chunk] * n_kept + [rev],
        out_specs=[rev] * n_in,
        out_shape=[SDS((s, heads * LANES), F32)] * n_in,
        scratch_shapes=[pltpu.VMEM((SCAN_HB, LANES, LANES), F32)],
        compiler_params=pltpu.CompilerParams(dimension_semantics=("arbitrary", "arbitrary")),
    )(*[a for a, _ in arrs], *kept, dy)


def _col_spec(tr, width, cb):
    return pl.BlockSpec((tr, width), lambda i: (i, cb))


def _whole(p):
    return pl.BlockSpec(p.shape, lambda i: (0,) * p.ndim)


def _row_fwd(fn, name, tiles, params, outs, tr):
    rows = tiles[0][0].shape[0]
    nt, npar = len(tiles), len(params)

    def body(*refs):
        vals = [r[...].astype(F32) for r in refs[:nt + npar]]
        for ref, o in zip(refs[nt + npar:], fn(*vals)):
            ref[...] = o.astype(ref.dtype)

    return pl.pallas_call(
        body, grid=(rows // tr,), name=name,
        in_specs=[_col_spec(tr, w, cb) for _, w, cb in tiles] + [_whole(p) for p in params],
        out_specs=[_col_spec(tr, w, 0) for w, _ in outs],
        out_shape=[SDS((rows, w), dt) for w, dt in outs],
        compiler_params=pltpu.CompilerParams(dimension_semantics=("arbitrary",), vmem_limit_bytes=VMEM_LIMIT),
    )(*[a for a, _, _ in tiles], *params)


def _row_bwd(fn, name, tiles, params, cts, tr, want_tiles=None):
    rows = tiles[0][0].shape[0]
    nt, npar = len(tiles), len(params)
    want = list(range(nt)) if want_tiles is None else list(want_tiles)
    flat_cts = [c for group in cts for c in group]
    n_ct = len(flat_cts)

    def body(*refs):
        vals = [r[...].astype(F32) for r in refs[:nt + npar]]
        ct_refs = refs[nt + npar:nt + npar + n_ct]
        out_refs = refs[nt + npar + n_ct:]
        ct_vals, at = [], 0
        for group in cts:
            total = ct_refs[at][...].astype(F32)
            for r in ct_refs[at + 1:at + len(group)]:
                total = total + r[...].astype(F32)
            ct_vals.append(total)
            at += len(group)
        _, vjp = jax.vjp(lambda *a: tuple(fn(*a)), *vals)
        grads = vjp(tuple(ct_vals))
        for ref, t in zip(out_refs[:len(want)], want):
            ref[...] = grads[t]
        first = pl.program_id(0) == 0
        for ref, g in zip(out_refs[len(want):], grads[nt:]):
            @pl.when(first)
            def _(ref=ref, g=g):
                ref[...] = g

            @pl.when(jnp.logical_not(first))
            def _(ref=ref, g=g):
                ref[...] += g

    res = pl.pallas_call(
        body, grid=(rows // tr,), name=name,
        in_specs=[_col_spec(tr, w, cb) for _, w, cb in tiles] + [_whole(p) for p in params]
        + [_col_spec(tr, w, cb) for _, w, cb in flat_cts],
        out_specs=[_col_spec(tr, tiles[t][1], 0) for t in want] + [_whole(p) for p in params],
        out_shape=[SDS((rows, tiles[t][1]), F32) for t in want] + [SDS(p.shape, F32) for p in params],
        compiler_params=pltpu.CompilerParams(dimension_semantics=("arbitrary",), vmem_limit_bytes=VMEM_LIMIT),
    )(*[a for a, _, _ in tiles], *params, *[a for a, _, _ in flat_cts])
    return res[:len(want)], res[len(want):]


def _col_fwd(fn, name, x, first_block, n_blocks, params):
    rows = x.shape[0]

    def body(*refs):
        refs[-1][...] = fn(*[r[...] for r in refs[:-1]])

    return pl.pallas_call(
        body, grid=(n_blocks,), name=name,
        in_specs=[pl.BlockSpec((rows, LANES), lambda j: (0, first_block + j))]
        + [pl.BlockSpec((p.shape[0], LANES), lambda j: (0, j)) for p in params],
        out_specs=pl.BlockSpec((rows, LANES), lambda j: (0, j)),
        out_shape=SDS((rows, n_blocks * LANES), F32),
        compiler_params=pltpu.CompilerParams(dimension_semantics=("arbitrary",), vmem_limit_bytes=VMEM_LIMIT),
    )(x, *params)


def _col_bwd(fn, name, x, first_block, n_blocks, params, dys):
    rows = x.shape[0]
    npar, nd = len(params), len(dys)
    starts = [sum(t.shape[1] for t in dys[:i]) // LANES for i in range(nd + 1)]

    def body(*refs):
        vals = [r[...] for r in refs[:1 + npar]]
        j = pl.program_id(0)
        dy = refs[1 + npar][...]
        for i in range(1, nd):
            dy = jnp.where(j >= starts[i], refs[1 + npar + i][...], dy)
        _, vjp = jax.vjp(fn, *vals)
        grads = vjp(dy)
        for ref, g in zip(refs[1 + npar + nd:], grads):
            ref[...] = g.astype(ref.dtype)

    def piece(i):
        last = starts[i + 1] - starts[i] - 1
        return pl.BlockSpec((rows, LANES), lambda j: (0, jnp.clip(j - starts[i], 0, last)))

    pspecs = [pl.BlockSpec((p.shape[0], LANES), lambda j: (0, j)) for p in params]
    blk = pl.BlockSpec((rows, LANES), lambda j: (0, j))
    res = pl.pallas_call(
        body, grid=(n_blocks,), name=name,
        in_specs=[pl.BlockSpec((rows, LANES), lambda j: (0, first_block + j))] + pspecs + [piece(i) for i in range(nd)],
        out_specs=[blk] + pspecs,
        out_shape=[SDS((rows, n_blocks * LANES), BF16)] + [SDS(p.shape, F32) for p in params],
        compiler_params=pltpu.CompilerParams(dimension_semantics=("arbitrary",), vmem_limit_bytes=VMEM_LIMIT),
    )(x, *params, *dys)
    return res[0], res[1:]


def _conv_fn(x, w):
    acc = x * w[3:4, :]
    for j in range(3):
        acc = acc + shift_rows(x, 3 - j) * w[j:j + 1, :]
    return _silu(acc)


def _lerp_fn(x, mu):
    return x + (shift_rows(x, 1) - x) * mu[0:1, :]


def _seg_sum(x, width):
    if width == LANES:
        return jnp.sum(x, axis=1, keepdims=True)
    lo = lax.broadcasted_iota(jnp.int32, x.shape, 1) < width
    s0 = jnp.sum(jnp.where(lo, x, 0.0), axis=1, keepdims=True)
    s1 = jnp.sum(jnp.where(lo, 0.0, x), axis=1, keepdims=True)
    return jnp.where(lo, s0, s1)


def _per_block(fn, *xs):
    n = xs[0].shape[1] // LANES
    return jnp.concatenate([fn(*[x[:, LANES * b:LANES * (b + 1)] for x in xs]) for b in range(n)], axis=1)


def _head_expand(col0):
    r = lax.broadcasted_iota(jnp.int32, (LANES, DN_WIDTH), 0)
    c = lax.shift_right_logical(lax.broadcasted_iota(jnp.int32, (LANES, DN_WIDTH), 1), 7)
    return jnp.where(r == c + col0, 1.0, 0.0)


def _dn_pre_fn(cq, ck, gates, a_log, dt_bias):
    l2 = lambda x: x * lax.rsqrt(_seg_sum(x * x, LANES) + 1e-6)
    qh = _per_block(l2, cq) * (LANES ** -0.5)
    kh = _per_block(l2, ck)
    g = -jnp.exp(a_log) * _softplus(gates + dt_bias)
    gb = mm(g, _head_expand(0), "nn", True)
    bb = mm(_sigmoid(gates), _head_expand(DN_HEADS), "nn", True)
    return qh, kh, gb, bb, _cumsum_rows(gb)


def _dn_post_fn(o, z, nw):
    def one(ob, zb):
        return ob * lax.rsqrt(_seg_sum(ob * ob, LANES) * (1.0 / LANES) + RMS_EPS) * nw * _silu(zb)
    return (_per_block(one, o, z),)


def _rw_pre_fn(pr, pk, pv, pwa, pg, w0, a0, k_k, k_a, w2p, a2p, g2):
    log_w = -_softplus(-(w0 + mm(jnp.tanh(pwa), w2p))) - 0.5
    lw = -jnp.exp(log_w)
    a = _sigmoid(a0 + mm(pwa, a2p))
    gate = mm(_sigmoid(pg), g2)
    kk = pk * k_k
    kk = _per_block(lambda x: x / jnp.maximum(jnp.sqrt(_seg_sum(x * x, RW_HEAD)), 1e-12), kk)
    k = pk * (1.0 + (a - 1.0) * k_a)
    return pr, lw, k, pv, kk * a, -kk, gate, _cumsum_rows(lw)


def _rw_post_fn(y, r, k, v, gate, ln_w, ln_b, r_k):
    def one(yb, rb, kb, vb, gb, wb, bb, rkb):
        d = yb - _seg_sum(yb, RW_HEAD) * (1.0 / RW_HEAD)
        var = _seg_sum(d * d, RW_HEAD) * (1.0 / RW_HEAD)
        yn = d * lax.rsqrt(var + RW_GN_EPS) * wb + bb
        return (yn + _seg_sum(rb * kb * rkb, RW_HEAD) * vb) * gb
    return (_per_block(one, y, r, k, v, gate, ln_w, ln_b, r_k),)


def _rms_fn(h, w):
    return (h * lax.rsqrt(jnp.mean(h * h, axis=1, keepdims=True) + RMS_EPS) * w,)


def _xattn_fn(q, k, v):
    outs = []
    for h in range(XA_HEADS):
        sl = slice(LANES * h, LANES * (h + 1))
        s = mm(q[:, sl], k[:, sl], "nt") * (LANES ** -0.5)
        e = jnp.exp(s - jnp.max(s, axis=1, keepdims=True))
        outs.append(mm(e / jnp.sum(e, axis=1, keepdims=True), v[:, sl]))
    return (jnp.concatenate(outs, axis=1),)


def _fit(tile, dim):
    best = [t for t in range(LANES, min(tile, dim) + 1, LANES) if dim % t == 0]
    assert best, (tile, dim)
    return best[-1]


def _matmul(name, a, b, mode, out_dtypes, epilogue=None, extras=(), tm=1024, tn=1024, tk=2048, after=None):
    if mode == "tn":
        (k_dim, m), n = a.shape, b.shape[1]
    else:
        (m, k_dim), n = a.shape, (b.shape[1] if mode == "nn" else b.shape[0])
    tm, tn, tk = _fit(tm, m), _fit(tn, n), _fit(tk, k_dim)
    nk = k_dim // tk
    a_spec = (pl.BlockSpec((tk, tm), lambda i, j, k: (k, i)) if mode == "tn"
              else pl.BlockSpec((tm, tk), lambda i, j, k: (i, k)))
    b_spec = (pl.BlockSpec((tn, tk), lambda i, j, k: (j, k)) if mode == "nt"
              else pl.BlockSpec((tk, tn), lambda i, j, k: (k, j)))
    o_spec = pl.BlockSpec((tm, tn), lambda i, j, k: (i, j))
    n_ex, n_out = len(extras), len(out_dtypes)
    ties = [] if after is None else [after]

    def finish(total, rest):
        ex = [r[...].astype(F32) for r in rest[:n_ex]]
        res = epilogue(total, *ex) if epilogue else (total,)
        for ref, o in zip(rest[n_ex + len(ties):n_ex + len(ties) + n_out], res):
            ref[...] = o.astype(ref.dtype)

    def body_single(a_ref, b_ref, *rest):
        finish(_raw_dot(a_ref[...], b_ref[...], mode, False), rest)

    def body_acc(a_ref, b_ref, *rest):
        acc = rest[-1]
        k = pl.program_id(2)

        @pl.when(k == 0)
        def _():
            acc[...] = jnp.zeros_like(acc)

        acc[...] += _raw_dot(a_ref[...], b_ref[...], mode, False)

        @pl.when(k == nk - 1)
        def _():
            finish(acc[...], rest)

    res = pl.pallas_call(
        body_single if nk == 1 else body_acc, grid=(m // tm, n // tn, nk), name=name,
        in_specs=[a_spec, b_spec] + [o_spec] * n_ex + [pl.BlockSpec((8, LANES), lambda i, j, k: (0, 0))] * len(ties),
        out_specs=[o_spec] * n_out,
        out_shape=[SDS((m, n), dt) for dt in out_dtypes],
        scratch_shapes=[] if nk == 1 else [pltpu.VMEM((tm, tn), F32)],
        compiler_params=pltpu.CompilerParams(dimension_semantics=("parallel", "parallel", "arbitrary"),
                                             vmem_limit_bytes=VMEM_LIMIT),
    )(a, b, *extras, *ties)
    return res


def _matmul_norm_bwd(name, a, b, mode, h, w, dres, after=None, tm=512, tk=1024):
    m, n = h.shape
    k_dim = a.shape[1]
    tm, tk = _fit(tm, m), _fit(tk, k_dim)
    nk = k_dim // tk
    ties = [] if after is None else [after]
    a_spec = pl.BlockSpec((tm, tk), lambda i, k: (i, k))
    b_spec = pl.BlockSpec((n, tk), lambda i, k: (0, k)) if mode == "nt" else pl.BlockSpec((tk, n), lambda i, k: (k, 0))
    row = pl.BlockSpec((tm, n), lambda i, k: (i, 0))
    w_spec = pl.BlockSpec((1, n), lambda i, k: (0, 0))

    def body(a_ref, b_ref, h_ref, w_ref, dres_ref, *rest):
        dh_ref, dw_ref, acc = rest[len(ties):]
        i, k = pl.program_id(0), pl.program_id(1)

        @pl.when(k == 0)
        def _():
            acc[...] = jnp.zeros_like(acc)

        acc[...] += _raw_dot(a_ref[...], b_ref[...], mode, False)

        @pl.when(k == nk - 1)
        def _():
            _, vjp = jax.vjp(_rms_res_fn, h_ref[...], w_ref[...])
            dh, dw = vjp((acc[...], dres_ref[...]))
            dh_ref[...] = dh

            @pl.when(i == 0)
            def _():
                dw_ref[...] = dw

            @pl.when(i != 0)
            def _():
                dw_ref[...] += dw

    return pl.pallas_call(
        body, grid=(m // tm, nk), name=name,
        in_specs=[a_spec, b_spec, row, w_spec, row] + [pl.BlockSpec((8, LANES), lambda i, k: (0, 0))] * len(ties),
        out_specs=[row, w_spec],
        out_shape=[SDS((m, n), F32), SDS((1, n), F32)],
        scratch_shapes=[pltpu.VMEM((tm, n), F32)],
        compiler_params=pltpu.CompilerParams(dimension_semantics=("arbitrary", "arbitrary"),
                                             vmem_limit_bytes=VMEM_LIMIT),
    )(a, b, h, w, dres, *ties)


def _loss_call(h, target, w, tr=256):
    rows, d = h.shape

    def fn(hv, wv, tv):
        y = _rms_fn(hv, wv)[0]
        return 0.5 * jnp.sum(jnp.mean(jnp.square(y - tv), axis=1, keepdims=True), axis=0, keepdims=True)

    def body(h_ref, t_ref, w_ref, loss_ref, dh_ref, dw_ref):
        tv = t_ref[...]
        val, vjp = jax.vjp(lambda hv, wv: fn(hv, wv, tv), h_ref[...], w_ref[...])
        dh, dw = vjp(jnp.ones((1, 1), F32))
        dh_ref[...] = dh
        first = pl.program_id(0) == 0

        @pl.when(first)
        def _():
            loss_ref[...] = jnp.broadcast_to(val, loss_ref.shape)
            dw_ref[...] = dw

        @pl.when(jnp.logical_not(first))
        def _():
            loss_ref[...] += jnp.broadcast_to(val, loss_ref.shape)
            dw_ref[...] += dw

    return pl.pallas_call(
        body, grid=(rows // tr,), name="loss_head",
        in_specs=[_col_spec(tr, d, 0), _col_spec(tr, d, 0), _whole(w)],
        out_specs=[pl.BlockSpec((8, LANES), lambda i: (0, 0)), _col_spec(tr, d, 0), _whole(w)],
        out_shape=[SDS((8, LANES), F32), SDS((rows, d), F32), SDS(w.shape, F32)],
        compiler_params=pltpu.CompilerParams(dimension_semantics=("arbitrary",), vmem_limit_bytes=VMEM_LIMIT),
    )(h, target, w)


def _adamw_vals(w, g, m, v):
    m = ADAM_B1 * m + (1.0 - ADAM_B1) * g
    v = ADAM_B2 * v + (1.0 - ADAM_B2) * jnp.square(g)
    m_hat = m / (1.0 - ADAM_B1 ** ADAM_STEP)
    v_hat = v / (1.0 - ADAM_B2 ** ADAM_STEP)
    delta = -ADAM_LR * (m_hat / (jnp.sqrt(v_hat) + ADAM_EPS) + ADAM_WD * w)
    return delta, m, v


def _sum_adamw(name, parts, w, m, v):
    r, c = w.shape
    n_parts = parts.shape[0]
    budget = 6 * 1024 * 1024
    tr, tc = r, c
    for cand in (512, 256, 128, 64, 32, 16, 8):
        if r % cand == 0 and n_parts * cand * c * 4 <= budget:
            tr = cand
            break
    if n_parts * tr * c * 4 > budget:
        tc = max(t for t in range(LANES, c + 1, LANES) if c % t == 0 and n_parts * r * t * 4 <= budget)

    def body(p_ref, w_ref, m_ref, v_ref, g_ref, d_ref, m2_ref, v2_ref):
        g = p_ref[0].astype(F32)
        for s in range(1, n_parts):
            g = g + p_ref[s].astype(F32)
        g_ref[...] = g
        d_ref[...], m2_ref[...], v2_ref[...] = _adamw_vals(w_ref[...], g, m_ref[...], v_ref[...])

    blk = pl.BlockSpec((tr, tc), lambda i: (i, 0)) if tc == c else pl.BlockSpec((tr, tc), lambda i: (0, i))
    parts_blk = (pl.BlockSpec((n_parts, tr, tc), lambda i: (0, i, 0)) if tc == c
                 else pl.BlockSpec((n_parts, tr, tc), lambda i: (0, 0, i)))
    return pl.pallas_call(
        body, grid=(r // tr if tc == c else c // tc,), name=name,
        in_specs=[parts_blk, blk, blk, blk],
        out_specs=[blk] * 4, out_shape=[SDS((r, c), F32)] * 4,
        compiler_params=pltpu.CompilerParams(dimension_semantics=("arbitrary",), vmem_limit_bytes=VMEM_LIMIT),
    )(parts, w, m, v)


def _peers():
    x, y, c = lax.axis_index("x"), lax.axis_index("y"), lax.axis_index("c")
    peers = []
    for k in range(1, N_DEV):
        px = 1 - x if k & 4 else x
        py = 1 - y if k & 2 else y
        pc = 1 - c if k & 1 else c
        peers.append(((px, py, pc), 4 * px + 2 * py + pc))
    return 4 * x + 2 * y + c, peers


def _slot(ref, idx, cols):
    if cols is None:
        return ref.at[idx]
    return ref.at[:, pl.ds(pl.multiple_of(idx * cols, LANES), cols)]


def _gather_two_level(name, srcs, dsts):
    n = len(srcs)
    dst_cols = [c for _, _, c in dsts]

    def body(*refs):
        src_refs, out_refs = refs[:n], refs[n:2 * n]
        send_sems, recv_sems, local_sems = refs[2 * n:]
        x, y, c = lax.axis_index("x"), lax.axis_index("y"), lax.axis_index("c")
        index = lambda px, py, pc: 4 * px + 2 * py + pc
        me, sibling = index(x, y, c), (x, y, 1 - c)
        chips = [(x, 1 - y), (1 - x, y), (1 - x, 1 - y)]

        def copy(a, k, src, block, to):
            return pltpu.make_async_remote_copy(
                src_ref=src, dst_ref=_slot(out_refs[a], block, dst_cols[a]),
                send_sem=send_sems.at[a, k], recv_sem=recv_sems.at[a, k],
                device_id=to, device_id_type=pl.DeviceIdType.MESH)

        local, first, passed = [], [], []
        for a in range(n):
            cp = pltpu.make_async_copy(src_refs[a], _slot(out_refs[a], me, dst_cols[a]), local_sems.at[a])
            cp.start()
            local.append(cp)
            first.append(copy(a, 0, src_refs[a], me, sibling))
            first += [copy(a, 1 + j, src_refs[a], me, (*chip, c)) for j, chip in enumerate(chips)]
        for cp in first:
            cp.start()
        for a in range(n):
            for j, chip in enumerate(chips):
                block = index(*chip, c)
                arrived = _slot(out_refs[a], block, dst_cols[a])
                copy(a, 1 + j, arrived, block, (*chip, c)).wait_recv()
                passed.append(copy(a, 4 + j, arrived, block, sibling))
                passed[-1].start()
        for a in range(n):
            copy(a, 0, src_refs[a], index(x, y, 1 - c), sibling).wait_recv()
            for j, chip in enumerate(chips):
                block = index(*chip, 1 - c)
                copy(a, 4 + j, src_refs[a], block, sibling).wait_recv()
        for cp in first + passed:
            cp.wait_send()
        for cp in local:
            cp.wait()

    any_spec = pl.BlockSpec(memory_space=pl.ANY)
    return pl.pallas_call(
        body, name=name,
        in_specs=[any_spec] * n, out_specs=[any_spec] * n,
        out_shape=[SDS(shape, dt) for shape, dt, _ in dsts],
        scratch_shapes=_exchange_sems(n),
    )(*[a for a, _ in srcs])


_HBM = pl.BlockSpec(memory_space=pltpu.HBM)
_SEM = pl.BlockSpec(memory_space=pltpu.SEMAPHORE)
_EFFECT = pltpu.SideEffectType.DATAFLOW_SIDE_EFFECTING


def _split_copies(src_cols, dst_cols, gather, chips, src_refs, land_refs, send_sems, recv_sems, landings):
    me, peers = _peers()
    if chips:
        me, peers = me // 2, [(pos, idx // 2) for k, (pos, idx) in enumerate(peers) if (k + 1) in (2, 4, 6)]
    n, width = len(src_cols), len(peers)
    remote, local = [], []
    for a, (s_cols, d_cols) in enumerate(zip(src_cols, dst_cols)):
        mine = src_refs[a] if gather else _slot(src_refs[a], me, s_cols)
        local.append(pltpu.make_async_copy(mine, _slot(land_refs[a], me, d_cols), send_sems.at[n * width + a]))
        for k, (pos, idx) in enumerate(peers):
            blk = src_refs[a] if gather else _slot(src_refs[a], idx, s_cols)
            remote.append(pltpu.make_async_remote_copy(
                src_ref=blk, dst_ref=_slot(land_refs[a], idx if landings else me, d_cols),
                send_sem=send_sems.at[a * width + k], recv_sem=recv_sems.at[a * width + k],
                device_id=pos, device_id_type=pl.DeviceIdType.MESH))
    return remote, local


def _exchange_start(name, srcs, dsts, gather, after, chips=False):
    n = len(srcs)
    src_cols, dst_cols = [c for _, c in srcs], [c for _, _, c in dsts]
    width = 3 if chips else N_DEV - 1

    def body(*refs):
        src_refs, land_refs = refs[:n], refs[n:2 * n]
        send_sems, recv_sems = refs[2 * n + 1:2 * n + 3]
        token = refs[-1]
        remote, local = _split_copies(src_cols, dst_cols, gather, chips, src_refs, land_refs, send_sems, recv_sems,
                                      False)
        for cp in remote + local:
            cp.start()
        token[...] = jnp.zeros_like(token)

    hbm = lambda a: pltpu.with_memory_space_constraint(a, pltpu.HBM)
    lands = [hbm(lax.empty(shape, dt)) for shape, dt, _ in dsts]
    res = pl.pallas_call(
        body, name=name,
        out_shape=(pltpu.SemaphoreType.DMA((n * (width + 1),)), pltpu.SemaphoreType.DMA((n * width,)),
                   *[pltpu.HBM(a.shape, a.dtype) for a, _ in srcs], *[pltpu.HBM(a.shape, a.dtype) for a in lands],
                   SDS((8, LANES), F32)),
        in_specs=[_HBM] * (2 * n) + [pl.BlockSpec(memory_space=pl.ANY)],
        out_specs=(_SEM, _SEM, *[_HBM] * (2 * n), pl.BlockSpec(memory_space=pltpu.VMEM)),
        input_output_aliases={i: 2 + i for i in range(2 * n)},
        compiler_params=pltpu.CompilerParams(has_side_effects=_EFFECT),
    )(*[hbm(a) for a, _ in srcs], *lands, after)
    handle = (res[0], res[1], res[2:2 + n], res[2 + n:2 + 2 * n], src_cols, dst_cols, gather, chips)
    return handle, res[-1]


def _exchange_wait(name, handle, after):
    send_sems, recv_sems, src_thru, land_thru, src_cols, dst_cols, gather, chips = handle
    n = len(src_thru)

    def body(*refs):
        src_refs, land_refs = refs[:n], refs[n:2 * n]
        s_sems, r_sems = refs[2 * n:2 * n + 2]
        remote, local = _split_copies(src_cols, dst_cols, gather, chips, src_refs, land_refs, s_sems, r_sems, True)
        for cp in remote:
            cp.wait_send()
            cp.wait_recv()
        for cp in local:
            cp.wait()

    res = pl.pallas_call(
        body, name=name,
        out_shape=tuple(pltpu.HBM(a.shape, a.dtype) for a in (*src_thru, *land_thru)),
        in_specs=[_HBM] * (2 * n) + [_SEM, _SEM, pl.BlockSpec(memory_space=pl.ANY)],
        out_specs=tuple([_HBM] * (2 * n)),
        input_output_aliases={i: i for i in range(2 * n)},
        compiler_params=pltpu.CompilerParams(has_side_effects=_EFFECT),
    )(*src_thru, *land_thru, send_sems, recv_sems, after)
    return res[n:]


def _pair_swap(name, arrs):
    n = len(arrs)

    def body(*refs):
        src_refs, out_refs = refs[:n], refs[n:2 * n]
        send_sems, recv_sems = refs[2 * n:]
        x, y, c = lax.axis_index("x"), lax.axis_index("y"), lax.axis_index("c")
        copies = [pltpu.make_async_remote_copy(
            src_ref=src_refs[a].at[:, 1 - c], dst_ref=out_refs[a], send_sem=send_sems.at[a], recv_sem=recv_sems.at[a],
            device_id=(x, y, 1 - c), device_id_type=pl.DeviceIdType.MESH) for a in range(n)]
        for cp in copies:
            cp.start()
        for cp in copies:
            cp.wait()

    any_spec = pl.BlockSpec(memory_space=pl.ANY)
    return pl.pallas_call(
        body, name=name,
        in_specs=[any_spec] * n, out_specs=[any_spec] * n,
        out_shape=[SDS((a.shape[0],) + a.shape[2:], a.dtype) for a in arrs],
        scratch_shapes=[pltpu.SemaphoreType.DMA((n,)), pltpu.SemaphoreType.DMA((n,))],
    )(*arrs)


def _pair_add(name, mine, theirs):
    four, _, r, c = mine.shape
    tr = r
    for cand in (512, 256, 128, 64, 32, 16, 8):
        if r % cand == 0:
            tr = cand
            break
    tc = max(t for t in range(LANES, c + 1, LANES) if c % t == 0 and (t == LANES or 2 * tr * t * 4 <= 4 * 1024 * 1024))

    def body(m_ref, t_ref, o_ref):
        core = lax.axis_index("c")
        both = m_ref[...].astype(F32)
        own = jnp.where(core == 0, both[0], both[1])
        o_ref[...] = (own + t_ref[...].astype(F32)).astype(o_ref.dtype)

    return pl.pallas_call(
        body, grid=(four, r // tr, c // tc), name=name,
        in_specs=[pl.BlockSpec((None, 2, tr, tc), lambda i, j, k: (i, 0, j, k)),
                  pl.BlockSpec((None, tr, tc), lambda i, j, k: (i, j, k))],
        out_specs=pl.BlockSpec((None, tr, tc), lambda i, j, k: (i, j, k)),
        out_shape=SDS(theirs.shape, theirs.dtype),
        compiler_params=pltpu.CompilerParams(dimension_semantics=("arbitrary",) * 3, vmem_limit_bytes=VMEM_LIMIT),
    )(mine, theirs)


def _my_index():
    return 4 * lax.axis_index("x") + 2 * lax.axis_index("y") + lax.axis_index("c")


def _two_level_copies(stage, dst_cols, src_refs, land_refs, send_sems, recv_sems, landings):
    x, y, c = lax.axis_index("x"), lax.axis_index("y"), lax.axis_index("c")

    def pos(k):
        return (1 - x if k & 4 else x, 1 - y if k & 2 else y, 1 - c if k & 1 else c)

    def idx(k):
        px, py, pc = pos(k)
        return 4 * px + 2 * py + pc

    out = []
    for a, cols in enumerate(dst_cols):
        if stage == 1:
            for i, k in enumerate((1, 2, 4, 6)):
                out.append(pltpu.make_async_remote_copy(
                    src_ref=src_refs[a], dst_ref=_slot(land_refs[a], idx(k) if landings else idx(0), cols),
                    send_sem=send_sems.at[4 * a + i], recv_sem=recv_sems.at[4 * a + i],
                    device_id=pos(k), device_id_type=pl.DeviceIdType.MESH))
        else:
            for i, k in enumerate((2, 4, 6)):
                out.append(pltpu.make_async_remote_copy(
                    src_ref=_slot(land_refs[a], idx(k), cols),
                    dst_ref=_slot(land_refs[a], idx(k ^ 1) if landings else idx(k), cols),
                    send_sem=send_sems.at[3 * a + i], recv_sem=recv_sems.at[3 * a + i],
                    device_id=pos(1), device_id_type=pl.DeviceIdType.MESH))
    return out


def _gather2_start(name, srcs, dsts, after):
    n = len(srcs)
    dst_cols = [c for _, _, c in dsts]

    def body(*refs):
        src_refs, land_refs = refs[:n], refs[n:2 * n]
        send_sems, recv_sems = refs[2 * n + 1:2 * n + 3]
        me = _my_index()
        for a in range(n):
            pltpu.make_async_copy(src_refs[a], _slot(land_refs[a], me, dst_cols[a]), send_sems.at[4 * n + a]).start()
        for cp in _two_level_copies(1, dst_cols, src_refs, land_refs, send_sems, recv_sems, False):
            cp.start()
        refs[-1][...] = jnp.zeros_like(refs[-1])

    hbm = lambda a: pltpu.with_memory_space_constraint(a, pltpu.HBM)
    lands = [hbm(lax.empty(shape, dt)) for shape, dt, _ in dsts]
    res = pl.pallas_call(
        body, name=name,
        out_shape=(pltpu.SemaphoreType.DMA((5 * n,)), pltpu.SemaphoreType.DMA((4 * n,)),
                   *[pltpu.HBM(a.shape, a.dtype) for a, _ in srcs], *[pltpu.HBM(a.shape, a.dtype) for a in lands],
                   SDS((8, LANES), F32)),
        in_specs=[_HBM] * (2 * n) + [pl.BlockSpec(memory_space=pl.ANY)],
        out_specs=(_SEM, _SEM, *[_HBM] * (2 * n), pl.BlockSpec(memory_space=pltpu.VMEM)),
        input_output_aliases={i: 2 + i for i in range(2 * n)},
        compiler_params=pltpu.CompilerParams(has_side_effects=_EFFECT),
    )(*[hbm(a) for a, _ in srcs], *lands, after)
    return (res[0], res[1], res[2:2 + n], res[2 + n:2 + 2 * n], dst_cols), res[-1]


def _gather2_pass(name, handle, after):
    send1, recv1, src_thru, land_thru, dst_cols = handle
    n = len(src_thru)

    def body(*refs):
        src_refs, land_refs = refs[:n], refs[n:2 * n]
        s1, r1 = refs[2 * n:2 * n + 2]
        send2, recv2 = refs[2 * n + 3:2 * n + 5]
        me = _my_index()
        for cp in _two_level_copies(1, dst_cols, src_refs, land_refs, s1, r1, True):
            cp.wait_send()
            cp.wait_recv()
        for a in range(n):
            pltpu.make_async_copy(src_refs[a], _slot(land_refs[a], me, dst_cols[a]), s1.at[4 * n + a]).wait()
        for cp in _two_level_copies(2, dst_cols, src_refs, land_refs, send2, recv2, False):
            cp.start()
        refs[-1][...] = jnp.zeros_like(refs[-1])

    res = pl.pallas_call(
        body, name=name,
        out_shape=(pltpu.SemaphoreType.DMA((3 * n,)), pltpu.SemaphoreType.DMA((3 * n,)),
                   *[pltpu.HBM(a.shape, a.dtype) for a in (*src_thru, *land_thru)], SDS((8, LANES), F32)),
        in_specs=[_HBM] * (2 * n) + [_SEM, _SEM, pl.BlockSpec(memory_space=pl.ANY)],
        out_specs=(_SEM, _SEM, *[_HBM] * (2 * n), pl.BlockSpec(memory_space=pltpu.VMEM)),
        input_output_aliases={i: 2 + i for i in range(2 * n)},
        compiler_params=pltpu.CompilerParams(has_side_effects=_EFFECT),
    )(*src_thru, *land_thru, send1, recv1, after)
    return (res[0], res[1], res[2:2 + n], res[2 + n:2 + 2 * n], dst_cols), res[-1]


def _gather2_wait(name, handle, after):
    send2, recv2, src_thru, land_thru, dst_cols = handle
    n = len(src_thru)

    def body(*refs):
        src_refs, land_refs = refs[:n], refs[n:2 * n]
        s2, r2 = refs[2 * n:2 * n + 2]
        for cp in _two_level_copies(2, dst_cols, src_refs, land_refs, s2, r2, True):
            cp.wait_send()
            cp.wait_recv()

    res = pl.pallas_call(
        body, name=name,
        out_shape=tuple(pltpu.HBM(a.shape, a.dtype) for a in (*src_thru, *land_thru)),
        in_specs=[_HBM] * (2 * n) + [_SEM, _SEM, pl.BlockSpec(memory_space=pl.ANY)],
        out_specs=tuple([_HBM] * (2 * n)),
        input_output_aliases={i: i for i in range(2 * n)},
        compiler_params=pltpu.CompilerParams(has_side_effects=_EFFECT),
    )(*src_thru, *land_thru, send2, recv2, after)
    return res[n:]


def _exchange_sems(n):
    return [pltpu.SemaphoreType.DMA((n, N_DEV - 1)), pltpu.SemaphoreType.DMA((n, N_DEV - 1)),
            pltpu.SemaphoreType.DMA((n,))]


def _rms_res_fn(h, w):
    return _rms_fn(h, w)[0], h


def _add_epilogue(acc, res):
    return (acc + res,)


def _gather_plan(shards):
    srcs, dsts = [], []
    for n, sh in shards.items():
        r, c = sh.shape
        srcs.append((sh, None))
        if SHARDED[n] and c % LANES == 0:
            dsts.append(((r, N_DEV * c), sh.dtype, c))
        else:
            dsts.append(((N_DEV, r, c), sh.dtype, None))
    return srcs, dsts, True


def _w_in_segments():
    out = []
    for j in range(N_DEV):
        lo, hi = W_IN_SHARD * j, W_IN_SHARD * (j + 1)
        for a, b in ((lo, min(hi, DN_COLS)), (max(lo, DN_COLS), hi)):
            if a < b:
                out.append((j, a - lo, b - lo, a if a < DN_COLS else a + RW_OFF - DN_COLS))
    return out


def _w_in_to_padded(shards, tc=512):
    _, _, cols = shards.shape

    def body(g_ref, o_ref):
        o_ref[...] = jnp.zeros_like(o_ref)
        for j, a, b, dst in _w_in_segments():
            o_ref[dst:dst + b - a, :] = g_ref[j, a:b, :]

    return pl.pallas_call(
        body, grid=(cols // tc,), name="w_in_to_padded",
        in_specs=[pl.BlockSpec((N_DEV, W_IN_SHARD, tc), lambda i: (0, 0, i))],
        out_specs=pl.BlockSpec((IN_PAD, tc), lambda i: (0, i)),
        out_shape=SDS((IN_PAD, cols), shards.dtype),
        compiler_params=pltpu.CompilerParams(dimension_semantics=("arbitrary",), vmem_limit_bytes=VMEM_LIMIT),
    )(shards)


def _w_in_grad_to_shards(gw, tc=512):
    _, cols = gw.shape

    def body(w_ref, o_ref):
        for j, a, b, dst in _w_in_segments():
            o_ref[j, a:b, :] = w_ref[dst:dst + b - a, :]

    return pl.pallas_call(
        body, grid=(cols // tc,), name="w_in_grad_to_shards",
        in_specs=[pl.BlockSpec((IN_PAD, tc), lambda i: (0, i))],
        out_specs=pl.BlockSpec((N_DEV, W_IN_SHARD, tc), lambda i: (0, 0, i)),
        out_shape=SDS((N_DEV, W_IN_SHARD, cols), gw.dtype),
        compiler_params=pltpu.CompilerParams(dimension_semantics=("arbitrary",), vmem_limit_bytes=VMEM_LIMIT),
    )(gw)


def _gather_finish(names, outs):
    full = {}
    for n, arr in zip(names, outs):
        if n == "w_in":
            full[n] = _w_in_to_padded(arr)
        elif arr.ndim == 2:
            full[n] = arr
        elif SHARDED[n]:
            full[n] = arr.transpose(1, 0, 2).reshape(arr.shape[1], -1)
        else:
            full[n] = arr.reshape(-1, arr.shape[2])
    return full


def _scatter_plan(grads):
    srcs, dsts = [], []
    for n, gr in grads.items():
        if gr.ndim == 3:
            srcs.append((gr, None))
            dsts.append((gr.shape, gr.dtype, None))
            continue
        rows, cols = gr.shape
        if not SHARDED[n]:
            r, c = rows // N_DEV, cols
            srcs.append((gr.reshape(N_DEV, r, c), None))
        else:
            r, c = rows, cols // N_DEV
            if c % LANES == 0:
                srcs.append((gr, c))
            else:
                srcs.append((gr.reshape(r, N_DEV, c).transpose(1, 0, 2), None))
        dsts.append(((N_DEV, r, c), gr.dtype, None))
    return srcs, dsts, False


def _local_step(x, mem, target, wt, late):
    d = D_MODEL
    g = {}
    wt = dict(wt)
    grp_a = ("w_out", "xa_wq", "xa_wk", "xa_wv", "xa_wo")
    grp_b = ("ffn_w1", "ffn_w2")
    plan = lambda names: _gather_plan({n: late[n] for n in names})[:2]
    handle_a, tok_a = _gather2_start("late_gather_a_start", *plan(grp_a), wt["w_in"])
    handle_w1, tok_b = _gather2_start("late_gather_w1_start", *plan(("ffn_w1",)), tok_a)
    handle_w2, tok_c = _gather2_start("late_gather_w2_start", *plan(("ffn_w2",)), tok_b)
    mix_w = wt["mix_norm_w"] + (tok_a[0:1, 0:1] + tok_b[0:1, 0:1] + tok_c[0:1, 0:1])
    u = _row_fwd(_rms_fn, "mix_norm", [(x, d, 0)], [mix_w], [(d, BF16)], 256)[0]
    p = _matmul("in_proj", u, wt["w_in"], "nt", [F32], tn=1536)[0]
    c = _col_fwd(_conv_fn, "dn_conv", p, 0, 24, [wt["dn_conv_w"]])
    handle_a, tok = _gather2_pass("late_gather_a_pass", handle_a, c)
    dn_pre_tiles = [(c, DN_WIDTH, 0), (c, DN_WIDTH, 1), (p, LANES, 32)]
    dn_pre_params = [wt["dn_a_log"], wt["dn_dt_bias"]]
    qh, kh, gb, bb, gcb = _row_fwd(_dn_pre_fn, "dn_pre", dn_pre_tiles, [dn_pre_params[0] + tok[0:1, :], dn_pre_params[1]],
                                   [(DN_WIDTH, F32)] * 5, CHUNK)
    dn_arrs = [(qh, 0), (kh, 0), (c, 16), (gb, 0), (bb, 0), (gcb, 0)]
    o, kept_dn = _scan_fwd(_gdn_group, "gdn_scan", dn_arrs, DN_HEADS, 1)
    dn_post_tiles = [(o, DN_WIDTH, 0), (p, DN_WIDTH, 3)]
    o_dn = _row_fwd(_dn_post_fn, "dn_post", dn_post_tiles, [wt["dn_norm_w"]], [(DN_WIDTH, BF16)], 256)[0]

    ps = _col_fwd(_lerp_fn, "rw_shift", p, RW_OFF // LANES, 26, [wt["rw_mu"]])
    rw_pre_tiles = [(ps, RW_WIDTH, 0), (ps, RW_WIDTH, 1), (ps, RW_WIDTH, 2), (ps, LANES, 24), (ps, LANES, 25)]
    rw_pre_params = [wt[n] for n in ("rw_w0", "rw_a0", "rw_k_k", "rw_k_a", "rw_w2", "rw_a2", "rw_g2")]
    r, lw, k, v, al, be, gate, gcw = _row_fwd(_rw_pre_fn, "rw_pre", rw_pre_tiles, rw_pre_params,
                                              [(RW_WIDTH, F32)] * 8, CHUNK)
    rw_arrs = [(r, 0), (lw, 0), (k, 0), (v, 0), (al, 0), (be, 0), (gcw, 0)]
    y, kept_rw = _scan_fwd(_rw_group, "rw_scan", rw_arrs, RW_WIDTH // LANES, 2)
    handle_w1, tok = _gather2_pass("late_gather_w1_pass", handle_w1, y)
    rw_post_tiles = [(t, RW_WIDTH, 0) for t in (y, r, k, v, gate)]
    rw_post_params = [wt["rw_ln_w"], wt["rw_ln_b"], wt["rw_r_k"]]
    o_rw = _row_fwd(_rw_post_fn, "rw_post", rw_post_tiles, [rw_post_params[0] + tok[0:1, 0:1]] + rw_post_params[1:],
                    [(RW_WIDTH, BF16)], 128)[0]
    o_cat = jnp.concatenate([o_dn, o_rw], axis=1)
    wt.update(_gather_finish(grp_a, _gather2_wait("late_gather_a_wait", handle_a, o_cat)))
    h1 = _matmul("out_proj", o_cat, wt["w_out"], "nn", [F32], _add_epilogue, (x,))[0]

    handle_w2, tok = _gather2_pass("late_gather_w2_pass", handle_w2, h1)
    hn = _row_fwd(_rms_fn, "xa_norm", [(h1, d, 0)], [wt["xa_norm_w"] + tok[0:1, 0:1]], [(d, BF16)], 256)[0]
    mn = _row_fwd(_rms_fn, "mem_norm", [(mem, d, 0)], [wt["mem_norm_w"]], [(d, BF16)], 256)[0]
    q = _matmul("xa_q", hn, wt["xa_wq"], "nn", [F32])[0]
    kx = _matmul("xa_k", mn, wt["xa_wk"], "nn", [F32])[0]
    vx = _matmul("xa_v", mn, wt["xa_wv"], "nn", [F32])[0]
    ao = _row_fwd(_xattn_fn, "xattn", [(q, XA_WIDTH, 0)], [kx, vx], [(XA_WIDTH, BF16)], 256)[0]
    h2 = _matmul("xa_o", ao, wt["xa_wo"], "nn", [F32], _add_epilogue, (h1,))[0]

    f = _row_fwd(_rms_fn, "ffn_norm", [(h2, d, 0)], [wt["ffn_norm_w"]], [(d, BF16)], 256)[0]
    wt.update(_gather_finish(("ffn_w1",), _gather2_wait("late_gather_w1_wait", handle_w1, f)))
    a, hid = _matmul("ffn_up", f, wt["ffn_w1"], "nn", [F32, BF16],
                     lambda acc: (acc, jnp.square(jnp.maximum(acc, 0.0))))
    wt.update(_gather_finish(("ffn_w2",), _gather2_wait("late_gather_w2_wait", handle_w2, hid)))
    h3 = _matmul("ffn_down", hid, wt["ffn_w2"], "nn", [F32], _add_epilogue, (h2,))[0]
    loss8, dh3, g["final_norm_w"] = _loss_call(h3, target, wt["final_norm_w"])

    da = _matmul("ffn_down_dx", dh3, wt["ffn_w2"], "nt", [BF16],
                 lambda acc, av: (acc * 2.0 * jnp.maximum(av, 0.0),), (a,))[0]
    g["ffn_w2"] = _matmul("ffn_down_dw", hid, dh3, "tn", [BF16])[0]
    g["ffn_w1"] = _matmul("ffn_up_dw", f, da, "tn", [BF16])[0]
    pending = {}
    plan = _scatter_plan({n: g.pop(n) for n in grp_b})
    pending[grp_b], tok = _exchange_start("late_grad_b_start", *plan, loss8)
    dh2, g["ffn_norm_w"] = _matmul_norm_bwd("ffn_up_dx", da, wt["ffn_w1"], "nt", h2, wt["ffn_norm_w"], dh3, tok)

    dao = _matmul("xa_o_dx", dh2, wt["xa_wo"], "nt", [F32])[0]
    g["xa_wo"] = _matmul("xa_o_dw", ao, dh2, "tn", [BF16])[0]
    (dq,), (dkx, dvx) = _row_bwd(_xattn_fn, "xattn_bwd", [(q, XA_WIDTH, 0)], [kx, vx], [[(dao, XA_WIDTH, 0)]], 256)
    dh1, g["xa_norm_w"] = _matmul_norm_bwd("xa_q_dx", dq, wt["xa_wq"], "nt", h1, wt["xa_norm_w"], dh2)
    g["xa_wq"] = _matmul("xa_q_dw", hn, dq, "tn", [BF16])[0]
    g["xa_wk"] = _matmul("xa_k_dw", mn, dkx, "tn", [BF16])[0]
    g["xa_wv"] = _matmul("xa_v_dw", mn, dvx, "tn", [BF16])[0]
    dmn = _matmul("xa_k_dx", dkx, wt["xa_wk"], "nt", [F32])[0]
    dmn = _matmul("xa_v_dx", dvx, wt["xa_wv"], "nt", [F32], _add_epilogue, (dmn,))[0]
    _, (g["mem_norm_w"],) = _row_bwd(_rms_fn, "mem_norm_bwd", [(mem, d, 0)], [wt["mem_norm_w"]],
                                     [[(dmn, d, 0)]], 256, want_tiles=())

    do_cat = _matmul("out_proj_dx", dh1, wt["w_out"], "nt", [F32])[0]
    g["w_out"] = _matmul("out_proj_dw", o_cat, dh1, "tn", [BF16])[0]

    plan = _scatter_plan({n: g.pop(n) for n in grp_a})
    pending[grp_a], tok = _exchange_start("late_grad_a_start", *plan, tok)
    (dy, dr1, dk1, dv1, dgate), (g["rw_ln_w"], g["rw_ln_b"], g["rw_r_k"]) = _row_bwd(
        _rw_post_fn, "rw_post_bwd", rw_post_tiles, [rw_post_params[0] + tok[0:1, 0:1]] + rw_post_params[1:],
        [[(do_cat, RW_WIDTH, 1)]], 128)
    dr2, dlw, dk2, dv2, dal, dbe, dgcw = _scan_bwd(_rw_group, "rw_scan_bwd", rw_arrs, kept_rw, dy,
                                                   RW_WIDTH // LANES)
    one = lambda t: [(t, RW_WIDTH, 0)]
    two = lambda s, t: [(s, RW_WIDTH, 0), (t, RW_WIDTH, 0)]
    d_ps, rw_pre_grads = _row_bwd(
        _rw_pre_fn, "rw_pre_bwd", rw_pre_tiles, rw_pre_params,
        [two(dr1, dr2), one(dlw), two(dk1, dk2), two(dv1, dv2), one(dal), one(dbe), one(dgate), one(dgcw)],
        CHUNK)
    for n, val in zip(("rw_w0", "rw_a0", "rw_k_k", "rw_k_a", "rw_w2", "rw_a2", "rw_g2"), rw_pre_grads):
        g[n] = val
    dp_rw, (g["rw_mu"],) = _col_bwd(_lerp_fn, "rw_shift_bwd", p, RW_OFF // LANES, 26, [wt["rw_mu"]], list(d_ps))

    (do, dz), (g["dn_norm_w"],) = _row_bwd(_dn_post_fn, "dn_post_bwd", dn_post_tiles, [wt["dn_norm_w"]],
                                           [[(do_cat, DN_WIDTH, 0)]], 256)
    dqh, dkh, dv_dn, dgb, dbb, dgcb = _scan_bwd(_gdn_group, "gdn_scan_bwd", dn_arrs, kept_dn, do, DN_HEADS)
    one = lambda t: [(t, DN_WIDTH, 0)]
    (dcq, dck, dgates), (g["dn_a_log"], g["dn_dt_bias"]) = _row_bwd(
        _dn_pre_fn, "dn_pre_bwd", dn_pre_tiles, dn_pre_params,
        [one(dqh), one(dkh), one(dgb), one(dbb), one(dgcb)], CHUNK)
    dp_qkv, (g["dn_conv_w"],) = _col_bwd(_conv_fn, "dn_conv_bwd", p, 0, 24, [wt["dn_conv_w"]], [dcq, dck, dv_dn])
    dp = jnp.concatenate([t.astype(BF16) for t in (dp_qkv, dz, dgates, dp_rw, jnp.zeros((x.shape[0], LANES), F32))],
                         axis=1)
    g["w_in"] = _matmul("in_proj_dw", dp, u, "tn", [BF16], tm=1536)[0]
    early = _logical_grads(g)
    blocks = []
    for src, cols in _scatter_plan({n: early.pop(n) for n in EARLY})[0]:
        if cols is not None:
            src = src.reshape(src.shape[0], N_DEV, cols).transpose(1, 0, 2)
        blocks.append(src.reshape((4, 2) + src.shape[1:]))
    sums = [_pair_add("early_grad_pair_add_%d" % i, mine, theirs)
            for i, (mine, theirs) in enumerate(zip(blocks, _pair_swap("early_grad_pair_swap", blocks)))]
    pending[EARLY], tok = _exchange_start("early_grad_start", [(t, None) for t in sums],
                                          [(t.shape, t.dtype, None) for t in sums], False, tok, chips=True)
    dx, early["mix_norm_w"] = _matmul_norm_bwd("in_proj_dx", dp, wt["w_in"], "nn", x, wt["mix_norm_w"], dh1, tok)
    return loss8, dx, early, pending, tok


WEIGHTS = ["mix_norm_w", "w_in", "dn_conv_w", "dn_a_log", "dn_dt_bias", "dn_norm_w", "rw_mu", "rw_w0", "rw_w2",
           "rw_a0", "rw_a2", "rw_g2", "rw_k_k", "rw_k_a", "rw_r_k", "rw_ln_w", "rw_ln_b", "w_out", "xa_norm_w",
           "mem_norm_w", "xa_wq", "xa_wk", "xa_wv", "xa_wo", "ffn_norm_w", "ffn_w1", "ffn_w2", "final_norm_w"]
SHARDED = {"w_in": False, "w_out": False, "xa_wq": False, "xa_wk": False, "xa_wv": False, "xa_wo": True,
           "ffn_w1": True, "ffn_w2": False, "dn_conv_w": True, "rw_w2": True, "rw_a2": True, "rw_g2": True}
BF16_PAYLOAD = ("w_in", "w_out", "xa_wq", "xa_wk", "xa_wv", "xa_wo", "ffn_w1", "ffn_w2")
REPLICATED = [n for n in WEIGHTS if n not in SHARDED]
EARLY = ("w_in", "dn_conv_w", "rw_w2", "rw_a2", "rw_g2")
RW_IN_COLS = IN_COLS - DN_COLS
W_IN_SHARD = IN_COLS // N_DEV


def _layout_weights(fw):
    wt = dict(fw)
    wt["dn_conv_w"] = jnp.pad(fw["dn_conv_w"], ((0, 4), (0, 0)))
    wt["dn_a_log"] = jnp.pad(fw["dn_a_log"], ((0, 0), (0, LANES - DN_HEADS)))
    wt["dn_dt_bias"] = jnp.pad(fw["dn_dt_bias"], ((0, 0), (0, LANES - DN_HEADS)))
    wt["rw_w2"] = jnp.pad(fw["rw_w2"], ((0, 64), (0, 0)))
    wt["rw_a2"] = jnp.pad(fw["rw_a2"], ((64, 0), (0, 0)))
    return wt


def _logical_grads(g):
    out = dict(g)
    out["w_in"] = _w_in_grad_to_shards(g["w_in"])
    out["dn_conv_w"] = g["dn_conv_w"][:4]
    out["dn_a_log"] = g["dn_a_log"][:, :DN_HEADS]
    out["dn_dt_bias"] = g["dn_dt_bias"][:, :DN_HEADS]
    out["rw_w2"] = g["rw_w2"][:64]
    out["rw_a2"] = g["rw_a2"][64:]
    return out


def _pack(vals):
    parts = []
    for v in vals:
        flat = v.reshape(-1)
        parts.append(jnp.pad(flat, (0, -flat.shape[0] % LANES)))
    flat = jnp.concatenate(parts)
    flat = jnp.pad(flat, (0, -flat.shape[0] % (8 * LANES)))
    return flat.reshape(-1, LANES)


def _unpack(packed, shapes):
    flat = packed.reshape(-1)
    out, at = [], 0
    for shp in shapes:
        size = math.prod(shp)
        out.append(flat[at:at + size].reshape(shp))
        at += size + (-size % LANES)
    return out


def kernel(x, mem, mix_norm_w, w_in, dn_conv_w, dn_a_log, dn_dt_bias, dn_norm_w, rw_mu, rw_w0, rw_w2, rw_a0, rw_a2, rw_g2, rw_k_k, rw_k_a, rw_r_k, rw_ln_w, rw_ln_b, w_out, xa_norm_w, mem_norm_w, xa_wq, xa_wk, xa_wv, xa_wo, ffn_norm_w, ffn_w1, ffn_w2, final_norm_w, loss_target, m_mix_norm_w, m_w_in, m_dn_conv_w, m_dn_a_log, m_dn_dt_bias, m_dn_norm_w, m_rw_mu, m_rw_w0, m_rw_w2, m_rw_a0, m_rw_a2, m_rw_g2, m_rw_k_k, m_rw_k_a, m_rw_r_k, m_rw_ln_w, m_rw_ln_b, m_w_out, m_xa_norm_w, m_mem_norm_w, m_xa_wq, m_xa_wk, m_xa_wv, m_xa_wo, m_ffn_norm_w, m_ffn_w1, m_ffn_w2, m_final_norm_w, v_mix_norm_w, v_w_in, v_dn_conv_w, v_dn_a_log, v_dn_dt_bias, v_dn_norm_w, v_rw_mu, v_rw_w0, v_rw_w2, v_rw_a0, v_rw_a2, v_rw_g2, v_rw_k_k, v_rw_k_a, v_rw_r_k, v_rw_ln_w, v_rw_ln_b, v_w_out, v_xa_norm_w, v_mem_norm_w, v_xa_wq, v_xa_wk, v_xa_wv, v_xa_wo, v_ffn_norm_w, v_ffn_w1, v_ffn_w2, v_final_norm_w):
    given = dict(locals())
    w = {n: given[n] for n in WEIGHTS}
    m = {n: given["m_" + n] for n in WEIGHTS}
    v = {n: given["v_" + n] for n in WEIGHTS}

    local = {n: (lambda t: t[0].T) if n == "w_in" else (lambda t: t[0]) for n in SHARDED}
    shards = {n: (local[n](w[n]).astype(BF16) if n in BF16_PAYLOAD else local[n](w[n])) for n in SHARDED}
    srcs, dsts, _ = _gather_plan({n: shards[n] for n in EARLY})
    full = _gather_finish(EARLY, _gather_two_level("early_all_gather", srcs, dsts))
    for n in REPLICATED:
        full[n] = w[n].reshape(1, -1)

    loss8, dx, g, pending, after = _local_step(x[0], mem[0], loss_target[0], _layout_weights(full),
                                               {n: shards[n] for n in SHARDED if n not in EARLY})

    packed = _pack([g[n] for n in REPLICATED] + [loss8[:1, :1]])
    small, _ = _exchange_start("small_gather_start", [(packed, None)], [((N_DEV,) + packed.shape, F32, None)], True,
                               after)
    grad, delta, new_m, new_v = {}, {}, {}, {}
    done = [dx]

    def tie():
        return jnp.broadcast_to(sum(t[:1, :1] for t in done), (8, LANES))

    for names in sorted(pending, key=lambda names: names == EARLY):
        handle = pending[names]
        for n, parts in zip(names, _exchange_wait("grad_wait_" + names[0], handle, tie())):
            res = _sum_adamw("adamw_" + n, parts, local[n](w[n]), local[n](m[n]), local[n](v[n]))
            grad[n], delta[n], new_m[n], new_v[n] = [(t.T if n == "w_in" else t)[None] for t in res]
            done.append(res[1])

    (parts,) = _exchange_wait("small_gather_wait", small, tie())
    blank = [jnp.zeros((1, 1), F32)]
    res = _sum_adamw("adamw_small", parts, _pack([w[n] for n in REPLICATED] + blank),
                     _pack([m[n] for n in REPLICATED] + blank), _pack([v[n] for n in REPLICATED] + blank))
    shapes = [w[n].shape for n in REPLICATED] + [()]
    loss = _unpack(res[0], shapes)[-1]
    for store, packed_out in zip((grad, delta, new_m, new_v), res):
        for n, val in zip(REPLICATED, _unpack(packed_out, shapes)):
            store[n] = val

    return (loss, dx[None], *[grad[n] for n in WEIGHTS], *[delta[n] for n in WEIGHTS],
            *[new_m[n] for n in WEIGHTS], *[new_v[n] for n in WEIGHTS])
```

```python
import functools
import math

import jax
import jax.numpy as jnp
from jax import lax
from jax.experimental import pallas as pl
from jax.experimental.pallas import tpu as pltpu

F32 = jnp.float32
BF16 = jnp.bfloat16
SDS = jax.ShapeDtypeStruct

N_DEV = 8
D_MODEL = 2048
LANES = 128
CHUNK = 128
DN_HEADS = 8
DN_WIDTH = 1024
RW_WIDTH = 1024
RW_HEAD = 64
XA_HEADS = 4
XA_WIDTH = 512
FFN_HIDDEN = 8192
IN_COLS = 7440
DN_COLS = 4112
IN_PAD = 7680
RW_OFF = 4224
RMS_EPS = 1e-6
RW_GN_EPS = 64e-5
VMEM_LIMIT = 56 * 1024 * 1024

ADAM_LR = 0.001
ADAM_B1 = 0.9
ADAM_B2 = 0.999
ADAM_EPS = 1e-08
ADAM_WD = 0.01
ADAM_STEP = 10

_DIMS = {"nn": (((1,), (0,)), ((), ())), "nt": (((1,), (1,)), ((), ())), "tn": (((0,), (0,)), ((), ()))}


def _raw_dot(a, b, mode, hi):
    if hi:
        return lax.dot_general(a, b, _DIMS[mode], precision=lax.Precision.HIGHEST, preferred_element_type=F32)
    return lax.dot_general(a.astype(BF16), b.astype(BF16), _DIMS[mode], preferred_element_type=F32)


@functools.partial(jax.custom_vjp, nondiff_argnums=(2, 3))
def mm(a, b, mode="nn", hi=False):
    return _raw_dot(a, b, mode, hi)


def _mm_fwd(a, b, mode, hi):
    return _raw_dot(a, b, mode, hi), (a, b)


def _mm_bwd(mode, hi, res, g):
    a, b = res
    if mode == "nn":
        return _raw_dot(g, b, "nt", hi), _raw_dot(a, g, "tn", hi)
    if mode == "nt":
        return _raw_dot(g, b, "nn", hi), _raw_dot(g, a, "tn", hi)
    return _raw_dot(b, g, "nt", hi), _raw_dot(a, g, "nn", hi)


mm.defvjp(_mm_fwd, _mm_bwd)


def _shift_rows_raw(x, k):
    n = x.shape[0]
    rolled = pltpu.roll(x, k % n, axis=0)
    row = lax.broadcasted_iota(jnp.int32, x.shape, 0)
    keep = row >= k if k > 0 else row < n + k
    return jnp.where(keep, rolled, 0.0)


@functools.partial(jax.custom_vjp, nondiff_argnums=(1,))
def shift_rows(x, k):
    return _shift_rows_raw(x, k)


shift_rows.defvjp(lambda x, k: (_shift_rows_raw(x, k), None), lambda k, _, g: (_shift_rows_raw(g, -k),))


@jax.custom_vjp
def _sigmoid(x):
    return 1.0 / (1.0 + jnp.exp(-x))


def _sigmoid_fwd(x):
    s = 1.0 / (1.0 + jnp.exp(-x))
    return s, s


_sigmoid.defvjp(_sigmoid_fwd, lambda s, g: (g * s * (1.0 - s),))


@jax.custom_vjp
def _softplus(x):
    return jnp.maximum(x, 0.0) + jnp.log(1.0 + jnp.exp(-jnp.abs(x)))


_softplus.defvjp(lambda x: (_softplus(x), x), lambda x, g: (g / (1.0 + jnp.exp(-x)),))


@jax.custom_vjp
def _silu(x):
    return x / (1.0 + jnp.exp(-x))


def _silu_fwd(x):
    s = 1.0 / (1.0 + jnp.exp(-x))
    return x * s, (x, s)


_silu.defvjp(_silu_fwd, lambda res, g: (g * res[1] * (1.0 + res[0] * (1.0 - res[1])),))


def _tri_masks(n):
    ii = lax.broadcasted_iota(jnp.int32, (n, n), 0)
    jj = lax.broadcasted_iota(jnp.int32, (n, n), 1)
    return ii >= jj, ii > jj, ii == jj


def _neumann_inv_raw(m):
    n = m.shape[0]
    _, _, eye = _tri_masks(n)
    eye = jnp.where(eye, 1.0, 0.0)
    p = eye + m
    mk = m
    for _ in range(int(math.log2(n)) - 1):
        mk = _raw_dot(mk, mk, "nn", False)
        p = p + _raw_dot(p, mk, "nn", False)
    m_hi, p_hi = m.astype(BF16), p.astype(BF16)
    m_lo, p_lo = m - m_hi.astype(F32), p - p_hi.astype(F32)
    mp = _raw_dot(m_hi, p_hi, "nn", False) + _raw_dot(m_hi, p_lo, "nn", False) + _raw_dot(m_lo, p_hi, "nn", False)
    return p + _raw_dot(p, eye - p + mp, "nn", False)


@jax.custom_vjp
def _neumann_inv(m):
    return _neumann_inv_raw(m)


def _neumann_inv_fwd(m):
    p = _neumann_inv_raw(m)
    return p, p


def _neumann_inv_bwd(p, g):
    return (_raw_dot(_raw_dot(p, g, "tn", False), p, "nt", False),)


_neumann_inv.defvjp(_neumann_inv_fwd, _neumann_inv_bwd)


@jax.custom_vjp
def _saved_inv(m, p):
    return p


_saved_inv.defvjp(lambda m, p: (p, p), lambda p, g: (_neumann_inv_bwd(p, g)[0], jnp.zeros_like(p)))


def _inverse(m, saved):
    return _neumann_inv(m) if saved is None else _saved_inv(m, saved)


def _split3(x):
    hi = x.astype(BF16)
    rest = x - hi.astype(F32)
    mid = rest.astype(BF16)
    return hi, mid, (rest - mid.astype(F32)).astype(BF16)


def _select_dot_raw(sel, x, mode, x_first):
    s = sel.astype(BF16)
    parts = [lax.dot_general(p, s, _DIMS[mode], preferred_element_type=F32) if x_first
             else lax.dot_general(s, p, _DIMS[mode], preferred_element_type=F32) for p in _split3(x)]
    return parts[0] + parts[1] + parts[2]


@functools.partial(jax.custom_vjp, nondiff_argnums=(2,))
def _select_rows(sel, x, transposed=False):
    return _select_dot_raw(sel, x, "tn" if transposed else "nn", False)


_select_rows.defvjp(lambda sel, x, transposed: (_select_rows(sel, x, transposed), sel),
                    lambda transposed, sel, g: (jnp.zeros_like(sel), _select_rows(sel, g, not transposed)))


@functools.partial(jax.custom_vjp, nondiff_argnums=(2,))
def _select_cols(x, sel, transposed=False):
    return _select_dot_raw(sel, x, "nt" if transposed else "nn", True)


_select_cols.defvjp(lambda x, sel, transposed: (_select_cols(x, sel, transposed), sel),
                    lambda transposed, sel, g: (_select_cols(g, sel, not transposed), jnp.zeros_like(sel)))


def _cumsum_rows(x):
    causal, _, _ = _tri_masks(x.shape[0])
    return _select_rows(jnp.where(causal, 1.0, 0.0), x)


def _gdn_group(s0, q, k, v, gb, bb, gc, *saved):
    diff = jnp.stack([gc[j] - gc[j].T for j in range(gc.shape[0])])
    return jax.vmap(_gdn_chunk)(s0, q, k, v, gb, bb, gc, diff, *saved)


def _rw_group(*args):
    return jax.vmap(_rw_chunk)(*args)


def _gdn_chunk(s0, q, k, v, gb, bb, gc, diff, saved=None):
    c = q.shape[0]
    causal, strict, _ = _tri_masks(c)
    decay = jnp.exp(jnp.where(causal, diff, -jnp.inf))
    kb = k * bb
    a = jnp.where(strict, mm(kb, k, "nt") * decay, 0.0)
    p = _inverse(-a, saved)
    uw = mm(p, jnp.concatenate([v * bb, kb * jnp.exp(gc)], axis=1))
    u, w = uw[:, :LANES], uw[:, LANES:]
    attn = mm(q, k, "nt") * decay
    v_new = u - mm(w, s0)
    o = mm(q * jnp.exp(gc), s0) + mm(attn, v_new)
    g_last = jnp.sum(gb, axis=0, keepdims=True)
    s1 = s0 * jnp.exp(g_last) + mm(k * jnp.exp(g_last - gc), v_new, "tn")
    return o, s1, p


def _rw_chunk(s0, r, lw, k, v, al, be, gc, saved0=None, saved1=None):
    c = r.shape[0]
    causal, strict, _ = _tri_masks(c)
    gp = gc - lw
    row = lax.broadcasted_iota(jnp.int32, lw.shape, 0)
    lane = lax.broadcasted_iota(jnp.int32, lw.shape, 1)
    g_mid = jnp.sum(jnp.where(row < c // 2, lw, 0.0), axis=0, keepdims=True)
    g_last = jnp.sum(lw, axis=0, keepdims=True)
    e_n = jnp.exp(g_mid - gc)
    rg = r * jnp.exp(gc - g_mid)
    bg = be * jnp.exp(gp - g_mid)
    an = al * e_n
    kn = k * e_n
    bt = mm(be * jnp.exp(gp), s0, "nt")
    rt = mm(r * jnp.exp(gc), s0, "nt")
    us, ys, ps = [], [], []
    ank = jnp.concatenate([an, kn], axis=0)
    for h, saved in enumerate((saved0, saved1)):
        mine = (lane >= RW_HEAD) if h else (lane < RW_HEAD)
        from_b = mm(jnp.where(mine, bg, 0.0), ank, "nt")
        from_r = mm(jnp.where(mine, rg, 0.0), ank, "nt")
        a_ab = jnp.where(strict, from_b[:, :c], 0.0)
        a_kb = jnp.where(strict, from_b[:, c:], 0.0)
        a_ra = jnp.where(causal, from_r[:, :c], 0.0)
        a_rk = jnp.where(causal, from_r[:, c:], 0.0)
        p = _inverse(a_ab, saved)
        ps.append(p)
        u_h = mm(p, bt + mm(a_kb, v))
        us.append(u_h)
        ys.append(rt + mm(a_ra, u_h) + mm(a_rk, v))
    lo = lane < RW_HEAD
    u = jnp.where(lo, us[0], us[1])
    y = jnp.where(lo, ys[0], ys[1])
    tail = jnp.exp(g_last - gc)
    s1 = s0 * jnp.exp(g_last) + mm(u, al * tail, "tn") + mm(v, k * tail, "tn")
    vi = lax.broadcasted_iota(jnp.int32, s0.shape, 0)
    ki = lax.broadcasted_iota(jnp.int32, s0.shape, 1)
    s1 = jnp.where((vi < RW_HEAD) == (ki < RW_HEAD), s1, 0.0)
    return y, s1, ps[0], ps[1]


SCAN_HB = 8


def _scan_specs(arrs, n_chunks, reverse):
    def spec(off):
        assert off % SCAN_HB == 0
        if reverse:
            return pl.BlockSpec((CHUNK, SCAN_HB * LANES), lambda h, n: (n_chunks - 1 - n, off // SCAN_HB + h))
        return pl.BlockSpec((CHUNK, SCAN_HB * LANES), lambda h, n: (n, off // SCAN_HB + h))
    return [spec(off) for _, off in arrs]


def _split_heads(x):
    return jnp.stack([x[:, LANES * j:LANES * (j + 1)] for j in range(SCAN_HB)], axis=0)


def _merge_heads(x):
    return jnp.concatenate([x[j] for j in range(SCAN_HB)], axis=1)


def _scan_fwd(group_fn, name, arrs, heads, n_kept):
    s = arrs[0][0].shape[0]
    n_chunks = s // CHUNK
    n_in = len(arrs)

    def body(*refs):
        y_ref, st_ref = refs[n_in:n_in + 2]
        kept_refs, s_scr = refs[n_in + 2:-1], refs[-1]

        @pl.when(pl.program_id(1) == 0)
        def _():
            s_scr[...] = jnp.zeros_like(s_scr)

        s0 = s_scr[...]
        st_ref[...] = s0
        y, s1, *kept = group_fn(s0, *[_split_heads(r[...]) for r in refs[:n_in]])
        y_ref[...] = _merge_heads(y)
        s_scr[...] = s1
        for ref, val in zip(kept_refs, kept):
            ref[...] = val

    per_chunk = pl.BlockSpec((SCAN_HB, None, LANES, LANES), lambda h, n: (h, n, 0, 0))
    res = pl.pallas_call(
        body, grid=(heads // SCAN_HB, n_chunks), name=name,
        in_specs=_scan_specs(arrs, n_chunks, False),
        out_specs=[pl.BlockSpec((CHUNK, SCAN_HB * LANES), lambda h, n: (n, h))] + [per_chunk] * (1 + n_kept),
        out_shape=[SDS((s, heads * LANES), F32)] + [SDS((heads, n_chunks, LANES, LANES), F32)] * (1 + n_kept),
        scratch_shapes=[pltpu.VMEM((SCAN_HB, LANES, LANES), F32)],
        compiler_params=pltpu.CompilerParams(dimension_semantics=("arbitrary", "arbitrary")),
    )(*[a for a, _ in arrs])
    return res[0], res[1:]


def _scan_bwd(group_fn, name, arrs, kept, dy, heads):
    s = arrs[0][0].shape[0]
    n_chunks = s // CHUNK
    n_in, n_kept = len(arrs), len(kept)

    def body(*refs):
        kept_vals = [r[...] for r in refs[n_in:n_in + n_kept]]
        dy_ref = refs[n_in + n_kept]
        d_refs = refs[n_in + n_kept + 1:2 * n_in + n_kept + 1]
        ds_scr = refs[-1]

        @pl.when(pl.program_id(1) == 0)
        def _():
            ds_scr[...] = jnp.zeros_like(ds_scr)

        def fn(s0, *ins):
            return group_fn(s0, *ins, *kept_vals[1:])[:2]

        _, vjp = jax.vjp(fn, kept_vals[0], *[_split_heads(r[...]) for r in refs[:n_in]])
        grads = vjp((_split_heads(dy_ref[...]), ds_scr[...]))
        ds_scr[...] = grads[0]
        for ref, g in zip(d_refs, grads[1:]):
            ref[...] = _merge_heads(g)

    rev = pl.BlockSpec((CHUNK, SCAN_HB * LANES), lambda h, n: (n_chunks - 1 - n, h))
    per_chunk = pl.BlockSpec((SCAN_HB, None, LANES, LANES), lambda h, n: (h, n_chunks - 1 - n, 0, 0))
    return pl.pallas_call(
        body, grid=(heads // SCAN_HB, n_chunks), name=name,
        in_specs=_scan_specs(arrs, n_chunks, True) + [per_chunk] * n_kept + [rev],
        out_specs=[rev] * n_in,
        out_shape=[SDS((s, heads * LANES), F32)] * n_in,
        scratch_shapes=[pltpu.VMEM((SCAN_HB, LANES, LANES), F32)],
        compiler_params=pltpu.CompilerParams(dimension_semantics=("arbitrary", "arbitrary")),
    )(*[a for a, _ in arrs], *kept, dy)


def _col_spec(tr, width, cb):
    return pl.BlockSpec((tr, width), lambda i: (i, cb))


def _whole(p):
    return pl.BlockSpec(p.shape, lambda i: (0,) * p.ndim)


def _row_fwd(fn, name, tiles, params, outs, tr):
    rows = tiles[0][0].shape[0]
    nt, npar = len(tiles), len(params)

    def body(*refs):
        vals = [r[...].astype(F32) for r in refs[:nt + npar]]
        for ref, o in zip(refs[nt + npar:], fn(*vals)):
            ref[...] = o.astype(ref.dtype)

    return pl.pallas_call(
        body, grid=(rows // tr,), name=name,
        in_specs=[_col_spec(tr, w, cb) for _, w, cb in tiles] + [_whole(p) for p in params],
        out_specs=[_col_spec(tr, w, 0) for w, _ in outs],
        out_shape=[SDS((rows, w), dt) for w, dt in outs],
        compiler_params=pltpu.CompilerParams(dimension_semantics=("arbitrary",), vmem_limit_bytes=VMEM_LIMIT),
    )(*[a for a, _, _ in tiles], *params)


def _row_bwd(fn, name, tiles, params, cts, tr, want_tiles=None):
    rows = tiles[0][0].shape[0]
    nt, npar = len(tiles), len(params)
    want = list(range(nt)) if want_tiles is None else list(want_tiles)
    flat_cts = [c for group in cts for c in group]
    n_ct = len(flat_cts)

    def body(*refs):
        vals = [r[...].astype(F32) for r in refs[:nt + npar]]
        ct_refs = refs[nt + npar:nt + npar + n_ct]
        out_refs = refs[nt + npar + n_ct:]
        ct_vals, at = [], 0
        for group in cts:
            total = ct_refs[at][...].astype(F32)
            for r in ct_refs[at + 1:at + len(group)]:
                total = total + r[...].astype(F32)
            ct_vals.append(total)
            at += len(group)
        _, vjp = jax.vjp(lambda *a: tuple(fn(*a)), *vals)
        grads = vjp(tuple(ct_vals))
        for ref, t in zip(out_refs[:len(want)], want):
            ref[...] = grads[t]
        first = pl.program_id(0) == 0
        for ref, g in zip(out_refs[len(want):], grads[nt:]):
            @pl.when(first)
            def _(ref=ref, g=g):
                ref[...] = g

            @pl.when(jnp.logical_not(first))
            def _(ref=ref, g=g):
                ref[...] += g

    res = pl.pallas_call(
        body, grid=(rows // tr,), name=name,
        in_specs=[_col_spec(tr, w, cb) for _, w, cb in tiles] + [_whole(p) for p in params]
        + [_col_spec(tr, w, cb) for _, w, cb in flat_cts],
        out_specs=[_col_spec(tr, tiles[t][1], 0) for t in want] + [_whole(p) for p in params],
        out_shape=[SDS((rows, tiles[t][1]), F32) for t in want] + [SDS(p.shape, F32) for p in params],
        compiler_params=pltpu.CompilerParams(dimension_semantics=("arbitrary",), vmem_limit_bytes=VMEM_LIMIT),
    )(*[a for a, _, _ in tiles], *params, *[a for a, _, _ in flat_cts])
    return res[:len(want)], res[len(want):]


def _col_fwd(fn, name, x, first_block, n_blocks, params):
    rows = x.shape[0]

    def body(*refs):
        refs[-1][...] = fn(*[r[...] for r in refs[:-1]])

    return pl.pallas_call(
        body, grid=(n_blocks,), name=name,
        in_specs=[pl.BlockSpec((rows, LANES), lambda j: (0, first_block + j))]
        + [pl.BlockSpec((p.shape[0], LANES), lambda j: (0, j)) for p in params],
        out_specs=pl.BlockSpec((rows, LANES), lambda j: (0, j)),
        out_shape=SDS((rows, n_blocks * LANES), F32),
        compiler_params=pltpu.CompilerParams(dimension_semantics=("arbitrary",), vmem_limit_bytes=VMEM_LIMIT),
    )(x, *params)


def _col_bwd(fn, name, x, first_block, n_blocks, params, dys):
    rows = x.shape[0]
    npar, nd = len(params), len(dys)
    starts = [sum(t.shape[1] for t in dys[:i]) // LANES for i in range(nd + 1)]

    def body(*refs):
        vals = [r[...] for r in refs[:1 + npar]]
        j = pl.program_id(0)
        dy = refs[1 + npar][...]
        for i in range(1, nd):
            dy = jnp.where(j >= starts[i], refs[1 + npar + i][...], dy)
        _, vjp = jax.vjp(fn, *vals)
        grads = vjp(dy)
        for ref, g in zip(refs[1 + npar + nd:], grads):
            ref[...] = g.astype(ref.dtype)

    def piece(i):
        last = starts[i + 1] - starts[i] - 1
        return pl.BlockSpec((rows, LANES), lambda j: (0, jnp.clip(j - starts[i], 0, last)))

    pspecs = [pl.BlockSpec((p.shape[0], LANES), lambda j: (0, j)) for p in params]
    blk = pl.BlockSpec((rows, LANES), lambda j: (0, j))
    res = pl.pallas_call(
        body, grid=(n_blocks,), name=name,
        in_specs=[pl.BlockSpec((rows, LANES), lambda j: (0, first_block + j))] + pspecs + [piece(i) for i in range(nd)],
        out_specs=[blk] + pspecs,
        out_shape=[SDS((rows, n_blocks * LANES), BF16)] + [SDS(p.shape, F32) for p in params],
        compiler_params=pltpu.CompilerParams(dimension_semantics=("arbitrary",), vmem_limit_bytes=VMEM_LIMIT),
    )(x, *params, *dys)
    return res[0], res[1:]


def _conv_fn(x, w):
    acc = x * w[3:4, :]
    for j in range(3):
        acc = acc + shift_rows(x, 3 - j) * w[j:j + 1, :]
    return _silu(acc)


def _lerp_fn(x, mu):
    return x + (shift_rows(x, 1) - x) * mu[0:1, :]


def _seg_sum(x, width):
    if width == LANES:
        return jnp.sum(x, axis=1, keepdims=True)
    lo = lax.broadcasted_iota(jnp.int32, x.shape, 1) < width
    s0 = jnp.sum(jnp.where(lo, x, 0.0), axis=1, keepdims=True)
    s1 = jnp.sum(jnp.where(lo, 0.0, x), axis=1, keepdims=True)
    return jnp.where(lo, s0, s1)


def _per_block(fn, *xs):
    n = xs[0].shape[1] // LANES
    return jnp.concatenate([fn(*[x[:, LANES * b:LANES * (b + 1)] for x in xs]) for b in range(n)], axis=1)


def _head_expand(col0):
    r = lax.broadcasted_iota(jnp.int32, (LANES, DN_WIDTH), 0)
    c = lax.shift_right_logical(lax.broadcasted_iota(jnp.int32, (LANES, DN_WIDTH), 1), 7)
    return jnp.where(r == c + col0, 1.0, 0.0)


def _dn_pre_fn(cq, ck, gates, a_log, dt_bias):
    l2 = lambda x: x * lax.rsqrt(_seg_sum(x * x, LANES) + 1e-6)
    qh = _per_block(l2, cq) * (LANES ** -0.5)
    kh = _per_block(l2, ck)
    g = -jnp.exp(a_log) * _softplus(gates + dt_bias)
    gb = _select_cols(g, _head_expand(0))
    bb = _select_cols(_sigmoid(gates), _head_expand(DN_HEADS))
    return qh, kh, gb, bb, _cumsum_rows(gb)


def _dn_post_fn(o, z, nw):
    def one(ob, zb):
        return ob * lax.rsqrt(_seg_sum(ob * ob, LANES) * (1.0 / LANES) + RMS_EPS) * nw * _silu(zb)
    return (_per_block(one, o, z),)


def _rw_pre_fn(pr, pk, pv, pwa, pg, w0, a0, k_k, k_a, w2p, a2p, g2):
    log_w = -_softplus(-(w0 + mm(jnp.tanh(pwa), w2p))) - 0.5
    lw = -jnp.exp(log_w)
    a = _sigmoid(a0 + mm(pwa, a2p))
    gate = mm(_sigmoid(pg), g2)
    kk = pk * k_k
    kk = _per_block(lambda x: x / jnp.maximum(jnp.sqrt(_seg_sum(x * x, RW_HEAD)), 1e-12), kk)
    k = pk * (1.0 + (a - 1.0) * k_a)
    return pr, lw, k, pv, kk * a, -kk, gate, _cumsum_rows(lw)


def _rw_post_fn(y, r, k, v, gate, ln_w, ln_b, r_k):
    def one(yb, rb, kb, vb, gb, wb, bb, rkb):
        d = yb - _seg_sum(yb, RW_HEAD) * (1.0 / RW_HEAD)
        var = _seg_sum(d * d, RW_HEAD) * (1.0 / RW_HEAD)
        yn = d * lax.rsqrt(var + RW_GN_EPS) * wb + bb
        return (yn + _seg_sum(rb * kb * rkb, RW_HEAD) * vb) * gb
    return (_per_block(one, y, r, k, v, gate, ln_w, ln_b, r_k),)


def _rms_fn(h, w):
    return (h * lax.rsqrt(jnp.mean(h * h, axis=1, keepdims=True) + RMS_EPS) * w,)


def _xattn_fn(q, k, v):
    outs = []
    for h in range(XA_HEADS):
        sl = slice(LANES * h, LANES * (h + 1))
        s = mm(q[:, sl], k[:, sl], "nt") * (LANES ** -0.5)
        e = jnp.exp(s - jnp.max(s, axis=1, keepdims=True))
        outs.append(mm(e / jnp.sum(e, axis=1, keepdims=True), v[:, sl]))
    return (jnp.concatenate(outs, axis=1),)


def _fit(tile, dim):
    best = [t for t in range(LANES, min(tile, dim) + 1, LANES) if dim % t == 0]
    assert best, (tile, dim)
    return best[-1]


def _matmul(name, a, b, mode, out_dtypes, epilogue=None, extras=(), tm=1024, tn=1024, tk=2048, after=None):
    if mode == "tn":
        (k_dim, m), n = a.shape, b.shape[1]
    else:
        (m, k_dim), n = a.shape, (b.shape[1] if mode == "nn" else b.shape[0])
    tm, tn, tk = _fit(tm, m), _fit(tn, n), _fit(tk, k_dim)
    nk = k_dim // tk
    a_spec = (pl.BlockSpec((tk, tm), lambda i, j, k: (k, i)) if mode == "tn"
              else pl.BlockSpec((tm, tk), lambda i, j, k: (i, k)))
    b_spec = (pl.BlockSpec((tn, tk), lambda i, j, k: (j, k)) if mode == "nt"
              else pl.BlockSpec((tk, tn), lambda i, j, k: (k, j)))
    o_spec = pl.BlockSpec((tm, tn), lambda i, j, k: (i, j))
    n_ex, n_out = len(extras), len(out_dtypes)
    ties = [] if after is None else [after]

    def finish(total, rest):
        ex = [r[...].astype(F32) for r in rest[:n_ex]]
        res = epilogue(total, *ex) if epilogue else (total,)
        for ref, o in zip(rest[n_ex + len(ties):n_ex + len(ties) + n_out], res):
            ref[...] = o.astype(ref.dtype)

    def body_single(a_ref, b_ref, *rest):
        finish(_raw_dot(a_ref[...], b_ref[...], mode, False), rest)

    def body_acc(a_ref, b_ref, *rest):
        acc = rest[-1]
        k = pl.program_id(2)

        @pl.when(k == 0)
        def _():
            acc[...] = jnp.zeros_like(acc)

        acc[...] += _raw_dot(a_ref[...], b_ref[...], mode, False)

        @pl.when(k == nk - 1)
        def _():
            finish(acc[...], rest)

    res = pl.pallas_call(
        body_single if nk == 1 else body_acc, grid=(m // tm, n // tn, nk), name=name,
        in_specs=[a_spec, b_spec] + [o_spec] * n_ex + [pl.BlockSpec((8, LANES), lambda i, j, k: (0, 0))] * len(ties),
        out_specs=[o_spec] * n_out,
        out_shape=[SDS((m, n), dt) for dt in out_dtypes],
        scratch_shapes=[] if nk == 1 else [pltpu.VMEM((tm, tn), F32)],
        compiler_params=pltpu.CompilerParams(dimension_semantics=("parallel", "parallel", "arbitrary"),
                                             vmem_limit_bytes=VMEM_LIMIT),
    )(a, b, *extras, *ties)
    return res


def _matmul_norm_bwd(name, a, b, mode, h, w, dres, after=None, tm=512, tk=1024):
    m, n = h.shape
    k_dim = a.shape[1]
    tm, tk = _fit(tm, m), _fit(tk, k_dim)
    nk = k_dim // tk
    ties = [] if after is None else [after]
    a_spec = pl.BlockSpec((tm, tk), lambda i, k: (i, k))
    b_spec = pl.BlockSpec((n, tk), lambda i, k: (0, k)) if mode == "nt" else pl.BlockSpec((tk, n), lambda i, k: (k, 0))
    row = pl.BlockSpec((tm, n), lambda i, k: (i, 0))
    w_spec = pl.BlockSpec((1, n), lambda i, k: (0, 0))

    def body(a_ref, b_ref, h_ref, w_ref, dres_ref, *rest):
        dh_ref, dw_ref, acc = rest[len(ties):]
        i, k = pl.program_id(0), pl.program_id(1)

        @pl.when(k == 0)
        def _():
            acc[...] = jnp.zeros_like(acc)

        acc[...] += _raw_dot(a_ref[...], b_ref[...], mode, False)

        @pl.when(k == nk - 1)
        def _():
            _, vjp = jax.vjp(_rms_res_fn, h_ref[...], w_ref[...])
            dh, dw = vjp((acc[...], dres_ref[...]))
            dh_ref[...] = dh

            @pl.when(i == 0)
            def _():
                dw_ref[...] = dw

            @pl.when(i != 0)
            def _():
                dw_ref[...] += dw

    return pl.pallas_call(
        body, grid=(m // tm, nk), name=name,
        in_specs=[a_spec, b_spec, row, w_spec, row] + [pl.BlockSpec((8, LANES), lambda i, k: (0, 0))] * len(ties),
        out_specs=[row, w_spec],
        out_shape=[SDS((m, n), F32), SDS((1, n), F32)],
        scratch_shapes=[pltpu.VMEM((tm, n), F32)],
        compiler_params=pltpu.CompilerParams(dimension_semantics=("arbitrary", "arbitrary"),
                                             vmem_limit_bytes=VMEM_LIMIT),
    )(a, b, h, w, dres, *ties)


def _loss_call(h, target, w, tr=256):
    rows, d = h.shape

    def fn(hv, wv, tv):
        y = _rms_fn(hv, wv)[0]
        return 0.5 * jnp.sum(jnp.mean(jnp.square(y - tv), axis=1, keepdims=True), axis=0, keepdims=True)

    def body(h_ref, t_ref, w_ref, loss_ref, dh_ref, dw_ref):
        tv = t_ref[...]
        val, vjp = jax.vjp(lambda hv, wv: fn(hv, wv, tv), h_ref[...], w_ref[...])
        dh, dw = vjp(jnp.ones((1, 1), F32))
        dh_ref[...] = dh
        first = pl.program_id(0) == 0

        @pl.when(first)
        def _():
            loss_ref[...] = jnp.broadcast_to(val, loss_ref.shape)
            dw_ref[...] = dw

        @pl.when(jnp.logical_not(first))
        def _():
            loss_ref[...] += jnp.broadcast_to(val, loss_ref.shape)
            dw_ref[...] += dw

    return pl.pallas_call(
        body, grid=(rows // tr,), name="loss_head",
        in_specs=[_col_spec(tr, d, 0), _col_spec(tr, d, 0), _whole(w)],
        out_specs=[pl.BlockSpec((8, LANES), lambda i: (0, 0)), _col_spec(tr, d, 0), _whole(w)],
        out_shape=[SDS((8, LANES), F32), SDS((rows, d), F32), SDS(w.shape, F32)],
        compiler_params=pltpu.CompilerParams(dimension_semantics=("arbitrary",), vmem_limit_bytes=VMEM_LIMIT),
    )(h, target, w)


def _adamw_vals(w, g, m, v):
    m = ADAM_B1 * m + (1.0 - ADAM_B1) * g
    v = ADAM_B2 * v + (1.0 - ADAM_B2) * jnp.square(g)
    m_hat = m / (1.0 - ADAM_B1 ** ADAM_STEP)
    v_hat = v / (1.0 - ADAM_B2 ** ADAM_STEP)
    delta = -ADAM_LR * (m_hat / (jnp.sqrt(v_hat) + ADAM_EPS) + ADAM_WD * w)
    return delta, m, v


def _sum_adamw(name, parts, w, m, v):
    r, c = w.shape
    n_parts = parts.shape[0]
    budget = 6 * 1024 * 1024
    tr, tc = r, c
    for cand in (512, 256, 128, 64, 32, 16, 8):
        if r % cand == 0 and n_parts * cand * c * 4 <= budget:
            tr = cand
            break
    if n_parts * tr * c * 4 > budget:
        tc = max(t for t in range(LANES, c + 1, LANES) if c % t == 0 and n_parts * r * t * 4 <= budget)

    def body(p_ref, w_ref, m_ref, v_ref, g_ref, d_ref, m2_ref, v2_ref):
        g = p_ref[0].astype(F32)
        for s in range(1, n_parts):
            g = g + p_ref[s].astype(F32)
        g_ref[...] = g
        d_ref[...], m2_ref[...], v2_ref[...] = _adamw_vals(w_ref[...], g, m_ref[...], v_ref[...])

    blk = pl.BlockSpec((tr, tc), lambda i: (i, 0)) if tc == c else pl.BlockSpec((tr, tc), lambda i: (0, i))
    parts_blk = (pl.BlockSpec((n_parts, tr, tc), lambda i: (0, i, 0)) if tc == c
                 else pl.BlockSpec((n_parts, tr, tc), lambda i: (0, 0, i)))
    return pl.pallas_call(
        body, grid=(r // tr if tc == c else c // tc,), name=name,
        in_specs=[parts_blk, blk, blk, blk],
        out_specs=[blk] * 4, out_shape=[SDS((r, c), F32)] * 4,
        compiler_params=pltpu.CompilerParams(dimension_semantics=("arbitrary",), vmem_limit_bytes=VMEM_LIMIT),
    )(parts, w, m, v)


def _peers():
    x, y, c = lax.axis_index("x"), lax.axis_index("y"), lax.axis_index("c")
    peers = []
    for k in range(1, N_DEV):
        px = 1 - x if k & 4 else x
        py = 1 - y if k & 2 else y
        pc = 1 - c if k & 1 else c
        peers.append(((px, py, pc), 4 * px + 2 * py + pc))
    return 4 * x + 2 * y + c, peers


def _slot(ref, idx, cols):
    if cols is None:
        return ref.at[idx]
    return ref.at[:, pl.ds(pl.multiple_of(idx * cols, LANES), cols)]


def _gather_two_level(name, srcs, dsts):
    n = len(srcs)
    dst_cols = [c for _, _, c in dsts]

    def body(*refs):
        src_refs, out_refs = refs[:n], refs[n:2 * n]
        send_sems, recv_sems, local_sems = refs[2 * n:]
        x, y, c = lax.axis_index("x"), lax.axis_index("y"), lax.axis_index("c")
        index = lambda px, py, pc: 4 * px + 2 * py + pc
        me, sibling = index(x, y, c), (x, y, 1 - c)
        chips = [(x, 1 - y), (1 - x, y), (1 - x, 1 - y)]

        def copy(a, k, src, block, to):
            return pltpu.make_async_remote_copy(
                src_ref=src, dst_ref=_slot(out_refs[a], block, dst_cols[a]),
                send_sem=send_sems.at[a, k], recv_sem=recv_sems.at[a, k],
                device_id=to, device_id_type=pl.DeviceIdType.MESH)

        local, first, passed = [], [], []
        for a in range(n):
            cp = pltpu.make_async_copy(src_refs[a], _slot(out_refs[a], me, dst_cols[a]), local_sems.at[a])
            cp.start()
            local.append(cp)
            first.append(copy(a, 0, src_refs[a], me, sibling))
            first += [copy(a, 1 + j, src_refs[a], me, (*chip, c)) for j, chip in enumerate(chips)]
        for cp in first:
            cp.start()
        for a in range(n):
            for j, chip in enumerate(chips):
                block = index(*chip, c)
                arrived = _slot(out_refs[a], block, dst_cols[a])
                copy(a, 1 + j, arrived, block, (*chip, c)).wait_recv()
                passed.append(copy(a, 4 + j, arrived, block, sibling))
                passed[-1].start()
        for a in range(n):
            copy(a, 0, src_refs[a], index(x, y, 1 - c), sibling).wait_recv()
            for j, chip in enumerate(chips):
                block = index(*chip, 1 - c)
                copy(a, 4 + j, src_refs[a], block, sibling).wait_recv()
        for cp in first + passed:
            cp.wait_send()
        for cp in local:
            cp.wait()

    any_spec = pl.BlockSpec(memory_space=pl.ANY)
    return pl.pallas_call(
        body, name=name,
        in_specs=[any_spec] * n, out_specs=[any_spec] * n,
        out_shape=[SDS(shape, dt) for shape, dt, _ in dsts],
        scratch_shapes=_exchange_sems(n),
    )(*[a for a, _ in srcs])


_HBM = pl.BlockSpec(memory_space=pltpu.HBM)
_SEM = pl.BlockSpec(memory_space=pltpu.SEMAPHORE)
_EFFECT = pltpu.SideEffectType.DATAFLOW_SIDE_EFFECTING


def _split_copies(src_cols, dst_cols, gather, chips, src_refs, land_refs, send_sems, recv_sems, landings):
    me, peers = _peers()
    if chips:
        me, peers = me // 2, [(pos, idx // 2) for k, (pos, idx) in enumerate(peers) if (k + 1) in (2, 4, 6)]
    n, width = len(src_cols), len(peers)
    remote, local = [], []
    for a, (s_cols, d_cols) in enumerate(zip(src_cols, dst_cols)):
        mine = src_refs[a] if gather else _slot(src_refs[a], me, s_cols)
        local.append(pltpu.make_async_copy(mine, _slot(land_refs[a], me, d_cols), send_sems.at[n * width + a]))
        for k, (pos, idx) in enumerate(peers):
            blk = src_refs[a] if gather else _slot(src_refs[a], idx, s_cols)
            remote.append(pltpu.make_async_remote_copy(
                src_ref=blk, dst_ref=_slot(land_refs[a], idx if landings else me, d_cols),
                send_sem=send_sems.at[a * width + k], recv_sem=recv_sems.at[a * width + k],
                device_id=pos, device_id_type=pl.DeviceIdType.MESH))
    return remote, local


def _exchange_start(name, srcs, dsts, gather, after, chips=False):
    n = len(srcs)
    src_cols, dst_cols = [c for _, c in srcs], [c for _, _, c in dsts]
    width = 3 if chips else N_DEV - 1

    def body(*refs):
        src_refs, land_refs = refs[:n], refs[n:2 * n]
        send_sems, recv_sems = refs[2 * n + 1:2 * n + 3]
        token = refs[-1]
        remote, local = _split_copies(src_cols, dst_cols, gather, chips, src_refs, land_refs, send_sems, recv_sems,
                                      False)
        for cp in remote + local:
            cp.start()
        token[...] = jnp.zeros_like(token)

    hbm = lambda a: pltpu.with_memory_space_constraint(a, pltpu.HBM)
    lands = [hbm(lax.empty(shape, dt)) for shape, dt, _ in dsts]
    res = pl.pallas_call(
        body, name=name,
        out_shape=(pltpu.SemaphoreType.DMA((n * (width + 1),)), pltpu.SemaphoreType.DMA((n * width,)),
                   *[pltpu.HBM(a.shape, a.dtype) for a, _ in srcs], *[pltpu.HBM(a.shape, a.dtype) for a in lands],
                   SDS((8, LANES), F32)),
        in_specs=[_HBM] * (2 * n) + [pl.BlockSpec(memory_space=pl.ANY)],
        out_specs=(_SEM, _SEM, *[_HBM] * (2 * n), pl.BlockSpec(memory_space=pltpu.VMEM)),
        input_output_aliases={i: 2 + i for i in range(2 * n)},
        compiler_params=pltpu.CompilerParams(has_side_effects=_EFFECT),
    )(*[hbm(a) for a, _ in srcs], *lands, after)
    handle = (res[0], res[1], res[2:2 + n], res[2 + n:2 + 2 * n], src_cols, dst_cols, gather, chips)
    return handle, res[-1]


def _exchange_wait(name, handle, after):
    send_sems, recv_sems, src_thru, land_thru, src_cols, dst_cols, gather, chips = handle
    n = len(src_thru)

    def body(*refs):
        src_refs, land_refs = refs[:n], refs[n:2 * n]
        s_sems, r_sems = refs[2 * n:2 * n + 2]
        remote, local = _split_copies(src_cols, dst_cols, gather, chips, src_refs, land_refs, s_sems, r_sems, True)
        for cp in remote:
            cp.wait_send()
            cp.wait_recv()
        for cp in local:
            cp.wait()

    res = pl.pallas_call(
        body, name=name,
        out_shape=tuple(pltpu.HBM(a.shape, a.dtype) for a in (*src_thru, *land_thru)),
        in_specs=[_HBM] * (2 * n) + [_SEM, _SEM, pl.BlockSpec(memory_space=pl.ANY)],
        out_specs=tuple([_HBM] * (2 * n)),
        input_output_aliases={i: i for i in range(2 * n)},
        compiler_params=pltpu.CompilerParams(has_side_effects=_EFFECT),
    )(*src_thru, *land_thru, send_sems, recv_sems, after)
    return res[n:]


def _pair_swap(name, arrs):
    n = len(arrs)

    def body(*refs):
        src_refs, out_refs = refs[:n], refs[n:2 * n]
        send_sems, recv_sems = refs[2 * n:]
        x, y, c = lax.axis_index("x"), lax.axis_index("y"), lax.axis_index("c")
        copies = [pltpu.make_async_remote_copy(
            src_ref=src_refs[a].at[:, 1 - c], dst_ref=out_refs[a], send_sem=send_sems.at[a], recv_sem=recv_sems.at[a],
            device_id=(x, y, 1 - c), device_id_type=pl.DeviceIdType.MESH) for a in range(n)]
        for cp in copies:
            cp.start()
        for cp in copies:
            cp.wait()

    any_spec = pl.BlockSpec(memory_space=pl.ANY)
    return pl.pallas_call(
        body, name=name,
        in_specs=[any_spec] * n, out_specs=[any_spec] * n,
        out_shape=[SDS((a.shape[0],) + a.shape[2:], a.dtype) for a in arrs],
        scratch_shapes=[pltpu.SemaphoreType.DMA((n,)), pltpu.SemaphoreType.DMA((n,))],
    )(*arrs)


def _pair_add(name, mine, theirs):
    four, _, r, c = mine.shape
    tr = r
    for cand in (512, 256, 128, 64, 32, 16, 8):
        if r % cand == 0:
            tr = cand
            break
    tc = max(t for t in range(LANES, c + 1, LANES) if c % t == 0 and (t == LANES or 2 * tr * t * 4 <= 4 * 1024 * 1024))

    def body(m_ref, t_ref, o_ref):
        core = lax.axis_index("c")
        both = m_ref[...].astype(F32)
        own = jnp.where(core == 0, both[0], both[1])
        o_ref[...] = (own + t_ref[...].astype(F32)).astype(o_ref.dtype)

    return pl.pallas_call(
        body, grid=(four, r // tr, c // tc), name=name,
        in_specs=[pl.BlockSpec((None, 2, tr, tc), lambda i, j, k: (i, 0, j, k)),
                  pl.BlockSpec((None, tr, tc), lambda i, j, k: (i, j, k))],
        out_specs=pl.BlockSpec((None, tr, tc), lambda i, j, k: (i, j, k)),
        out_shape=SDS(theirs.shape, theirs.dtype),
        compiler_params=pltpu.CompilerParams(dimension_semantics=("arbitrary",) * 3, vmem_limit_bytes=VMEM_LIMIT),
    )(mine, theirs)


def _my_index():
    return 4 * lax.axis_index("x") + 2 * lax.axis_index("y") + lax.axis_index("c")


def _two_level_copies(stage, dst_cols, src_refs, land_refs, send_sems, recv_sems, landings):
    x, y, c = lax.axis_index("x"), lax.axis_index("y"), lax.axis_index("c")

    def pos(k):
        return (1 - x if k & 4 else x, 1 - y if k & 2 else y, 1 - c if k & 1 else c)

    def idx(k):
        px, py, pc = pos(k)
        return 4 * px + 2 * py + pc

    out = []
    for a, cols in enumerate(dst_cols):
        if stage == 1:
            for i, k in enumerate((1, 2, 4, 6)):
                out.append(pltpu.make_async_remote_copy(
                    src_ref=src_refs[a], dst_ref=_slot(land_refs[a], idx(k) if landings else idx(0), cols),
                    send_sem=send_sems.at[4 * a + i], recv_sem=recv_sems.at[4 * a + i],
                    device_id=pos(k), device_id_type=pl.DeviceIdType.MESH))
        else:
            for i, k in enumerate((2, 4, 6)):
                out.append(pltpu.make_async_remote_copy(
                    src_ref=_slot(land_refs[a], idx(k), cols),
                    dst_ref=_slot(land_refs[a], idx(k ^ 1) if landings else idx(k), cols),
                    send_sem=send_sems.at[3 * a + i], recv_sem=recv_sems.at[3 * a + i],
                    device_id=pos(1), device_id_type=pl.DeviceIdType.MESH))
    return out


def _gather2_start(name, srcs, dsts, after):
    n = len(srcs)
    dst_cols = [c for _, _, c in dsts]

    def body(*refs):
        src_refs, land_refs = refs[:n], refs[n:2 * n]
        send_sems, recv_sems = refs[2 * n + 1:2 * n + 3]
        me = _my_index()
        for a in range(n):
            pltpu.make_async_copy(src_refs[a], _slot(land_refs[a], me, dst_cols[a]), send_sems.at[4 * n + a]).start()
        for cp in _two_level_copies(1, dst_cols, src_refs, land_refs, send_sems, recv_sems, False):
            cp.start()
        refs[-1][...] = jnp.zeros_like(refs[-1])

    hbm = lambda a: pltpu.with_memory_space_constraint(a, pltpu.HBM)
    lands = [hbm(lax.empty(shape, dt)) for shape, dt, _ in dsts]
    res = pl.pallas_call(
        body, name=name,
        out_shape=(pltpu.SemaphoreType.DMA((5 * n,)), pltpu.SemaphoreType.DMA((4 * n,)),
                   *[pltpu.HBM(a.shape, a.dtype) for a, _ in srcs], *[pltpu.HBM(a.shape, a.dtype) for a in lands],
                   SDS((8, LANES), F32)),
        in_specs=[_HBM] * (2 * n) + [pl.BlockSpec(memory_space=pl.ANY)],
        out_specs=(_SEM, _SEM, *[_HBM] * (2 * n), pl.BlockSpec(memory_space=pltpu.VMEM)),
        input_output_aliases={i: 2 + i for i in range(2 * n)},
        compiler_params=pltpu.CompilerParams(has_side_effects=_EFFECT),
    )(*[hbm(a) for a, _ in srcs], *lands, after)
    return (res[0], res[1], res[2:2 + n], res[2 + n:2 + 2 * n], dst_cols), res[-1]


def _gather2_pass(name, handle, after):
    send1, recv1, src_thru, land_thru, dst_cols = handle
    n = len(src_thru)

    def body(*refs):
        src_refs, land_refs = refs[:n], refs[n:2 * n]
        s1, r1 = refs[2 * n:2 * n + 2]
        send2, recv2 = refs[2 * n + 3:2 * n + 5]
        me = _my_index()
        for cp in _two_level_copies(1, dst_cols, src_refs, land_refs, s1, r1, True):
            cp.wait_send()
            cp.wait_recv()
        for a in range(n):
            pltpu.make_async_copy(src_refs[a], _slot(land_refs[a], me, dst_cols[a]), s1.at[4 * n + a]).wait()
        for cp in _two_level_copies(2, dst_cols, src_refs, land_refs, send2, recv2, False):
            cp.start()
        refs[-1][...] = jnp.zeros_like(refs[-1])

    res = pl.pallas_call(
        body, name=name,
        out_shape=(pltpu.SemaphoreType.DMA((3 * n,)), pltpu.SemaphoreType.DMA((3 * n,)),
                   *[pltpu.HBM(a.shape, a.dtype) for a in (*src_thru, *land_thru)], SDS((8, LANES), F32)),
        in_specs=[_HBM] * (2 * n) + [_SEM, _SEM, pl.BlockSpec(memory_space=pl.ANY)],
        out_specs=(_SEM, _SEM, *[_HBM] * (2 * n), pl.BlockSpec(memory_space=pltpu.VMEM)),
        input_output_aliases={i: 2 + i for i in range(2 * n)},
        compiler_params=pltpu.CompilerParams(has_side_effects=_EFFECT),
    )(*src_thru, *land_thru, send1, recv1, after)
    return (res[0], res[1], res[2:2 + n], res[2 + n:2 + 2 * n], dst_cols), res[-1]


def _gather2_wait(name, handle, after):
    send2, recv2, src_thru, land_thru, dst_cols = handle
    n = len(src_thru)

    def body(*refs):
        src_refs, land_refs = refs[:n], refs[n:2 * n]
        s2, r2 = refs[2 * n:2 * n + 2]
        for cp in _two_level_copies(2, dst_cols, src_refs, land_refs, s2, r2, True):
            cp.wait_send()
            cp.wait_recv()

    res = pl.pallas_call(
        body, name=name,
        out_shape=tuple(pltpu.HBM(a.shape, a.dtype) for a in (*src_thru, *land_thru)),
        in_specs=[_HBM] * (2 * n) + [_SEM, _SEM, pl.BlockSpec(memory_space=pl.ANY)],
        out_specs=tuple([_HBM] * (2 * n)),
        input_output_aliases={i: i for i in range(2 * n)},
        compiler_params=pltpu.CompilerParams(has_side_effects=_EFFECT),
    )(*src_thru, *land_thru, send2, recv2, after)
    return res[n:]


def _exchange_sems(n):
    return [pltpu.SemaphoreType.DMA((n, N_DEV - 1)), pltpu.SemaphoreType.DMA((n, N_DEV - 1)),
            pltpu.SemaphoreType.DMA((n,))]


def _rms_res_fn(h, w):
    return _rms_fn(h, w)[0], h


def _add_epilogue(acc, res):
    return (acc + res,)


def _gather_plan(shards):
    srcs, dsts = [], []
    for n, sh in shards.items():
        r, c = sh.shape
        srcs.append((sh, None))
        if SHARDED[n] and c % LANES == 0:
            dsts.append(((r, N_DEV * c), sh.dtype, c))
        else:
            dsts.append(((N_DEV, r, c), sh.dtype, None))
    return srcs, dsts, True


def _w_in_segments():
    out = []
    for j in range(N_DEV):
        lo, hi = W_IN_SHARD * j, W_IN_SHARD * (j + 1)
        for a, b in ((lo, min(hi, DN_COLS)), (max(lo, DN_COLS), hi)):
            if a < b:
                out.append((j, a - lo, b - lo, a if a < DN_COLS else a + RW_OFF - DN_COLS))
    return out


def _w_in_to_padded(shards, tc=512):
    _, _, cols = shards.shape

    def body(g_ref, o_ref):
        o_ref[...] = jnp.zeros_like(o_ref)
        for j, a, b, dst in _w_in_segments():
            o_ref[dst:dst + b - a, :] = g_ref[j, a:b, :]

    return pl.pallas_call(
        body, grid=(cols // tc,), name="w_in_to_padded",
        in_specs=[pl.BlockSpec((N_DEV, W_IN_SHARD, tc), lambda i: (0, 0, i))],
        out_specs=pl.BlockSpec((IN_PAD, tc), lambda i: (0, i)),
        out_shape=SDS((IN_PAD, cols), shards.dtype),
        compiler_params=pltpu.CompilerParams(dimension_semantics=("arbitrary",), vmem_limit_bytes=VMEM_LIMIT),
    )(shards)


def _w_in_grad_to_shards(gw, tc=512):
    _, cols = gw.shape

    def body(w_ref, o_ref):
        for j, a, b, dst in _w_in_segments():
            o_ref[j, a:b, :] = w_ref[dst:dst + b - a, :]

    return pl.pallas_call(
        body, grid=(cols // tc,), name="w_in_grad_to_shards",
        in_specs=[pl.BlockSpec((IN_PAD, tc), lambda i: (0, i))],
        out_specs=pl.BlockSpec((N_DEV, W_IN_SHARD, tc), lambda i: (0, 0, i)),
        out_shape=SDS((N_DEV, W_IN_SHARD, cols), gw.dtype),
        compiler_params=pltpu.CompilerParams(dimension_semantics=("arbitrary",), vmem_limit_bytes=VMEM_LIMIT),
    )(gw)


def _gather_finish(names, outs):
    full = {}
    for n, arr in zip(names, outs):
        if n == "w_in":
            full[n] = _w_in_to_padded(arr)
        elif arr.ndim == 2:
            full[n] = arr
        elif SHARDED[n]:
            full[n] = arr.transpose(1, 0, 2).reshape(arr.shape[1], -1)
        else:
            full[n] = arr.reshape(-1, arr.shape[2])
    return full


def _scatter_plan(grads):
    srcs, dsts = [], []
    for n, gr in grads.items():
        if gr.ndim == 3:
            srcs.append((gr, None))
            dsts.append((gr.shape, gr.dtype, None))
            continue
        rows, cols = gr.shape
        if not SHARDED[n]:
            r, c = rows // N_DEV, cols
            srcs.append((gr.reshape(N_DEV, r, c), None))
        else:
            r, c = rows, cols // N_DEV
            if c % LANES == 0:
                srcs.append((gr, c))
            else:
                srcs.append((gr.reshape(r, N_DEV, c).transpose(1, 0, 2), None))
        dsts.append(((N_DEV, r, c), gr.dtype, None))
    return srcs, dsts, False


def _local_step(x, mem, target, wt, late):
    d = D_MODEL
    g = {}
    wt = dict(wt)
    grp_a = ("w_out", "xa_wq", "xa_wk", "xa_wv", "xa_wo")
    grp_b = ("ffn_w1", "ffn_w2")
    plan = lambda names: _gather_plan({n: late[n] for n in names})[:2]
    handle_a, tok_a = _gather2_start("late_gather_a_start", *plan(grp_a), wt["w_in"])
    handle_w1, tok_b = _gather2_start("late_gather_w1_start", *plan(("ffn_w1",)), tok_a)
    handle_w2, tok_c = _gather2_start("late_gather_w2_start", *plan(("ffn_w2",)), tok_b)
    mix_w = wt["mix_norm_w"] + (tok_a[0:1, 0:1] + tok_b[0:1, 0:1] + tok_c[0:1, 0:1])
    u = _row_fwd(_rms_fn, "mix_norm", [(x, d, 0)], [mix_w], [(d, BF16)], 256)[0]
    p = _matmul("in_proj", u, wt["w_in"], "nt", [F32], tn=1536)[0]
    c = _col_fwd(_conv_fn, "dn_conv", p, 0, 24, [wt["dn_conv_w"]])
    handle_a, tok = _gather2_pass("late_gather_a_pass", handle_a, c)
    dn_pre_tiles = [(c, DN_WIDTH, 0), (c, DN_WIDTH, 1), (p, LANES, 32)]
    dn_pre_params = [wt["dn_a_log"], wt["dn_dt_bias"]]
    qh, kh, gb, bb, gcb = _row_fwd(_dn_pre_fn, "dn_pre", dn_pre_tiles, [dn_pre_params[0] + tok[0:1, :], dn_pre_params[1]],
                                   [(DN_WIDTH, F32)] * 5, CHUNK)
    dn_arrs = [(qh, 0), (kh, 0), (c, 16), (gb, 0), (bb, 0), (gcb, 0)]
    o, kept_dn = _scan_fwd(_gdn_group, "gdn_scan", dn_arrs, DN_HEADS, 1)
    dn_post_tiles = [(o, DN_WIDTH, 0), (p, DN_WIDTH, 3)]
    o_dn = _row_fwd(_dn_post_fn, "dn_post", dn_post_tiles, [wt["dn_norm_w"]], [(DN_WIDTH, BF16)], 256)[0]

    ps = _col_fwd(_lerp_fn, "rw_shift", p, RW_OFF // LANES, 26, [wt["rw_mu"]])
    rw_pre_tiles = [(ps, RW_WIDTH, 0), (ps, RW_WIDTH, 1), (ps, RW_WIDTH, 2), (ps, LANES, 24), (ps, LANES, 25)]
    rw_pre_params = [wt[n] for n in ("rw_w0", "rw_a0", "rw_k_k", "rw_k_a", "rw_w2", "rw_a2", "rw_g2")]
    r, lw, k, v, al, be, gate, gcw = _row_fwd(_rw_pre_fn, "rw_pre", rw_pre_tiles, rw_pre_params,
                                              [(RW_WIDTH, F32)] * 8, CHUNK)
    rw_arrs = [(r, 0), (lw, 0), (k, 0), (v, 0), (al, 0), (be, 0), (gcw, 0)]
    y, kept_rw = _scan_fwd(_rw_group, "rw_scan", rw_arrs, RW_WIDTH // LANES, 2)
    handle_w1, tok = _gather2_pass("late_gather_w1_pass", handle_w1, y)
    rw_post_tiles = [(t, RW_WIDTH, 0) for t in (y, r, k, v, gate)]
    rw_post_params = [wt["rw_ln_w"], wt["rw_ln_b"], wt["rw_r_k"]]
    o_rw = _row_fwd(_rw_post_fn, "rw_post", rw_post_tiles, [rw_post_params[0] + tok[0:1, 0:1]] + rw_post_params[1:],
                    [(RW_WIDTH, BF16)], 128)[0]
    o_cat = jnp.concatenate([o_dn, o_rw], axis=1)
    wt.update(_gather_finish(grp_a, _gather2_wait("late_gather_a_wait", handle_a, o_cat)))
    h1 = _matmul("out_proj", o_cat, wt["w_out"], "nn", [F32], _add_epilogue, (x,))[0]

    handle_w2, tok = _gather2_pass("late_gather_w2_pass", handle_w2, h1)
    hn = _row_fwd(_rms_fn, "xa_norm", [(h1, d, 0)], [wt["xa_norm_w"] + tok[0:1, 0:1]], [(d, BF16)], 256)[0]
    mn = _row_fwd(_rms_fn, "mem_norm", [(mem, d, 0)], [wt["mem_norm_w"]], [(d, BF16)], 256)[0]
    q = _matmul("xa_q", hn, wt["xa_wq"], "nn", [F32])[0]
    kx = _matmul("xa_k", mn, wt["xa_wk"], "nn", [F32])[0]
    vx = _matmul("xa_v", mn, wt["xa_wv"], "nn", [F32])[0]
    ao = _row_fwd(_xattn_fn, "xattn", [(q, XA_WIDTH, 0)], [kx, vx], [(XA_WIDTH, BF16)], 256)[0]
    h2 = _matmul("xa_o", ao, wt["xa_wo"], "nn", [F32], _add_epilogue, (h1,))[0]

    f = _row_fwd(_rms_fn, "ffn_norm", [(h2, d, 0)], [wt["ffn_norm_w"]], [(d, BF16)], 256)[0]
    wt.update(_gather_finish(("ffn_w1",), _gather2_wait("late_gather_w1_wait", handle_w1, f)))
    a, hid = _matmul("ffn_up", f, wt["ffn_w1"], "nn", [F32, BF16],
                     lambda acc: (acc, jnp.square(jnp.maximum(acc, 0.0))))
    wt.update(_gather_finish(("ffn_w2",), _gather2_wait("late_gather_w2_wait", handle_w2, hid)))
    h3 = _matmul("ffn_down", hid, wt["ffn_w2"], "nn", [F32], _add_epilogue, (h2,))[0]
    loss8, dh3, g["final_norm_w"] = _loss_call(h3, target, wt["final_norm_w"])

    da = _matmul("ffn_down_dx", dh3, wt["ffn_w2"], "nt", [BF16],
                 lambda acc, av: (acc * 2.0 * jnp.maximum(av, 0.0),), (a,))[0]
    g["ffn_w2"] = _matmul("ffn_down_dw", hid, dh3, "tn", [BF16])[0]
    g["ffn_w1"] = _matmul("ffn_up_dw", f, da, "tn", [BF16])[0]
    pending = {}
    plan = _scatter_plan({n: g.pop(n) for n in grp_b})
    pending[grp_b], tok = _exchange_start("late_grad_b_start", *plan, loss8)
    dh2, g["ffn_norm_w"] = _matmul_norm_bwd("ffn_up_dx", da, wt["ffn_w1"], "nt", h2, wt["ffn_norm_w"], dh3, tok)

    dao = _matmul("xa_o_dx", dh2, wt["xa_wo"], "nt", [F32])[0]
    g["xa_wo"] = _matmul("xa_o_dw", ao, dh2, "tn", [BF16])[0]
    (dq,), (dkx, dvx) = _row_bwd(_xattn_fn, "xattn_bwd", [(q, XA_WIDTH, 0)], [kx, vx], [[(dao, XA_WIDTH, 0)]], 256)
    dh1, g["xa_norm_w"] = _matmul_norm_bwd("xa_q_dx", dq, wt["xa_wq"], "nt", h1, wt["xa_norm_w"], dh2)
    g["xa_wq"] = _matmul("xa_q_dw", hn, dq, "tn", [BF16])[0]
    g["xa_wk"] = _matmul("xa_k_dw", mn, dkx, "tn", [BF16])[0]
    g["xa_wv"] = _matmul("xa_v_dw", mn, dvx, "tn", [BF16])[0]
    dmn = _matmul("xa_k_dx", dkx, wt["xa_wk"], "nt", [F32])[0]
    dmn = _matmul("xa_v_dx", dvx, wt["xa_wv"], "nt", [F32], _add_epilogue, (dmn,))[0]
    _, (g["mem_norm_w"],) = _row_bwd(_rms_fn, "mem_norm_bwd", [(mem, d, 0)], [wt["mem_norm_w"]],
                                     [[(dmn, d, 0)]], 256, want_tiles=())

    do_cat = _matmul("out_proj_dx", dh1, wt["w_out"], "nt", [F32])[0]
    g["w_out"] = _matmul("out_proj_dw", o_cat, dh1, "tn", [BF16])[0]

    plan = _scatter_plan({n: g.pop(n) for n in grp_a})
    pending[grp_a], tok = _exchange_start("late_grad_a_start", *plan, tok)
    (dy, dr1, dk1, dv1, dgate), (g["rw_ln_w"], g["rw_ln_b"], g["rw_r_k"]) = _row_bwd(
        _rw_post_fn, "rw_post_bwd", rw_post_tiles, [rw_post_params[0] + tok[0:1, 0:1]] + rw_post_params[1:],
        [[(do_cat, RW_WIDTH, 1)]], 128)
    dr2, dlw, dk2, dv2, dal, dbe, dgcw = _scan_bwd(_rw_group, "rw_scan_bwd", rw_arrs, kept_rw, dy,
                                                   RW_WIDTH // LANES)
    one = lambda t: [(t, RW_WIDTH, 0)]
    two = lambda s, t: [(s, RW_WIDTH, 0), (t, RW_WIDTH, 0)]
    d_ps, rw_pre_grads = _row_bwd(
        _rw_pre_fn, "rw_pre_bwd", rw_pre_tiles, rw_pre_params,
        [two(dr1, dr2), one(dlw), two(dk1, dk2), two(dv1, dv2), one(dal), one(dbe), one(dgate), one(dgcw)],
        CHUNK)
    for n, val in zip(("rw_w0", "rw_a0", "rw_k_k", "rw_k_a", "rw_w2", "rw_a2", "rw_g2"), rw_pre_grads):
        g[n] = val
    dp_rw, (g["rw_mu"],) = _col_bwd(_lerp_fn, "rw_shift_bwd", p, RW_OFF // LANES, 26, [wt["rw_mu"]], list(d_ps))

    (do, dz), (g["dn_norm_w"],) = _row_bwd(_dn_post_fn, "dn_post_bwd", dn_post_tiles, [wt["dn_norm_w"]],
                                           [[(do_cat, DN_WIDTH, 0)]], 256)
    dqh, dkh, dv_dn, dgb, dbb, dgcb = _scan_bwd(_gdn_group, "gdn_scan_bwd", dn_arrs, kept_dn, do, DN_HEADS)
    one = lambda t: [(t, DN_WIDTH, 0)]
    (dcq, dck, dgates), (g["dn_a_log"], g["dn_dt_bias"]) = _row_bwd(
        _dn_pre_fn, "dn_pre_bwd", dn_pre_tiles, dn_pre_params,
        [one(dqh), one(dkh), one(dgb), one(dbb), one(dgcb)], CHUNK)
    dp_qkv, (g["dn_conv_w"],) = _col_bwd(_conv_fn, "dn_conv_bwd", p, 0, 24, [wt["dn_conv_w"]], [dcq, dck, dv_dn])
    dp = jnp.concatenate([t.astype(BF16) for t in (dp_qkv, dz, dgates, dp_rw, jnp.zeros((x.shape[0], LANES), F32))],
                         axis=1)
    g["w_in"] = _matmul("in_proj_dw", dp, u, "tn", [BF16], tm=1536)[0]
    early = _logical_grads(g)
    blocks = []
    for src, cols in _scatter_plan({n: early.pop(n) for n in EARLY})[0]:
        if cols is not None:
            src = src.reshape(src.shape[0], N_DEV, cols).transpose(1, 0, 2)
        blocks.append(src.reshape((4, 2) + src.shape[1:]))
    sums = [_pair_add("early_grad_pair_add_%d" % i, mine, theirs)
            for i, (mine, theirs) in enumerate(zip(blocks, _pair_swap("early_grad_pair_swap", blocks)))]
    pending[EARLY], tok = _exchange_start("early_grad_start", [(t, None) for t in sums],
                                          [(t.shape, t.dtype, None) for t in sums], False, tok, chips=True)
    dx, early["mix_norm_w"] = _matmul_norm_bwd("in_proj_dx", dp, wt["w_in"], "nn", x, wt["mix_norm_w"], dh1, tok)
    return loss8, dx, early, pending, tok


WEIGHTS = ["mix_norm_w", "w_in", "dn_conv_w", "dn_a_log", "dn_dt_bias", "dn_norm_w", "rw_mu", "rw_w0", "rw_w2",
           "rw_a0", "rw_a2", "rw_g2", "rw_k_k", "rw_k_a", "rw_r_k", "rw_ln_w", "rw_ln_b", "w_out", "xa_norm_w",
           "mem_norm_w", "xa_wq", "xa_wk", "xa_wv", "xa_wo", "ffn_norm_w", "ffn_w1", "ffn_w2", "final_norm_w"]
SHARDED = {"w_in": False, "w_out": False, "xa_wq": False, "xa_wk": False, "xa_wv": False, "xa_wo": True,
           "ffn_w1": True, "ffn_w2": False, "dn_conv_w": True, "rw_w2": True, "rw_a2": True, "rw_g2": True}
BF16_PAYLOAD = ("w_in", "w_out", "xa_wq", "xa_wk", "xa_wv", "xa_wo", "ffn_w1", "ffn_w2")
REPLICATED = [n for n in WEIGHTS if n not in SHARDED]
EARLY = ("w_in", "dn_conv_w", "rw_w2", "rw_a2", "rw_g2")
RW_IN_COLS = IN_COLS - DN_COLS
W_IN_SHARD = IN_COLS // N_DEV


def _layout_weights(fw):
    wt = dict(fw)
    wt["dn_conv_w"] = jnp.pad(fw["dn_conv_w"], ((0, 4), (0, 0)))
    wt["dn_a_log"] = jnp.pad(fw["dn_a_log"], ((0, 0), (0, LANES - DN_HEADS)))
    wt["dn_dt_bias"] = jnp.pad(fw["dn_dt_bias"], ((0, 0), (0, LANES - DN_HEADS)))
    wt["rw_w2"] = jnp.pad(fw["rw_w2"], ((0, 64), (0, 0)))
    wt["rw_a2"] = jnp.pad(fw["rw_a2"], ((64, 0), (0, 0)))
    return wt


def _logical_grads(g):
    out = dict(g)
    out["w_in"] = _w_in_grad_to_shards(g["w_in"])
    out["dn_conv_w"] = g["dn_conv_w"][:4]
    out["dn_a_log"] = g["dn_a_log"][:, :DN_HEADS]
    out["dn_dt_bias"] = g["dn_dt_bias"][:, :DN_HEADS]
    out["rw_w2"] = g["rw_w2"][:64]
    out["rw_a2"] = g["rw_a2"][64:]
    return out


def _pack(vals):
    parts = []
    for v in vals:
        flat = v.reshape(-1)
        parts.append(jnp.pad(flat, (0, -flat.shape[0] % LANES)))
    flat = jnp.concatenate(parts)
    flat = jnp.pad(flat, (0, -flat.shape[0] % (8 * LANES)))
    return flat.reshape(-1, LANES)


def _unpack(packed, shapes):
    flat = packed.reshape(-1)
    out, at = [], 0
    for shp in shapes:
        size = math.prod(shp)
        out.append(flat[at:at + size].reshape(shp))
        at += size + (-size % LANES)
    return out


def kernel(x, mem, mix_norm_w, w_in, dn_conv_w, dn_a_log, dn_dt_bias, dn_norm_w, rw_mu, rw_w0, rw_w2, rw_a0, rw_a2, rw_g2, rw_k_k, rw_k_a, rw_r_k, rw_ln_w, rw_ln_b, w_out, xa_norm_w, mem_norm_w, xa_wq, xa_wk, xa_wv, xa_wo, ffn_norm_w, ffn_w1, ffn_w2, final_norm_w, loss_target, m_mix_norm_w, m_w_in, m_dn_conv_w, m_dn_a_log, m_dn_dt_bias, m_dn_norm_w, m_rw_mu, m_rw_w0, m_rw_w2, m_rw_a0, m_rw_a2, m_rw_g2, m_rw_k_k, m_rw_k_a, m_rw_r_k, m_rw_ln_w, m_rw_ln_b, m_w_out, m_xa_norm_w, m_mem_norm_w, m_xa_wq, m_xa_wk, m_xa_wv, m_xa_wo, m_ffn_norm_w, m_ffn_w1, m_ffn_w2, m_final_norm_w, v_mix_norm_w, v_w_in, v_dn_conv_w, v_dn_a_log, v_dn_dt_bias, v_dn_norm_w, v_rw_mu, v_rw_w0, v_rw_w2, v_rw_a0, v_rw_a2, v_rw_g2, v_rw_k_k, v_rw_k_a, v_rw_r_k, v_rw_ln_w, v_rw_ln_b, v_w_out, v_xa_norm_w, v_mem_norm_w, v_xa_wq, v_xa_wk, v_xa_wv, v_xa_wo, v_ffn_norm_w, v_ffn_w1, v_ffn_w2, v_final_norm_w):
    given = dict(locals())
    w = {n: given[n] for n in WEIGHTS}
    m = {n: given["m_" + n] for n in WEIGHTS}
    v = {n: given["v_" + n] for n in WEIGHTS}

    local = {n: (lambda t: t[0].T) if n == "w_in" else (lambda t: t[0]) for n in SHARDED}
    shards = {n: (local[n](w[n]).astype(BF16) if n in BF16_PAYLOAD else local[n](w[n])) for n in SHARDED}
    srcs, dsts, _ = _gather_plan({n: shards[n] for n in EARLY})
    full = _gather_finish(EARLY, _gather_two_level("early_all_gather", srcs, dsts))
    for n in REPLICATED:
        full[n] = w[n].reshape(1, -1)

    loss8, dx, g, pending, after = _local_step(x[0], mem[0], loss_target[0], _layout_weights(full),
                                               {n: shards[n] for n in SHARDED if n not in EARLY})

    packed = _pack([g[n] for n in REPLICATED] + [loss8[:1, :1]])
    small, _ = _exchange_start("small_gather_start", [(packed, None)], [((N_DEV,) + packed.shape, F32, None)], True,
                               after)
    grad, delta, new_m, new_v = {}, {}, {}, {}
    done = [dx]

    def tie():
        return jnp.broadcast_to(sum(t[:1, :1] for t in done), (8, LANES))

    for names in sorted(pending, key=lambda names: names == EARLY):
        handle = pending[names]
        for n, parts in zip(names, _exchange_wait("grad_wait_" + names[0], handle, tie())):
            res = _sum_adamw("adamw_" + n, parts, local[n](w[n]), local[n](m[n]), local[n](v[n]))
            grad[n], delta[n], new_m[n], new_v[n] = [(t.T if n == "w_in" else t)[None] for t in res]
            done.append(res[1])

    (parts,) = _exchange_wait("small_gather_wait", small, tie())
    blank = [jnp.zeros((1, 1), F32)]
    res = _sum_adamw("adamw_small", parts, _pack([w[n] for n in REPLICATED] + blank),
                     _pack([m[n] for n in REPLICATED] + blank), _pack([v[n] for n in REPLICATED] + blank))
    shapes = [w[n].shape for n in REPLICATED] + [()]
    loss = _unpack(res[0], shapes)[-1]
    for store, packed_out in zip((grad, delta, new_m, new_v), res):
        for n, val in zip(REPLICATED, _unpack(packed_out, shapes)):
            store[n] = val

    return (loss, dx[None], *[grad[n] for n in WEIGHTS], *[delta[n] for n in WEIGHTS],
            *[new_m[n] for n in WEIGHTS], *[new_v[n] for n in WEIGHTS])
```

```python
import functools
import math

import jax
import jax.numpy as jnp
from jax import lax
from jax.experimental import pallas as pl
from jax.experimental.pallas import tpu as pltpu

F32 = jnp.float32
BF16 = jnp.bfloat16
SDS = jax.ShapeDtypeStruct

N_DEV = 8
D_MODEL = 2048
LANES = 128
CHUNK = 128
DN_HEADS = 8
DN_WIDTH = 1024
RW_WIDTH = 1024
RW_HEAD = 64
XA_HEADS = 4
XA_WIDTH = 512
FFN_HIDDEN = 8192
IN_COLS = 7440
DN_COLS = 4112
IN_PAD = 7680
RW_OFF = 4224
RMS_EPS = 1e-6
RW_GN_EPS = 64e-5
VMEM_LIMIT = 56 * 1024 * 1024

ADAM_LR = 0.001
ADAM_B1 = 0.9
ADAM_B2 = 0.999
ADAM_EPS = 1e-08
ADAM_WD = 0.01
ADAM_STEP = 10

_DIMS = {"nn": (((1,), (0,)), ((), ())), "nt": (((1,), (1,)), ((), ())), "tn": (((0,), (0,)), ((), ()))}


def _raw_dot(a, b, mode, hi):
    if hi:
        return lax.dot_general(a, b, _DIMS[mode], precision=lax.Precision.HIGHEST, preferred_element_type=F32)
    return lax.dot_general(a.astype(BF16), b.astype(BF16), _DIMS[mode], preferred_element_type=F32)


@functools.partial(jax.custom_vjp, nondiff_argnums=(2, 3))
def mm(a, b, mode="nn", hi=False):
    return _raw_dot(a, b, mode, hi)


def _mm_fwd(a, b, mode, hi):
    return _raw_dot(a, b, mode, hi), (a, b)


def _mm_bwd(mode, hi, res, g):
    a, b = res
    if mode == "nn":
        return _raw_dot(g, b, "nt", hi), _raw_dot(a, g, "tn", hi)
    if mode == "nt":
        return _raw_dot(g, b, "nn", hi), _raw_dot(g, a, "tn", hi)
    return _raw_dot(b, g, "nt", hi), _raw_dot(a, g, "nn", hi)


mm.defvjp(_mm_fwd, _mm_bwd)


def _shift_rows_raw(x, k):
    n = x.shape[0]
    rolled = pltpu.roll(x, k % n, axis=0)
    row = lax.broadcasted_iota(jnp.int32, x.shape, 0)
    keep = row >= k if k > 0 else row < n + k
    return jnp.where(keep, rolled, 0.0)


@functools.partial(jax.custom_vjp, nondiff_argnums=(1,))
def shift_rows(x, k):
    return _shift_rows_raw(x, k)


shift_rows.defvjp(lambda x, k: (_shift_rows_raw(x, k), None), lambda k, _, g: (_shift_rows_raw(g, -k),))


@jax.custom_vjp
def _sigmoid(x):
    return 1.0 / (1.0 + jnp.exp(-x))


def _sigmoid_fwd(x):
    s = 1.0 / (1.0 + jnp.exp(-x))
    return s, s


_sigmoid.defvjp(_sigmoid_fwd, lambda s, g: (g * s * (1.0 - s),))


@jax.custom_vjp
def _softplus(x):
    return jnp.maximum(x, 0.0) + jnp.log(1.0 + jnp.exp(-jnp.abs(x)))


_softplus.defvjp(lambda x: (_softplus(x), x), lambda x, g: (g / (1.0 + jnp.exp(-x)),))


@jax.custom_vjp
def _silu(x):
    return x / (1.0 + jnp.exp(-x))


def _silu_fwd(x):
    s = 1.0 / (1.0 + jnp.exp(-x))
    return x * s, (x, s)


_silu.defvjp(_silu_fwd, lambda res, g: (g * res[1] * (1.0 + res[0] * (1.0 - res[1])),))


def _tri_masks(n):
    ii = lax.broadcasted_iota(jnp.int32, (n, n), 0)
    jj = lax.broadcasted_iota(jnp.int32, (n, n), 1)
    return ii >= jj, ii > jj, ii == jj


def _neumann_inv_raw(m):
    n = m.shape[0]
    _, _, eye = _tri_masks(n)
    eye = jnp.where(eye, 1.0, 0.0)
    p = eye + m
    mk = m
    for _ in range(int(math.log2(n)) - 1):
        mk = _raw_dot(mk, mk, "nn", False)
        p = p + _raw_dot(p, mk, "nn", False)
    m_hi, p_hi = m.astype(BF16), p.astype(BF16)
    m_lo, p_lo = m - m_hi.astype(F32), p - p_hi.astype(F32)
    mp = _raw_dot(m_hi, p_hi, "nn", False) + _raw_dot(m_hi, p_lo, "nn", False) + _raw_dot(m_lo, p_hi, "nn", False)
    return p + _raw_dot(p, eye - p + mp, "nn", False)


@jax.custom_vjp
def _neumann_inv(m):
    return _neumann_inv_raw(m)


def _neumann_inv_fwd(m):
    p = _neumann_inv_raw(m)
    return p, p


def _neumann_inv_bwd(p, g):
    return (_raw_dot(_raw_dot(p, g, "tn", False), p, "nt", False),)


_neumann_inv.defvjp(_neumann_inv_fwd, _neumann_inv_bwd)


@jax.custom_vjp
def _saved_inv(m, p):
    return p


_saved_inv.defvjp(lambda m, p: (p, p), lambda p, g: (_neumann_inv_bwd(p, g)[0], jnp.zeros_like(p)))


def _inverse(m, saved):
    return _neumann_inv(m) if saved is None else _saved_inv(m, saved)


def _split3(x):
    hi = x.astype(BF16)
    rest = x - hi.astype(F32)
    mid = rest.astype(BF16)
    return hi, mid, (rest - mid.astype(F32)).astype(BF16)


def _select_dot_raw(sel, x, mode, x_first):
    s = sel.astype(BF16)
    parts = [lax.dot_general(p, s, _DIMS[mode], preferred_element_type=F32) if x_first
             else lax.dot_general(s, p, _DIMS[mode], preferred_element_type=F32) for p in _split3(x)]
    return parts[0] + parts[1] + parts[2]


@functools.partial(jax.custom_vjp, nondiff_argnums=(2,))
def _select_rows(sel, x, transposed=False):
    return _select_dot_raw(sel, x, "tn" if transposed else "nn", False)


_select_rows.defvjp(lambda sel, x, transposed: (_select_rows(sel, x, transposed), sel),
                    lambda transposed, sel, g: (jnp.zeros_like(sel), _select_rows(sel, g, not transposed)))


@functools.partial(jax.custom_vjp, nondiff_argnums=(2,))
def _select_cols(x, sel, transposed=False):
    return _select_dot_raw(sel, x, "nt" if transposed else "nn", True)


_select_cols.defvjp(lambda x, sel, transposed: (_select_cols(x, sel, transposed), sel),
                    lambda transposed, sel, g: (_select_cols(g, sel, not transposed), jnp.zeros_like(sel)))


def _cumsum_rows(x):
    causal, _, _ = _tri_masks(x.shape[0])
    return _select_rows(jnp.where(causal, 1.0, 0.0), x)


def _gdn_group(s0, q, k, v, gb, bb, gc, *saved):
    diff = jnp.stack([gc[j] - gc[j].T for j in range(gc.shape[0])])
    return jax.vmap(_gdn_chunk)(s0, q, k, v, gb, bb, gc, diff, *saved)


def _rw_group(*args):
    return jax.vmap(_rw_chunk)(*args)


def _gdn_chunk(s0, q, k, v, gb, bb, gc, diff, saved=None):
    c = q.shape[0]
    causal, strict, _ = _tri_masks(c)
    decay = jnp.exp(jnp.where(causal, diff, -jnp.inf))
    kb = k * bb
    a = jnp.where(strict, mm(kb, k, "nt") * decay, 0.0)
    p = _inverse(-a, saved)
    uw = mm(p, jnp.concatenate([v * bb, kb * jnp.exp(gc)], axis=1))
    u, w = uw[:, :LANES], uw[:, LANES:]
    attn = mm(q, k, "nt") * decay
    v_new = u - mm(w, s0)
    o = mm(q * jnp.exp(gc), s0) + mm(attn, v_new)
    g_last = jnp.sum(gb, axis=0, keepdims=True)
    s1 = s0 * jnp.exp(g_last) + mm(k * jnp.exp(g_last - gc), v_new, "tn")
    return o, s1, p


def _rw_chunk(s0, r, lw, k, v, al, be, gc, saved0=None, saved1=None):
    c = r.shape[0]
    causal, strict, _ = _tri_masks(c)
    gp = gc - lw
    row = lax.broadcasted_iota(jnp.int32, lw.shape, 0)
    lane = lax.broadcasted_iota(jnp.int32, lw.shape, 1)
    g_mid = jnp.sum(jnp.where(row < c // 2, lw, 0.0), axis=0, keepdims=True)
    g_last = jnp.sum(lw, axis=0, keepdims=True)
    e_n = jnp.exp(g_mid - gc)
    rg = r * jnp.exp(gc - g_mid)
    bg = be * jnp.exp(gp - g_mid)
    an = al * e_n
    kn = k * e_n
    bt = mm(be * jnp.exp(gp), s0, "nt")
    rt = mm(r * jnp.exp(gc), s0, "nt")
    us, ys, ps = [], [], []
    ank = jnp.concatenate([an, kn], axis=0)
    for h, saved in enumerate((saved0, saved1)):
        mine = (lane >= RW_HEAD) if h else (lane < RW_HEAD)
        from_b = mm(jnp.where(mine, bg, 0.0), ank, "nt")
        from_r = mm(jnp.where(mine, rg, 0.0), ank, "nt")
        a_ab = jnp.where(strict, from_b[:, :c], 0.0)
        a_kb = jnp.where(strict, from_b[:, c:], 0.0)
        a_ra = jnp.where(causal, from_r[:, :c], 0.0)
        a_rk = jnp.where(causal, from_r[:, c:], 0.0)
        p = _inverse(a_ab, saved)
        ps.append(p)
        u_h = mm(p, bt + mm(a_kb, v))
        us.append(u_h)
        ys.append(rt + mm(a_ra, u_h) + mm(a_rk, v))
    lo = lane < RW_HEAD
    u = jnp.where(lo, us[0], us[1])
    y = jnp.where(lo, ys[0], ys[1])
    tail = jnp.exp(g_last - gc)
    s1 = s0 * jnp.exp(g_last) + mm(u, al * tail, "tn") + mm(v, k * tail, "tn")
    vi = lax.broadcasted_iota(jnp.int32, s0.shape, 0)
    ki = lax.broadcasted_iota(jnp.int32, s0.shape, 1)
    s1 = jnp.where((vi < RW_HEAD) == (ki < RW_HEAD), s1, 0.0)
    return y, s1, ps[0], ps[1]


SCAN_HB = 8


def _scan_specs(arrs, n_chunks, reverse):
    def spec(off):
        assert off % SCAN_HB == 0
        if reverse:
            return pl.BlockSpec((CHUNK, SCAN_HB * LANES), lambda h, n: (n_chunks - 1 - n, off // SCAN_HB + h))
        return pl.BlockSpec((CHUNK, SCAN_HB * LANES), lambda h, n: (n, off // SCAN_HB + h))
    return [spec(off) for _, off in arrs]


def _split_heads(x):
    return jnp.stack([x[:, LANES * j:LANES * (j + 1)] for j in range(SCAN_HB)], axis=0)


def _merge_heads(x):
    return jnp.concatenate([x[j] for j in range(SCAN_HB)], axis=1)


def _scan_fwd(group_fn, name, arrs, heads, n_kept):
    s = arrs[0][0].shape[0]
    n_chunks = s // CHUNK
    n_in = len(arrs)

    def body(*refs):
        y_ref, st_ref = refs[n_in:n_in + 2]
        kept_refs, s_scr = refs[n_in + 2:-1], refs[-1]

        @pl.when(pl.program_id(1) == 0)
        def _():
            s_scr[...] = jnp.zeros_like(s_scr)

        s0 = s_scr[...]
        st_ref[...] = s0
        y, s1, *kept = group_fn(s0, *[_split_heads(r[...]) for r in refs[:n_in]])
        y_ref[...] = _merge_heads(y)
        s_scr[...] = s1
        for ref, val in zip(kept_refs, kept):
            ref[...] = val

    per_chunk = pl.BlockSpec((SCAN_HB, None, LANES, LANES), lambda h, n: (h, n, 0, 0))
    res = pl.pallas_call(
        body, grid=(heads // SCAN_HB, n_chunks), name=name,
        in_specs=_scan_specs(arrs, n_chunks, False),
        out_specs=[pl.BlockSpec((CHUNK, SCAN_HB * LANES), lambda h, n: (n, h))] + [per_chunk] * (1 + n_kept),
        out_shape=[SDS((s, heads * LANES), F32)] + [SDS((heads, n_chunks, LANES, LANES), F32)] * (1 + n_kept),
        scratch_shapes=[pltpu.VMEM((SCAN_HB, LANES, LANES), F32)],
        compiler_params=pltpu.CompilerParams(dimension_semantics=("arbitrary", "arbitrary")),
    )(*[a for a, _ in arrs])
    return res[0], res[1:]


def _scan_bwd(group_fn, name, arrs, kept, dy, heads):
    s = arrs[0][0].shape[0]
    n_chunks = s // CHUNK
    n_in, n_kept = len(arrs), len(kept)

    def body(*refs):
        kept_vals = [r[...] for r in refs[n_in:n_in + n_kept]]
        dy_ref = refs[n_in + n_kept]
        d_refs = refs[n_in + n_kept + 1:2 * n_in + n_kept + 1]
        ds_scr = refs[-1]

        @pl.when(pl.program_id(1) == 0)
        def _():
            ds_scr[...] = jnp.zeros_like(ds_scr)

        def fn(s0, *ins):
            return group_fn(s0, *ins, *kept_vals[1:])[:2]

        _, vjp = jax.vjp(fn, kept_vals[0], *[_split_heads(r[...]) for r in refs[:n_in]])
        grads = vjp((_split_heads(dy_ref[...]), ds_scr[...]))
        ds_scr[...] = grads[0]
        for ref, g in zip(d_refs, grads[1:]):
            ref[...] = _merge_heads(g)

    rev = pl.BlockSpec((CHUNK, SCAN_HB * LANES), lambda h, n: (n_chunks - 1 - n, h))
    per_chunk = pl.BlockSpec((SCAN_HB, None, LANES, LANES), lambda h, n: (h, n_chunks - 1 - n, 0, 0))
    return pl.pallas_call(
        body, grid=(heads // SCAN_HB, n_chunks), name=name,
        in_specs=_scan_specs(arrs, n_chunks, True) + [per_chunk] * n_kept + [rev],
        out_specs=[rev] * n_in,
        out_shape=[SDS((s, heads * LANES), F32)] * n_in,
        scratch_shapes=[pltpu.VMEM((SCAN_HB, LANES, LANES), F32)],
        compiler_params=pltpu.CompilerParams(dimension_semantics=("arbitrary", "arbitrary")),
    )(*[a for a, _ in arrs], *kept, dy)


def _col_spec(tr, width, cb):
    return pl.BlockSpec((tr, width), lambda i: (i, cb))


def _whole(p):
    return pl.BlockSpec(p.shape, lambda i: (0,) * p.ndim)


def _row_fwd(fn, name, tiles, params, outs, tr):
    rows = tiles[0][0].shape[0]
    nt, npar = len(tiles), len(params)

    def body(*refs):
        vals = [r[...].astype(F32) for r in refs[:nt + npar]]
        for ref, o in zip(refs[nt + npar:], fn(*vals)):
            ref[...] = o.astype(ref.dtype)

    return pl.pallas_call(
        body, grid=(rows // tr,), name=name,
        in_specs=[_col_spec(tr, w, cb) for _, w, cb in tiles] + [_whole(p) for p in params],
        out_specs=[_col_spec(tr, w, 0) for w, _ in outs],
        out_shape=[SDS((rows, w), dt) for w, dt in outs],
        compiler_params=pltpu.CompilerParams(dimension_semantics=("arbitrary",), vmem_limit_bytes=VMEM_LIMIT),
    )(*[a for a, _, _ in tiles], *params)


def _row_bwd(fn, name, tiles, params, cts, tr, want_tiles=None):
    rows = tiles[0][0].shape[0]
    nt, npar = len(tiles), len(params)
    want = list(range(nt)) if want_tiles is None else list(want_tiles)
    flat_cts = [c for group in cts for c in group]
    n_ct = len(flat_cts)

    def body(*refs):
        vals = [r[...].astype(F32) for r in refs[:nt + npar]]
        ct_refs = refs[nt + npar:nt + npar + n_ct]
        out_refs = refs[nt + npar + n_ct:]
        ct_vals, at = [], 0
        for group in cts:
            total = ct_refs[at][...].astype(F32)
            for r in ct_refs[at + 1:at + len(group)]:
                total = total + r[...].astype(F32)
            ct_vals.append(total)
            at += len(group)
        _, vjp = jax.vjp(lambda *a: tuple(fn(*a)), *vals)
        grads = vjp(tuple(ct_vals))
        for ref, t in zip(out_refs[:len(want)], want):
            ref[...] = grads[t]
        first = pl.program_id(0) == 0
        for ref, g in zip(out_refs[len(want):], grads[nt:]):
            @pl.when(first)
            def _(ref=ref, g=g):
                ref[...] = g

            @pl.when(jnp.logical_not(first))
            def _(ref=ref, g=g):
                ref[...] += g

    res = pl.pallas_call(
        body, grid=(rows // tr,), name=name,
        in_specs=[_col_spec(tr, w, cb) for _, w, cb in tiles] + [_whole(p) for p in params]
        + [_col_spec(tr, w, cb) for _, w, cb in flat_cts],
        out_specs=[_col_spec(tr, tiles[t][1], 0) for t in want] + [_whole(p) for p in params],
        out_shape=[SDS((rows, tiles[t][1]), F32) for t in want] + [SDS(p.shape, F32) for p in params],
        compiler_params=pltpu.CompilerParams(dimension_semantics=("arbitrary",), vmem_limit_bytes=VMEM_LIMIT),
    )(*[a for a, _, _ in tiles], *params, *[a for a, _, _ in flat_cts])
    return res[:len(want)], res[len(want):]


def _col_fwd(fn, name, x, first_block, n_blocks, params):
    rows = x.shape[0]

    def body(*refs):
        refs[-1][...] = fn(*[r[...] for r in refs[:-1]])

    return pl.pallas_call(
        body, grid=(n_blocks,), name=name,
        in_specs=[pl.BlockSpec((rows, LANES), lambda j: (0, first_block + j))]
        + [pl.BlockSpec((p.shape[0], LANES), lambda j: (0, j)) for p in params],
        out_specs=pl.BlockSpec((rows, LANES), lambda j: (0, j)),
        out_shape=SDS((rows, n_blocks * LANES), F32),
        compiler_params=pltpu.CompilerParams(dimension_semantics=("arbitrary",), vmem_limit_bytes=VMEM_LIMIT),
    )(x, *params)


def _col_bwd(fn, name, x, first_block, n_blocks, params, dys):
    rows = x.shape[0]
    npar, nd = len(params), len(dys)
    starts = [sum(t.shape[1] for t in dys[:i]) // LANES for i in range(nd + 1)]

    def body(*refs):
        vals = [r[...] for r in refs[:1 + npar]]
        j = pl.program_id(0)
        dy = refs[1 + npar][...]
        for i in range(1, nd):
            dy = jnp.where(j >= starts[i], refs[1 + npar + i][...], dy)
        _, vjp = jax.vjp(fn, *vals)
        grads = vjp(dy)
        for ref, g in zip(refs[1 + npar + nd:], grads):
            ref[...] = g.astype(ref.dtype)

    def piece(i):
        last = starts[i + 1] - starts[i] - 1
        return pl.BlockSpec((rows, LANES), lambda j: (0, jnp.clip(j - starts[i], 0, last)))

    pspecs = [pl.BlockSpec((p.shape[0], LANES), lambda j: (0, j)) for p in params]
    blk = pl.BlockSpec((rows, LANES), lambda j: (0, j))
    res = pl.pallas_call(
        body, grid=(n_blocks,), name=name,
        in_specs=[pl.BlockSpec((rows, LANES), lambda j: (0, first_block + j))] + pspecs + [piece(i) for i in range(nd)],
        out_specs=[blk] + pspecs,
        out_shape=[SDS((rows, n_blocks * LANES), BF16)] + [SDS(p.shape, F32) for p in params],
        compiler_params=pltpu.CompilerParams(dimension_semantics=("arbitrary",), vmem_limit_bytes=VMEM_LIMIT),
    )(x, *params, *dys)
    return res[0], res[1:]


def _conv_fn(x, w):
    acc = x * w[3:4, :]
    for j in range(3):
        acc = acc + shift_rows(x, 3 - j) * w[j:j + 1, :]
    return _silu(acc)


def _lerp_fn(x, mu):
    return x + (shift_rows(x, 1) - x) * mu[0:1, :]


def _seg_sum(x, width):
    if width == LANES:
        return jnp.sum(x, axis=1, keepdims=True)
    lo = lax.broadcasted_iota(jnp.int32, x.shape, 1) < width
    s0 = jnp.sum(jnp.where(lo, x, 0.0), axis=1, keepdims=True)
    s1 = jnp.sum(jnp.where(lo, 0.0, x), axis=1, keepdims=True)
    return jnp.where(lo, s0, s1)


def _per_block(fn, *xs):
    n = xs[0].shape[1] // LANES
    return jnp.concatenate([fn(*[x[:, LANES * b:LANES * (b + 1)] for x in xs]) for b in range(n)], axis=1)


def _head_expand(col0):
    r = lax.broadcasted_iota(jnp.int32, (LANES, DN_WIDTH), 0)
    c = lax.shift_right_logical(lax.broadcasted_iota(jnp.int32, (LANES, DN_WIDTH), 1), 7)
    return jnp.where(r == c + col0, 1.0, 0.0)


def _dn_pre_fn(cq, ck, gates, a_log, dt_bias):
    l2 = lambda x: x * lax.rsqrt(_seg_sum(x * x, LANES) + 1e-6)
    qh = _per_block(l2, cq) * (LANES ** -0.5)
    kh = _per_block(l2, ck)
    g = -jnp.exp(a_log) * _softplus(gates + dt_bias)
    gb = _select_cols(g, _head_expand(0))
    bb = _select_cols(_sigmoid(gates), _head_expand(DN_HEADS))
    return qh, kh, gb, bb, _cumsum_rows(gb)


def _dn_post_fn(o, z, nw):
    def one(ob, zb):
        return ob * lax.rsqrt(_seg_sum(ob * ob, LANES) * (1.0 / LANES) + RMS_EPS) * nw * _silu(zb)
    return (_per_block(one, o, z),)


def _rw_pre_fn(pr, pk, pv, pwa, pg, w0, a0, k_k, k_a, w2p, a2p, g2):
    log_w = -_softplus(-(w0 + mm(jnp.tanh(pwa), w2p))) - 0.5
    lw = -jnp.exp(log_w)
    a = _sigmoid(a0 + mm(pwa, a2p))
    gate = mm(_sigmoid(pg), g2)
    kk = pk * k_k
    kk = _per_block(lambda x: x / jnp.maximum(jnp.sqrt(_seg_sum(x * x, RW_HEAD)), 1e-12), kk)
    k = pk * (1.0 + (a - 1.0) * k_a)
    return pr, lw, k, pv, kk * a, -kk, gate, _cumsum_rows(lw)


def _rw_post_fn(y, r, k, v, gate, ln_w, ln_b, r_k):
    def one(yb, rb, kb, vb, gb, wb, bb, rkb):
        d = yb - _seg_sum(yb, RW_HEAD) * (1.0 / RW_HEAD)
        var = _seg_sum(d * d, RW_HEAD) * (1.0 / RW_HEAD)
        yn = d * lax.rsqrt(var + RW_GN_EPS) * wb + bb
        return (yn + _seg_sum(rb * kb * rkb, RW_HEAD) * vb) * gb
    return (_per_block(one, y, r, k, v, gate, ln_w, ln_b, r_k),)


def _rms_fn(h, w):
    return (h * lax.rsqrt(jnp.mean(h * h, axis=1, keepdims=True) + RMS_EPS) * w,)


def _xattn_fn(q, k, v):
    outs = []
    for h in range(XA_HEADS):
        sl = slice(LANES * h, LANES * (h + 1))
        s = mm(q[:, sl], k[:, sl], "nt") * (LANES ** -0.5)
        e = jnp.exp(s - jnp.max(s, axis=1, keepdims=True))
        outs.append(mm(e / jnp.sum(e, axis=1, keepdims=True), v[:, sl]))
    return (jnp.concatenate(outs, axis=1),)


def _fit(tile, dim):
    best = [t for t in range(LANES, min(tile, dim) + 1, LANES) if dim % t == 0]
    assert best, (tile, dim)
    return best[-1]


def _matmul(name, a, b, mode, out_dtypes, epilogue=None, extras=(), tm=1024, tn=1024, tk=2048, after=None):
    if mode == "tn":
        (k_dim, m), n = a.shape, b.shape[1]
    else:
        (m, k_dim), n = a.shape, (b.shape[1] if mode == "nn" else b.shape[0])
    tm, tn, tk = _fit(tm, m), _fit(tn, n), _fit(tk, k_dim)
    nk = k_dim // tk
    a_spec = (pl.BlockSpec((tk, tm), lambda i, j, k: (k, i)) if mode == "tn"
              else pl.BlockSpec((tm, tk), lambda i, j, k: (i, k)))
    b_spec = (pl.BlockSpec((tn, tk), lambda i, j, k: (j, k)) if mode == "nt"
              else pl.BlockSpec((tk, tn), lambda i, j, k: (k, j)))
    o_spec = pl.BlockSpec((tm, tn), lambda i, j, k: (i, j))
    n_ex, n_out = len(extras), len(out_dtypes)
    ties = [] if after is None else [after]

    def finish(total, rest):
        ex = [r[...].astype(F32) for r in rest[:n_ex]]
        res = epilogue(total, *ex) if epilogue else (total,)
        for ref, o in zip(rest[n_ex + len(ties):n_ex + len(ties) + n_out], res):
            ref[...] = o.astype(ref.dtype)

    def body_single(a_ref, b_ref, *rest):
        finish(_raw_dot(a_ref[...], b_ref[...], mode, False), rest)

    def body_acc(a_ref, b_ref, *rest):
        acc = rest[-1]
        k = pl.program_id(2)

        @pl.when(k == 0)
        def _():
            acc[...] = jnp.zeros_like(acc)

        acc[...] += _raw_dot(a_ref[...], b_ref[...], mode, False)

        @pl.when(k == nk - 1)
        def _():
            finish(acc[...], rest)

    res = pl.pallas_call(
        body_single if nk == 1 else body_acc, grid=(m // tm, n // tn, nk), name=name,
        in_specs=[a_spec, b_spec] + [o_spec] * n_ex + [pl.BlockSpec((8, LANES), lambda i, j, k: (0, 0))] * len(ties),
        out_specs=[o_spec] * n_out,
        out_shape=[SDS((m, n), dt) for dt in out_dtypes],
        scratch_shapes=[] if nk == 1 else [pltpu.VMEM((tm, tn), F32)],
        compiler_params=pltpu.CompilerParams(dimension_semantics=("parallel", "parallel", "arbitrary"),
                                             vmem_limit_bytes=VMEM_LIMIT),
    )(a, b, *extras, *ties)
    return res


def _matmul_norm_bwd(name, a, b, mode, h, w, dres, after=None, tm=512, tk=1024):
    m, n = h.shape
    k_dim = a.shape[1]
    tm, tk = _fit(tm, m), _fit(tk, k_dim)
    nk = k_dim // tk
    ties = [] if after is None else [after]
    a_spec = pl.BlockSpec((tm, tk), lambda i, k: (i, k))
    b_spec = pl.BlockSpec((n, tk), lambda i, k: (0, k)) if mode == "nt" else pl.BlockSpec((tk, n), lambda i, k: (k, 0))
    row = pl.BlockSpec((tm, n), lambda i, k: (i, 0))
    w_spec = pl.BlockSpec((1, n), lambda i, k: (0, 0))

    def body(a_ref, b_ref, h_ref, w_ref, dres_ref, *rest):
        dh_ref, dw_ref, acc = rest[len(ties):]
        i, k = pl.program_id(0), pl.program_id(1)

        @pl.when(k == 0)
        def _():
            acc[...] = jnp.zeros_like(acc)

        acc[...] += _raw_dot(a_ref[...], b_ref[...], mode, False)

        @pl.when(k == nk - 1)
        def _():
            _, vjp = jax.vjp(_rms_res_fn, h_ref[...], w_ref[...])
            dh, dw = vjp((acc[...], dres_ref[...]))
            dh_ref[...] = dh

            @pl.when(i == 0)
            def _():
                dw_ref[...] = dw

            @pl.when(i != 0)
            def _():
                dw_ref[...] += dw

    return pl.pallas_call(
        body, grid=(m // tm, nk), name=name,
        in_specs=[a_spec, b_spec, row, w_spec, row] + [pl.BlockSpec((8, LANES), lambda i, k: (0, 0))] * len(ties),
        out_specs=[row, w_spec],
        out_shape=[SDS((m, n), F32), SDS((1, n), F32)],
        scratch_shapes=[pltpu.VMEM((tm, n), F32)],
        compiler_params=pltpu.CompilerParams(dimension_semantics=("arbitrary", "arbitrary"),
                                             vmem_limit_bytes=VMEM_LIMIT),
    )(a, b, h, w, dres, *ties)


def _loss_call(h, target, w, tr=256):
    rows, d = h.shape

    def fn(hv, wv, tv):
        y = _rms_fn(hv, wv)[0]
        return 0.5 * jnp.sum(jnp.mean(jnp.square(y - tv), axis=1, keepdims=True), axis=0, keepdims=True)

    def body(h_ref, t_ref, w_ref, loss_ref, dh_ref, dw_ref):
        tv = t_ref[...]
        val, vjp = jax.vjp(lambda hv, wv: fn(hv, wv, tv), h_ref[...], w_ref[...])
        dh, dw = vjp(jnp.ones((1, 1), F32))
        dh_ref[...] = dh
        first = pl.program_id(0) == 0

        @pl.when(first)
        def _():
            loss_ref[...] = jnp.broadcast_to(val, loss_ref.shape)
            dw_ref[...] = dw

        @pl.when(jnp.logical_not(first))
        def _():
            loss_ref[...] += jnp.broadcast_to(val, loss_ref.shape)
            dw_ref[...] += dw

    return pl.pallas_call(
        body, grid=(rows // tr,), name="loss_head",
        in_specs=[_col_spec(tr, d, 0), _col_spec(tr, d, 0), _whole(w)],
        out_specs=[pl.BlockSpec((8, LANES), lambda i: (0, 0)), _col_spec(tr, d, 0), _whole(w)],
        out_shape=[SDS((8, LANES), F32), SDS((rows, d), F32), SDS(w.shape, F32)],
        compiler_params=pltpu.CompilerParams(dimension_semantics=("arbitrary",), vmem_limit_bytes=VMEM_LIMIT),
    )(h, target, w)


def _adamw_vals(w, g, m, v):
    m = ADAM_B1 * m + (1.0 - ADAM_B1) * g
    v = ADAM_B2 * v + (1.0 - ADAM_B2) * jnp.square(g)
    m_hat = m / (1.0 - ADAM_B1 ** ADAM_STEP)
    v_hat = v / (1.0 - ADAM_B2 ** ADAM_STEP)
    delta = -ADAM_LR * (m_hat / (jnp.sqrt(v_hat) + ADAM_EPS) + ADAM_WD * w)
    return delta, m, v


def _sum_adamw(name, parts, w, m, v):
    r, c = w.shape
    n_parts = parts.shape[0]
    budget = 6 * 1024 * 1024
    tr, tc = r, c
    for cand in (512, 256, 128, 64, 32, 16, 8):
        if r % cand == 0 and n_parts * cand * c * 4 <= budget:
            tr = cand
            break
    if n_parts * tr * c * 4 > budget:
        tc = max(t for t in range(LANES, c + 1, LANES) if c % t == 0 and n_parts * r * t * 4 <= budget)

    def body(p_ref, w_ref, m_ref, v_ref, g_ref, d_ref, m2_ref, v2_ref):
        g = p_ref[0].astype(F32)
        for s in range(1, n_parts):
            g = g + p_ref[s].astype(F32)
        g_ref[...] = g
        d_ref[...], m2_ref[...], v2_ref[...] = _adamw_vals(w_ref[...], g, m_ref[...], v_ref[...])

    blk = pl.BlockSpec((tr, tc), lambda i: (i, 0)) if tc == c else pl.BlockSpec((tr, tc), lambda i: (0, i))
    parts_blk = (pl.BlockSpec((n_parts, tr, tc), lambda i: (0, i, 0)) if tc == c
                 else pl.BlockSpec((n_parts, tr, tc), lambda i: (0, 0, i)))
    return pl.pallas_call(
        body, grid=(r // tr if tc == c else c // tc,), name=name,
        in_specs=[parts_blk, blk, blk, blk],
        out_specs=[blk] * 4, out_shape=[SDS((r, c), F32)] * 4,
        compiler_params=pltpu.CompilerParams(dimension_semantics=("arbitrary",), vmem_limit_bytes=VMEM_LIMIT),
    )(parts, w, m, v)


def _peers():
    x, y, c = lax.axis_index("x"), lax.axis_index("y"), lax.axis_index("c")
    peers = []
    for k in range(1, N_DEV):
        px = 1 - x if k & 4 else x
        py = 1 - y if k & 2 else y
        pc = 1 - c if k & 1 else c
        peers.append(((px, py, pc), 4 * px + 2 * py + pc))
    return 4 * x + 2 * y + c, peers


def _slot(ref, idx, cols):
    if cols is None:
        return ref.at[idx]
    return ref.at[:, pl.ds(pl.multiple_of(idx * cols, LANES), cols)]


def _gather_two_level(name, srcs, dsts):
    n = len(srcs)
    dst_cols = [c for _, _, c in dsts]

    def body(*refs):
        src_refs, out_refs = refs[:n], refs[n:2 * n]
        send_sems, recv_sems, local_sems = refs[2 * n:]
        x, y, c = lax.axis_index("x"), lax.axis_index("y"), lax.axis_index("c")
        index = lambda px, py, pc: 4 * px + 2 * py + pc
        me, sibling = index(x, y, c), (x, y, 1 - c)
        chips = [(x, 1 - y), (1 - x, y), (1 - x, 1 - y)]

        def copy(a, k, src, block, to):
            return pltpu.make_async_remote_copy(
                src_ref=src, dst_ref=_slot(out_refs[a], block, dst_cols[a]),
                send_sem=send_sems.at[a, k], recv_sem=recv_sems.at[a, k],
                device_id=to, device_id_type=pl.DeviceIdType.MESH)

        local, first, passed = [], [], []
        for a in range(n):
            cp = pltpu.make_async_copy(src_refs[a], _slot(out_refs[a], me, dst_cols[a]), local_sems.at[a])
            cp.start()
            local.append(cp)
            first.append(copy(a, 0, src_refs[a], me, sibling))
            first += [copy(a, 1 + j, src_refs[a], me, (*chip, c)) for j, chip in enumerate(chips)]
        for cp in first:
            cp.start()
        for a in range(n):
            for j, chip in enumerate(chips):
                block = index(*chip, c)
                arrived = _slot(out_refs[a], block, dst_cols[a])
                copy(a, 1 + j, arrived, block, (*chip, c)).wait_recv()
                passed.append(copy(a, 4 + j, arrived, block, sibling))
                passed[-1].start()
        for a in range(n):
            copy(a, 0, src_refs[a], index(x, y, 1 - c), sibling).wait_recv()
            for j, chip in enumerate(chips):
                block = index(*chip, 1 - c)
                copy(a, 4 + j, src_refs[a], block, sibling).wait_recv()
        for cp in first + passed:
            cp.wait_send()
        for cp in local:
            cp.wait()

    any_spec = pl.BlockSpec(memory_space=pl.ANY)
    return pl.pallas_call(
        body, name=name,
        in_specs=[any_spec] * n, out_specs=[any_spec] * n,
        out_shape=[SDS(shape, dt) for shape, dt, _ in dsts],
        scratch_shapes=_exchange_sems(n),
    )(*[a for a, _ in srcs])


_HBM = pl.BlockSpec(memory_space=pltpu.HBM)
_SEM = pl.BlockSpec(memory_space=pltpu.SEMAPHORE)
_EFFECT = pltpu.SideEffectType.DATAFLOW_SIDE_EFFECTING


def _split_copies(src_cols, dst_cols, gather, chips, src_refs, land_refs, send_sems, recv_sems, landings):
    me, peers = _peers()
    if chips:
        me, peers = me // 2, [(pos, idx // 2) for k, (pos, idx) in enumerate(peers) if (k + 1) in (2, 4, 6)]
    n, width = len(src_cols), len(peers)
    remote, local = [], []
    for a, (s_cols, d_cols) in enumerate(zip(src_cols, dst_cols)):
        mine = src_refs[a] if gather else _slot(src_refs[a], me, s_cols)
        local.append(pltpu.make_async_copy(mine, _slot(land_refs[a], me, d_cols), send_sems.at[n * width + a]))
        for k, (pos, idx) in enumerate(peers):
            blk = src_refs[a] if gather else _slot(src_refs[a], idx, s_cols)
            remote.append(pltpu.make_async_remote_copy(
                src_ref=blk, dst_ref=_slot(land_refs[a], idx if landings else me, d_cols),
                send_sem=send_sems.at[a * width + k], recv_sem=recv_sems.at[a * width + k],
                device_id=pos, device_id_type=pl.DeviceIdType.MESH))
    return remote, local


def _exchange_start(name, srcs, dsts, gather, after, chips=False):
    n = len(srcs)
    src_cols, dst_cols = [c for _, c in srcs], [c for _, _, c in dsts]
    width = 3 if chips else N_DEV - 1

    def body(*refs):
        src_refs, land_refs = refs[:n], refs[n:2 * n]
        send_sems, recv_sems = refs[2 * n + 1:2 * n + 3]
        token = refs[-1]
        remote, local = _split_copies(src_cols, dst_cols, gather, chips, src_refs, land_refs, send_sems, recv_sems,
                                      False)
        for cp in remote + local:
            cp.start()
        token[...] = jnp.zeros_like(token)

    hbm = lambda a: pltpu.with_memory_space_constraint(a, pltpu.HBM)
    lands = [hbm(lax.empty(shape, dt)) for shape, dt, _ in dsts]
    res = pl.pallas_call(
        body, name=name,
        out_shape=(pltpu.SemaphoreType.DMA((n * (width + 1),)), pltpu.SemaphoreType.DMA((n * width,)),
                   *[pltpu.HBM(a.shape, a.dtype) for a, _ in srcs], *[pltpu.HBM(a.shape, a.dtype) for a in lands],
                   SDS((8, LANES), F32)),
        in_specs=[_HBM] * (2 * n) + [pl.BlockSpec(memory_space=pl.ANY)],
        out_specs=(_SEM, _SEM, *[_HBM] * (2 * n), pl.BlockSpec(memory_space=pltpu.VMEM)),
        input_output_aliases={i: 2 + i for i in range(2 * n)},
        compiler_params=pltpu.CompilerParams(has_side_effects=_EFFECT),
    )(*[hbm(a) for a, _ in srcs], *lands, after)
    handle = (res[0], res[1], res[2:2 + n], res[2 + n:2 + 2 * n], src_cols, dst_cols, gather, chips)
    return handle, res[-1]


def _exchange_wait(name, handle, after):
    send_sems, recv_sems, src_thru, land_thru, src_cols, dst_cols, gather, chips = handle
    n = len(src_thru)

    def body(*refs):
        src_refs, land_refs = refs[:n], refs[n:2 * n]
        s_sems, r_sems = refs[2 * n:2 * n + 2]
        remote, local = _split_copies(src_cols, dst_cols, gather, chips, src_refs, land_refs, s_sems, r_sems, True)
        for cp in remote:
            cp.wait_send()
            cp.wait_recv()
        for cp in local:
            cp.wait()

    res = pl.pallas_call(
        body, name=name,
        out_shape=tuple(pltpu.HBM(a.shape, a.dtype) for a in (*src_thru, *land_thru)),
        in_specs=[_HBM] * (2 * n) + [_SEM, _SEM, pl.BlockSpec(memory_space=pl.ANY)],
        out_specs=tuple([_HBM] * (2 * n)),
        input_output_aliases={i: i for i in range(2 * n)},
        compiler_params=pltpu.CompilerParams(has_side_effects=_EFFECT),
    )(*src_thru, *land_thru, send_sems, recv_sems, after)
    return res[n:]


def _pair_swap(name, arrs):
    n = len(arrs)

    def body(*refs):
        src_refs, out_refs = refs[:n], refs[n:2 * n]
        send_sems, recv_sems = refs[2 * n:]
        x, y, c = lax.axis_index("x"), lax.axis_index("y"), lax.axis_index("c")
        copies = [pltpu.make_async_remote_copy(
            src_ref=src_refs[a].at[:, 1 - c], dst_ref=out_refs[a], send_sem=send_sems.at[a], recv_sem=recv_sems.at[a],
            device_id=(x, y, 1 - c), device_id_type=pl.DeviceIdType.MESH) for a in range(n)]
        for cp in copies:
            cp.start()
        for cp in copies:
            cp.wait()

    any_spec = pl.BlockSpec(memory_space=pl.ANY)
    return pl.pallas_call(
        body, name=name,
        in_specs=[any_spec] * n, out_specs=[any_spec] * n,
        out_shape=[SDS((a.shape[0],) + a.shape[2:], a.dtype) for a in arrs],
        scratch_shapes=[pltpu.SemaphoreType.DMA((n,)), pltpu.SemaphoreType.DMA((n,))],
    )(*arrs)


def _pair_add(name, mine, theirs):
    four, _, r, c = mine.shape
    tr = r
    for cand in (512, 256, 128, 64, 32, 16, 8):
        if r % cand == 0:
            tr = cand
            break
    tc = max(t for t in range(LANES, c + 1, LANES) if c % t == 0 and (t == LANES or 2 * tr * t * 4 <= 4 * 1024 * 1024))

    def body(m_ref, t_ref, o_ref):
        core = lax.axis_index("c")
        both = m_ref[...].astype(F32)
        own = jnp.where(core == 0, both[0], both[1])
        o_ref[...] = (own + t_ref[...].astype(F32)).astype(o_ref.dtype)

    return pl.pallas_call(
        body, grid=(four, r // tr, c // tc), name=name,
        in_specs=[pl.BlockSpec((None, 2, tr, tc), lambda i, j, k: (i, 0, j, k)),
                  pl.BlockSpec((None, tr, tc), lambda i, j, k: (i, j, k))],
        out_specs=pl.BlockSpec((None, tr, tc), lambda i, j, k: (i, j, k)),
        out_shape=SDS(theirs.shape, theirs.dtype),
        compiler_params=pltpu.CompilerParams(dimension_semantics=("arbitrary",) * 3, vmem_limit_bytes=VMEM_LIMIT),
    )(mine, theirs)


def _my_index():
    return 4 * lax.axis_index("x") + 2 * lax.axis_index("y") + lax.axis_index("c")


def _two_level_copies(stage, dst_cols, src_refs, land_refs, send_sems, recv_sems, landings):
    x, y, c = lax.axis_index("x"), lax.axis_index("y"), lax.axis_index("c")

    def pos(k):
        return (1 - x if k & 4 else x, 1 - y if k & 2 else y, 1 - c if k & 1 else c)

    def idx(k):
        px, py, pc = pos(k)
        return 4 * px + 2 * py + pc

    out = []
    for a, cols in enumerate(dst_cols):
        if stage == 1:
            for i, k in enumerate((1, 2, 4, 6)):
                out.append(pltpu.make_async_remote_copy(
                    src_ref=src_refs[a], dst_ref=_slot(land_refs[a], idx(k) if landings else idx(0), cols),
                    send_sem=send_sems.at[4 * a + i], recv_sem=recv_sems.at[4 * a + i],
                    device_id=pos(k), device_id_type=pl.DeviceIdType.MESH))
        else:
            for i, k in enumerate((2, 4, 6)):
                out.append(pltpu.make_async_remote_copy(
                    src_ref=_slot(land_refs[a], idx(k), cols),
                    dst_ref=_slot(land_refs[a], idx(k ^ 1) if landings else idx(k), cols),
                    send_sem=send_sems.at[3 * a + i], recv_sem=recv_sems.at[3 * a + i],
                    device_id=pos(1), device_id_type=pl.DeviceIdType.MESH))
    return out


def _gather2_start(name, srcs, dsts, after):
    n = len(srcs)
    dst_cols = [c for _, _, c in dsts]

    def body(*refs):
        src_refs, land_refs = refs[:n], refs[n:2 * n]
        send_sems, recv_sems = refs[2 * n + 1:2 * n + 3]
        me = _my_index()
        for a in range(n):
            pltpu.make_async_copy(src_refs[a], _slot(land_refs[a], me, dst_cols[a]), send_sems.at[4 * n + a]).start()
        for cp in _two_level_copies(1, dst_cols, src_refs, land_refs, send_sems, recv_sems, False):
            cp.start()
        refs[-1][...] = jnp.zeros_like(refs[-1])

    hbm = lambda a: pltpu.with_memory_space_constraint(a, pltpu.HBM)
    lands = [hbm(lax.empty(shape, dt)) for shape, dt, _ in dsts]
    res = pl.pallas_call(
        body, name=name,
        out_shape=(pltpu.SemaphoreType.DMA((5 * n,)), pltpu.SemaphoreType.DMA((4 * n,)),
                   *[pltpu.HBM(a.shape, a.dtype) for a, _ in srcs], *[pltpu.HBM(a.shape, a.dtype) for a in lands],
                   SDS((8, LANES), F32)),
        in_specs=[_HBM] * (2 * n) + [pl.BlockSpec(memory_space=pl.ANY)],
        out_specs=(_SEM, _SEM, *[_HBM] * (2 * n), pl.BlockSpec(memory_space=pltpu.VMEM)),
        input_output_aliases={i: 2 + i for i in range(2 * n)},
        compiler_params=pltpu.CompilerParams(has_side_effects=_EFFECT),
    )(*[hbm(a) for a, _ in srcs], *lands, after)
    return (res[0], res[1], res[2:2 + n], res[2 + n:2 + 2 * n], dst_cols), res[-1]


def _gather2_pass(name, handle, after):
    send1, recv1, src_thru, land_thru, dst_cols = handle
    n = len(src_thru)

    def body(*refs):
        src_refs, land_refs = refs[:n], refs[n:2 * n]
        s1, r1 = refs[2 * n:2 * n + 2]
        send2, recv2 = refs[2 * n + 3:2 * n + 5]
        me = _my_index()
        for cp in _two_level_copies(1, dst_cols, src_refs, land_refs, s1, r1, True):
            cp.wait_send()
            cp.wait_recv()
        for a in range(n):
            pltpu.make_async_copy(src_refs[a], _slot(land_refs[a], me, dst_cols[a]), s1.at[4 * n + a]).wait()
        for cp in _two_level_copies(2, dst_cols, src_refs, land_refs, send2, recv2, False):
            cp.start()
        refs[-1][...] = jnp.zeros_like(refs[-1])

    res = pl.pallas_call(
        body, name=name,
        out_shape=(pltpu.SemaphoreType.DMA((3 * n,)), pltpu.SemaphoreType.DMA((3 * n,)),
                   *[pltpu.HBM(a.shape, a.dtype) for a in (*src_thru, *land_thru)], SDS((8, LANES), F32)),
        in_specs=[_HBM] * (2 * n) + [_SEM, _SEM, pl.BlockSpec(memory_space=pl.ANY)],
        out_specs=(_SEM, _SEM, *[_HBM] * (2 * n), pl.BlockSpec(memory_space=pltpu.VMEM)),
        input_output_aliases={i: 2 + i for i in range(2 * n)},
        compiler_params=pltpu.CompilerParams(has_side_effects=_EFFECT),
    )(*src_thru, *land_thru, send1, recv1, after)
    return (res[0], res[1], res[2:2 + n], res[2 + n:2 + 2 * n], dst_cols), res[-1]


def _gather2_wait(name, handle, after):
    send2, recv2, src_thru, land_thru, dst_cols = handle
    n = len(src_thru)

    def body(*refs):
        src_refs, land_refs = refs[:n], refs[n:2 * n]
        s2, r2 = refs[2 * n:2 * n + 2]
        for cp in _two_level_copies(2, dst_cols, src_refs, land_refs, s2, r2, True):
            cp.wait_send()
            cp.wait_recv()

    res = pl.pallas_call(
        body, name=name,
        out_shape=tuple(pltpu.HBM(a.shape, a.dtype) for a in (*src_thru, *land_thru)),
        in_specs=[_HBM] * (2 * n) + [_SEM, _SEM, pl.BlockSpec(memory_space=pl.ANY)],
        out_specs=tuple([_HBM] * (2 * n)),
        input_output_aliases={i: i for i in range(2 * n)},
        compiler_params=pltpu.CompilerParams(has_side_effects=_EFFECT),
    )(*src_thru, *land_thru, send2, recv2, after)
    return res[n:]


def _exchange_sems(n):
    return [pltpu.SemaphoreType.DMA((n, N_DEV - 1)), pltpu.SemaphoreType.DMA((n, N_DEV - 1)),
            pltpu.SemaphoreType.DMA((n,))]


def _rms_res_fn(h, w):
    return _rms_fn(h, w)[0], h


def _add_epilogue(acc, res):
    return (acc + res,)


def _gather_plan(shards):
    srcs, dsts = [], []
    for n, sh in shards.items():
        r, c = sh.shape
        srcs.append((sh, None))
        if SHARDED[n] and c % LANES == 0:
            dsts.append(((r, N_DEV * c), sh.dtype, c))
        else:
            dsts.append(((N_DEV, r, c), sh.dtype, None))
    return srcs, dsts, True


def _w_in_segments():
    out = []
    for j in range(N_DEV):
        lo, hi = W_IN_SHARD * j, W_IN_SHARD * (j + 1)
        for a, b in ((lo, min(hi, DN_COLS)), (max(lo, DN_COLS), hi)):
            if a < b:
                out.append((j, a - lo, b - lo, a if a < DN_COLS else a + RW_OFF - DN_COLS))
    return out


def _w_in_to_padded(shards, tc=512):
    _, _, cols = shards.shape

    def body(g_ref, o_ref):
        o_ref[...] = jnp.zeros_like(o_ref)
        for j, a, b, dst in _w_in_segments():
            o_ref[dst:dst + b - a, :] = g_ref[j, a:b, :]

    return pl.pallas_call(
        body, grid=(cols // tc,), name="w_in_to_padded",
        in_specs=[pl.BlockSpec((N_DEV, W_IN_SHARD, tc), lambda i: (0, 0, i))],
        out_specs=pl.BlockSpec((IN_PAD, tc), lambda i: (0, i)),
        out_shape=SDS((IN_PAD, cols), shards.dtype),
        compiler_params=pltpu.CompilerParams(dimension_semantics=("arbitrary",), vmem_limit_bytes=VMEM_LIMIT),
    )(shards)


def _w_in_grad_to_shards(gw, tc=512):
    _, cols = gw.shape

    def body(w_ref, o_ref):
        for j, a, b, dst in _w_in_segments():
            o_ref[j, a:b, :] = w_ref[dst:dst + b - a, :]

    return pl.pallas_call(
        body, grid=(cols // tc,), name="w_in_grad_to_shards",
        in_specs=[pl.BlockSpec((IN_PAD, tc), lambda i: (0, i))],
        out_specs=pl.BlockSpec((N_DEV, W_IN_SHARD, tc), lambda i: (0, 0, i)),
        out_shape=SDS((N_DEV, W_IN_SHARD, cols), gw.dtype),
        compiler_params=pltpu.CompilerParams(dimension_semantics=("arbitrary",), vmem_limit_bytes=VMEM_LIMIT),
    )(gw)


def _gather_finish(names, outs):
    full = {}
    for n, arr in zip(names, outs):
        if n == "w_in":
            full[n] = _w_in_to_padded(arr)
        elif arr.ndim == 2:
            full[n] = arr
        elif SHARDED[n]:
            full[n] = arr.transpose(1, 0, 2).reshape(arr.shape[1], -1)
        else:
            full[n] = arr.reshape(-1, arr.shape[2])
    return full


def _scatter_plan(grads):
    srcs, dsts = [], []
    for n, gr in grads.items():
        if gr.ndim == 3:
            srcs.append((gr, None))
            dsts.append((gr.shape, gr.dtype, None))
            continue
        rows, cols = gr.shape
        if not SHARDED[n]:
            r, c = rows // N_DEV, cols
            srcs.append((gr.reshape(N_DEV, r, c), None))
        else:
            r, c = rows, cols // N_DEV
            if c % LANES == 0:
                srcs.append((gr, c))
            else:
                srcs.append((gr.reshape(r, N_DEV, c).transpose(1, 0, 2), None))
        dsts.append(((N_DEV, r, c), gr.dtype, None))
    return srcs, dsts, False


def _local_step(x, u, mem, target, wt, late):
    d = D_MODEL
    g = {}
    wt = dict(wt)
    grp_a = ("w_out", "xa_wq", "xa_wk", "xa_wv", "xa_wo")
    grp_b = ("ffn_w1", "ffn_w2")
    plan = lambda names: _gather_plan({n: late[n] for n in names})[:2]
    handle_a, tok_a = _gather2_start("late_gather_a_start", *plan(grp_a), wt["w_in"])
    handle_w1, tok_b = _gather2_start("late_gather_w1_start", *plan(("ffn_w1",)), tok_a)
    handle_w2, tok_c = _gather2_start("late_gather_w2_start", *plan(("ffn_w2",)), tok_b)
    p = _matmul("in_proj", u, wt["w_in"], "nt", [F32], tn=1536, after=tok_c)[0]
    c = _col_fwd(_conv_fn, "dn_conv", p, 0, 24, [wt["dn_conv_w"]])
    handle_a, tok = _gather2_pass("late_gather_a_pass", handle_a, c)
    dn_pre_tiles = [(c, DN_WIDTH, 0), (c, DN_WIDTH, 1), (p, LANES, 32)]
    dn_pre_params = [wt["dn_a_log"], wt["dn_dt_bias"]]
    qh, kh, gb, bb, gcb = _row_fwd(_dn_pre_fn, "dn_pre", dn_pre_tiles, [dn_pre_params[0] + tok[0:1, :], dn_pre_params[1]],
                                   [(DN_WIDTH, F32)] * 5, CHUNK)
    dn_arrs = [(qh, 0), (kh, 0), (c, 16), (gb, 0), (bb, 0), (gcb, 0)]
    o, kept_dn = _scan_fwd(_gdn_group, "gdn_scan", dn_arrs, DN_HEADS, 1)
    dn_post_tiles = [(o, DN_WIDTH, 0), (p, DN_WIDTH, 3)]
    o_dn = _row_fwd(_dn_post_fn, "dn_post", dn_post_tiles, [wt["dn_norm_w"]], [(DN_WIDTH, BF16)], 256)[0]

    ps = _col_fwd(_lerp_fn, "rw_shift", p, RW_OFF // LANES, 26, [wt["rw_mu"]])
    rw_pre_tiles = [(ps, RW_WIDTH, 0), (ps, RW_WIDTH, 1), (ps, RW_WIDTH, 2), (ps, LANES, 24), (ps, LANES, 25)]
    rw_pre_params = [wt[n] for n in ("rw_w0", "rw_a0", "rw_k_k", "rw_k_a", "rw_w2", "rw_a2", "rw_g2")]
    r, lw, k, v, al, be, gate, gcw = _row_fwd(_rw_pre_fn, "rw_pre", rw_pre_tiles, rw_pre_params,
                                              [(RW_WIDTH, F32)] * 8, CHUNK)
    rw_arrs = [(r, 0), (lw, 0), (k, 0), (v, 0), (al, 0), (be, 0), (gcw, 0)]
    y, kept_rw = _scan_fwd(_rw_group, "rw_scan", rw_arrs, RW_WIDTH // LANES, 2)
    handle_w1, tok = _gather2_pass("late_gather_w1_pass", handle_w1, y)
    rw_post_tiles = [(t, RW_WIDTH, 0) for t in (y, r, k, v, gate)]
    rw_post_params = [wt["rw_ln_w"], wt["rw_ln_b"], wt["rw_r_k"]]
    o_rw = _row_fwd(_rw_post_fn, "rw_post", rw_post_tiles, [rw_post_params[0] + tok[0:1, 0:1]] + rw_post_params[1:],
                    [(RW_WIDTH, BF16)], 128)[0]
    o_cat = jnp.concatenate([o_dn, o_rw], axis=1)
    wt.update(_gather_finish(grp_a, _gather2_wait("late_gather_a_wait", handle_a, o_cat)))
    h1 = _matmul("out_proj", o_cat, wt["w_out"], "nn", [F32], _add_epilogue, (x,))[0]

    handle_w2, tok = _gather2_pass("late_gather_w2_pass", handle_w2, h1)
    hn = _row_fwd(_rms_fn, "xa_norm", [(h1, d, 0)], [wt["xa_norm_w"] + tok[0:1, 0:1]], [(d, BF16)], 256)[0]
    mn = _row_fwd(_rms_fn, "mem_norm", [(mem, d, 0)], [wt["mem_norm_w"]], [(d, BF16)], 256)[0]
    q = _matmul("xa_q", hn, wt["xa_wq"], "nn", [F32])[0]
    kx = _matmul("xa_k", mn, wt["xa_wk"], "nn", [F32])[0]
    vx = _matmul("xa_v", mn, wt["xa_wv"], "nn", [F32])[0]
    ao = _row_fwd(_xattn_fn, "xattn", [(q, XA_WIDTH, 0)], [kx, vx], [(XA_WIDTH, BF16)], 256)[0]
    h2 = _matmul("xa_o", ao, wt["xa_wo"], "nn", [F32], _add_epilogue, (h1,))[0]

    f = _row_fwd(_rms_fn, "ffn_norm", [(h2, d, 0)], [wt["ffn_norm_w"]], [(d, BF16)], 256)[0]
    wt.update(_gather_finish(("ffn_w1",), _gather2_wait("late_gather_w1_wait", handle_w1, f)))
    a, hid = _matmul("ffn_up", f, wt["ffn_w1"], "nn", [F32, BF16],
                     lambda acc: (acc, jnp.square(jnp.maximum(acc, 0.0))))
    wt.update(_gather_finish(("ffn_w2",), _gather2_wait("late_gather_w2_wait", handle_w2, hid)))
    h3 = _matmul("ffn_down", hid, wt["ffn_w2"], "nn", [F32], _add_epilogue, (h2,))[0]
    loss8, dh3, g["final_norm_w"] = _loss_call(h3, target, wt["final_norm_w"])

    da = _matmul("ffn_down_dx", dh3, wt["ffn_w2"], "nt", [BF16],
                 lambda acc, av: (acc * 2.0 * jnp.maximum(av, 0.0),), (a,))[0]
    g["ffn_w2"] = _matmul("ffn_down_dw", hid, dh3, "tn", [BF16])[0]
    g["ffn_w1"] = _matmul("ffn_up_dw", f, da, "tn", [BF16])[0]
    pending = {}
    plan = _scatter_plan({n: g.pop(n) for n in grp_b})
    pending[grp_b], tok = _exchange_start("late_grad_b_start", *plan, loss8)
    dh2, g["ffn_norm_w"] = _matmul_norm_bwd("ffn_up_dx", da, wt["ffn_w1"], "nt", h2, wt["ffn_norm_w"], dh3, tok)

    dao = _matmul("xa_o_dx", dh2, wt["xa_wo"], "nt", [F32])[0]
    g["xa_wo"] = _matmul("xa_o_dw", ao, dh2, "tn", [BF16])[0]
    (dq,), (dkx, dvx) = _row_bwd(_xattn_fn, "xattn_bwd", [(q, XA_WIDTH, 0)], [kx, vx], [[(dao, XA_WIDTH, 0)]], 256)
    dh1, g["xa_norm_w"] = _matmul_norm_bwd("xa_q_dx", dq, wt["xa_wq"], "nt", h1, wt["xa_norm_w"], dh2)
    g["xa_wq"] = _matmul("xa_q_dw", hn, dq, "tn", [BF16])[0]
    g["xa_wk"] = _matmul("xa_k_dw", mn, dkx, "tn", [BF16])[0]
    g["xa_wv"] = _matmul("xa_v_dw", mn, dvx, "tn", [BF16])[0]
    dmn = _matmul("xa_k_dx", dkx, wt["xa_wk"], "nt", [F32])[0]
    dmn = _matmul("xa_v_dx", dvx, wt["xa_wv"], "nt", [F32], _add_epilogue, (dmn,))[0]
    _, (g["mem_norm_w"],) = _row_bwd(_rms_fn, "mem_norm_bwd", [(mem, d, 0)], [wt["mem_norm_w"]],
                                     [[(dmn, d, 0)]], 256, want_tiles=())

    do_cat = _matmul("out_proj_dx", dh1, wt["w_out"], "nt", [F32])[0]
    g["w_out"] = _matmul("out_proj_dw", o_cat, dh1, "tn", [BF16])[0]

    plan = _scatter_plan({n: g.pop(n) for n in grp_a})
    pending[grp_a], tok = _exchange_start("late_grad_a_start", *plan, tok)
    (dy, dr1, dk1, dv1, dgate), (g["rw_ln_w"], g["rw_ln_b"], g["rw_r_k"]) = _row_bwd(
        _rw_post_fn, "rw_post_bwd", rw_post_tiles, [rw_post_params[0] + tok[0:1, 0:1]] + rw_post_params[1:],
        [[(do_cat, RW_WIDTH, 1)]], 128)
    dr2, dlw, dk2, dv2, dal, dbe, dgcw = _scan_bwd(_rw_group, "rw_scan_bwd", rw_arrs, kept_rw, dy,
                                                   RW_WIDTH // LANES)
    one = lambda t: [(t, RW_WIDTH, 0)]
    two = lambda s, t: [(s, RW_WIDTH, 0), (t, RW_WIDTH, 0)]
    d_ps, rw_pre_grads = _row_bwd(
        _rw_pre_fn, "rw_pre_bwd", rw_pre_tiles, rw_pre_params,
        [two(dr1, dr2), one(dlw), two(dk1, dk2), two(dv1, dv2), one(dal), one(dbe), one(dgate), one(dgcw)],
        CHUNK)
    for n, val in zip(("rw_w0", "rw_a0", "rw_k_k", "rw_k_a", "rw_w2", "rw_a2", "rw_g2"), rw_pre_grads):
        g[n] = val
    dp_rw, (g["rw_mu"],) = _col_bwd(_lerp_fn, "rw_shift_bwd", p, RW_OFF // LANES, 26, [wt["rw_mu"]], list(d_ps))

    (do, dz), (g["dn_norm_w"],) = _row_bwd(_dn_post_fn, "dn_post_bwd", dn_post_tiles, [wt["dn_norm_w"]],
                                           [[(do_cat, DN_WIDTH, 0)]], 256)
    dqh, dkh, dv_dn, dgb, dbb, dgcb = _scan_bwd(_gdn_group, "gdn_scan_bwd", dn_arrs, kept_dn, do, DN_HEADS)
    one = lambda t: [(t, DN_WIDTH, 0)]
    (dcq, dck, dgates), (g["dn_a_log"], g["dn_dt_bias"]) = _row_bwd(
        _dn_pre_fn, "dn_pre_bwd", dn_pre_tiles, dn_pre_params,
        [one(dqh), one(dkh), one(dgb), one(dbb), one(dgcb)], CHUNK)
    dp_qkv, (g["dn_conv_w"],) = _col_bwd(_conv_fn, "dn_conv_bwd", p, 0, 24, [wt["dn_conv_w"]], [dcq, dck, dv_dn])
    dp = jnp.concatenate([t.astype(BF16) for t in (dp_qkv, dz, dgates, dp_rw, jnp.zeros((x.shape[0], LANES), F32))],
                         axis=1)
    g["w_in"] = _matmul("in_proj_dw", dp, u, "tn", [BF16], tm=1536)[0]
    early = _logical_grads(g)
    blocks = []
    for src, cols in _scatter_plan({n: early.pop(n) for n in EARLY})[0]:
        if cols is not None:
            src = src.reshape(src.shape[0], N_DEV, cols).transpose(1, 0, 2)
        blocks.append(src.reshape((4, 2) + src.shape[1:]))
    sums = [_pair_add("early_grad_pair_add_%d" % i, mine, theirs)
            for i, (mine, theirs) in enumerate(zip(blocks, _pair_swap("early_grad_pair_swap", blocks)))]
    pending[EARLY], tok = _exchange_start("early_grad_start", [(t, None) for t in sums],
                                          [(t.shape, t.dtype, None) for t in sums], False, tok, chips=True)
    dx, early["mix_norm_w"] = _matmul_norm_bwd("in_proj_dx", dp, wt["w_in"], "nn", x, wt["mix_norm_w"], dh1, tok)
    return loss8, dx, early, pending, tok


WEIGHTS = ["mix_norm_w", "w_in", "dn_conv_w", "dn_a_log", "dn_dt_bias", "dn_norm_w", "rw_mu", "rw_w0", "rw_w2",
           "rw_a0", "rw_a2", "rw_g2", "rw_k_k", "rw_k_a", "rw_r_k", "rw_ln_w", "rw_ln_b", "w_out", "xa_norm_w",
           "mem_norm_w", "xa_wq", "xa_wk", "xa_wv", "xa_wo", "ffn_norm_w", "ffn_w1", "ffn_w2", "final_norm_w"]
SHARDED = {"w_in": False, "w_out": False, "xa_wq": False, "xa_wk": False, "xa_wv": False, "xa_wo": True,
           "ffn_w1": True, "ffn_w2": False, "dn_conv_w": True, "rw_w2": True, "rw_a2": True, "rw_g2": True}
BF16_PAYLOAD = ("w_in", "w_out", "xa_wq", "xa_wk", "xa_wv", "xa_wo", "ffn_w1", "ffn_w2")
REPLICATED = [n for n in WEIGHTS if n not in SHARDED]
EARLY = ("w_in", "dn_conv_w", "rw_w2", "rw_a2", "rw_g2")
RW_IN_COLS = IN_COLS - DN_COLS
W_IN_SHARD = IN_COLS // N_DEV


def _layout_weights(fw):
    wt = dict(fw)
    wt["dn_conv_w"] = jnp.pad(fw["dn_conv_w"], ((0, 4), (0, 0)))
    wt["dn_a_log"] = jnp.pad(fw["dn_a_log"], ((0, 0), (0, LANES - DN_HEADS)))
    wt["dn_dt_bias"] = jnp.pad(fw["dn_dt_bias"], ((0, 0), (0, LANES - DN_HEADS)))
    wt["rw_w2"] = jnp.pad(fw["rw_w2"], ((0, 64), (0, 0)))
    wt["rw_a2"] = jnp.pad(fw["rw_a2"], ((64, 0), (0, 0)))
    return wt


def _logical_grads(g):
    out = dict(g)
    out["w_in"] = _w_in_grad_to_shards(g["w_in"])
    out["dn_conv_w"] = g["dn_conv_w"][:4]
    out["dn_a_log"] = g["dn_a_log"][:, :DN_HEADS]
    out["dn_dt_bias"] = g["dn_dt_bias"][:, :DN_HEADS]
    out["rw_w2"] = g["rw_w2"][:64]
    out["rw_a2"] = g["rw_a2"][64:]
    return out


def _pack(vals):
    parts = []
    for v in vals:
        flat = v.reshape(-1)
        parts.append(jnp.pad(flat, (0, -flat.shape[0] % LANES)))
    flat = jnp.concatenate(parts)
    flat = jnp.pad(flat, (0, -flat.shape[0] % (8 * LANES)))
    return flat.reshape(-1, LANES)


def _unpack(packed, shapes):
    flat = packed.reshape(-1)
    out, at = [], 0
    for shp in shapes:
        size = math.prod(shp)
        out.append(flat[at:at + size].reshape(shp))
        at += size + (-size % LANES)
    return out


def kernel(x, mem, mix_norm_w, w_in, dn_conv_w, dn_a_log, dn_dt_bias, dn_norm_w, rw_mu, rw_w0, rw_w2, rw_a0, rw_a2, rw_g2, rw_k_k, rw_k_a, rw_r_k, rw_ln_w, rw_ln_b, w_out, xa_norm_w, mem_norm_w, xa_wq, xa_wk, xa_wv, xa_wo, ffn_norm_w, ffn_w1, ffn_w2, final_norm_w, loss_target, m_mix_norm_w, m_w_in, m_dn_conv_w, m_dn_a_log, m_dn_dt_bias, m_dn_norm_w, m_rw_mu, m_rw_w0, m_rw_w2, m_rw_a0, m_rw_a2, m_rw_g2, m_rw_k_k, m_rw_k_a, m_rw_r_k, m_rw_ln_w, m_rw_ln_b, m_w_out, m_xa_norm_w, m_mem_norm_w, m_xa_wq, m_xa_wk, m_xa_wv, m_xa_wo, m_ffn_norm_w, m_ffn_w1, m_ffn_w2, m_final_norm_w, v_mix_norm_w, v_w_in, v_dn_conv_w, v_dn_a_log, v_dn_dt_bias, v_dn_norm_w, v_rw_mu, v_rw_w0, v_rw_w2, v_rw_a0, v_rw_a2, v_rw_g2, v_rw_k_k, v_rw_k_a, v_rw_r_k, v_rw_ln_w, v_rw_ln_b, v_w_out, v_xa_norm_w, v_mem_norm_w, v_xa_wq, v_xa_wk, v_xa_wv, v_xa_wo, v_ffn_norm_w, v_ffn_w1, v_ffn_w2, v_final_norm_w):
    given = dict(locals())
    w = {n: given[n] for n in WEIGHTS}
    m = {n: given["m_" + n] for n in WEIGHTS}
    v = {n: given["v_" + n] for n in WEIGHTS}

    local = {n: (lambda t: t[0].T) if n == "w_in" else (lambda t: t[0]) for n in SHARDED}
    shards = {n: (local[n](w[n]).astype(BF16) if n in BF16_PAYLOAD else local[n](w[n])) for n in SHARDED}
    srcs, dsts, _ = _gather_plan({n: shards[n] for n in EARLY})
    handle, tok = _gather2_start("early_gather_start", srcs, dsts, mix_norm_w)
    u = _row_fwd(_rms_fn, "mix_norm", [(x[0], D_MODEL, 0)], [mix_norm_w + tok[0:1, 0:1]], [(D_MODEL, BF16)], 256)[0]
    handle, tok = _gather2_pass("early_gather_pass", handle, u)
    full = _gather_finish(EARLY, _gather2_wait("early_gather_wait", handle, tok))
    for n in REPLICATED:
        full[n] = w[n].reshape(1, -1)

    loss8, dx, g, pending, after = _local_step(x[0], u, mem[0], loss_target[0], _layout_weights(full),
                                               {n: shards[n] for n in SHARDED if n not in EARLY})

    packed = _pack([g[n] for n in REPLICATED] + [loss8[:1, :1]])
    small, _ = _exchange_start("small_gather_start", [(packed, None)], [((N_DEV,) + packed.shape, F32, None)], True,
                               after)
    grad, delta, new_m, new_v = {}, {}, {}, {}
    done = [dx]

    def tie():
        return jnp.broadcast_to(sum(t[:1, :1] for t in done), (8, LANES))

    for names in sorted(pending, key=lambda names: names == EARLY):
        handle = pending[names]
        for n, parts in zip(names, _exchange_wait("grad_wait_" + names[0], handle, tie())):
            res = _sum_adamw("adamw_" + n, parts, local[n](w[n]), local[n](m[n]), local[n](v[n]))
            grad[n], delta[n], new_m[n], new_v[n] = [(t.T if n == "w_in" else t)[None] for t in res]
            done.append(res[1])

    (parts,) = _exchange_wait("small_gather_wait", small, tie())
    blank = [jnp.zeros((1, 1), F32)]
    res = _sum_adamw("adamw_small", parts, _pack([w[n] for n in REPLICATED] + blank),
                     _pack([m[n] for n in REPLICATED] + blank), _pack([v[n] for n in REPLICATED] + blank))
    shapes = [w[n].shape for n in REPLICATED] + [()]
    loss = _unpack(res[0], shapes)[-1]
    for store, packed_out in zip((grad, delta, new_m, new_v), res):
        for n, val in zip(REPLICATED, _unpack(packed_out, shapes)):
            store[n] = val

    return (loss, dx[None], *[grad[n] for n in WEIGHTS], *[delta[n] for n in WEIGHTS],
            *[new_m[n] for n in WEIGHTS], *[new_v[n] for n in WEIGHTS])
```
